```python
import jax, jax.numpy as jnp
from jax import lax
import numpy as np

D_MODEL = 1024
BATCH = 8
SEQ = 8192
DEPTH = 2

D_MIX = D_MODEL
HEAD_DIM = 64
ATTN_WIDTH = D_MIX // 2
N_HEADS = ATTN_WIDTH // HEAD_DIM
N_KV_HEADS = 2
GROUP = N_HEADS // N_KV_HEADS
KV_WIDTH = N_KV_HEADS * HEAD_DIM
CONV_CHANNELS = D_MIX - ATTN_WIDTH
CONV_WIDTH = 31
WINDOW = 128
BLOCK = 128
ROPE_THETA = 10000.0
D_FF = ((8 * D_MODEL // 3 + 127) // 128) * 128
D_IN = ATTN_WIDTH + 2 * KV_WIDTH + 2 * CONV_CHANNELS
EPS = 1e-5

kernel_name = "hybrid_swa_sink_conformer_conv_macaron"


def rms_norm(x, g):
    xf = x.astype(jnp.float32)
    y = xf * lax.rsqrt(jnp.mean(xf * xf, axis=-1, keepdims=True) + EPS)
    return (y * g.astype(jnp.float32)).astype(x.dtype)


def layer_norm(x, g, b):
    xf = x.astype(jnp.float32)
    mu = jnp.mean(xf, axis=-1, keepdims=True)
    xc = xf - mu
    y = xc * lax.rsqrt(jnp.mean(xc * xc, axis=-1, keepdims=True) + EPS)
    return (y * g.astype(jnp.float32) + b.astype(jnp.float32)).astype(x.dtype)


def swiglu(h, w_gate, w_up, w_down):
    return (jax.nn.silu(h @ w_gate) * (h @ w_up)) @ w_down


def rope_tables(positions):
    inv_freq = 1.0 / (ROPE_THETA ** (jnp.arange(0, HEAD_DIM, 2, dtype=jnp.float32) / HEAD_DIM))
    ang = positions.astype(jnp.float32)[..., None] * inv_freq
    return jnp.cos(ang), jnp.sin(ang)


def apply_rope(t, cos, sin):
    tf = t.astype(jnp.float32)
    t1, t2 = jnp.split(tf, 2, axis=-1)
    c = cos[:, :, None, :]
    s = sin[:, :, None, :]
    return jnp.concatenate([t1 * c - t2 * s, t2 * c + t1 * s], axis=-1).astype(t.dtype)


def sliding_window_attention(q, k, v, sinks):
    B, S = q.shape[0], q.shape[1]
    nb = S // BLOCK
    qb = q.reshape(B, nb, BLOCK, N_KV_HEADS, GROUP, HEAD_DIM).astype(jnp.float32)

    def band(t):
        tb = t.reshape(B, nb, BLOCK, N_KV_HEADS, HEAD_DIM)
        prev = jnp.pad(tb[:, :-1], ((0, 0), (1, 0), (0, 0), (0, 0), (0, 0)))
        return jnp.concatenate([prev, tb], axis=2).astype(jnp.float32)

    kb, vb = band(k), band(v)
    scores = jnp.einsum('bnqkgd,bnjkd->bnkgqj', qb, kb) * (HEAD_DIM ** -0.5)

    q_local = jnp.arange(BLOCK)[:, None] + BLOCK
    k_local = jnp.arange(2 * BLOCK)[None, :]
    rel = q_local - k_local
    in_window = (rel >= 0) & (rel < WINDOW)
    block_valid = (jnp.arange(nb)[:, None] > 0) | (k_local >= BLOCK)
    mask = in_window[None, :, :] & block_valid[:, None, :]
    neg = jnp.finfo(jnp.float32).min
    scores = jnp.where(mask[None, :, None, None, :, :], scores, neg)

    sink = sinks.astype(jnp.float32).reshape(N_KV_HEADS, GROUP)[None, None, :, :, None, None]
    m = jnp.maximum(jnp.max(scores, axis=-1, keepdims=True), sink)
    p = jnp.exp(scores - m)
    denom = jnp.sum(p, axis=-1, keepdims=True) + jnp.exp(sink - m)
    probs = p / denom
    out = jnp.einsum('bnkgqj,bnjkd->bnqkgd', probs, vb)
    return out.reshape(B, S, N_HEADS * HEAD_DIM).astype(q.dtype)


def conformer_conv(u, conv_w, conv_b, ln_g, ln_b):
    a, gate = jnp.split(u, 2, axis=-1)
    h = a * jax.nn.sigmoid(gate)
    h = lax.conv_general_dilated(
        h, conv_w[:, None, :].astype(h.dtype),
        window_strides=(1,), padding=[(CONV_WIDTH - 1, 0)],
        dimension_numbers=('NWC', 'WIO', 'NWC'),
        feature_group_count=CONV_CHANNELS) + conv_b
    h = layer_norm(h, ln_g, ln_b)
    return jax.nn.silu(h)


def _fwd_setup_inputs(seed: int = 0) -> dict:
    key = jax.random.key(seed)
    ks = jax.random.split(key, 20)
    f32 = jnp.float32

    def w(k, shape, fan_in):
        return jax.random.normal(k, shape, f32) * (fan_in ** -0.5)

    def gain(k, shape):
        return 1.0 + 0.05 * jax.random.normal(k, shape, f32)

    x = jax.random.normal(ks[0], (BATCH, SEQ, D_MODEL), f32)
    positions = jnp.broadcast_to(jnp.arange(SEQ, dtype=jnp.int32), (BATCH, SEQ))
    return {
        "x": x,
        "positions": positions,
        "ffn1_norm": gain(ks[1], (DEPTH, D_MODEL)),
        "ffn1_w_gate": w(ks[2], (DEPTH, D_MODEL, D_FF), D_MODEL),
        "ffn1_w_up": w(ks[3], (DEPTH, D_MODEL, D_FF), D_MODEL),
        "ffn1_w_down": w(ks[4], (DEPTH, D_FF, D_MODEL), D_FF),
        "mix_norm": gain(ks[5], (DEPTH, D_MODEL)),
        "w_in": w(ks[6], (DEPTH, D_MODEL, D_IN), D_MODEL),
        "conv_w": w(ks[7], (DEPTH, CONV_WIDTH, CONV_CHANNELS), CONV_WIDTH),
        "conv_b": 0.02 * jax.random.normal(ks[8], (DEPTH, CONV_CHANNELS), f32),
        "conv_ln_g": gain(ks[9], (DEPTH, CONV_CHANNELS)),
        "conv_ln_b": 0.02 * jax.random.normal(ks[10], (DEPTH, CONV_CHANNELS), f32),
        "attn_sinks": 0.5 * jax.random.normal(ks[11], (DEPTH, N_HEADS), f32),
        "w_out": w(ks[12], (DEPTH, D_MIX, D_MODEL), D_MIX),
        "ffn2_norm": gain(ks[13], (DEPTH, D_MODEL)),
        "ffn2_w_gate": w(ks[14], (DEPTH, D_MODEL, D_FF), D_MODEL),
        "ffn2_w_up": w(ks[15], (DEPTH, D_MODEL, D_FF), D_MODEL),
        "ffn2_w_down": w(ks[16], (DEPTH, D_FF, D_MODEL), D_FF),
        "final_norm": gain(ks[17], (D_MODEL,)),
    }


def _fwd_reference(x, positions, ffn1_norm, ffn1_w_gate, ffn1_w_up, ffn1_w_down,
              mix_norm, w_in, conv_w, conv_b, conv_ln_g, conv_ln_b, attn_sinks, w_out,
              ffn2_norm, ffn2_w_gate, ffn2_w_up, ffn2_w_down, final_norm):
    B, S = x.shape[0], x.shape[1]
    cos, sin = rope_tables(positions)
    q_end = ATTN_WIDTH
    k_end = q_end + KV_WIDTH
    v_end = k_end + KV_WIDTH
    for l in range(DEPTH):
        x = x + 0.5 * swiglu(rms_norm(x, ffn1_norm[l]), ffn1_w_gate[l], ffn1_w_up[l], ffn1_w_down[l])
        h = rms_norm(x, mix_norm[l])
        p = h @ w_in[l]
        q = apply_rope(p[..., :q_end].reshape(B, S, N_HEADS, HEAD_DIM), cos, sin)
        k = apply_rope(p[..., q_end:k_end].reshape(B, S, N_KV_HEADS, HEAD_DIM), cos, sin)
        v = p[..., k_end:v_end].reshape(B, S, N_KV_HEADS, HEAD_DIM)
        u = p[..., v_end:]
        attn_out = sliding_window_attention(q, k, v, attn_sinks[l])
        conv_out = conformer_conv(u, conv_w[l], conv_b[l], conv_ln_g[l], conv_ln_b[l])
        x = x + jnp.concatenate([attn_out, conv_out], axis=-1) @ w_out[l]
        x = x + 0.5 * swiglu(rms_norm(x, ffn2_norm[l]), ffn2_w_gate[l], ffn2_w_up[l], ffn2_w_down[l])
    return rms_norm(x, final_norm)


import jax as _jax
import jax.numpy as _jnp

TWIN_FORMAT = 'train_step'
FWD_PARAMS = ['x', 'positions', 'ffn1_norm', 'ffn1_w_gate', 'ffn1_w_up', 'ffn1_w_down', 'mix_norm', 'w_in', 'conv_w', 'conv_b', 'conv_ln_g', 'conv_ln_b', 'attn_sinks', 'w_out', 'ffn2_norm', 'ffn2_w_gate', 'ffn2_w_up', 'ffn2_w_down', 'final_norm']
TWIN_WEIGHTS = ['ffn1_norm', 'ffn1_w_gate', 'ffn1_w_up', 'ffn1_w_down', 'mix_norm', 'w_in', 'conv_w', 'conv_b', 'conv_ln_g', 'conv_ln_b', 'attn_sinks', 'w_out', 'ffn2_norm', 'ffn2_w_gate', 'ffn2_w_up', 'ffn2_w_down', 'final_norm']
TWIN_DIFF_INPUT = 'x'
TWIN_INPUTS = ['x', 'positions', 'ffn1_norm', 'ffn1_w_gate', 'ffn1_w_up', 'ffn1_w_down', 'mix_norm', 'w_in', 'conv_w', 'conv_b', 'conv_ln_g', 'conv_ln_b', 'attn_sinks', 'w_out', 'ffn2_norm', 'ffn2_w_gate', 'ffn2_w_up', 'ffn2_w_down', 'final_norm', 'loss_target', 'm_ffn1_norm', 'm_ffn1_w_gate', 'm_ffn1_w_up', 'm_ffn1_w_down', 'm_mix_norm', 'm_w_in', 'm_conv_w', 'm_conv_b', 'm_conv_ln_g', 'm_conv_ln_b', 'm_attn_sinks', 'm_w_out', 'm_ffn2_norm', 'm_ffn2_w_gate', 'm_ffn2_w_up', 'm_ffn2_w_down', 'm_final_norm', 'v_ffn1_norm', 'v_ffn1_w_gate', 'v_ffn1_w_up', 'v_ffn1_w_down', 'v_mix_norm', 'v_w_in', 'v_conv_w', 'v_conv_b', 'v_conv_ln_g', 'v_conv_ln_b', 'v_attn_sinks', 'v_w_out', 'v_ffn2_norm', 'v_ffn2_w_gate', 'v_ffn2_w_up', 'v_ffn2_w_down', 'v_final_norm']
TWIN_OUTPUTS = ['loss', 'grad_x', 'grad_ffn1_norm', 'grad_ffn1_w_gate', 'grad_ffn1_w_up', 'grad_ffn1_w_down', 'grad_mix_norm', 'grad_w_in', 'grad_conv_w', 'grad_conv_b', 'grad_conv_ln_g', 'grad_conv_ln_b', 'grad_attn_sinks', 'grad_w_out', 'grad_ffn2_norm', 'grad_ffn2_w_gate', 'grad_ffn2_w_up', 'grad_ffn2_w_down', 'grad_final_norm', 'delta_ffn1_norm', 'delta_ffn1_w_gate', 'delta_ffn1_w_up', 'delta_ffn1_w_down', 'delta_mix_norm', 'delta_w_in', 'delta_conv_w', 'delta_conv_b', 'delta_conv_ln_g', 'delta_conv_ln_b', 'delta_attn_sinks', 'delta_w_out', 'delta_ffn2_norm', 'delta_ffn2_w_gate', 'delta_ffn2_w_up', 'delta_ffn2_w_down', 'delta_final_norm', 'new_m_ffn1_norm', 'new_m_ffn1_w_gate', 'new_m_ffn1_w_up', 'new_m_ffn1_w_down', 'new_m_mix_norm', 'new_m_w_in', 'new_m_conv_w', 'new_m_conv_b', 'new_m_conv_ln_g', 'new_m_conv_ln_b', 'new_m_attn_sinks', 'new_m_w_out', 'new_m_ffn2_norm', 'new_m_ffn2_w_gate', 'new_m_ffn2_w_up', 'new_m_ffn2_w_down', 'new_m_final_norm', 'new_v_ffn1_norm', 'new_v_ffn1_w_gate', 'new_v_ffn1_w_up', 'new_v_ffn1_w_down', 'new_v_mix_norm', 'new_v_w_in', 'new_v_conv_w', 'new_v_conv_b', 'new_v_conv_ln_g', 'new_v_conv_ln_b', 'new_v_attn_sinks', 'new_v_w_out', 'new_v_ffn2_norm', 'new_v_ffn2_w_gate', 'new_v_ffn2_w_up', 'new_v_ffn2_w_down', 'new_v_final_norm']
TWIN_LEAF_KINDS = {'loss': 'loss', 'grad_x': 'grad_x', 'grad_ffn1_norm': 'grad_w', 'grad_ffn1_w_gate': 'grad_w', 'grad_ffn1_w_up': 'grad_w', 'grad_ffn1_w_down': 'grad_w', 'grad_mix_norm': 'grad_w', 'grad_w_in': 'grad_w', 'grad_conv_w': 'grad_w', 'grad_conv_b': 'grad_w', 'grad_conv_ln_g': 'grad_w', 'grad_conv_ln_b': 'grad_w', 'grad_attn_sinks': 'grad_w', 'grad_w_out': 'grad_w', 'grad_ffn2_norm': 'grad_w', 'grad_ffn2_w_gate': 'grad_w', 'grad_ffn2_w_up': 'grad_w', 'grad_ffn2_w_down': 'grad_w', 'grad_final_norm': 'grad_w', 'delta_ffn1_norm': 'delta_w', 'delta_ffn1_w_gate': 'delta_w', 'delta_ffn1_w_up': 'delta_w', 'delta_ffn1_w_down': 'delta_w', 'delta_mix_norm': 'delta_w', 'delta_w_in': 'delta_w', 'delta_conv_w': 'delta_w', 'delta_conv_b': 'delta_w', 'delta_conv_ln_g': 'delta_w', 'delta_conv_ln_b': 'delta_w', 'delta_attn_sinks': 'delta_w', 'delta_w_out': 'delta_w', 'delta_ffn2_norm': 'delta_w', 'delta_ffn2_w_gate': 'delta_w', 'delta_ffn2_w_up': 'delta_w', 'delta_ffn2_w_down': 'delta_w', 'delta_final_norm': 'delta_w', 'new_m_ffn1_norm': 'new_m', 'new_m_ffn1_w_gate': 'new_m', 'new_m_ffn1_w_up': 'new_m', 'new_m_ffn1_w_down': 'new_m', 'new_m_mix_norm': 'new_m', 'new_m_w_in': 'new_m', 'new_m_conv_w': 'new_m', 'new_m_conv_b': 'new_m', 'new_m_conv_ln_g': 'new_m', 'new_m_conv_ln_b': 'new_m', 'new_m_attn_sinks': 'new_m', 'new_m_w_out': 'new_m', 'new_m_ffn2_norm': 'new_m', 'new_m_ffn2_w_gate': 'new_m', 'new_m_ffn2_w_up': 'new_m', 'new_m_ffn2_w_down': 'new_m', 'new_m_final_norm': 'new_m', 'new_v_ffn1_norm': 'new_v', 'new_v_ffn1_w_gate': 'new_v', 'new_v_ffn1_w_up': 'new_v', 'new_v_ffn1_w_down': 'new_v', 'new_v_mix_norm': 'new_v', 'new_v_w_in': 'new_v', 'new_v_conv_w': 'new_v', 'new_v_conv_b': 'new_v', 'new_v_conv_ln_g': 'new_v', 'new_v_conv_ln_b': 'new_v', 'new_v_attn_sinks': 'new_v', 'new_v_w_out': 'new_v', 'new_v_ffn2_norm': 'new_v', 'new_v_ffn2_w_gate': 'new_v', 'new_v_ffn2_w_up': 'new_v', 'new_v_ffn2_w_down': 'new_v', 'new_v_final_norm': 'new_v'}


def _forward(args):
    return _fwd_reference(*[args[k] for k in FWD_PARAMS])


def _output_shape():
    def fwd():
        inp = _fwd_setup_inputs(0)
        return _fwd_reference(*[inp[k] for k in FWD_PARAMS])
    out = _jax.eval_shape(fwd)
    return out.shape, out.dtype

N_MICROBATCH = 1
ADAM_LR = 0.001
ADAM_B1 = 0.9
ADAM_B2 = 0.999
ADAM_EPS = 1e-08
ADAM_WD = 0.01
ADAM_STEP = 10
PER_EXAMPLE_BATCH_AXIS = {'x': 0, 'positions': 0, 'loss_target': 0}
SHARED_INPUTS = []
_WEIGHT_DTYPES = {'ffn1_norm': _jnp.float32, 'ffn1_w_gate': _jnp.float32, 'ffn1_w_up': _jnp.float32, 'ffn1_w_down': _jnp.float32, 'mix_norm': _jnp.float32, 'w_in': _jnp.float32, 'conv_w': _jnp.float32, 'conv_b': _jnp.float32, 'conv_ln_g': _jnp.float32, 'conv_ln_b': _jnp.float32, 'attn_sinks': _jnp.float32, 'w_out': _jnp.float32, 'ffn2_norm': _jnp.float32, 'ffn2_w_gate': _jnp.float32, 'ffn2_w_up': _jnp.float32, 'ffn2_w_down': _jnp.float32, 'final_norm': _jnp.float32}
MOMENT_SCALE = {'ffn1_norm': 1.119939e-01, 'ffn1_w_gate': 4.629047e-02, 'ffn1_w_up': 4.495928e-02, 'ffn1_w_down': 7.472878e-02, 'mix_norm': 1.198357e-01, 'w_in': 9.016512e-02, 'conv_w': 1.464689e-01, 'conv_b': 6.073759e-01, 'conv_ln_g': 2.650335e-01, 'conv_ln_b': 3.807662e-01, 'attn_sinks': 3.674934e-02, 'w_out': 1.336280e-01, 'ffn2_norm': 9.768461e-02, 'ffn2_w_gate': 4.001373e-02, 'ffn2_w_up': 3.930751e-02, 'ffn2_w_down': 6.532121e-02, 'final_norm': 6.414653e+01}


def _to_microbatches(a, axis):
    t = _jnp.moveaxis(a, axis, 0)
    t = t.reshape((N_MICROBATCH, t.shape[0] // N_MICROBATCH) + t.shape[1:])
    return _jnp.moveaxis(t, 1, axis + 1)


def setup_inputs(seed: int = 0) -> dict:
    inp = _fwd_setup_inputs(seed)
    key = _jax.random.fold_in(_jax.random.key(seed), 7919)
    shape, _ = _output_shape()
    out = dict(inp)
    out["loss_target"] = _jax.random.normal(_jax.random.fold_in(key, 0), shape, _jnp.float32)
    for i, name in enumerate(TWIN_WEIGHTS):
        w = inp[name].astype(_jnp.float32)
        if MOMENT_SCALE is None:
            s = _jnp.sqrt(_jnp.mean(_jnp.square(w)) + 1e-30)
        else:
            s = MOMENT_SCALE[name]
        km, kv = _jax.random.split(_jax.random.fold_in(key, i + 1))
        out[name] = w
        out["m_" + name] = s * _jax.random.normal(km, w.shape, _jnp.float32)
        out["v_" + name] = (s * s) * _jax.random.uniform(kv, w.shape, _jnp.float32, 0.5, 1.5)
    if N_MICROBATCH > 1:
        for name, axis in PER_EXAMPLE_BATCH_AXIS.items():
            out[name] = _to_microbatches(out[name], axis)
    return {'x': out['x'], 'positions': out['positions'], 'ffn1_norm': out['ffn1_norm'], 'ffn1_w_gate': out['ffn1_w_gate'], 'ffn1_w_up': out['ffn1_w_up'], 'ffn1_w_down': out['ffn1_w_down'], 'mix_norm': out['mix_norm'], 'w_in': out['w_in'], 'conv_w': out['conv_w'], 'conv_b': out['conv_b'], 'conv_ln_g': out['conv_ln_g'], 'conv_ln_b': out['conv_ln_b'], 'attn_sinks': out['attn_sinks'], 'w_out': out['w_out'], 'ffn2_norm': out['ffn2_norm'], 'ffn2_w_gate': out['ffn2_w_gate'], 'ffn2_w_up': out['ffn2_w_up'], 'ffn2_w_down': out['ffn2_w_down'], 'final_norm': out['final_norm'], 'loss_target': out['loss_target'], 'm_ffn1_norm': out['m_ffn1_norm'], 'm_ffn1_w_gate': out['m_ffn1_w_gate'], 'm_ffn1_w_up': out['m_ffn1_w_up'], 'm_ffn1_w_down': out['m_ffn1_w_down'], 'm_mix_norm': out['m_mix_norm'], 'm_w_in': out['m_w_in'], 'm_conv_w': out['m_conv_w'], 'm_conv_b': out['m_conv_b'], 'm_conv_ln_g': out['m_conv_ln_g'], 'm_conv_ln_b': out['m_conv_ln_b'], 'm_attn_sinks': out['m_attn_sinks'], 'm_w_out': out['m_w_out'], 'm_ffn2_norm': out['m_ffn2_norm'], 'm_ffn2_w_gate': out['m_ffn2_w_gate'], 'm_ffn2_w_up': out['m_ffn2_w_up'], 'm_ffn2_w_down': out['m_ffn2_w_down'], 'm_final_norm': out['m_final_norm'], 'v_ffn1_norm': out['v_ffn1_norm'], 'v_ffn1_w_gate': out['v_ffn1_w_gate'], 'v_ffn1_w_up': out['v_ffn1_w_up'], 'v_ffn1_w_down': out['v_ffn1_w_down'], 'v_mix_norm': out['v_mix_norm'], 'v_w_in': out['v_w_in'], 'v_conv_w': out['v_conv_w'], 'v_conv_b': out['v_conv_b'], 'v_conv_ln_g': out['v_conv_ln_g'], 'v_conv_ln_b': out['v_conv_ln_b'], 'v_attn_sinks': out['v_attn_sinks'], 'v_w_out': out['v_w_out'], 'v_ffn2_norm': out['v_ffn2_norm'], 'v_ffn2_w_gate': out['v_ffn2_w_gate'], 'v_ffn2_w_up': out['v_ffn2_w_up'], 'v_ffn2_w_down': out['v_ffn2_w_down'], 'v_final_norm': out['v_final_norm']}


def _loss(weights, diff, rest, loss_target):
    with _jax.named_scope("forward"):
        args = {**rest, TWIN_DIFF_INPUT: diff, **{k: w.astype(_WEIGHT_DTYPES[k]) for k, w in weights.items()}}
        y = _forward(args)
    with _jax.named_scope("loss_head"):
        err = _jnp.square(y.astype(_jnp.float32) - loss_target)
        return 0.5 * _jnp.sum(_jnp.mean(err, axis=-1)) if err.ndim else 0.5 * err


def _adamw(w, g, m, v):
    m = ADAM_B1 * m + (1.0 - ADAM_B1) * g
    v = ADAM_B2 * v + (1.0 - ADAM_B2) * _jnp.square(g)
    m_hat = m / (1.0 - ADAM_B1 ** ADAM_STEP)
    v_hat = v / (1.0 - ADAM_B2 ** ADAM_STEP)
    delta = -ADAM_LR * (m_hat / (_jnp.sqrt(v_hat) + ADAM_EPS) + ADAM_WD * w)
    return delta, m, v


def reference(x, positions, ffn1_norm, ffn1_w_gate, ffn1_w_up, ffn1_w_down, mix_norm, w_in, conv_w, conv_b, conv_ln_g, conv_ln_b, attn_sinks, w_out, ffn2_norm, ffn2_w_gate, ffn2_w_up, ffn2_w_down, final_norm, loss_target, m_ffn1_norm, m_ffn1_w_gate, m_ffn1_w_up, m_ffn1_w_down, m_mix_norm, m_w_in, m_conv_w, m_conv_b, m_conv_ln_g, m_conv_ln_b, m_attn_sinks, m_w_out, m_ffn2_norm, m_ffn2_w_gate, m_ffn2_w_up, m_ffn2_w_down, m_final_norm, v_ffn1_norm, v_ffn1_w_gate, v_ffn1_w_up, v_ffn1_w_down, v_mix_norm, v_w_in, v_conv_w, v_conv_b, v_conv_ln_g, v_conv_ln_b, v_attn_sinks, v_w_out, v_ffn2_norm, v_ffn2_w_gate, v_ffn2_w_up, v_ffn2_w_down, v_final_norm):
    given = dict(x=x, positions=positions, ffn1_norm=ffn1_norm, ffn1_w_gate=ffn1_w_gate, ffn1_w_up=ffn1_w_up, ffn1_w_down=ffn1_w_down, mix_norm=mix_norm, w_in=w_in, conv_w=conv_w, conv_b=conv_b, conv_ln_g=conv_ln_g, conv_ln_b=conv_ln_b, attn_sinks=attn_sinks, w_out=w_out, ffn2_norm=ffn2_norm, ffn2_w_gate=ffn2_w_gate, ffn2_w_up=ffn2_w_up, ffn2_w_down=ffn2_w_down, final_norm=final_norm, loss_target=loss_target, m_ffn1_norm=m_ffn1_norm, m_ffn1_w_gate=m_ffn1_w_gate, m_ffn1_w_up=m_ffn1_w_up, m_ffn1_w_down=m_ffn1_w_down, m_mix_norm=m_mix_norm, m_w_in=m_w_in, m_conv_w=m_conv_w, m_conv_b=m_conv_b, m_conv_ln_g=m_conv_ln_g, m_conv_ln_b=m_conv_ln_b, m_attn_sinks=m_attn_sinks, m_w_out=m_w_out, m_ffn2_norm=m_ffn2_norm, m_ffn2_w_gate=m_ffn2_w_gate, m_ffn2_w_up=m_ffn2_w_up, m_ffn2_w_down=m_ffn2_w_down, m_final_norm=m_final_norm, v_ffn1_norm=v_ffn1_norm, v_ffn1_w_gate=v_ffn1_w_gate, v_ffn1_w_up=v_ffn1_w_up, v_ffn1_w_down=v_ffn1_w_down, v_mix_norm=v_mix_norm, v_w_in=v_w_in, v_conv_w=v_conv_w, v_conv_b=v_conv_b, v_conv_ln_g=v_conv_ln_g, v_conv_ln_b=v_conv_ln_b, v_attn_sinks=v_attn_sinks, v_w_out=v_w_out, v_ffn2_norm=v_ffn2_norm, v_ffn2_w_gate=v_ffn2_w_gate, v_ffn2_w_up=v_ffn2_w_up, v_ffn2_w_down=v_ffn2_w_down, v_final_norm=v_final_norm)
    weights = {n: given[n] for n in TWIN_WEIGHTS}
    shared = {n: given[n] for n in SHARED_INPUTS}
    per_example = {n: given[n] for n in ['x', 'positions']}
    grad_fn = _jax.value_and_grad(_loss, argnums=(0, 1))

    def one_microbatch(ex, loss_target):
        ex = dict(ex)
        diff = ex.pop(TWIN_DIFF_INPUT)
        return grad_fn(weights, diff, {**shared, **ex}, loss_target)

    if N_MICROBATCH == 1:
        loss, (grad_w, grad_x) = one_microbatch(per_example, given["loss_target"])
    else:
        def body(carry, xs):
            loss_sum, grad_sum = carry
            l_k, (gw_k, gx_k) = one_microbatch(xs[0], xs[1])
            with _jax.named_scope("update"):
                return (loss_sum + l_k, _jax.tree.map(_jnp.add, grad_sum, gw_k)), gx_k

        init = (_jnp.zeros((), _jnp.float32), _jax.tree.map(_jnp.zeros_like, weights))
        (loss, grad_w), grad_x = _jax.lax.scan(body, init, (per_example, given["loss_target"]))
    with _jax.named_scope("update"):
        delta_w, new_m, new_v = {}, {}, {}
        for n in TWIN_WEIGHTS:
            delta_w[n], new_m[n], new_v[n] = _adamw(weights[n], grad_w[n], given["m_" + n], given["v_" + n])
    return (loss, grad_x, *[grad_w[n] for n in TWIN_WEIGHTS], *[delta_w[n] for n in TWIN_WEIGHTS],
            *[new_m[n] for n in TWIN_WEIGHTS], *[new_v[n] for n in TWIN_WEIGHTS])
```

```python
import math

import jax
import jax.numpy as jnp
from jax import lax
from jax.experimental import pallas as pl
from jax.experimental.pallas import tpu as pltpu

F32 = jnp.float32
CDT = jnp.bfloat16
D_MODEL = 1024
D_FF = 2816
N_HEADS = 8
HEAD_DIM = 64
BLOCK = 128
CONV_K = 31
CONV_C = 512
ATT_W = 512
D_EXT = 2048
EPS = 1e-5
HALO = 32
NEG = float(jnp.finfo(jnp.float32).min)
VMEM_LIMIT = 56 * 1024 * 1024

ADAM_LR = 0.001
ADAM_B1 = 0.9
ADAM_B2 = 0.999
ADAM_EPS = 1e-08
ADAM_WD = 0.01
ADAM_STEP = 10

NT = (((1,), (1,)), ((), ()))
TN = (((0,), (0,)), ((), ()))


def _params(n_axes):
    return pltpu.CompilerParams(dimension_semantics=("arbitrary",) * n_axes, vmem_limit_bytes=VMEM_LIMIT)


def _sigmoid(z):
    return 1.0 / (1.0 + jnp.exp(-z))


def _rms_parts(xf):
    r = lax.rsqrt(jnp.mean(xf * xf, axis=-1, keepdims=True) + EPS)
    return xf * r, r


def _rms_bwd(dh, xhat, r, g):
    dg = jnp.sum(dh * xhat, axis=0, keepdims=True)
    dxhat = dh * g
    dx = r * (dxhat - xhat * jnp.mean(dxhat * xhat, axis=-1, keepdims=True))
    return dx, dg


def ffn_fwd(x, g, wgT, wuT, wd, *, tm=512, tf=1408):
    S, D = x.shape
    F = wgT.shape[0]
    nf = F // tf

    def body(x_ref, g_ref, wg_ref, wu_ref, wd_ref, o_ref, gate_ref, up_ref, h_sc, acc_sc):
        j = pl.program_id(1)

        @pl.when(j == 0)
        def _():
            xhat, _ = _rms_parts(x_ref[...])
            h_sc[...] = (xhat * g_ref[...]).astype(CDT)
            acc_sc[...] = jnp.zeros_like(acc_sc)

        h = h_sc[...]
        gt = lax.dot_general(h, wg_ref[...], NT, preferred_element_type=F32)
        ut = lax.dot_general(h, wu_ref[...], NT, preferred_element_type=F32)
        gate_ref[...] = gt.astype(CDT)
        up_ref[...] = ut.astype(CDT)
        a = (gt * _sigmoid(gt) * ut).astype(CDT)
        acc_sc[...] += jnp.dot(a, wd_ref[...], preferred_element_type=F32)

        @pl.when(j == nf - 1)
        def _():
            o_ref[...] = x_ref[...] + 0.5 * acc_sc[...]

    return pl.pallas_call(
        body, name="ffn_fwd",
        grid=(S // tm, nf),
        in_specs=[pl.BlockSpec((tm, D), lambda i, j: (i, 0)),
                  pl.BlockSpec((1, D), lambda i, j: (0, 0)),
                  pl.BlockSpec((tf, D), lambda i, j: (j, 0)),
                  pl.BlockSpec((tf, D), lambda i, j: (j, 0)),
                  pl.BlockSpec((tf, D), lambda i, j: (j, 0))],
        out_specs=[pl.BlockSpec((tm, D), lambda i, j: (i, 0)),
                   pl.BlockSpec((tm, tf), lambda i, j: (i, j)),
                   pl.BlockSpec((tm, tf), lambda i, j: (i, j))],
        out_shape=[jax.ShapeDtypeStruct((S, D), F32),
                   jax.ShapeDtypeStruct((S, F), CDT),
                   jax.ShapeDtypeStruct((S, F), CDT)],
        scratch_shapes=[pltpu.VMEM((tm, D), CDT), pltpu.VMEM((tm, D), F32)],
        compiler_params=_params(2),
    )(x, g, wgT, wuT, wd)


def ffn_bwd_dgrad(x, g, dy, gate, up, wgT, wuT, wd, *, tm=512, tf=1408):
    S, D = x.shape
    F = wgT.shape[0]
    nf = F // tf

    def body(x_ref, g_ref, dy_ref, gate_ref, up_ref, wg_ref, wu_ref, wd_ref,
             dx_ref, dgate_ref, dup_ref, h_ref, dacc_ref, dg_ref, dh_sc):
        i = pl.program_id(0)
        j = pl.program_id(1)

        @pl.when(j == 0)
        def _():
            xhat, _ = _rms_parts(x_ref[...])
            h_ref[...] = (xhat * g_ref[...]).astype(CDT)
            dacc_ref[...] = (0.5 * dy_ref[...]).astype(CDT)
            dh_sc[...] = jnp.zeros_like(dh_sc)

        @pl.when((i == 0) & (j == 0))
        def _():
            dg_ref[...] = jnp.zeros_like(dg_ref)

        d_a = lax.dot_general(dacc_ref[...], wd_ref[...], NT, preferred_element_type=F32)
        gt = gate_ref[...].astype(F32)
        ut = up_ref[...].astype(F32)
        sg = _sigmoid(gt)
        d_up = (d_a * (gt * sg)).astype(CDT)
        d_gate = (d_a * ut * (sg * (1.0 + gt * (1.0 - sg)))).astype(CDT)
        dgate_ref[...] = d_gate
        dup_ref[...] = d_up
        dh_sc[...] += (jnp.dot(d_gate, wg_ref[...], preferred_element_type=F32)
                       + jnp.dot(d_up, wu_ref[...], preferred_element_type=F32))

        @pl.when(j == nf - 1)
        def _():
            xhat, r = _rms_parts(x_ref[...])
            dx, dg = _rms_bwd(dh_sc[...], xhat, r, g_ref[...])
            dx_ref[...] = dy_ref[...] + dx
            dg_ref[...] += dg

    return pl.pallas_call(
        body, name="ffn_bwd_dgrad",
        grid=(S // tm, nf),
        in_specs=[pl.BlockSpec((tm, D), lambda i, j: (i, 0)),
                  pl.BlockSpec((1, D), lambda i, j: (0, 0)),
                  pl.BlockSpec((tm, D), lambda i, j: (i, 0)),
                  pl.BlockSpec((tm, tf), lambda i, j: (i, j)),
                  pl.BlockSpec((tm, tf), lambda i, j: (i, j)),
                  pl.BlockSpec((tf, D), lambda i, j: (j, 0)),
                  pl.BlockSpec((tf, D), lambda i, j: (j, 0)),
                  pl.BlockSpec((tf, D), lambda i, j: (j, 0))],
        out_specs=[pl.BlockSpec((tm, D), lambda i, j: (i, 0)),
                   pl.BlockSpec((tm, tf), lambda i, j: (i, j)),
                   pl.BlockSpec((tm, tf), lambda i, j: (i, j)),
                   pl.BlockSpec((tm, D), lambda i, j: (i, 0)),
                   pl.BlockSpec((tm, D), lambda i, j: (i, 0)),
                   pl.BlockSpec((1, D), lambda i, j: (0, 0))],
        out_shape=[jax.ShapeDtypeStruct((S, D), F32),
                   jax.ShapeDtypeStruct((S, F), CDT),
                   jax.ShapeDtypeStruct((S, F), CDT),
                   jax.ShapeDtypeStruct((S, D), CDT),
                   jax.ShapeDtypeStruct((S, D), CDT),
                   jax.ShapeDtypeStruct((1, D), F32)],
        scratch_shapes=[pltpu.VMEM((tm, D), F32)],
        compiler_params=_params(2),
    )(x, g, dy, gate, up, wgT, wuT, wd)


def ffn_bwd_wgrad(h, dacc, gate, up, dgate, dup, *, tk=1024, tf=256):
    S, D = h.shape
    F = gate.shape[1]
    tk = min(tk, S)

    def body(h_ref, dacc_ref, gate_ref, up_ref, dgate_ref, dup_ref, dwg_ref, dwu_ref, dwd_ref):
        k = pl.program_id(1)

        @pl.when(k == 0)
        def _():
            dwg_ref[...] = jnp.zeros_like(dwg_ref)
            dwu_ref[...] = jnp.zeros_like(dwu_ref)
            dwd_ref[...] = jnp.zeros_like(dwd_ref)

        hh = h_ref[...]
        dwg_ref[...] += lax.dot_general(dgate_ref[...], hh, TN, preferred_element_type=F32)
        dwu_ref[...] += lax.dot_general(dup_ref[...], hh, TN, preferred_element_type=F32)
        gt = gate_ref[...].astype(F32)
        a = (gt * _sigmoid(gt) * up_ref[...].astype(F32)).astype(CDT)
        dwd_ref[...] += lax.dot_general(a, dacc_ref[...], TN, preferred_element_type=F32)

    tok = pl.BlockSpec((tk, D), lambda j, k: (k, 0))
    act = pl.BlockSpec((tk, tf), lambda j, k: (k, j))
    out = pl.BlockSpec((tf, D), lambda j, k: (j, 0))
    return pl.pallas_call(
        body, name="ffn_bwd_wgrad",
        grid=(F // tf, S // tk),
        in_specs=[tok, tok, act, act, act, act],
        out_specs=[out, out, out],
        out_shape=[jax.ShapeDtypeStruct((F, D), F32)] * 3,
        compiler_params=_params(2),
    )(h, dacc, gate, up, dgate, dup)


def loss_head(x, g, target, *, tm=512):
    S, D = x.shape

    def body(x_ref, g_ref, t_ref, loss_ref, dx_ref, dg_ref):
        @pl.when(pl.program_id(0) == 0)
        def _():
            loss_ref[...] = jnp.zeros_like(loss_ref)
            dg_ref[...] = jnp.zeros_like(dg_ref)

        xhat, r = _rms_parts(x_ref[...])
        gg = g_ref[...]
        err = xhat * gg - t_ref[...]
        loss_ref[...] += 0.5 * jnp.sum(jnp.mean(err * err, axis=-1, keepdims=True), axis=0, keepdims=True)
        dx, dg = _rms_bwd(err * (1.0 / D), xhat, r, gg)
        dx_ref[...] = dx
        dg_ref[...] += dg

    row = pl.BlockSpec((tm, D), lambda i: (i, 0))
    vec = pl.BlockSpec((1, D), lambda i: (0, 0))
    return pl.pallas_call(
        body, name="loss_head",
        grid=(S // tm,),
        in_specs=[row, vec, row],
        out_specs=[pl.BlockSpec((1, 1), lambda i: (0, 0)), row, vec],
        out_shape=[jax.ShapeDtypeStruct((1, 1), F32), jax.ShapeDtypeStruct((S, D), F32),
                   jax.ShapeDtypeStruct((1, D), F32)],
        compiler_params=_params(1),
    )(x, g, target)


def _rope_apply(t, cs, sn):
    lane = lax.broadcasted_iota(jnp.int32, t.shape, 1)
    first = (lane % HEAD_DIM) < (HEAD_DIM // 2)
    rot = jnp.where(first, pltpu.roll(t, 128 - HEAD_DIM // 2, 1), pltpu.roll(t, HEAD_DIM // 2, 1))
    return t * cs + rot * sn


def _rope_transpose(d, cs, sn):
    lane = lax.broadcasted_iota(jnp.int32, d.shape, 1)
    first = (lane % HEAD_DIM) < (HEAD_DIM // 2)
    ds = d * sn
    rot = jnp.where(first, pltpu.roll(ds, 128 - HEAD_DIM // 2, 1), pltpu.roll(ds, HEAD_DIM // 2, 1))
    return d * cs + rot


def inproj_fwd(x, g, wextT, cs, sn, *, tm=512):
    S, D = x.shape
    scale = HEAD_DIM ** -0.5

    def body(x_ref, g_ref, w_ref, cs_ref, sn_ref, q_ref, k_ref, v_ref, u_ref):
        xhat, _ = _rms_parts(x_ref[...])
        h = (xhat * g_ref[...]).astype(CDT)
        p = lax.dot_general(h, w_ref[...], NT, preferred_element_type=F32)
        c, s = cs_ref[...], sn_ref[...]
        for b in range(4):
            q_ref[:, 128 * b:128 * (b + 1)] = (_rope_apply(p[:, 128 * b:128 * (b + 1)], c, s) * scale).astype(CDT)
        for b in range(2):
            k_ref[:, 128 * b:128 * (b + 1)] = _rope_apply(p[:, 512 + 128 * b:512 + 128 * (b + 1)], c, s).astype(CDT)
        v_ref[...] = p[:, 768:1024].astype(CDT)
        u_ref[...] = p[:, 1024:2048]

    def row(w):
        return pl.BlockSpec((tm, w), lambda i: (i, 0))

    return pl.pallas_call(
        body, name="inproj_fwd",
        grid=(S // tm,),
        in_specs=[row(D), pl.BlockSpec((1, D), lambda i: (0, 0)),
                  pl.BlockSpec((D_EXT, D), lambda i: (0, 0)), row(128), row(128)],
        out_specs=[row(512), row(256), row(256), row(1024)],
        out_shape=[jax.ShapeDtypeStruct((S, 512), CDT), jax.ShapeDtypeStruct((S, 256), CDT),
                   jax.ShapeDtypeStruct((S, 256), CDT), jax.ShapeDtypeStruct((S, 1024), F32)],
        compiler_params=_params(1),
    )(x, g, wextT, cs, sn)


def _stack_heads(p0, p1):
    lane = lax.broadcasted_iota(jnp.int32, p0.shape, 1)
    lo = lane < HEAD_DIM
    z = jnp.zeros_like(p0)
    return jnp.concatenate([jnp.where(lo, p0, z), jnp.where(lo, z, p0),
                            jnp.where(lo, p1, z), jnp.where(lo, z, p1)], axis=0)


def _unstack_heads(o):
    lane = lax.broadcasted_iota(jnp.int32, (BLOCK, 128), 1)
    lo = lane < HEAD_DIM
    return (jnp.where(lo, o[0:128], o[128:256]), jnp.where(lo, o[256:384], o[384:512]))


def _band_mask_qk(n):
    i = lax.broadcasted_iota(jnp.int32, (4 * BLOCK, 2 * BLOCK), 0) % BLOCK
    c = lax.broadcasted_iota(jnp.int32, (4 * BLOCK, 2 * BLOCK), 1)
    return (c > i) & (c <= i + BLOCK) & ((n > 0) | (c >= BLOCK))


def _band_mask_kq(n):
    c = lax.broadcasted_iota(jnp.int32, (2 * BLOCK, 4 * BLOCK), 0)
    i = lax.broadcasted_iota(jnp.int32, (2 * BLOCK, 4 * BLOCK), 1) % BLOCK
    return (c > i) & (c <= i + BLOCK) & ((n > 0) | (c >= BLOCK))


def attn_fwd(q, k, v, sink_col, *, nb=2):
    S = q.shape[0]
    tq = nb * BLOCK

    def body(q_ref, k_ref, v_ref, sink_ref, o_ref):
        t = pl.program_id(0)
        for b in range(nb):
            n = t * nb + b
            prev = pl.multiple_of(jnp.maximum(n - 1, 0) * BLOCK, BLOCK)
            cur = pl.multiple_of(n * BLOCK, BLOCK)
            rows = slice(b * BLOCK, (b + 1) * BLOCK)
            mask = _band_mask_qk(n)
            for gidx in range(2):
                lanes = slice(128 * gidx, 128 * (gidx + 1))
                qs = _stack_heads(q_ref[rows, 256 * gidx:256 * gidx + 128],
                                  q_ref[rows, 256 * gidx + 128:256 * gidx + 256])
                kb = jnp.concatenate([k_ref[pl.ds(prev, BLOCK), lanes], k_ref[pl.ds(cur, BLOCK), lanes]], axis=0)
                vb = jnp.concatenate([v_ref[pl.ds(prev, BLOCK), lanes], v_ref[pl.ds(cur, BLOCK), lanes]], axis=0)
                s = lax.dot_general(qs, kb, NT, preferred_element_type=F32)
                s = jnp.where(mask, s, NEG)
                sink = sink_ref[gidx]
                m = jnp.maximum(jnp.max(s, axis=1, keepdims=True), sink)
                p = jnp.exp(s - m)
                den = jnp.sum(p, axis=1, keepdims=True) + jnp.exp(sink - m)
                o = jnp.dot(p.astype(CDT), vb, preferred_element_type=F32) / den
                o0, o1 = _unstack_heads(o)
                o_ref[rows, 256 * gidx:256 * gidx + 128] = o0.astype(CDT)
                o_ref[rows, 256 * gidx + 128:256 * gidx + 256] = o1.astype(CDT)

    return pl.pallas_call(
        body, name="attn_fwd",
        grid=(S // tq,),
        in_specs=[pl.BlockSpec((tq, 512), lambda t: (t, 0)),
                  pl.BlockSpec((S, 256), lambda t: (0, 0)),
                  pl.BlockSpec((S, 256), lambda t: (0, 0)),
                  pl.BlockSpec((2, 4 * BLOCK, 1), lambda t: (0, 0, 0))],
        out_specs=pl.BlockSpec((tq, 512), lambda t: (t, 0)),
        out_shape=jax.ShapeDtypeStruct((S, 512), CDT),
        compiler_params=_params(1),
    )(q, k, v, sink_col)


def attn_bwd(q, k, v, do, sink_row, *, nb=2):
    S = q.shape[0]
    tq = nb * BLOCK
    scale = HEAD_DIM ** -0.5

    def body(q_ref, k_ref, v_ref, do_ref, sink_ref, dq_ref, dk_ref, dv_ref, dsink_ref):
        t = pl.program_id(0)

        @pl.when(t == 0)
        def _():
            dk_ref[...] = jnp.zeros_like(dk_ref)
            dv_ref[...] = jnp.zeros_like(dv_ref)
            dsink_ref[...] = jnp.zeros_like(dsink_ref)

        for b in range(nb):
            n = t * nb + b
            prev = pl.multiple_of(jnp.maximum(n - 1, 0) * BLOCK, BLOCK)
            cur = pl.multiple_of(n * BLOCK, BLOCK)
            rows = slice(b * BLOCK, (b + 1) * BLOCK)
            mask = _band_mask_kq(n)
            for gidx in range(2):
                lanes = slice(128 * gidx, 128 * (gidx + 1))
                qs = _stack_heads(q_ref[rows, 256 * gidx:256 * gidx + 128],
                                  q_ref[rows, 256 * gidx + 128:256 * gidx + 256])
                dos = _stack_heads(do_ref[rows, 256 * gidx:256 * gidx + 128],
                                   do_ref[rows, 256 * gidx + 128:256 * gidx + 256])
                kb = jnp.concatenate([k_ref[pl.ds(prev, BLOCK), lanes], k_ref[pl.ds(cur, BLOCK), lanes]], axis=0)
                vb = jnp.concatenate([v_ref[pl.ds(prev, BLOCK), lanes], v_ref[pl.ds(cur, BLOCK), lanes]], axis=0)
                st = lax.dot_general(kb, qs, NT, preferred_element_type=F32)
                st = jnp.where(mask, st, NEG)
                sink = sink_ref[gidx]
                m = jnp.maximum(jnp.max(st, axis=0, keepdims=True), sink)
                e = jnp.exp(st - m)
                es = jnp.exp(sink - m)
                inv = 1.0 / (jnp.sum(e, axis=0, keepdims=True) + es)
                pt = e * inv
                dpt = lax.dot_general(vb, dos, NT, preferred_element_type=F32)
                delta = jnp.sum(pt * dpt, axis=0, keepdims=True)
                dst = (pt * (dpt - delta)).astype(CDT)
                dsink_ref[gidx] += -(es * inv) * delta
                dvb = jnp.dot(pt.astype(CDT), dos, preferred_element_type=F32)
                dkb = jnp.dot(dst, qs, preferred_element_type=F32)
                dqs = lax.dot_general(dst, kb, TN, preferred_element_type=F32) * scale
                dq0, dq1 = _unstack_heads(dqs)
                dq_ref[rows, 256 * gidx:256 * gidx + 128] = dq0
                dq_ref[rows, 256 * gidx + 128:256 * gidx + 256] = dq1
                dk_ref[pl.ds(prev, BLOCK), lanes] += dkb[0:BLOCK]
                dk_ref[pl.ds(cur, BLOCK), lanes] += dkb[BLOCK:2 * BLOCK]
                dv_ref[pl.ds(prev, BLOCK), lanes] += dvb[0:BLOCK]
                dv_ref[pl.ds(cur, BLOCK), lanes] += dvb[BLOCK:2 * BLOCK]

    full = pl.BlockSpec((S, 256), lambda t: (0, 0))
    tile = pl.BlockSpec((tq, 512), lambda t: (t, 0))
    srow = pl.BlockSpec((2, 1, 4 * BLOCK), lambda t: (0, 0, 0))
    return pl.pallas_call(
        body, name="attn_bwd",
        grid=(S // tq,),
        in_specs=[tile, full, full, tile, srow],
        out_specs=[tile, full, full, srow],
        out_shape=[jax.ShapeDtypeStruct((S, 512), F32), jax.ShapeDtypeStruct((S, 256), F32),
                   jax.ShapeDtypeStruct((S, 256), F32), jax.ShapeDtypeStruct((2, 1, 4 * BLOCK), F32)],
        compiler_params=_params(1),
    )(q, k, v, do, sink_row)


def _glu(u):
    a = u[:, 0:CONV_C]
    gt = u[:, CONV_C:2 * CONV_C]
    sg = _sigmoid(gt)
    return a, sg, a * sg


CONV_CHUNK = 64


def conv_fwd(u, cw, cb, lg, lb, *, tm=512):
    S = u.shape[0]
    nh = tm // HALO

    def body(u_ref, uh_ref, cw_ref, cb_ref, lg_ref, lb_ref, o_ref, y_ref, hbuf):
        t = pl.program_id(0)
        _, _, hg = _glu(u_ref[...])
        _, _, hh = _glu(uh_ref[...])
        hbuf[0:HALO, :] = jnp.where(t > 0, hh, jnp.zeros_like(hh))
        hbuf[HALO:HALO + tm, :] = hg
        off = HALO - (CONV_K - 1)
        for c0 in range(0, tm, CONV_CHUNK):
            acc = jnp.zeros((CONV_CHUNK, CONV_C), F32) + cb_ref[...]
            for j in range(CONV_K):
                acc = acc + cw_ref[j:j + 1, :] * hbuf[c0 + off + j:c0 + off + j + CONV_CHUNK, :]
            y_ref[c0:c0 + CONV_CHUNK, :] = acc
        y = y_ref[...]
        yc = y - jnp.mean(y, axis=-1, keepdims=True)
        r = lax.rsqrt(jnp.mean(yc * yc, axis=-1, keepdims=True) + EPS)
        z = yc * r * lg_ref[...] + lb_ref[...]
        o_ref[...] = (z * _sigmoid(z)).astype(CDT)

    vec = pl.BlockSpec((1, CONV_C), lambda t: (0, 0))
    return pl.pallas_call(
        body, name="conv_fwd",
        grid=(S // tm,),
        in_specs=[pl.BlockSpec((tm, 2 * CONV_C), lambda t: (t, 0)),
                  pl.BlockSpec((HALO, 2 * CONV_C), lambda t: (jnp.maximum(t * nh - 1, 0), 0)),
                  pl.BlockSpec((CONV_K, CONV_C), lambda t: (0, 0)), vec, vec, vec],
        out_specs=[pl.BlockSpec((tm, CONV_C), lambda t: (t, 0)), pl.BlockSpec((tm, CONV_C), lambda t: (t, 0))],
        out_shape=[jax.ShapeDtypeStruct((S, CONV_C), CDT), jax.ShapeDtypeStruct((S, CONV_C), F32)],
        scratch_shapes=[pltpu.VMEM((HALO + tm, CONV_C), F32)],
        compiler_params=_params(1),
    )(u, u, cw, cb, lg, lb)


def conv_bwd(dc, u, y, cw, lg, lb, *, tm=512):
    S = u.shape[0]
    nh = tm // HALO
    nt = S // tm

    def ln_bwd(dcv, yv, lgv, lbv):
        yc = yv - jnp.mean(yv, axis=-1, keepdims=True)
        r = lax.rsqrt(jnp.mean(yc * yc, axis=-1, keepdims=True) + EPS)
        yhat = yc * r
        z = yhat * lgv + lbv
        sg = _sigmoid(z)
        dz = dcv * (sg * (1.0 + z * (1.0 - sg)))
        dyhat = dz * lgv
        dy = r * (dyhat - jnp.mean(dyhat, axis=-1, keepdims=True)
                  - yhat * jnp.mean(dyhat * yhat, axis=-1, keepdims=True))
        return dy, dz, yhat

    def body(dc_ref, dcn_ref, u_ref, uh_ref, y_ref, yn_ref, cw_ref, lg_ref, lb_ref,
             du_ref, dcw_ref, dcb_ref, dlg_ref, dlb_ref, hbuf, dybuf, dhg_sc, dw_sc):
        t = pl.program_id(0)

        @pl.when(t == 0)
        def _():
            dw_sc[...] = jnp.zeros_like(dw_sc)
            dcb_ref[...] = jnp.zeros_like(dcb_ref)
            dlg_ref[...] = jnp.zeros_like(dlg_ref)
            dlb_ref[...] = jnp.zeros_like(dlb_ref)

        lgv, lbv = lg_ref[...], lb_ref[...]
        dy, dz, yhat = ln_bwd(dc_ref[...].astype(F32), y_ref[...], lgv, lbv)
        dyn, _, _ = ln_bwd(dcn_ref[...].astype(F32), yn_ref[...], lgv, lbv)
        dlb_ref[...] += jnp.sum(dz, axis=0, keepdims=True)
        dlg_ref[...] += jnp.sum(dz * yhat, axis=0, keepdims=True)
        dcb_ref[...] += jnp.sum(dy, axis=0, keepdims=True)
        dybuf[0:tm, :] = dy
        dybuf[tm:tm + HALO, :] = jnp.where(t < nt - 1, dyn, jnp.zeros_like(dyn))

        a, sg, hg = _glu(u_ref[...])
        _, _, hh = _glu(uh_ref[...])
        hbuf[0:HALO, :] = jnp.where(t > 0, hh, jnp.zeros_like(hh))
        hbuf[HALO:HALO + tm, :] = hg

        off = HALO - (CONV_K - 1)
        for c0 in range(0, tm, CONV_CHUNK):
            acc = jnp.zeros((CONV_CHUNK, CONV_C), F32)
            dyc = dybuf[c0:c0 + CONV_CHUNK, :]
            for j in range(CONV_K):
                acc = acc + cw_ref[j:j + 1, :] * dybuf[c0 + (CONV_K - 1) - j:c0 + (CONV_K - 1) - j + CONV_CHUNK, :]
                prod = dyc * hbuf[c0 + off + j:c0 + off + j + CONV_CHUNK, :]
                dw_sc[j] += jnp.sum(prod.reshape(CONV_CHUNK // 8, 8, CONV_C), axis=0)
            dhg_sc[c0:c0 + CONV_CHUNK, :] = acc

        dhg = dhg_sc[...]
        du_ref[:, 0:CONV_C] = dhg * sg
        du_ref[:, CONV_C:2 * CONV_C] = dhg * a * sg * (1.0 - sg)

        @pl.when(t == nt - 1)
        def _():
            dcw_ref[...] = jnp.sum(dw_sc[...], axis=1)

    vec = pl.BlockSpec((1, CONV_C), lambda t: (0, 0))
    tile = pl.BlockSpec((tm, CONV_C), lambda t: (t, 0))
    nxt = pl.BlockSpec((HALO, CONV_C), lambda t: (jnp.minimum((t + 1) * nh, S // HALO - 1), 0))
    return pl.pallas_call(
        body, name="conv_bwd",
        grid=(nt,),
        in_specs=[tile, nxt,
                  pl.BlockSpec((tm, 2 * CONV_C), lambda t: (t, 0)),
                  pl.BlockSpec((HALO, 2 * CONV_C), lambda t: (jnp.maximum(t * nh - 1, 0), 0)),
                  tile, nxt,
                  pl.BlockSpec((CONV_K, CONV_C), lambda t: (0, 0)), vec, vec],
        out_specs=[pl.BlockSpec((tm, 2 * CONV_C), lambda t: (t, 0)),
                   pl.BlockSpec((CONV_K, CONV_C), lambda t: (0, 0)), vec, vec, vec],
        out_shape=[jax.ShapeDtypeStruct((S, 2 * CONV_C), F32), jax.ShapeDtypeStruct((CONV_K, CONV_C), F32),
                   jax.ShapeDtypeStruct((1, CONV_C), F32), jax.ShapeDtypeStruct((1, CONV_C), F32),
                   jax.ShapeDtypeStruct((1, CONV_C), F32)],
        scratch_shapes=[pltpu.VMEM((HALO + tm, CONV_C), F32), pltpu.VMEM((tm + HALO, CONV_C), F32),
                        pltpu.VMEM((tm, CONV_C), F32), pltpu.VMEM((CONV_K, 8, CONV_C), F32)],
        compiler_params=_params(1),
    )(dc, dc, u, u, y, y, cw, lg, lb)


def outproj_fwd(x, ao, co, wout, *, tm=512):
    S, D = x.shape

    def body(x_ref, a_ref, c_ref, w_ref, o_ref):
        o_ref[...] = (x_ref[...]
                      + jnp.dot(a_ref[...], w_ref[0:ATT_W, :], preferred_element_type=F32)
                      + jnp.dot(c_ref[...], w_ref[ATT_W:ATT_W + CONV_C, :], preferred_element_type=F32))

    return pl.pallas_call(
        body, name="outproj_fwd",
        grid=(S // tm,),
        in_specs=[pl.BlockSpec((tm, D), lambda i: (i, 0)), pl.BlockSpec((tm, ATT_W), lambda i: (i, 0)),
                  pl.BlockSpec((tm, CONV_C), lambda i: (i, 0)), pl.BlockSpec((D, D), lambda i: (0, 0))],
        out_specs=pl.BlockSpec((tm, D), lambda i: (i, 0)),
        out_shape=jax.ShapeDtypeStruct((S, D), F32),
        compiler_params=_params(1),
    )(x, ao, co, wout)


def outproj_bwd(dx, ao, co, wout, *, tm=512):
    S, D = dx.shape

    def body(dx_ref, a_ref, c_ref, w_ref, da_ref, dc_ref, dw_ref):
        @pl.when(pl.program_id(0) == 0)
        def _():
            dw_ref[...] = jnp.zeros_like(dw_ref)

        dxb = dx_ref[...].astype(CDT)
        da_ref[...] = lax.dot_general(dxb, w_ref[0:ATT_W, :], NT, preferred_element_type=F32).astype(CDT)
        dc_ref[...] = lax.dot_general(dxb, w_ref[ATT_W:ATT_W + CONV_C, :], NT, preferred_element_type=F32)
        dw_ref[0:ATT_W, :] += lax.dot_general(a_ref[...], dxb, TN, preferred_element_type=F32)
        dw_ref[ATT_W:ATT_W + CONV_C, :] += lax.dot_general(c_ref[...], dxb, TN, preferred_element_type=F32)

    return pl.pallas_call(
        body, name="outproj_bwd",
        grid=(S // tm,),
        in_specs=[pl.BlockSpec((tm, D), lambda i: (i, 0)), pl.BlockSpec((tm, ATT_W), lambda i: (i, 0)),
                  pl.BlockSpec((tm, CONV_C), lambda i: (i, 0)), pl.BlockSpec((D, D), lambda i: (0, 0))],
        out_specs=[pl.BlockSpec((tm, ATT_W), lambda i: (i, 0)), pl.BlockSpec((tm, CONV_C), lambda i: (i, 0)),
                   pl.BlockSpec((D, D), lambda i: (0, 0))],
        out_shape=[jax.ShapeDtypeStruct((S, ATT_W), CDT), jax.ShapeDtypeStruct((S, CONV_C), F32),
                   jax.ShapeDtypeStruct((D, D), F32)],
        compiler_params=_params(1),
    )(dx, ao, co, wout)


def inproj_bwd(x, g, dres, dq, dk, dv, du, wextT, cs, sn, *, tm=256):
    S, D = x.shape

    def body(x_ref, g_ref, dres_ref, dq_ref, dk_ref, dv_ref, du_ref, w_ref, cs_ref, sn_ref,
             dx_ref, dw_ref, dg_ref, dp_sc):
        @pl.when(pl.program_id(0) == 0)
        def _():
            dw_ref[...] = jnp.zeros_like(dw_ref)
            dg_ref[...] = jnp.zeros_like(dg_ref)

        c, s = cs_ref[...], sn_ref[...]
        for b in range(4):
            dp_sc[:, 128 * b:128 * (b + 1)] = _rope_transpose(dq_ref[:, 128 * b:128 * (b + 1)], c, s).astype(CDT)
        for b in range(2):
            dp_sc[:, 512 + 128 * b:512 + 128 * (b + 1)] = _rope_transpose(
                dk_ref[:, 128 * b:128 * (b + 1)], c, s).astype(CDT)
        dp_sc[:, 768:1024] = dv_ref[...].astype(CDT)
        dp_sc[:, 1024:2048] = du_ref[...].astype(CDT)
        dp = dp_sc[...]
        xhat, r = _rms_parts(x_ref[...])
        gg = g_ref[...]
        h = (xhat * gg).astype(CDT)
        dh = jnp.dot(dp, w_ref[...], preferred_element_type=F32)
        dw_ref[...] += lax.dot_general(dp, h, TN, preferred_element_type=F32)
        dx, dg = _rms_bwd(dh, xhat, r, gg)
        dx_ref[...] = dres_ref[...] + dx
        dg_ref[...] += dg

    def row(w):
        return pl.BlockSpec((tm, w), lambda i: (i, 0))

    return pl.pallas_call(
        body, name="inproj_bwd",
        grid=(S // tm,),
        in_specs=[row(D), pl.BlockSpec((1, D), lambda i: (0, 0)), row(D), row(512), row(256), row(256),
                  row(1024), pl.BlockSpec((D_EXT, D), lambda i: (0, 0)), row(128), row(128)],
        out_specs=[row(D), pl.BlockSpec((D_EXT, D), lambda i: (0, 0)), pl.BlockSpec((1, D), lambda i: (0, 0))],
        out_shape=[jax.ShapeDtypeStruct((S, D), F32), jax.ShapeDtypeStruct((D_EXT, D), F32),
                   jax.ShapeDtypeStruct((1, D), F32)],
        scratch_shapes=[pltpu.VMEM((tm, D_EXT), CDT)],
        compiler_params=_params(1),
    )(x, g, dres, dq, dk, dv, du, wextT, cs, sn)


def _rope_tables(positions):
    inv_freq = 1.0 / (10000.0 ** (jnp.arange(0, HEAD_DIM, 2, dtype=F32) / HEAD_DIM))
    ang = positions.astype(F32).reshape(-1, 1) * inv_freq
    cos, sin = jnp.cos(ang), jnp.sin(ang)
    cs = jnp.tile(jnp.concatenate([cos, cos], axis=-1), (1, 2))
    sn = jnp.tile(jnp.concatenate([-sin, sin], axis=-1), (1, 2))
    return cs, sn


def _widen_w_in(w):
    q, u = w[0:512], w[768:1792]
    parts = [q]
    for base in (512, 576, 640, 704):
        parts += [w[base:base + 64], w[base:base + 64]]
    return jnp.concatenate(parts + [u], axis=0)


def _fold_w_in(d):
    parts = [d[0:512]]
    for base in (512, 640, 768, 896):
        parts.append(d[base:base + 64] + d[base + 64:base + 128])
    return jnp.concatenate(parts + [d[1024:2048]], axis=0)


def _local_step(x, positions, target, W, *, tiles=None):
    tl = tiles or {}
    L = W["n1"].shape[0]
    cs, sn = _rope_tables(positions)
    saved = []
    h = x
    for l in range(L):
        sink = W["sinks"][l].reshape(2, 4)
        sink_col = jnp.repeat(sink, BLOCK, axis=1).reshape(2, 4 * BLOCK, 1)
        x0 = h
        x1, g1, u1 = ffn_fwd(x0, W["n1"][l][None], W["f1g"][l], W["f1u"][l], W["f1d"][l], **tl.get("ffn", {}))
        q, k, v, u = inproj_fwd(x1, W["nm"][l][None], W["wext"][l], cs, sn, **tl.get("row", {}))
        ao = attn_fwd(q, k, v, sink_col)
        co, yc = conv_fwd(u, W["cw"][l], W["cb"][l][None], W["lg"][l][None], W["lb"][l][None], **tl.get("row", {}))
        x2 = outproj_fwd(x1, ao, co, W["wout"][l], **tl.get("row", {}))
        x3, g2, u2 = ffn_fwd(x2, W["n2"][l][None], W["f2g"][l], W["f2u"][l], W["f2d"][l], **tl.get("ffn", {}))
        saved.append((x0, x1, x2, g1, u1, g2, u2, q, k, v, u, ao, co, yc, sink))
        h = x3

    loss, dx, dnf = loss_head(h, W["nf"][None], target, **tl.get("row", {}))
    G = {k_: [None] * L for k_ in ("n1", "nm", "n2", "f1g", "f1u", "f1d", "f2g", "f2u", "f2d",
                                  "wext", "wout", "cw", "cb", "lg", "lb", "sinks")}
    for l in reversed(range(L)):
        x0, x1, x2, g1, u1, g2, u2, q, k, v, u, ao, co, yc, sink = saved[l]
        sink_row = jnp.repeat(sink, BLOCK, axis=1).reshape(2, 1, 4 * BLOCK)
        dx2, dgt, dup, hh, dacc, G["n2"][l] = ffn_bwd_dgrad(
            x2, W["n2"][l][None], dx, g2, u2, W["f2g"][l], W["f2u"][l], W["f2d"][l], **tl.get("ffn", {}))
        G["f2g"][l], G["f2u"][l], G["f2d"][l] = ffn_bwd_wgrad(hh, dacc, g2, u2, dgt, dup, **tl.get("wgrad", {}))
        da, dc, G["wout"][l] = outproj_bwd(dx2, ao, co, W["wout"][l], **tl.get("row", {}))
        du, G["cw"][l], G["cb"][l], G["lg"][l], G["lb"][l] = conv_bwd(
            dc, u, yc, W["cw"][l], W["lg"][l][None], W["lb"][l][None], **tl.get("row", {}))
        dq, dk, dv, dsink = attn_bwd(q, k, v, da, sink_row)
        G["sinks"][l] = jnp.sum(dsink.reshape(2, 4, BLOCK), axis=-1).reshape(N_HEADS)
        dx1, G["wext"][l], G["nm"][l] = inproj_bwd(
            x1, W["nm"][l][None], dx2, dq, dk, dv, du, W["wext"][l], cs, sn, **tl.get("inbwd", {}))
        dx, dgt, dup, hh, dacc, G["n1"][l] = ffn_bwd_dgrad(
            x0, W["n1"][l][None], dx1, g1, u1, W["f1g"][l], W["f1u"][l], W["f1d"][l], **tl.get("ffn", {}))
        G["f1g"][l], G["f1u"][l], G["f1d"][l] = ffn_bwd_wgrad(hh, dacc, g1, u1, dgt, dup, **tl.get("wgrad", {}))
    G = {k_: jnp.stack(v_) for k_, v_ in G.items()}
    G["nf"] = dnf
    return loss, dx, G


MESH = pl.DeviceIdType.MESH
ANY = pl.BlockSpec(memory_space=pl.ANY)
N_CHIPS = 4
N_DEV = 8


def _coords():
    return lax.axis_index("x"), lax.axis_index("y"), lax.axis_index("c")


def _other_chips(x, y):
    return [(1 - x, y), (x, 1 - y), (1 - x, 1 - y)]


def all_gather_weights(shards, cw):
    n = len(shards)
    ins_all = list(shards) + [cw]
    out_shape = [jax.ShapeDtypeStruct((s.shape[0], N_CHIPS * s.shape[1], s.shape[2]), s.dtype) for s in shards]
    out_shape.append(jax.ShapeDtypeStruct((N_CHIPS,) + cw.shape, cw.dtype))

    def body(*refs):
        ins = refs[:n + 1]
        outs = refs[n + 1:2 * n + 2]
        send, recv, lsem = refs[2 * n + 2:]
        x, y, c = _coords()
        j = 2 * x + y

        def slab(a):
            if a == n:
                return outs[a].at[j]
            rows = ins[a].shape[1]
            return outs[a].at[:, pl.ds(j * rows, rows), :]

        local = [pltpu.make_async_copy(ins[a], slab(a), lsem.at[a]) for a in range(n + 1)]
        for cp in local:
            cp.start()
        remote = []
        for a in range(n + 1):
            for p, (px, py) in enumerate(_other_chips(x, y)):
                cp = pltpu.make_async_remote_copy(ins[a], slab(a), send.at[a, p], recv.at[a, p],
                                                  device_id=(px, py, c), device_id_type=MESH)
                cp.start()
                remote.append(cp)
        for cp in local:
            cp.wait()
        for cp in remote:
            cp.wait()

    return pl.pallas_call(
        body, name="all_gather_weights",
        in_specs=[ANY] * (n + 1), out_specs=[ANY] * (n + 1), out_shape=out_shape,
        scratch_shapes=[pltpu.SemaphoreType.DMA((n + 1, 3)), pltpu.SemaphoreType.DMA((n + 1, 3)),
                        pltpu.SemaphoreType.DMA((n + 1,))],
    )(*ins_all)


def swap_halves(grads):
    n = len(grads)
    out_shape = [jax.ShapeDtypeStruct(g.shape[:2] + g.shape[3:], g.dtype) for g in grads]

    def body(*refs):
        ins, outs = refs[:n], refs[n:2 * n]
        send, recv = refs[2 * n:]
        x, y, c = _coords()
        cps = []
        for a in range(n):
            cp = pltpu.make_async_remote_copy(ins[a].at[:, :, 1 - c], outs[a], send.at[a], recv.at[a],
                                              device_id=(x, y, 1 - c), device_id_type=MESH)
            cp.start()
            cps.append(cp)
        for cp in cps:
            cp.wait()

    return pl.pallas_call(
        body, name="swap_halves",
        in_specs=[ANY] * n, out_specs=[ANY] * n, out_shape=out_shape,
        scratch_shapes=[pltpu.SemaphoreType.DMA((n,)), pltpu.SemaphoreType.DMA((n,))],
    )(*grads)


def add_half(g5, r1, c_idx):
    L, _, _, r, D = g5.shape

    def body(c_ref, g_ref, r_ref, o_ref):
        o_ref[...] = (g_ref[...] + r_ref[...]).astype(CDT)

    return pl.pallas_call(
        body, name="add_half",
        grid_spec=pltpu.PrefetchScalarGridSpec(
            num_scalar_prefetch=1, grid=(L, N_CHIPS),
            in_specs=[pl.BlockSpec((None, None, None, r, D), lambda l, s, cr: (l, s, cr[0], 0, 0)),
                      pl.BlockSpec((None, None, r, D), lambda l, s, cr: (l, s, 0, 0))],
            out_specs=pl.BlockSpec((None, None, r, D), lambda l, s, cr: (l, s, 0, 0))),
        out_shape=jax.ShapeDtypeStruct((L, N_CHIPS, r, D), CDT),
        compiler_params=_params(2),
    )(c_idx, g5, r1)


def exchange_partials(parts):
    n = len(parts)
    out_shape = [jax.ShapeDtypeStruct((3, p.shape[0]) + p.shape[2:], p.dtype) for p in parts]

    def body(*refs):
        ins, outs = refs[:n], refs[n:2 * n]
        send, recv = refs[2 * n:]
        x, y, c = _coords()
        cps = []
        for a in range(n):
            for p, (px, py) in enumerate(_other_chips(x, y)):
                cp = pltpu.make_async_remote_copy(ins[a].at[:, 2 * px + py], outs[a].at[p], send.at[a, p],
                                                  recv.at[a, p], device_id=(px, py, c), device_id_type=MESH)
                cp.start()
                cps.append(cp)
        for cp in cps:
            cp.wait()

    return pl.pallas_call(
        body, name="exchange_partials",
        in_specs=[ANY] * n, out_specs=[ANY] * n, out_shape=out_shape,
        scratch_shapes=[pltpu.SemaphoreType.DMA((n, 3)), pltpu.SemaphoreType.DMA((n, 3))],
    )(*parts)


def sum_partials(part, recv3, j_idx):
    L, _, r, D = part.shape

    def body(j_ref, p_ref, r_ref, o_ref):
        o_ref[...] = ((p_ref[...].astype(F32) + r_ref[0].astype(F32)) + r_ref[1].astype(F32)) + r_ref[2].astype(F32)

    return pl.pallas_call(
        body, name="sum_partials",
        grid_spec=pltpu.PrefetchScalarGridSpec(
            num_scalar_prefetch=1, grid=(L,),
            in_specs=[pl.BlockSpec((None, None, r, D), lambda l, jr: (l, jr[0], 0, 0)),
                      pl.BlockSpec((3, None, r, D), lambda l, jr: (0, l, 0, 0))],
            out_specs=pl.BlockSpec((None, r, D), lambda l, jr: (l, 0, 0))),
        out_shape=jax.ShapeDtypeStruct((L, r, D), F32),
        compiler_params=_params(1),
    )(j_idx, part, recv3)


def share_halves(totals):
    n = len(totals)
    out_shape = [jax.ShapeDtypeStruct((t.shape[0], 2) + t.shape[1:], t.dtype) for t in totals]

    def body(*refs):
        ins, outs = refs[:n], refs[n:2 * n]
        send, recv, lsem = refs[2 * n:]
        x, y, c = _coords()
        local, remote = [], []
        for a in range(n):
            lc = pltpu.make_async_copy(ins[a], outs[a].at[:, c], lsem.at[a])
            lc.start()
            local.append(lc)
            cp = pltpu.make_async_remote_copy(ins[a], outs[a].at[:, c], send.at[a], recv.at[a],
                                              device_id=(x, y, 1 - c), device_id_type=MESH)
            cp.start()
            remote.append(cp)
        for cp in local + remote:
            cp.wait()

    return pl.pallas_call(
        body, name="share_halves",
        in_specs=[ANY] * n, out_specs=[ANY] * n, out_shape=out_shape,
        scratch_shapes=[pltpu.SemaphoreType.DMA((n,)), pltpu.SemaphoreType.DMA((n,)),
                        pltpu.SemaphoreType.DMA((n,))],
    )(*totals)


def all_reduce_small(vec):
    R = vec.shape[0]

    def body(v_ref, o_ref, buf, send, recv):
        x, y, c = _coords()
        me = 4 * x + 2 * y + c
        buf[me] = v_ref[...]
        cps = []
        for m in range(1, N_DEV):
            dx, dy, dc = (m >> 2) & 1, (m >> 1) & 1, m & 1
            cp = pltpu.make_async_remote_copy(v_ref, buf.at[me], send.at[m - 1], recv.at[m - 1],
                                              device_id=((x + dx) % 2, (y + dy) % 2, (c + dc) % 2),
                                              device_id_type=MESH)
            cp.start()
            cps.append(cp)
        for cp in cps:
            cp.wait()
        acc = buf[0]
        for d in range(1, N_DEV):
            acc = acc + buf[d]
        o_ref[...] = acc

    return pl.pallas_call(
        body, name="all_reduce_small",
        in_specs=[pl.BlockSpec(memory_space=pltpu.VMEM)], out_specs=pl.BlockSpec(memory_space=pltpu.VMEM),
        out_shape=jax.ShapeDtypeStruct(vec.shape, F32),
        scratch_shapes=[pltpu.VMEM((N_DEV, R, 128), F32), pltpu.SemaphoreType.DMA((N_DEV - 1,)),
                        pltpu.SemaphoreType.DMA((N_DEV - 1,))],
    )(vec)


def adamw(w, g, m, v, *, tm=512):
    R, C = w.shape
    tm = max(t for t in range(8, min(tm, R) + 1, 8) if R % t == 0)
    c1 =1.0 - ADAM_B1 ** ADAM_STEP
    c2 = 1.0 - ADAM_B2 ** ADAM_STEP

    def body(w_ref, g_ref, m_ref, v_ref, d_ref, nm_ref, nv_ref):
        gg = g_ref[...]
        nm = ADAM_B1 * m_ref[...] + (1.0 - ADAM_B1) * gg
        nv = ADAM_B2 * v_ref[...] + (1.0 - ADAM_B2) * (gg * gg)
        nm_ref[...] = nm
        nv_ref[...] = nv
        d_ref[...] = -ADAM_LR * ((nm / c1) / (jnp.sqrt(nv / c2) + ADAM_EPS) + ADAM_WD * w_ref[...])

    blk = pl.BlockSpec((tm, C), lambda i: (i, 0))
    return pl.pallas_call(
        body, name="adamw",
        grid=(pl.cdiv(R, tm),),
        in_specs=[blk] * 4, out_specs=[blk] * 3,
        out_shape=[jax.ShapeDtypeStruct((R, C), F32)] * 3,
        compiler_params=_params(1),
    )(w, g, m, v)


_SMALL = (("n1", (2, D_MODEL)), ("nm", (2, D_MODEL)), ("n2", (2, D_MODEL)), ("nf", (D_MODEL,)),
          ("cb", (2, CONV_C)), ("lg", (2, CONV_C)), ("lb", (2, CONV_C)), ("sinks", (2, N_HEADS)),
          ("cw", (2, CONV_K, CONV_C)))


def _pack(parts, rows):
    flat = jnp.concatenate([p.reshape(-1).astype(F32) for p in parts])
    return jnp.pad(flat, (0, rows * 128 - flat.shape[0])).reshape(rows, 128)


def _unpack(block, shapes):
    flat = block.reshape(-1)
    out, o = [], 0
    for shp in shapes:
        n = 1
        for s in shp:
            n *= s
        out.append(flat[o:o + n].reshape(shp))
        o += n
    return out


def kernel(x, positions, ffn1_norm, ffn1_w_gate, ffn1_w_up, ffn1_w_down, mix_norm, w_in, conv_w, conv_b, conv_ln_g, conv_ln_b, attn_sinks, w_out, ffn2_norm, ffn2_w_gate, ffn2_w_up, ffn2_w_down, final_norm, loss_target, m_ffn1_norm, m_ffn1_w_gate, m_ffn1_w_up, m_ffn1_w_down, m_mix_norm, m_w_in, m_conv_w, m_conv_b, m_conv_ln_g, m_conv_ln_b, m_attn_sinks, m_w_out, m_ffn2_norm, m_ffn2_w_gate, m_ffn2_w_up, m_ffn2_w_down, m_final_norm, v_ffn1_norm, v_ffn1_w_gate, v_ffn1_w_up, v_ffn1_w_down, v_mix_norm, v_w_in, v_conv_w, v_conv_b, v_conv_ln_g, v_conv_ln_b, v_attn_sinks, v_w_out, v_ffn2_norm, v_ffn2_w_gate, v_ffn2_w_up, v_ffn2_w_down, v_final_norm):
    cx, cy, cc = _coords()
    chip = 2 * cx + cy
    L = ffn1_norm.shape[0]
    tr = lambda a: jnp.swapaxes(a, 1, 2)

    shards = [tr(ffn1_w_gate), tr(ffn1_w_up), ffn1_w_down, tr(ffn2_w_gate), tr(ffn2_w_up), ffn2_w_down,
              tr(w_in), w_out]
    gathered = all_gather_weights([s.astype(CDT) for s in shards], conv_w)
    f1g, f1u, f1d, f2g, f2u, f2d, winT, wout = gathered[:8]
    cw_full = jnp.transpose(gathered[8], (1, 2, 0, 3)).reshape(L, CONV_K, CONV_C)
    W = dict(n1=ffn1_norm, nm=mix_norm, n2=ffn2_norm, nf=final_norm, f1g=f1g, f1u=f1u, f1d=f1d,
             f2g=f2g, f2u=f2u, f2d=f2d, wext=jnp.stack([_widen_w_in(winT[l]) for l in range(L)]), wout=wout,
             cw=cw_full, cb=conv_b, lg=conv_ln_g, lb=conv_ln_b, sinks=attn_sinks)

    loss, dx, G = _local_step(x[0], positions, loss_target[0], W)

    small_shapes = [shp for _, shp in _SMALL]
    n_small = 1 + sum(math.prod(s) for s in small_shapes)
    rows = -(-n_small // 1024) * 8
    packed = _pack([loss] + [G[k] for k, _ in _SMALL], rows)
    summed = all_reduce_small(packed)
    loss_out, *small = _unpack(summed, [()] + small_shapes)
    gs = dict(zip([k for k, _ in _SMALL], small))
    gs["cw"] = lax.dynamic_slice_in_dim(gs["cw"], chip * (CONV_C // N_CHIPS), CONV_C // N_CHIPS, axis=2)

    big = [G["f1g"], G["f1u"], G["f1d"], G["f2g"], G["f2u"], G["f2d"],
           jnp.stack([_fold_w_in(G["wext"][l]) for l in range(L)]), G["wout"]]
    g5 = [b.reshape(L, N_CHIPS, 2, b.shape[1] // (2 * N_CHIPS), b.shape[2]) for b in big]
    r1 = swap_halves(g5)
    c_idx = jnp.reshape(cc, (1,)).astype(jnp.int32)
    j_idx = jnp.reshape(chip, (1,)).astype(jnp.int32)
    parts = [add_half(a, b, c_idx) for a, b in zip(g5, r1)]
    recv3 = exchange_partials(parts)
    totals = [sum_partials(p, r, j_idx) for p, r in zip(parts, recv3)]
    halves = share_halves(totals)
    gsh = [h.reshape(L, 2 * h.shape[2], h.shape[3]) for h in halves]
    g_f1g, g_f1u, g_f1d, g_f2g, g_f2u, g_f2d, g_win, g_wout = gsh
    grads = dict(ffn1_norm=gs["n1"], ffn1_w_gate=tr(g_f1g), ffn1_w_up=tr(g_f1u), ffn1_w_down=g_f1d,
                 mix_norm=gs["nm"], w_in=tr(g_win), conv_w=gs["cw"], conv_b=gs["cb"], conv_ln_g=gs["lg"],
                 conv_ln_b=gs["lb"], attn_sinks=gs["sinks"], w_out=g_wout, ffn2_norm=gs["n2"],
                 ffn2_w_gate=tr(g_f2g), ffn2_w_up=tr(g_f2u), ffn2_w_down=g_f2d, final_norm=gs["nf"])

    weights = dict(ffn1_norm=ffn1_norm, ffn1_w_gate=ffn1_w_gate, ffn1_w_up=ffn1_w_up, ffn1_w_down=ffn1_w_down,
                   mix_norm=mix_norm, w_in=w_in, conv_w=conv_w, conv_b=conv_b, conv_ln_g=conv_ln_g,
                   conv_ln_b=conv_ln_b, attn_sinks=attn_sinks, w_out=w_out, ffn2_norm=ffn2_norm,
                   ffn2_w_gate=ffn2_w_gate, ffn2_w_up=ffn2_w_up, ffn2_w_down=ffn2_w_down, final_norm=final_norm)
    moms = dict(ffn1_norm=(m_ffn1_norm, v_ffn1_norm), ffn1_w_gate=(m_ffn1_w_gate, v_ffn1_w_gate),
                ffn1_w_up=(m_ffn1_w_up, v_ffn1_w_up), ffn1_w_down=(m_ffn1_w_down, v_ffn1_w_down),
                mix_norm=(m_mix_norm, v_mix_norm), w_in=(m_w_in, v_w_in), conv_w=(m_conv_w, v_conv_w),
                conv_b=(m_conv_b, v_conv_b), conv_ln_g=(m_conv_ln_g, v_conv_ln_g),
                conv_ln_b=(m_conv_ln_b, v_conv_ln_b), attn_sinks=(m_attn_sinks, v_attn_sinks),
                w_out=(m_w_out, v_w_out), ffn2_norm=(m_ffn2_norm, v_ffn2_norm),
                ffn2_w_gate=(m_ffn2_w_gate, v_ffn2_w_gate), ffn2_w_up=(m_ffn2_w_up, v_ffn2_w_up),
                ffn2_w_down=(m_ffn2_w_down, v_ffn2_w_down), final_norm=(m_final_norm, v_final_norm))
    names = list(weights)
    big_names = ("ffn1_w_gate", "ffn1_w_up", "ffn1_w_down", "w_in", "w_out", "ffn2_w_gate", "ffn2_w_up",
                 "ffn2_w_down")
    delta, new_m, new_v = {}, {}, {}
    for nme in big_names:
        shp = weights[nme].shape
        two = lambda a: a.reshape(shp[0] * shp[1], shp[2])
        d, nm_, nv_ = adamw(two(weights[nme]), two(grads[nme]), two(moms[nme][0]), two(moms[nme][1]))
        delta[nme], new_m[nme], new_v[nme] = d.reshape(shp), nm_.reshape(shp), nv_.reshape(shp)
    small_names = [nme for nme in names if nme not in big_names]
    s_shapes = [weights[nme].shape for nme in small_names]
    n_tot = sum(math.prod(s) for s in s_shapes)
    srows = -(-n_tot // 1024) * 8
    d, nm_, nv_ = adamw(_pack([weights[nme] for nme in small_names], srows),
                        _pack([grads[nme] for nme in small_names], srows),
                        _pack([moms[nme][0] for nme in small_names], srows),
                        _pack([moms[nme][1] for nme in small_names], srows))
    for nme, dd, mm, vv in zip(small_names, _unpack(d, s_shapes), _unpack(nm_, s_shapes), _unpack(nv_, s_shapes)):
        delta[nme], new_m[nme], new_v[nme] = dd, mm, vv

    return (loss_out, dx[None], *[grads[nme] for nme in names], *[delta[nme] for nme in names],
            *[new_m[nme] for nme in names], *[new_v[nme] for nme in names])
```

```python
import math

import jax
import jax.numpy as jnp
from jax import lax
from jax.experimental import pallas as pl
from jax.experimental.pallas import tpu as pltpu

F32 = jnp.float32
CDT = jnp.bfloat16
D_MODEL = 1024
D_FF = 2816
N_HEADS = 8
HEAD_DIM = 64
BLOCK = 128
CONV_K = 31
CONV_C = 512
ATT_W = 512
D_EXT = 2048
EPS = 1e-5
HALO = 32
NEG = float(jnp.finfo(jnp.float32).min)
VMEM_LIMIT = 56 * 1024 * 1024

ADAM_LR = 0.001
ADAM_B1 = 0.9
ADAM_B2 = 0.999
ADAM_EPS = 1e-08
ADAM_WD = 0.01
ADAM_STEP = 10

NT = (((1,), (1,)), ((), ()))
TN = (((0,), (0,)), ((), ()))


MESH = pl.DeviceIdType.MESH
ANY = pl.BlockSpec(memory_space=pl.ANY)
N_CHIPS = 4
N_DEV = 8


def _params(n_axes):
    return pltpu.CompilerParams(dimension_semantics=("arbitrary",) * n_axes, vmem_limit_bytes=VMEM_LIMIT)


class _Comm:
    def __init__(self, name, inputs, out_shape, sems, descs):
        self.name, self.inputs, self.out_shape, self.sems, self.descs = name, list(inputs), list(out_shape), list(sems), descs


def _merge(*ops):
    ops = [o for o in ops if o is not None]
    if len(ops) == 1:
        return ops[0]

    def descs(cins, couts, sems):
        out, i, o, s = [], 0, 0, 0
        for op in ops:
            ni, no, ns = len(op.inputs), len(op.out_shape), len(op.sems)
            out += op.descs(cins[i:i + ni], couts[o:o + no], sems[s:s + ns])
            i, o, s = i + ni, o + no, s + ns
        return out

    return _Comm("_".join(o.name for o in ops), sum((o.inputs for o in ops), []),
                 sum((o.out_shape for o in ops), []), sum((o.sems for o in ops), []), descs)


def _split(couts, *ops):
    res, o = [], 0
    for op in ops:
        res.append(couts[o:o + len(op.out_shape)])
        o += len(op.out_shape)
    return res


def _hosted(comm):
    if comm is None:
        return pl.pallas_call

    def make(body, *, name, grid, in_specs, out_specs, out_shape, compiler_params, scratch_shapes=()):
        single = not isinstance(out_shape, (list, tuple))
        o_specs = [out_specs] if single else list(out_specs)
        o_shape = [out_shape] if single else list(out_shape)
        n_in, n_out, n_sc = len(in_specs), len(o_specs), len(scratch_shapes)
        c_in, c_out = len(comm.inputs), len(comm.out_shape)

        def hosted(*refs):
            ins, cins = refs[:n_in], refs[n_in:n_in + c_in]
            o0 = n_in + c_in
            outs, couts = refs[o0:o0 + n_out], refs[o0 + n_out:o0 + n_out + c_out]
            s0 = o0 + n_out + c_out
            scr, sems = refs[s0:s0 + n_sc], refs[s0 + n_sc:]
            first = pl.program_id(0) == 0
            last = pl.program_id(0) == grid[0] - 1
            for ax in range(1, len(grid)):
                first = first & (pl.program_id(ax) == 0)
                last = last & (pl.program_id(ax) == grid[ax] - 1)

            @pl.when(first)
            def _():
                for d in comm.descs(cins, couts, sems):
                    d.start()

            body(*ins, *outs, *scr)

            @pl.when(last)
            def _():
                for d in comm.descs(cins, couts, sems):
                    d.wait()

        call = pl.pallas_call(
            hosted, name=f"{name}_{comm.name}", grid=grid,
            in_specs=list(in_specs) + [ANY] * c_in, out_specs=o_specs + [ANY] * c_out,
            out_shape=o_shape + comm.out_shape, scratch_shapes=list(scratch_shapes) + comm.sems,
            compiler_params=compiler_params)

        def run(*args):
            res = call(*args, *comm.inputs)
            return (res[0] if single else list(res[:n_out])), list(res[n_out:])

        return run

    return make


def _run_comm(comm):
    c_in = len(comm.inputs)

    def body(*refs):
        cins, couts, sems = refs[:c_in], refs[c_in:c_in + len(comm.out_shape)], refs[c_in + len(comm.out_shape):]
        ds = comm.descs(cins, couts, sems)
        for d in ds:
            d.start()
        for d in ds:
            d.wait()

    return list(pl.pallas_call(
        body, name=comm.name, in_specs=[ANY] * c_in, out_specs=[ANY] * len(comm.out_shape),
        out_shape=comm.out_shape, scratch_shapes=comm.sems)(*comm.inputs))


def _coords():
    return lax.axis_index("x"), lax.axis_index("y"), lax.axis_index("c")


def _other_chips(x, y):
    return [(1 - x, y), (x, 1 - y), (1 - x, 1 - y)]


def ag_op(shards):
    n = len(shards)

    def descs(cins, couts, sems):
        send, recv, lsem = sems
        x, y, c = _coords()
        j = 2 * x + y
        ds = [pltpu.make_async_copy(cins[a], couts[a].at[j], lsem.at[a]) for a in range(n)]
        for a in range(n):
            for p, (px, py) in enumerate(_other_chips(x, y)):
                ds.append(pltpu.make_async_remote_copy(cins[a], couts[a].at[j], send.at[a, p], recv.at[a, p],
                                                       device_id=(px, py, c), device_id_type=MESH))
        return ds

    return _Comm("ag", shards, [jax.ShapeDtypeStruct((N_CHIPS,) + s.shape, s.dtype) for s in shards],
                 [pltpu.SemaphoreType.DMA((n, 3)), pltpu.SemaphoreType.DMA((n, 3)), pltpu.SemaphoreType.DMA((n,))],
                 descs)


def swap_op(grads):
    n = len(grads)

    def descs(cins, couts, sems):
        send, recv = sems
        x, y, c = _coords()
        return [pltpu.make_async_remote_copy(cins[a].at[:, 1 - c], couts[a], send.at[a], recv.at[a],
                                             device_id=(x, y, 1 - c), device_id_type=MESH) for a in range(n)]

    return _Comm("swap", grads, [jax.ShapeDtypeStruct(g.shape[:1] + g.shape[2:], g.dtype) for g in grads],
                 [pltpu.SemaphoreType.DMA((n,)), pltpu.SemaphoreType.DMA((n,))], descs)


def exchange_op(parts):
    n = len(parts)

    def descs(cins, couts, sems):
        send, recv = sems
        x, y, c = _coords()
        ds = []
        for a in range(n):
            for p, (px, py) in enumerate(_other_chips(x, y)):
                ds.append(pltpu.make_async_remote_copy(cins[a].at[2 * px + py], couts[a].at[p], send.at[a, p],
                                                       recv.at[a, p], device_id=(px, py, c), device_id_type=MESH))
        return ds

    return _Comm("xchg", parts, [jax.ShapeDtypeStruct((3,) + p.shape[1:], p.dtype) for p in parts],
                 [pltpu.SemaphoreType.DMA((n, 3)), pltpu.SemaphoreType.DMA((n, 3))], descs)


def share_op(totals):
    n = len(totals)

    def descs(cins, couts, sems):
        send, recv, lsem = sems
        x, y, c = _coords()
        ds = [pltpu.make_async_copy(cins[a], couts[a].at[c], lsem.at[a]) for a in range(n)]
        ds += [pltpu.make_async_remote_copy(cins[a], couts[a].at[c], send.at[a], recv.at[a],
                                            device_id=(x, y, 1 - c), device_id_type=MESH) for a in range(n)]
        return ds

    return _Comm("share", totals, [jax.ShapeDtypeStruct((2,) + t.shape, t.dtype) for t in totals],
                 [pltpu.SemaphoreType.DMA((n,)), pltpu.SemaphoreType.DMA((n,)), pltpu.SemaphoreType.DMA((n,))],
                 descs)


def _sigmoid(z):
    return 1.0 / (1.0 + jnp.exp(-z))


def _rms_parts(xf):
    r = lax.rsqrt(jnp.mean(xf * xf, axis=-1, keepdims=True) + EPS)
    return xf * r, r


def _rms_bwd(dh, xhat, r, g):
    dg = jnp.sum(dh * xhat, axis=0, keepdims=True)
    dxhat = dh * g
    dx = r * (dxhat - xhat * jnp.mean(dxhat * xhat, axis=-1, keepdims=True))
    return dx, dg


def ffn_fwd(x, g, wgT, wuT, wd, *, tm=512, tf=1408, comm=None):
    S, D = x.shape
    F = wgT.shape[0]
    nf = F // tf

    def body(x_ref, g_ref, wg_ref, wu_ref, wd_ref, o_ref, gate_ref, up_ref, h_sc, acc_sc):
        j = pl.program_id(1)

        @pl.when(j == 0)
        def _():
            xhat, _ = _rms_parts(x_ref[...])
            h_sc[...] = (xhat * g_ref[...]).astype(CDT)
            acc_sc[...] = jnp.zeros_like(acc_sc)

        h = h_sc[...]
        gt = lax.dot_general(h, wg_ref[...], NT, preferred_element_type=F32)
        ut = lax.dot_general(h, wu_ref[...], NT, preferred_element_type=F32)
        gate_ref[...] = gt.astype(CDT)
        up_ref[...] = ut.astype(CDT)
        a = (gt * _sigmoid(gt) * ut).astype(CDT)
        acc_sc[...] += jnp.dot(a, wd_ref[...], preferred_element_type=F32)

        @pl.when(j == nf - 1)
        def _():
            o_ref[...] = x_ref[...] + 0.5 * acc_sc[...]

    return _hosted(comm)(
        body, name="ffn_fwd",
        grid=(S // tm, nf),
        in_specs=[pl.BlockSpec((tm, D), lambda i, j: (i, 0)),
                  pl.BlockSpec((1, D), lambda i, j: (0, 0)),
                  pl.BlockSpec((tf, D), lambda i, j: (j, 0)),
                  pl.BlockSpec((tf, D), lambda i, j: (j, 0)),
                  pl.BlockSpec((tf, D), lambda i, j: (j, 0))],
        out_specs=[pl.BlockSpec((tm, D), lambda i, j: (i, 0)),
                   pl.BlockSpec((tm, tf), lambda i, j: (i, j)),
                   pl.BlockSpec((tm, tf), lambda i, j: (i, j))],
        out_shape=[jax.ShapeDtypeStruct((S, D), F32),
                   jax.ShapeDtypeStruct((S, F), CDT),
                   jax.ShapeDtypeStruct((S, F), CDT)],
        scratch_shapes=[pltpu.VMEM((tm, D), CDT), pltpu.VMEM((tm, D), F32)],
        compiler_params=_params(2),
    )(x, g, wgT, wuT, wd)


def ffn_bwd_dgrad(x, g, dy, gate, up, wgT, wuT, wd, *, tm=512, tf=1408, comm=None):
    S, D = x.shape
    F = wgT.shape[0]
    nf = F // tf

    def body(x_ref, g_ref, dy_ref, gate_ref, up_ref, wg_ref, wu_ref, wd_ref,
             dx_ref, dgate_ref, dup_ref, h_ref, dacc_ref, dg_ref, dh_sc):
        i = pl.program_id(0)
        j = pl.program_id(1)

        @pl.when(j == 0)
        def _():
            xhat, _ = _rms_parts(x_ref[...])
            h_ref[...] = (xhat * g_ref[...]).astype(CDT)
            dacc_ref[...] = (0.5 * dy_ref[...]).astype(CDT)
            dh_sc[...] = jnp.zeros_like(dh_sc)

        @pl.when((i == 0) & (j == 0))
        def _():
            dg_ref[...] = jnp.zeros_like(dg_ref)

        d_a = lax.dot_general(dacc_ref[...], wd_ref[...], NT, preferred_element_type=F32)
        gt = gate_ref[...].astype(F32)
        ut = up_ref[...].astype(F32)
        sg = _sigmoid(gt)
        d_up = (d_a * (gt * sg)).astype(CDT)
        d_gate = (d_a * ut * (sg * (1.0 + gt * (1.0 - sg)))).astype(CDT)
        dgate_ref[...] = d_gate
        dup_ref[...] = d_up
        dh_sc[...] += (jnp.dot(d_gate, wg_ref[...], preferred_element_type=F32)
                       + jnp.dot(d_up, wu_ref[...], preferred_element_type=F32))

        @pl.when(j == nf - 1)
        def _():
            xhat, r = _rms_parts(x_ref[...])
            dx, dg = _rms_bwd(dh_sc[...], xhat, r, g_ref[...])
            dx_ref[...] = dy_ref[...] + dx
            dg_ref[...] += dg

    return _hosted(comm)(
        body, name="ffn_bwd_dgrad",
        grid=(S // tm, nf),
        in_specs=[pl.BlockSpec((tm, D), lambda i, j: (i, 0)),
                  pl.BlockSpec((1, D), lambda i, j: (0, 0)),
                  pl.BlockSpec((tm, D), lambda i, j: (i, 0)),
                  pl.BlockSpec((tm, tf), lambda i, j: (i, j)),
                  pl.BlockSpec((tm, tf), lambda i, j: (i, j)),
                  pl.BlockSpec((tf, D), lambda i, j: (j, 0)),
                  pl.BlockSpec((tf, D), lambda i, j: (j, 0)),
                  pl.BlockSpec((tf, D), lambda i, j: (j, 0))],
        out_specs=[pl.BlockSpec((tm, D), lambda i, j: (i, 0)),
                   pl.BlockSpec((tm, tf), lambda i, j: (i, j)),
                   pl.BlockSpec((tm, tf), lambda i, j: (i, j)),
                   pl.BlockSpec((tm, D), lambda i, j: (i, 0)),
                   pl.BlockSpec((tm, D), lambda i, j: (i, 0)),
                   pl.BlockSpec((1, D), lambda i, j: (0, 0))],
        out_shape=[jax.ShapeDtypeStruct((S, D), F32),
                   jax.ShapeDtypeStruct((S, F), CDT),
                   jax.ShapeDtypeStruct((S, F), CDT),
                   jax.ShapeDtypeStruct((S, D), CDT),
                   jax.ShapeDtypeStruct((S, D), CDT),
                   jax.ShapeDtypeStruct((1, D), F32)],
        scratch_shapes=[pltpu.VMEM((tm, D), F32)],
        compiler_params=_params(2),
    )(x, g, dy, gate, up, wgT, wuT, wd)


def ffn_bwd_wgrad(h, dacc, gate, up, dgate, dup, *, tk=1024, tf=256, comm=None):
    S, D = h.shape
    F = gate.shape[1]
    tk = min(tk, S)

    def body(h_ref, dacc_ref, gate_ref, up_ref, dgate_ref, dup_ref, dwg_ref, dwu_ref, dwd_ref):
        k = pl.program_id(1)

        @pl.when(k == 0)
        def _():
            dwg_ref[...] = jnp.zeros_like(dwg_ref)
            dwu_ref[...] = jnp.zeros_like(dwu_ref)
            dwd_ref[...] = jnp.zeros_like(dwd_ref)

        hh = h_ref[...]
        dwg_ref[...] += lax.dot_general(dgate_ref[...], hh, TN, preferred_element_type=F32)
        dwu_ref[...] += lax.dot_general(dup_ref[...], hh, TN, preferred_element_type=F32)
        gt = gate_ref[...].astype(F32)
        a = (gt * _sigmoid(gt) * up_ref[...].astype(F32)).astype(CDT)
        dwd_ref[...] += lax.dot_general(a, dacc_ref[...], TN, preferred_element_type=F32)

    tok = pl.BlockSpec((tk, D), lambda j, k: (k, 0))
    act = pl.BlockSpec((tk, tf), lambda j, k: (k, j))
    out = pl.BlockSpec((tf, D), lambda j, k: (j, 0))
    return _hosted(comm)(
        body, name="ffn_bwd_wgrad",
        grid=(F // tf, S // tk),
        in_specs=[tok, tok, act, act, act, act],
        out_specs=[out, out, out],
        out_shape=[jax.ShapeDtypeStruct((F, D), F32)] * 3,
        compiler_params=_params(2),
    )(h, dacc, gate, up, dgate, dup)


def loss_head(x, g, target, *, tm=512):
    S, D = x.shape

    def body(x_ref, g_ref, t_ref, loss_ref, dx_ref, dg_ref):
        @pl.when(pl.program_id(0) == 0)
        def _():
            loss_ref[...] = jnp.zeros_like(loss_ref)
            dg_ref[...] = jnp.zeros_like(dg_ref)

        xhat, r = _rms_parts(x_ref[...])
        gg = g_ref[...]
        err = xhat * gg - t_ref[...]
        loss_ref[...] += 0.5 * jnp.sum(jnp.mean(err * err, axis=-1, keepdims=True), axis=0, keepdims=True)
        dx, dg = _rms_bwd(err * (1.0 / D), xhat, r, gg)
        dx_ref[...] = dx
        dg_ref[...] += dg

    row = pl.BlockSpec((tm, D), lambda i: (i, 0))
    vec = pl.BlockSpec((1, D), lambda i: (0, 0))
    return pl.pallas_call(
        body, name="loss_head",
        grid=(S // tm,),
        in_specs=[row, vec, row],
        out_specs=[pl.BlockSpec((1, 1), lambda i: (0, 0)), row, vec],
        out_shape=[jax.ShapeDtypeStruct((1, 1), F32), jax.ShapeDtypeStruct((S, D), F32),
                   jax.ShapeDtypeStruct((1, D), F32)],
        compiler_params=_params(1),
    )(x, g, target)


def _rope_apply(t, cs, sn):
    lane = lax.broadcasted_iota(jnp.int32, t.shape, 1)
    first = (lane % HEAD_DIM) < (HEAD_DIM // 2)
    rot = jnp.where(first, pltpu.roll(t, 128 - HEAD_DIM // 2, 1), pltpu.roll(t, HEAD_DIM // 2, 1))
    return t * cs + rot * sn


def _rope_transpose(d, cs, sn):
    lane = lax.broadcasted_iota(jnp.int32, d.shape, 1)
    first = (lane % HEAD_DIM) < (HEAD_DIM // 2)
    ds = d * sn
    rot = jnp.where(first, pltpu.roll(ds, 128 - HEAD_DIM // 2, 1), pltpu.roll(ds, HEAD_DIM // 2, 1))
    return d * cs + rot


def inproj_fwd(x, g, wextT, cs, sn, *, tm=512, comm=None):
    S, D = x.shape
    scale = HEAD_DIM ** -0.5

    def body(x_ref, g_ref, w_ref, cs_ref, sn_ref, q_ref, k_ref, v_ref, u_ref):
        xhat, _ = _rms_parts(x_ref[...])
        h = (xhat * g_ref[...]).astype(CDT)
        p = lax.dot_general(h, w_ref[...], NT, preferred_element_type=F32)
        c, s = cs_ref[...], sn_ref[...]
        for b in range(4):
            q_ref[:, 128 * b:128 * (b + 1)] = (_rope_apply(p[:, 128 * b:128 * (b + 1)], c, s) * scale).astype(CDT)
        for b in range(2):
            k_ref[:, 128 * b:128 * (b + 1)] = _rope_apply(p[:, 512 + 128 * b:512 + 128 * (b + 1)], c, s).astype(CDT)
        v_ref[...] = p[:, 768:1024].astype(CDT)
        u_ref[...] = p[:, 1024:2048]

    def row(w):
        return pl.BlockSpec((tm, w), lambda i: (i, 0))

    return _hosted(comm)(
        body, name="inproj_fwd",
        grid=(S // tm,),
        in_specs=[row(D), pl.BlockSpec((1, D), lambda i: (0, 0)),
                  pl.BlockSpec((D_EXT, D), lambda i: (0, 0)), row(128), row(128)],
        out_specs=[row(512), row(256), row(256), row(1024)],
        out_shape=[jax.ShapeDtypeStruct((S, 512), CDT), jax.ShapeDtypeStruct((S, 256), CDT),
                   jax.ShapeDtypeStruct((S, 256), CDT), jax.ShapeDtypeStruct((S, 1024), F32)],
        compiler_params=_params(1),
    )(x, g, wextT, cs, sn)


def _stack_heads(p0, p1):
    lane = lax.broadcasted_iota(jnp.int32, p0.shape, 1)
    lo = lane < HEAD_DIM
    z = jnp.zeros_like(p0)
    return jnp.concatenate([jnp.where(lo, p0, z), jnp.where(lo, z, p0),
                            jnp.where(lo, p1, z), jnp.where(lo, z, p1)], axis=0)


def _unstack_heads(o):
    lane = lax.broadcasted_iota(jnp.int32, (BLOCK, 128), 1)
    lo = lane < HEAD_DIM
    return (jnp.where(lo, o[0:128], o[128:256]), jnp.where(lo, o[256:384], o[384:512]))


def _band_mask_qk(n):
    i = lax.broadcasted_iota(jnp.int32, (4 * BLOCK, 2 * BLOCK), 0) % BLOCK
    c = lax.broadcasted_iota(jnp.int32, (4 * BLOCK, 2 * BLOCK), 1)
    return (c > i) & (c <= i + BLOCK) & ((n > 0) | (c >= BLOCK))


def _band_mask_kq(n):
    c = lax.broadcasted_iota(jnp.int32, (2 * BLOCK, 4 * BLOCK), 0)
    i = lax.broadcasted_iota(jnp.int32, (2 * BLOCK, 4 * BLOCK), 1) % BLOCK
    return (c > i) & (c <= i + BLOCK) & ((n > 0) | (c >= BLOCK))


def attn_fwd(q, k, v, sink_col, *, nb=2, comm=None):
    S = q.shape[0]
    tq = nb * BLOCK

    def body(q_ref, k_ref, v_ref, sink_ref, o_ref):
        t = pl.program_id(0)
        for b in range(nb):
            n = t * nb + b
            prev = pl.multiple_of(jnp.maximum(n - 1, 0) * BLOCK, BLOCK)
            cur = pl.multiple_of(n * BLOCK, BLOCK)
            rows = slice(b * BLOCK, (b + 1) * BLOCK)
            mask = _band_mask_qk(n)
            for gidx in range(2):
                lanes = slice(128 * gidx, 128 * (gidx + 1))
                qs = _stack_heads(q_ref[rows, 256 * gidx:256 * gidx + 128],
                                  q_ref[rows, 256 * gidx + 128:256 * gidx + 256])
                kb = jnp.concatenate([k_ref[pl.ds(prev, BLOCK), lanes], k_ref[pl.ds(cur, BLOCK), lanes]], axis=0)
                vb = jnp.concatenate([v_ref[pl.ds(prev, BLOCK), lanes], v_ref[pl.ds(cur, BLOCK), lanes]], axis=0)
                s = lax.dot_general(qs, kb, NT, preferred_element_type=F32)
                s = jnp.where(mask, s, NEG)
                sink = sink_ref[gidx]
                m = jnp.maximum(jnp.max(s, axis=1, keepdims=True), sink)
                p = jnp.exp(s - m)
                den = jnp.sum(p, axis=1, keepdims=True) + jnp.exp(sink - m)
                o = jnp.dot(p.astype(CDT), vb, preferred_element_type=F32) / den
                o0, o1 = _unstack_heads(o)
                o_ref[rows, 256 * gidx:256 * gidx + 128] = o0.astype(CDT)
                o_ref[rows, 256 * gidx + 128:256 * gidx + 256] = o1.astype(CDT)

    return _hosted(comm)(
        body, name="attn_fwd",
        grid=(S // tq,),
        in_specs=[pl.BlockSpec((tq, 512), lambda t: (t, 0)),
                  pl.BlockSpec((S, 256), lambda t: (0, 0)),
                  pl.BlockSpec((S, 256), lambda t: (0, 0)),
                  pl.BlockSpec((2, 4 * BLOCK, 1), lambda t: (0, 0, 0))],
        out_specs=pl.BlockSpec((tq, 512), lambda t: (t, 0)),
        out_shape=jax.ShapeDtypeStruct((S, 512), CDT),
        compiler_params=_params(1),
    )(q, k, v, sink_col)


def attn_bwd(q, k, v, do, sink_row, *, nb=2, comm=None):
    S = q.shape[0]
    tq = nb * BLOCK
    scale = HEAD_DIM ** -0.5

    def body(q_ref, k_ref, v_ref, do_ref, sink_ref, dq_ref, dk_ref, dv_ref, dsink_ref):
        t = pl.program_id(0)

        @pl.when(t == 0)
        def _():
            dk_ref[...] = jnp.zeros_like(dk_ref)
            dv_ref[...] = jnp.zeros_like(dv_ref)
            dsink_ref[...] = jnp.zeros_like(dsink_ref)

        for b in range(nb):
            n = t * nb + b
            prev = pl.multiple_of(jnp.maximum(n - 1, 0) * BLOCK, BLOCK)
            cur = pl.multiple_of(n * BLOCK, BLOCK)
            rows = slice(b * BLOCK, (b + 1) * BLOCK)
            mask = _band_mask_kq(n)
            for gidx in range(2):
                lanes = slice(128 * gidx, 128 * (gidx + 1))
                qs = _stack_heads(q_ref[rows, 256 * gidx:256 * gidx + 128],
                                  q_ref[rows, 256 * gidx + 128:256 * gidx + 256])
                dos = _stack_heads(do_ref[rows, 256 * gidx:256 * gidx + 128],
                                   do_ref[rows, 256 * gidx + 128:256 * gidx + 256])
                kb = jnp.concatenate([k_ref[pl.ds(prev, BLOCK), lanes], k_ref[pl.ds(cur, BLOCK), lanes]], axis=0)
                vb = jnp.concatenate([v_ref[pl.ds(prev, BLOCK), lanes], v_ref[pl.ds(cur, BLOCK), lanes]], axis=0)
                st = lax.dot_general(kb, qs, NT, preferred_element_type=F32)
                st = jnp.where(mask, st, NEG)
                sink = sink_ref[gidx]
                m = jnp.maximum(jnp.max(st, axis=0, keepdims=True), sink)
                e = jnp.exp(st - m)
                es = jnp.exp(sink - m)
                inv = 1.0 / (jnp.sum(e, axis=0, keepdims=True) + es)
                pt = e * inv
                dpt = lax.dot_general(vb, dos, NT, preferred_element_type=F32)
                delta = jnp.sum(pt * dpt, axis=0, keepdims=True)
                dst = (pt * (dpt - delta)).astype(CDT)
                dsink_ref[gidx] += -(es * inv) * delta
                dvb = jnp.dot(pt.astype(CDT), dos, preferred_element_type=F32)
                dkb = jnp.dot(dst, qs, preferred_element_type=F32)
                dqs = lax.dot_general(dst, kb, TN, preferred_element_type=F32) * scale
                dq0, dq1 = _unstack_heads(dqs)
                dq_ref[rows, 256 * gidx:256 * gidx + 128] = dq0
                dq_ref[rows, 256 * gidx + 128:256 * gidx + 256] = dq1
                dk_ref[pl.ds(prev, BLOCK), lanes] += dkb[0:BLOCK]
                dk_ref[pl.ds(cur, BLOCK), lanes] += dkb[BLOCK:2 * BLOCK]
                dv_ref[pl.ds(prev, BLOCK), lanes] += dvb[0:BLOCK]
                dv_ref[pl.ds(cur, BLOCK), lanes] += dvb[BLOCK:2 * BLOCK]

    full = pl.BlockSpec((S, 256), lambda t: (0, 0))
    tile = pl.BlockSpec((tq, 512), lambda t: (t, 0))
    srow = pl.BlockSpec((2, 1, 4 * BLOCK), lambda t: (0, 0, 0))
    return _hosted(comm)(
        body, name="attn_bwd",
        grid=(S // tq,),
        in_specs=[tile, full, full, tile, srow],
        out_specs=[tile, full, full, srow],
        out_shape=[jax.ShapeDtypeStruct((S, 512), F32), jax.ShapeDtypeStruct((S, 256), F32),
                   jax.ShapeDtypeStruct((S, 256), F32), jax.ShapeDtypeStruct((2, 1, 4 * BLOCK), F32)],
        compiler_params=_params(1),
    )(q, k, v, do, sink_row)


def _glu(u):
    a = u[:, 0:CONV_C]
    gt = u[:, CONV_C:2 * CONV_C]
    sg = _sigmoid(gt)
    return a, sg, a * sg


CONV_CHUNK = 64


def conv_fwd(u, cw, cb, lg, lb, *, tm=512, comm=None):
    S = u.shape[0]
    nh = tm // HALO

    def body(u_ref, uh_ref, cw_ref, cb_ref, lg_ref, lb_ref, o_ref, y_ref, hbuf):
        t = pl.program_id(0)
        _, _, hg = _glu(u_ref[...])
        _, _, hh = _glu(uh_ref[...])
        hbuf[0:HALO, :] = jnp.where(t > 0, hh, jnp.zeros_like(hh))
        hbuf[HALO:HALO + tm, :] = hg
        off = HALO - (CONV_K - 1)
        for c0 in range(0, tm, CONV_CHUNK):
            acc = jnp.zeros((CONV_CHUNK, CONV_C), F32) + cb_ref[...]
            for j in range(CONV_K):
                acc = acc + cw_ref[j:j + 1, :] * hbuf[c0 + off + j:c0 + off + j + CONV_CHUNK, :]
            y_ref[c0:c0 + CONV_CHUNK, :] = acc
        y = y_ref[...]
        yc = y - jnp.mean(y, axis=-1, keepdims=True)
        r = lax.rsqrt(jnp.mean(yc * yc, axis=-1, keepdims=True) + EPS)
        z = yc * r * lg_ref[...] + lb_ref[...]
        o_ref[...] = (z * _sigmoid(z)).astype(CDT)

    vec = pl.BlockSpec((1, CONV_C), lambda t: (0, 0))
    return _hosted(comm)(
        body, name="conv_fwd",
        grid=(S // tm,),
        in_specs=[pl.BlockSpec((tm, 2 * CONV_C), lambda t: (t, 0)),
                  pl.BlockSpec((HALO, 2 * CONV_C), lambda t: (jnp.maximum(t * nh - 1, 0), 0)),
                  pl.BlockSpec((CONV_K, CONV_C), lambda t: (0, 0)), vec, vec, vec],
        out_specs=[pl.BlockSpec((tm, CONV_C), lambda t: (t, 0)), pl.BlockSpec((tm, CONV_C), lambda t: (t, 0))],
        out_shape=[jax.ShapeDtypeStruct((S, CONV_C), CDT), jax.ShapeDtypeStruct((S, CONV_C), F32)],
        scratch_shapes=[pltpu.VMEM((HALO + tm, CONV_C), F32)],
        compiler_params=_params(1),
    )(u, u, cw, cb, lg, lb)


def conv_bwd(dc, u, y, cw, lg, lb, *, tm=512, comm=None):
    S = u.shape[0]
    nh = tm // HALO
    nt = S // tm

    def ln_bwd(dcv, yv, lgv, lbv):
        yc = yv - jnp.mean(yv, axis=-1, keepdims=True)
        r = lax.rsqrt(jnp.mean(yc * yc, axis=-1, keepdims=True) + EPS)
        yhat = yc * r
        z = yhat * lgv + lbv
        sg = _sigmoid(z)
        dz = dcv * (sg * (1.0 + z * (1.0 - sg)))
        dyhat = dz * lgv
        dy = r * (dyhat - jnp.mean(dyhat, axis=-1, keepdims=True)
                  - yhat * jnp.mean(dyhat * yhat, axis=-1, keepdims=True))
        return dy, dz, yhat

    def body(dc_ref, dcn_ref, u_ref, uh_ref, y_ref, yn_ref, cw_ref, lg_ref, lb_ref,
             du_ref, dcw_ref, dcb_ref, dlg_ref, dlb_ref, hbuf, dybuf, dhg_sc, dw_sc):
        t = pl.program_id(0)

        @pl.when(t == 0)
        def _():
            dw_sc[...] = jnp.zeros_like(dw_sc)
            dcb_ref[...] = jnp.zeros_like(dcb_ref)
            dlg_ref[...] = jnp.zeros_like(dlg_ref)
            dlb_ref[...] = jnp.zeros_like(dlb_ref)

        lgv, lbv = lg_ref[...], lb_ref[...]
        dy, dz, yhat = ln_bwd(dc_ref[...].astype(F32), y_ref[...], lgv, lbv)
        dyn, _, _ = ln_bwd(dcn_ref[...].astype(F32), yn_ref[...], lgv, lbv)
        dlb_ref[...] += jnp.sum(dz, axis=0, keepdims=True)
        dlg_ref[...] += jnp.sum(dz * yhat, axis=0, keepdims=True)
        dcb_ref[...] += jnp.sum(dy, axis=0, keepdims=True)
        dybuf[0:tm, :] = dy
        dybuf[tm:tm + HALO, :] = jnp.where(t < nt - 1, dyn, jnp.zeros_like(dyn))

        a, sg, hg = _glu(u_ref[...])
        _, _, hh = _glu(uh_ref[...])
        hbuf[0:HALO, :] = jnp.where(t > 0, hh, jnp.zeros_like(hh))
        hbuf[HALO:HALO + tm, :] = hg

        off = HALO - (CONV_K - 1)
        for c0 in range(0, tm, CONV_CHUNK):
            acc = jnp.zeros((CONV_CHUNK, CONV_C), F32)
            dyc = dybuf[c0:c0 + CONV_CHUNK, :]
            for j in range(CONV_K):
                acc = acc + cw_ref[j:j + 1, :] * dybuf[c0 + (CONV_K - 1) - j:c0 + (CONV_K - 1) - j + CONV_CHUNK, :]
                prod = dyc * hbuf[c0 + off + j:c0 + off + j + CONV_CHUNK, :]
                dw_sc[j] += jnp.sum(prod.reshape(CONV_CHUNK // 8, 8, CONV_C), axis=0)
            dhg_sc[c0:c0 + CONV_CHUNK, :] = acc

        dhg = dhg_sc[...]
        du_ref[:, 0:CONV_C] = dhg * sg
        du_ref[:, CONV_C:2 * CONV_C] = dhg * a * sg * (1.0 - sg)

        @pl.when(t == nt - 1)
        def _():
            dcw_ref[...] = jnp.sum(dw_sc[...], axis=1)

    vec = pl.BlockSpec((1, CONV_C), lambda t: (0, 0))
    tile = pl.BlockSpec((tm, CONV_C), lambda t: (t, 0))
    nxt = pl.BlockSpec((HALO, CONV_C), lambda t: (jnp.minimum((t + 1) * nh, S // HALO - 1), 0))
    return _hosted(comm)(
        body, name="conv_bwd",
        grid=(nt,),
        in_specs=[tile, nxt,
                  pl.BlockSpec((tm, 2 * CONV_C), lambda t: (t, 0)),
                  pl.BlockSpec((HALO, 2 * CONV_C), lambda t: (jnp.maximum(t * nh - 1, 0), 0)),
                  tile, nxt,
                  pl.BlockSpec((CONV_K, CONV_C), lambda t: (0, 0)), vec, vec],
        out_specs=[pl.BlockSpec((tm, 2 * CONV_C), lambda t: (t, 0)),
                   pl.BlockSpec((CONV_K, CONV_C), lambda t: (0, 0)), vec, vec, vec],
        out_shape=[jax.ShapeDtypeStruct((S, 2 * CONV_C), F32), jax.ShapeDtypeStruct((CONV_K, CONV_C), F32),
                   jax.ShapeDtypeStruct((1, CONV_C), F32), jax.ShapeDtypeStruct((1, CONV_C), F32),
                   jax.ShapeDtypeStruct((1, CONV_C), F32)],
        scratch_shapes=[pltpu.VMEM((HALO + tm, CONV_C), F32), pltpu.VMEM((tm + HALO, CONV_C), F32),
                        pltpu.VMEM((tm, CONV_C), F32), pltpu.VMEM((CONV_K, 8, CONV_C), F32)],
        compiler_params=_params(1),
    )(dc, dc, u, u, y, y, cw, lg, lb)


def outproj_fwd(x, ao, co, wout, *, tm=512):
    S, D = x.shape

    def body(x_ref, a_ref, c_ref, w_ref, o_ref):
        o_ref[...] = (x_ref[...]
                      + jnp.dot(a_ref[...], w_ref[0:ATT_W, :], preferred_element_type=F32)
                      + jnp.dot(c_ref[...], w_ref[ATT_W:ATT_W + CONV_C, :], preferred_element_type=F32))

    return pl.pallas_call(
        body, name="outproj_fwd",
        grid=(S // tm,),
        in_specs=[pl.BlockSpec((tm, D), lambda i: (i, 0)), pl.BlockSpec((tm, ATT_W), lambda i: (i, 0)),
                  pl.BlockSpec((tm, CONV_C), lambda i: (i, 0)), pl.BlockSpec((D, D), lambda i: (0, 0))],
        out_specs=pl.BlockSpec((tm, D), lambda i: (i, 0)),
        out_shape=jax.ShapeDtypeStruct((S, D), F32),
        compiler_params=_params(1),
    )(x, ao, co, wout)


def outproj_bwd(dx, ao, co, wout, *, tm=512, comm=None):
    S, D = dx.shape

    def body(dx_ref, a_ref, c_ref, w_ref, da_ref, dc_ref, dw_ref):
        @pl.when(pl.program_id(0) == 0)
        def _():
            dw_ref[...] = jnp.zeros_like(dw_ref)

        dxb = dx_ref[...].astype(CDT)
        da_ref[...] = lax.dot_general(dxb, w_ref[0:ATT_W, :], NT, preferred_element_type=F32).astype(CDT)
        dc_ref[...] = lax.dot_general(dxb, w_ref[ATT_W:ATT_W + CONV_C, :], NT, preferred_element_type=F32)
        dw_ref[0:ATT_W, :] += lax.dot_general(a_ref[...], dxb, TN, preferred_element_type=F32)
        dw_ref[ATT_W:ATT_W + CONV_C, :] += lax.dot_general(c_ref[...], dxb, TN, preferred_element_type=F32)

    return _hosted(comm)(
        body, name="outproj_bwd",
        grid=(S // tm,),
        in_specs=[pl.BlockSpec((tm, D), lambda i: (i, 0)), pl.BlockSpec((tm, ATT_W), lambda i: (i, 0)),
                  pl.BlockSpec((tm, CONV_C), lambda i: (i, 0)), pl.BlockSpec((D, D), lambda i: (0, 0))],
        out_specs=[pl.BlockSpec((tm, ATT_W), lambda i: (i, 0)), pl.BlockSpec((tm, CONV_C), lambda i: (i, 0)),
                   pl.BlockSpec((D, D), lambda i: (0, 0))],
        out_shape=[jax.ShapeDtypeStruct((S, ATT_W), CDT), jax.ShapeDtypeStruct((S, CONV_C), F32),
                   jax.ShapeDtypeStruct((D, D), F32)],
        compiler_params=_params(1),
    )(dx, ao, co, wout)


def inproj_bwd(x, g, dres, dq, dk, dv, du, wextT, cs, sn, *, tm=256):
    S, D = x.shape

    def body(x_ref, g_ref, dres_ref, dq_ref, dk_ref, dv_ref, du_ref, w_ref, cs_ref, sn_ref,
             dx_ref, dw_ref, dg_ref, dp_sc):
        @pl.when(pl.program_id(0) == 0)
        def _():
            dw_ref[...] = jnp.zeros_like(dw_ref)
            dg_ref[...] = jnp.zeros_like(dg_ref)

        c, s = cs_ref[...], sn_ref[...]
        for b in range(4):
            dp_sc[:, 128 * b:128 * (b + 1)] = _rope_transpose(dq_ref[:, 128 * b:128 * (b + 1)], c, s).astype(CDT)
        for b in range(2):
            dp_sc[:, 512 + 128 * b:512 + 128 * (b + 1)] = _rope_transpose(
                dk_ref[:, 128 * b:128 * (b + 1)], c, s).astype(CDT)
        dp_sc[:, 768:1024] = dv_ref[...].astype(CDT)
        dp_sc[:, 1024:2048] = du_ref[...].astype(CDT)
        dp = dp_sc[...]
        xhat, r = _rms_parts(x_ref[...])
        gg = g_ref[...]
        h = (xhat * gg).astype(CDT)
        dh = jnp.dot(dp, w_ref[...], preferred_element_type=F32)
        dw_ref[...] += lax.dot_general(dp, h, TN, preferred_element_type=F32)
        dx, dg = _rms_bwd(dh, xhat, r, gg)
        dx_ref[...] = dres_ref[...] + dx
        dg_ref[...] += dg

    def row(w):
        return pl.BlockSpec((tm, w), lambda i: (i, 0))

    return pl.pallas_call(
        body, name="inproj_bwd",
        grid=(S // tm,),
        in_specs=[row(D), pl.BlockSpec((1, D), lambda i: (0, 0)), row(D), row(512), row(256), row(256),
                  row(1024), pl.BlockSpec((D_EXT, D), lambda i: (0, 0)), row(128), row(128)],
        out_specs=[row(D), pl.BlockSpec((D_EXT, D), lambda i: (0, 0)), pl.BlockSpec((1, D), lambda i: (0, 0))],
        out_shape=[jax.ShapeDtypeStruct((S, D), F32), jax.ShapeDtypeStruct((D_EXT, D), F32),
                   jax.ShapeDtypeStruct((1, D), F32)],
        scratch_shapes=[pltpu.VMEM((tm, D_EXT), CDT)],
        compiler_params=_params(1),
    )(x, g, dres, dq, dk, dv, du, wextT, cs, sn)


def _rope_tables(positions):
    inv_freq = 1.0 / (10000.0 ** (jnp.arange(0, HEAD_DIM, 2, dtype=F32) / HEAD_DIM))
    ang = positions.astype(F32).reshape(-1, 1) * inv_freq
    cos, sin = jnp.cos(ang), jnp.sin(ang)
    cs = jnp.tile(jnp.concatenate([cos, cos], axis=-1), (1, 2))
    sn = jnp.tile(jnp.concatenate([-sin, sin], axis=-1), (1, 2))
    return cs, sn


def _widen_w_in(w):
    q, u = w[0:512], w[768:1792]
    parts = [q]
    for base in (512, 576, 640, 704):
        parts += [w[base:base + 64], w[base:base + 64]]
    return jnp.concatenate(parts + [u], axis=0)


def _fold_w_in(d):
    parts = [d[0:512]]
    for base in (512, 640, 768, 896):
        parts.append(d[base:base + 64] + d[base + 64:base + 128])
    return jnp.concatenate(parts + [d[1024:2048]], axis=0)


def add_half(g5, r1, c_idx):
    _, _, r, D = g5.shape

    def body(c_ref, g_ref, r_ref, o_ref):
        o_ref[...] = (g_ref[...] + r_ref[...]).astype(CDT)

    return pl.pallas_call(
        body, name="add_half",
        grid_spec=pltpu.PrefetchScalarGridSpec(
            num_scalar_prefetch=1, grid=(N_CHIPS,),
            in_specs=[pl.BlockSpec((None, None, r, D), lambda s, cr: (s, cr[0], 0, 0)),
                      pl.BlockSpec((None, r, D), lambda s, cr: (s, 0, 0))],
            out_specs=pl.BlockSpec((None, r, D), lambda s, cr: (s, 0, 0))),
        out_shape=jax.ShapeDtypeStruct((N_CHIPS, r, D), CDT),
        compiler_params=_params(1),
    )(c_idx, g5, r1)


def sum_partials(part, recv3, j_idx):
    _, r, D = part.shape
    tr_ = r // 2

    def body(j_ref, p_ref, r_ref, o_ref):
        o_ref[...] = ((p_ref[...].astype(F32) + r_ref[0].astype(F32)) + r_ref[1].astype(F32)) + r_ref[2].astype(F32)

    return pl.pallas_call(
        body, name="sum_partials",
        grid_spec=pltpu.PrefetchScalarGridSpec(
            num_scalar_prefetch=1, grid=(2,),
            in_specs=[pl.BlockSpec((None, tr_, D), lambda i, jr: (jr[0], i, 0)),
                      pl.BlockSpec((3, tr_, D), lambda i, jr: (0, i, 0))],
            out_specs=pl.BlockSpec((tr_, D), lambda i, jr: (i, 0))),
        out_shape=jax.ShapeDtypeStruct((r, D), F32),
        compiler_params=_params(1),
    )(j_idx, part, recv3)


class _Chain:
    def __init__(self, grads, c_idx, j_idx):
        self.c_idx, self.j_idx = c_idx, j_idx
        self.g5 = [g.reshape(N_CHIPS, 2, g.shape[0] // (2 * N_CHIPS), g.shape[1]) for g in grads]

    def swap(self):
        return swap_op(self.g5)

    def after_swap(self, recv):
        self.parts = [add_half(g, r, self.c_idx) for g, r in zip(self.g5, recv)]

    def xchg(self):
        return exchange_op(self.parts)

    def after_xchg(self, recv):
        self.totals = [sum_partials(p, r, self.j_idx) for p, r in zip(self.parts, recv)]

    def share(self):
        return share_op(self.totals)

    def after_share(self, recv):
        self.final = [h.reshape(2 * h.shape[1], h.shape[2]) for h in recv]


def all_reduce_small(vec):
    R = vec.shape[0]

    def body(v_ref, o_ref, buf, send, recv):
        x, y, c = _coords()
        me = 4 * x + 2 * y + c
        buf[me] = v_ref[...]
        cps = []
        for m in range(1, N_DEV):
            dx, dy, dc = (m >> 2) & 1, (m >> 1) & 1, m & 1
            cp = pltpu.make_async_remote_copy(v_ref, buf.at[me], send.at[m - 1], recv.at[m - 1],
                                              device_id=((x + dx) % 2, (y + dy) % 2, (c + dc) % 2),
                                              device_id_type=MESH)
            cp.start()
            cps.append(cp)
        for cp in cps:
            cp.wait()
        acc = buf[0]
        for d in range(1, N_DEV):
            acc = acc + buf[d]
        o_ref[...] = acc

    return pl.pallas_call(
        body, name="all_reduce_small",
        in_specs=[pl.BlockSpec(memory_space=pltpu.VMEM)], out_specs=pl.BlockSpec(memory_space=pltpu.VMEM),
        out_shape=jax.ShapeDtypeStruct(vec.shape, F32),
        scratch_shapes=[pltpu.VMEM((N_DEV, R, 128), F32), pltpu.SemaphoreType.DMA((N_DEV - 1,)),
                        pltpu.SemaphoreType.DMA((N_DEV - 1,))],
    )(vec)


def adamw(w, g, m, v, *, tm=512):
    R, C = w.shape
    tm = max(t for t in range(8, min(tm, R) + 1, 8) if R % t == 0)
    c1 =1.0 - ADAM_B1 ** ADAM_STEP
    c2 = 1.0 - ADAM_B2 ** ADAM_STEP

    def body(w_ref, g_ref, m_ref, v_ref, d_ref, nm_ref, nv_ref):
        gg = g_ref[...]
        nm = ADAM_B1 * m_ref[...] + (1.0 - ADAM_B1) * gg
        nv = ADAM_B2 * v_ref[...] + (1.0 - ADAM_B2) * (gg * gg)
        nm_ref[...] = nm
        nv_ref[...] = nv
        d_ref[...] = -ADAM_LR * ((nm / c1) / (jnp.sqrt(nv / c2) + ADAM_EPS) + ADAM_WD * w_ref[...])

    blk = pl.BlockSpec((tm, C), lambda i: (i, 0))
    return pl.pallas_call(
        body, name="adamw",
        grid=(pl.cdiv(R, tm),),
        in_specs=[blk] * 4, out_specs=[blk] * 3,
        out_shape=[jax.ShapeDtypeStruct((R, C), F32)] * 3,
        compiler_params=_params(1),
    )(w, g, m, v)


_SMALL = (("n1", (2, D_MODEL)), ("nm", (2, D_MODEL)), ("n2", (2, D_MODEL)), ("nf", (D_MODEL,)),
          ("cb", (2, CONV_C)), ("lg", (2, CONV_C)), ("lb", (2, CONV_C)), ("sinks", (2, N_HEADS)),
          ("cw", (2, CONV_K, CONV_C)))


def _pack(parts, rows):
    flat = jnp.concatenate([p.reshape(-1).astype(F32) for p in parts])
    return jnp.pad(flat, (0, rows * 128 - flat.shape[0])).reshape(rows, 128)


def _unpack(block, shapes):
    flat = block.reshape(-1)
    out, o = [], 0
    for shp in shapes:
        n = 1
        for s in shp:
            n *= s
        out.append(flat[o:o + n].reshape(shp))
        o += n
    return out


def kernel(x, positions, ffn1_norm, ffn1_w_gate, ffn1_w_up, ffn1_w_down, mix_norm, w_in, conv_w, conv_b, conv_ln_g, conv_ln_b, attn_sinks, w_out, ffn2_norm, ffn2_w_gate, ffn2_w_up, ffn2_w_down, final_norm, loss_target, m_ffn1_norm, m_ffn1_w_gate, m_ffn1_w_up, m_ffn1_w_down, m_mix_norm, m_w_in, m_conv_w, m_conv_b, m_conv_ln_g, m_conv_ln_b, m_attn_sinks, m_w_out, m_ffn2_norm, m_ffn2_w_gate, m_ffn2_w_up, m_ffn2_w_down, m_final_norm, v_ffn1_norm, v_ffn1_w_gate, v_ffn1_w_up, v_ffn1_w_down, v_mix_norm, v_w_in, v_conv_w, v_conv_b, v_conv_ln_g, v_conv_ln_b, v_attn_sinks, v_w_out, v_ffn2_norm, v_ffn2_w_gate, v_ffn2_w_up, v_ffn2_w_down, v_final_norm):
    cx, cy, cc = _coords()
    chip = 2 * cx + cy
    c_idx = jnp.reshape(cc, (1,)).astype(jnp.int32)
    j_idx = jnp.reshape(chip, (1,)).astype(jnp.int32)
    L = ffn1_norm.shape[0]
    tr = lambda a: jnp.swapaxes(a, 1, 2)

    sh = dict(f1g=tr(ffn1_w_gate), f1u=tr(ffn1_w_up), f1d=ffn1_w_down, f2g=tr(ffn2_w_gate),
              f2u=tr(ffn2_w_up), f2d=ffn2_w_down, win=tr(w_in), wout=w_out)
    sh = {k: [v[l].astype(CDT) for l in range(L)] for k, v in sh.items()}
    W = {}

    def gather_op(keys):
        return ag_op([conv_w if k == "cw" else sh[k[0]][k[1]] for k in keys])

    def take(keys, res):
        for k, a in zip(keys, res):
            W[k] = a if k == "cw" else a.reshape((N_CHIPS * a.shape[1],) + a.shape[2:])

    def with_ag(fn, keys, *args):
        if not keys:
            return fn(*args)
        main, res = fn(*args, comm=gather_op(keys))
        take(keys, res)
        return main

    ag_hosts = {("ffn1", 0): [("win", 0), "cw", ("f2g", 0), ("f1d", 1)],
                ("inproj", 0): [("wout", 0)], ("attn", 0): [("f2u", 0)], ("conv", 0): [("f2d", 0)],
                ("ffn2", 0): [("f1g", 1), ("f1u", 1)],
                ("ffn1", 1): [("win", 1), ("f2g", 1), ("f2u", 1)],
                ("inproj", 1): [("wout", 1)], ("attn", 1): [("f2d", 1)]}
    first = [("f1g", 0), ("f1u", 0), ("f1d", 0)]
    take(first, _run_comm(gather_op(first)))

    cs, sn = _rope_tables(positions)
    saved = []
    h = x[0]
    for l in range(L):
        sink = attn_sinks[l].reshape(2, 4)
        sink_col = jnp.repeat(sink, BLOCK, axis=1).reshape(2, 4 * BLOCK, 1)
        x0 = h
        x1, g1, u1 = with_ag(ffn_fwd, ag_hosts.get(("ffn1", l)), x0, ffn1_norm[l][None],
                             W[("f1g", l)], W[("f1u", l)], W[("f1d", l)])
        wext = _widen_w_in(W[("win", l)])
        q, k, v, u = with_ag(inproj_fwd, ag_hosts.get(("inproj", l)), x1, mix_norm[l][None], wext, cs, sn)
        ao = with_ag(attn_fwd, ag_hosts.get(("attn", l)), q, k, v, sink_col)
        cwl = jnp.transpose(W["cw"][:, l], (1, 0, 2)).reshape(CONV_K, CONV_C)
        co, yc = with_ag(conv_fwd, ag_hosts.get(("conv", l)), u, cwl, conv_b[l][None], conv_ln_g[l][None],
                         conv_ln_b[l][None])
        x2 = outproj_fwd(x1, ao, co, W[("wout", l)])
        x3, g2, u2 = with_ag(ffn_fwd, ag_hosts.get(("ffn2", l)), x2, ffn2_norm[l][None],
                             W[("f2g", l)], W[("f2u", l)], W[("f2d", l)])
        saved.append((x0, x1, x2, g1, u1, g2, u2, q, k, v, u, ao, co, yc, sink, wext, cwl))
        h = x3

    loss, dx, dnf = loss_head(h, final_norm[None], loss_target[0])

    def hosted(fn, stages, *args):
        stages = [s for s in stages if s is not None]
        if not stages:
            return fn(*args)
        ops = [op for op, _ in stages]
        main, res = fn(*args, comm=_merge(*ops))
        for (_, cb), r in zip(stages, _split(res, *ops)):
            cb(r)
        return main

    def st(chain, stage):
        if chain is None:
            return None
        return {"swap": (chain.swap, chain.after_swap), "xchg": (chain.xchg, chain.after_xchg),
                "share": (chain.share, chain.after_share)}[stage]

    def stage(chain, name):
        s = st(chain, name)
        return None if s is None else (s[0](), s[1])

    small = {k_: [None] * L for k_ in ("n1", "nm", "n2", "cw", "cb", "lg", "lb", "sinks")}
    chains = {}
    up_mx = up_f1 = None
    for l in reversed(range(L)):
        x0, x1, x2, g1, u1, g2, u2, q, k, v, u, ao, co, yc, sink, wext, cwl = saved[l]
        sink_row = jnp.repeat(sink, BLOCK, axis=1).reshape(2, 1, 4 * BLOCK)
        dx2, dgt, dup, hh, dacc, small["n2"][l] = hosted(
            ffn_bwd_dgrad, [stage(up_mx, "share"), stage(up_f1, "swap")],
            x2, ffn2_norm[l][None], dx, g2, u2, W[("f2g", l)], W[("f2u", l)], W[("f2d", l)])
        gf2 = hosted(ffn_bwd_wgrad, [stage(up_f1, "xchg")], hh, dacc, g2, u2, dgt, dup)
        f2 = chains[("f2", l)] = _Chain(gf2, c_idx, j_idx)
        da, dc, gwout = hosted(outproj_bwd, [stage(up_f1, "share"), stage(f2, "swap")],
                               dx2, ao, co, W[("wout", l)])
        du, small["cw"][l], small["cb"][l], small["lg"][l], small["lb"][l] = hosted(
            conv_bwd, [stage(f2, "xchg")], dc, u, yc, cwl, conv_ln_g[l][None], conv_ln_b[l][None])
        dq, dk, dv, dsink = hosted(attn_bwd, [stage(f2, "share")], q, k, v, da, sink_row)
        small["sinks"][l] = jnp.sum(dsink.reshape(2, 4, BLOCK), axis=-1).reshape(N_HEADS)
        dx1, gwext, small["nm"][l] = inproj_bwd(x1, mix_norm[l][None], dx2, dq, dk, dv, du, wext, cs, sn)
        mx = chains[("mx", l)] = _Chain([gwout, _fold_w_in(gwext)], c_idx, j_idx)
        dx, dgt, dup, hh, dacc, small["n1"][l] = hosted(
            ffn_bwd_dgrad, [stage(mx, "swap")],
            x0, ffn1_norm[l][None], dx1, g1, u1, W[("f1g", l)], W[("f1u", l)], W[("f1d", l)])
        gf1 = hosted(ffn_bwd_wgrad, [stage(mx, "xchg")], hh, dacc, g1, u1, dgt, dup)
        f1 = chains[("f1", l)] = _Chain(gf1, c_idx, j_idx)
        up_mx, up_f1 = mx, f1

    up_f1.after_swap(_run_comm(up_f1.swap()))
    up_f1.after_xchg(_run_comm(up_f1.xchg()))
    ops = [up_mx.share(), up_f1.share()]
    res = _split(_run_comm(_merge(*ops)), *ops)
    up_mx.after_share(res[0])
    up_f1.after_share(res[1])

    G = {k_: jnp.stack(v_) for k_, v_ in small.items()}
    G["nf"] = dnf
    small_shapes = [shp for _, shp in _SMALL]
    n_small = 1 + sum(math.prod(s) for s in small_shapes)
    rows = -(-n_small // 1024) * 8
    packed = _pack([loss] + [G[k_] for k_, _ in _SMALL], rows)
    summed = all_reduce_small(packed)
    loss_out, *small_sum = _unpack(summed, [()] + small_shapes)
    gs = dict(zip([k_ for k_, _ in _SMALL], small_sum))
    gs["cw"] = lax.dynamic_slice_in_dim(gs["cw"], chip * (CONV_C // N_CHIPS), CONV_C // N_CHIPS, axis=2)

    def big(group, idx, transpose):
        per_layer = [chains[(group, l)].final[idx] for l in range(L)]
        return jnp.stack([a.T if transpose else a for a in per_layer])

    grads = dict(ffn1_norm=gs["n1"], ffn1_w_gate=big("f1", 0, True), ffn1_w_up=big("f1", 1, True),
                 ffn1_w_down=big("f1", 2, False), mix_norm=gs["nm"], w_in=big("mx", 1, True), conv_w=gs["cw"],
                 conv_b=gs["cb"], conv_ln_g=gs["lg"], conv_ln_b=gs["lb"], attn_sinks=gs["sinks"],
                 w_out=big("mx", 0, False), ffn2_norm=gs["n2"], ffn2_w_gate=big("f2", 0, True),
                 ffn2_w_up=big("f2", 1, True), ffn2_w_down=big("f2", 2, False), final_norm=gs["nf"])

    weights = dict(ffn1_norm=ffn1_norm, ffn1_w_gate=ffn1_w_gate, ffn1_w_up=ffn1_w_up, ffn1_w_down=ffn1_w_down,
                   mix_norm=mix_norm, w_in=w_in, conv_w=conv_w, conv_b=conv_b, conv_ln_g=conv_ln_g,
                   conv_ln_b=conv_ln_b, attn_sinks=attn_sinks, w_out=w_out, ffn2_norm=ffn2_norm,
                   ffn2_w_gate=ffn2_w_gate, ffn2_w_up=ffn2_w_up, ffn2_w_down=ffn2_w_down, final_norm=final_norm)
    moms = dict(ffn1_norm=(m_ffn1_norm, v_ffn1_norm), ffn1_w_gate=(m_ffn1_w_gate, v_ffn1_w_gate),
                ffn1_w_up=(m_ffn1_w_up, v_ffn1_w_up), ffn1_w_down=(m_ffn1_w_down, v_ffn1_w_down),
                mix_norm=(m_mix_norm, v_mix_norm), w_in=(m_w_in, v_w_in), conv_w=(m_conv_w, v_conv_w),
                conv_b=(m_conv_b, v_conv_b), conv_ln_g=(m_conv_ln_g, v_conv_ln_g),
                conv_ln_b=(m_conv_ln_b, v_conv_ln_b), attn_sinks=(m_attn_sinks, v_attn_sinks),
                w_out=(m_w_out, v_w_out), ffn2_norm=(m_ffn2_norm, v_ffn2_norm),
                ffn2_w_gate=(m_ffn2_w_gate, v_ffn2_w_gate), ffn2_w_up=(m_ffn2_w_up, v_ffn2_w_up),
                ffn2_w_down=(m_ffn2_w_down, v_ffn2_w_down), final_norm=(m_final_norm, v_final_norm))
    names = list(weights)
    big_names = ("ffn1_w_gate", "ffn1_w_up", "ffn1_w_down", "w_in", "w_out", "ffn2_w_gate", "ffn2_w_up",
                 "ffn2_w_down")
    delta, new_m, new_v = {}, {}, {}
    for nme in big_names:
        shp = weights[nme].shape
        two = lambda a: a.reshape(shp[0] * shp[1], shp[2])
        d, nm_, nv_ = adamw(two(weights[nme]), two(grads[nme]), two(moms[nme][0]), two(moms[nme][1]))
        delta[nme], new_m[nme], new_v[nme] = d.reshape(shp), nm_.reshape(shp), nv_.reshape(shp)
    small_names = [nme for nme in names if nme not in big_names]
    s_shapes = [weights[nme].shape for nme in small_names]
    n_tot = sum(math.prod(s) for s in s_shapes)
    srows = -(-n_tot // 1024) * 8
    d, nm_, nv_ = adamw(_pack([weights[nme] for nme in small_names], srows),
                        _pack([grads[nme] for nme in small_names], srows),
                        _pack([moms[nme][0] for nme in small_names], srows),
                        _pack([moms[nme][1] for nme in small_names], srows))
    for nme, dd, mm, vv in zip(small_names, _unpack(d, s_shapes), _unpack(nm_, s_shapes), _unpack(nv_, s_shapes)):
        delta[nme], new_m[nme], new_v[nme] = dd, mm, vv

    return (loss_out, dx[None], *[grads[nme] for nme in names], *[delta[nme] for nme in names],
            *[new_m[nme] for nme in names], *[new_v[nme] for nme in names])
```

```python
import math

import jax
import jax.numpy as jnp
from jax import lax
from jax.experimental import pallas as pl
from jax.experimental.pallas import tpu as pltpu

F32 = jnp.float32
CDT = jnp.bfloat16
D_MODEL = 1024
D_FF = 2816
N_HEADS = 8
HEAD_DIM = 64
BLOCK = 128
CONV_K = 31
CONV_C = 512
ATT_W = 512
D_EXT = 2048
EPS = 1e-5
HALO = 32
NEG = float(jnp.finfo(jnp.float32).min)
VMEM_LIMIT = 56 * 1024 * 1024

ADAM_LR = 0.001
ADAM_B1 = 0.9
ADAM_B2 = 0.999
ADAM_EPS = 1e-08
ADAM_WD = 0.01
ADAM_STEP = 10

NT = (((1,), (1,)), ((), ()))
TN = (((0,), (0,)), ((), ()))


MESH = pl.DeviceIdType.MESH
ANY = pl.BlockSpec(memory_space=pl.ANY)
N_CHIPS = 4
N_DEV = 8


def _params(n_axes):
    return pltpu.CompilerParams(dimension_semantics=("arbitrary",) * n_axes, vmem_limit_bytes=VMEM_LIMIT)


class _Comm:
    def __init__(self, name, inputs, out_shape, sems, descs):
        self.name, self.inputs, self.out_shape, self.sems, self.descs = name, list(inputs), list(out_shape), list(sems), descs


def _merge(*ops):
    ops = [o for o in ops if o is not None]
    if len(ops) == 1:
        return ops[0]

    def descs(cins, couts, sems):
        out, i, o, s = [], 0, 0, 0
        for op in ops:
            ni, no, ns = len(op.inputs), len(op.out_shape), len(op.sems)
            out += op.descs(cins[i:i + ni], couts[o:o + no], sems[s:s + ns])
            i, o, s = i + ni, o + no, s + ns
        return out

    return _Comm("_".join(o.name for o in ops), sum((o.inputs for o in ops), []),
                 sum((o.out_shape for o in ops), []), sum((o.sems for o in ops), []), descs)


def _split(couts, *ops):
    res, o = [], 0
    for op in ops:
        res.append(couts[o:o + len(op.out_shape)])
        o += len(op.out_shape)
    return res


def _hosted(comm):
    if comm is None:
        return pl.pallas_call

    def make(body, *, name, grid, in_specs, out_specs, out_shape, compiler_params, scratch_shapes=()):
        single = not isinstance(out_shape, (list, tuple))
        o_specs = [out_specs] if single else list(out_specs)
        o_shape = [out_shape] if single else list(out_shape)
        n_in, n_out, n_sc = len(in_specs), len(o_specs), len(scratch_shapes)
        c_in, c_out = len(comm.inputs), len(comm.out_shape)

        def hosted(*refs):
            ins, cins = refs[:n_in], refs[n_in:n_in + c_in]
            o0 = n_in + c_in
            outs, couts = refs[o0:o0 + n_out], refs[o0 + n_out:o0 + n_out + c_out]
            s0 = o0 + n_out + c_out
            scr, sems = refs[s0:s0 + n_sc], refs[s0 + n_sc:]
            first = pl.program_id(0) == 0
            last = pl.program_id(0) == grid[0] - 1
            for ax in range(1, len(grid)):
                first = first & (pl.program_id(ax) == 0)
                last = last & (pl.program_id(ax) == grid[ax] - 1)

            @pl.when(first)
            def _():
                for d in comm.descs(cins, couts, sems):
                    d.start()

            body(*ins, *outs, *scr)

            @pl.when(last)
            def _():
                for d in comm.descs(cins, couts, sems):
                    d.wait()

        call = pl.pallas_call(
            hosted, name=f"{name}_{comm.name}", grid=grid,
            in_specs=list(in_specs) + [ANY] * c_in, out_specs=o_specs + [ANY] * c_out,
            out_shape=o_shape + comm.out_shape, scratch_shapes=list(scratch_shapes) + comm.sems,
            compiler_params=compiler_params)

        def run(*args):
            res = call(*args, *comm.inputs)
            return (res[0] if single else list(res[:n_out])), list(res[n_out:])

        return run

    return make


def _run_comm(comm):
    c_in = len(comm.inputs)

    def body(*refs):
        cins, couts, sems = refs[:c_in], refs[c_in:c_in + len(comm.out_shape)], refs[c_in + len(comm.out_shape):]
        ds = comm.descs(cins, couts, sems)
        for d in ds:
            d.start()
        for d in ds:
            d.wait()

    return list(pl.pallas_call(
        body, name=comm.name, in_specs=[ANY] * c_in, out_specs=[ANY] * len(comm.out_shape),
        out_shape=comm.out_shape, scratch_shapes=comm.sems)(*comm.inputs))


def _coords():
    return lax.axis_index("x"), lax.axis_index("y"), lax.axis_index("c")


def _other_chips(x, y):
    return [(1 - x, y), (x, 1 - y), (1 - x, 1 - y)]


def ag_op(shards):
    n = len(shards)

    def descs(cins, couts, sems):
        send, recv = sems
        x, y, c = _coords()
        j = 2 * x + y
        ds = []
        for a in range(n):
            for p, (px, py) in enumerate(_other_chips(x, y)):
                ds.append(pltpu.make_async_remote_copy(cins[a], couts[a].at[j], send.at[a, p], recv.at[a, p],
                                                       device_id=(px, py, c), device_id_type=MESH))
        return ds

    return _Comm("ag", shards, [jax.ShapeDtypeStruct((N_CHIPS,) + s.shape, s.dtype) for s in shards],
                 [pltpu.SemaphoreType.DMA((n, 3)), pltpu.SemaphoreType.DMA((n, 3))], descs)


def _own_slab(gathered, mine, idx):
    return lax.dynamic_update_slice_in_dim(gathered, mine[None], idx, axis=0)


def first_gather(shards):
    n = len(shards)
    halves = [s.reshape(2, s.shape[0] // 2, s.shape[1]) for s in shards]

    def body(*refs):
        ins, outs = refs[:n], refs[n:2 * n]
        send1, recv1, send2, recv2 = refs[2 * n:]
        x, y, c = _coords()
        j = 2 * x + y
        chips = _other_chips(x, y)
        ici = [pltpu.make_async_remote_copy(ins[a].at[c], outs[a].at[j, c], send1.at[a, p], recv1.at[a, p],
                                            device_id=(px, py, c), device_id_type=MESH)
               for a in range(n) for p, (px, py) in enumerate(chips)]
        for d in ici:
            d.start()
        for d in ici:
            d.wait()
        d2d = [pltpu.make_async_remote_copy(outs[a].at[2 * px + py, c], outs[a].at[2 * px + py, c],
                                            send2.at[a, p], recv2.at[a, p],
                                            device_id=(x, y, 1 - c), device_id_type=MESH)
               for a in range(n) for p, (px, py) in enumerate(chips)]
        for d in d2d:
            d.start()
        for d in d2d:
            d.wait()

    return list(pl.pallas_call(
        body, name="first_gather", in_specs=[ANY] * n, out_specs=[ANY] * n,
        out_shape=[jax.ShapeDtypeStruct((N_CHIPS,) + h.shape, h.dtype) for h in halves],
        scratch_shapes=[pltpu.SemaphoreType.DMA((n, 3))] * 4)(*halves))


def swap_op(grads):
    n = len(grads)

    def descs(cins, couts, sems):
        send, recv = sems
        x, y, c = _coords()
        return [pltpu.make_async_remote_copy(cins[a].at[:, 1 - c], couts[a], send.at[a], recv.at[a],
                                             device_id=(x, y, 1 - c), device_id_type=MESH) for a in range(n)]

    return _Comm("swap", grads, [jax.ShapeDtypeStruct(g.shape[:1] + g.shape[2:], g.dtype) for g in grads],
                 [pltpu.SemaphoreType.DMA((n,)), pltpu.SemaphoreType.DMA((n,))], descs)


def exchange_op(parts):
    n = len(parts)

    def descs(cins, couts, sems):
        send, recv = sems
        x, y, c = _coords()
        ds = []
        for a in range(n):
            for p, (px, py) in enumerate(_other_chips(x, y)):
                ds.append(pltpu.make_async_remote_copy(cins[a].at[2 * px + py], couts[a].at[p], send.at[a, p],
                                                       recv.at[a, p], device_id=(px, py, c), device_id_type=MESH))
        return ds

    return _Comm("xchg", parts, [jax.ShapeDtypeStruct((3,) + p.shape[1:], p.dtype) for p in parts],
                 [pltpu.SemaphoreType.DMA((n, 3)), pltpu.SemaphoreType.DMA((n, 3))], descs)


def share_op(totals):
    n = len(totals)

    def descs(cins, couts, sems):
        send, recv = sems
        x, y, c = _coords()
        return [pltpu.make_async_remote_copy(cins[a], couts[a].at[c], send.at[a], recv.at[a],
                                             device_id=(x, y, 1 - c), device_id_type=MESH) for a in range(n)]

    return _Comm("share", totals, [jax.ShapeDtypeStruct((2,) + t.shape, t.dtype) for t in totals],
                 [pltpu.SemaphoreType.DMA((n,)), pltpu.SemaphoreType.DMA((n,))], descs)


def _sigmoid(z):
    return 1.0 / (1.0 + jnp.exp(-z))


def _rms_parts(xf):
    r = lax.rsqrt(jnp.mean(xf * xf, axis=-1, keepdims=True) + EPS)
    return xf * r, r


def _rms_bwd(dh, xhat, r, g):
    dg = jnp.sum(dh * xhat, axis=0, keepdims=True)
    dxhat = dh * g
    dx = r * (dxhat - xhat * jnp.mean(dxhat * xhat, axis=-1, keepdims=True))
    return dx, dg


def ffn_fwd(x, g, wgT, wuT, wd, *, tm=512, tf=1408, comm=None):
    S, D = x.shape
    F = wgT.shape[0]
    nf = F // tf

    def body(x_ref, g_ref, wg_ref, wu_ref, wd_ref, o_ref, gate_ref, up_ref, h_sc, acc_sc):
        j = pl.program_id(1)

        @pl.when(j == 0)
        def _():
            xhat, _ = _rms_parts(x_ref[...])
            h_sc[...] = (xhat * g_ref[...]).astype(CDT)
            acc_sc[...] = jnp.zeros_like(acc_sc)

        h = h_sc[...]
        gt = lax.dot_general(h, wg_ref[...], NT, preferred_element_type=F32)
        ut = lax.dot_general(h, wu_ref[...], NT, preferred_element_type=F32)
        gate_ref[...] = gt.astype(CDT)
        up_ref[...] = ut.astype(CDT)
        a = (gt * _sigmoid(gt) * ut).astype(CDT)
        acc_sc[...] += jnp.dot(a, wd_ref[...], preferred_element_type=F32)

        @pl.when(j == nf - 1)
        def _():
            o_ref[...] = x_ref[...] + 0.5 * acc_sc[...]

    return _hosted(comm)(
        body, name="ffn_fwd",
        grid=(S // tm, nf),
        in_specs=[pl.BlockSpec((tm, D), lambda i, j: (i, 0)),
                  pl.BlockSpec((1, D), lambda i, j: (0, 0)),
                  pl.BlockSpec((tf, D), lambda i, j: (j, 0)),
                  pl.BlockSpec((tf, D), lambda i, j: (j, 0)),
                  pl.BlockSpec((tf, D), lambda i, j: (j, 0))],
        out_specs=[pl.BlockSpec((tm, D), lambda i, j: (i, 0)),
                   pl.BlockSpec((tm, tf), lambda i, j: (i, j)),
                   pl.BlockSpec((tm, tf), lambda i, j: (i, j))],
        out_shape=[jax.ShapeDtypeStruct((S, D), F32),
                   jax.ShapeDtypeStruct((S, F), CDT),
                   jax.ShapeDtypeStruct((S, F), CDT)],
        scratch_shapes=[pltpu.VMEM((tm, D), CDT), pltpu.VMEM((tm, D), F32)],
        compiler_params=_params(2),
    )(x, g, wgT, wuT, wd)


def ffn_bwd_dgrad(x, g, dy, gate, up, wgT, wuT, wd, *, tm=512, tf=1408, comm=None):
    S, D = x.shape
    F = wgT.shape[0]
    nf = F // tf

    def body(x_ref, g_ref, dy_ref, gate_ref, up_ref, wg_ref, wu_ref, wd_ref,
             dx_ref, dgate_ref, dup_ref, h_ref, dacc_ref, dg_ref, dh_sc):
        i = pl.program_id(0)
        j = pl.program_id(1)

        @pl.when(j == 0)
        def _():
            xhat, _ = _rms_parts(x_ref[...])
            h_ref[...] = (xhat * g_ref[...]).astype(CDT)
            dacc_ref[...] = (0.5 * dy_ref[...]).astype(CDT)
            dh_sc[...] = jnp.zeros_like(dh_sc)

        @pl.when((i == 0) & (j == 0))
        def _():
            dg_ref[...] = jnp.zeros_like(dg_ref)

        d_a = lax.dot_general(dacc_ref[...], wd_ref[...], NT, preferred_element_type=F32)
        gt = gate_ref[...].astype(F32)
        ut = up_ref[...].astype(F32)
        sg = _sigmoid(gt)
        d_up = (d_a * (gt * sg)).astype(CDT)
        d_gate = (d_a * ut * (sg * (1.0 + gt * (1.0 - sg)))).astype(CDT)
        dgate_ref[...] = d_gate
        dup_ref[...] = d_up
        dh_sc[...] += (jnp.dot(d_gate, wg_ref[...], preferred_element_type=F32)
                       + jnp.dot(d_up, wu_ref[...], preferred_element_type=F32))

        @pl.when(j == nf - 1)
        def _():
            xhat, r = _rms_parts(x_ref[...])
            dx, dg = _rms_bwd(dh_sc[...], xhat, r, g_ref[...])
            dx_ref[...] = dy_ref[...] + dx
            dg_ref[...] += dg

    return _hosted(comm)(
        body, name="ffn_bwd_dgrad",
        grid=(S // tm, nf),
        in_specs=[pl.BlockSpec((tm, D), lambda i, j: (i, 0)),
                  pl.BlockSpec((1, D), lambda i, j: (0, 0)),
                  pl.BlockSpec((tm, D), lambda i, j: (i, 0)),
                  pl.BlockSpec((tm, tf), lambda i, j: (i, j)),
                  pl.BlockSpec((tm, tf), lambda i, j: (i, j)),
                  pl.BlockSpec((tf, D), lambda i, j: (j, 0)),
                  pl.BlockSpec((tf, D), lambda i, j: (j, 0)),
                  pl.BlockSpec((tf, D), lambda i, j: (j, 0))],
        out_specs=[pl.BlockSpec((tm, D), lambda i, j: (i, 0)),
                   pl.BlockSpec((tm, tf), lambda i, j: (i, j)),
                   pl.BlockSpec((tm, tf), lambda i, j: (i, j)),
                   pl.BlockSpec((tm, D), lambda i, j: (i, 0)),
                   pl.BlockSpec((tm, D), lambda i, j: (i, 0)),
                   pl.BlockSpec((1, D), lambda i, j: (0, 0))],
        out_shape=[jax.ShapeDtypeStruct((S, D), F32),
                   jax.ShapeDtypeStruct((S, F), CDT),
                   jax.ShapeDtypeStruct((S, F), CDT),
                   jax.ShapeDtypeStruct((S, D), CDT),
                   jax.ShapeDtypeStruct((S, D), CDT),
                   jax.ShapeDtypeStruct((1, D), F32)],
        scratch_shapes=[pltpu.VMEM((tm, D), F32)],
        compiler_params=_params(2),
    )(x, g, dy, gate, up, wgT, wuT, wd)


def ffn_bwd_wgrad(h, dacc, gate, up, dgate, dup, *, tk=1024, tf=256, comm=None):
    S, D = h.shape
    F = gate.shape[1]
    tk = min(tk, S)

    def body(h_ref, dacc_ref, gate_ref, up_ref, dgate_ref, dup_ref, dwg_ref, dwu_ref, dwd_ref):
        k = pl.program_id(1)

        @pl.when(k == 0)
        def _():
            dwg_ref[...] = jnp.zeros_like(dwg_ref)
            dwu_ref[...] = jnp.zeros_like(dwu_ref)
            dwd_ref[...] = jnp.zeros_like(dwd_ref)

        hh = h_ref[...]
        dwg_ref[...] += lax.dot_general(dgate_ref[...], hh, TN, preferred_element_type=F32)
        dwu_ref[...] += lax.dot_general(dup_ref[...], hh, TN, preferred_element_type=F32)
        gt = gate_ref[...].astype(F32)
        a = (gt * _sigmoid(gt) * up_ref[...].astype(F32)).astype(CDT)
        dwd_ref[...] += lax.dot_general(a, dacc_ref[...], TN, preferred_element_type=F32)

    tok = pl.BlockSpec((tk, D), lambda j, k: (k, 0))
    act = pl.BlockSpec((tk, tf), lambda j, k: (k, j))
    out = pl.BlockSpec((tf, D), lambda j, k: (j, 0))
    return _hosted(comm)(
        body, name="ffn_bwd_wgrad",
        grid=(F // tf, S // tk),
        in_specs=[tok, tok, act, act, act, act],
        out_specs=[out, out, out],
        out_shape=[jax.ShapeDtypeStruct((F, D), F32)] * 3,
        compiler_params=_params(2),
    )(h, dacc, gate, up, dgate, dup)


def loss_head(x, g, target, *, tm=512):
    S, D = x.shape

    def body(x_ref, g_ref, t_ref, loss_ref, dx_ref, dg_ref):
        @pl.when(pl.program_id(0) == 0)
        def _():
            loss_ref[...] = jnp.zeros_like(loss_ref)
            dg_ref[...] = jnp.zeros_like(dg_ref)

        xhat, r = _rms_parts(x_ref[...])
        gg = g_ref[...]
        err = xhat * gg - t_ref[...]
        loss_ref[...] += 0.5 * jnp.sum(jnp.mean(err * err, axis=-1, keepdims=True), axis=0, keepdims=True)
        dx, dg = _rms_bwd(err * (1.0 / D), xhat, r, gg)
        dx_ref[...] = dx
        dg_ref[...] += dg

    row = pl.BlockSpec((tm, D), lambda i: (i, 0))
    vec = pl.BlockSpec((1, D), lambda i: (0, 0))
    return pl.pallas_call(
        body, name="loss_head",
        grid=(S // tm,),
        in_specs=[row, vec, row],
        out_specs=[pl.BlockSpec((1, 1), lambda i: (0, 0)), row, vec],
        out_shape=[jax.ShapeDtypeStruct((1, 1), F32), jax.ShapeDtypeStruct((S, D), F32),
                   jax.ShapeDtypeStruct((1, D), F32)],
        compiler_params=_params(1),
    )(x, g, target)


def _rope_apply(t, cs, sn):
    lane = lax.broadcasted_iota(jnp.int32, t.shape, 1)
    first = (lane % HEAD_DIM) < (HEAD_DIM // 2)
    rot = jnp.where(first, pltpu.roll(t, 128 - HEAD_DIM // 2, 1), pltpu.roll(t, HEAD_DIM // 2, 1))
    return t * cs + rot * sn


def _rope_transpose(d, cs, sn):
    lane = lax.broadcasted_iota(jnp.int32, d.shape, 1)
    first = (lane % HEAD_DIM) < (HEAD_DIM // 2)
    ds = d * sn
    rot = jnp.where(first, pltpu.roll(ds, 128 - HEAD_DIM // 2, 1), pltpu.roll(ds, HEAD_DIM // 2, 1))
    return d * cs + rot


def inproj_fwd(x, g, wextT, cs, sn, *, tm=512, comm=None):
    S, D = x.shape
    scale = HEAD_DIM ** -0.5

    def body(x_ref, g_ref, w_ref, cs_ref, sn_ref, q_ref, k_ref, v_ref, u_ref):
        xhat, _ = _rms_parts(x_ref[...])
        h = (xhat * g_ref[...]).astype(CDT)
        p = lax.dot_general(h, w_ref[...], NT, preferred_element_type=F32)
        c, s = cs_ref[...], sn_ref[...]
        for b in range(4):
            q_ref[:, 128 * b:128 * (b + 1)] = (_rope_apply(p[:, 128 * b:128 * (b + 1)], c, s) * scale).astype(CDT)
        for b in range(2):
            k_ref[:, 128 * b:128 * (b + 1)] = _rope_apply(p[:, 512 + 128 * b:512 + 128 * (b + 1)], c, s).astype(CDT)
        v_ref[...] = p[:, 768:1024].astype(CDT)
        u_ref[...] = p[:, 1024:2048]

    def row(w):
        return pl.BlockSpec((tm, w), lambda i: (i, 0))

    return _hosted(comm)(
        body, name="inproj_fwd",
        grid=(S // tm,),
        in_specs=[row(D), pl.BlockSpec((1, D), lambda i: (0, 0)),
                  pl.BlockSpec((D_EXT, D), lambda i: (0, 0)), row(128), row(128)],
        out_specs=[row(512), row(256), row(256), row(1024)],
        out_shape=[jax.ShapeDtypeStruct((S, 512), CDT), jax.ShapeDtypeStruct((S, 256), CDT),
                   jax.ShapeDtypeStruct((S, 256), CDT), jax.ShapeDtypeStruct((S, 1024), F32)],
        compiler_params=_params(1),
    )(x, g, wextT, cs, sn)


def _stack_heads(p0, p1):
    lane = lax.broadcasted_iota(jnp.int32, p0.shape, 1)
    lo = lane < HEAD_DIM
    z = jnp.zeros_like(p0)
    return jnp.concatenate([jnp.where(lo, p0, z), jnp.where(lo, z, p0),
                            jnp.where(lo, p1, z), jnp.where(lo, z, p1)], axis=0)


def _unstack_heads(o):
    lane = lax.broadcasted_iota(jnp.int32, (BLOCK, 128), 1)
    lo = lane < HEAD_DIM
    return (jnp.where(lo, o[0:128], o[128:256]), jnp.where(lo, o[256:384], o[384:512]))


def _band_mask_qk(n):
    i = lax.broadcasted_iota(jnp.int32, (4 * BLOCK, 2 * BLOCK), 0) % BLOCK
    c = lax.broadcasted_iota(jnp.int32, (4 * BLOCK, 2 * BLOCK), 1)
    return (c > i) & (c <= i + BLOCK) & ((n > 0) | (c >= BLOCK))


def _band_mask_kq(n):
    c = lax.broadcasted_iota(jnp.int32, (2 * BLOCK, 4 * BLOCK), 0)
    i = lax.broadcasted_iota(jnp.int32, (2 * BLOCK, 4 * BLOCK), 1) % BLOCK
    return (c > i) & (c <= i + BLOCK) & ((n > 0) | (c >= BLOCK))


def attn_fwd(q, k, v, sink_col, *, nb=2, comm=None):
    S = q.shape[0]
    tq = nb * BLOCK

    def body(q_ref, k_ref, v_ref, sink_ref, o_ref):
        t = pl.program_id(0)
        for b in range(nb):
            n = t * nb + b
            prev = pl.multiple_of(jnp.maximum(n - 1, 0) * BLOCK, BLOCK)
            cur = pl.multiple_of(n * BLOCK, BLOCK)
            rows = slice(b * BLOCK, (b + 1) * BLOCK)
            mask = _band_mask_qk(n)
            for gidx in range(2):
                lanes = slice(128 * gidx, 128 * (gidx + 1))
                qs = _stack_heads(q_ref[rows, 256 * gidx:256 * gidx + 128],
                                  q_ref[rows, 256 * gidx + 128:256 * gidx + 256])
                kb = jnp.concatenate([k_ref[pl.ds(prev, BLOCK), lanes], k_ref[pl.ds(cur, BLOCK), lanes]], axis=0)
                vb = jnp.concatenate([v_ref[pl.ds(prev, BLOCK), lanes], v_ref[pl.ds(cur, BLOCK), lanes]], axis=0)
                s = lax.dot_general(qs, kb, NT, preferred_element_type=F32)
                s = jnp.where(mask, s, NEG)
                sink = sink_ref[gidx]
                m = jnp.maximum(jnp.max(s, axis=1, keepdims=True), sink)
                p = jnp.exp(s - m)
                den = jnp.sum(p, axis=1, keepdims=True) + jnp.exp(sink - m)
                o = jnp.dot(p.astype(CDT), vb, preferred_element_type=F32) / den
                o0, o1 = _unstack_heads(o)
                o_ref[rows, 256 * gidx:256 * gidx + 128] = o0.astype(CDT)
                o_ref[rows, 256 * gidx + 128:256 * gidx + 256] = o1.astype(CDT)

    return _hosted(comm)(
        body, name="attn_fwd",
        grid=(S // tq,),
        in_specs=[pl.BlockSpec((tq, 512), lambda t: (t, 0)),
                  pl.BlockSpec((S, 256), lambda t: (0, 0)),
                  pl.BlockSpec((S, 256), lambda t: (0, 0)),
                  pl.BlockSpec((2, 4 * BLOCK, 1), lambda t: (0, 0, 0))],
        out_specs=pl.BlockSpec((tq, 512), lambda t: (t, 0)),
        out_shape=jax.ShapeDtypeStruct((S, 512), CDT),
        compiler_params=_params(1),
    )(q, k, v, sink_col)


def attn_bwd(q, k, v, do, sink_row, *, nb=2, comm=None):
    S = q.shape[0]
    tq = nb * BLOCK
    scale = HEAD_DIM ** -0.5

    def body(q_ref, k_ref, v_ref, do_ref, sink_ref, dq_ref, dk_ref, dv_ref, dsink_ref):
        t = pl.program_id(0)

        @pl.when(t == 0)
        def _():
            dk_ref[...] = jnp.zeros_like(dk_ref)
            dv_ref[...] = jnp.zeros_like(dv_ref)
            dsink_ref[...] = jnp.zeros_like(dsink_ref)

        for b in range(nb):
            n = t * nb + b
            prev = pl.multiple_of(jnp.maximum(n - 1, 0) * BLOCK, BLOCK)
            cur = pl.multiple_of(n * BLOCK, BLOCK)
            rows = slice(b * BLOCK, (b + 1) * BLOCK)
            mask = _band_mask_kq(n)
            for gidx in range(2):
                lanes = slice(128 * gidx, 128 * (gidx + 1))
                qs = _stack_heads(q_ref[rows, 256 * gidx:256 * gidx + 128],
                                  q_ref[rows, 256 * gidx + 128:256 * gidx + 256])
                dos = _stack_heads(do_ref[rows, 256 * gidx:256 * gidx + 128],
                                   do_ref[rows, 256 * gidx + 128:256 * gidx + 256])
                kb = jnp.concatenate([k_ref[pl.ds(prev, BLOCK), lanes], k_ref[pl.ds(cur, BLOCK), lanes]], axis=0)
                vb = jnp.concatenate([v_ref[pl.ds(prev, BLOCK), lanes], v_ref[pl.ds(cur, BLOCK), lanes]], axis=0)
                st = lax.dot_general(kb, qs, NT, preferred_element_type=F32)
                st = jnp.where(mask, st, NEG)
                sink = sink_ref[gidx]
                m = jnp.maximum(jnp.max(st, axis=0, keepdims=True), sink)
                e = jnp.exp(st - m)
                es = jnp.exp(sink - m)
                inv = 1.0 / (jnp.sum(e, axis=0, keepdims=True) + es)
                pt = e * inv
                dpt = lax.dot_general(vb, dos, NT, preferred_element_type=F32)
                delta = jnp.sum(pt * dpt, axis=0, keepdims=True)
                dst = (pt * (dpt - delta)).astype(CDT)
                dsink_ref[gidx] += -(es * inv) * delta
                dvb = jnp.dot(pt.astype(CDT), dos, preferred_element_type=F32)
                dkb = jnp.dot(dst, qs, preferred_element_type=F32)
                dqs = lax.dot_general(dst, kb, TN, preferred_element_type=F32) * scale
                dq0, dq1 = _unstack_heads(dqs)
                dq_ref[rows, 256 * gidx:256 * gidx + 128] = dq0
                dq_ref[rows, 256 * gidx + 128:256 * gidx + 256] = dq1
                dk_ref[pl.ds(prev, BLOCK), lanes] += dkb[0:BLOCK]
                dk_ref[pl.ds(cur, BLOCK), lanes] += dkb[BLOCK:2 * BLOCK]
                dv_ref[pl.ds(prev, BLOCK), lanes] += dvb[0:BLOCK]
                dv_ref[pl.ds(cur, BLOCK), lanes] += dvb[BLOCK:2 * BLOCK]

    full = pl.BlockSpec((S, 256), lambda t: (0, 0))
    tile = pl.BlockSpec((tq, 512), lambda t: (t, 0))
    srow = pl.BlockSpec((2, 1, 4 * BLOCK), lambda t: (0, 0, 0))
    return _hosted(comm)(
        body, name="attn_bwd",
        grid=(S // tq,),
        in_specs=[tile, full, full, tile, srow],
        out_specs=[tile, full, full, srow],
        out_shape=[jax.ShapeDtypeStruct((S, 512), F32), jax.ShapeDtypeStruct((S, 256), F32),
                   jax.ShapeDtypeStruct((S, 256), F32), jax.ShapeDtypeStruct((2, 1, 4 * BLOCK), F32)],
        compiler_params=_params(1),
    )(q, k, v, do, sink_row)


def _glu(u):
    a = u[:, 0:CONV_C]
    gt = u[:, CONV_C:2 * CONV_C]
    sg = _sigmoid(gt)
    return a, sg, a * sg


CONV_CHUNK = 64


def conv_fwd(u, cw, cb, lg, lb, *, tm=512, comm=None):
    S = u.shape[0]
    nh = tm // HALO

    def body(u_ref, uh_ref, cw_ref, cb_ref, lg_ref, lb_ref, o_ref, y_ref, hbuf):
        t = pl.program_id(0)
        _, _, hg = _glu(u_ref[...])
        _, _, hh = _glu(uh_ref[...])
        hbuf[0:HALO, :] = jnp.where(t > 0, hh, jnp.zeros_like(hh))
        hbuf[HALO:HALO + tm, :] = hg
        off = HALO - (CONV_K - 1)
        for c0 in range(0, tm, CONV_CHUNK):
            acc = jnp.zeros((CONV_CHUNK, CONV_C), F32) + cb_ref[...]
            for j in range(CONV_K):
                acc = acc + cw_ref[j:j + 1, :] * hbuf[c0 + off + j:c0 + off + j + CONV_CHUNK, :]
            y_ref[c0:c0 + CONV_CHUNK, :] = acc
        y = y_ref[...]
        yc = y - jnp.mean(y, axis=-1, keepdims=True)
        r = lax.rsqrt(jnp.mean(yc * yc, axis=-1, keepdims=True) + EPS)
        z = yc * r * lg_ref[...] + lb_ref[...]
        o_ref[...] = (z * _sigmoid(z)).astype(CDT)

    vec = pl.BlockSpec((1, CONV_C), lambda t: (0, 0))
    return _hosted(comm)(
        body, name="conv_fwd",
        grid=(S // tm,),
        in_specs=[pl.BlockSpec((tm, 2 * CONV_C), lambda t: (t, 0)),
                  pl.BlockSpec((HALO, 2 * CONV_C), lambda t: (jnp.maximum(t * nh - 1, 0), 0)),
                  pl.BlockSpec((CONV_K, CONV_C), lambda t: (0, 0)), vec, vec, vec],
        out_specs=[pl.BlockSpec((tm, CONV_C), lambda t: (t, 0)), pl.BlockSpec((tm, CONV_C), lambda t: (t, 0))],
        out_shape=[jax.ShapeDtypeStruct((S, CONV_C), CDT), jax.ShapeDtypeStruct((S, CONV_C), F32)],
        scratch_shapes=[pltpu.VMEM((HALO + tm, CONV_C), F32)],
        compiler_params=_params(1),
    )(u, u, cw, cb, lg, lb)


def conv_bwd(dc, u, y, cw, lg, lb, *, tm=512, comm=None):
    S = u.shape[0]
    nh = tm // HALO
    nt = S // tm

    def ln_bwd(dcv, yv, lgv, lbv):
        yc = yv - jnp.mean(yv, axis=-1, keepdims=True)
        r = lax.rsqrt(jnp.mean(yc * yc, axis=-1, keepdims=True) + EPS)
        yhat = yc * r
        z = yhat * lgv + lbv
        sg = _sigmoid(z)
        dz = dcv * (sg * (1.0 + z * (1.0 - sg)))
        dyhat = dz * lgv
        dy = r * (dyhat - jnp.mean(dyhat, axis=-1, keepdims=True)
                  - yhat * jnp.mean(dyhat * yhat, axis=-1, keepdims=True))
        return dy, dz, yhat

    def body(dc_ref, dcn_ref, u_ref, uh_ref, y_ref, yn_ref, cw_ref, lg_ref, lb_ref,
             du_ref, dcw_ref, dcb_ref, dlg_ref, dlb_ref, hbuf, dybuf, dhg_sc, dw_sc):
        t = pl.program_id(0)

        @pl.when(t == 0)
        def _():
            dw_sc[...] = jnp.zeros_like(dw_sc)
            dcb_ref[...] = jnp.zeros_like(dcb_ref)
            dlg_ref[...] = jnp.zeros_like(dlg_ref)
            dlb_ref[...] = jnp.zeros_like(dlb_ref)

        lgv, lbv = lg_ref[...], lb_ref[...]
        dy, dz, yhat = ln_bwd(dc_ref[...].astype(F32), y_ref[...], lgv, lbv)
        dyn, _, _ = ln_bwd(dcn_ref[...].astype(F32), yn_ref[...], lgv, lbv)
        dlb_ref[...] += jnp.sum(dz, axis=0, keepdims=True)
        dlg_ref[...] += jnp.sum(dz * yhat, axis=0, keepdims=True)
        dcb_ref[...] += jnp.sum(dy, axis=0, keepdims=True)
        dybuf[0:tm, :] = dy
        dybuf[tm:tm + HALO, :] = jnp.where(t < nt - 1, dyn, jnp.zeros_like(dyn))

        a, sg, hg = _glu(u_ref[...])
        _, _, hh = _glu(uh_ref[...])
        hbuf[0:HALO, :] = jnp.where(t > 0, hh, jnp.zeros_like(hh))
        hbuf[HALO:HALO + tm, :] = hg

        off = HALO - (CONV_K - 1)
        for c0 in range(0, tm, CONV_CHUNK):
            acc = jnp.zeros((CONV_CHUNK, CONV_C), F32)
            dyc = dybuf[c0:c0 + CONV_CHUNK, :]
            for j in range(CONV_K):
                acc = acc + cw_ref[j:j + 1, :] * dybuf[c0 + (CONV_K - 1) - j:c0 + (CONV_K - 1) - j + CONV_CHUNK, :]
                prod = dyc * hbuf[c0 + off + j:c0 + off + j + CONV_CHUNK, :]
                dw_sc[j] += jnp.sum(prod.reshape(CONV_CHUNK // 8, 8, CONV_C), axis=0)
            dhg_sc[c0:c0 + CONV_CHUNK, :] = acc

        dhg = dhg_sc[...]
        du_ref[:, 0:CONV_C] = dhg * sg
        du_ref[:, CONV_C:2 * CONV_C] = dhg * a * sg * (1.0 - sg)

        @pl.when(t == nt - 1)
        def _():
            dcw_ref[...] = jnp.sum(dw_sc[...], axis=1)

    vec = pl.BlockSpec((1, CONV_C), lambda t: (0, 0))
    tile = pl.BlockSpec((tm, CONV_C), lambda t: (t, 0))
    nxt = pl.BlockSpec((HALO, CONV_C), lambda t: (jnp.minimum((t + 1) * nh, S // HALO - 1), 0))
    return _hosted(comm)(
        body, name="conv_bwd",
        grid=(nt,),
        in_specs=[tile, nxt,
                  pl.BlockSpec((tm, 2 * CONV_C), lambda t: (t, 0)),
                  pl.BlockSpec((HALO, 2 * CONV_C), lambda t: (jnp.maximum(t * nh - 1, 0), 0)),
                  tile, nxt,
                  pl.BlockSpec((CONV_K, CONV_C), lambda t: (0, 0)), vec, vec],
        out_specs=[pl.BlockSpec((tm, 2 * CONV_C), lambda t: (t, 0)),
                   pl.BlockSpec((CONV_K, CONV_C), lambda t: (0, 0)), vec, vec, vec],
        out_shape=[jax.ShapeDtypeStruct((S, 2 * CONV_C), F32), jax.ShapeDtypeStruct((CONV_K, CONV_C), F32),
                   jax.ShapeDtypeStruct((1, CONV_C), F32), jax.ShapeDtypeStruct((1, CONV_C), F32),
                   jax.ShapeDtypeStruct((1, CONV_C), F32)],
        scratch_shapes=[pltpu.VMEM((HALO + tm, CONV_C), F32), pltpu.VMEM((tm + HALO, CONV_C), F32),
                        pltpu.VMEM((tm, CONV_C), F32), pltpu.VMEM((CONV_K, 8, CONV_C), F32)],
        compiler_params=_params(1),
    )(dc, dc, u, u, y, y, cw, lg, lb)


def outproj_fwd(x, ao, co, wout, *, tm=512):
    S, D = x.shape

    def body(x_ref, a_ref, c_ref, w_ref, o_ref):
        o_ref[...] = (x_ref[...]
                      + jnp.dot(a_ref[...], w_ref[0:ATT_W, :], preferred_element_type=F32)
                      + jnp.dot(c_ref[...], w_ref[ATT_W:ATT_W + CONV_C, :], preferred_element_type=F32))

    return pl.pallas_call(
        body, name="outproj_fwd",
        grid=(S // tm,),
        in_specs=[pl.BlockSpec((tm, D), lambda i: (i, 0)), pl.BlockSpec((tm, ATT_W), lambda i: (i, 0)),
                  pl.BlockSpec((tm, CONV_C), lambda i: (i, 0)), pl.BlockSpec((D, D), lambda i: (0, 0))],
        out_specs=pl.BlockSpec((tm, D), lambda i: (i, 0)),
        out_shape=jax.ShapeDtypeStruct((S, D), F32),
        compiler_params=_params(1),
    )(x, ao, co, wout)


def outproj_bwd(dx, ao, co, wout, *, tm=512, comm=None):
    S, D = dx.shape

    def body(dx_ref, a_ref, c_ref, w_ref, da_ref, dc_ref, dw_ref):
        @pl.when(pl.program_id(0) == 0)
        def _():
            dw_ref[...] = jnp.zeros_like(dw_ref)

        dxb = dx_ref[...].astype(CDT)
        da_ref[...] = lax.dot_general(dxb, w_ref[0:ATT_W, :], NT, preferred_element_type=F32).astype(CDT)
        dc_ref[...] = lax.dot_general(dxb, w_ref[ATT_W:ATT_W + CONV_C, :], NT, preferred_element_type=F32)
        dw_ref[0:ATT_W, :] += lax.dot_general(a_ref[...], dxb, TN, preferred_element_type=F32)
        dw_ref[ATT_W:ATT_W + CONV_C, :] += lax.dot_general(c_ref[...], dxb, TN, preferred_element_type=F32)

    return _hosted(comm)(
        body, name="outproj_bwd",
        grid=(S // tm,),
        in_specs=[pl.BlockSpec((tm, D), lambda i: (i, 0)), pl.BlockSpec((tm, ATT_W), lambda i: (i, 0)),
                  pl.BlockSpec((tm, CONV_C), lambda i: (i, 0)), pl.BlockSpec((D, D), lambda i: (0, 0))],
        out_specs=[pl.BlockSpec((tm, ATT_W), lambda i: (i, 0)), pl.BlockSpec((tm, CONV_C), lambda i: (i, 0)),
                   pl.BlockSpec((D, D), lambda i: (0, 0))],
        out_shape=[jax.ShapeDtypeStruct((S, ATT_W), CDT), jax.ShapeDtypeStruct((S, CONV_C), F32),
                   jax.ShapeDtypeStruct((D, D), F32)],
        compiler_params=_params(1),
    )(dx, ao, co, wout)


def inproj_bwd(x, g, dres, dq, dk, dv, du, wextT, cs, sn, *, tm=256):
    S, D = x.shape

    def body(x_ref, g_ref, dres_ref, dq_ref, dk_ref, dv_ref, du_ref, w_ref, cs_ref, sn_ref,
             dx_ref, dw_ref, dg_ref, dp_sc):
        @pl.when(pl.program_id(0) == 0)
        def _():
            dw_ref[...] = jnp.zeros_like(dw_ref)
            dg_ref[...] = jnp.zeros_like(dg_ref)

        c, s = cs_ref[...], sn_ref[...]
        for b in range(4):
            dp_sc[:, 128 * b:128 * (b + 1)] = _rope_transpose(dq_ref[:, 128 * b:128 * (b + 1)], c, s).astype(CDT)
        for b in range(2):
            dp_sc[:, 512 + 128 * b:512 + 128 * (b + 1)] = _rope_transpose(
                dk_ref[:, 128 * b:128 * (b + 1)], c, s).astype(CDT)
        dp_sc[:, 768:1024] = dv_ref[...].astype(CDT)
        dp_sc[:, 1024:2048] = du_ref[...].astype(CDT)
        dp = dp_sc[...]
        xhat, r = _rms_parts(x_ref[...])
        gg = g_ref[...]
        h = (xhat * gg).astype(CDT)
        dh = jnp.dot(dp, w_ref[...], preferred_element_type=F32)
        dw_ref[...] += lax.dot_general(dp, h, TN, preferred_element_type=F32)
        dx, dg = _rms_bwd(dh, xhat, r, gg)
        dx_ref[...] = dres_ref[...] + dx
        dg_ref[...] += dg

    def row(w):
        return pl.BlockSpec((tm, w), lambda i: (i, 0))

    return pl.pallas_call(
        body, name="inproj_bwd",
        grid=(S // tm,),
        in_specs=[row(D), pl.BlockSpec((1, D), lambda i: (0, 0)), row(D), row(512), row(256), row(256),
                  row(1024), pl.BlockSpec((D_EXT, D), lambda i: (0, 0)), row(128), row(128)],
        out_specs=[row(D), pl.BlockSpec((D_EXT, D), lambda i: (0, 0)), pl.BlockSpec((1, D), lambda i: (0, 0))],
        out_shape=[jax.ShapeDtypeStruct((S, D), F32), jax.ShapeDtypeStruct((D_EXT, D), F32),
                   jax.ShapeDtypeStruct((1, D), F32)],
        scratch_shapes=[pltpu.VMEM((tm, D_EXT), CDT)],
        compiler_params=_params(1),
    )(x, g, dres, dq, dk, dv, du, wextT, cs, sn)


def _rope_tables(positions):
    inv_freq = 1.0 / (10000.0 ** (jnp.arange(0, HEAD_DIM, 2, dtype=F32) / HEAD_DIM))
    ang = positions.astype(F32).reshape(-1, 1) * inv_freq
    cos, sin = jnp.cos(ang), jnp.sin(ang)
    cs = jnp.tile(jnp.concatenate([cos, cos], axis=-1), (1, 2))
    sn = jnp.tile(jnp.concatenate([-sin, sin], axis=-1), (1, 2))
    return cs, sn


def _widen_w_in(w):
    q, u = w[0:512], w[768:1792]
    parts = [q]
    for base in (512, 576, 640, 704):
        parts += [w[base:base + 64], w[base:base + 64]]
    return jnp.concatenate(parts + [u], axis=0)


def _fold_w_in(d):
    parts = [d[0:512]]
    for base in (512, 640, 768, 896):
        parts.append(d[base:base + 64] + d[base + 64:base + 128])
    return jnp.concatenate(parts + [d[1024:2048]], axis=0)


def add_half(g5, r1, c_idx):
    _, _, r, D = g5.shape

    def body(c_ref, g_ref, r_ref, o_ref):
        o_ref[...] = (g_ref[...] + r_ref[...]).astype(CDT)

    return pl.pallas_call(
        body, name="add_half",
        grid_spec=pltpu.PrefetchScalarGridSpec(
            num_scalar_prefetch=1, grid=(N_CHIPS,),
            in_specs=[pl.BlockSpec((None, None, r, D), lambda s, cr: (s, cr[0], 0, 0)),
                      pl.BlockSpec((None, r, D), lambda s, cr: (s, 0, 0))],
            out_specs=pl.BlockSpec((None, r, D), lambda s, cr: (s, 0, 0))),
        out_shape=jax.ShapeDtypeStruct((N_CHIPS, r, D), CDT),
        compiler_params=_params(1),
    )(c_idx, g5, r1)


def sum_partials(part, recv3, j_idx):
    _, r, D = part.shape
    tr_ = r // 2

    def body(j_ref, p_ref, r_ref, o_ref):
        o_ref[...] = ((p_ref[...].astype(F32) + r_ref[0].astype(F32)) + r_ref[1].astype(F32)) + r_ref[2].astype(F32)

    return pl.pallas_call(
        body, name="sum_partials",
        grid_spec=pltpu.PrefetchScalarGridSpec(
            num_scalar_prefetch=1, grid=(2,),
            in_specs=[pl.BlockSpec((None, tr_, D), lambda i, jr: (jr[0], i, 0)),
                      pl.BlockSpec((3, tr_, D), lambda i, jr: (0, i, 0))],
            out_specs=pl.BlockSpec((tr_, D), lambda i, jr: (i, 0))),
        out_shape=jax.ShapeDtypeStruct((r, D), F32),
        compiler_params=_params(1),
    )(j_idx, part, recv3)


class _Chain:
    def __init__(self, grads, c_idx, j_idx):
        self.c_idx, self.j_idx = c_idx, j_idx
        self.g5 = [g.reshape(N_CHIPS, 2, g.shape[0] // (2 * N_CHIPS), g.shape[1]) for g in grads]

    def swap(self):
        return swap_op(self.g5)

    def after_swap(self, recv):
        self.parts = [add_half(g, r, self.c_idx) for g, r in zip(self.g5, recv)]

    def xchg(self):
        return exchange_op(self.parts)

    def after_xchg(self, recv):
        self.totals = [sum_partials(p, r, self.j_idx) for p, r in zip(self.parts, recv)]

    def share(self):
        return share_op(self.totals)

    def after_share(self, recv):
        both = [_own_slab(h, t, self.c_idx[0]) for h, t in zip(recv, self.totals)]
        self.final = [h.reshape(2 * h.shape[1], h.shape[2]) for h in both]


def all_reduce_small(vec):
    R = vec.shape[0]

    def body(v_ref, o_ref, buf, send, recv):
        x, y, c = _coords()
        me = 4 * x + 2 * y + c
        buf[me] = v_ref[...]
        cps = []
        for m in range(1, N_DEV):
            dx, dy, dc = (m >> 2) & 1, (m >> 1) & 1, m & 1
            cp = pltpu.make_async_remote_copy(v_ref, buf.at[me], send.at[m - 1], recv.at[m - 1],
                                              device_id=((x + dx) % 2, (y + dy) % 2, (c + dc) % 2),
                                              device_id_type=MESH)
            cp.start()
            cps.append(cp)
        for cp in cps:
            cp.wait()
        acc = buf[0]
        for d in range(1, N_DEV):
            acc = acc + buf[d]
        o_ref[...] = acc

    return pl.pallas_call(
        body, name="all_reduce_small",
        in_specs=[pl.BlockSpec(memory_space=pltpu.VMEM)], out_specs=pl.BlockSpec(memory_space=pltpu.VMEM),
        out_shape=jax.ShapeDtypeStruct(vec.shape, F32),
        scratch_shapes=[pltpu.VMEM((N_DEV, R, 128), F32), pltpu.SemaphoreType.DMA((N_DEV - 1,)),
                        pltpu.SemaphoreType.DMA((N_DEV - 1,))],
    )(vec)


def adamw(w, g, m, v, *, tm=512):
    R, C = w.shape
    tm = max(t for t in range(8, min(tm, R) + 1, 8) if R % t == 0)
    c1 =1.0 - ADAM_B1 ** ADAM_STEP
    c2 = 1.0 - ADAM_B2 ** ADAM_STEP

    def body(w_ref, g_ref, m_ref, v_ref, d_ref, nm_ref, nv_ref):
        gg = g_ref[...]
        nm = ADAM_B1 * m_ref[...] + (1.0 - ADAM_B1) * gg
        nv = ADAM_B2 * v_ref[...] + (1.0 - ADAM_B2) * (gg * gg)
        nm_ref[...] = nm
        nv_ref[...] = nv
        d_ref[...] = -ADAM_LR * ((nm / c1) / (jnp.sqrt(nv / c2) + ADAM_EPS) + ADAM_WD * w_ref[...])

    blk = pl.BlockSpec((tm, C), lambda i: (i, 0))
    return pl.pallas_call(
        body, name="adamw",
        grid=(pl.cdiv(R, tm),),
        in_specs=[blk] * 4, out_specs=[blk] * 3,
        out_shape=[jax.ShapeDtypeStruct((R, C), F32)] * 3,
        compiler_params=_params(1),
    )(w, g, m, v)


_SMALL = (("n1", (2, D_MODEL)), ("nm", (2, D_MODEL)), ("n2", (2, D_MODEL)), ("nf", (D_MODEL,)),
          ("cb", (2, CONV_C)), ("lg", (2, CONV_C)), ("lb", (2, CONV_C)), ("sinks", (2, N_HEADS)),
          ("cw", (2, CONV_K, CONV_C)))


def _pack(parts, rows):
    flat = jnp.concatenate([p.reshape(-1).astype(F32) for p in parts])
    return jnp.pad(flat, (0, rows * 128 - flat.shape[0])).reshape(rows, 128)


def _unpack(block, shapes):
    flat = block.reshape(-1)
    out, o = [], 0
    for shp in shapes:
        n = 1
        for s in shp:
            n *= s
        out.append(flat[o:o + n].reshape(shp))
        o += n
    return out


def kernel(x, positions, ffn1_norm, ffn1_w_gate, ffn1_w_up, ffn1_w_down, mix_norm, w_in, conv_w, conv_b, conv_ln_g, conv_ln_b, attn_sinks, w_out, ffn2_norm, ffn2_w_gate, ffn2_w_up, ffn2_w_down, final_norm, loss_target, m_ffn1_norm, m_ffn1_w_gate, m_ffn1_w_up, m_ffn1_w_down, m_mix_norm, m_w_in, m_conv_w, m_conv_b, m_conv_ln_g, m_conv_ln_b, m_attn_sinks, m_w_out, m_ffn2_norm, m_ffn2_w_gate, m_ffn2_w_up, m_ffn2_w_down, m_final_norm, v_ffn1_norm, v_ffn1_w_gate, v_ffn1_w_up, v_ffn1_w_down, v_mix_norm, v_w_in, v_conv_w, v_conv_b, v_conv_ln_g, v_conv_ln_b, v_attn_sinks, v_w_out, v_ffn2_norm, v_ffn2_w_gate, v_ffn2_w_up, v_ffn2_w_down, v_final_norm):
    cx, cy, cc = _coords()
    chip = 2 * cx + cy
    c_idx = jnp.reshape(cc, (1,)).astype(jnp.int32)
    j_idx = jnp.reshape(chip, (1,)).astype(jnp.int32)
    L = ffn1_norm.shape[0]
    tr = lambda a: jnp.swapaxes(a, 1, 2)

    sh = dict(f1g=tr(ffn1_w_gate), f1u=tr(ffn1_w_up), f1d=ffn1_w_down, f2g=tr(ffn2_w_gate),
              f2u=tr(ffn2_w_up), f2d=ffn2_w_down, win=tr(w_in), wout=w_out)
    sh = {k: [v[l].astype(CDT) for l in range(L)] for k, v in sh.items()}
    W = {}

    def gather_op(keys):
        return ag_op([conv_w if k == "cw" else sh[k[0]][k[1]] for k in keys])

    def take(keys, res):
        for k, a in zip(keys, res):
            if k == "cw":
                W[k] = _own_slab(a, conv_w, chip)
            else:
                W[k] = _own_slab(a, sh[k[0]][k[1]], chip).reshape(N_CHIPS * a.shape[1], a.shape[2])

    def with_ag(fn, keys, *args):
        if not keys:
            return fn(*args)
        main, res = fn(*args, comm=gather_op(keys))
        take(keys, res)
        return main

    ag_hosts = {("ffn1", 0): [("win", 0), "cw", ("f2g", 0), ("f1d", 1)],
                ("inproj", 0): [("wout", 0)], ("attn", 0): [("f2u", 0)], ("conv", 0): [("f2d", 0)],
                ("ffn2", 0): [("f1g", 1), ("f1u", 1)],
                ("ffn1", 1): [("win", 1), ("f2g", 1), ("f2u", 1)],
                ("inproj", 1): [("wout", 1)], ("attn", 1): [("f2d", 1)]}
    for k_, a in zip(("f1g", "f1u", "f1d"), first_gather([sh[k_][0] for k_ in ("f1g", "f1u", "f1d")])):
        own = sh[k_][0].reshape(a.shape[1:])
        W[(k_, 0)] = _own_slab(a, own, chip).reshape(N_CHIPS * sh[k_][0].shape[0], sh[k_][0].shape[1])

    cs, sn = _rope_tables(positions)
    saved = []
    h = x[0]
    for l in range(L):
        sink = attn_sinks[l].reshape(2, 4)
        sink_col = jnp.repeat(sink, BLOCK, axis=1).reshape(2, 4 * BLOCK, 1)
        x0 = h
        x1, g1, u1 = with_ag(ffn_fwd, ag_hosts.get(("ffn1", l)), x0, ffn1_norm[l][None],
                             W[("f1g", l)], W[("f1u", l)], W[("f1d", l)])
        wext = _widen_w_in(W[("win", l)])
        q, k, v, u = with_ag(inproj_fwd, ag_hosts.get(("inproj", l)), x1, mix_norm[l][None], wext, cs, sn)
        ao = with_ag(attn_fwd, ag_hosts.get(("attn", l)), q, k, v, sink_col)
        cwl = jnp.transpose(W["cw"][:, l], (1, 0, 2)).reshape(CONV_K, CONV_C)
        co, yc = with_ag(conv_fwd, ag_hosts.get(("conv", l)), u, cwl, conv_b[l][None], conv_ln_g[l][None],
                         conv_ln_b[l][None])
        x2 = outproj_fwd(x1, ao, co, W[("wout", l)])
        x3, g2, u2 = with_ag(ffn_fwd, ag_hosts.get(("ffn2", l)), x2, ffn2_norm[l][None],
                             W[("f2g", l)], W[("f2u", l)], W[("f2d", l)])
        saved.append((x0, x1, x2, g1, u1, g2, u2, q, k, v, u, ao, co, yc, sink, wext, cwl))
        h = x3

    loss, dx, dnf = loss_head(h, final_norm[None], loss_target[0])

    def hosted(fn, stages, *args):
        stages = [s for s in stages if s is not None]
        if not stages:
            return fn(*args)
        ops = [op for op, _ in stages]
        main, res = fn(*args, comm=_merge(*ops))
        for (_, cb), r in zip(stages, _split(res, *ops)):
            cb(r)
        return main

    def st(chain, stage):
        if chain is None:
            return None
        return {"swap": (chain.swap, chain.after_swap), "xchg": (chain.xchg, chain.after_xchg),
                "share": (chain.share, chain.after_share)}[stage]

    def stage(chain, name):
        s = st(chain, name)
        return None if s is None else (s[0](), s[1])

    small = {k_: [None] * L for k_ in ("n1", "nm", "n2", "cw", "cb", "lg", "lb", "sinks")}
    chains = {}
    up_mx = up_f1 = None
    for l in reversed(range(L)):
        x0, x1, x2, g1, u1, g2, u2, q, k, v, u, ao, co, yc, sink, wext, cwl = saved[l]
        sink_row = jnp.repeat(sink, BLOCK, axis=1).reshape(2, 1, 4 * BLOCK)
        dx2, dgt, dup, hh, dacc, small["n2"][l] = hosted(
            ffn_bwd_dgrad, [stage(up_mx, "share"), stage(up_f1, "swap")],
            x2, ffn2_norm[l][None], dx, g2, u2, W[("f2g", l)], W[("f2u", l)], W[("f2d", l)])
        gf2 = hosted(ffn_bwd_wgrad, [stage(up_f1, "xchg")], hh, dacc, g2, u2, dgt, dup)
        f2 = chains[("f2", l)] = _Chain(gf2, c_idx, j_idx)
        da, dc, gwout = hosted(outproj_bwd, [stage(up_f1, "share"), stage(f2, "swap")],
                               dx2, ao, co, W[("wout", l)])
        du, small["cw"][l], small["cb"][l], small["lg"][l], small["lb"][l] = hosted(
            conv_bwd, [stage(f2, "xchg")], dc, u, yc, cwl, conv_ln_g[l][None], conv_ln_b[l][None])
        dq, dk, dv, dsink = hosted(attn_bwd, [stage(f2, "share")], q, k, v, da, sink_row)
        small["sinks"][l] = jnp.sum(dsink.reshape(2, 4, BLOCK), axis=-1).reshape(N_HEADS)
        dx1, gwext, small["nm"][l] = inproj_bwd(x1, mix_norm[l][None], dx2, dq, dk, dv, du, wext, cs, sn)
        mx = chains[("mx", l)] = _Chain([gwout, _fold_w_in(gwext)], c_idx, j_idx)
        dx, dgt, dup, hh, dacc, small["n1"][l] = hosted(
            ffn_bwd_dgrad, [stage(mx, "swap")],
            x0, ffn1_norm[l][None], dx1, g1, u1, W[("f1g", l)], W[("f1u", l)], W[("f1d", l)])
        gf1 = hosted(ffn_bwd_wgrad, [stage(mx, "xchg")], hh, dacc, g1, u1, dgt, dup)
        f1 = chains[("f1", l)] = _Chain(gf1, c_idx, j_idx)
        up_mx, up_f1 = mx, f1

    up_f1.after_swap(_run_comm(up_f1.swap()))
    up_f1.after_xchg(_run_comm(up_f1.xchg()))
    ops = [up_mx.share(), up_f1.share()]
    res = _split(_run_comm(_merge(*ops)), *ops)
    up_mx.after_share(res[0])
    up_f1.after_share(res[1])

    G = {k_: jnp.stack(v_) for k_, v_ in small.items()}
    G["nf"] = dnf
    small_shapes = [shp for _, shp in _SMALL]
    n_small = 1 + sum(math.prod(s) for s in small_shapes)
    rows = -(-n_small // 1024) * 8
    packed = _pack([loss] + [G[k_] for k_, _ in _SMALL], rows)
    summed = all_reduce_small(packed)
    loss_out, *small_sum = _unpack(summed, [()] + small_shapes)
    gs = dict(zip([k_ for k_, _ in _SMALL], small_sum))
    gs["cw"] = lax.dynamic_slice_in_dim(gs["cw"], chip * (CONV_C // N_CHIPS), CONV_C // N_CHIPS, axis=2)

    def big(group, idx, transpose):
        per_layer = [chains[(group, l)].final[idx] for l in range(L)]
        return jnp.stack([a.T if transpose else a for a in per_layer])

    grads = dict(ffn1_norm=gs["n1"], ffn1_w_gate=big("f1", 0, True), ffn1_w_up=big("f1", 1, True),
                 ffn1_w_down=big("f1", 2, False), mix_norm=gs["nm"], w_in=big("mx", 1, True), conv_w=gs["cw"],
                 conv_b=gs["cb"], conv_ln_g=gs["lg"], conv_ln_b=gs["lb"], attn_sinks=gs["sinks"],
                 w_out=big("mx", 0, False), ffn2_norm=gs["n2"], ffn2_w_gate=big("f2", 0, True),
                 ffn2_w_up=big("f2", 1, True), ffn2_w_down=big("f2", 2, False), final_norm=gs["nf"])

    weights = dict(ffn1_norm=ffn1_norm, ffn1_w_gate=ffn1_w_gate, ffn1_w_up=ffn1_w_up, ffn1_w_down=ffn1_w_down,
                   mix_norm=mix_norm, w_in=w_in, conv_w=conv_w, conv_b=conv_b, conv_ln_g=conv_ln_g,
                   conv_ln_b=conv_ln_b, attn_sinks=attn_sinks, w_out=w_out, ffn2_norm=ffn2_norm,
                   ffn2_w_gate=ffn2_w_gate, ffn2_w_up=ffn2_w_up, ffn2_w_down=ffn2_w_down, final_norm=final_norm)
    moms = dict(ffn1_norm=(m_ffn1_norm, v_ffn1_norm), ffn1_w_gate=(m_ffn1_w_gate, v_ffn1_w_gate),
                ffn1_w_up=(m_ffn1_w_up, v_ffn1_w_up), ffn1_w_down=(m_ffn1_w_down, v_ffn1_w_down),
                mix_norm=(m_mix_norm, v_mix_norm), w_in=(m_w_in, v_w_in), conv_w=(m_conv_w, v_conv_w),
                conv_b=(m_conv_b, v_conv_b), conv_ln_g=(m_conv_ln_g, v_conv_ln_g),
                conv_ln_b=(m_conv_ln_b, v_conv_ln_b), attn_sinks=(m_attn_sinks, v_attn_sinks),
                w_out=(m_w_out, v_w_out), ffn2_norm=(m_ffn2_norm, v_ffn2_norm),
                ffn2_w_gate=(m_ffn2_w_gate, v_ffn2_w_gate), ffn2_w_up=(m_ffn2_w_up, v_ffn2_w_up),
                ffn2_w_down=(m_ffn2_w_down, v_ffn2_w_down), final_norm=(m_final_norm, v_final_norm))
    names = list(weights)
    big_names = ("ffn1_w_gate", "ffn1_w_up", "ffn1_w_down", "w_in", "w_out", "ffn2_w_gate", "ffn2_w_up",
                 "ffn2_w_down")
    delta, new_m, new_v = {}, {}, {}
    for nme in big_names:
        shp = weights[nme].shape
        two = lambda a: a.reshape(shp[0] * shp[1], shp[2])
        d, nm_, nv_ = adamw(two(weights[nme]), two(grads[nme]), two(moms[nme][0]), two(moms[nme][1]))
        delta[nme], new_m[nme], new_v[nme] = d.reshape(shp), nm_.reshape(shp), nv_.reshape(shp)
    small_names = [nme for nme in names if nme not in big_names]
    s_shapes = [weights[nme].shape for nme in small_names]
    n_tot = sum(math.prod(s) for s in s_shapes)
    srows = -(-n_tot // 1024) * 8
    d, nm_, nv_ = adamw(_pack([weights[nme] for nme in small_names], srows),
                        _pack([grads[nme] for nme in small_names], srows),
                        _pack([moms[nme][0] for nme in small_names], srows),
                        _pack([moms[nme][1] for nme in small_names], srows))
    for nme, dd, mm, vv in zip(small_names, _unpack(d, s_shapes), _unpack(nm_, s_shapes), _unpack(nv_, s_shapes)):
        delta[nme], new_m[nme], new_v[nme] = dd, mm, vv

    return (loss_out, dx[None], *[grads[nme] for nme in names], *[delta[nme] for nme in names],
            *[new_m[nme] for nme in names], *[new_v[nme] for nme in names])
```

```python
import math

import jax
import jax.numpy as jnp
from jax import lax
from jax.experimental import pallas as pl
from jax.experimental.pallas import tpu as pltpu

F32 = jnp.float32
CDT = jnp.bfloat16
D_MODEL = 1024
D_FF = 2816
N_HEADS = 8
HEAD_DIM = 64
BLOCK = 128
CONV_K = 31
CONV_C = 512
ATT_W = 512
D_EXT = 2048
EPS = 1e-5
HALO = 32
FF_CHUNK = 256
NEG = float(jnp.finfo(jnp.float32).min)
VMEM_LIMIT = 56 * 1024 * 1024

ADAM_LR = 0.001
ADAM_B1 = 0.9
ADAM_B2 = 0.999
ADAM_EPS = 1e-08
ADAM_WD = 0.01
ADAM_STEP = 10

NT = (((1,), (1,)), ((), ()))
TN = (((0,), (0,)), ((), ()))


MESH = pl.DeviceIdType.MESH
ANY = pl.BlockSpec(memory_space=pl.ANY)
N_CHIPS = 4
N_DEV = 8


def _params(n_axes):
    return pltpu.CompilerParams(dimension_semantics=("arbitrary",) * n_axes, vmem_limit_bytes=VMEM_LIMIT)


class _Comm:
    def __init__(self, name, inputs, out_shape, sems, descs):
        self.name, self.inputs, self.out_shape, self.sems, self.descs = name, list(inputs), list(out_shape), list(sems), descs


def _merge(*ops):
    ops = [o for o in ops if o is not None]
    if len(ops) == 1:
        return ops[0]

    def descs(cins, couts, sems):
        out, i, o, s = [], 0, 0, 0
        for op in ops:
            ni, no, ns = len(op.inputs), len(op.out_shape), len(op.sems)
            out += op.descs(cins[i:i + ni], couts[o:o + no], sems[s:s + ns])
            i, o, s = i + ni, o + no, s + ns
        return out

    return _Comm("_".join(o.name for o in ops), sum((o.inputs for o in ops), []),
                 sum((o.out_shape for o in ops), []), sum((o.sems for o in ops), []), descs)


def _split(couts, *ops):
    res, o = [], 0
    for op in ops:
        res.append(couts[o:o + len(op.out_shape)])
        o += len(op.out_shape)
    return res


def _hosted(comm):
    if comm is None:
        return pl.pallas_call

    def make(body, *, name, grid, in_specs, out_specs, out_shape, compiler_params, scratch_shapes=()):
        single = not isinstance(out_shape, (list, tuple))
        o_specs = [out_specs] if single else list(out_specs)
        o_shape = [out_shape] if single else list(out_shape)
        n_in, n_out, n_sc = len(in_specs), len(o_specs), len(scratch_shapes)
        c_in, c_out = len(comm.inputs), len(comm.out_shape)

        def hosted(*refs):
            ins, cins = refs[:n_in], refs[n_in:n_in + c_in]
            o0 = n_in + c_in
            outs, couts = refs[o0:o0 + n_out], refs[o0 + n_out:o0 + n_out + c_out]
            s0 = o0 + n_out + c_out
            scr, sems = refs[s0:s0 + n_sc], refs[s0 + n_sc:]
            first = pl.program_id(0) == 0
            last = pl.program_id(0) == grid[0] - 1
            for ax in range(1, len(grid)):
                first = first & (pl.program_id(ax) == 0)
                last = last & (pl.program_id(ax) == grid[ax] - 1)

            @pl.when(first)
            def _():
                for d in comm.descs(cins, couts, sems):
                    d.start()

            body(*ins, *outs, *scr)

            @pl.when(last)
            def _():
                for d in comm.descs(cins, couts, sems):
                    d.wait()

        call = pl.pallas_call(
            hosted, name=f"{name}_{comm.name}", grid=grid,
            in_specs=list(in_specs) + [ANY] * c_in, out_specs=o_specs + [ANY] * c_out,
            out_shape=o_shape + comm.out_shape, scratch_shapes=list(scratch_shapes) + comm.sems,
            compiler_params=compiler_params)

        def run(*args):
            res = call(*args, *comm.inputs)
            return (res[0] if single else list(res[:n_out])), list(res[n_out:])

        return run

    return make


def _run_comm(comm):
    c_in = len(comm.inputs)

    def body(*refs):
        cins, couts, sems = refs[:c_in], refs[c_in:c_in + len(comm.out_shape)], refs[c_in + len(comm.out_shape):]
        ds = comm.descs(cins, couts, sems)
        for d in ds:
            d.start()
        for d in ds:
            d.wait()

    return list(pl.pallas_call(
        body, name=comm.name, in_specs=[ANY] * c_in, out_specs=[ANY] * len(comm.out_shape),
        out_shape=comm.out_shape, scratch_shapes=comm.sems)(*comm.inputs))


def _coords():
    return lax.axis_index("x"), lax.axis_index("y"), lax.axis_index("c")


def _other_chips(x, y):
    return [(1 - x, y), (x, 1 - y), (1 - x, 1 - y)]


def ag_op(shards):
    n = len(shards)

    def descs(cins, couts, sems):
        send, recv = sems
        x, y, c = _coords()
        j = 2 * x + y
        ds = []
        for a in range(n):
            for p, (px, py) in enumerate(_other_chips(x, y)):
                ds.append(pltpu.make_async_remote_copy(cins[a], couts[a].at[j], send.at[a, p], recv.at[a, p],
                                                       device_id=(px, py, c), device_id_type=MESH))
        return ds

    return _Comm("ag", shards, [jax.ShapeDtypeStruct((N_CHIPS,) + s.shape, s.dtype) for s in shards],
                 [pltpu.SemaphoreType.DMA((n, 3)), pltpu.SemaphoreType.DMA((n, 3))], descs)


def _own_slab(gathered, mine, idx):
    return lax.dynamic_update_slice_in_dim(gathered, mine[None], idx, axis=0)


def first_gather(shards):
    n = len(shards)
    halves = [s.reshape(2, s.shape[0] // 2, s.shape[1]) for s in shards]

    def body(*refs):
        ins, outs = refs[:n], refs[n:2 * n]
        send1, recv1, send2, recv2 = refs[2 * n:]
        x, y, c = _coords()
        j = 2 * x + y
        chips = _other_chips(x, y)
        ici = [pltpu.make_async_remote_copy(ins[a].at[c], outs[a].at[j, c], send1.at[a, p], recv1.at[a, p],
                                            device_id=(px, py, c), device_id_type=MESH)
               for a in range(n) for p, (px, py) in enumerate(chips)]
        for d in ici:
            d.start()
        for d in ici:
            d.wait()
        d2d = [pltpu.make_async_remote_copy(outs[a].at[2 * px + py, c], outs[a].at[2 * px + py, c],
                                            send2.at[a, p], recv2.at[a, p],
                                            device_id=(x, y, 1 - c), device_id_type=MESH)
               for a in range(n) for p, (px, py) in enumerate(chips)]
        for d in d2d:
            d.start()
        for d in d2d:
            d.wait()

    return list(pl.pallas_call(
        body, name="first_gather", in_specs=[ANY] * n, out_specs=[ANY] * n,
        out_shape=[jax.ShapeDtypeStruct((N_CHIPS,) + h.shape, h.dtype) for h in halves],
        scratch_shapes=[pltpu.SemaphoreType.DMA((n, 3))] * 4)(*halves))


def swap_op(grads):
    n = len(grads)

    def descs(cins, couts, sems):
        send, recv = sems
        x, y, c = _coords()
        return [pltpu.make_async_remote_copy(cins[a].at[:, 1 - c], couts[a], send.at[a], recv.at[a],
                                             device_id=(x, y, 1 - c), device_id_type=MESH) for a in range(n)]

    return _Comm("swap", grads, [jax.ShapeDtypeStruct(g.shape[:1] + g.shape[2:], g.dtype) for g in grads],
                 [pltpu.SemaphoreType.DMA((n,)), pltpu.SemaphoreType.DMA((n,))], descs)


def exchange_op(parts):
    n = len(parts)

    def descs(cins, couts, sems):
        send, recv = sems
        x, y, c = _coords()
        ds = []
        for a in range(n):
            for p, (px, py) in enumerate(_other_chips(x, y)):
                ds.append(pltpu.make_async_remote_copy(cins[a].at[2 * px + py], couts[a].at[p], send.at[a, p],
                                                       recv.at[a, p], device_id=(px, py, c), device_id_type=MESH))
        return ds

    return _Comm("xchg", parts, [jax.ShapeDtypeStruct((3,) + p.shape[1:], p.dtype) for p in parts],
                 [pltpu.SemaphoreType.DMA((n, 3)), pltpu.SemaphoreType.DMA((n, 3))], descs)


def share_op(totals):
    n = len(totals)

    def descs(cins, couts, sems):
        send, recv = sems
        x, y, c = _coords()
        return [pltpu.make_async_remote_copy(cins[a], couts[a].at[c], send.at[a], recv.at[a],
                                             device_id=(x, y, 1 - c), device_id_type=MESH) for a in range(n)]

    return _Comm("share", totals, [jax.ShapeDtypeStruct((2,) + t.shape, t.dtype) for t in totals],
                 [pltpu.SemaphoreType.DMA((n,)), pltpu.SemaphoreType.DMA((n,))], descs)


def _sigmoid(z):
    return 1.0 / (1.0 + jnp.exp(-z))


def _rms_parts(xf):
    r = lax.rsqrt(jnp.mean(xf * xf, axis=-1, keepdims=True) + EPS)
    return xf * r, r


def _rms_bwd(dh, xhat, r, g):
    dg = jnp.sum(dh * xhat, axis=0, keepdims=True)
    dxhat = dh * g
    dx = r * (dxhat - xhat * jnp.mean(dxhat * xhat, axis=-1, keepdims=True))
    return dx, dg


def _chunks(n, ck=FF_CHUNK):
    return [(c0, min(ck, n - c0)) for c0 in range(0, n, ck)]


def ffn_fwd(x, g, wgT, wuT, wd, *, tm=512, comm=None):
    S, D = x.shape
    F = wgT.shape[0]

    def body(x_ref, g_ref, wg_ref, wu_ref, wd_ref, o_ref, gate_ref, up_ref):
        xf = x_ref[...]
        xhat, _ = _rms_parts(xf)
        h = (xhat * g_ref[...]).astype(CDT)
        acc = jnp.zeros((tm, D), F32)
        for c0, cw_ in _chunks(F):
            sl = slice(c0, c0 + cw_)
            gt = lax.dot_general(h, wg_ref[sl, :], NT, preferred_element_type=F32)
            ut = lax.dot_general(h, wu_ref[sl, :], NT, preferred_element_type=F32)
            gate_ref[:, sl] = gt.astype(CDT)
            up_ref[:, sl] = ut.astype(CDT)
            a = (gt * _sigmoid(gt) * ut).astype(CDT)
            acc = acc + jnp.dot(a, wd_ref[sl, :], preferred_element_type=F32)
        o_ref[...] = xf + 0.5 * acc

    wspec = pl.BlockSpec((F, D), lambda i: (0, 0), pipeline_mode=pl.Buffered(1))
    return _hosted(comm)(
        body, name="ffn_fwd",
        grid=(S // tm,),
        in_specs=[pl.BlockSpec((tm, D), lambda i: (i, 0)), pl.BlockSpec((1, D), lambda i: (0, 0)),
                  wspec, wspec, wspec],
        out_specs=[pl.BlockSpec((tm, D), lambda i: (i, 0)),
                   pl.BlockSpec((tm, F), lambda i: (i, 0)),
                   pl.BlockSpec((tm, F), lambda i: (i, 0))],
        out_shape=[jax.ShapeDtypeStruct((S, D), F32),
                   jax.ShapeDtypeStruct((S, F), CDT),
                   jax.ShapeDtypeStruct((S, F), CDT)],
        compiler_params=_params(1),
    )(x, g, wgT, wuT, wd)


def ffn_bwd_dgrad(x, g, dy, gate, up, wgT, wuT, wd, *, tm=512, tf=1408, comm=None):
    S, D = x.shape
    F = wgT.shape[0]
    nf = F // tf

    def body(x_ref, g_ref, dy_ref, gate_ref, up_ref, wg_ref, wu_ref, wd_ref,
             dx_ref, dgate_ref, dup_ref, h_ref, dg_ref, dh_sc, dacc_sc):
        i = pl.program_id(0)
        j = pl.program_id(1)

        @pl.when(j == 0)
        def _():
            xhat, _ = _rms_parts(x_ref[...])
            h_ref[...] = (xhat * g_ref[...]).astype(CDT)
            dacc_sc[...] = (0.5 * dy_ref[...]).astype(CDT)
            dh_sc[...] = jnp.zeros_like(dh_sc)

        @pl.when((i == 0) & (j == 0))
        def _():
            dg_ref[...] = jnp.zeros_like(dg_ref)

        dacc = dacc_sc[...]
        dh = jnp.zeros((tm, D), F32)
        for c0, cw_ in _chunks(tf):
            sl = slice(c0, c0 + cw_)
            d_a = lax.dot_general(dacc, wd_ref[sl, :], NT, preferred_element_type=F32)
            gt = gate_ref[:, sl].astype(F32)
            ut = up_ref[:, sl].astype(F32)
            sg = _sigmoid(gt)
            d_up = (d_a * (gt * sg)).astype(CDT)
            d_gate = (d_a * ut * (sg * (1.0 + gt * (1.0 - sg)))).astype(CDT)
            dgate_ref[:, sl] = d_gate
            dup_ref[:, sl] = d_up
            dh = dh + (jnp.dot(d_gate, wg_ref[sl, :], preferred_element_type=F32)
                       + jnp.dot(d_up, wu_ref[sl, :], preferred_element_type=F32))
        dh_sc[...] += dh

        @pl.when(j == nf - 1)
        def _():
            xhat, r = _rms_parts(x_ref[...])
            dx, dg = _rms_bwd(dh_sc[...], xhat, r, g_ref[...])
            dx_ref[...] = dy_ref[...] + dx
            dg_ref[...] += dg

    return _hosted(comm)(
        body, name="ffn_bwd_dgrad",
        grid=(S // tm, nf),
        in_specs=[pl.BlockSpec((tm, D), lambda i, j: (i, 0)),
                  pl.BlockSpec((1, D), lambda i, j: (0, 0)),
                  pl.BlockSpec((tm, D), lambda i, j: (i, 0)),
                  pl.BlockSpec((tm, tf), lambda i, j: (i, j)),
                  pl.BlockSpec((tm, tf), lambda i, j: (i, j)),
                  pl.BlockSpec((tf, D), lambda i, j: (j, 0)),
                  pl.BlockSpec((tf, D), lambda i, j: (j, 0)),
                  pl.BlockSpec((tf, D), lambda i, j: (j, 0))],
        out_specs=[pl.BlockSpec((tm, D), lambda i, j: (i, 0)),
                   pl.BlockSpec((tm, tf), lambda i, j: (i, j)),
                   pl.BlockSpec((tm, tf), lambda i, j: (i, j)),
                   pl.BlockSpec((tm, D), lambda i, j: (i, 0)),
                   pl.BlockSpec((1, D), lambda i, j: (0, 0))],
        out_shape=[jax.ShapeDtypeStruct((S, D), F32),
                   jax.ShapeDtypeStruct((S, F), CDT),
                   jax.ShapeDtypeStruct((S, F), CDT),
                   jax.ShapeDtypeStruct((S, D), CDT),
                   jax.ShapeDtypeStruct((1, D), F32)],
        scratch_shapes=[pltpu.VMEM((tm, D), F32), pltpu.VMEM((tm, D), CDT)],
        compiler_params=_params(2),
    )(x, g, dy, gate, up, wgT, wuT, wd)


def ffn_wgrad_down(dy, gate, up, *, tk=512, comm=None):
    S, D = dy.shape
    F = gate.shape[1]
    tk = min(tk, S)

    def body(dy_ref, gate_ref, up_ref, dwd_ref):
        @pl.when(pl.program_id(0) == 0)
        def _():
            dwd_ref[...] = jnp.zeros_like(dwd_ref)

        dacc = (0.5 * dy_ref[...]).astype(CDT)
        for c0, cw_ in _chunks(F):
            sl = slice(c0, c0 + cw_)
            gt = gate_ref[:, sl].astype(F32)
            a = (gt * _sigmoid(gt) * up_ref[:, sl].astype(F32)).astype(CDT)
            dwd_ref[sl, :] += lax.dot_general(a, dacc, TN, preferred_element_type=F32)

    act = pl.BlockSpec((tk, F), lambda k: (k, 0))
    return _hosted(comm)(
        body, name="ffn_wgrad_down",
        grid=(S // tk,),
        in_specs=[pl.BlockSpec((tk, D), lambda k: (k, 0)), act, act],
        out_specs=pl.BlockSpec((F, D), lambda k: (0, 0)),
        out_shape=jax.ShapeDtypeStruct((F, D), F32),
        compiler_params=_params(1),
    )(dy, gate, up)


def ffn_wgrad_gate_up(h, dgate, dup, *, tk=1024, tf=1408, comm=None):
    S, D = h.shape
    F = dgate.shape[1]
    tk = min(tk, S)

    def body(h_ref, dgate_ref, dup_ref, dwg_ref, dwu_ref):
        k = pl.program_id(1)

        @pl.when(k == 0)
        def _():
            dwg_ref[...] = jnp.zeros_like(dwg_ref)
            dwu_ref[...] = jnp.zeros_like(dwu_ref)

        hh = h_ref[...]
        for c0, cw_ in _chunks(tf):
            sl = slice(c0, c0 + cw_)
            dwg_ref[sl, :] += lax.dot_general(dgate_ref[:, sl], hh, TN, preferred_element_type=F32)
            dwu_ref[sl, :] += lax.dot_general(dup_ref[:, sl], hh, TN, preferred_element_type=F32)

    tok = pl.BlockSpec((tk, D), lambda j, k: (k, 0))
    act = pl.BlockSpec((tk, tf), lambda j, k: (k, j))
    out = pl.BlockSpec((tf, D), lambda j, k: (j, 0))
    return _hosted(comm)(
        body, name="ffn_wgrad_gate_up",
        grid=(F // tf, S // tk),
        in_specs=[tok, act, act],
        out_specs=[out, out],
        out_shape=[jax.ShapeDtypeStruct((F, D), F32)] * 2,
        compiler_params=_params(2),
    )(h, dgate, dup)


def loss_head(x, g, target, *, tm=512):
    S, D = x.shape

    def body(x_ref, g_ref, t_ref, loss_ref, dx_ref, dg_ref):
        @pl.when(pl.program_id(0) == 0)
        def _():
            loss_ref[...] = jnp.zeros_like(loss_ref)
            dg_ref[...] = jnp.zeros_like(dg_ref)

        xhat, r = _rms_parts(x_ref[...])
        gg = g_ref[...]
        err = xhat * gg - t_ref[...]
        loss_ref[...] += 0.5 * jnp.sum(jnp.mean(err * err, axis=-1, keepdims=True), axis=0, keepdims=True)
        dx, dg = _rms_bwd(err * (1.0 / D), xhat, r, gg)
        dx_ref[...] = dx
        dg_ref[...] += dg

    row = pl.BlockSpec((tm, D), lambda i: (i, 0))
    vec = pl.BlockSpec((1, D), lambda i: (0, 0))
    return pl.pallas_call(
        body, name="loss_head",
        grid=(S // tm,),
        in_specs=[row, vec, row],
        out_specs=[pl.BlockSpec((1, 1), lambda i: (0, 0)), row, vec],
        out_shape=[jax.ShapeDtypeStruct((1, 1), F32), jax.ShapeDtypeStruct((S, D), F32),
                   jax.ShapeDtypeStruct((1, D), F32)],
        compiler_params=_params(1),
    )(x, g, target)


def _rope_apply(t, cs, sn):
    lane = lax.broadcasted_iota(jnp.int32, t.shape, 1)
    first = (lane % HEAD_DIM) < (HEAD_DIM // 2)
    rot = jnp.where(first, pltpu.roll(t, 128 - HEAD_DIM // 2, 1), pltpu.roll(t, HEAD_DIM // 2, 1))
    return t * cs + rot * sn


def _rope_transpose(d, cs, sn):
    lane = lax.broadcasted_iota(jnp.int32, d.shape, 1)
    first = (lane % HEAD_DIM) < (HEAD_DIM // 2)
    ds = d * sn
    rot = jnp.where(first, pltpu.roll(ds, 128 - HEAD_DIM // 2, 1), pltpu.roll(ds, HEAD_DIM // 2, 1))
    return d * cs + rot


def inproj_fwd(x, g, wextT, cs, sn, *, tm=512, comm=None):
    S, D = x.shape
    scale = HEAD_DIM ** -0.5

    def body(x_ref, g_ref, w_ref, cs_ref, sn_ref, q_ref, k_ref, v_ref, u_ref):
        xhat, _ = _rms_parts(x_ref[...])
        h = (xhat * g_ref[...]).astype(CDT)
        p = lax.dot_general(h, w_ref[...], NT, preferred_element_type=F32)
        c, s = cs_ref[...], sn_ref[...]
        for b in range(4):
            q_ref[:, 128 * b:128 * (b + 1)] = (_rope_apply(p[:, 128 * b:128 * (b + 1)], c, s) * scale).astype(CDT)
        for b in range(2):
            k_ref[:, 128 * b:128 * (b + 1)] = _rope_apply(p[:, 512 + 128 * b:512 + 128 * (b + 1)], c, s).astype(CDT)
        v_ref[...] = p[:, 768:1024].astype(CDT)
        u_ref[...] = p[:, 1024:2048]

    def row(w):
        return pl.BlockSpec((tm, w), lambda i: (i, 0))

    return _hosted(comm)(
        body, name="inproj_fwd",
        grid=(S // tm,),
        in_specs=[row(D), pl.BlockSpec((1, D), lambda i: (0, 0)),
                  pl.BlockSpec((D_EXT, D), lambda i: (0, 0)), row(128), row(128)],
        out_specs=[row(512), row(256), row(256), row(1024)],
        out_shape=[jax.ShapeDtypeStruct((S, 512), CDT), jax.ShapeDtypeStruct((S, 256), CDT),
                   jax.ShapeDtypeStruct((S, 256), CDT), jax.ShapeDtypeStruct((S, 1024), F32)],
        compiler_params=_params(1),
    )(x, g, wextT, cs, sn)


def _stack_heads(p0, p1):
    lane = lax.broadcasted_iota(jnp.int32, p0.shape, 1)
    lo = lane < HEAD_DIM
    z = jnp.zeros_like(p0)
    return jnp.concatenate([jnp.where(lo, p0, z), jnp.where(lo, z, p0),
                            jnp.where(lo, p1, z), jnp.where(lo, z, p1)], axis=0)


def _unstack_heads(o):
    lane = lax.broadcasted_iota(jnp.int32, (BLOCK, 128), 1)
    lo = lane < HEAD_DIM
    return (jnp.where(lo, o[0:128], o[128:256]), jnp.where(lo, o[256:384], o[384:512]))


def _band_mask_qk(n):
    i = lax.broadcasted_iota(jnp.int32, (4 * BLOCK, 2 * BLOCK), 0) % BLOCK
    c = lax.broadcasted_iota(jnp.int32, (4 * BLOCK, 2 * BLOCK), 1)
    return (c > i) & (c <= i + BLOCK) & ((n > 0) | (c >= BLOCK))


def _band_mask_kq(n):
    c = lax.broadcasted_iota(jnp.int32, (2 * BLOCK, 4 * BLOCK), 0)
    i = lax.broadcasted_iota(jnp.int32, (2 * BLOCK, 4 * BLOCK), 1) % BLOCK
    return (c > i) & (c <= i + BLOCK) & ((n > 0) | (c >= BLOCK))


def attn_fwd(q, k, v, sink_col, *, nb=2, comm=None):
    S = q.shape[0]
    tq = nb * BLOCK

    def body(q_ref, k_ref, v_ref, sink_ref, o_ref):
        t = pl.program_id(0)
        for b in range(nb):
            n = t * nb + b
            prev = pl.multiple_of(jnp.maximum(n - 1, 0) * BLOCK, BLOCK)
            cur = pl.multiple_of(n * BLOCK, BLOCK)
            rows = slice(b * BLOCK, (b + 1) * BLOCK)
            mask = _band_mask_qk(n)
            for gidx in range(2):
                lanes = slice(128 * gidx, 128 * (gidx + 1))
                qs = _stack_heads(q_ref[rows, 256 * gidx:256 * gidx + 128],
                                  q_ref[rows, 256 * gidx + 128:256 * gidx + 256])
                kb = jnp.concatenate([k_ref[pl.ds(prev, BLOCK), lanes], k_ref[pl.ds(cur, BLOCK), lanes]], axis=0)
                vb = jnp.concatenate([v_ref[pl.ds(prev, BLOCK), lanes], v_ref[pl.ds(cur, BLOCK), lanes]], axis=0)
                s = lax.dot_general(qs, kb, NT, preferred_element_type=F32)
                s = jnp.where(mask, s, NEG)
                sink = sink_ref[gidx]
                m = jnp.maximum(jnp.max(s, axis=1, keepdims=True), sink)
                p = jnp.exp(s - m)
                den = jnp.sum(p, axis=1, keepdims=True) + jnp.exp(sink - m)
                o = jnp.dot(p.astype(CDT), vb, preferred_element_type=F32) / den
                o0, o1 = _unstack_heads(o)
                o_ref[rows, 256 * gidx:256 * gidx + 128] = o0.astype(CDT)
                o_ref[rows, 256 * gidx + 128:256 * gidx + 256] = o1.astype(CDT)

    return _hosted(comm)(
        body, name="attn_fwd",
        grid=(S // tq,),
        in_specs=[pl.BlockSpec((tq, 512), lambda t: (t, 0)),
                  pl.BlockSpec((S, 256), lambda t: (0, 0)),
                  pl.BlockSpec((S, 256), lambda t: (0, 0)),
                  pl.BlockSpec((2, 4 * BLOCK, 1), lambda t: (0, 0, 0))],
        out_specs=pl.BlockSpec((tq, 512), lambda t: (t, 0)),
        out_shape=jax.ShapeDtypeStruct((S, 512), CDT),
        compiler_params=_params(1),
    )(q, k, v, sink_col)


def attn_bwd(q, k, v, do, sink_row, *, nb=2, comm=None):
    S = q.shape[0]
    tq = nb * BLOCK
    scale = HEAD_DIM ** -0.5

    def body(q_ref, k_ref, v_ref, do_ref, sink_ref, dq_ref, dk_ref, dv_ref, dsink_ref):
        t = pl.program_id(0)

        @pl.when(t == 0)
        def _():
            dk_ref[...] = jnp.zeros_like(dk_ref)
            dv_ref[...] = jnp.zeros_like(dv_ref)
            dsink_ref[...] = jnp.zeros_like(dsink_ref)

        for b in range(nb):
            n = t * nb + b
            prev = pl.multiple_of(jnp.maximum(n - 1, 0) * BLOCK, BLOCK)
            cur = pl.multiple_of(n * BLOCK, BLOCK)
            rows = slice(b * BLOCK, (b + 1) * BLOCK)
            mask = _band_mask_kq(n)
            for gidx in range(2):
                lanes = slice(128 * gidx, 128 * (gidx + 1))
                qs = _stack_heads(q_ref[rows, 256 * gidx:256 * gidx + 128],
                                  q_ref[rows, 256 * gidx + 128:256 * gidx + 256])
                dos = _stack_heads(do_ref[rows, 256 * gidx:256 * gidx + 128],
                                   do_ref[rows, 256 * gidx + 128:256 * gidx + 256])
                kb = jnp.concatenate([k_ref[pl.ds(prev, BLOCK), lanes], k_ref[pl.ds(cur, BLOCK), lanes]], axis=0)
                vb = jnp.concatenate([v_ref[pl.ds(prev, BLOCK), lanes], v_ref[pl.ds(cur, BLOCK), lanes]], axis=0)
                st = lax.dot_general(kb, qs, NT, preferred_element_type=F32)
                st = jnp.where(mask, st, NEG)
                sink = sink_ref[gidx]
                m = jnp.maximum(jnp.max(st, axis=0, keepdims=True), sink)
                e = jnp.exp(st - m)
                es = jnp.exp(sink - m)
                inv = 1.0 / (jnp.sum(e, axis=0, keepdims=True) + es)
                pt = e * inv
                dpt = lax.dot_general(vb, dos, NT, preferred_element_type=F32)
                delta = jnp.sum(pt * dpt, axis=0, keepdims=True)
                dst = (pt * (dpt - delta)).astype(CDT)
                dsink_ref[gidx] += -(es * inv) * delta
                dvb = jnp.dot(pt.astype(CDT), dos, preferred_element_type=F32)
                dkb = jnp.dot(dst, qs, preferred_element_type=F32)
                dqs = lax.dot_general(dst, kb, TN, preferred_element_type=F32) * scale
                dq0, dq1 = _unstack_heads(dqs)
                dq_ref[rows, 256 * gidx:256 * gidx + 128] = dq0
                dq_ref[rows, 256 * gidx + 128:256 * gidx + 256] = dq1
                dk_ref[pl.ds(prev, BLOCK), lanes] += dkb[0:BLOCK]
                dk_ref[pl.ds(cur, BLOCK), lanes] += dkb[BLOCK:2 * BLOCK]
                dv_ref[pl.ds(prev, BLOCK), lanes] += dvb[0:BLOCK]
                dv_ref[pl.ds(cur, BLOCK), lanes] += dvb[BLOCK:2 * BLOCK]

    full = pl.BlockSpec((S, 256), lambda t: (0, 0))
    tile = pl.BlockSpec((tq, 512), lambda t: (t, 0))
    srow = pl.BlockSpec((2, 1, 4 * BLOCK), lambda t: (0, 0, 0))
    return _hosted(comm)(
        body, name="attn_bwd",
        grid=(S // tq,),
        in_specs=[tile, full, full, tile, srow],
        out_specs=[tile, full, full, srow],
        out_shape=[jax.ShapeDtypeStruct((S, 512), F32), jax.ShapeDtypeStruct((S, 256), F32),
                   jax.ShapeDtypeStruct((S, 256), F32), jax.ShapeDtypeStruct((2, 1, 4 * BLOCK), F32)],
        compiler_params=_params(1),
    )(q, k, v, do, sink_row)


def _glu(u):
    a = u[:, 0:CONV_C]
    gt = u[:, CONV_C:2 * CONV_C]
    sg = _sigmoid(gt)
    return a, sg, a * sg


CONV_CHUNK = 64


def conv_fwd(u, cw, cb, lg, lb, *, tm=512, comm=None):
    S = u.shape[0]
    nh = tm // HALO

    def body(u_ref, uh_ref, cw_ref, cb_ref, lg_ref, lb_ref, o_ref, y_ref, hbuf):
        t = pl.program_id(0)
        _, _, hg = _glu(u_ref[...])
        _, _, hh = _glu(uh_ref[...])
        hbuf[0:HALO, :] = jnp.where(t > 0, hh, jnp.zeros_like(hh))
        hbuf[HALO:HALO + tm, :] = hg
        off = HALO - (CONV_K - 1)
        for c0 in range(0, tm, CONV_CHUNK):
            acc = jnp.zeros((CONV_CHUNK, CONV_C), F32) + cb_ref[...]
            for j in range(CONV_K):
                acc = acc + cw_ref[j:j + 1, :] * hbuf[c0 + off + j:c0 + off + j + CONV_CHUNK, :]
            y_ref[c0:c0 + CONV_CHUNK, :] = acc
        y = y_ref[...]
        yc = y - jnp.mean(y, axis=-1, keepdims=True)
        r = lax.rsqrt(jnp.mean(yc * yc, axis=-1, keepdims=True) + EPS)
        z = yc * r * lg_ref[...] + lb_ref[...]
        o_ref[...] = (z * _sigmoid(z)).astype(CDT)

    vec = pl.BlockSpec((1, CONV_C), lambda t: (0, 0))
    return _hosted(comm)(
        body, name="conv_fwd",
        grid=(S // tm,),
        in_specs=[pl.BlockSpec((tm, 2 * CONV_C), lambda t: (t, 0)),
                  pl.BlockSpec((HALO, 2 * CONV_C), lambda t: (jnp.maximum(t * nh - 1, 0), 0)),
                  pl.BlockSpec((CONV_K, CONV_C), lambda t: (0, 0)), vec, vec, vec],
        out_specs=[pl.BlockSpec((tm, CONV_C), lambda t: (t, 0)), pl.BlockSpec((tm, CONV_C), lambda t: (t, 0))],
        out_shape=[jax.ShapeDtypeStruct((S, CONV_C), CDT), jax.ShapeDtypeStruct((S, CONV_C), F32)],
        scratch_shapes=[pltpu.VMEM((HALO + tm, CONV_C), F32)],
        compiler_params=_params(1),
    )(u, u, cw, cb, lg, lb)


def conv_bwd(dc, u, y, cw, lg, lb, *, tm=512, comm=None):
    S = u.shape[0]
    nh = tm // HALO
    nt = S // tm

    def ln_bwd(dcv, yv, lgv, lbv):
        yc = yv - jnp.mean(yv, axis=-1, keepdims=True)
        r = lax.rsqrt(jnp.mean(yc * yc, axis=-1, keepdims=True) + EPS)
        yhat = yc * r
        z = yhat * lgv + lbv
        sg = _sigmoid(z)
        dz = dcv * (sg * (1.0 + z * (1.0 - sg)))
        dyhat = dz * lgv
        dy = r * (dyhat - jnp.mean(dyhat, axis=-1, keepdims=True)
                  - yhat * jnp.mean(dyhat * yhat, axis=-1, keepdims=True))
        return dy, dz, yhat

    def body(dc_ref, dcn_ref, u_ref, uh_ref, y_ref, yn_ref, cw_ref, lg_ref, lb_ref,
             du_ref, dcw_ref, dcb_ref, dlg_ref, dlb_ref, hbuf, dybuf, dhg_sc, dw_sc):
        t = pl.program_id(0)

        @pl.when(t == 0)
        def _():
            dw_sc[...] = jnp.zeros_like(dw_sc)
            dcb_ref[...] = jnp.zeros_like(dcb_ref)
            dlg_ref[...] = jnp.zeros_like(dlg_ref)
            dlb_ref[...] = jnp.zeros_like(dlb_ref)

        lgv, lbv = lg_ref[...], lb_ref[...]
        dy, dz, yhat = ln_bwd(dc_ref[...].astype(F32), y_ref[...], lgv, lbv)
        dyn, _, _ = ln_bwd(dcn_ref[...].astype(F32), yn_ref[...], lgv, lbv)
        dlb_ref[...] += jnp.sum(dz, axis=0, keepdims=True)
        dlg_ref[...] += jnp.sum(dz * yhat, axis=0, keepdims=True)
        dcb_ref[...] += jnp.sum(dy, axis=0, keepdims=True)
        dybuf[0:tm, :] = dy
        dybuf[tm:tm + HALO, :] = jnp.where(t < nt - 1, dyn, jnp.zeros_like(dyn))

        a, sg, hg = _glu(u_ref[...])
        _, _, hh = _glu(uh_ref[...])
        hbuf[0:HALO, :] = jnp.where(t > 0, hh, jnp.zeros_like(hh))
        hbuf[HALO:HALO + tm, :] = hg

        off = HALO - (CONV_K - 1)
        for c0 in range(0, tm, CONV_CHUNK):
            acc = jnp.zeros((CONV_CHUNK, CONV_C), F32)
            dyc = dybuf[c0:c0 + CONV_CHUNK, :]
            for j in range(CONV_K):
                acc = acc + cw_ref[j:j + 1, :] * dybuf[c0 + (CONV_K - 1) - j:c0 + (CONV_K - 1) - j + CONV_CHUNK, :]
                prod = dyc * hbuf[c0 + off + j:c0 + off + j + CONV_CHUNK, :]
                dw_sc[j] += jnp.sum(prod.reshape(CONV_CHUNK // 8, 8, CONV_C), axis=0)
            dhg_sc[c0:c0 + CONV_CHUNK, :] = acc

        dhg = dhg_sc[...]
        du_ref[:, 0:CONV_C] = dhg * sg
        du_ref[:, CONV_C:2 * CONV_C] = dhg * a * sg * (1.0 - sg)

        @pl.when(t == nt - 1)
        def _():
            dcw_ref[...] = jnp.sum(dw_sc[...], axis=1)

    vec = pl.BlockSpec((1, CONV_C), lambda t: (0, 0))
    tile = pl.BlockSpec((tm, CONV_C), lambda t: (t, 0))
    nxt = pl.BlockSpec((HALO, CONV_C), lambda t: (jnp.minimum((t + 1) * nh, S // HALO - 1), 0))
    return _hosted(comm)(
        body, name="conv_bwd",
        grid=(nt,),
        in_specs=[tile, nxt,
                  pl.BlockSpec((tm, 2 * CONV_C), lambda t: (t, 0)),
                  pl.BlockSpec((HALO, 2 * CONV_C), lambda t: (jnp.maximum(t * nh - 1, 0), 0)),
                  tile, nxt,
                  pl.BlockSpec((CONV_K, CONV_C), lambda t: (0, 0)), vec, vec],
        out_specs=[pl.BlockSpec((tm, 2 * CONV_C), lambda t: (t, 0)),
                   pl.BlockSpec((CONV_K, CONV_C), lambda t: (0, 0)), vec, vec, vec],
        out_shape=[jax.ShapeDtypeStruct((S, 2 * CONV_C), F32), jax.ShapeDtypeStruct((CONV_K, CONV_C), F32),
                   jax.ShapeDtypeStruct((1, CONV_C), F32), jax.ShapeDtypeStruct((1, CONV_C), F32),
                   jax.ShapeDtypeStruct((1, CONV_C), F32)],
        scratch_shapes=[pltpu.VMEM((HALO + tm, CONV_C), F32), pltpu.VMEM((tm + HALO, CONV_C), F32),
                        pltpu.VMEM((tm, CONV_C), F32), pltpu.VMEM((CONV_K, 8, CONV_C), F32)],
        compiler_params=_params(1),
    )(dc, dc, u, u, y, y, cw, lg, lb)


def outproj_fwd(x, ao, co, wout, *, tm=512):
    S, D = x.shape

    def body(x_ref, a_ref, c_ref, w_ref, o_ref):
        o_ref[...] = (x_ref[...]
                      + jnp.dot(a_ref[...], w_ref[0:ATT_W, :], preferred_element_type=F32)
                      + jnp.dot(c_ref[...], w_ref[ATT_W:ATT_W + CONV_C, :], preferred_element_type=F32))

    return pl.pallas_call(
        body, name="outproj_fwd",
        grid=(S // tm,),
        in_specs=[pl.BlockSpec((tm, D), lambda i: (i, 0)), pl.BlockSpec((tm, ATT_W), lambda i: (i, 0)),
                  pl.BlockSpec((tm, CONV_C), lambda i: (i, 0)), pl.BlockSpec((D, D), lambda i: (0, 0))],
        out_specs=pl.BlockSpec((tm, D), lambda i: (i, 0)),
        out_shape=jax.ShapeDtypeStruct((S, D), F32),
        compiler_params=_params(1),
    )(x, ao, co, wout)


def outproj_bwd(dx, ao, co, wout, *, tm=512, comm=None):
    S, D = dx.shape

    def body(dx_ref, a_ref, c_ref, w_ref, da_ref, dc_ref, dw_ref):
        @pl.when(pl.program_id(0) == 0)
        def _():
            dw_ref[...] = jnp.zeros_like(dw_ref)

        dxb = dx_ref[...].astype(CDT)
        da_ref[...] = lax.dot_general(dxb, w_ref[0:ATT_W, :], NT, preferred_element_type=F32).astype(CDT)
        dc_ref[...] = lax.dot_general(dxb, w_ref[ATT_W:ATT_W + CONV_C, :], NT, preferred_element_type=F32)
        dw_ref[0:ATT_W, :] += lax.dot_general(a_ref[...], dxb, TN, preferred_element_type=F32)
        dw_ref[ATT_W:ATT_W + CONV_C, :] += lax.dot_general(c_ref[...], dxb, TN, preferred_element_type=F32)

    return _hosted(comm)(
        body, name="outproj_bwd",
        grid=(S // tm,),
        in_specs=[pl.BlockSpec((tm, D), lambda i: (i, 0)), pl.BlockSpec((tm, ATT_W), lambda i: (i, 0)),
                  pl.BlockSpec((tm, CONV_C), lambda i: (i, 0)), pl.BlockSpec((D, D), lambda i: (0, 0))],
        out_specs=[pl.BlockSpec((tm, ATT_W), lambda i: (i, 0)), pl.BlockSpec((tm, CONV_C), lambda i: (i, 0)),
                   pl.BlockSpec((D, D), lambda i: (0, 0))],
        out_shape=[jax.ShapeDtypeStruct((S, ATT_W), CDT), jax.ShapeDtypeStruct((S, CONV_C), F32),
                   jax.ShapeDtypeStruct((D, D), F32)],
        compiler_params=_params(1),
    )(dx, ao, co, wout)


def inproj_bwd(x, g, dres, dq, dk, dv, du, wextT, cs, sn, *, tm=256):
    S, D = x.shape

    def body(x_ref, g_ref, dres_ref, dq_ref, dk_ref, dv_ref, du_ref, w_ref, cs_ref, sn_ref,
             dx_ref, dw_ref, dg_ref, dp_sc):
        @pl.when(pl.program_id(0) == 0)
        def _():
            dw_ref[...] = jnp.zeros_like(dw_ref)
            dg_ref[...] = jnp.zeros_like(dg_ref)

        c, s = cs_ref[...], sn_ref[...]
        for b in range(4):
            dp_sc[:, 128 * b:128 * (b + 1)] = _rope_transpose(dq_ref[:, 128 * b:128 * (b + 1)], c, s).astype(CDT)
        for b in range(2):
            dp_sc[:, 512 + 128 * b:512 + 128 * (b + 1)] = _rope_transpose(
                dk_ref[:, 128 * b:128 * (b + 1)], c, s).astype(CDT)
        dp_sc[:, 768:1024] = dv_ref[...].astype(CDT)
        dp_sc[:, 1024:2048] = du_ref[...].astype(CDT)
        dp = dp_sc[...]
        xhat, r = _rms_parts(x_ref[...])
        gg = g_ref[...]
        h = (xhat * gg).astype(CDT)
        dh = jnp.dot(dp, w_ref[...], preferred_element_type=F32)
        dw_ref[...] += lax.dot_general(dp, h, TN, preferred_element_type=F32)
        dx, dg = _rms_bwd(dh, xhat, r, gg)
        dx_ref[...] = dres_ref[...] + dx
        dg_ref[...] += dg

    def row(w):
        return pl.BlockSpec((tm, w), lambda i: (i, 0))

    return pl.pallas_call(
        body, name="inproj_bwd",
        grid=(S // tm,),
        in_specs=[row(D), pl.BlockSpec((1, D), lambda i: (0, 0)), row(D), row(512), row(256), row(256),
                  row(1024), pl.BlockSpec((D_EXT, D), lambda i: (0, 0)), row(128), row(128)],
        out_specs=[row(D), pl.BlockSpec((D_EXT, D), lambda i: (0, 0)), pl.BlockSpec((1, D), lambda i: (0, 0))],
        out_shape=[jax.ShapeDtypeStruct((S, D), F32), jax.ShapeDtypeStruct((D_EXT, D), F32),
                   jax.ShapeDtypeStruct((1, D), F32)],
        scratch_shapes=[pltpu.VMEM((tm, D_EXT), CDT)],
        compiler_params=_params(1),
    )(x, g, dres, dq, dk, dv, du, wextT, cs, sn)


def _rope_tables(positions):
    inv_freq = 1.0 / (10000.0 ** (jnp.arange(0, HEAD_DIM, 2, dtype=F32) / HEAD_DIM))
    ang = positions.astype(F32).reshape(-1, 1) * inv_freq
    cos, sin = jnp.cos(ang), jnp.sin(ang)
    cs = jnp.tile(jnp.concatenate([cos, cos], axis=-1), (1, 2))
    sn = jnp.tile(jnp.concatenate([-sin, sin], axis=-1), (1, 2))
    return cs, sn


def _widen_w_in(w):
    q, u = w[0:512], w[768:1792]
    parts = [q]
    for base in (512, 576, 640, 704):
        parts += [w[base:base + 64], w[base:base + 64]]
    return jnp.concatenate(parts + [u], axis=0)


def _fold_w_in(d):
    parts = [d[0:512]]
    for base in (512, 640, 768, 896):
        parts.append(d[base:base + 64] + d[base + 64:base + 128])
    return jnp.concatenate(parts + [d[1024:2048]], axis=0)


def add_half(g5, r1, c_idx):
    _, _, r, D = g5.shape

    def body(c_ref, g_ref, r_ref, o_ref):
        o_ref[...] = (g_ref[...] + r_ref[...]).astype(CDT)

    return pl.pallas_call(
        body, name="add_half",
        grid_spec=pltpu.PrefetchScalarGridSpec(
            num_scalar_prefetch=1, grid=(N_CHIPS,),
            in_specs=[pl.BlockSpec((None, None, r, D), lambda s, cr: (s, cr[0], 0, 0)),
                      pl.BlockSpec((None, r, D), lambda s, cr: (s, 0, 0))],
            out_specs=pl.BlockSpec((None, r, D), lambda s, cr: (s, 0, 0))),
        out_shape=jax.ShapeDtypeStruct((N_CHIPS, r, D), CDT),
        compiler_params=_params(1),
    )(c_idx, g5, r1)


def sum_partials(part, recv3, j_idx):
    _, r, D = part.shape
    tr_ = r // 2

    def body(j_ref, p_ref, r_ref, o_ref):
        o_ref[...] = ((p_ref[...].astype(F32) + r_ref[0].astype(F32)) + r_ref[1].astype(F32)) + r_ref[2].astype(F32)

    return pl.pallas_call(
        body, name="sum_partials",
        grid_spec=pltpu.PrefetchScalarGridSpec(
            num_scalar_prefetch=1, grid=(2,),
            in_specs=[pl.BlockSpec((None, tr_, D), lambda i, jr: (jr[0], i, 0)),
                      pl.BlockSpec((3, tr_, D), lambda i, jr: (0, i, 0))],
            out_specs=pl.BlockSpec((tr_, D), lambda i, jr: (i, 0))),
        out_shape=jax.ShapeDtypeStruct((r, D), F32),
        compiler_params=_params(1),
    )(j_idx, part, recv3)


class _Chain:
    STAGES = ("swap", "xchg", "share")

    def __init__(self, grads, c_idx, j_idx):
        self.c_idx, self.j_idx = c_idx, j_idx
        self.g5 = [g.reshape(N_CHIPS, 2, g.shape[0] // (2 * N_CHIPS), g.shape[1]) for g in grads]
        self.stage_no = 0

    @property
    def done(self):
        return self.stage_no == len(self.STAGES)

    def next_stage(self):
        name = self.STAGES[self.stage_no]

        def callback(res):
            getattr(self, "after_" + name)(res)
            self.stage_no += 1

        return getattr(self, name)(), callback

    def swap(self):
        return swap_op(self.g5)

    def after_swap(self, recv):
        self.parts = [add_half(g, r, self.c_idx) for g, r in zip(self.g5, recv)]

    def xchg(self):
        return exchange_op(self.parts)

    def after_xchg(self, recv):
        self.totals = [sum_partials(p, r, self.j_idx) for p, r in zip(self.parts, recv)]

    def share(self):
        return share_op(self.totals)

    def after_share(self, recv):
        both = [_own_slab(h, t, self.c_idx[0]) for h, t in zip(recv, self.totals)]
        self.final = [h.reshape(2 * h.shape[1], h.shape[2]) for h in both]


def all_reduce_small(vec):
    R = vec.shape[0]

    def body(v_ref, o_ref, buf, send, recv):
        x, y, c = _coords()
        me = 4 * x + 2 * y + c
        buf[me] = v_ref[...]
        cps = []
        for m in range(1, N_DEV):
            dx, dy, dc = (m >> 2) & 1, (m >> 1) & 1, m & 1
            cp = pltpu.make_async_remote_copy(v_ref, buf.at[me], send.at[m - 1], recv.at[m - 1],
                                              device_id=((x + dx) % 2, (y + dy) % 2, (c + dc) % 2),
                                              device_id_type=MESH)
            cp.start()
            cps.append(cp)
        for cp in cps:
            cp.wait()
        acc = buf[0]
        for d in range(1, N_DEV):
            acc = acc + buf[d]
        o_ref[...] = acc

    return pl.pallas_call(
        body, name="all_reduce_small",
        in_specs=[pl.BlockSpec(memory_space=pltpu.VMEM)], out_specs=pl.BlockSpec(memory_space=pltpu.VMEM),
        out_shape=jax.ShapeDtypeStruct(vec.shape, F32),
        scratch_shapes=[pltpu.VMEM((N_DEV, R, 128), F32), pltpu.SemaphoreType.DMA((N_DEV - 1,)),
                        pltpu.SemaphoreType.DMA((N_DEV - 1,))],
    )(vec)


def adamw(w, g, m, v, *, tm=512):
    R, C = w.shape
    tm = max(t for t in range(8, min(tm, R) + 1, 8) if R % t == 0)
    c1 =1.0 - ADAM_B1 ** ADAM_STEP
    c2 = 1.0 - ADAM_B2 ** ADAM_STEP

    def body(w_ref, g_ref, m_ref, v_ref, d_ref, nm_ref, nv_ref):
        gg = g_ref[...]
        nm = ADAM_B1 * m_ref[...] + (1.0 - ADAM_B1) * gg
        nv = ADAM_B2 * v_ref[...] + (1.0 - ADAM_B2) * (gg * gg)
        nm_ref[...] = nm
        nv_ref[...] = nv
        d_ref[...] = -ADAM_LR * ((nm / c1) / (jnp.sqrt(nv / c2) + ADAM_EPS) + ADAM_WD * w_ref[...])

    blk = pl.BlockSpec((tm, C), lambda i: (i, 0))
    return pl.pallas_call(
        body, name="adamw",
        grid=(pl.cdiv(R, tm),),
        in_specs=[blk] * 4, out_specs=[blk] * 3,
        out_shape=[jax.ShapeDtypeStruct((R, C), F32)] * 3,
        compiler_params=_params(1),
    )(w, g, m, v)


_SMALL = (("n1", (2, D_MODEL)), ("nm", (2, D_MODEL)), ("n2", (2, D_MODEL)), ("nf", (D_MODEL,)),
          ("cb", (2, CONV_C)), ("lg", (2, CONV_C)), ("lb", (2, CONV_C)), ("sinks", (2, N_HEADS)),
          ("cw", (2, CONV_K, CONV_C)))


def _pack(parts, rows):
    flat = jnp.concatenate([p.reshape(-1).astype(F32) for p in parts])
    return jnp.pad(flat, (0, rows * 128 - flat.shape[0])).reshape(rows, 128)


def _unpack(block, shapes):
    flat = block.reshape(-1)
    out, o = [], 0
    for shp in shapes:
        n = 1
        for s in shp:
            n *= s
        out.append(flat[o:o + n].reshape(shp))
        o += n
    return out


def kernel(x, positions, ffn1_norm, ffn1_w_gate, ffn1_w_up, ffn1_w_down, mix_norm, w_in, conv_w, conv_b, conv_ln_g, conv_ln_b, attn_sinks, w_out, ffn2_norm, ffn2_w_gate, ffn2_w_up, ffn2_w_down, final_norm, loss_target, m_ffn1_norm, m_ffn1_w_gate, m_ffn1_w_up, m_ffn1_w_down, m_mix_norm, m_w_in, m_conv_w, m_conv_b, m_conv_ln_g, m_conv_ln_b, m_attn_sinks, m_w_out, m_ffn2_norm, m_ffn2_w_gate, m_ffn2_w_up, m_ffn2_w_down, m_final_norm, v_ffn1_norm, v_ffn1_w_gate, v_ffn1_w_up, v_ffn1_w_down, v_mix_norm, v_w_in, v_conv_w, v_conv_b, v_conv_ln_g, v_conv_ln_b, v_attn_sinks, v_w_out, v_ffn2_norm, v_ffn2_w_gate, v_ffn2_w_up, v_ffn2_w_down, v_final_norm):
    cx, cy, cc = _coords()
    chip = 2 * cx + cy
    c_idx = jnp.reshape(cc, (1,)).astype(jnp.int32)
    j_idx = jnp.reshape(chip, (1,)).astype(jnp.int32)
    L = ffn1_norm.shape[0]
    tr = lambda a: jnp.swapaxes(a, 1, 2)

    sh = dict(f1g=tr(ffn1_w_gate), f1u=tr(ffn1_w_up), f1d=ffn1_w_down, f2g=tr(ffn2_w_gate),
              f2u=tr(ffn2_w_up), f2d=ffn2_w_down, win=tr(w_in), wout=w_out)
    sh = {k: [v[l].astype(CDT) for l in range(L)] for k, v in sh.items()}
    W = {}

    def gather_op(keys):
        return ag_op([conv_w if k == "cw" else sh[k[0]][k[1]] for k in keys])

    def take(keys, res):
        for k, a in zip(keys, res):
            if k == "cw":
                W[k] = _own_slab(a, conv_w, chip)
            else:
                W[k] = _own_slab(a, sh[k[0]][k[1]], chip).reshape(N_CHIPS * a.shape[1], a.shape[2])

    def with_ag(fn, keys, *args):
        if not keys:
            return fn(*args)
        main, res = fn(*args, comm=gather_op(keys))
        take(keys, res)
        return main

    ag_hosts = {("ffn1", 0): [("win", 0), "cw", ("f2g", 0), ("f1d", 1)],
                ("inproj", 0): [("wout", 0)], ("attn", 0): [("f2u", 0)], ("conv", 0): [("f2d", 0)],
                ("ffn2", 0): [("f1g", 1), ("f1u", 1)],
                ("ffn1", 1): [("win", 1), ("f2g", 1), ("f2u", 1)],
                ("inproj", 1): [("wout", 1)], ("attn", 1): [("f2d", 1)]}
    for k_, a in zip(("f1g", "f1u", "f1d"), first_gather([sh[k_][0] for k_ in ("f1g", "f1u", "f1d")])):
        own = sh[k_][0].reshape(a.shape[1:])
        W[(k_, 0)] = _own_slab(a, own, chip).reshape(N_CHIPS * sh[k_][0].shape[0], sh[k_][0].shape[1])

    cs, sn = _rope_tables(positions)
    saved = []
    h = x[0]
    for l in range(L):
        sink = attn_sinks[l].reshape(2, 4)
        sink_col = jnp.repeat(sink, BLOCK, axis=1).reshape(2, 4 * BLOCK, 1)
        x0 = h
        x1, g1, u1 = with_ag(ffn_fwd, ag_hosts.get(("ffn1", l)), x0, ffn1_norm[l][None],
                             W[("f1g", l)], W[("f1u", l)], W[("f1d", l)])
        wext = _widen_w_in(W[("win", l)])
        q, k, v, u = with_ag(inproj_fwd, ag_hosts.get(("inproj", l)), x1, mix_norm[l][None], wext, cs, sn)
        ao = with_ag(attn_fwd, ag_hosts.get(("attn", l)), q, k, v, sink_col)
        cwl = jnp.transpose(W["cw"][:, l], (1, 0, 2)).reshape(CONV_K, CONV_C)
        co, yc = with_ag(conv_fwd, ag_hosts.get(("conv", l)), u, cwl, conv_b[l][None], conv_ln_g[l][None],
                         conv_ln_b[l][None])
        x2 = outproj_fwd(x1, ao, co, W[("wout", l)])
        x3, g2, u2 = with_ag(ffn_fwd, ag_hosts.get(("ffn2", l)), x2, ffn2_norm[l][None],
                             W[("f2g", l)], W[("f2u", l)], W[("f2d", l)])
        saved.append((x0, x1, x2, g1, u1, g2, u2, q, k, v, u, ao, co, yc, sink, wext, cwl))
        h = x3

    loss, dx, dnf = loss_head(h, final_norm[None], loss_target[0])

    active = []

    def advance(run):
        stages = [ch.next_stage() for ch in active]
        ops = [op for op, _ in stages]
        main, res = run(_merge(*ops) if ops else None)
        for (_, cb), r in zip(stages, _split(res, *ops)):
            cb(r)
        active[:] = [ch for ch in active if not ch.done]
        return main

    def hosted(fn, *args):
        def run(comm):
            if comm is None:
                return fn(*args), []
            return fn(*args, comm=comm)
        return advance(run)

    def chain(key, grads):
        chains[key] = _Chain(grads, c_idx, j_idx)
        active.append(chains[key])

    small = {k_: [None] * L for k_ in ("n1", "nm", "n2", "cw", "cb", "lg", "lb", "sinks")}
    chains = {}
    for l in reversed(range(L)):
        x0, x1, x2, g1, u1, g2, u2, q, k, v, u, ao, co, yc, sink, wext, cwl = saved[l]
        sink_row = jnp.repeat(sink, BLOCK, axis=1).reshape(2, 1, 4 * BLOCK)
        chain(("f2d", l), [hosted(ffn_wgrad_down, dx, g2, u2)])
        dx2, dgt, dup, hh, small["n2"][l] = hosted(
            ffn_bwd_dgrad, x2, ffn2_norm[l][None], dx, g2, u2, W[("f2g", l)], W[("f2u", l)], W[("f2d", l)])
        chain(("f2gu", l), hosted(ffn_wgrad_gate_up, hh, dgt, dup))
        da, dc, gwout = hosted(outproj_bwd, dx2, ao, co, W[("wout", l)])
        du, small["cw"][l], small["cb"][l], small["lg"][l], small["lb"][l] = hosted(
            conv_bwd, dc, u, yc, cwl, conv_ln_g[l][None], conv_ln_b[l][None])
        dq, dk, dv, dsink = hosted(attn_bwd, q, k, v, da, sink_row)
        small["sinks"][l] = jnp.sum(dsink.reshape(2, 4, BLOCK), axis=-1).reshape(N_HEADS)
        dx1, gwext, small["nm"][l] = inproj_bwd(x1, mix_norm[l][None], dx2, dq, dk, dv, du, wext, cs, sn)
        chain(("mx", l), [gwout, _fold_w_in(gwext)])
        chain(("f1d", l), [hosted(ffn_wgrad_down, dx1, g1, u1)])
        dx, dgt, dup, hh, small["n1"][l] = hosted(
            ffn_bwd_dgrad, x0, ffn1_norm[l][None], dx1, g1, u1, W[("f1g", l)], W[("f1u", l)], W[("f1d", l)])
        chain(("f1gu", l), hosted(ffn_wgrad_gate_up, hh, dgt, dup))

    while active:
        advance(lambda comm: (None, _run_comm(comm)))

    G = {k_: jnp.stack(v_) for k_, v_ in small.items()}
    G["nf"] = dnf
    small_shapes = [shp for _, shp in _SMALL]
    n_small = 1 + sum(math.prod(s) for s in small_shapes)
    rows = -(-n_small // 1024) * 8
    packed = _pack([loss] + [G[k_] for k_, _ in _SMALL], rows)
    summed = all_reduce_small(packed)
    loss_out, *small_sum = _unpack(summed, [()] + small_shapes)
    gs = dict(zip([k_ for k_, _ in _SMALL], small_sum))
    gs["cw"] = lax.dynamic_slice_in_dim(gs["cw"], chip * (CONV_C // N_CHIPS), CONV_C // N_CHIPS, axis=2)

    def big(group, idx, transpose):
        per_layer = [chains[(group, l)].final[idx] for l in range(L)]
        return jnp.stack([a.T if transpose else a for a in per_layer])

    grads = dict(ffn1_norm=gs["n1"], ffn1_w_gate=big("f1gu", 0, True), ffn1_w_up=big("f1gu", 1, True),
                 ffn1_w_down=big("f1d", 0, False), mix_norm=gs["nm"], w_in=big("mx", 1, True), conv_w=gs["cw"],
                 conv_b=gs["cb"], conv_ln_g=gs["lg"], conv_ln_b=gs["lb"], attn_sinks=gs["sinks"],
                 w_out=big("mx", 0, False), ffn2_norm=gs["n2"], ffn2_w_gate=big("f2gu", 0, True),
                 ffn2_w_up=big("f2gu", 1, True), ffn2_w_down=big("f2d", 0, False), final_norm=gs["nf"])

    weights = dict(ffn1_norm=ffn1_norm, ffn1_w_gate=ffn1_w_gate, ffn1_w_up=ffn1_w_up, ffn1_w_down=ffn1_w_down,
                   mix_norm=mix_norm, w_in=w_in, conv_w=conv_w, conv_b=conv_b, conv_ln_g=conv_ln_g,
                   conv_ln_b=conv_ln_b, attn_sinks=attn_sinks, w_out=w_out, ffn2_norm=ffn2_norm,
                   ffn2_w_gate=ffn2_w_gate, ffn2_w_up=ffn2_w_up, ffn2_w_down=ffn2_w_down, final_norm=final_norm)
    moms = dict(ffn1_norm=(m_ffn1_norm, v_ffn1_norm), ffn1_w_gate=(m_ffn1_w_gate, v_ffn1_w_gate),
                ffn1_w_up=(m_ffn1_w_up, v_ffn1_w_up), ffn1_w_down=(m_ffn1_w_down, v_ffn1_w_down),
                mix_norm=(m_mix_norm, v_mix_norm), w_in=(m_w_in, v_w_in), conv_w=(m_conv_w, v_conv_w),
                conv_b=(m_conv_b, v_conv_b), conv_ln_g=(m_conv_ln_g, v_conv_ln_g),
                conv_ln_b=(m_conv_ln_b, v_conv_ln_b), attn_sinks=(m_attn_sinks, v_attn_sinks),
                w_out=(m_w_out, v_w_out), ffn2_norm=(m_ffn2_norm, v_ffn2_norm),
                ffn2_w_gate=(m_ffn2_w_gate, v_ffn2_w_gate), ffn2_w_up=(m_ffn2_w_up, v_ffn2_w_up),
                ffn2_w_down=(m_ffn2_w_down, v_ffn2_w_down), final_norm=(m_final_norm, v_final_norm))
    names = list(weights)
    big_names = ("ffn1_w_gate", "ffn1_w_up", "ffn1_w_down", "w_in", "w_out", "ffn2_w_gate", "ffn2_w_up",
                 "ffn2_w_down")
    delta, new_m, new_v = {}, {}, {}
    for nme in big_names:
        shp = weights[nme].shape
        two = lambda a: a.reshape(shp[0] * shp[1], shp[2])
        d, nm_, nv_ = adamw(two(weights[nme]), two(grads[nme]), two(moms[nme][0]), two(moms[nme][1]))
        delta[nme], new_m[nme], new_v[nme] = d.reshape(shp), nm_.reshape(shp), nv_.reshape(shp)
    small_names = [nme for nme in names if nme not in big_names]
    s_shapes = [weights[nme].shape for nme in small_names]
    n_tot = sum(math.prod(s) for s in s_shapes)
    srows = -(-n_tot // 1024) * 8
    d, nm_, nv_ = adamw(_pack([weights[nme] for nme in small_names], srows),
                        _pack([grads[nme] for nme in small_names], srows),
                        _pack([moms[nme][0] for nme in small_names], srows),
                        _pack([moms[nme][1] for nme in small_names], srows))
    for nme, dd, mm, vv in zip(small_names, _unpack(d, s_shapes), _unpack(nm_, s_shapes), _unpack(nv_, s_shapes)):
        delta[nme], new_m[nme], new_v[nme] = dd, mm, vv

    return (loss_out, dx[None], *[grads[nme] for nme in names], *[delta[nme] for nme in names],
            *[new_m[nme] for nme in names], *[new_v[nme] for nme in names])
```

```python
import math

import jax
import jax.numpy as jnp
from jax import lax
from jax.experimental import pallas as pl
from jax.experimental.pallas import tpu as pltpu

F32 = jnp.float32
CDT = jnp.bfloat16
D_MODEL = 1024
D_FF = 2816
N_HEADS = 8
HEAD_DIM = 64
BLOCK = 128
CONV_K = 31
CONV_C = 512
ATT_W = 512
D_EXT = 2048
EPS = 1e-5
HALO = 32
FF_CHUNK = 256
NEG = float(jnp.finfo(jnp.float32).min)
VMEM_LIMIT = 56 * 1024 * 1024

ADAM_LR = 0.001
ADAM_B1 = 0.9
ADAM_B2 = 0.999
ADAM_EPS = 1e-08
ADAM_WD = 0.01
ADAM_STEP = 10

NT = (((1,), (1,)), ((), ()))
TN = (((0,), (0,)), ((), ()))


MESH = pl.DeviceIdType.MESH
ANY = pl.BlockSpec(memory_space=pl.ANY)
N_CHIPS = 4
N_DEV = 8


def _params(n_axes):
    return pltpu.CompilerParams(dimension_semantics=("arbitrary",) * n_axes, vmem_limit_bytes=VMEM_LIMIT)


class _Comm:
    def __init__(self, name, inputs, out_shape, sems, descs):
        self.name, self.inputs, self.out_shape, self.sems, self.descs = name, list(inputs), list(out_shape), list(sems), descs


def _merge(*ops):
    ops = [o for o in ops if o is not None]
    if len(ops) == 1:
        return ops[0]

    def descs(cins, couts, sems):
        out, i, o, s = [], 0, 0, 0
        for op in ops:
            ni, no, ns = len(op.inputs), len(op.out_shape), len(op.sems)
            out += op.descs(cins[i:i + ni], couts[o:o + no], sems[s:s + ns])
            i, o, s = i + ni, o + no, s + ns
        return out

    return _Comm("_".join(o.name for o in ops), sum((o.inputs for o in ops), []),
                 sum((o.out_shape for o in ops), []), sum((o.sems for o in ops), []), descs)


def _split(couts, *ops):
    res, o = [], 0
    for op in ops:
        res.append(couts[o:o + len(op.out_shape)])
        o += len(op.out_shape)
    return res


def _hosted(comm):
    if comm is None:
        return pl.pallas_call

    def make(body, *, name, grid, in_specs, out_specs, out_shape, compiler_params, scratch_shapes=()):
        single = not isinstance(out_shape, (list, tuple))
        o_specs = [out_specs] if single else list(out_specs)
        o_shape = [out_shape] if single else list(out_shape)
        n_in, n_out, n_sc = len(in_specs), len(o_specs), len(scratch_shapes)
        c_in, c_out = len(comm.inputs), len(comm.out_shape)

        def hosted(*refs):
            ins, cins = refs[:n_in], refs[n_in:n_in + c_in]
            o0 = n_in + c_in
            outs, couts = refs[o0:o0 + n_out], refs[o0 + n_out:o0 + n_out + c_out]
            s0 = o0 + n_out + c_out
            scr, sems = refs[s0:s0 + n_sc], refs[s0 + n_sc:]
            first = pl.program_id(0) == 0
            last = pl.program_id(0) == grid[0] - 1
            for ax in range(1, len(grid)):
                first = first & (pl.program_id(ax) == 0)
                last = last & (pl.program_id(ax) == grid[ax] - 1)

            @pl.when(first)
            def _():
                for d in comm.descs(cins, couts, sems):
                    d.start()

            body(*ins, *outs, *scr)

            @pl.when(last)
            def _():
                for d in comm.descs(cins, couts, sems):
                    d.wait()

        call = pl.pallas_call(
            hosted, name=f"{name}_{comm.name}", grid=grid,
            in_specs=list(in_specs) + [ANY] * c_in, out_specs=o_specs + [ANY] * c_out,
            out_shape=o_shape + comm.out_shape, scratch_shapes=list(scratch_shapes) + comm.sems,
            compiler_params=compiler_params)

        def run(*args):
            res = call(*args, *comm.inputs)
            return (res[0] if single else list(res[:n_out])), list(res[n_out:])

        return run

    return make


def _run_comm(comm):
    c_in = len(comm.inputs)

    def body(*refs):
        cins, couts, sems = refs[:c_in], refs[c_in:c_in + len(comm.out_shape)], refs[c_in + len(comm.out_shape):]
        ds = comm.descs(cins, couts, sems)
        for d in ds:
            d.start()
        for d in ds:
            d.wait()

    return list(pl.pallas_call(
        body, name=comm.name, in_specs=[ANY] * c_in, out_specs=[ANY] * len(comm.out_shape),
        out_shape=comm.out_shape, scratch_shapes=comm.sems)(*comm.inputs))


def _coords():
    return lax.axis_index("x"), lax.axis_index("y"), lax.axis_index("c")


def _other_chips(x, y):
    return [(1 - x, y), (x, 1 - y), (1 - x, 1 - y)]


def ag_op(shards):
    n = len(shards)

    def descs(cins, couts, sems):
        send, recv = sems
        x, y, c = _coords()
        j = 2 * x + y
        ds = []
        for a in range(n):
            for p, (px, py) in enumerate(_other_chips(x, y)):
                ds.append(pltpu.make_async_remote_copy(cins[a], couts[a].at[j], send.at[a, p], recv.at[a, p],
                                                       device_id=(px, py, c), device_id_type=MESH))
        return ds

    return _Comm("ag", shards, [jax.ShapeDtypeStruct((N_CHIPS,) + s.shape, s.dtype) for s in shards],
                 [pltpu.SemaphoreType.DMA((n, 3)), pltpu.SemaphoreType.DMA((n, 3))], descs)


def _own_slab(gathered, mine, idx):
    return lax.dynamic_update_slice_in_dim(gathered, mine[None], idx, axis=0)


def first_gather(shards):
    n = len(shards)
    halves = [s.reshape(2, s.shape[0] // 2, s.shape[1]) for s in shards]

    def body(*refs):
        ins, outs = refs[:n], refs[n:2 * n]
        send1, recv1, send2, recv2 = refs[2 * n:]
        x, y, c = _coords()
        j = 2 * x + y
        chips = _other_chips(x, y)
        ici = [pltpu.make_async_remote_copy(ins[a].at[c], outs[a].at[j, c], send1.at[a, p], recv1.at[a, p],
                                            device_id=(px, py, c), device_id_type=MESH)
               for a in range(n) for p, (px, py) in enumerate(chips)]
        for d in ici:
            d.start()
        for d in ici:
            d.wait()
        d2d = [pltpu.make_async_remote_copy(outs[a].at[2 * px + py, c], outs[a].at[2 * px + py, c],
                                            send2.at[a, p], recv2.at[a, p],
                                            device_id=(x, y, 1 - c), device_id_type=MESH)
               for a in range(n) for p, (px, py) in enumerate(chips)]
        for d in d2d:
            d.start()
        for d in d2d:
            d.wait()

    return list(pl.pallas_call(
        body, name="first_gather", in_specs=[ANY] * n, out_specs=[ANY] * n,
        out_shape=[jax.ShapeDtypeStruct((N_CHIPS,) + h.shape, h.dtype) for h in halves],
        scratch_shapes=[pltpu.SemaphoreType.DMA((n, 3))] * 4)(*halves))


def swap_op(grads):
    n = len(grads)

    def descs(cins, couts, sems):
        send, recv = sems
        x, y, c = _coords()
        return [pltpu.make_async_remote_copy(cins[a].at[:, 1 - c], couts[a], send.at[a], recv.at[a],
                                             device_id=(x, y, 1 - c), device_id_type=MESH) for a in range(n)]

    return _Comm("swap", grads, [jax.ShapeDtypeStruct(g.shape[:1] + g.shape[2:], g.dtype) for g in grads],
                 [pltpu.SemaphoreType.DMA((n,)), pltpu.SemaphoreType.DMA((n,))], descs)


def exchange_op(parts):
    n = len(parts)

    def descs(cins, couts, sems):
        send, recv = sems
        x, y, c = _coords()
        ds = []
        for a in range(n):
            for p, (px, py) in enumerate(_other_chips(x, y)):
                ds.append(pltpu.make_async_remote_copy(cins[a].at[2 * px + py], couts[a].at[p], send.at[a, p],
                                                       recv.at[a, p], device_id=(px, py, c), device_id_type=MESH))
        return ds

    return _Comm("xchg", parts, [jax.ShapeDtypeStruct((3,) + p.shape[1:], p.dtype) for p in parts],
                 [pltpu.SemaphoreType.DMA((n, 3)), pltpu.SemaphoreType.DMA((n, 3))], descs)


def share_op(totals):
    n = len(totals)

    def descs(cins, couts, sems):
        send, recv = sems
        x, y, c = _coords()
        return [pltpu.make_async_remote_copy(cins[a], couts[a].at[c], send.at[a], recv.at[a],
                                             device_id=(x, y, 1 - c), device_id_type=MESH) for a in range(n)]

    return _Comm("share", totals, [jax.ShapeDtypeStruct((2,) + t.shape, t.dtype) for t in totals],
                 [pltpu.SemaphoreType.DMA((n,)), pltpu.SemaphoreType.DMA((n,))], descs)


def _sigmoid(z):
    return 1.0 / (1.0 + jnp.exp(-z))


def _rms_parts(xf):
    r = lax.rsqrt(jnp.mean(xf * xf, axis=-1, keepdims=True) + EPS)
    return xf * r, r


def _rms_bwd(dh, xhat, r, g):
    dg = jnp.sum(dh * xhat, axis=0, keepdims=True)
    dxhat = dh * g
    dx = r * (dxhat - xhat * jnp.mean(dxhat * xhat, axis=-1, keepdims=True))
    return dx, dg


def _chunks(n, ck=FF_CHUNK):
    return [(c0, min(ck, n - c0)) for c0 in range(0, n, ck)]


def ffn_fwd(x, g, wgT, wuT, wd, *, tm=512, comm=None):
    S, D = x.shape
    F = wgT.shape[0]

    def body(x_ref, g_ref, wg_ref, wu_ref, wd_ref, o_ref, gate_ref, up_ref, a_sc):
        xf = x_ref[...]
        xhat, _ = _rms_parts(xf)
        h = (xhat * g_ref[...]).astype(CDT)
        for c0, cw_ in _chunks(F):
            sl = slice(c0, c0 + cw_)
            gt = lax.dot_general(h, wg_ref[sl, :], NT, preferred_element_type=F32)
            ut = lax.dot_general(h, wu_ref[sl, :], NT, preferred_element_type=F32)
            gate_ref[:, sl] = gt.astype(CDT)
            up_ref[:, sl] = ut.astype(CDT)
            a_sc[:, sl] = (gt * _sigmoid(gt) * ut).astype(CDT)
        o_ref[...] = xf + 0.5 * jnp.dot(a_sc[...], wd_ref[...], preferred_element_type=F32)

    wspec = pl.BlockSpec((F, D), lambda i: (0, 0), pipeline_mode=pl.Buffered(1))
    return _hosted(comm)(
        body, name="ffn_fwd",
        grid=(S // tm,),
        in_specs=[pl.BlockSpec((tm, D), lambda i: (i, 0)), pl.BlockSpec((1, D), lambda i: (0, 0)),
                  wspec, wspec, wspec],
        out_specs=[pl.BlockSpec((tm, D), lambda i: (i, 0)),
                   pl.BlockSpec((tm, F), lambda i: (i, 0)),
                   pl.BlockSpec((tm, F), lambda i: (i, 0))],
        out_shape=[jax.ShapeDtypeStruct((S, D), F32),
                   jax.ShapeDtypeStruct((S, F), CDT),
                   jax.ShapeDtypeStruct((S, F), CDT)],
        scratch_shapes=[pltpu.VMEM((tm, F), CDT)],
        compiler_params=_params(1),
    )(x, g, wgT, wuT, wd)


def ffn_bwd_dgrad(x, g, dy, gate, up, wgT, wuT, wd, *, tm=512, tf=1408, comm=None):
    S, D = x.shape
    F = wgT.shape[0]
    nf = F // tf

    def body(x_ref, g_ref, dy_ref, gate_ref, up_ref, wg_ref, wu_ref, wd_ref,
             dx_ref, dgate_ref, dup_ref, h_ref, dg_ref, dh_sc, dacc_sc):
        i = pl.program_id(0)
        j = pl.program_id(1)

        @pl.when(j == 0)
        def _():
            xhat, _ = _rms_parts(x_ref[...])
            h_ref[...] = (xhat * g_ref[...]).astype(CDT)
            dacc_sc[...] = (0.5 * dy_ref[...]).astype(CDT)
            dh_sc[...] = jnp.zeros_like(dh_sc)

        @pl.when((i == 0) & (j == 0))
        def _():
            dg_ref[...] = jnp.zeros_like(dg_ref)

        dacc = dacc_sc[...]
        for c0, cw_ in _chunks(tf):
            sl = slice(c0, c0 + cw_)
            d_a = lax.dot_general(dacc, wd_ref[sl, :], NT, preferred_element_type=F32)
            gt = gate_ref[:, sl].astype(F32)
            ut = up_ref[:, sl].astype(F32)
            sg = _sigmoid(gt)
            dup_ref[:, sl] = (d_a * (gt * sg)).astype(CDT)
            dgate_ref[:, sl] = (d_a * ut * (sg * (1.0 + gt * (1.0 - sg)))).astype(CDT)
        dh_sc[...] += (jnp.dot(dgate_ref[...], wg_ref[...], preferred_element_type=F32)
                       + jnp.dot(dup_ref[...], wu_ref[...], preferred_element_type=F32))

        @pl.when(j == nf - 1)
        def _():
            xhat, r = _rms_parts(x_ref[...])
            dx, dg = _rms_bwd(dh_sc[...], xhat, r, g_ref[...])
            dx_ref[...] = dy_ref[...] + dx
            dg_ref[...] += dg

    return _hosted(comm)(
        body, name="ffn_bwd_dgrad",
        grid=(S // tm, nf),
        in_specs=[pl.BlockSpec((tm, D), lambda i, j: (i, 0)),
                  pl.BlockSpec((1, D), lambda i, j: (0, 0)),
                  pl.BlockSpec((tm, D), lambda i, j: (i, 0)),
                  pl.BlockSpec((tm, tf), lambda i, j: (i, j)),
                  pl.BlockSpec((tm, tf), lambda i, j: (i, j)),
                  pl.BlockSpec((tf, D), lambda i, j: (j, 0)),
                  pl.BlockSpec((tf, D), lambda i, j: (j, 0)),
                  pl.BlockSpec((tf, D), lambda i, j: (j, 0))],
        out_specs=[pl.BlockSpec((tm, D), lambda i, j: (i, 0)),
                   pl.BlockSpec((tm, tf), lambda i, j: (i, j)),
                   pl.BlockSpec((tm, tf), lambda i, j: (i, j)),
                   pl.BlockSpec((tm, D), lambda i, j: (i, 0)),
                   pl.BlockSpec((1, D), lambda i, j: (0, 0))],
        out_shape=[jax.ShapeDtypeStruct((S, D), F32),
                   jax.ShapeDtypeStruct((S, F), CDT),
                   jax.ShapeDtypeStruct((S, F), CDT),
                   jax.ShapeDtypeStruct((S, D), CDT),
                   jax.ShapeDtypeStruct((1, D), F32)],
        scratch_shapes=[pltpu.VMEM((tm, D), F32), pltpu.VMEM((tm, D), CDT)],
        compiler_params=_params(2),
    )(x, g, dy, gate, up, wgT, wuT, wd)


def ffn_wgrad_down(dy, gate, up, *, tk=512, comm=None):
    S, D = dy.shape
    F = gate.shape[1]
    tk = min(tk, S)

    def body(dy_ref, gate_ref, up_ref, dwd_ref):
        @pl.when(pl.program_id(0) == 0)
        def _():
            dwd_ref[...] = jnp.zeros_like(dwd_ref)

        dacc = (0.5 * dy_ref[...]).astype(CDT)
        for c0, cw_ in _chunks(F):
            sl = slice(c0, c0 + cw_)
            gt = gate_ref[:, sl].astype(F32)
            a = (gt * _sigmoid(gt) * up_ref[:, sl].astype(F32)).astype(CDT)
            dwd_ref[sl, :] += lax.dot_general(a, dacc, TN, preferred_element_type=F32)

    act = pl.BlockSpec((tk, F), lambda k: (k, 0))
    return _hosted(comm)(
        body, name="ffn_wgrad_down",
        grid=(S // tk,),
        in_specs=[pl.BlockSpec((tk, D), lambda k: (k, 0)), act, act],
        out_specs=pl.BlockSpec((F, D), lambda k: (0, 0)),
        out_shape=jax.ShapeDtypeStruct((F, D), F32),
        compiler_params=_params(1),
    )(dy, gate, up)


def ffn_wgrad_gate_up(h, dgate, dup, *, tk=1024, tf=1408, comm=None):
    S, D = h.shape
    F = dgate.shape[1]
    tk = min(tk, S)

    def body(h_ref, dgate_ref, dup_ref, dwg_ref, dwu_ref):
        k = pl.program_id(1)

        @pl.when(k == 0)
        def _():
            dwg_ref[...] = jnp.zeros_like(dwg_ref)
            dwu_ref[...] = jnp.zeros_like(dwu_ref)

        hh = h_ref[...]
        for c0, cw_ in _chunks(tf):
            sl = slice(c0, c0 + cw_)
            dwg_ref[sl, :] += lax.dot_general(dgate_ref[:, sl], hh, TN, preferred_element_type=F32)
            dwu_ref[sl, :] += lax.dot_general(dup_ref[:, sl], hh, TN, preferred_element_type=F32)

    tok = pl.BlockSpec((tk, D), lambda j, k: (k, 0))
    act = pl.BlockSpec((tk, tf), lambda j, k: (k, j))
    out = pl.BlockSpec((tf, D), lambda j, k: (j, 0))
    return _hosted(comm)(
        body, name="ffn_wgrad_gate_up",
        grid=(F // tf, S // tk),
        in_specs=[tok, act, act],
        out_specs=[out, out],
        out_shape=[jax.ShapeDtypeStruct((F, D), F32)] * 2,
        compiler_params=_params(2),
    )(h, dgate, dup)


def loss_head(x, g, target, *, tm=512):
    S, D = x.shape

    def body(x_ref, g_ref, t_ref, loss_ref, dx_ref, dg_ref):
        @pl.when(pl.program_id(0) == 0)
        def _():
            loss_ref[...] = jnp.zeros_like(loss_ref)
            dg_ref[...] = jnp.zeros_like(dg_ref)

        xhat, r = _rms_parts(x_ref[...])
        gg = g_ref[...]
        err = xhat * gg - t_ref[...]
        loss_ref[...] += 0.5 * jnp.sum(jnp.mean(err * err, axis=-1, keepdims=True), axis=0, keepdims=True)
        dx, dg = _rms_bwd(err * (1.0 / D), xhat, r, gg)
        dx_ref[...] = dx
        dg_ref[...] += dg

    row = pl.BlockSpec((tm, D), lambda i: (i, 0))
    vec = pl.BlockSpec((1, D), lambda i: (0, 0))
    return pl.pallas_call(
        body, name="loss_head",
        grid=(S // tm,),
        in_specs=[row, vec, row],
        out_specs=[pl.BlockSpec((1, 1), lambda i: (0, 0)), row, vec],
        out_shape=[jax.ShapeDtypeStruct((1, 1), F32), jax.ShapeDtypeStruct((S, D), F32),
                   jax.ShapeDtypeStruct((1, D), F32)],
        compiler_params=_params(1),
    )(x, g, target)


def _rope_apply(t, cs, sn):
    lane = lax.broadcasted_iota(jnp.int32, t.shape, 1)
    first = (lane % HEAD_DIM) < (HEAD_DIM // 2)
    rot = jnp.where(first, pltpu.roll(t, 128 - HEAD_DIM // 2, 1), pltpu.roll(t, HEAD_DIM // 2, 1))
    return t * cs + rot * sn


def _rope_transpose(d, cs, sn):
    lane = lax.broadcasted_iota(jnp.int32, d.shape, 1)
    first = (lane % HEAD_DIM) < (HEAD_DIM // 2)
    ds = d * sn
    rot = jnp.where(first, pltpu.roll(ds, 128 - HEAD_DIM // 2, 1), pltpu.roll(ds, HEAD_DIM // 2, 1))
    return d * cs + rot


def inproj_fwd(x, g, wextT, cs, sn, *, tm=512, comm=None):
    S, D = x.shape
    scale = HEAD_DIM ** -0.5

    def body(x_ref, g_ref, w_ref, cs_ref, sn_ref, q_ref, k_ref, v_ref, u_ref):
        xhat, _ = _rms_parts(x_ref[...])
        h = (xhat * g_ref[...]).astype(CDT)
        p = lax.dot_general(h, w_ref[...], NT, preferred_element_type=F32)
        c, s = cs_ref[...], sn_ref[...]
        for b in range(4):
            q_ref[:, 128 * b:128 * (b + 1)] = (_rope_apply(p[:, 128 * b:128 * (b + 1)], c, s) * scale).astype(CDT)
        for b in range(2):
            k_ref[:, 128 * b:128 * (b + 1)] = _rope_apply(p[:, 512 + 128 * b:512 + 128 * (b + 1)], c, s).astype(CDT)
        v_ref[...] = p[:, 768:1024].astype(CDT)
        u_ref[...] = p[:, 1024:2048]

    def row(w):
        return pl.BlockSpec((tm, w), lambda i: (i, 0))

    return _hosted(comm)(
        body, name="inproj_fwd",
        grid=(S // tm,),
        in_specs=[row(D), pl.BlockSpec((1, D), lambda i: (0, 0)),
                  pl.BlockSpec((D_EXT, D), lambda i: (0, 0)), row(128), row(128)],
        out_specs=[row(512), row(256), row(256), row(1024)],
        out_shape=[jax.ShapeDtypeStruct((S, 512), CDT), jax.ShapeDtypeStruct((S, 256), CDT),
                   jax.ShapeDtypeStruct((S, 256), CDT), jax.ShapeDtypeStruct((S, 1024), F32)],
        compiler_params=_params(1),
    )(x, g, wextT, cs, sn)


def _stack_heads(p0, p1):
    lane = lax.broadcasted_iota(jnp.int32, p0.shape, 1)
    lo = lane < HEAD_DIM
    z = jnp.zeros_like(p0)
    return jnp.concatenate([jnp.where(lo, p0, z), jnp.where(lo, z, p0),
                            jnp.where(lo, p1, z), jnp.where(lo, z, p1)], axis=0)


def _unstack_heads(o):
    lane = lax.broadcasted_iota(jnp.int32, (BLOCK, 128), 1)
    lo = lane < HEAD_DIM
    return (jnp.where(lo, o[0:128], o[128:256]), jnp.where(lo, o[256:384], o[384:512]))


def _band_mask_qk(n):
    i = lax.broadcasted_iota(jnp.int32, (4 * BLOCK, 2 * BLOCK), 0) % BLOCK
    c = lax.broadcasted_iota(jnp.int32, (4 * BLOCK, 2 * BLOCK), 1)
    return (c > i) & (c <= i + BLOCK) & ((n > 0) | (c >= BLOCK))


def _band_mask_kq(n):
    c = lax.broadcasted_iota(jnp.int32, (2 * BLOCK, 4 * BLOCK), 0)
    i = lax.broadcasted_iota(jnp.int32, (2 * BLOCK, 4 * BLOCK), 1) % BLOCK
    return (c > i) & (c <= i + BLOCK) & ((n > 0) | (c >= BLOCK))


def attn_fwd(q, k, v, sink_col, *, nb=2, comm=None):
    S = q.shape[0]
    tq = nb * BLOCK

    def body(q_ref, k_ref, v_ref, sink_ref, o_ref):
        t = pl.program_id(0)
        for b in range(nb):
            n = t * nb + b
            prev = pl.multiple_of(jnp.maximum(n - 1, 0) * BLOCK, BLOCK)
            cur = pl.multiple_of(n * BLOCK, BLOCK)
            rows = slice(b * BLOCK, (b + 1) * BLOCK)
            mask = _band_mask_qk(n)
            for gidx in range(2):
                lanes = slice(128 * gidx, 128 * (gidx + 1))
                qs = _stack_heads(q_ref[rows, 256 * gidx:256 * gidx + 128],
                                  q_ref[rows, 256 * gidx + 128:256 * gidx + 256])
                kb = jnp.concatenate([k_ref[pl.ds(prev, BLOCK), lanes], k_ref[pl.ds(cur, BLOCK), lanes]], axis=0)
                vb = jnp.concatenate([v_ref[pl.ds(prev, BLOCK), lanes], v_ref[pl.ds(cur, BLOCK), lanes]], axis=0)
                s = lax.dot_general(qs, kb, NT, preferred_element_type=F32)
                s = jnp.where(mask, s, NEG)
                sink = sink_ref[gidx]
                m = jnp.maximum(jnp.max(s, axis=1, keepdims=True), sink)
                p = jnp.exp(s - m)
                den = jnp.sum(p, axis=1, keepdims=True) + jnp.exp(sink - m)
                o = jnp.dot(p.astype(CDT), vb, preferred_element_type=F32) / den
                o0, o1 = _unstack_heads(o)
                o_ref[rows, 256 * gidx:256 * gidx + 128] = o0.astype(CDT)
                o_ref[rows, 256 * gidx + 128:256 * gidx + 256] = o1.astype(CDT)

    return _hosted(comm)(
        body, name="attn_fwd",
        grid=(S // tq,),
        in_specs=[pl.BlockSpec((tq, 512), lambda t: (t, 0)),
                  pl.BlockSpec((S, 256), lambda t: (0, 0)),
                  pl.BlockSpec((S, 256), lambda t: (0, 0)),
                  pl.BlockSpec((2, 4 * BLOCK, 1), lambda t: (0, 0, 0))],
        out_specs=pl.BlockSpec((tq, 512), lambda t: (t, 0)),
        out_shape=jax.ShapeDtypeStruct((S, 512), CDT),
        compiler_params=_params(1),
    )(q, k, v, sink_col)


def attn_bwd(q, k, v, do, sink_row, *, nb=2, comm=None):
    S = q.shape[0]
    tq = nb * BLOCK
    scale = HEAD_DIM ** -0.5

    def body(q_ref, k_ref, v_ref, do_ref, sink_ref, dq_ref, dk_ref, dv_ref, dsink_ref):
        t = pl.program_id(0)

        @pl.when(t == 0)
        def _():
            dk_ref[...] = jnp.zeros_like(dk_ref)
            dv_ref[...] = jnp.zeros_like(dv_ref)
            dsink_ref[...] = jnp.zeros_like(dsink_ref)

        for b in range(nb):
            n = t * nb + b
            prev = pl.multiple_of(jnp.maximum(n - 1, 0) * BLOCK, BLOCK)
            cur = pl.multiple_of(n * BLOCK, BLOCK)
            rows = slice(b * BLOCK, (b + 1) * BLOCK)
            mask = _band_mask_kq(n)
            for gidx in range(2):
                lanes = slice(128 * gidx, 128 * (gidx + 1))
                qs = _stack_heads(q_ref[rows, 256 * gidx:256 * gidx + 128],
                                  q_ref[rows, 256 * gidx + 128:256 * gidx + 256])
                dos = _stack_heads(do_ref[rows, 256 * gidx:256 * gidx + 128],
                                   do_ref[rows, 256 * gidx + 128:256 * gidx + 256])
                kb = jnp.concatenate([k_ref[pl.ds(prev, BLOCK), lanes], k_ref[pl.ds(cur, BLOCK), lanes]], axis=0)
                vb = jnp.concatenate([v_ref[pl.ds(prev, BLOCK), lanes], v_ref[pl.ds(cur, BLOCK), lanes]], axis=0)
                st = lax.dot_general(kb, qs, NT, preferred_element_type=F32)
                st = jnp.where(mask, st, NEG)
                sink = sink_ref[gidx]
                m = jnp.maximum(jnp.max(st, axis=0, keepdims=True), sink)
                e = jnp.exp(st - m)
                es = jnp.exp(sink - m)
                inv = 1.0 / (jnp.sum(e, axis=0, keepdims=True) + es)
                pt = e * inv
                dpt = lax.dot_general(vb, dos, NT, preferred_element_type=F32)
                delta = jnp.sum(pt * dpt, axis=0, keepdims=True)
                dst = (pt * (dpt - delta)).astype(CDT)
                dsink_ref[gidx] += -(es * inv) * delta
                dvb = jnp.dot(pt.astype(CDT), dos, preferred_element_type=F32)
                dkb = jnp.dot(dst, qs, preferred_element_type=F32)
                dqs = lax.dot_general(dst, kb, TN, preferred_element_type=F32) * scale
                dq0, dq1 = _unstack_heads(dqs)
                dq_ref[rows, 256 * gidx:256 * gidx + 128] = dq0
                dq_ref[rows, 256 * gidx + 128:256 * gidx + 256] = dq1
                dk_ref[pl.ds(prev, BLOCK), lanes] += dkb[0:BLOCK]
                dk_ref[pl.ds(cur, BLOCK), lanes] += dkb[BLOCK:2 * BLOCK]
                dv_ref[pl.ds(prev, BLOCK), lanes] += dvb[0:BLOCK]
                dv_ref[pl.ds(cur, BLOCK), lanes] += dvb[BLOCK:2 * BLOCK]

    full = pl.BlockSpec((S, 256), lambda t: (0, 0))
    tile = pl.BlockSpec((tq, 512), lambda t: (t, 0))
    srow = pl.BlockSpec((2, 1, 4 * BLOCK), lambda t: (0, 0, 0))
    return _hosted(comm)(
        body, name="attn_bwd",
        grid=(S // tq,),
        in_specs=[tile, full, full, tile, srow],
        out_specs=[tile, full, full, srow],
        out_shape=[jax.ShapeDtypeStruct((S, 512), F32), jax.ShapeDtypeStruct((S, 256), F32),
                   jax.ShapeDtypeStruct((S, 256), F32), jax.ShapeDtypeStruct((2, 1, 4 * BLOCK), F32)],
        compiler_params=_params(1),
    )(q, k, v, do, sink_row)


def _glu(u):
    a = u[:, 0:CONV_C]
    gt = u[:, CONV_C:2 * CONV_C]
    sg = _sigmoid(gt)
    return a, sg, a * sg


CONV_CHUNK = 64


def conv_fwd(u, cw, cb, lg, lb, *, tm=512, comm=None):
    S = u.shape[0]
    nh = tm // HALO

    def body(u_ref, uh_ref, cw_ref, cb_ref, lg_ref, lb_ref, o_ref, y_ref, hbuf):
        t = pl.program_id(0)
        _, _, hg = _glu(u_ref[...])
        _, _, hh = _glu(uh_ref[...])
        hbuf[0:HALO, :] = jnp.where(t > 0, hh, jnp.zeros_like(hh))
        hbuf[HALO:HALO + tm, :] = hg
        off = HALO - (CONV_K - 1)
        for c0 in range(0, tm, CONV_CHUNK):
            acc = jnp.zeros((CONV_CHUNK, CONV_C), F32) + cb_ref[...]
            for j in range(CONV_K):
                acc = acc + cw_ref[j:j + 1, :] * hbuf[c0 + off + j:c0 + off + j + CONV_CHUNK, :]
            y_ref[c0:c0 + CONV_CHUNK, :] = acc
        y = y_ref[...]
        yc = y - jnp.mean(y, axis=-1, keepdims=True)
        r = lax.rsqrt(jnp.mean(yc * yc, axis=-1, keepdims=True) + EPS)
        z = yc * r * lg_ref[...] + lb_ref[...]
        o_ref[...] = (z * _sigmoid(z)).astype(CDT)

    vec = pl.BlockSpec((1, CONV_C), lambda t: (0, 0))
    return _hosted(comm)(
        body, name="conv_fwd",
        grid=(S // tm,),
        in_specs=[pl.BlockSpec((tm, 2 * CONV_C), lambda t: (t, 0)),
                  pl.BlockSpec((HALO, 2 * CONV_C), lambda t: (jnp.maximum(t * nh - 1, 0), 0)),
                  pl.BlockSpec((CONV_K, CONV_C), lambda t: (0, 0)), vec, vec, vec],
        out_specs=[pl.BlockSpec((tm, CONV_C), lambda t: (t, 0)), pl.BlockSpec((tm, CONV_C), lambda t: (t, 0))],
        out_shape=[jax.ShapeDtypeStruct((S, CONV_C), CDT), jax.ShapeDtypeStruct((S, CONV_C), F32)],
        scratch_shapes=[pltpu.VMEM((HALO + tm, CONV_C), F32)],
        compiler_params=_params(1),
    )(u, u, cw, cb, lg, lb)


def conv_bwd(dc, u, y, cw, lg, lb, *, tm=512, comm=None):
    S = u.shape[0]
    nh = tm // HALO
    nt = S // tm

    def ln_bwd(dcv, yv, lgv, lbv):
        yc = yv - jnp.mean(yv, axis=-1, keepdims=True)
        r = lax.rsqrt(jnp.mean(yc * yc, axis=-1, keepdims=True) + EPS)
        yhat = yc * r
        z = yhat * lgv + lbv
        sg = _sigmoid(z)
        dz = dcv * (sg * (1.0 + z * (1.0 - sg)))
        dyhat = dz * lgv
        dy = r * (dyhat - jnp.mean(dyhat, axis=-1, keepdims=True)
                  - yhat * jnp.mean(dyhat * yhat, axis=-1, keepdims=True))
        return dy, dz, yhat

    def body(dc_ref, dcn_ref, u_ref, uh_ref, y_ref, yn_ref, cw_ref, lg_ref, lb_ref,
             du_ref, dcw_ref, dcb_ref, dlg_ref, dlb_ref, hbuf, dybuf, dhg_sc, dw_sc):
        t = pl.program_id(0)

        @pl.when(t == 0)
        def _():
            dw_sc[...] = jnp.zeros_like(dw_sc)
            dcb_ref[...] = jnp.zeros_like(dcb_ref)
            dlg_ref[...] = jnp.zeros_like(dlg_ref)
            dlb_ref[...] = jnp.zeros_like(dlb_ref)

        lgv, lbv = lg_ref[...], lb_ref[...]
        dy, dz, yhat = ln_bwd(dc_ref[...].astype(F32), y_ref[...], lgv, lbv)
        dyn, _, _ = ln_bwd(dcn_ref[...].astype(F32), yn_ref[...], lgv, lbv)
        dlb_ref[...] += jnp.sum(dz, axis=0, keepdims=True)
        dlg_ref[...] += jnp.sum(dz * yhat, axis=0, keepdims=True)
        dcb_ref[...] += jnp.sum(dy, axis=0, keepdims=True)
        dybuf[0:tm, :] = dy
        dybuf[tm:tm + HALO, :] = jnp.where(t < nt - 1, dyn, jnp.zeros_like(dyn))

        a, sg, hg = _glu(u_ref[...])
        _, _, hh = _glu(uh_ref[...])
        hbuf[0:HALO, :] = jnp.where(t > 0, hh, jnp.zeros_like(hh))
        hbuf[HALO:HALO + tm, :] = hg

        off = HALO - (CONV_K - 1)
        for c0 in range(0, tm, CONV_CHUNK):
            acc = jnp.zeros((CONV_CHUNK, CONV_C), F32)
            dyc = dybuf[c0:c0 + CONV_CHUNK, :]
            for j in range(CONV_K):
                acc = acc + cw_ref[j:j + 1, :] * dybuf[c0 + (CONV_K - 1) - j:c0 + (CONV_K - 1) - j + CONV_CHUNK, :]
                prod = dyc * hbuf[c0 + off + j:c0 + off + j + CONV_CHUNK, :]
                dw_sc[j] += jnp.sum(prod.reshape(CONV_CHUNK // 8, 8, CONV_C), axis=0)
            dhg_sc[c0:c0 + CONV_CHUNK, :] = acc

        dhg = dhg_sc[...]
        du_ref[:, 0:CONV_C] = dhg * sg
        du_ref[:, CONV_C:2 * CONV_C] = dhg * a * sg * (1.0 - sg)

        @pl.when(t == nt - 1)
        def _():
            dcw_ref[...] = jnp.sum(dw_sc[...], axis=1)

    vec = pl.BlockSpec((1, CONV_C), lambda t: (0, 0))
    tile = pl.BlockSpec((tm, CONV_C), lambda t: (t, 0))
    nxt = pl.BlockSpec((HALO, CONV_C), lambda t: (jnp.minimum((t + 1) * nh, S // HALO - 1), 0))
    return _hosted(comm)(
        body, name="conv_bwd",
        grid=(nt,),
        in_specs=[tile, nxt,
                  pl.BlockSpec((tm, 2 * CONV_C), lambda t: (t, 0)),
                  pl.BlockSpec((HALO, 2 * CONV_C), lambda t: (jnp.maximum(t * nh - 1, 0), 0)),
                  tile, nxt,
                  pl.BlockSpec((CONV_K, CONV_C), lambda t: (0, 0)), vec, vec],
        out_specs=[pl.BlockSpec((tm, 2 * CONV_C), lambda t: (t, 0)),
                   pl.BlockSpec((CONV_K, CONV_C), lambda t: (0, 0)), vec, vec, vec],
        out_shape=[jax.ShapeDtypeStruct((S, 2 * CONV_C), F32), jax.ShapeDtypeStruct((CONV_K, CONV_C), F32),
                   jax.ShapeDtypeStruct((1, CONV_C), F32), jax.ShapeDtypeStruct((1, CONV_C), F32),
                   jax.ShapeDtypeStruct((1, CONV_C), F32)],
        scratch_shapes=[pltpu.VMEM((HALO + tm, CONV_C), F32), pltpu.VMEM((tm + HALO, CONV_C), F32),
                        pltpu.VMEM((tm, CONV_C), F32), pltpu.VMEM((CONV_K, 8, CONV_C), F32)],
        compiler_params=_params(1),
    )(dc, dc, u, u, y, y, cw, lg, lb)


def outproj_fwd(x, ao, co, wout, *, tm=512):
    S, D = x.shape

    def body(x_ref, a_ref, c_ref, w_ref, o_ref):
        o_ref[...] = (x_ref[...]
                      + jnp.dot(a_ref[...], w_ref[0:ATT_W, :], preferred_element_type=F32)
                      + jnp.dot(c_ref[...], w_ref[ATT_W:ATT_W + CONV_C, :], preferred_element_type=F32))

    return pl.pallas_call(
        body, name="outproj_fwd",
        grid=(S // tm,),
        in_specs=[pl.BlockSpec((tm, D), lambda i: (i, 0)), pl.BlockSpec((tm, ATT_W), lambda i: (i, 0)),
                  pl.BlockSpec((tm, CONV_C), lambda i: (i, 0)), pl.BlockSpec((D, D), lambda i: (0, 0))],
        out_specs=pl.BlockSpec((tm, D), lambda i: (i, 0)),
        out_shape=jax.ShapeDtypeStruct((S, D), F32),
        compiler_params=_params(1),
    )(x, ao, co, wout)


def outproj_bwd(dx, ao, co, wout, *, tm=512, comm=None):
    S, D = dx.shape

    def body(dx_ref, a_ref, c_ref, w_ref, da_ref, dc_ref, dw_ref):
        @pl.when(pl.program_id(0) == 0)
        def _():
            dw_ref[...] = jnp.zeros_like(dw_ref)

        dxb = dx_ref[...].astype(CDT)
        da_ref[...] = lax.dot_general(dxb, w_ref[0:ATT_W, :], NT, preferred_element_type=F32).astype(CDT)
        dc_ref[...] = lax.dot_general(dxb, w_ref[ATT_W:ATT_W + CONV_C, :], NT, preferred_element_type=F32)
        dw_ref[0:ATT_W, :] += lax.dot_general(a_ref[...], dxb, TN, preferred_element_type=F32)
        dw_ref[ATT_W:ATT_W + CONV_C, :] += lax.dot_general(c_ref[...], dxb, TN, preferred_element_type=F32)

    return _hosted(comm)(
        body, name="outproj_bwd",
        grid=(S // tm,),
        in_specs=[pl.BlockSpec((tm, D), lambda i: (i, 0)), pl.BlockSpec((tm, ATT_W), lambda i: (i, 0)),
                  pl.BlockSpec((tm, CONV_C), lambda i: (i, 0)), pl.BlockSpec((D, D), lambda i: (0, 0))],
        out_specs=[pl.BlockSpec((tm, ATT_W), lambda i: (i, 0)), pl.BlockSpec((tm, CONV_C), lambda i: (i, 0)),
                   pl.BlockSpec((D, D), lambda i: (0, 0))],
        out_shape=[jax.ShapeDtypeStruct((S, ATT_W), CDT), jax.ShapeDtypeStruct((S, CONV_C), F32),
                   jax.ShapeDtypeStruct((D, D), F32)],
        compiler_params=_params(1),
    )(dx, ao, co, wout)


def inproj_bwd(x, g, dres, dq, dk, dv, du, wextT, cs, sn, *, tm=256):
    S, D = x.shape

    def body(x_ref, g_ref, dres_ref, dq_ref, dk_ref, dv_ref, du_ref, w_ref, cs_ref, sn_ref,
             dx_ref, dw_ref, dg_ref, dp_sc):
        @pl.when(pl.program_id(0) == 0)
        def _():
            dw_ref[...] = jnp.zeros_like(dw_ref)
            dg_ref[...] = jnp.zeros_like(dg_ref)

        c, s = cs_ref[...], sn_ref[...]
        for b in range(4):
            dp_sc[:, 128 * b:128 * (b + 1)] = _rope_transpose(dq_ref[:, 128 * b:128 * (b + 1)], c, s).astype(CDT)
        for b in range(2):
            dp_sc[:, 512 + 128 * b:512 + 128 * (b + 1)] = _rope_transpose(
                dk_ref[:, 128 * b:128 * (b + 1)], c, s).astype(CDT)
        dp_sc[:, 768:1024] = dv_ref[...].astype(CDT)
        dp_sc[:, 1024:2048] = du_ref[...].astype(CDT)
        dp = dp_sc[...]
        xhat, r = _rms_parts(x_ref[...])
        gg = g_ref[...]
        h = (xhat * gg).astype(CDT)
        dh = jnp.dot(dp, w_ref[...], preferred_element_type=F32)
        dw_ref[...] += lax.dot_general(dp, h, TN, preferred_element_type=F32)
        dx, dg = _rms_bwd(dh, xhat, r, gg)
        dx_ref[...] = dres_ref[...] + dx
        dg_ref[...] += dg

    def row(w):
        return pl.BlockSpec((tm, w), lambda i: (i, 0))

    return pl.pallas_call(
        body, name="inproj_bwd",
        grid=(S // tm,),
        in_specs=[row(D), pl.BlockSpec((1, D), lambda i: (0, 0)), row(D), row(512), row(256), row(256),
                  row(1024), pl.BlockSpec((D_EXT, D), lambda i: (0, 0)), row(128), row(128)],
        out_specs=[row(D), pl.BlockSpec((D_EXT, D), lambda i: (0, 0)), pl.BlockSpec((1, D), lambda i: (0, 0))],
        out_shape=[jax.ShapeDtypeStruct((S, D), F32), jax.ShapeDtypeStruct((D_EXT, D), F32),
                   jax.ShapeDtypeStruct((1, D), F32)],
        scratch_shapes=[pltpu.VMEM((tm, D_EXT), CDT)],
        compiler_params=_params(1),
    )(x, g, dres, dq, dk, dv, du, wextT, cs, sn)


def _rope_tables(positions):
    inv_freq = 1.0 / (10000.0 ** (jnp.arange(0, HEAD_DIM, 2, dtype=F32) / HEAD_DIM))
    ang = positions.astype(F32).reshape(-1, 1) * inv_freq
    cos, sin = jnp.cos(ang), jnp.sin(ang)
    cs = jnp.tile(jnp.concatenate([cos, cos], axis=-1), (1, 2))
    sn = jnp.tile(jnp.concatenate([-sin, sin], axis=-1), (1, 2))
    return cs, sn


def _widen_w_in(w):
    q, u = w[0:512], w[768:1792]
    parts = [q]
    for base in (512, 576, 640, 704):
        parts += [w[base:base + 64], w[base:base + 64]]
    return jnp.concatenate(parts + [u], axis=0)


def _fold_w_in(d):
    parts = [d[0:512]]
    for base in (512, 640, 768, 896):
        parts.append(d[base:base + 64] + d[base + 64:base + 128])
    return jnp.concatenate(parts + [d[1024:2048]], axis=0)


def add_half(g5, r1, c_idx):
    _, _, r, D = g5.shape

    def body(c_ref, g_ref, r_ref, o_ref):
        o_ref[...] = (g_ref[...] + r_ref[...]).astype(CDT)

    return pl.pallas_call(
        body, name="add_half",
        grid_spec=pltpu.PrefetchScalarGridSpec(
            num_scalar_prefetch=1, grid=(N_CHIPS,),
            in_specs=[pl.BlockSpec((None, None, r, D), lambda s, cr: (s, cr[0], 0, 0)),
                      pl.BlockSpec((None, r, D), lambda s, cr: (s, 0, 0))],
            out_specs=pl.BlockSpec((None, r, D), lambda s, cr: (s, 0, 0))),
        out_shape=jax.ShapeDtypeStruct((N_CHIPS, r, D), CDT),
        compiler_params=_params(1),
    )(c_idx, g5, r1)


def sum_partials(part, recv3, j_idx):
    _, r, D = part.shape
    tr_ = r // 2

    def body(j_ref, p_ref, r_ref, o_ref):
        o_ref[...] = ((p_ref[...].astype(F32) + r_ref[0].astype(F32)) + r_ref[1].astype(F32)) + r_ref[2].astype(F32)

    return pl.pallas_call(
        body, name="sum_partials",
        grid_spec=pltpu.PrefetchScalarGridSpec(
            num_scalar_prefetch=1, grid=(2,),
            in_specs=[pl.BlockSpec((None, tr_, D), lambda i, jr: (jr[0], i, 0)),
                      pl.BlockSpec((3, tr_, D), lambda i, jr: (0, i, 0))],
            out_specs=pl.BlockSpec((tr_, D), lambda i, jr: (i, 0))),
        out_shape=jax.ShapeDtypeStruct((r, D), F32),
        compiler_params=_params(1),
    )(j_idx, part, recv3)


class _Chain:
    STAGES = ("swap", "xchg", "share")

    def __init__(self, grads, c_idx, j_idx):
        self.c_idx, self.j_idx = c_idx, j_idx
        self.g5 = [g.reshape(N_CHIPS, 2, g.shape[0] // (2 * N_CHIPS), g.shape[1]) for g in grads]
        self.stage_no = 0

    @property
    def done(self):
        return self.stage_no == len(self.STAGES)

    def next_stage(self):
        name = self.STAGES[self.stage_no]

        def callback(res):
            getattr(self, "after_" + name)(res)
            self.stage_no += 1

        return getattr(self, name)(), callback

    def swap(self):
        return swap_op(self.g5)

    def after_swap(self, recv):
        self.parts = [add_half(g, r, self.c_idx) for g, r in zip(self.g5, recv)]

    def xchg(self):
        return exchange_op(self.parts)

    def after_xchg(self, recv):
        self.totals = [sum_partials(p, r, self.j_idx) for p, r in zip(self.parts, recv)]

    def share(self):
        return share_op(self.totals)

    def after_share(self, recv):
        both = [_own_slab(h, t, self.c_idx[0]) for h, t in zip(recv, self.totals)]
        self.final = [h.reshape(2 * h.shape[1], h.shape[2]) for h in both]


def all_reduce_small(vec):
    R = vec.shape[0]

    def body(v_ref, o_ref, buf, send, recv):
        x, y, c = _coords()
        me = 4 * x + 2 * y + c
        buf[me] = v_ref[...]
        cps = []
        for m in range(1, N_DEV):
            dx, dy, dc = (m >> 2) & 1, (m >> 1) & 1, m & 1
            cp = pltpu.make_async_remote_copy(v_ref, buf.at[me], send.at[m - 1], recv.at[m - 1],
                                              device_id=((x + dx) % 2, (y + dy) % 2, (c + dc) % 2),
                                              device_id_type=MESH)
            cp.start()
            cps.append(cp)
        for cp in cps:
            cp.wait()
        acc = buf[0]
        for d in range(1, N_DEV):
            acc = acc + buf[d]
        o_ref[...] = acc

    return pl.pallas_call(
        body, name="all_reduce_small",
        in_specs=[pl.BlockSpec(memory_space=pltpu.VMEM)], out_specs=pl.BlockSpec(memory_space=pltpu.VMEM),
        out_shape=jax.ShapeDtypeStruct(vec.shape, F32),
        scratch_shapes=[pltpu.VMEM((N_DEV, R, 128), F32), pltpu.SemaphoreType.DMA((N_DEV - 1,)),
                        pltpu.SemaphoreType.DMA((N_DEV - 1,))],
    )(vec)


def adamw(w, g, m, v, *, tm=512):
    R, C = w.shape
    tm = max(t for t in range(8, min(tm, R) + 1, 8) if R % t == 0)
    c1 =1.0 - ADAM_B1 ** ADAM_STEP
    c2 = 1.0 - ADAM_B2 ** ADAM_STEP

    def body(w_ref, g_ref, m_ref, v_ref, d_ref, nm_ref, nv_ref):
        gg = g_ref[...]
        nm = ADAM_B1 * m_ref[...] + (1.0 - ADAM_B1) * gg
        nv = ADAM_B2 * v_ref[...] + (1.0 - ADAM_B2) * (gg * gg)
        nm_ref[...] = nm
        nv_ref[...] = nv
        d_ref[...] = -ADAM_LR * ((nm / c1) / (jnp.sqrt(nv / c2) + ADAM_EPS) + ADAM_WD * w_ref[...])

    blk = pl.BlockSpec((tm, C), lambda i: (i, 0))
    return pl.pallas_call(
        body, name="adamw",
        grid=(pl.cdiv(R, tm),),
        in_specs=[blk] * 4, out_specs=[blk] * 3,
        out_shape=[jax.ShapeDtypeStruct((R, C), F32)] * 3,
        compiler_params=_params(1),
    )(w, g, m, v)


_SMALL = (("n1", (2, D_MODEL)), ("nm", (2, D_MODEL)), ("n2", (2, D_MODEL)), ("nf", (D_MODEL,)),
          ("cb", (2, CONV_C)), ("lg", (2, CONV_C)), ("lb", (2, CONV_C)), ("sinks", (2, N_HEADS)),
          ("cw", (2, CONV_K, CONV_C)))


def _pack(parts, rows):
    flat = jnp.concatenate([p.reshape(-1).astype(F32) for p in parts])
    return jnp.pad(flat, (0, rows * 128 - flat.shape[0])).reshape(rows, 128)


def _unpack(block, shapes):
    flat = block.reshape(-1)
    out, o = [], 0
    for shp in shapes:
        n = 1
        for s in shp:
            n *= s
        out.append(flat[o:o + n].reshape(shp))
        o += n
    return out


def kernel(x, positions, ffn1_norm, ffn1_w_gate, ffn1_w_up, ffn1_w_down, mix_norm, w_in, conv_w, conv_b, conv_ln_g, conv_ln_b, attn_sinks, w_out, ffn2_norm, ffn2_w_gate, ffn2_w_up, ffn2_w_down, final_norm, loss_target, m_ffn1_norm, m_ffn1_w_gate, m_ffn1_w_up, m_ffn1_w_down, m_mix_norm, m_w_in, m_conv_w, m_conv_b, m_conv_ln_g, m_conv_ln_b, m_attn_sinks, m_w_out, m_ffn2_norm, m_ffn2_w_gate, m_ffn2_w_up, m_ffn2_w_down, m_final_norm, v_ffn1_norm, v_ffn1_w_gate, v_ffn1_w_up, v_ffn1_w_down, v_mix_norm, v_w_in, v_conv_w, v_conv_b, v_conv_ln_g, v_conv_ln_b, v_attn_sinks, v_w_out, v_ffn2_norm, v_ffn2_w_gate, v_ffn2_w_up, v_ffn2_w_down, v_final_norm):
    cx, cy, cc = _coords()
    chip = 2 * cx + cy
    c_idx = jnp.reshape(cc, (1,)).astype(jnp.int32)
    j_idx = jnp.reshape(chip, (1,)).astype(jnp.int32)
    L = ffn1_norm.shape[0]
    tr = lambda a: jnp.swapaxes(a, 1, 2)

    sh = dict(f1g=tr(ffn1_w_gate), f1u=tr(ffn1_w_up), f1d=ffn1_w_down, f2g=tr(ffn2_w_gate),
              f2u=tr(ffn2_w_up), f2d=ffn2_w_down, win=tr(w_in), wout=w_out)
    sh = {k: [v[l].astype(CDT) for l in range(L)] for k, v in sh.items()}
    W = {}

    def gather_op(keys):
        return ag_op([conv_w if k == "cw" else sh[k[0]][k[1]] for k in keys])

    def take(keys, res):
        for k, a in zip(keys, res):
            if k == "cw":
                W[k] = _own_slab(a, conv_w, chip)
            else:
                W[k] = _own_slab(a, sh[k[0]][k[1]], chip).reshape(N_CHIPS * a.shape[1], a.shape[2])

    def with_ag(fn, keys, *args):
        if not keys:
            return fn(*args)
        main, res = fn(*args, comm=gather_op(keys))
        take(keys, res)
        return main

    ag_hosts = {("ffn1", 0): [("win", 0), "cw", ("f2g", 0), ("f1d", 1)],
                ("inproj", 0): [("wout", 0)], ("attn", 0): [("f2u", 0)], ("conv", 0): [("f2d", 0)],
                ("ffn2", 0): [("f1g", 1), ("f1u", 1)],
                ("ffn1", 1): [("win", 1), ("f2g", 1), ("f2u", 1)],
                ("inproj", 1): [("wout", 1)], ("attn", 1): [("f2d", 1)]}
    for k_, a in zip(("f1g", "f1u", "f1d"), first_gather([sh[k_][0] for k_ in ("f1g", "f1u", "f1d")])):
        own = sh[k_][0].reshape(a.shape[1:])
        W[(k_, 0)] = _own_slab(a, own, chip).reshape(N_CHIPS * sh[k_][0].shape[0], sh[k_][0].shape[1])

    cs, sn = _rope_tables(positions)
    saved = []
    h = x[0]
    for l in range(L):
        sink = attn_sinks[l].reshape(2, 4)
        sink_col = jnp.repeat(sink, BLOCK, axis=1).reshape(2, 4 * BLOCK, 1)
        x0 = h
        x1, g1, u1 = with_ag(ffn_fwd, ag_hosts.get(("ffn1", l)), x0, ffn1_norm[l][None],
                             W[("f1g", l)], W[("f1u", l)], W[("f1d", l)])
        wext = _widen_w_in(W[("win", l)])
        q, k, v, u = with_ag(inproj_fwd, ag_hosts.get(("inproj", l)), x1, mix_norm[l][None], wext, cs, sn)
        ao = with_ag(attn_fwd, ag_hosts.get(("attn", l)), q, k, v, sink_col)
        cwl = jnp.transpose(W["cw"][:, l], (1, 0, 2)).reshape(CONV_K, CONV_C)
        co, yc = with_ag(conv_fwd, ag_hosts.get(("conv", l)), u, cwl, conv_b[l][None], conv_ln_g[l][None],
                         conv_ln_b[l][None])
        x2 = outproj_fwd(x1, ao, co, W[("wout", l)])
        x3, g2, u2 = with_ag(ffn_fwd, ag_hosts.get(("ffn2", l)), x2, ffn2_norm[l][None],
                             W[("f2g", l)], W[("f2u", l)], W[("f2d", l)])
        saved.append((x0, x1, x2, g1, u1, g2, u2, q, k, v, u, ao, co, yc, sink, wext, cwl))
        h = x3

    loss, dx, dnf = loss_head(h, final_norm[None], loss_target[0])

    active = []

    def advance(run):
        stages = [ch.next_stage() for ch in active]
        ops = [op for op, _ in stages]
        main, res = run(_merge(*ops) if ops else None)
        for (_, cb), r in zip(stages, _split(res, *ops)):
            cb(r)
        active[:] = [ch for ch in active if not ch.done]
        return main

    def hosted(fn, *args):
        def run(comm):
            if comm is None:
                return fn(*args), []
            return fn(*args, comm=comm)
        return advance(run)

    def chain(key, grads):
        chains[key] = _Chain(grads, c_idx, j_idx)
        active.append(chains[key])

    small = {k_: [None] * L for k_ in ("n1", "nm", "n2", "cw", "cb", "lg", "lb", "sinks")}
    chains = {}
    for l in reversed(range(L)):
        x0, x1, x2, g1, u1, g2, u2, q, k, v, u, ao, co, yc, sink, wext, cwl = saved[l]
        sink_row = jnp.repeat(sink, BLOCK, axis=1).reshape(2, 1, 4 * BLOCK)
        chain(("f2d", l), [hosted(ffn_wgrad_down, dx, g2, u2)])
        dx2, dgt, dup, hh, small["n2"][l] = hosted(
            ffn_bwd_dgrad, x2, ffn2_norm[l][None], dx, g2, u2, W[("f2g", l)], W[("f2u", l)], W[("f2d", l)])
        chain(("f2gu", l), hosted(ffn_wgrad_gate_up, hh, dgt, dup))
        da, dc, gwout = hosted(outproj_bwd, dx2, ao, co, W[("wout", l)])
        du, small["cw"][l], small["cb"][l], small["lg"][l], small["lb"][l] = hosted(
            conv_bwd, dc, u, yc, cwl, conv_ln_g[l][None], conv_ln_b[l][None])
        dq, dk, dv, dsink = hosted(attn_bwd, q, k, v, da, sink_row)
        small["sinks"][l] = jnp.sum(dsink.reshape(2, 4, BLOCK), axis=-1).reshape(N_HEADS)
        dx1, gwext, small["nm"][l] = inproj_bwd(x1, mix_norm[l][None], dx2, dq, dk, dv, du, wext, cs, sn)
        chain(("mx", l), [gwout, _fold_w_in(gwext)])
        chain(("f1d", l), [hosted(ffn_wgrad_down, dx1, g1, u1)])
        dx, dgt, dup, hh, small["n1"][l] = hosted(
            ffn_bwd_dgrad, x0, ffn1_norm[l][None], dx1, g1, u1, W[("f1g", l)], W[("f1u", l)], W[("f1d", l)])
        chain(("f1gu", l), hosted(ffn_wgrad_gate_up, hh, dgt, dup))

    while active:
        advance(lambda comm: (None, _run_comm(comm)))

    G = {k_: jnp.stack(v_) for k_, v_ in small.items()}
    G["nf"] = dnf
    small_shapes = [shp for _, shp in _SMALL]
    n_small = 1 + sum(math.prod(s) for s in small_shapes)
    rows = -(-n_small // 1024) * 8
    packed = _pack([loss] + [G[k_] for k_, _ in _SMALL], rows)
    summed = all_reduce_small(packed)
    loss_out, *small_sum = _unpack(summed, [()] + small_shapes)
    gs = dict(zip([k_ for k_, _ in _SMALL], small_sum))
    gs["cw"] = lax.dynamic_slice_in_dim(gs["cw"], chip * (CONV_C // N_CHIPS), CONV_C // N_CHIPS, axis=2)

    def big(group, idx, transpose):
        per_layer = [chains[(group, l)].final[idx] for l in range(L)]
        return jnp.stack([a.T if transpose else a for a in per_layer])

    grads = dict(ffn1_norm=gs["n1"], ffn1_w_gate=big("f1gu", 0, True), ffn1_w_up=big("f1gu", 1, True),
                 ffn1_w_down=big("f1d", 0, False), mix_norm=gs["nm"], w_in=big("mx", 1, True), conv_w=gs["cw"],
                 conv_b=gs["cb"], conv_ln_g=gs["lg"], conv_ln_b=gs["lb"], attn_sinks=gs["sinks"],
                 w_out=big("mx", 0, False), ffn2_norm=gs["n2"], ffn2_w_gate=big("f2gu", 0, True),
                 ffn2_w_up=big("f2gu", 1, True), ffn2_w_down=big("f2d", 0, False), final_norm=gs["nf"])

    weights = dict(ffn1_norm=ffn1_norm, ffn1_w_gate=ffn1_w_gate, ffn1_w_up=ffn1_w_up, ffn1_w_down=ffn1_w_down,
                   mix_norm=mix_norm, w_in=w_in, conv_w=conv_w, conv_b=conv_b, conv_ln_g=conv_ln_g,
                   conv_ln_b=conv_ln_b, attn_sinks=attn_sinks, w_out=w_out, ffn2_norm=ffn2_norm,
                   ffn2_w_gate=ffn2_w_gate, ffn2_w_up=ffn2_w_up, ffn2_w_down=ffn2_w_down, final_norm=final_norm)
    moms = dict(ffn1_norm=(m_ffn1_norm, v_ffn1_norm), ffn1_w_gate=(m_ffn1_w_gate, v_ffn1_w_gate),
                ffn1_w_up=(m_ffn1_w_up, v_ffn1_w_up), ffn1_w_down=(m_ffn1_w_down, v_ffn1_w_down),
                mix_norm=(m_mix_norm, v_mix_norm), w_in=(m_w_in, v_w_in), conv_w=(m_conv_w, v_conv_w),
                conv_b=(m_conv_b, v_conv_b), conv_ln_g=(m_conv_ln_g, v_conv_ln_g),
                conv_ln_b=(m_conv_ln_b, v_conv_ln_b), attn_sinks=(m_attn_sinks, v_attn_sinks),
                w_out=(m_w_out, v_w_out), ffn2_norm=(m_ffn2_norm, v_ffn2_norm),
                ffn2_w_gate=(m_ffn2_w_gate, v_ffn2_w_gate), ffn2_w_up=(m_ffn2_w_up, v_ffn2_w_up),
                ffn2_w_down=(m_ffn2_w_down, v_ffn2_w_down), final_norm=(m_final_norm, v_final_norm))
    names = list(weights)
    big_names = ("ffn1_w_gate", "ffn1_w_up", "ffn1_w_down", "w_in", "w_out", "ffn2_w_gate", "ffn2_w_up",
                 "ffn2_w_down")
    delta, new_m, new_v = {}, {}, {}
    for nme in big_names:
        shp = weights[nme].shape
        two = lambda a: a.reshape(shp[0] * shp[1], shp[2])
        d, nm_, nv_ = adamw(two(weights[nme]), two(grads[nme]), two(moms[nme][0]), two(moms[nme][1]))
        delta[nme], new_m[nme], new_v[nme] = d.reshape(shp), nm_.reshape(shp), nv_.reshape(shp)
    small_names = [nme for nme in names if nme not in big_names]
    s_shapes = [weights[nme].shape for nme in small_names]
    n_tot = sum(math.prod(s) for s in s_shapes)
    srows = -(-n_tot // 1024) * 8
    d, nm_, nv_ = adamw(_pack([weights[nme] for nme in small_names], srows),
                        _pack([grads[nme] for nme in small_names], srows),
                        _pack([moms[nme][0] for nme in small_names], srows),
                        _pack([moms[nme][1] for nme in small_names], srows))
    for nme, dd, mm, vv in zip(small_names, _unpack(d, s_shapes), _unpack(nm_, s_shapes), _unpack(nv_, s_shapes)):
        delta[nme], new_m[nme], new_v[nme] = dd, mm, vv

    return (loss_out, dx[None], *[grads[nme] for nme in names], *[delta[nme] for nme in names],
            *[new_m[nme] for nme in names], *[new_v[nme] for nme in names])
```

```python
import math

import jax
import jax.numpy as jnp
from jax import lax
from jax.experimental import pallas as pl
from jax.experimental.pallas import tpu as pltpu

F32 = jnp.float32
CDT = jnp.bfloat16
D_MODEL = 1024
D_FF = 2816
N_HEADS = 8
HEAD_DIM = 64
BLOCK = 128
CONV_K = 31
CONV_C = 512
ATT_W = 512
D_EXT = 2048
EPS = 1e-5
HALO = 32
FF_CHUNK = 256
NEG = float(jnp.finfo(jnp.float32).min)
VMEM_LIMIT = 56 * 1024 * 1024

ADAM_LR = 0.001
ADAM_B1 = 0.9
ADAM_B2 = 0.999
ADAM_EPS = 1e-08
ADAM_WD = 0.01
ADAM_STEP = 10

NT = (((1,), (1,)), ((), ()))
TN = (((0,), (0,)), ((), ()))


MESH = pl.DeviceIdType.MESH
ANY = pl.BlockSpec(memory_space=pl.ANY)
N_CHIPS = 4
N_DEV = 8


def _params(n_axes):
    return pltpu.CompilerParams(dimension_semantics=("arbitrary",) * n_axes, vmem_limit_bytes=VMEM_LIMIT)


class _Comm:
    def __init__(self, name, inputs, out_shape, sems, descs):
        self.name, self.inputs, self.out_shape, self.sems, self.descs = name, list(inputs), list(out_shape), list(sems), descs


def _merge(*ops):
    ops = [o for o in ops if o is not None]
    if len(ops) == 1:
        return ops[0]

    def descs(cins, couts, sems):
        out, i, o, s = [], 0, 0, 0
        for op in ops:
            ni, no, ns = len(op.inputs), len(op.out_shape), len(op.sems)
            out += op.descs(cins[i:i + ni], couts[o:o + no], sems[s:s + ns])
            i, o, s = i + ni, o + no, s + ns
        return out

    return _Comm("_".join(o.name for o in ops), sum((o.inputs for o in ops), []),
                 sum((o.out_shape for o in ops), []), sum((o.sems for o in ops), []), descs)


def _split(couts, *ops):
    res, o = [], 0
    for op in ops:
        res.append(couts[o:o + len(op.out_shape)])
        o += len(op.out_shape)
    return res


def _hosted(comm):
    if comm is None:
        return pl.pallas_call

    def make(body, *, name, grid, in_specs, out_specs, out_shape, compiler_params, scratch_shapes=()):
        single = not isinstance(out_shape, (list, tuple))
        o_specs = [out_specs] if single else list(out_specs)
        o_shape = [out_shape] if single else list(out_shape)
        n_in, n_out, n_sc = len(in_specs), len(o_specs), len(scratch_shapes)
        c_in, c_out = len(comm.inputs), len(comm.out_shape)

        def hosted(*refs):
            ins, cins = refs[:n_in], refs[n_in:n_in + c_in]
            o0 = n_in + c_in
            outs, couts = refs[o0:o0 + n_out], refs[o0 + n_out:o0 + n_out + c_out]
            s0 = o0 + n_out + c_out
            scr, sems = refs[s0:s0 + n_sc], refs[s0 + n_sc:]
            first = pl.program_id(0) == 0
            last = pl.program_id(0) == grid[0] - 1
            for ax in range(1, len(grid)):
                first = first & (pl.program_id(ax) == 0)
                last = last & (pl.program_id(ax) == grid[ax] - 1)

            @pl.when(first)
            def _():
                for d in comm.descs(cins, couts, sems):
                    d.start()

            body(*ins, *outs, *scr)

            @pl.when(last)
            def _():
                for d in comm.descs(cins, couts, sems):
                    d.wait()

        call = pl.pallas_call(
            hosted, name=f"{name}_{comm.name}", grid=grid,
            in_specs=list(in_specs) + [ANY] * c_in, out_specs=o_specs + [ANY] * c_out,
            out_shape=o_shape + comm.out_shape, scratch_shapes=list(scratch_shapes) + comm.sems,
            compiler_params=compiler_params)

        def run(*args):
            res = call(*args, *comm.inputs)
            return (res[0] if single else list(res[:n_out])), list(res[n_out:])

        return run

    return make


def _run_comm(comm):
    c_in = len(comm.inputs)

    def body(*refs):
        cins, couts, sems = refs[:c_in], refs[c_in:c_in + len(comm.out_shape)], refs[c_in + len(comm.out_shape):]
        ds = comm.descs(cins, couts, sems)
        for d in ds:
            d.start()
        for d in ds:
            d.wait()

    return list(pl.pallas_call(
        body, name=comm.name, in_specs=[ANY] * c_in, out_specs=[ANY] * len(comm.out_shape),
        out_shape=comm.out_shape, scratch_shapes=comm.sems)(*comm.inputs))


def _coords():
    return lax.axis_index("x"), lax.axis_index("y"), lax.axis_index("c")


def _other_chips(x, y):
    return [(1 - x, y), (x, 1 - y), (1 - x, 1 - y)]


def ag_op(shards):
    n = len(shards)

    def descs(cins, couts, sems):
        send, recv = sems
        x, y, c = _coords()
        j = 2 * x + y
        ds = []
        for a in range(n):
            for p, (px, py) in enumerate(_other_chips(x, y)):
                ds.append(pltpu.make_async_remote_copy(cins[a], couts[a].at[j], send.at[a, p], recv.at[a, p],
                                                       device_id=(px, py, c), device_id_type=MESH))
        return ds

    return _Comm("ag", shards, [jax.ShapeDtypeStruct((N_CHIPS,) + s.shape, s.dtype) for s in shards],
                 [pltpu.SemaphoreType.DMA((n, 3)), pltpu.SemaphoreType.DMA((n, 3))], descs)


def _own_slab(gathered, mine, idx):
    return lax.dynamic_update_slice_in_dim(gathered, mine[None], idx, axis=0)


def first_gather(shards):
    n = len(shards)
    halves = [s.reshape(2, s.shape[0] // 2, s.shape[1]) for s in shards]

    def body(*refs):
        ins, outs = refs[:n], refs[n:2 * n]
        send1, recv1, send2, recv2 = refs[2 * n:]
        x, y, c = _coords()
        j = 2 * x + y
        chips = _other_chips(x, y)
        ici = [pltpu.make_async_remote_copy(ins[a].at[c], outs[a].at[j, c], send1.at[a, p], recv1.at[a, p],
                                            device_id=(px, py, c), device_id_type=MESH)
               for a in range(n) for p, (px, py) in enumerate(chips)]
        for d in ici:
            d.start()
        for d in ici:
            d.wait()
        d2d = [pltpu.make_async_remote_copy(outs[a].at[2 * px + py, c], outs[a].at[2 * px + py, c],
                                            send2.at[a, p], recv2.at[a, p],
                                            device_id=(x, y, 1 - c), device_id_type=MESH)
               for a in range(n) for p, (px, py) in enumerate(chips)]
        for d in d2d:
            d.start()
        for d in d2d:
            d.wait()

    return list(pl.pallas_call(
        body, name="first_gather", in_specs=[ANY] * n, out_specs=[ANY] * n,
        out_shape=[jax.ShapeDtypeStruct((N_CHIPS,) + h.shape, h.dtype) for h in halves],
        scratch_shapes=[pltpu.SemaphoreType.DMA((n, 3))] * 4)(*halves))


def swap_op(grads):
    n = len(grads)

    def descs(cins, couts, sems):
        send, recv = sems
        x, y, c = _coords()
        return [pltpu.make_async_remote_copy(cins[a].at[:, 1 - c], couts[a], send.at[a], recv.at[a],
                                             device_id=(x, y, 1 - c), device_id_type=MESH) for a in range(n)]

    return _Comm("swap", grads, [jax.ShapeDtypeStruct(g.shape[:1] + g.shape[2:], g.dtype) for g in grads],
                 [pltpu.SemaphoreType.DMA((n,)), pltpu.SemaphoreType.DMA((n,))], descs)


def exchange_op(parts):
    n = len(parts)

    def descs(cins, couts, sems):
        send, recv = sems
        x, y, c = _coords()
        ds = []
        for a in range(n):
            for p, (px, py) in enumerate(_other_chips(x, y)):
                ds.append(pltpu.make_async_remote_copy(cins[a].at[2 * px + py], couts[a].at[p], send.at[a, p],
                                                       recv.at[a, p], device_id=(px, py, c), device_id_type=MESH))
        return ds

    return _Comm("xchg", parts, [jax.ShapeDtypeStruct((3,) + p.shape[1:], p.dtype) for p in parts],
                 [pltpu.SemaphoreType.DMA((n, 3)), pltpu.SemaphoreType.DMA((n, 3))], descs)


def share_op(totals):
    n = len(totals)

    def descs(cins, couts, sems):
        send, recv = sems
        x, y, c = _coords()
        return [pltpu.make_async_remote_copy(cins[a], couts[a].at[c], send.at[a], recv.at[a],
                                             device_id=(x, y, 1 - c), device_id_type=MESH) for a in range(n)]

    return _Comm("share", totals, [jax.ShapeDtypeStruct((2,) + t.shape, t.dtype) for t in totals],
                 [pltpu.SemaphoreType.DMA((n,)), pltpu.SemaphoreType.DMA((n,))], descs)


def _sigmoid(z):
    return 1.0 / (1.0 + jnp.exp(-z))


def _rms_parts(xf):
    r = lax.rsqrt(jnp.mean(xf * xf, axis=-1, keepdims=True) + EPS)
    return xf * r, r


def _rms_bwd(dh, xhat, r, g):
    dg = jnp.sum(dh * xhat, axis=0, keepdims=True)
    dxhat = dh * g
    dx = r * (dxhat - xhat * jnp.mean(dxhat * xhat, axis=-1, keepdims=True))
    return dx, dg


def _chunks(n, ck=FF_CHUNK):
    return [(c0, min(ck, n - c0)) for c0 in range(0, n, ck)]


def ffn_fwd(x, g, wgT, wuT, wd, *, tm=512, comm=None):
    S, D = x.shape
    F = wgT.shape[0]

    def body(x_ref, g_ref, wg_ref, wu_ref, wd_ref, o_ref, gate_ref, up_ref, a_sc):
        xf = x_ref[...]
        xhat, _ = _rms_parts(xf)
        h = (xhat * g_ref[...]).astype(CDT)
        for c0, cw_ in _chunks(F):
            sl = slice(c0, c0 + cw_)
            gt = lax.dot_general(h, wg_ref[sl, :], NT, preferred_element_type=F32)
            ut = lax.dot_general(h, wu_ref[sl, :], NT, preferred_element_type=F32)
            gate_ref[:, sl] = gt.astype(CDT)
            up_ref[:, sl] = ut.astype(CDT)
            a_sc[:, sl] = (gt * _sigmoid(gt) * ut).astype(CDT)
        o_ref[...] = xf + 0.5 * jnp.dot(a_sc[...], wd_ref[...], preferred_element_type=F32)

    wspec = pl.BlockSpec((F, D), lambda i: (0, 0), pipeline_mode=pl.Buffered(1))
    return _hosted(comm)(
        body, name="ffn_fwd",
        grid=(S // tm,),
        in_specs=[pl.BlockSpec((tm, D), lambda i: (i, 0)), pl.BlockSpec((1, D), lambda i: (0, 0)),
                  wspec, wspec, wspec],
        out_specs=[pl.BlockSpec((tm, D), lambda i: (i, 0)),
                   pl.BlockSpec((tm, F), lambda i: (i, 0)),
                   pl.BlockSpec((tm, F), lambda i: (i, 0))],
        out_shape=[jax.ShapeDtypeStruct((S, D), F32),
                   jax.ShapeDtypeStruct((S, F), CDT),
                   jax.ShapeDtypeStruct((S, F), CDT)],
        scratch_shapes=[pltpu.VMEM((tm, F), CDT)],
        compiler_params=_params(1),
    )(x, g, wgT, wuT, wd)


def ffn_bwd_dgrad(x, g, dy, gate, up, wgT, wuT, wd, *, tm=512, tf=1408, comm=None):
    S, D = x.shape
    F = wgT.shape[0]
    nf = F // tf

    def body(x_ref, g_ref, dy_ref, gate_ref, up_ref, wg_ref, wu_ref, wd_ref,
             dx_ref, dgate_ref, dup_ref, h_ref, dg_ref, dh_sc, dacc_sc):
        i = pl.program_id(0)
        j = pl.program_id(1)

        @pl.when(j == 0)
        def _():
            xhat, _ = _rms_parts(x_ref[...])
            h_ref[...] = (xhat * g_ref[...]).astype(CDT)
            dacc_sc[...] = (0.5 * dy_ref[...]).astype(CDT)
            dh_sc[...] = jnp.zeros_like(dh_sc)

        @pl.when((i == 0) & (j == 0))
        def _():
            dg_ref[...] = jnp.zeros_like(dg_ref)

        dacc = dacc_sc[...]
        for c0, cw_ in _chunks(tf):
            sl = slice(c0, c0 + cw_)
            d_a = lax.dot_general(dacc, wd_ref[sl, :], NT, preferred_element_type=F32)
            gt = gate_ref[:, sl].astype(F32)
            ut = up_ref[:, sl].astype(F32)
            sg = _sigmoid(gt)
            dup_ref[:, sl] = (d_a * (gt * sg)).astype(CDT)
            dgate_ref[:, sl] = (d_a * ut * (sg * (1.0 + gt * (1.0 - sg)))).astype(CDT)
        dh_sc[...] += (jnp.dot(dgate_ref[...], wg_ref[...], preferred_element_type=F32)
                       + jnp.dot(dup_ref[...], wu_ref[...], preferred_element_type=F32))

        @pl.when(j == nf - 1)
        def _():
            xhat, r = _rms_parts(x_ref[...])
            dx, dg = _rms_bwd(dh_sc[...], xhat, r, g_ref[...])
            dx_ref[...] = dy_ref[...] + dx
            dg_ref[...] += dg

    return _hosted(comm)(
        body, name="ffn_bwd_dgrad",
        grid=(S // tm, nf),
        in_specs=[pl.BlockSpec((tm, D), lambda i, j: (i, 0)),
                  pl.BlockSpec((1, D), lambda i, j: (0, 0)),
                  pl.BlockSpec((tm, D), lambda i, j: (i, 0)),
                  pl.BlockSpec((tm, tf), lambda i, j: (i, j)),
                  pl.BlockSpec((tm, tf), lambda i, j: (i, j)),
                  pl.BlockSpec((tf, D), lambda i, j: (j, 0)),
                  pl.BlockSpec((tf, D), lambda i, j: (j, 0)),
                  pl.BlockSpec((tf, D), lambda i, j: (j, 0))],
        out_specs=[pl.BlockSpec((tm, D), lambda i, j: (i, 0)),
                   pl.BlockSpec((tm, tf), lambda i, j: (i, j)),
                   pl.BlockSpec((tm, tf), lambda i, j: (i, j)),
                   pl.BlockSpec((tm, D), lambda i, j: (i, 0)),
                   pl.BlockSpec((1, D), lambda i, j: (0, 0))],
        out_shape=[jax.ShapeDtypeStruct((S, D), F32),
                   jax.ShapeDtypeStruct((S, F), CDT),
                   jax.ShapeDtypeStruct((S, F), CDT),
                   jax.ShapeDtypeStruct((S, D), CDT),
                   jax.ShapeDtypeStruct((1, D), F32)],
        scratch_shapes=[pltpu.VMEM((tm, D), F32), pltpu.VMEM((tm, D), CDT)],
        compiler_params=_params(2),
    )(x, g, dy, gate, up, wgT, wuT, wd)


def ffn_wgrad_down(dy, gate, up, *, tk=512, comm=None):
    S, D = dy.shape
    F = gate.shape[1]
    tk = min(tk, S)

    def body(dy_ref, gate_ref, up_ref, dwd_ref):
        @pl.when(pl.program_id(0) == 0)
        def _():
            dwd_ref[...] = jnp.zeros_like(dwd_ref)

        dacc = (0.5 * dy_ref[...]).astype(CDT)
        for c0, cw_ in _chunks(F):
            sl = slice(c0, c0 + cw_)
            gt = gate_ref[:, sl].astype(F32)
            a = (gt * _sigmoid(gt) * up_ref[:, sl].astype(F32)).astype(CDT)
            dwd_ref[sl, :] += lax.dot_general(a, dacc, TN, preferred_element_type=F32)

    act = pl.BlockSpec((tk, F), lambda k: (k, 0))
    return _hosted(comm)(
        body, name="ffn_wgrad_down",
        grid=(S // tk,),
        in_specs=[pl.BlockSpec((tk, D), lambda k: (k, 0)), act, act],
        out_specs=pl.BlockSpec((F, D), lambda k: (0, 0)),
        out_shape=jax.ShapeDtypeStruct((F, D), F32),
        compiler_params=_params(1),
    )(dy, gate, up)


def ffn_wgrad_gate_up(h, dgate, dup, *, tk=1024, tf=1408, comm=None):
    S, D = h.shape
    F = dgate.shape[1]
    tk = min(tk, S)

    def body(h_ref, dgate_ref, dup_ref, dwg_ref, dwu_ref):
        k = pl.program_id(1)

        @pl.when(k == 0)
        def _():
            dwg_ref[...] = jnp.zeros_like(dwg_ref)
            dwu_ref[...] = jnp.zeros_like(dwu_ref)

        hh = h_ref[...]
        for c0, cw_ in _chunks(tf):
            sl = slice(c0, c0 + cw_)
            dwg_ref[sl, :] += lax.dot_general(dgate_ref[:, sl], hh, TN, preferred_element_type=F32)
            dwu_ref[sl, :] += lax.dot_general(dup_ref[:, sl], hh, TN, preferred_element_type=F32)

    tok = pl.BlockSpec((tk, D), lambda j, k: (k, 0))
    act = pl.BlockSpec((tk, tf), lambda j, k: (k, j))
    out = pl.BlockSpec((tf, D), lambda j, k: (j, 0))
    return _hosted(comm)(
        body, name="ffn_wgrad_gate_up",
        grid=(F // tf, S // tk),
        in_specs=[tok, act, act],
        out_specs=[out, out],
        out_shape=[jax.ShapeDtypeStruct((F, D), F32)] * 2,
        compiler_params=_params(2),
    )(h, dgate, dup)


def loss_head(x, g, target, *, tm=512):
    S, D = x.shape

    def body(x_ref, g_ref, t_ref, loss_ref, dx_ref, dg_ref):
        @pl.when(pl.program_id(0) == 0)
        def _():
            loss_ref[...] = jnp.zeros_like(loss_ref)
            dg_ref[...] = jnp.zeros_like(dg_ref)

        xhat, r = _rms_parts(x_ref[...])
        gg = g_ref[...]
        err = xhat * gg - t_ref[...]
        loss_ref[...] += 0.5 * jnp.sum(jnp.mean(err * err, axis=-1, keepdims=True), axis=0, keepdims=True)
        dx, dg = _rms_bwd(err * (1.0 / D), xhat, r, gg)
        dx_ref[...] = dx
        dg_ref[...] += dg

    row = pl.BlockSpec((tm, D), lambda i: (i, 0))
    vec = pl.BlockSpec((1, D), lambda i: (0, 0))
    return pl.pallas_call(
        body, name="loss_head",
        grid=(S // tm,),
        in_specs=[row, vec, row],
        out_specs=[pl.BlockSpec((1, 1), lambda i: (0, 0)), row, vec],
        out_shape=[jax.ShapeDtypeStruct((1, 1), F32), jax.ShapeDtypeStruct((S, D), F32),
                   jax.ShapeDtypeStruct((1, D), F32)],
        compiler_params=_params(1),
    )(x, g, target)


def _rope_apply(t, cs, sn):
    lane = lax.broadcasted_iota(jnp.int32, t.shape, 1)
    first = (lane % HEAD_DIM) < (HEAD_DIM // 2)
    rot = jnp.where(first, pltpu.roll(t, 128 - HEAD_DIM // 2, 1), pltpu.roll(t, HEAD_DIM // 2, 1))
    return t * cs + rot * sn


def _rope_transpose(d, cs, sn):
    lane = lax.broadcasted_iota(jnp.int32, d.shape, 1)
    first = (lane % HEAD_DIM) < (HEAD_DIM // 2)
    ds = d * sn
    rot = jnp.where(first, pltpu.roll(ds, 128 - HEAD_DIM // 2, 1), pltpu.roll(ds, HEAD_DIM // 2, 1))
    return d * cs + rot


def inproj_fwd(x, g, wextT, cs, sn, *, tm=512, comm=None):
    S, D = x.shape
    scale = HEAD_DIM ** -0.5

    def body(x_ref, g_ref, w_ref, cs_ref, sn_ref, q_ref, k_ref, v_ref, u_ref):
        xhat, _ = _rms_parts(x_ref[...])
        h = (xhat * g_ref[...]).astype(CDT)
        p = lax.dot_general(h, w_ref[...], NT, preferred_element_type=F32)
        c, s = cs_ref[...], sn_ref[...]
        for b in range(4):
            q_ref[:, 128 * b:128 * (b + 1)] = (_rope_apply(p[:, 128 * b:128 * (b + 1)], c, s) * scale).astype(CDT)
        for b in range(2):
            k_ref[:, 128 * b:128 * (b + 1)] = _rope_apply(p[:, 512 + 128 * b:512 + 128 * (b + 1)], c, s).astype(CDT)
        v_ref[...] = p[:, 768:1024].astype(CDT)
        u_ref[...] = p[:, 1024:2048]

    def row(w):
        return pl.BlockSpec((tm, w), lambda i: (i, 0))

    return _hosted(comm)(
        body, name="inproj_fwd",
        grid=(S // tm,),
        in_specs=[row(D), pl.BlockSpec((1, D), lambda i: (0, 0)),
                  pl.BlockSpec((D_EXT, D), lambda i: (0, 0)), row(128), row(128)],
        out_specs=[row(512), row(256), row(256), row(1024)],
        out_shape=[jax.ShapeDtypeStruct((S, 512), CDT), jax.ShapeDtypeStruct((S, 256), CDT),
                   jax.ShapeDtypeStruct((S, 256), CDT), jax.ShapeDtypeStruct((S, 1024), F32)],
        compiler_params=_params(1),
    )(x, g, wextT, cs, sn)


def _stack_heads(p0, p1):
    lane = lax.broadcasted_iota(jnp.int32, p0.shape, 1)
    lo = lane < HEAD_DIM
    z = jnp.zeros_like(p0)
    return jnp.concatenate([jnp.where(lo, p0, z), jnp.where(lo, z, p0),
                            jnp.where(lo, p1, z), jnp.where(lo, z, p1)], axis=0)


def _unstack_heads(o):
    lane = lax.broadcasted_iota(jnp.int32, (BLOCK, 128), 1)
    lo = lane < HEAD_DIM
    return (jnp.where(lo, o[0:128], o[128:256]), jnp.where(lo, o[256:384], o[384:512]))


def _band_mask_kq(n):
    c = lax.broadcasted_iota(jnp.int32, (2 * BLOCK, 4 * BLOCK), 0)
    i = lax.broadcasted_iota(jnp.int32, (2 * BLOCK, 4 * BLOCK), 1) % BLOCK
    return (c > i) & (c <= i + BLOCK) & ((n > 0) | (c >= BLOCK))


def attn_fwd(q, k, v, sink_row, *, nb=4, comm=None):
    S = q.shape[0]
    tq = nb * BLOCK

    def body(q_ref, k_ref, v_ref, sink_ref, o_ref):
        t = pl.program_id(0)
        for b in range(nb):
            n = t * nb + b
            prev = pl.multiple_of(jnp.maximum(n - 1, 0) * BLOCK, BLOCK)
            cur = pl.multiple_of(n * BLOCK, BLOCK)
            rows = slice(b * BLOCK, (b + 1) * BLOCK)
            mask = _band_mask_kq(n)
            for gidx in range(2):
                lanes = slice(128 * gidx, 128 * (gidx + 1))
                qs = _stack_heads(q_ref[rows, 256 * gidx:256 * gidx + 128],
                                  q_ref[rows, 256 * gidx + 128:256 * gidx + 256])
                kb = jnp.concatenate([k_ref[pl.ds(prev, BLOCK), lanes], k_ref[pl.ds(cur, BLOCK), lanes]], axis=0)
                vb = jnp.concatenate([v_ref[pl.ds(prev, BLOCK), lanes], v_ref[pl.ds(cur, BLOCK), lanes]], axis=0)
                st = lax.dot_general(kb, qs, NT, preferred_element_type=F32)
                st = jnp.where(mask, st, NEG)
                sink = sink_ref[gidx]
                m = jnp.maximum(jnp.max(st, axis=0, keepdims=True), sink)
                e = jnp.exp(st - m)
                inv = 1.0 / (jnp.sum(e, axis=0, keepdims=True) + jnp.exp(sink - m))
                o = lax.dot_general((e * inv).astype(CDT), vb, TN, preferred_element_type=F32)
                o0, o1 = _unstack_heads(o)
                o_ref[rows, 256 * gidx:256 * gidx + 128] = o0.astype(CDT)
                o_ref[rows, 256 * gidx + 128:256 * gidx + 256] = o1.astype(CDT)

    return _hosted(comm)(
        body, name="attn_fwd",
        grid=(S // tq,),
        in_specs=[pl.BlockSpec((tq, 512), lambda t: (t, 0)),
                  pl.BlockSpec((S, 256), lambda t: (0, 0)),
                  pl.BlockSpec((S, 256), lambda t: (0, 0)),
                  pl.BlockSpec((2, 1, 4 * BLOCK), lambda t: (0, 0, 0))],
        out_specs=pl.BlockSpec((tq, 512), lambda t: (t, 0)),
        out_shape=jax.ShapeDtypeStruct((S, 512), CDT),
        compiler_params=_params(1),
    )(q, k, v, sink_row)


def attn_bwd(q, k, v, do, sink_row, *, nb=4, comm=None):
    S = q.shape[0]
    tq = nb * BLOCK
    scale = HEAD_DIM ** -0.5

    def body(q_ref, k_ref, v_ref, do_ref, sink_ref, dq_ref, dk_ref, dv_ref, dsink_ref):
        t = pl.program_id(0)

        @pl.when(t == 0)
        def _():
            dk_ref[...] = jnp.zeros_like(dk_ref)
            dv_ref[...] = jnp.zeros_like(dv_ref)
            dsink_ref[...] = jnp.zeros_like(dsink_ref)

        for b in range(nb):
            n = t * nb + b
            prev = pl.multiple_of(jnp.maximum(n - 1, 0) * BLOCK, BLOCK)
            cur = pl.multiple_of(n * BLOCK, BLOCK)
            rows = slice(b * BLOCK, (b + 1) * BLOCK)
            mask = _band_mask_kq(n)
            for gidx in range(2):
                lanes = slice(128 * gidx, 128 * (gidx + 1))
                qs = _stack_heads(q_ref[rows, 256 * gidx:256 * gidx + 128],
                                  q_ref[rows, 256 * gidx + 128:256 * gidx + 256])
                dos = _stack_heads(do_ref[rows, 256 * gidx:256 * gidx + 128],
                                   do_ref[rows, 256 * gidx + 128:256 * gidx + 256])
                kb = jnp.concatenate([k_ref[pl.ds(prev, BLOCK), lanes], k_ref[pl.ds(cur, BLOCK), lanes]], axis=0)
                vb = jnp.concatenate([v_ref[pl.ds(prev, BLOCK), lanes], v_ref[pl.ds(cur, BLOCK), lanes]], axis=0)
                st = lax.dot_general(kb, qs, NT, preferred_element_type=F32)
                st = jnp.where(mask, st, NEG)
                sink = sink_ref[gidx]
                m = jnp.maximum(jnp.max(st, axis=0, keepdims=True), sink)
                e = jnp.exp(st - m)
                es = jnp.exp(sink - m)
                inv = 1.0 / (jnp.sum(e, axis=0, keepdims=True) + es)
                pt = e * inv
                dpt = lax.dot_general(vb, dos, NT, preferred_element_type=F32)
                delta = jnp.sum(pt * dpt, axis=0, keepdims=True)
                dst = (pt * (dpt - delta)).astype(CDT)
                dsink_ref[gidx] += -(es * inv) * delta
                dvb = jnp.dot(pt.astype(CDT), dos, preferred_element_type=F32)
                dkb = jnp.dot(dst, qs, preferred_element_type=F32)
                dqs = lax.dot_general(dst, kb, TN, preferred_element_type=F32) * scale
                dq0, dq1 = _unstack_heads(dqs)
                dq_ref[rows, 256 * gidx:256 * gidx + 128] = dq0
                dq_ref[rows, 256 * gidx + 128:256 * gidx + 256] = dq1
                dk_ref[pl.ds(prev, BLOCK), lanes] += dkb[0:BLOCK]
                dk_ref[pl.ds(cur, BLOCK), lanes] += dkb[BLOCK:2 * BLOCK]
                dv_ref[pl.ds(prev, BLOCK), lanes] += dvb[0:BLOCK]
                dv_ref[pl.ds(cur, BLOCK), lanes] += dvb[BLOCK:2 * BLOCK]

    full = pl.BlockSpec((S, 256), lambda t: (0, 0))
    tile = pl.BlockSpec((tq, 512), lambda t: (t, 0))
    srow = pl.BlockSpec((2, 1, 4 * BLOCK), lambda t: (0, 0, 0))
    return _hosted(comm)(
        body, name="attn_bwd",
        grid=(S // tq,),
        in_specs=[tile, full, full, tile, srow],
        out_specs=[tile, full, full, srow],
        out_shape=[jax.ShapeDtypeStruct((S, 512), F32), jax.ShapeDtypeStruct((S, 256), F32),
                   jax.ShapeDtypeStruct((S, 256), F32), jax.ShapeDtypeStruct((2, 1, 4 * BLOCK), F32)],
        compiler_params=_params(1),
    )(q, k, v, do, sink_row)


def _glu(u):
    a = u[:, 0:CONV_C]
    gt = u[:, CONV_C:2 * CONV_C]
    sg = _sigmoid(gt)
    return a, sg, a * sg


CONV_CHUNK = 64


def conv_fwd(u, cw, cb, lg, lb, *, tm=512, comm=None):
    S = u.shape[0]
    nh = tm // HALO

    def body(u_ref, uh_ref, cw_ref, cb_ref, lg_ref, lb_ref, o_ref, y_ref, hbuf):
        t = pl.program_id(0)
        _, _, hg = _glu(u_ref[...])
        _, _, hh = _glu(uh_ref[...])
        hbuf[0:HALO, :] = jnp.where(t > 0, hh, jnp.zeros_like(hh))
        hbuf[HALO:HALO + tm, :] = hg
        off = HALO - (CONV_K - 1)
        for c0 in range(0, tm, CONV_CHUNK):
            acc = jnp.zeros((CONV_CHUNK, CONV_C), F32) + cb_ref[...]
            for j in range(CONV_K):
                acc = acc + cw_ref[j:j + 1, :] * hbuf[c0 + off + j:c0 + off + j + CONV_CHUNK, :]
            y_ref[c0:c0 + CONV_CHUNK, :] = acc
        y = y_ref[...]
        yc = y - jnp.mean(y, axis=-1, keepdims=True)
        r = lax.rsqrt(jnp.mean(yc * yc, axis=-1, keepdims=True) + EPS)
        z = yc * r * lg_ref[...] + lb_ref[...]
        o_ref[...] = (z * _sigmoid(z)).astype(CDT)

    vec = pl.BlockSpec((1, CONV_C), lambda t: (0, 0))
    return _hosted(comm)(
        body, name="conv_fwd",
        grid=(S // tm,),
        in_specs=[pl.BlockSpec((tm, 2 * CONV_C), lambda t: (t, 0)),
                  pl.BlockSpec((HALO, 2 * CONV_C), lambda t: (jnp.maximum(t * nh - 1, 0), 0)),
                  pl.BlockSpec((CONV_K, CONV_C), lambda t: (0, 0)), vec, vec, vec],
        out_specs=[pl.BlockSpec((tm, CONV_C), lambda t: (t, 0)), pl.BlockSpec((tm, CONV_C), lambda t: (t, 0))],
        out_shape=[jax.ShapeDtypeStruct((S, CONV_C), CDT), jax.ShapeDtypeStruct((S, CONV_C), F32)],
        scratch_shapes=[pltpu.VMEM((HALO + tm, CONV_C), F32)],
        compiler_params=_params(1),
    )(u, u, cw, cb, lg, lb)


def conv_bwd(dc, u, y, cw, lg, lb, *, tm=512, comm=None):
    S = u.shape[0]
    nh = tm // HALO
    nt = S // tm

    def ln_bwd(dcv, yv, lgv, lbv):
        yc = yv - jnp.mean(yv, axis=-1, keepdims=True)
        r = lax.rsqrt(jnp.mean(yc * yc, axis=-1, keepdims=True) + EPS)
        yhat = yc * r
        z = yhat * lgv + lbv
        sg = _sigmoid(z)
        dz = dcv * (sg * (1.0 + z * (1.0 - sg)))
        dyhat = dz * lgv
        dy = r * (dyhat - jnp.mean(dyhat, axis=-1, keepdims=True)
                  - yhat * jnp.mean(dyhat * yhat, axis=-1, keepdims=True))
        return dy, dz, yhat

    def body(dc_ref, dcn_ref, u_ref, uh_ref, y_ref, yn_ref, cw_ref, lg_ref, lb_ref,
             du_ref, dcw_ref, dcb_ref, dlg_ref, dlb_ref, hbuf, dybuf, dhg_sc, dw_sc):
        t = pl.program_id(0)

        @pl.when(t == 0)
        def _():
            dw_sc[...] = jnp.zeros_like(dw_sc)
            dcb_ref[...] = jnp.zeros_like(dcb_ref)
            dlg_ref[...] = jnp.zeros_like(dlg_ref)
            dlb_ref[...] = jnp.zeros_like(dlb_ref)

        lgv, lbv = lg_ref[...], lb_ref[...]
        dy, dz, yhat = ln_bwd(dc_ref[...].astype(F32), y_ref[...], lgv, lbv)
        dyn, _, _ = ln_bwd(dcn_ref[...].astype(F32), yn_ref[...], lgv, lbv)
        dlb_ref[...] += jnp.sum(dz, axis=0, keepdims=True)
        dlg_ref[...] += jnp.sum(dz * yhat, axis=0, keepdims=True)
        dcb_ref[...] += jnp.sum(dy, axis=0, keepdims=True)
        dybuf[0:tm, :] = dy
        dybuf[tm:tm + HALO, :] = jnp.where(t < nt - 1, dyn, jnp.zeros_like(dyn))

        a, sg, hg = _glu(u_ref[...])
        _, _, hh = _glu(uh_ref[...])
        hbuf[0:HALO, :] = jnp.where(t > 0, hh, jnp.zeros_like(hh))
        hbuf[HALO:HALO + tm, :] = hg

        off = HALO - (CONV_K - 1)
        for c0 in range(0, tm, CONV_CHUNK):
            acc = jnp.zeros((CONV_CHUNK, CONV_C), F32)
            dyc = dybuf[c0:c0 + CONV_CHUNK, :]
            for j in range(CONV_K):
                acc = acc + cw_ref[j:j + 1, :] * dybuf[c0 + (CONV_K - 1) - j:c0 + (CONV_K - 1) - j + CONV_CHUNK, :]
                prod = dyc * hbuf[c0 + off + j:c0 + off + j + CONV_CHUNK, :]
                dw_sc[j] += jnp.sum(prod.reshape(CONV_CHUNK // 8, 8, CONV_C), axis=0)
            dhg_sc[c0:c0 + CONV_CHUNK, :] = acc

        dhg = dhg_sc[...]
        du_ref[:, 0:CONV_C] = dhg * sg
        du_ref[:, CONV_C:2 * CONV_C] = dhg * a * sg * (1.0 - sg)

        @pl.when(t == nt - 1)
        def _():
            dcw_ref[...] = jnp.sum(dw_sc[...], axis=1)

    vec = pl.BlockSpec((1, CONV_C), lambda t: (0, 0))
    tile = pl.BlockSpec((tm, CONV_C), lambda t: (t, 0))
    nxt = pl.BlockSpec((HALO, CONV_C), lambda t: (jnp.minimum((t + 1) * nh, S // HALO - 1), 0))
    return _hosted(comm)(
        body, name="conv_bwd",
        grid=(nt,),
        in_specs=[tile, nxt,
                  pl.BlockSpec((tm, 2 * CONV_C), lambda t: (t, 0)),
                  pl.BlockSpec((HALO, 2 * CONV_C), lambda t: (jnp.maximum(t * nh - 1, 0), 0)),
                  tile, nxt,
                  pl.BlockSpec((CONV_K, CONV_C), lambda t: (0, 0)), vec, vec],
        out_specs=[pl.BlockSpec((tm, 2 * CONV_C), lambda t: (t, 0)),
                   pl.BlockSpec((CONV_K, CONV_C), lambda t: (0, 0)), vec, vec, vec],
        out_shape=[jax.ShapeDtypeStruct((S, 2 * CONV_C), F32), jax.ShapeDtypeStruct((CONV_K, CONV_C), F32),
                   jax.ShapeDtypeStruct((1, CONV_C), F32), jax.ShapeDtypeStruct((1, CONV_C), F32),
                   jax.ShapeDtypeStruct((1, CONV_C), F32)],
        scratch_shapes=[pltpu.VMEM((HALO + tm, CONV_C), F32), pltpu.VMEM((tm + HALO, CONV_C), F32),
                        pltpu.VMEM((tm, CONV_C), F32), pltpu.VMEM((CONV_K, 8, CONV_C), F32)],
        compiler_params=_params(1),
    )(dc, dc, u, u, y, y, cw, lg, lb)


def outproj_fwd(x, ao, co, wout, *, tm=512):
    S, D = x.shape

    def body(x_ref, a_ref, c_ref, w_ref, o_ref):
        o_ref[...] = (x_ref[...]
                      + jnp.dot(a_ref[...], w_ref[0:ATT_W, :], preferred_element_type=F32)
                      + jnp.dot(c_ref[...], w_ref[ATT_W:ATT_W + CONV_C, :], preferred_element_type=F32))

    return pl.pallas_call(
        body, name="outproj_fwd",
        grid=(S // tm,),
        in_specs=[pl.BlockSpec((tm, D), lambda i: (i, 0)), pl.BlockSpec((tm, ATT_W), lambda i: (i, 0)),
                  pl.BlockSpec((tm, CONV_C), lambda i: (i, 0)), pl.BlockSpec((D, D), lambda i: (0, 0))],
        out_specs=pl.BlockSpec((tm, D), lambda i: (i, 0)),
        out_shape=jax.ShapeDtypeStruct((S, D), F32),
        compiler_params=_params(1),
    )(x, ao, co, wout)


def outproj_bwd(dx, ao, co, wout, *, tm=512, comm=None):
    S, D = dx.shape

    def body(dx_ref, a_ref, c_ref, w_ref, da_ref, dc_ref, dw_ref):
        @pl.when(pl.program_id(0) == 0)
        def _():
            dw_ref[...] = jnp.zeros_like(dw_ref)

        dxb = dx_ref[...].astype(CDT)
        da_ref[...] = lax.dot_general(dxb, w_ref[0:ATT_W, :], NT, preferred_element_type=F32).astype(CDT)
        dc_ref[...] = lax.dot_general(dxb, w_ref[ATT_W:ATT_W + CONV_C, :], NT, preferred_element_type=F32)
        dw_ref[0:ATT_W, :] += lax.dot_general(a_ref[...], dxb, TN, preferred_element_type=F32)
        dw_ref[ATT_W:ATT_W + CONV_C, :] += lax.dot_general(c_ref[...], dxb, TN, preferred_element_type=F32)

    return _hosted(comm)(
        body, name="outproj_bwd",
        grid=(S // tm,),
        in_specs=[pl.BlockSpec((tm, D), lambda i: (i, 0)), pl.BlockSpec((tm, ATT_W), lambda i: (i, 0)),
                  pl.BlockSpec((tm, CONV_C), lambda i: (i, 0)), pl.BlockSpec((D, D), lambda i: (0, 0))],
        out_specs=[pl.BlockSpec((tm, ATT_W), lambda i: (i, 0)), pl.BlockSpec((tm, CONV_C), lambda i: (i, 0)),
                   pl.BlockSpec((D, D), lambda i: (0, 0))],
        out_shape=[jax.ShapeDtypeStruct((S, ATT_W), CDT), jax.ShapeDtypeStruct((S, CONV_C), F32),
                   jax.ShapeDtypeStruct((D, D), F32)],
        compiler_params=_params(1),
    )(dx, ao, co, wout)


def inproj_bwd(x, g, dres, dq, dk, dv, du, wextT, cs, sn, *, tm=256):
    S, D = x.shape

    def body(x_ref, g_ref, dres_ref, dq_ref, dk_ref, dv_ref, du_ref, w_ref, cs_ref, sn_ref,
             dx_ref, dw_ref, dg_ref, dp_sc):
        @pl.when(pl.program_id(0) == 0)
        def _():
            dw_ref[...] = jnp.zeros_like(dw_ref)
            dg_ref[...] = jnp.zeros_like(dg_ref)

        c, s = cs_ref[...], sn_ref[...]
        for b in range(4):
            dp_sc[:, 128 * b:128 * (b + 1)] = _rope_transpose(dq_ref[:, 128 * b:128 * (b + 1)], c, s).astype(CDT)
        for b in range(2):
            dp_sc[:, 512 + 128 * b:512 + 128 * (b + 1)] = _rope_transpose(
                dk_ref[:, 128 * b:128 * (b + 1)], c, s).astype(CDT)
        dp_sc[:, 768:1024] = dv_ref[...].astype(CDT)
        dp_sc[:, 1024:2048] = du_ref[...].astype(CDT)
        dp = dp_sc[...]
        xhat, r = _rms_parts(x_ref[...])
        gg = g_ref[...]
        h = (xhat * gg).astype(CDT)
        dh = jnp.dot(dp, w_ref[...], preferred_element_type=F32)
        dw_ref[...] += lax.dot_general(dp, h, TN, preferred_element_type=F32)
        dx, dg = _rms_bwd(dh, xhat, r, gg)
        dx_ref[...] = dres_ref[...] + dx
        dg_ref[...] += dg

    def row(w):
        return pl.BlockSpec((tm, w), lambda i: (i, 0))

    return pl.pallas_call(
        body, name="inproj_bwd",
        grid=(S // tm,),
        in_specs=[row(D), pl.BlockSpec((1, D), lambda i: (0, 0)), row(D), row(512), row(256), row(256),
                  row(1024), pl.BlockSpec((D_EXT, D), lambda i: (0, 0)), row(128), row(128)],
        out_specs=[row(D), pl.BlockSpec((D_EXT, D), lambda i: (0, 0)), pl.BlockSpec((1, D), lambda i: (0, 0))],
        out_shape=[jax.ShapeDtypeStruct((S, D), F32), jax.ShapeDtypeStruct((D_EXT, D), F32),
                   jax.ShapeDtypeStruct((1, D), F32)],
        scratch_shapes=[pltpu.VMEM((tm, D_EXT), CDT)],
        compiler_params=_params(1),
    )(x, g, dres, dq, dk, dv, du, wextT, cs, sn)


def _rope_tables(positions):
    inv_freq = 1.0 / (10000.0 ** (jnp.arange(0, HEAD_DIM, 2, dtype=F32) / HEAD_DIM))
    ang = positions.astype(F32).reshape(-1, 1) * inv_freq
    cos, sin = jnp.cos(ang), jnp.sin(ang)
    cs = jnp.tile(jnp.concatenate([cos, cos], axis=-1), (1, 2))
    sn = jnp.tile(jnp.concatenate([-sin, sin], axis=-1), (1, 2))
    return cs, sn


def _widen_w_in(w):
    q, u = w[0:512], w[768:1792]
    parts = [q]
    for base in (512, 576, 640, 704):
        parts += [w[base:base + 64], w[base:base + 64]]
    return jnp.concatenate(parts + [u], axis=0)


def _fold_w_in(d):
    parts = [d[0:512]]
    for base in (512, 640, 768, 896):
        parts.append(d[base:base + 64] + d[base + 64:base + 128])
    return jnp.concatenate(parts + [d[1024:2048]], axis=0)


def add_half(g5, r1, c_idx):
    _, _, r, D = g5.shape

    def body(c_ref, g_ref, r_ref, o_ref):
        o_ref[...] = (g_ref[...] + r_ref[...]).astype(CDT)

    return pl.pallas_call(
        body, name="add_half",
        grid_spec=pltpu.PrefetchScalarGridSpec(
            num_scalar_prefetch=1, grid=(N_CHIPS,),
            in_specs=[pl.BlockSpec((None, None, r, D), lambda s, cr: (s, cr[0], 0, 0)),
                      pl.BlockSpec((None, r, D), lambda s, cr: (s, 0, 0))],
            out_specs=pl.BlockSpec((None, r, D), lambda s, cr: (s, 0, 0))),
        out_shape=jax.ShapeDtypeStruct((N_CHIPS, r, D), CDT),
        compiler_params=_params(1),
    )(c_idx, g5, r1)


def sum_partials(part, recv3, j_idx):
    _, r, D = part.shape
    tr_ = r // 2

    def body(j_ref, p_ref, r_ref, o_ref):
        o_ref[...] = ((p_ref[...].astype(F32) + r_ref[0].astype(F32)) + r_ref[1].astype(F32)) + r_ref[2].astype(F32)

    return pl.pallas_call(
        body, name="sum_partials",
        grid_spec=pltpu.PrefetchScalarGridSpec(
            num_scalar_prefetch=1, grid=(2,),
            in_specs=[pl.BlockSpec((None, tr_, D), lambda i, jr: (jr[0], i, 0)),
                      pl.BlockSpec((3, tr_, D), lambda i, jr: (0, i, 0))],
            out_specs=pl.BlockSpec((tr_, D), lambda i, jr: (i, 0))),
        out_shape=jax.ShapeDtypeStruct((r, D), F32),
        compiler_params=_params(1),
    )(j_idx, part, recv3)


class _Chain:
    STAGES = ("swap", "xchg", "share")

    def __init__(self, grads, c_idx, j_idx):
        self.c_idx, self.j_idx = c_idx, j_idx
        self.g5 = [g.reshape(N_CHIPS, 2, g.shape[0] // (2 * N_CHIPS), g.shape[1]) for g in grads]
        self.stage_no = 0

    @property
    def done(self):
        return self.stage_no == len(self.STAGES)

    def next_stage(self):
        name = self.STAGES[self.stage_no]

        def callback(res):
            getattr(self, "after_" + name)(res)
            self.stage_no += 1

        return getattr(self, name)(), callback

    def swap(self):
        return swap_op(self.g5)

    def after_swap(self, recv):
        self.parts = [add_half(g, r, self.c_idx) for g, r in zip(self.g5, recv)]

    def xchg(self):
        return exchange_op(self.parts)

    def after_xchg(self, recv):
        self.totals = [sum_partials(p, r, self.j_idx) for p, r in zip(self.parts, recv)]

    def share(self):
        return share_op(self.totals)

    def after_share(self, recv):
        both = [_own_slab(h, t, self.c_idx[0]) for h, t in zip(recv, self.totals)]
        self.final = [h.reshape(2 * h.shape[1], h.shape[2]) for h in both]


def all_reduce_small(vec):
    R = vec.shape[0]

    def body(v_ref, o_ref, buf, send, recv):
        x, y, c = _coords()
        me = 4 * x + 2 * y + c
        buf[me] = v_ref[...]
        cps = []
        for m in range(1, N_DEV):
            dx, dy, dc = (m >> 2) & 1, (m >> 1) & 1, m & 1
            cp = pltpu.make_async_remote_copy(v_ref, buf.at[me], send.at[m - 1], recv.at[m - 1],
                                              device_id=((x + dx) % 2, (y + dy) % 2, (c + dc) % 2),
                                              device_id_type=MESH)
            cp.start()
            cps.append(cp)
        for cp in cps:
            cp.wait()
        acc = buf[0]
        for d in range(1, N_DEV):
            acc = acc + buf[d]
        o_ref[...] = acc

    return pl.pallas_call(
        body, name="all_reduce_small",
        in_specs=[pl.BlockSpec(memory_space=pltpu.VMEM)], out_specs=pl.BlockSpec(memory_space=pltpu.VMEM),
        out_shape=jax.ShapeDtypeStruct(vec.shape, F32),
        scratch_shapes=[pltpu.VMEM((N_DEV, R, 128), F32), pltpu.SemaphoreType.DMA((N_DEV - 1,)),
                        pltpu.SemaphoreType.DMA((N_DEV - 1,))],
    )(vec)


def adamw(w, g, m, v, *, tm=512):
    R, C = w.shape
    tm = max(t for t in range(8, min(tm, R) + 1, 8) if R % t == 0)
    c1 =1.0 - ADAM_B1 ** ADAM_STEP
    c2 = 1.0 - ADAM_B2 ** ADAM_STEP

    def body(w_ref, g_ref, m_ref, v_ref, d_ref, nm_ref, nv_ref):
        gg = g_ref[...]
        nm = ADAM_B1 * m_ref[...] + (1.0 - ADAM_B1) * gg
        nv = ADAM_B2 * v_ref[...] + (1.0 - ADAM_B2) * (gg * gg)
        nm_ref[...] = nm
        nv_ref[...] = nv
        d_ref[...] = -ADAM_LR * ((nm / c1) / (jnp.sqrt(nv / c2) + ADAM_EPS) + ADAM_WD * w_ref[...])

    blk = pl.BlockSpec((tm, C), lambda i: (i, 0))
    return pl.pallas_call(
        body, name="adamw",
        grid=(pl.cdiv(R, tm),),
        in_specs=[blk] * 4, out_specs=[blk] * 3,
        out_shape=[jax.ShapeDtypeStruct((R, C), F32)] * 3,
        compiler_params=_params(1),
    )(w, g, m, v)


def adamw_layers(w, m, v, g_layers, *, tm=352):
    L, R, C = w.shape
    tm = max(t for t in range(8, min(tm, R) + 1, 8) if R % t == 0)
    c1 = 1.0 - ADAM_B1 ** ADAM_STEP
    c2 = 1.0 - ADAM_B2 ** ADAM_STEP

    def body(w_ref, m_ref, v_ref, *rest):
        g_refs, (go_ref, d_ref, nm_ref, nv_ref) = rest[:L], rest[L:]
        layer = pl.program_id(0)
        gg = g_refs[0][...]
        for l in range(1, L):
            gg = jnp.where(layer == l, g_refs[l][...], gg)
        nm = ADAM_B1 * m_ref[...] + (1.0 - ADAM_B1) * gg
        nv = ADAM_B2 * v_ref[...] + (1.0 - ADAM_B2) * (gg * gg)
        go_ref[...] = gg
        nm_ref[...] = nm
        nv_ref[...] = nv
        d_ref[...] = -ADAM_LR * ((nm / c1) / (jnp.sqrt(nv / c2) + ADAM_EPS) + ADAM_WD * w_ref[...])

    blk = pl.BlockSpec((None, tm, C), lambda l, i: (l, i, 0))
    gblk = pl.BlockSpec((tm, C), lambda l, i: (i, 0))
    return pl.pallas_call(
        body, name="adamw_layers",
        grid=(L, R // tm),
        in_specs=[blk] * 3 + [gblk] * L, out_specs=[blk] * 4,
        out_shape=[jax.ShapeDtypeStruct((L, R, C), F32)] * 4,
        compiler_params=_params(2),
    )(w, m, v, *g_layers)


_SMALL = (("n1", (2, D_MODEL)), ("nm", (2, D_MODEL)), ("n2", (2, D_MODEL)), ("nf", (D_MODEL,)),
          ("cb", (2, CONV_C)), ("lg", (2, CONV_C)), ("lb", (2, CONV_C)), ("sinks", (2, N_HEADS)),
          ("cw", (2, CONV_K, CONV_C)))


def _pack(parts, rows):
    flat = jnp.concatenate([p.reshape(-1).astype(F32) for p in parts])
    return jnp.pad(flat, (0, rows * 128 - flat.shape[0])).reshape(rows, 128)


def _unpack(block, shapes):
    flat = block.reshape(-1)
    out, o = [], 0
    for shp in shapes:
        n = 1
        for s in shp:
            n *= s
        out.append(flat[o:o + n].reshape(shp))
        o += n
    return out


def kernel(x, positions, ffn1_norm, ffn1_w_gate, ffn1_w_up, ffn1_w_down, mix_norm, w_in, conv_w, conv_b, conv_ln_g, conv_ln_b, attn_sinks, w_out, ffn2_norm, ffn2_w_gate, ffn2_w_up, ffn2_w_down, final_norm, loss_target, m_ffn1_norm, m_ffn1_w_gate, m_ffn1_w_up, m_ffn1_w_down, m_mix_norm, m_w_in, m_conv_w, m_conv_b, m_conv_ln_g, m_conv_ln_b, m_attn_sinks, m_w_out, m_ffn2_norm, m_ffn2_w_gate, m_ffn2_w_up, m_ffn2_w_down, m_final_norm, v_ffn1_norm, v_ffn1_w_gate, v_ffn1_w_up, v_ffn1_w_down, v_mix_norm, v_w_in, v_conv_w, v_conv_b, v_conv_ln_g, v_conv_ln_b, v_attn_sinks, v_w_out, v_ffn2_norm, v_ffn2_w_gate, v_ffn2_w_up, v_ffn2_w_down, v_final_norm):
    cx, cy, cc = _coords()
    chip = 2 * cx + cy
    c_idx = jnp.reshape(cc, (1,)).astype(jnp.int32)
    j_idx = jnp.reshape(chip, (1,)).astype(jnp.int32)
    L = ffn1_norm.shape[0]
    tr = lambda a: jnp.swapaxes(a, 1, 2)

    sh = dict(f1g=tr(ffn1_w_gate), f1u=tr(ffn1_w_up), f1d=ffn1_w_down, f2g=tr(ffn2_w_gate),
              f2u=tr(ffn2_w_up), f2d=ffn2_w_down, win=tr(w_in), wout=w_out)
    sh = {k: [v[l].astype(CDT) for l in range(L)] for k, v in sh.items()}
    W = {}

    def gather_op(keys):
        return ag_op([conv_w if k == "cw" else sh[k[0]][k[1]] for k in keys])

    def take(keys, res):
        for k, a in zip(keys, res):
            if k == "cw":
                W[k] = _own_slab(a, conv_w, chip)
            else:
                W[k] = _own_slab(a, sh[k[0]][k[1]], chip).reshape(N_CHIPS * a.shape[1], a.shape[2])

    def with_ag(fn, keys, *args):
        if not keys:
            return fn(*args)
        main, res = fn(*args, comm=gather_op(keys))
        take(keys, res)
        return main

    ag_hosts = {("ffn1", 0): [("win", 0), "cw", ("f2g", 0), ("f1d", 1)],
                ("inproj", 0): [("wout", 0)], ("attn", 0): [("f2u", 0)], ("conv", 0): [("f2d", 0)],
                ("ffn2", 0): [("f1g", 1), ("f1u", 1)],
                ("ffn1", 1): [("win", 1), ("f2g", 1), ("f2u", 1)],
                ("inproj", 1): [("wout", 1)], ("attn", 1): [("f2d", 1)]}
    for k_, a in zip(("f1g", "f1u", "f1d"), first_gather([sh[k_][0] for k_ in ("f1g", "f1u", "f1d")])):
        own = sh[k_][0].reshape(a.shape[1:])
        W[(k_, 0)] = _own_slab(a, own, chip).reshape(N_CHIPS * sh[k_][0].shape[0], sh[k_][0].shape[1])

    cs, sn = _rope_tables(positions)
    saved = []
    h = x[0]
    for l in range(L):
        sink = attn_sinks[l].reshape(2, 4)
        sink_row = jnp.repeat(sink, BLOCK, axis=1).reshape(2, 1, 4 * BLOCK)
        x0 = h
        x1, g1, u1 = with_ag(ffn_fwd, ag_hosts.get(("ffn1", l)), x0, ffn1_norm[l][None],
                             W[("f1g", l)], W[("f1u", l)], W[("f1d", l)])
        wext = _widen_w_in(W[("win", l)])
        q, k, v, u = with_ag(inproj_fwd, ag_hosts.get(("inproj", l)), x1, mix_norm[l][None], wext, cs, sn)
        ao = with_ag(attn_fwd, ag_hosts.get(("attn", l)), q, k, v, sink_row)
        cwl = jnp.transpose(W["cw"][:, l], (1, 0, 2)).reshape(CONV_K, CONV_C)
        co, yc = with_ag(conv_fwd, ag_hosts.get(("conv", l)), u, cwl, conv_b[l][None], conv_ln_g[l][None],
                         conv_ln_b[l][None])
        x2 = outproj_fwd(x1, ao, co, W[("wout", l)])
        x3, g2, u2 = with_ag(ffn_fwd, ag_hosts.get(("ffn2", l)), x2, ffn2_norm[l][None],
                             W[("f2g", l)], W[("f2u", l)], W[("f2d", l)])
        saved.append((x0, x1, x2, g1, u1, g2, u2, q, k, v, u, ao, co, yc, sink, wext, cwl))
        h = x3

    loss, dx, dnf = loss_head(h, final_norm[None], loss_target[0])

    active = []

    def advance(run):
        stages = [ch.next_stage() for ch in active]
        ops = [op for op, _ in stages]
        main, res = run(_merge(*ops) if ops else None)
        for (_, cb), r in zip(stages, _split(res, *ops)):
            cb(r)
        active[:] = [ch for ch in active if not ch.done]
        return main

    def hosted(fn, *args):
        def run(comm):
            if comm is None:
                return fn(*args), []
            return fn(*args, comm=comm)
        return advance(run)

    def chain(key, grads):
        chains[key] = _Chain(grads, c_idx, j_idx)
        active.append(chains[key])

    small = {k_: [None] * L for k_ in ("n1", "nm", "n2", "cw", "cb", "lg", "lb", "sinks")}
    chains = {}
    for l in reversed(range(L)):
        x0, x1, x2, g1, u1, g2, u2, q, k, v, u, ao, co, yc, sink, wext, cwl = saved[l]
        sink_row = jnp.repeat(sink, BLOCK, axis=1).reshape(2, 1, 4 * BLOCK)
        chain(("f2d", l), [hosted(ffn_wgrad_down, dx, g2, u2)])
        dx2, dgt, dup, hh, small["n2"][l] = hosted(
            ffn_bwd_dgrad, x2, ffn2_norm[l][None], dx, g2, u2, W[("f2g", l)], W[("f2u", l)], W[("f2d", l)])
        chain(("f2gu", l), hosted(ffn_wgrad_gate_up, hh, dgt, dup))
        da, dc, gwout = hosted(outproj_bwd, dx2, ao, co, W[("wout", l)])
        du, small["cw"][l], small["cb"][l], small["lg"][l], small["lb"][l] = hosted(
            conv_bwd, dc, u, yc, cwl, conv_ln_g[l][None], conv_ln_b[l][None])
        dq, dk, dv, dsink = hosted(attn_bwd, q, k, v, da, sink_row)
        small["sinks"][l] = jnp.sum(dsink.reshape(2, 4, BLOCK), axis=-1).reshape(N_HEADS)
        dx1, gwext, small["nm"][l] = inproj_bwd(x1, mix_norm[l][None], dx2, dq, dk, dv, du, wext, cs, sn)
        chain(("mx", l), [gwout, _fold_w_in(gwext)])
        chain(("f1d", l), [hosted(ffn_wgrad_down, dx1, g1, u1)])
        dx, dgt, dup, hh, small["n1"][l] = hosted(
            ffn_bwd_dgrad, x0, ffn1_norm[l][None], dx1, g1, u1, W[("f1g", l)], W[("f1u", l)], W[("f1d", l)])
        chain(("f1gu", l), hosted(ffn_wgrad_gate_up, hh, dgt, dup))

    while active:
        advance(lambda comm: (None, _run_comm(comm)))

    G = {k_: jnp.stack(v_) for k_, v_ in small.items()}
    G["nf"] = dnf
    small_shapes = [shp for _, shp in _SMALL]
    n_small = 1 + sum(math.prod(s) for s in small_shapes)
    rows = -(-n_small // 1024) * 8
    packed = _pack([loss] + [G[k_] for k_, _ in _SMALL], rows)
    summed = all_reduce_small(packed)
    loss_out, *small_sum = _unpack(summed, [()] + small_shapes)
    gs = dict(zip([k_ for k_, _ in _SMALL], small_sum))
    gs["cw"] = lax.dynamic_slice_in_dim(gs["cw"], chip * (CONV_C // N_CHIPS), CONV_C // N_CHIPS, axis=2)

    grads = dict(ffn1_norm=gs["n1"], mix_norm=gs["nm"], conv_w=gs["cw"], conv_b=gs["cb"], conv_ln_g=gs["lg"],
                 conv_ln_b=gs["lb"], attn_sinks=gs["sinks"], ffn2_norm=gs["n2"], final_norm=gs["nf"])

    weights = dict(ffn1_norm=ffn1_norm, ffn1_w_gate=ffn1_w_gate, ffn1_w_up=ffn1_w_up, ffn1_w_down=ffn1_w_down,
                   mix_norm=mix_norm, w_in=w_in, conv_w=conv_w, conv_b=conv_b, conv_ln_g=conv_ln_g,
                   conv_ln_b=conv_ln_b, attn_sinks=attn_sinks, w_out=w_out, ffn2_norm=ffn2_norm,
                   ffn2_w_gate=ffn2_w_gate, ffn2_w_up=ffn2_w_up, ffn2_w_down=ffn2_w_down, final_norm=final_norm)
    moms = dict(ffn1_norm=(m_ffn1_norm, v_ffn1_norm), ffn1_w_gate=(m_ffn1_w_gate, v_ffn1_w_gate),
                ffn1_w_up=(m_ffn1_w_up, v_ffn1_w_up), ffn1_w_down=(m_ffn1_w_down, v_ffn1_w_down),
                mix_norm=(m_mix_norm, v_mix_norm), w_in=(m_w_in, v_w_in), conv_w=(m_conv_w, v_conv_w),
                conv_b=(m_conv_b, v_conv_b), conv_ln_g=(m_conv_ln_g, v_conv_ln_g),
                conv_ln_b=(m_conv_ln_b, v_conv_ln_b), attn_sinks=(m_attn_sinks, v_attn_sinks),
                w_out=(m_w_out, v_w_out), ffn2_norm=(m_ffn2_norm, v_ffn2_norm),
                ffn2_w_gate=(m_ffn2_w_gate, v_ffn2_w_gate), ffn2_w_up=(m_ffn2_w_up, v_ffn2_w_up),
                ffn2_w_down=(m_ffn2_w_down, v_ffn2_w_down), final_norm=(m_final_norm, v_final_norm))
    names = list(weights)
    big_names = dict(ffn1_w_gate=("f1gu", 0, True), ffn1_w_up=("f1gu", 1, True), ffn1_w_down=("f1d", 0, False),
                     w_in=("mx", 1, True), w_out=("mx", 0, False), ffn2_w_gate=("f2gu", 0, True),
                     ffn2_w_up=("f2gu", 1, True), ffn2_w_down=("f2d", 0, False))
    delta, new_m, new_v = {}, {}, {}
    for nme, (group, idx, transposed) in big_names.items():
        view = tr if transposed else (lambda a: a)
        res = adamw_layers(view(weights[nme]), view(moms[nme][0]), view(moms[nme][1]),
                           [chains[(group, l)].final[idx] for l in range(L)])
        grads[nme], delta[nme], new_m[nme], new_v[nme] = [view(a) for a in res]
    small_names = [nme for nme in names if nme not in big_names]
    s_shapes = [weights[nme].shape for nme in small_names]
    n_tot = sum(math.prod(s) for s in s_shapes)
    srows = -(-n_tot // 1024) * 8
    d, nm_, nv_ = adamw(_pack([weights[nme] for nme in small_names], srows),
                        _pack([grads[nme] for nme in small_names], srows),
                        _pack([moms[nme][0] for nme in small_names], srows),
                        _pack([moms[nme][1] for nme in small_names], srows))
    for nme, dd, mm, vv in zip(small_names, _unpack(d, s_shapes), _unpack(nm_, s_shapes), _unpack(nv_, s_shapes)):
        delta[nme], new_m[nme], new_v[nme] = dd, mm, vv

    return (loss_out, dx[None], *[grads[nme] for nme in names], *[delta[nme] for nme in names],
            *[new_m[nme] for nme in names], *[new_v[nme] for nme in names])
```

```python
import math

import jax
import jax.numpy as jnp
from jax import lax
from jax.experimental import pallas as pl
from jax.experimental.pallas import tpu as pltpu

F32 = jnp.float32
CDT = jnp.bfloat16
D_MODEL = 1024
D_FF = 2816
N_HEADS = 8
HEAD_DIM = 64
BLOCK = 128
CONV_K = 31
CONV_C = 512
ATT_W = 512
D_EXT = 2048
EPS = 1e-5
HALO = 32
FF_CHUNK = 256
NEG = float(jnp.finfo(jnp.float32).min)
VMEM_LIMIT = 56 * 1024 * 1024

ADAM_LR = 0.001
ADAM_B1 = 0.9
ADAM_B2 = 0.999
ADAM_EPS = 1e-08
ADAM_WD = 0.01
ADAM_STEP = 10

NT = (((1,), (1,)), ((), ()))
TN = (((0,), (0,)), ((), ()))


MESH = pl.DeviceIdType.MESH
ANY = pl.BlockSpec(memory_space=pl.ANY)
N_CHIPS = 4
N_DEV = 8


def _params(n_axes):
    return pltpu.CompilerParams(dimension_semantics=("arbitrary",) * n_axes, vmem_limit_bytes=VMEM_LIMIT)


class _Comm:
    def __init__(self, name, inputs, out_shape, sems, descs):
        self.name, self.inputs, self.out_shape, self.sems, self.descs = name, list(inputs), list(out_shape), list(sems), descs


def _merge(*ops):
    ops = [o for o in ops if o is not None]
    if len(ops) == 1:
        return ops[0]

    def descs(cins, couts, sems):
        out, i, o, s = [], 0, 0, 0
        for op in ops:
            ni, no, ns = len(op.inputs), len(op.out_shape), len(op.sems)
            out += op.descs(cins[i:i + ni], couts[o:o + no], sems[s:s + ns])
            i, o, s = i + ni, o + no, s + ns
        return out

    return _Comm("_".join(o.name for o in ops), sum((o.inputs for o in ops), []),
                 sum((o.out_shape for o in ops), []), sum((o.sems for o in ops), []), descs)


def _split(couts, *ops):
    res, o = [], 0
    for op in ops:
        res.append(couts[o:o + len(op.out_shape)])
        o += len(op.out_shape)
    return res


def _hosted(comm):
    if comm is None:
        return pl.pallas_call

    def make(body, *, name, grid, in_specs, out_specs, out_shape, compiler_params, scratch_shapes=()):
        single = not isinstance(out_shape, (list, tuple))
        o_specs = [out_specs] if single else list(out_specs)
        o_shape = [out_shape] if single else list(out_shape)
        n_in, n_out, n_sc = len(in_specs), len(o_specs), len(scratch_shapes)
        c_in, c_out = len(comm.inputs), len(comm.out_shape)

        def hosted(*refs):
            ins, cins = refs[:n_in], refs[n_in:n_in + c_in]
            o0 = n_in + c_in
            outs, couts = refs[o0:o0 + n_out], refs[o0 + n_out:o0 + n_out + c_out]
            s0 = o0 + n_out + c_out
            scr, sems = refs[s0:s0 + n_sc], refs[s0 + n_sc:]
            first = pl.program_id(0) == 0
            last = pl.program_id(0) == grid[0] - 1
            for ax in range(1, len(grid)):
                first = first & (pl.program_id(ax) == 0)
                last = last & (pl.program_id(ax) == grid[ax] - 1)

            @pl.when(first)
            def _():
                for d in comm.descs(cins, couts, sems):
                    d.start()

            body(*ins, *outs, *scr)

            @pl.when(last)
            def _():
                for d in comm.descs(cins, couts, sems):
                    d.wait()

        call = pl.pallas_call(
            hosted, name=f"{name}_{comm.name}", grid=grid,
            in_specs=list(in_specs) + [ANY] * c_in, out_specs=o_specs + [ANY] * c_out,
            out_shape=o_shape + comm.out_shape, scratch_shapes=list(scratch_shapes) + comm.sems,
            compiler_params=compiler_params)

        def run(*args):
            res = call(*args, *comm.inputs)
            return (res[0] if single else list(res[:n_out])), list(res[n_out:])

        return run

    return make


def _run_comm(comm):
    c_in = len(comm.inputs)

    def body(*refs):
        cins, couts, sems = refs[:c_in], refs[c_in:c_in + len(comm.out_shape)], refs[c_in + len(comm.out_shape):]
        ds = comm.descs(cins, couts, sems)
        for d in ds:
            d.start()
        for d in ds:
            d.wait()

    return list(pl.pallas_call(
        body, name=comm.name, in_specs=[ANY] * c_in, out_specs=[ANY] * len(comm.out_shape),
        out_shape=comm.out_shape, scratch_shapes=comm.sems)(*comm.inputs))


def _coords():
    return lax.axis_index("x"), lax.axis_index("y"), lax.axis_index("c")


def _other_chips(x, y):
    return [(1 - x, y), (x, 1 - y), (1 - x, 1 - y)]


def ag_op(shards):
    n = len(shards)

    def descs(cins, couts, sems):
        send, recv = sems
        x, y, c = _coords()
        j = 2 * x + y
        ds = []
        for a in range(n):
            for p, (px, py) in enumerate(_other_chips(x, y)):
                ds.append(pltpu.make_async_remote_copy(cins[a], couts[a].at[j], send.at[a, p], recv.at[a, p],
                                                       device_id=(px, py, c), device_id_type=MESH))
        return ds

    return _Comm("ag", shards, [jax.ShapeDtypeStruct((N_CHIPS,) + s.shape, s.dtype) for s in shards],
                 [pltpu.SemaphoreType.DMA((n, 3)), pltpu.SemaphoreType.DMA((n, 3))], descs)


def _own_slab(gathered, mine, idx):
    return lax.dynamic_update_slice_in_dim(gathered, mine[None], idx, axis=0)


def first_gather(shards):
    n = len(shards)
    halves = [s.reshape(2, s.shape[0] // 2, s.shape[1]) for s in shards]

    def body(*refs):
        ins, outs = refs[:n], refs[n:2 * n]
        send1, recv1, send2, recv2 = refs[2 * n:]
        x, y, c = _coords()
        j = 2 * x + y
        chips = _other_chips(x, y)
        ici = [pltpu.make_async_remote_copy(ins[a].at[c], outs[a].at[j, c], send1.at[a, p], recv1.at[a, p],
                                            device_id=(px, py, c), device_id_type=MESH)
               for a in range(n) for p, (px, py) in enumerate(chips)]
        for d in ici:
            d.start()
        for d in ici:
            d.wait()
        d2d = [pltpu.make_async_remote_copy(outs[a].at[2 * px + py, c], outs[a].at[2 * px + py, c],
                                            send2.at[a, p], recv2.at[a, p],
                                            device_id=(x, y, 1 - c), device_id_type=MESH)
               for a in range(n) for p, (px, py) in enumerate(chips)]
        for d in d2d:
            d.start()
        for d in d2d:
            d.wait()

    return list(pl.pallas_call(
        body, name="first_gather", in_specs=[ANY] * n, out_specs=[ANY] * n,
        out_shape=[jax.ShapeDtypeStruct((N_CHIPS,) + h.shape, h.dtype) for h in halves],
        scratch_shapes=[pltpu.SemaphoreType.DMA((n, 3))] * 4)(*halves))


def swap_op(grads):
    n = len(grads)

    def descs(cins, couts, sems):
        send, recv = sems
        x, y, c = _coords()
        return [pltpu.make_async_remote_copy(cins[a].at[:, 1 - c], couts[a], send.at[a], recv.at[a],
                                             device_id=(x, y, 1 - c), device_id_type=MESH) for a in range(n)]

    return _Comm("swap", grads, [jax.ShapeDtypeStruct(g.shape[:1] + g.shape[2:], g.dtype) for g in grads],
                 [pltpu.SemaphoreType.DMA((n,)), pltpu.SemaphoreType.DMA((n,))], descs)


def exchange_op(parts):
    n = len(parts)

    def descs(cins, couts, sems):
        send, recv = sems
        x, y, c = _coords()
        ds = []
        for a in range(n):
            for p, (px, py) in enumerate(_other_chips(x, y)):
                ds.append(pltpu.make_async_remote_copy(cins[a].at[2 * px + py], couts[a].at[p], send.at[a, p],
                                                       recv.at[a, p], device_id=(px, py, c), device_id_type=MESH))
        return ds

    return _Comm("xchg", parts, [jax.ShapeDtypeStruct((3,) + p.shape[1:], p.dtype) for p in parts],
                 [pltpu.SemaphoreType.DMA((n, 3)), pltpu.SemaphoreType.DMA((n, 3))], descs)


def share_op(totals):
    n = len(totals)

    def descs(cins, couts, sems):
        send, recv = sems
        x, y, c = _coords()
        return [pltpu.make_async_remote_copy(cins[a], couts[a].at[c], send.at[a], recv.at[a],
                                             device_id=(x, y, 1 - c), device_id_type=MESH) for a in range(n)]

    return _Comm("share", totals, [jax.ShapeDtypeStruct((2,) + t.shape, t.dtype) for t in totals],
                 [pltpu.SemaphoreType.DMA((n,)), pltpu.SemaphoreType.DMA((n,))], descs)


def _sigmoid(z):
    return 1.0 / (1.0 + jnp.exp(-z))


def _rms_parts(xf):
    r = lax.rsqrt(jnp.mean(xf * xf, axis=-1, keepdims=True) + EPS)
    return xf * r, r


def _rms_bwd(dh, xhat, r, g):
    dg = jnp.sum(dh * xhat, axis=0, keepdims=True)
    dxhat = dh * g
    dx = r * (dxhat - xhat * jnp.mean(dxhat * xhat, axis=-1, keepdims=True))
    return dx, dg


def _chunks(n, ck=FF_CHUNK):
    return [(c0, min(ck, n - c0)) for c0 in range(0, n, ck)]


def ffn_fwd(x, g, wgT, wuT, wd, *, tm=512, comm=None):
    S, D = x.shape
    F = wgT.shape[0]

    def body(x_ref, g_ref, wg_ref, wu_ref, wd_ref, o_ref, gate_ref, up_ref, a_sc):
        xf = x_ref[...]
        xhat, _ = _rms_parts(xf)
        h = (xhat * g_ref[...]).astype(CDT)
        for c0, cw_ in _chunks(F):
            sl = slice(c0, c0 + cw_)
            gt = lax.dot_general(h, wg_ref[sl, :], NT, preferred_element_type=F32)
            ut = lax.dot_general(h, wu_ref[sl, :], NT, preferred_element_type=F32)
            gate_ref[:, sl] = gt.astype(CDT)
            up_ref[:, sl] = ut.astype(CDT)
            a_sc[:, sl] = (gt * _sigmoid(gt) * ut).astype(CDT)
        o_ref[...] = xf + 0.5 * jnp.dot(a_sc[...], wd_ref[...], preferred_element_type=F32)

    wspec = pl.BlockSpec((F, D), lambda i: (0, 0), pipeline_mode=pl.Buffered(1))
    return _hosted(comm)(
        body, name="ffn_fwd",
        grid=(S // tm,),
        in_specs=[pl.BlockSpec((tm, D), lambda i: (i, 0)), pl.BlockSpec((1, D), lambda i: (0, 0)),
                  wspec, wspec, wspec],
        out_specs=[pl.BlockSpec((tm, D), lambda i: (i, 0)),
                   pl.BlockSpec((tm, F), lambda i: (i, 0)),
                   pl.BlockSpec((tm, F), lambda i: (i, 0))],
        out_shape=[jax.ShapeDtypeStruct((S, D), F32),
                   jax.ShapeDtypeStruct((S, F), CDT),
                   jax.ShapeDtypeStruct((S, F), CDT)],
        scratch_shapes=[pltpu.VMEM((tm, F), CDT)],
        compiler_params=_params(1),
    )(x, g, wgT, wuT, wd)


def ffn_bwd_dgrad(x, g, dy, gate, up, wgT, wuT, wd, *, tm=512, tf=1408, comm=None):
    S, D = x.shape
    F = wgT.shape[0]
    nf = F // tf

    def body(x_ref, g_ref, dy_ref, gate_ref, up_ref, wg_ref, wu_ref, wd_ref,
             dx_ref, dgate_ref, dup_ref, h_ref, dg_ref, dh_sc, dacc_sc):
        i = pl.program_id(0)
        j = pl.program_id(1)

        @pl.when(j == 0)
        def _():
            xhat, _ = _rms_parts(x_ref[...])
            h_ref[...] = (xhat * g_ref[...]).astype(CDT)
            dacc_sc[...] = (0.5 * dy_ref[...]).astype(CDT)
            dh_sc[...] = jnp.zeros_like(dh_sc)

        @pl.when((i == 0) & (j == 0))
        def _():
            dg_ref[...] = jnp.zeros_like(dg_ref)

        dacc = dacc_sc[...]
        for c0, cw_ in _chunks(tf):
            sl = slice(c0, c0 + cw_)
            d_a = lax.dot_general(dacc, wd_ref[sl, :], NT, preferred_element_type=F32)
            gt = gate_ref[:, sl].astype(F32)
            ut = up_ref[:, sl].astype(F32)
            sg = _sigmoid(gt)
            dup_ref[:, sl] = (d_a * (gt * sg)).astype(CDT)
            dgate_ref[:, sl] = (d_a * ut * (sg * (1.0 + gt * (1.0 - sg)))).astype(CDT)
        dh_sc[...] += (jnp.dot(dgate_ref[...], wg_ref[...], preferred_element_type=F32)
                       + jnp.dot(dup_ref[...], wu_ref[...], preferred_element_type=F32))

        @pl.when(j == nf - 1)
        def _():
            xhat, r = _rms_parts(x_ref[...])
            dx, dg = _rms_bwd(dh_sc[...], xhat, r, g_ref[...])
            dx_ref[...] = dy_ref[...] + dx
            dg_ref[...] += dg

    return _hosted(comm)(
        body, name="ffn_bwd_dgrad",
        grid=(S // tm, nf),
        in_specs=[pl.BlockSpec((tm, D), lambda i, j: (i, 0)),
                  pl.BlockSpec((1, D), lambda i, j: (0, 0)),
                  pl.BlockSpec((tm, D), lambda i, j: (i, 0)),
                  pl.BlockSpec((tm, tf), lambda i, j: (i, j)),
                  pl.BlockSpec((tm, tf), lambda i, j: (i, j)),
                  pl.BlockSpec((tf, D), lambda i, j: (j, 0)),
                  pl.BlockSpec((tf, D), lambda i, j: (j, 0)),
                  pl.BlockSpec((tf, D), lambda i, j: (j, 0))],
        out_specs=[pl.BlockSpec((tm, D), lambda i, j: (i, 0)),
                   pl.BlockSpec((tm, tf), lambda i, j: (i, j)),
                   pl.BlockSpec((tm, tf), lambda i, j: (i, j)),
                   pl.BlockSpec((tm, D), lambda i, j: (i, 0)),
                   pl.BlockSpec((1, D), lambda i, j: (0, 0))],
        out_shape=[jax.ShapeDtypeStruct((S, D), F32),
                   jax.ShapeDtypeStruct((S, F), CDT),
                   jax.ShapeDtypeStruct((S, F), CDT),
                   jax.ShapeDtypeStruct((S, D), CDT),
                   jax.ShapeDtypeStruct((1, D), F32)],
        scratch_shapes=[pltpu.VMEM((tm, D), F32), pltpu.VMEM((tm, D), CDT)],
        compiler_params=_params(2),
    )(x, g, dy, gate, up, wgT, wuT, wd)


def ffn_wgrad_down(dy, gate, up, *, tk=1024, comm=None):
    S, D = dy.shape
    F = gate.shape[1]
    tk = min(tk, S)

    def body(dy_ref, gate_ref, up_ref, dwd_ref):
        @pl.when(pl.program_id(0) == 0)
        def _():
            dwd_ref[...] = jnp.zeros_like(dwd_ref)

        dacc = (0.5 * dy_ref[...]).astype(CDT)
        for c0, cw_ in _chunks(F):
            sl = slice(c0, c0 + cw_)
            gt = gate_ref[:, sl].astype(F32)
            a = (gt * _sigmoid(gt) * up_ref[:, sl].astype(F32)).astype(CDT)
            dwd_ref[sl, :] += lax.dot_general(a, dacc, TN, preferred_element_type=F32)

    act = pl.BlockSpec((tk, F), lambda k: (k, 0))
    return _hosted(comm)(
        body, name="ffn_wgrad_down",
        grid=(S // tk,),
        in_specs=[pl.BlockSpec((tk, D), lambda k: (k, 0)), act, act],
        out_specs=pl.BlockSpec((F, D), lambda k: (0, 0), pipeline_mode=pl.Buffered(1)),
        out_shape=jax.ShapeDtypeStruct((F, D), F32),
        compiler_params=_params(1),
    )(dy, gate, up)


def ffn_wgrad_gate_up(h, dgate, dup, *, tk=1024, tf=1408, comm=None):
    S, D = h.shape
    F = dgate.shape[1]
    tk = min(tk, S)

    def body(h_ref, dgate_ref, dup_ref, dwg_ref, dwu_ref):
        k = pl.program_id(1)

        @pl.when(k == 0)
        def _():
            dwg_ref[...] = jnp.zeros_like(dwg_ref)
            dwu_ref[...] = jnp.zeros_like(dwu_ref)

        hh = h_ref[...]
        for c0, cw_ in _chunks(tf):
            sl = slice(c0, c0 + cw_)
            dwg_ref[sl, :] += lax.dot_general(dgate_ref[:, sl], hh, TN, preferred_element_type=F32)
            dwu_ref[sl, :] += lax.dot_general(dup_ref[:, sl], hh, TN, preferred_element_type=F32)

    tok = pl.BlockSpec((tk, D), lambda j, k: (k, 0))
    act = pl.BlockSpec((tk, tf), lambda j, k: (k, j))
    out = pl.BlockSpec((tf, D), lambda j, k: (j, 0))
    return _hosted(comm)(
        body, name="ffn_wgrad_gate_up",
        grid=(F // tf, S // tk),
        in_specs=[tok, act, act],
        out_specs=[out, out],
        out_shape=[jax.ShapeDtypeStruct((F, D), F32)] * 2,
        compiler_params=_params(2),
    )(h, dgate, dup)


def loss_head(x, g, target, *, tm=512):
    S, D = x.shape

    def body(x_ref, g_ref, t_ref, loss_ref, dx_ref, dg_ref):
        @pl.when(pl.program_id(0) == 0)
        def _():
            loss_ref[...] = jnp.zeros_like(loss_ref)
            dg_ref[...] = jnp.zeros_like(dg_ref)

        xhat, r = _rms_parts(x_ref[...])
        gg = g_ref[...]
        err = xhat * gg - t_ref[...]
        loss_ref[...] += 0.5 * jnp.sum(jnp.mean(err * err, axis=-1, keepdims=True), axis=0, keepdims=True)
        dx, dg = _rms_bwd(err * (1.0 / D), xhat, r, gg)
        dx_ref[...] = dx
        dg_ref[...] += dg

    row = pl.BlockSpec((tm, D), lambda i: (i, 0))
    vec = pl.BlockSpec((1, D), lambda i: (0, 0))
    return pl.pallas_call(
        body, name="loss_head",
        grid=(S // tm,),
        in_specs=[row, vec, row],
        out_specs=[pl.BlockSpec((1, 1), lambda i: (0, 0)), row, vec],
        out_shape=[jax.ShapeDtypeStruct((1, 1), F32), jax.ShapeDtypeStruct((S, D), F32),
                   jax.ShapeDtypeStruct((1, D), F32)],
        compiler_params=_params(1),
    )(x, g, target)


def _rope_apply(t, cs, sn):
    lane = lax.broadcasted_iota(jnp.int32, t.shape, 1)
    first = (lane % HEAD_DIM) < (HEAD_DIM // 2)
    rot = jnp.where(first, pltpu.roll(t, 128 - HEAD_DIM // 2, 1), pltpu.roll(t, HEAD_DIM // 2, 1))
    return t * cs + rot * sn


def _rope_transpose(d, cs, sn):
    lane = lax.broadcasted_iota(jnp.int32, d.shape, 1)
    first = (lane % HEAD_DIM) < (HEAD_DIM // 2)
    ds = d * sn
    rot = jnp.where(first, pltpu.roll(ds, 128 - HEAD_DIM // 2, 1), pltpu.roll(ds, HEAD_DIM // 2, 1))
    return d * cs + rot


def inproj_fwd(x, g, wextT, cs, sn, *, tm=512, comm=None):
    S, D = x.shape
    scale = HEAD_DIM ** -0.5

    def body(x_ref, g_ref, w_ref, cs_ref, sn_ref, q_ref, k_ref, v_ref, u_ref):
        xhat, _ = _rms_parts(x_ref[...])
        h = (xhat * g_ref[...]).astype(CDT)
        p = lax.dot_general(h, w_ref[...], NT, preferred_element_type=F32)
        c, s = cs_ref[...], sn_ref[...]
        for b in range(4):
            q_ref[:, 128 * b:128 * (b + 1)] = (_rope_apply(p[:, 128 * b:128 * (b + 1)], c, s) * scale).astype(CDT)
        for b in range(2):
            k_ref[:, 128 * b:128 * (b + 1)] = _rope_apply(p[:, 512 + 128 * b:512 + 128 * (b + 1)], c, s).astype(CDT)
        v_ref[...] = p[:, 768:1024].astype(CDT)
        u_ref[...] = p[:, 1024:2048]

    def row(w):
        return pl.BlockSpec((tm, w), lambda i: (i, 0))

    return _hosted(comm)(
        body, name="inproj_fwd",
        grid=(S // tm,),
        in_specs=[row(D), pl.BlockSpec((1, D), lambda i: (0, 0)),
                  pl.BlockSpec((D_EXT, D), lambda i: (0, 0)), row(128), row(128)],
        out_specs=[row(512), row(256), row(256), row(1024)],
        out_shape=[jax.ShapeDtypeStruct((S, 512), CDT), jax.ShapeDtypeStruct((S, 256), CDT),
                   jax.ShapeDtypeStruct((S, 256), CDT), jax.ShapeDtypeStruct((S, 1024), F32)],
        compiler_params=_params(1),
    )(x, g, wextT, cs, sn)


def _stack_heads(p0, p1):
    lane = lax.broadcasted_iota(jnp.int32, p0.shape, 1)
    lo = lane < HEAD_DIM
    z = jnp.zeros_like(p0)
    return jnp.concatenate([jnp.where(lo, p0, z), jnp.where(lo, z, p0),
                            jnp.where(lo, p1, z), jnp.where(lo, z, p1)], axis=0)


def _unstack_heads(o):
    lane = lax.broadcasted_iota(jnp.int32, (BLOCK, 128), 1)
    lo = lane < HEAD_DIM
    return (jnp.where(lo, o[0:128], o[128:256]), jnp.where(lo, o[256:384], o[384:512]))


def _band_mask_kq(n):
    c = lax.broadcasted_iota(jnp.int32, (2 * BLOCK, 4 * BLOCK), 0)
    i = lax.broadcasted_iota(jnp.int32, (2 * BLOCK, 4 * BLOCK), 1) % BLOCK
    return (c > i) & (c <= i + BLOCK) & ((n > 0) | (c >= BLOCK))


def attn_fwd(q, k, v, sink_row, *, nb=4, comm=None):
    S = q.shape[0]
    tq = nb * BLOCK

    def body(q_ref, k_ref, v_ref, sink_ref, o_ref):
        t = pl.program_id(0)
        for b in range(nb):
            n = t * nb + b
            prev = pl.multiple_of(jnp.maximum(n - 1, 0) * BLOCK, BLOCK)
            cur = pl.multiple_of(n * BLOCK, BLOCK)
            rows = slice(b * BLOCK, (b + 1) * BLOCK)
            mask = _band_mask_kq(n)
            for gidx in range(2):
                lanes = slice(128 * gidx, 128 * (gidx + 1))
                qs = _stack_heads(q_ref[rows, 256 * gidx:256 * gidx + 128],
                                  q_ref[rows, 256 * gidx + 128:256 * gidx + 256])
                kb = jnp.concatenate([k_ref[pl.ds(prev, BLOCK), lanes], k_ref[pl.ds(cur, BLOCK), lanes]], axis=0)
                vb = jnp.concatenate([v_ref[pl.ds(prev, BLOCK), lanes], v_ref[pl.ds(cur, BLOCK), lanes]], axis=0)
                st = lax.dot_general(kb, qs, NT, preferred_element_type=F32)
                st = jnp.where(mask, st, NEG)
                sink = sink_ref[gidx]
                m = jnp.maximum(jnp.max(st, axis=0, keepdims=True), sink)
                e = jnp.exp(st - m)
                inv = 1.0 / (jnp.sum(e, axis=0, keepdims=True) + jnp.exp(sink - m))
                o = lax.dot_general((e * inv).astype(CDT), vb, TN, preferred_element_type=F32)
                o0, o1 = _unstack_heads(o)
                o_ref[rows, 256 * gidx:256 * gidx + 128] = o0.astype(CDT)
                o_ref[rows, 256 * gidx + 128:256 * gidx + 256] = o1.astype(CDT)

    return _hosted(comm)(
        body, name="attn_fwd",
        grid=(S // tq,),
        in_specs=[pl.BlockSpec((tq, 512), lambda t: (t, 0)),
                  pl.BlockSpec((S, 256), lambda t: (0, 0)),
                  pl.BlockSpec((S, 256), lambda t: (0, 0)),
                  pl.BlockSpec((2, 1, 4 * BLOCK), lambda t: (0, 0, 0))],
        out_specs=pl.BlockSpec((tq, 512), lambda t: (t, 0)),
        out_shape=jax.ShapeDtypeStruct((S, 512), CDT),
        compiler_params=_params(1),
    )(q, k, v, sink_row)


def attn_bwd(q, k, v, do, sink_row, *, nb=4, comm=None):
    S = q.shape[0]
    tq = nb * BLOCK
    scale = HEAD_DIM ** -0.5

    def body(q_ref, k_ref, v_ref, do_ref, sink_ref, dq_ref, dk_ref, dv_ref, dsink_ref):
        t = pl.program_id(0)

        @pl.when(t == 0)
        def _():
            dk_ref[...] = jnp.zeros_like(dk_ref)
            dv_ref[...] = jnp.zeros_like(dv_ref)
            dsink_ref[...] = jnp.zeros_like(dsink_ref)

        for b in range(nb):
            n = t * nb + b
            prev = pl.multiple_of(jnp.maximum(n - 1, 0) * BLOCK, BLOCK)
            cur = pl.multiple_of(n * BLOCK, BLOCK)
            rows = slice(b * BLOCK, (b + 1) * BLOCK)
            mask = _band_mask_kq(n)
            for gidx in range(2):
                lanes = slice(128 * gidx, 128 * (gidx + 1))
                qs = _stack_heads(q_ref[rows, 256 * gidx:256 * gidx + 128],
                                  q_ref[rows, 256 * gidx + 128:256 * gidx + 256])
                dos = _stack_heads(do_ref[rows, 256 * gidx:256 * gidx + 128],
                                   do_ref[rows, 256 * gidx + 128:256 * gidx + 256])
                kb = jnp.concatenate([k_ref[pl.ds(prev, BLOCK), lanes], k_ref[pl.ds(cur, BLOCK), lanes]], axis=0)
                vb = jnp.concatenate([v_ref[pl.ds(prev, BLOCK), lanes], v_ref[pl.ds(cur, BLOCK), lanes]], axis=0)
                st = lax.dot_general(kb, qs, NT, preferred_element_type=F32)
                st = jnp.where(mask, st, NEG)
                sink = sink_ref[gidx]
                m = jnp.maximum(jnp.max(st, axis=0, keepdims=True), sink)
                e = jnp.exp(st - m)
                es = jnp.exp(sink - m)
                inv = 1.0 / (jnp.sum(e, axis=0, keepdims=True) + es)
                pt = e * inv
                dpt = lax.dot_general(vb, dos, NT, preferred_element_type=F32)
                delta = jnp.sum(pt * dpt, axis=0, keepdims=True)
                dst = (pt * (dpt - delta)).astype(CDT)
                dsink_ref[gidx] += -(es * inv) * delta
                dvb = jnp.dot(pt.astype(CDT), dos, preferred_element_type=F32)
                dkb = jnp.dot(dst, qs, preferred_element_type=F32)
                dqs = lax.dot_general(dst, kb, TN, preferred_element_type=F32) * scale
                dq0, dq1 = _unstack_heads(dqs)
                dq_ref[rows, 256 * gidx:256 * gidx + 128] = dq0
                dq_ref[rows, 256 * gidx + 128:256 * gidx + 256] = dq1
                dk_ref[pl.ds(prev, BLOCK), lanes] += dkb[0:BLOCK]
                dk_ref[pl.ds(cur, BLOCK), lanes] += dkb[BLOCK:2 * BLOCK]
                dv_ref[pl.ds(prev, BLOCK), lanes] += dvb[0:BLOCK]
                dv_ref[pl.ds(cur, BLOCK), lanes] += dvb[BLOCK:2 * BLOCK]

    full = pl.BlockSpec((S, 256), lambda t: (0, 0))
    tile = pl.BlockSpec((tq, 512), lambda t: (t, 0))
    srow = pl.BlockSpec((2, 1, 4 * BLOCK), lambda t: (0, 0, 0))
    return _hosted(comm)(
        body, name="attn_bwd",
        grid=(S // tq,),
        in_specs=[tile, full, full, tile, srow],
        out_specs=[tile, full, full, srow],
        out_shape=[jax.ShapeDtypeStruct((S, 512), F32), jax.ShapeDtypeStruct((S, 256), F32),
                   jax.ShapeDtypeStruct((S, 256), F32), jax.ShapeDtypeStruct((2, 1, 4 * BLOCK), F32)],
        compiler_params=_params(1),
    )(q, k, v, do, sink_row)


def _glu(u):
    a = u[:, 0:CONV_C]
    gt = u[:, CONV_C:2 * CONV_C]
    sg = _sigmoid(gt)
    return a, sg, a * sg


CONV_CHUNK = 32


def _shifted_copies(buf, shifted, n):
    for r in range(1, 8):
        shifted[r - 1, 0:n, :] = buf[r:r + n, :]


def _shifted_rows(buf, shifted, start, rows):
    r = start % 8
    if r == 0:
        return buf[start:start + rows, :]
    return shifted[r - 1, start - r:start - r + rows, :]


def conv_fwd(u, cw, cb, lg, lb, *, tm=512, comm=None):
    S = u.shape[0]
    nh = tm // HALO

    def body(u_ref, uh_ref, cw_ref, cb_ref, lg_ref, lb_ref, o_ref, y_ref, hbuf, hsh):
        t = pl.program_id(0)
        _, _, hg = _glu(u_ref[...])
        _, _, hh = _glu(uh_ref[...])
        hbuf[0:HALO, :] = jnp.where(t > 0, hh, jnp.zeros_like(hh))
        hbuf[HALO:HALO + tm, :] = hg
        hbuf[HALO + tm:HALO + tm + 8, :] = jnp.zeros((8, CONV_C), F32)
        _shifted_copies(hbuf, hsh, HALO + tm)
        off = HALO - (CONV_K - 1)
        for c0 in range(0, tm, CONV_CHUNK):
            acc = jnp.zeros((CONV_CHUNK, CONV_C), F32) + cb_ref[...]
            for j in range(CONV_K):
                acc = acc + cw_ref[j:j + 1, :] * _shifted_rows(hbuf, hsh, c0 + off + j, CONV_CHUNK)
            y_ref[c0:c0 + CONV_CHUNK, :] = acc
        y = y_ref[...]
        yc = y - jnp.mean(y, axis=-1, keepdims=True)
        r = lax.rsqrt(jnp.mean(yc * yc, axis=-1, keepdims=True) + EPS)
        z = yc * r * lg_ref[...] + lb_ref[...]
        o_ref[...] = (z * _sigmoid(z)).astype(CDT)

    vec = pl.BlockSpec((1, CONV_C), lambda t: (0, 0))
    return _hosted(comm)(
        body, name="conv_fwd",
        grid=(S // tm,),
        in_specs=[pl.BlockSpec((tm, 2 * CONV_C), lambda t: (t, 0)),
                  pl.BlockSpec((HALO, 2 * CONV_C), lambda t: (jnp.maximum(t * nh - 1, 0), 0)),
                  pl.BlockSpec((CONV_K, CONV_C), lambda t: (0, 0)), vec, vec, vec],
        out_specs=[pl.BlockSpec((tm, CONV_C), lambda t: (t, 0)), pl.BlockSpec((tm, CONV_C), lambda t: (t, 0))],
        out_shape=[jax.ShapeDtypeStruct((S, CONV_C), CDT), jax.ShapeDtypeStruct((S, CONV_C), F32)],
        scratch_shapes=[pltpu.VMEM((HALO + tm + 8, CONV_C), F32), pltpu.VMEM((7, HALO + tm, CONV_C), F32)],
        compiler_params=_params(1),
    )(u, u, cw, cb, lg, lb)


def conv_bwd(dc, u, y, cw, lg, lb, *, tm=512, comm=None):
    S = u.shape[0]
    nh = tm // HALO
    nt = S // tm

    def ln_bwd(dcv, yv, lgv, lbv):
        yc = yv - jnp.mean(yv, axis=-1, keepdims=True)
        r = lax.rsqrt(jnp.mean(yc * yc, axis=-1, keepdims=True) + EPS)
        yhat = yc * r
        z = yhat * lgv + lbv
        sg = _sigmoid(z)
        dz = dcv * (sg * (1.0 + z * (1.0 - sg)))
        dyhat = dz * lgv
        dy = r * (dyhat - jnp.mean(dyhat, axis=-1, keepdims=True)
                  - yhat * jnp.mean(dyhat * yhat, axis=-1, keepdims=True))
        return dy, dz, yhat

    def body(dc_ref, dcn_ref, u_ref, uh_ref, y_ref, yn_ref, cw_ref, lg_ref, lb_ref,
             du_ref, dcw_ref, dcb_ref, dlg_ref, dlb_ref, hbuf, dybuf, dhg_sc, dw_sc, hsh, dysh):
        t = pl.program_id(0)

        @pl.when(t == 0)
        def _():
            dw_sc[...] = jnp.zeros_like(dw_sc)
            dcb_ref[...] = jnp.zeros_like(dcb_ref)
            dlg_ref[...] = jnp.zeros_like(dlg_ref)
            dlb_ref[...] = jnp.zeros_like(dlb_ref)

        lgv, lbv = lg_ref[...], lb_ref[...]
        dy, dz, yhat = ln_bwd(dc_ref[...].astype(F32), y_ref[...], lgv, lbv)
        dyn, _, _ = ln_bwd(dcn_ref[...].astype(F32), yn_ref[...], lgv, lbv)
        dlb_ref[...] += jnp.sum(dz, axis=0, keepdims=True)
        dlg_ref[...] += jnp.sum(dz * yhat, axis=0, keepdims=True)
        dcb_ref[...] += jnp.sum(dy, axis=0, keepdims=True)
        dybuf[0:tm, :] = dy
        dybuf[tm:tm + HALO, :] = jnp.where(t < nt - 1, dyn, jnp.zeros_like(dyn))
        dybuf[tm + HALO:tm + HALO + 8, :] = jnp.zeros((8, CONV_C), F32)
        _shifted_copies(dybuf, dysh, tm + HALO)

        a, sg, hg = _glu(u_ref[...])
        _, _, hh = _glu(uh_ref[...])
        hbuf[0:HALO, :] = jnp.where(t > 0, hh, jnp.zeros_like(hh))
        hbuf[HALO:HALO + tm, :] = hg
        hbuf[HALO + tm:HALO + tm + 8, :] = jnp.zeros((8, CONV_C), F32)
        _shifted_copies(hbuf, hsh, HALO + tm)

        off = HALO - (CONV_K - 1)
        for c0 in range(0, tm, CONV_CHUNK):
            acc = jnp.zeros((CONV_CHUNK, CONV_C), F32)
            dyc = dybuf[c0:c0 + CONV_CHUNK, :]
            for j in range(CONV_K):
                acc = acc + cw_ref[j:j + 1, :] * _shifted_rows(dybuf, dysh, c0 + (CONV_K - 1) - j, CONV_CHUNK)
                prod = dyc * _shifted_rows(hbuf, hsh, c0 + off + j, CONV_CHUNK)
                dw_sc[j] += jnp.sum(prod.reshape(CONV_CHUNK // 8, 8, CONV_C), axis=0)
            dhg_sc[c0:c0 + CONV_CHUNK, :] = acc

        dhg = dhg_sc[...]
        du_ref[:, 0:CONV_C] = dhg * sg
        du_ref[:, CONV_C:2 * CONV_C] = dhg * a * sg * (1.0 - sg)

        @pl.when(t == nt - 1)
        def _():
            dcw_ref[...] = jnp.sum(dw_sc[...], axis=1)

    vec = pl.BlockSpec((1, CONV_C), lambda t: (0, 0))
    tile = pl.BlockSpec((tm, CONV_C), lambda t: (t, 0))
    nxt = pl.BlockSpec((HALO, CONV_C), lambda t: (jnp.minimum((t + 1) * nh, S // HALO - 1), 0))
    return _hosted(comm)(
        body, name="conv_bwd",
        grid=(nt,),
        in_specs=[tile, nxt,
                  pl.BlockSpec((tm, 2 * CONV_C), lambda t: (t, 0)),
                  pl.BlockSpec((HALO, 2 * CONV_C), lambda t: (jnp.maximum(t * nh - 1, 0), 0)),
                  tile, nxt,
                  pl.BlockSpec((CONV_K, CONV_C), lambda t: (0, 0)), vec, vec],
        out_specs=[pl.BlockSpec((tm, 2 * CONV_C), lambda t: (t, 0)),
                   pl.BlockSpec((CONV_K, CONV_C), lambda t: (0, 0)), vec, vec, vec],
        out_shape=[jax.ShapeDtypeStruct((S, 2 * CONV_C), F32), jax.ShapeDtypeStruct((CONV_K, CONV_C), F32),
                   jax.ShapeDtypeStruct((1, CONV_C), F32), jax.ShapeDtypeStruct((1, CONV_C), F32),
                   jax.ShapeDtypeStruct((1, CONV_C), F32)],
        scratch_shapes=[pltpu.VMEM((HALO + tm + 8, CONV_C), F32), pltpu.VMEM((tm + HALO + 8, CONV_C), F32),
                        pltpu.VMEM((tm, CONV_C), F32), pltpu.VMEM((CONV_K, 8, CONV_C), F32),
                        pltpu.VMEM((7, HALO + tm, CONV_C), F32), pltpu.VMEM((7, tm + HALO, CONV_C), F32)],
        compiler_params=_params(1),
    )(dc, dc, u, u, y, y, cw, lg, lb)


def outproj_fwd(x, ao, co, wout, *, tm=512):
    S, D = x.shape

    def body(x_ref, a_ref, c_ref, w_ref, o_ref):
        o_ref[...] = (x_ref[...]
                      + jnp.dot(a_ref[...], w_ref[0:ATT_W, :], preferred_element_type=F32)
                      + jnp.dot(c_ref[...], w_ref[ATT_W:ATT_W + CONV_C, :], preferred_element_type=F32))

    return pl.pallas_call(
        body, name="outproj_fwd",
        grid=(S // tm,),
        in_specs=[pl.BlockSpec((tm, D), lambda i: (i, 0)), pl.BlockSpec((tm, ATT_W), lambda i: (i, 0)),
                  pl.BlockSpec((tm, CONV_C), lambda i: (i, 0)), pl.BlockSpec((D, D), lambda i: (0, 0))],
        out_specs=pl.BlockSpec((tm, D), lambda i: (i, 0)),
        out_shape=jax.ShapeDtypeStruct((S, D), F32),
        compiler_params=_params(1),
    )(x, ao, co, wout)


def outproj_bwd(dx, ao, co, wout, *, tm=512, comm=None):
    S, D = dx.shape

    def body(dx_ref, a_ref, c_ref, w_ref, da_ref, dc_ref, dw_ref):
        @pl.when(pl.program_id(0) == 0)
        def _():
            dw_ref[...] = jnp.zeros_like(dw_ref)

        dxb = dx_ref[...].astype(CDT)
        da_ref[...] = lax.dot_general(dxb, w_ref[0:ATT_W, :], NT, preferred_element_type=F32).astype(CDT)
        dc_ref[...] = lax.dot_general(dxb, w_ref[ATT_W:ATT_W + CONV_C, :], NT, preferred_element_type=F32)
        dw_ref[0:ATT_W, :] += lax.dot_general(a_ref[...], dxb, TN, preferred_element_type=F32)
        dw_ref[ATT_W:ATT_W + CONV_C, :] += lax.dot_general(c_ref[...], dxb, TN, preferred_element_type=F32)

    return _hosted(comm)(
        body, name="outproj_bwd",
        grid=(S // tm,),
        in_specs=[pl.BlockSpec((tm, D), lambda i: (i, 0)), pl.BlockSpec((tm, ATT_W), lambda i: (i, 0)),
                  pl.BlockSpec((tm, CONV_C), lambda i: (i, 0)), pl.BlockSpec((D, D), lambda i: (0, 0))],
        out_specs=[pl.BlockSpec((tm, ATT_W), lambda i: (i, 0)), pl.BlockSpec((tm, CONV_C), lambda i: (i, 0)),
                   pl.BlockSpec((D, D), lambda i: (0, 0))],
        out_shape=[jax.ShapeDtypeStruct((S, ATT_W), CDT), jax.ShapeDtypeStruct((S, CONV_C), F32),
                   jax.ShapeDtypeStruct((D, D), F32)],
        compiler_params=_params(1),
    )(dx, ao, co, wout)


def inproj_bwd(x, g, dres, dq, dk, dv, du, wextT, cs, sn, *, tm=512):
    S, D = x.shape

    def body(x_ref, g_ref, dres_ref, dq_ref, dk_ref, dv_ref, du_ref, w_ref, cs_ref, sn_ref,
             dx_ref, dw_ref, dg_ref, dp_sc):
        @pl.when(pl.program_id(0) == 0)
        def _():
            dw_ref[...] = jnp.zeros_like(dw_ref)
            dg_ref[...] = jnp.zeros_like(dg_ref)

        c, s = cs_ref[...], sn_ref[...]
        for b in range(4):
            dp_sc[:, 128 * b:128 * (b + 1)] = _rope_transpose(dq_ref[:, 128 * b:128 * (b + 1)], c, s).astype(CDT)
        for b in range(2):
            dp_sc[:, 512 + 128 * b:512 + 128 * (b + 1)] = _rope_transpose(
                dk_ref[:, 128 * b:128 * (b + 1)], c, s).astype(CDT)
        dp_sc[:, 768:1024] = dv_ref[...].astype(CDT)
        dp_sc[:, 1024:2048] = du_ref[...].astype(CDT)
        dp = dp_sc[...]
        xhat, r = _rms_parts(x_ref[...])
        gg = g_ref[...]
        h = (xhat * gg).astype(CDT)
        dh = jnp.dot(dp, w_ref[...], preferred_element_type=F32)
        dw_ref[...] += lax.dot_general(dp, h, TN, preferred_element_type=F32)
        dx, dg = _rms_bwd(dh, xhat, r, gg)
        dx_ref[...] = dres_ref[...] + dx
        dg_ref[...] += dg

    def row(w):
        return pl.BlockSpec((tm, w), lambda i: (i, 0))

    return pl.pallas_call(
        body, name="inproj_bwd",
        grid=(S // tm,),
        in_specs=[row(D), pl.BlockSpec((1, D), lambda i: (0, 0)), row(D), row(512), row(256), row(256),
                  row(1024), pl.BlockSpec((D_EXT, D), lambda i: (0, 0), pipeline_mode=pl.Buffered(1)),
                  row(128), row(128)],
        out_specs=[row(D), pl.BlockSpec((D_EXT, D), lambda i: (0, 0), pipeline_mode=pl.Buffered(1)),
                   pl.BlockSpec((1, D), lambda i: (0, 0))],
        out_shape=[jax.ShapeDtypeStruct((S, D), F32), jax.ShapeDtypeStruct((D_EXT, D), F32),
                   jax.ShapeDtypeStruct((1, D), F32)],
        scratch_shapes=[pltpu.VMEM((tm, D_EXT), CDT)],
        compiler_params=_params(1),
    )(x, g, dres, dq, dk, dv, du, wextT, cs, sn)


def _rope_tables(positions):
    inv_freq = 1.0 / (10000.0 ** (jnp.arange(0, HEAD_DIM, 2, dtype=F32) / HEAD_DIM))
    ang = positions.astype(F32).reshape(-1, 1) * inv_freq
    cos, sin = jnp.cos(ang), jnp.sin(ang)
    cs = jnp.tile(jnp.concatenate([cos, cos], axis=-1), (1, 2))
    sn = jnp.tile(jnp.concatenate([-sin, sin], axis=-1), (1, 2))
    return cs, sn


def _widen_w_in(w):
    q, u = w[0:512], w[768:1792]
    parts = [q]
    for base in (512, 576, 640, 704):
        parts += [w[base:base + 64], w[base:base + 64]]
    return jnp.concatenate(parts + [u], axis=0)


def _fold_w_in(d):
    parts = [d[0:512]]
    for base in (512, 640, 768, 896):
        parts.append(d[base:base + 64] + d[base + 64:base + 128])
    return jnp.concatenate(parts + [d[1024:2048]], axis=0)


def add_half(g5, r1, c_idx):
    _, _, r, D = g5.shape

    def body(c_ref, g_ref, r_ref, o_ref):
        o_ref[...] = (g_ref[...] + r_ref[...]).astype(CDT)

    return pl.pallas_call(
        body, name="add_half",
        grid_spec=pltpu.PrefetchScalarGridSpec(
            num_scalar_prefetch=1, grid=(N_CHIPS,),
            in_specs=[pl.BlockSpec((None, None, r, D), lambda s, cr: (s, cr[0], 0, 0)),
                      pl.BlockSpec((None, r, D), lambda s, cr: (s, 0, 0))],
            out_specs=pl.BlockSpec((None, r, D), lambda s, cr: (s, 0, 0))),
        out_shape=jax.ShapeDtypeStruct((N_CHIPS, r, D), CDT),
        compiler_params=_params(1),
    )(c_idx, g5, r1)


def sum_partials(part, recv3, j_idx):
    _, r, D = part.shape
    tr_ = r // 2

    def body(j_ref, p_ref, r_ref, o_ref):
        o_ref[...] = ((p_ref[...].astype(F32) + r_ref[0].astype(F32)) + r_ref[1].astype(F32)) + r_ref[2].astype(F32)

    return pl.pallas_call(
        body, name="sum_partials",
        grid_spec=pltpu.PrefetchScalarGridSpec(
            num_scalar_prefetch=1, grid=(2,),
            in_specs=[pl.BlockSpec((None, tr_, D), lambda i, jr: (jr[0], i, 0)),
                      pl.BlockSpec((3, tr_, D), lambda i, jr: (0, i, 0))],
            out_specs=pl.BlockSpec((tr_, D), lambda i, jr: (i, 0))),
        out_shape=jax.ShapeDtypeStruct((r, D), F32),
        compiler_params=_params(1),
    )(j_idx, part, recv3)


class _Chain:
    STAGES = ("swap", "xchg", "share")

    def __init__(self, grads, c_idx, j_idx):
        self.c_idx, self.j_idx = c_idx, j_idx
        self.g5 = [g.reshape(N_CHIPS, 2, g.shape[0] // (2 * N_CHIPS), g.shape[1]) for g in grads]
        self.stage_no = 0

    @property
    def done(self):
        return self.stage_no == len(self.STAGES)

    def next_stage(self):
        name = self.STAGES[self.stage_no]

        def callback(res):
            getattr(self, "after_" + name)(res)
            self.stage_no += 1

        return getattr(self, name)(), callback

    def swap(self):
        return swap_op(self.g5)

    def after_swap(self, recv):
        self.parts = [add_half(g, r, self.c_idx) for g, r in zip(self.g5, recv)]

    def xchg(self):
        return exchange_op(self.parts)

    def after_xchg(self, recv):
        self.totals = [sum_partials(p, r, self.j_idx) for p, r in zip(self.parts, recv)]

    def share(self):
        return share_op(self.totals)

    def after_share(self, recv):
        both = [_own_slab(h, t, self.c_idx[0]) for h, t in zip(recv, self.totals)]
        self.final = [h.reshape(2 * h.shape[1], h.shape[2]) for h in both]


def all_reduce_small(vec):
    R = vec.shape[0]

    def body(v_ref, o_ref, buf, send, recv):
        x, y, c = _coords()
        me = 4 * x + 2 * y + c
        buf[me] = v_ref[...]
        cps = []
        for m in range(1, N_DEV):
            dx, dy, dc = (m >> 2) & 1, (m >> 1) & 1, m & 1
            cp = pltpu.make_async_remote_copy(v_ref, buf.at[me], send.at[m - 1], recv.at[m - 1],
                                              device_id=((x + dx) % 2, (y + dy) % 2, (c + dc) % 2),
                                              device_id_type=MESH)
            cp.start()
            cps.append(cp)
        for cp in cps:
            cp.wait()
        acc = buf[0]
        for d in range(1, N_DEV):
            acc = acc + buf[d]
        o_ref[...] = acc

    return pl.pallas_call(
        body, name="all_reduce_small",
        in_specs=[pl.BlockSpec(memory_space=pltpu.VMEM)], out_specs=pl.BlockSpec(memory_space=pltpu.VMEM),
        out_shape=jax.ShapeDtypeStruct(vec.shape, F32),
        scratch_shapes=[pltpu.VMEM((N_DEV, R, 128), F32), pltpu.SemaphoreType.DMA((N_DEV - 1,)),
                        pltpu.SemaphoreType.DMA((N_DEV - 1,))],
    )(vec)


def adamw(w, g, m, v, *, tm=512):
    R, C = w.shape
    tm = max(t for t in range(8, min(tm, R) + 1, 8) if R % t == 0)
    c1 =1.0 - ADAM_B1 ** ADAM_STEP
    c2 = 1.0 - ADAM_B2 ** ADAM_STEP

    def body(w_ref, g_ref, m_ref, v_ref, d_ref, nm_ref, nv_ref):
        gg = g_ref[...]
        nm = ADAM_B1 * m_ref[...] + (1.0 - ADAM_B1) * gg
        nv = ADAM_B2 * v_ref[...] + (1.0 - ADAM_B2) * (gg * gg)
        nm_ref[...] = nm
        nv_ref[...] = nv
        d_ref[...] = -ADAM_LR * ((nm / c1) / (jnp.sqrt(nv / c2) + ADAM_EPS) + ADAM_WD * w_ref[...])

    blk = pl.BlockSpec((tm, C), lambda i: (i, 0))
    return pl.pallas_call(
        body, name="adamw",
        grid=(pl.cdiv(R, tm),),
        in_specs=[blk] * 4, out_specs=[blk] * 3,
        out_shape=[jax.ShapeDtypeStruct((R, C), F32)] * 3,
        compiler_params=_params(1),
    )(w, g, m, v)


def adamw_layers(w, m, v, g_layers, *, tm=352):
    L, R, C = w.shape
    tm = max(t for t in range(8, min(tm, R) + 1, 8) if R % t == 0)
    c1 = 1.0 - ADAM_B1 ** ADAM_STEP
    c2 = 1.0 - ADAM_B2 ** ADAM_STEP

    def body(w_ref, m_ref, v_ref, *rest):
        g_refs, (go_ref, d_ref, nm_ref, nv_ref) = rest[:L], rest[L:]
        layer = pl.program_id(0)
        gg = g_refs[0][...]
        for l in range(1, L):
            gg = jnp.where(layer == l, g_refs[l][...], gg)
        nm = ADAM_B1 * m_ref[...] + (1.0 - ADAM_B1) * gg
        nv = ADAM_B2 * v_ref[...] + (1.0 - ADAM_B2) * (gg * gg)
        go_ref[...] = gg
        nm_ref[...] = nm
        nv_ref[...] = nv
        d_ref[...] = -ADAM_LR * ((nm / c1) / (jnp.sqrt(nv / c2) + ADAM_EPS) + ADAM_WD * w_ref[...])

    blk = pl.BlockSpec((None, tm, C), lambda l, i: (l, i, 0))
    gblk = pl.BlockSpec((tm, C), lambda l, i: (i, 0))
    return pl.pallas_call(
        body, name="adamw_layers",
        grid=(L, R // tm),
        in_specs=[blk] * 3 + [gblk] * L, out_specs=[blk] * 4,
        out_shape=[jax.ShapeDtypeStruct((L, R, C), F32)] * 4,
        compiler_params=_params(2),
    )(w, m, v, *g_layers)


_SMALL = (("n1", (2, D_MODEL)), ("nm", (2, D_MODEL)), ("n2", (2, D_MODEL)), ("nf", (D_MODEL,)),
          ("cb", (2, CONV_C)), ("lg", (2, CONV_C)), ("lb", (2, CONV_C)), ("sinks", (2, N_HEADS)),
          ("cw", (2, CONV_K, CONV_C)))


def _pack(parts, rows):
    flat = jnp.concatenate([p.reshape(-1).astype(F32) for p in parts])
    return jnp.pad(flat, (0, rows * 128 - flat.shape[0])).reshape(rows, 128)


def _unpack(block, shapes):
    flat = block.reshape(-1)
    out, o = [], 0
    for shp in shapes:
        n = 1
        for s in shp:
            n *= s
        out.append(flat[o:o + n].reshape(shp))
        o += n
    return out


def kernel(x, positions, ffn1_norm, ffn1_w_gate, ffn1_w_up, ffn1_w_down, mix_norm, w_in, conv_w, conv_b, conv_ln_g, conv_ln_b, attn_sinks, w_out, ffn2_norm, ffn2_w_gate, ffn2_w_up, ffn2_w_down, final_norm, loss_target, m_ffn1_norm, m_ffn1_w_gate, m_ffn1_w_up, m_ffn1_w_down, m_mix_norm, m_w_in, m_conv_w, m_conv_b, m_conv_ln_g, m_conv_ln_b, m_attn_sinks, m_w_out, m_ffn2_norm, m_ffn2_w_gate, m_ffn2_w_up, m_ffn2_w_down, m_final_norm, v_ffn1_norm, v_ffn1_w_gate, v_ffn1_w_up, v_ffn1_w_down, v_mix_norm, v_w_in, v_conv_w, v_conv_b, v_conv_ln_g, v_conv_ln_b, v_attn_sinks, v_w_out, v_ffn2_norm, v_ffn2_w_gate, v_ffn2_w_up, v_ffn2_w_down, v_final_norm):
    cx, cy, cc = _coords()
    chip = 2 * cx + cy
    c_idx = jnp.reshape(cc, (1,)).astype(jnp.int32)
    j_idx = jnp.reshape(chip, (1,)).astype(jnp.int32)
    L = ffn1_norm.shape[0]
    tr = lambda a: jnp.swapaxes(a, 1, 2)

    sh = dict(f1g=tr(ffn1_w_gate), f1u=tr(ffn1_w_up), f1d=ffn1_w_down, f2g=tr(ffn2_w_gate),
              f2u=tr(ffn2_w_up), f2d=ffn2_w_down, win=tr(w_in), wout=w_out)
    sh = {k: [v[l].astype(CDT) for l in range(L)] for k, v in sh.items()}
    W = {}

    def gather_op(keys):
        return ag_op([conv_w if k == "cw" else sh[k[0]][k[1]] for k in keys])

    def take(keys, res):
        for k, a in zip(keys, res):
            if k == "cw":
                W[k] = _own_slab(a, conv_w, chip)
            else:
                W[k] = _own_slab(a, sh[k[0]][k[1]], chip).reshape(N_CHIPS * a.shape[1], a.shape[2])

    def with_ag(fn, keys, *args):
        if not keys:
            return fn(*args)
        main, res = fn(*args, comm=gather_op(keys))
        take(keys, res)
        return main

    ag_hosts = {("ffn1", 0): [("win", 0), "cw", ("f2g", 0), ("f1d", 1)],
                ("inproj", 0): [("wout", 0)], ("attn", 0): [("f2u", 0)], ("conv", 0): [("f2d", 0)],
                ("ffn2", 0): [("f1g", 1), ("f1u", 1)],
                ("ffn1", 1): [("win", 1), ("f2g", 1), ("f2u", 1)],
                ("inproj", 1): [("wout", 1)], ("attn", 1): [("f2d", 1)]}
    for k_, a in zip(("f1g", "f1u", "f1d"), first_gather([sh[k_][0] for k_ in ("f1g", "f1u", "f1d")])):
        own = sh[k_][0].reshape(a.shape[1:])
        W[(k_, 0)] = _own_slab(a, own, chip).reshape(N_CHIPS * sh[k_][0].shape[0], sh[k_][0].shape[1])

    cs, sn = _rope_tables(positions)
    saved = []
    h = x[0]
    for l in range(L):
        sink = attn_sinks[l].reshape(2, 4)
        sink_row = jnp.repeat(sink, BLOCK, axis=1).reshape(2, 1, 4 * BLOCK)
        x0 = h
        x1, g1, u1 = with_ag(ffn_fwd, ag_hosts.get(("ffn1", l)), x0, ffn1_norm[l][None],
                             W[("f1g", l)], W[("f1u", l)], W[("f1d", l)])
        wext = _widen_w_in(W[("win", l)])
        q, k, v, u = with_ag(inproj_fwd, ag_hosts.get(("inproj", l)), x1, mix_norm[l][None], wext, cs, sn)
        ao = with_ag(attn_fwd, ag_hosts.get(("attn", l)), q, k, v, sink_row)
        cwl = jnp.transpose(W["cw"][:, l], (1, 0, 2)).reshape(CONV_K, CONV_C)
        co, yc = with_ag(conv_fwd, ag_hosts.get(("conv", l)), u, cwl, conv_b[l][None], conv_ln_g[l][None],
                         conv_ln_b[l][None])
        x2 = outproj_fwd(x1, ao, co, W[("wout", l)])
        x3, g2, u2 = with_ag(ffn_fwd, ag_hosts.get(("ffn2", l)), x2, ffn2_norm[l][None],
                             W[("f2g", l)], W[("f2u", l)], W[("f2d", l)])
        saved.append((x0, x1, x2, g1, u1, g2, u2, q, k, v, u, ao, co, yc, sink, wext, cwl))
        h = x3

    loss, dx, dnf = loss_head(h, final_norm[None], loss_target[0])

    active = []

    def advance(run):
        stages = [ch.next_stage() for ch in active]
        ops = [op for op, _ in stages]
        main, res = run(_merge(*ops) if ops else None)
        for (_, cb), r in zip(stages, _split(res, *ops)):
            cb(r)
        active[:] = [ch for ch in active if not ch.done]
        return main

    def hosted(fn, *args):
        def run(comm):
            if comm is None:
                return fn(*args), []
            return fn(*args, comm=comm)
        return advance(run)

    def chain(key, grads):
        chains[key] = _Chain(grads, c_idx, j_idx)
        active.append(chains[key])

    small = {k_: [None] * L for k_ in ("n1", "nm", "n2", "cw", "cb", "lg", "lb", "sinks")}
    chains = {}
    for l in reversed(range(L)):
        x0, x1, x2, g1, u1, g2, u2, q, k, v, u, ao, co, yc, sink, wext, cwl = saved[l]
        sink_row = jnp.repeat(sink, BLOCK, axis=1).reshape(2, 1, 4 * BLOCK)
        chain(("f2d", l), [hosted(ffn_wgrad_down, dx, g2, u2)])
        dx2, dgt, dup, hh, small["n2"][l] = hosted(
            ffn_bwd_dgrad, x2, ffn2_norm[l][None], dx, g2, u2, W[("f2g", l)], W[("f2u", l)], W[("f2d", l)])
        chain(("f2gu", l), hosted(ffn_wgrad_gate_up, hh, dgt, dup))
        da, dc, gwout = hosted(outproj_bwd, dx2, ao, co, W[("wout", l)])
        du, small["cw"][l], small["cb"][l], small["lg"][l], small["lb"][l] = hosted(
            conv_bwd, dc, u, yc, cwl, conv_ln_g[l][None], conv_ln_b[l][None])
        dq, dk, dv, dsink = hosted(attn_bwd, q, k, v, da, sink_row)
        small["sinks"][l] = jnp.sum(dsink.reshape(2, 4, BLOCK), axis=-1).reshape(N_HEADS)
        dx1, gwext, small["nm"][l] = inproj_bwd(x1, mix_norm[l][None], dx2, dq, dk, dv, du, wext, cs, sn)
        chain(("mx", l), [gwout, _fold_w_in(gwext)])
        chain(("f1d", l), [hosted(ffn_wgrad_down, dx1, g1, u1)])
        dx, dgt, dup, hh, small["n1"][l] = hosted(
            ffn_bwd_dgrad, x0, ffn1_norm[l][None], dx1, g1, u1, W[("f1g", l)], W[("f1u", l)], W[("f1d", l)])
        chain(("f1gu", l), hosted(ffn_wgrad_gate_up, hh, dgt, dup))

    while active:
        advance(lambda comm: (None, _run_comm(comm)))

    G = {k_: jnp.stack(v_) for k_, v_ in small.items()}
    G["nf"] = dnf
    small_shapes = [shp for _, shp in _SMALL]
    n_small = 1 + sum(math.prod(s) for s in small_shapes)
    rows = -(-n_small // 1024) * 8
    packed = _pack([loss] + [G[k_] for k_, _ in _SMALL], rows)
    summed = all_reduce_small(packed)
    loss_out, *small_sum = _unpack(summed, [()] + small_shapes)
    gs = dict(zip([k_ for k_, _ in _SMALL], small_sum))
    gs["cw"] = lax.dynamic_slice_in_dim(gs["cw"], chip * (CONV_C // N_CHIPS), CONV_C // N_CHIPS, axis=2)

    grads = dict(ffn1_norm=gs["n1"], mix_norm=gs["nm"], conv_w=gs["cw"], conv_b=gs["cb"], conv_ln_g=gs["lg"],
                 conv_ln_b=gs["lb"], attn_sinks=gs["sinks"], ffn2_norm=gs["n2"], final_norm=gs["nf"])

    weights = dict(ffn1_norm=ffn1_norm, ffn1_w_gate=ffn1_w_gate, ffn1_w_up=ffn1_w_up, ffn1_w_down=ffn1_w_down,
                   mix_norm=mix_norm, w_in=w_in, conv_w=conv_w, conv_b=conv_b, conv_ln_g=conv_ln_g,
                   conv_ln_b=conv_ln_b, attn_sinks=attn_sinks, w_out=w_out, ffn2_norm=ffn2_norm,
                   ffn2_w_gate=ffn2_w_gate, ffn2_w_up=ffn2_w_up, ffn2_w_down=ffn2_w_down, final_norm=final_norm)
    moms = dict(ffn1_norm=(m_ffn1_norm, v_ffn1_norm), ffn1_w_gate=(m_ffn1_w_gate, v_ffn1_w_gate),
                ffn1_w_up=(m_ffn1_w_up, v_ffn1_w_up), ffn1_w_down=(m_ffn1_w_down, v_ffn1_w_down),
                mix_norm=(m_mix_norm, v_mix_norm), w_in=(m_w_in, v_w_in), conv_w=(m_conv_w, v_conv_w),
                conv_b=(m_conv_b, v_conv_b), conv_ln_g=(m_conv_ln_g, v_conv_ln_g),
                conv_ln_b=(m_conv_ln_b, v_conv_ln_b), attn_sinks=(m_attn_sinks, v_attn_sinks),
                w_out=(m_w_out, v_w_out), ffn2_norm=(m_ffn2_norm, v_ffn2_norm),
                ffn2_w_gate=(m_ffn2_w_gate, v_ffn2_w_gate), ffn2_w_up=(m_ffn2_w_up, v_ffn2_w_up),
                ffn2_w_down=(m_ffn2_w_down, v_ffn2_w_down), final_norm=(m_final_norm, v_final_norm))
    names = list(weights)
    big_names = dict(ffn1_w_gate=("f1gu", 0, True), ffn1_w_up=("f1gu", 1, True), ffn1_w_down=("f1d", 0, False),
                     w_in=("mx", 1, True), w_out=("mx", 0, False), ffn2_w_gate=("f2gu", 0, True),
                     ffn2_w_up=("f2gu", 1, True), ffn2_w_down=("f2d", 0, False))
    delta, new_m, new_v = {}, {}, {}
    for nme, (group, idx, transposed) in big_names.items():
        view = tr if transposed else (lambda a: a)
        res = adamw_layers(view(weights[nme]), view(moms[nme][0]), view(moms[nme][1]),
                           [chains[(group, l)].final[idx] for l in range(L)])
        grads[nme], delta[nme], new_m[nme], new_v[nme] = [view(a) for a in res]
    small_names = [nme for nme in names if nme not in big_names]
    s_shapes = [weights[nme].shape for nme in small_names]
    n_tot = sum(math.prod(s) for s in s_shapes)
    srows = -(-n_tot // 1024) * 8
    d, nm_, nv_ = adamw(_pack([weights[nme] for nme in small_names], srows),
                        _pack([grads[nme] for nme in small_names], srows),
                        _pack([moms[nme][0] for nme in small_names], srows),
                        _pack([moms[nme][1] for nme in small_names], srows))
    for nme, dd, mm, vv in zip(small_names, _unpack(d, s_shapes), _unpack(nm_, s_shapes), _unpack(nv_, s_shapes)):
        delta[nme], new_m[nme], new_v[nme] = dd, mm, vv

    return (loss_out, dx[None], *[grads[nme] for nme in names], *[delta[nme] for nme in names],
            *[new_m[nme] for nme in names], *[new_v[nme] for nme in names])
```

```python
import math

import jax
import jax.numpy as jnp
from jax import lax
from jax.experimental import pallas as pl
from jax.experimental.pallas import tpu as pltpu

F32 = jnp.float32
CDT = jnp.bfloat16
D_MODEL = 1024
D_FF = 2816
N_HEADS = 8
HEAD_DIM = 64
BLOCK = 128
CONV_K = 31
CONV_C = 512
ATT_W = 512
D_EXT = 2048
EPS = 1e-5
HALO = 32
FF_CHUNK = 256
NEG = float(jnp.finfo(jnp.float32).min)
VMEM_LIMIT = 56 * 1024 * 1024

ADAM_LR = 0.001
ADAM_B1 = 0.9
ADAM_B2 = 0.999
ADAM_EPS = 1e-08
ADAM_WD = 0.01
ADAM_STEP = 10

NT = (((1,), (1,)), ((), ()))
TN = (((0,), (0,)), ((), ()))


MESH = pl.DeviceIdType.MESH
ANY = pl.BlockSpec(memory_space=pl.ANY)
N_CHIPS = 4
N_DEV = 8


def _params(n_axes):
    return pltpu.CompilerParams(dimension_semantics=("arbitrary",) * n_axes, vmem_limit_bytes=VMEM_LIMIT)


class _Comm:
    def __init__(self, name, inputs, out_shape, sems, descs):
        self.name, self.inputs, self.out_shape, self.sems, self.descs = name, list(inputs), list(out_shape), list(sems), descs


def _merge(*ops):
    ops = [o for o in ops if o is not None]
    if len(ops) == 1:
        return ops[0]

    def descs(cins, couts, sems):
        out, i, o, s = [], 0, 0, 0
        for op in ops:
            ni, no, ns = len(op.inputs), len(op.out_shape), len(op.sems)
            out += op.descs(cins[i:i + ni], couts[o:o + no], sems[s:s + ns])
            i, o, s = i + ni, o + no, s + ns
        return out

    return _Comm("_".join(o.name for o in ops), sum((o.inputs for o in ops), []),
                 sum((o.out_shape for o in ops), []), sum((o.sems for o in ops), []), descs)


def _split(couts, *ops):
    res, o = [], 0
    for op in ops:
        res.append(couts[o:o + len(op.out_shape)])
        o += len(op.out_shape)
    return res


def _hosted(comm):
    if comm is None:
        return pl.pallas_call

    def make(body, *, name, grid, in_specs, out_specs, out_shape, compiler_params, scratch_shapes=()):
        single = not isinstance(out_shape, (list, tuple))
        o_specs = [out_specs] if single else list(out_specs)
        o_shape = [out_shape] if single else list(out_shape)
        n_in, n_out, n_sc = len(in_specs), len(o_specs), len(scratch_shapes)
        c_in, c_out = len(comm.inputs), len(comm.out_shape)

        def hosted(*refs):
            ins, cins = refs[:n_in], refs[n_in:n_in + c_in]
            o0 = n_in + c_in
            outs, couts = refs[o0:o0 + n_out], refs[o0 + n_out:o0 + n_out + c_out]
            s0 = o0 + n_out + c_out
            scr, sems = refs[s0:s0 + n_sc], refs[s0 + n_sc:]
            first = pl.program_id(0) == 0
            last = pl.program_id(0) == grid[0] - 1
            for ax in range(1, len(grid)):
                first = first & (pl.program_id(ax) == 0)
                last = last & (pl.program_id(ax) == grid[ax] - 1)

            @pl.when(first)
            def _():
                for d in comm.descs(cins, couts, sems):
                    d.start()

            body(*ins, *outs, *scr)

            @pl.when(last)
            def _():
                for d in comm.descs(cins, couts, sems):
                    d.wait()

        call = pl.pallas_call(
            hosted, name=f"{name}_{comm.name}", grid=grid,
            in_specs=list(in_specs) + [ANY] * c_in, out_specs=o_specs + [ANY] * c_out,
            out_shape=o_shape + comm.out_shape, scratch_shapes=list(scratch_shapes) + comm.sems,
            compiler_params=compiler_params)

        def run(*args):
            res = call(*args, *comm.inputs)
            return (res[0] if single else list(res[:n_out])), list(res[n_out:])

        return run

    return make


def _run_comm(comm):
    c_in = len(comm.inputs)

    def body(*refs):
        cins, couts, sems = refs[:c_in], refs[c_in:c_in + len(comm.out_shape)], refs[c_in + len(comm.out_shape):]
        ds = comm.descs(cins, couts, sems)
        for d in ds:
            d.start()
        for d in ds:
            d.wait()

    return list(pl.pallas_call(
        body, name=comm.name, in_specs=[ANY] * c_in, out_specs=[ANY] * len(comm.out_shape),
        out_shape=comm.out_shape, scratch_shapes=comm.sems)(*comm.inputs))


def _coords():
    return lax.axis_index("x"), lax.axis_index("y"), lax.axis_index("c")


def _other_chips(x, y):
    return [(1 - x, y), (x, 1 - y), (1 - x, 1 - y)]


def ag_op(shards):
    n = len(shards)

    def descs(cins, couts, sems):
        send, recv = sems
        x, y, c = _coords()
        j = 2 * x + y
        ds = []
        for a in range(n):
            for p, (px, py) in enumerate(_other_chips(x, y)):
                ds.append(pltpu.make_async_remote_copy(cins[a], couts[a].at[j], send.at[a, p], recv.at[a, p],
                                                       device_id=(px, py, c), device_id_type=MESH))
        return ds

    return _Comm("ag", shards, [jax.ShapeDtypeStruct((N_CHIPS,) + s.shape, s.dtype) for s in shards],
                 [pltpu.SemaphoreType.DMA((n, 3)), pltpu.SemaphoreType.DMA((n, 3))], descs)


def _own_slab(gathered, mine, idx):
    return lax.dynamic_update_slice_in_dim(gathered, mine[None], idx, axis=0)


def first_gather(shards):
    n = len(shards)
    halves = [s.reshape(2, s.shape[0] // 2, s.shape[1]) for s in shards]

    def body(*refs):
        ins, outs = refs[:n], refs[n:2 * n]
        send1, recv1, send2, recv2 = refs[2 * n:]
        x, y, c = _coords()
        j = 2 * x + y
        chips = _other_chips(x, y)
        ici = [pltpu.make_async_remote_copy(ins[a].at[c], outs[a].at[j, c], send1.at[a, p], recv1.at[a, p],
                                            device_id=(px, py, c), device_id_type=MESH)
               for a in range(n) for p, (px, py) in enumerate(chips)]
        for d in ici:
            d.start()
        d2d = [pltpu.make_async_remote_copy(outs[a].at[2 * px + py, c], outs[a].at[2 * px + py, c],
                                            send2.at[a, p], recv2.at[a, p],
                                            device_id=(x, y, 1 - c), device_id_type=MESH)
               for a in range(n) for p, (px, py) in enumerate(chips)]
        for d1, d2 in zip(ici, d2d):
            d1.wait()
            d2.start()
        for d in d2d:
            d.wait()

    return list(pl.pallas_call(
        body, name="first_gather", in_specs=[ANY] * n, out_specs=[ANY] * n,
        out_shape=[jax.ShapeDtypeStruct((N_CHIPS,) + h.shape, h.dtype) for h in halves],
        scratch_shapes=[pltpu.SemaphoreType.DMA((n, 3))] * 4)(*halves))


def swap_op(grads):
    n = len(grads)

    def descs(cins, couts, sems):
        send, recv = sems
        x, y, c = _coords()
        return [pltpu.make_async_remote_copy(cins[a].at[:, 1 - c], couts[a], send.at[a], recv.at[a],
                                             device_id=(x, y, 1 - c), device_id_type=MESH) for a in range(n)]

    return _Comm("swap", grads, [jax.ShapeDtypeStruct(g.shape[:1] + g.shape[2:], g.dtype) for g in grads],
                 [pltpu.SemaphoreType.DMA((n,)), pltpu.SemaphoreType.DMA((n,))], descs)


def exchange_op(parts):
    n = len(parts)

    def descs(cins, couts, sems):
        send, recv = sems
        x, y, c = _coords()
        ds = []
        for a in range(n):
            for p, (px, py) in enumerate(_other_chips(x, y)):
                ds.append(pltpu.make_async_remote_copy(cins[a].at[2 * px + py], couts[a].at[p], send.at[a, p],
                                                       recv.at[a, p], device_id=(px, py, c), device_id_type=MESH))
        return ds

    return _Comm("xchg", parts, [jax.ShapeDtypeStruct((3,) + p.shape[1:], p.dtype) for p in parts],
                 [pltpu.SemaphoreType.DMA((n, 3)), pltpu.SemaphoreType.DMA((n, 3))], descs)


def share_op(totals):
    n = len(totals)

    def descs(cins, couts, sems):
        send, recv = sems
        x, y, c = _coords()
        return [pltpu.make_async_remote_copy(cins[a], couts[a].at[c], send.at[a], recv.at[a],
                                             device_id=(x, y, 1 - c), device_id_type=MESH) for a in range(n)]

    return _Comm("share", totals, [jax.ShapeDtypeStruct((2,) + t.shape, t.dtype) for t in totals],
                 [pltpu.SemaphoreType.DMA((n,)), pltpu.SemaphoreType.DMA((n,))], descs)


def _sigmoid(z):
    return 1.0 / (1.0 + jnp.exp(-z))


def _rms_parts(xf):
    r = lax.rsqrt(jnp.mean(xf * xf, axis=-1, keepdims=True) + EPS)
    return xf * r, r


def _rms_bwd(dh, xhat, r, g):
    dg = jnp.sum(dh * xhat, axis=0, keepdims=True)
    dxhat = dh * g
    dx = r * (dxhat - xhat * jnp.mean(dxhat * xhat, axis=-1, keepdims=True))
    return dx, dg


def _chunks(n, ck=FF_CHUNK):
    return [(c0, min(ck, n - c0)) for c0 in range(0, n, ck)]


def ffn_fwd(x, g, wgT, wuT, wd, *, tm=512, comm=None):
    S, D = x.shape
    F = wgT.shape[0]

    def body(x_ref, g_ref, wg_ref, wu_ref, wd_ref, o_ref, gate_ref, up_ref, a_sc):
        xf = x_ref[...]
        xhat, _ = _rms_parts(xf)
        h = (xhat * g_ref[...]).astype(CDT)
        for c0, cw_ in _chunks(F):
            sl = slice(c0, c0 + cw_)
            gt = lax.dot_general(h, wg_ref[sl, :], NT, preferred_element_type=F32)
            ut = lax.dot_general(h, wu_ref[sl, :], NT, preferred_element_type=F32)
            gate_ref[:, sl] = gt.astype(CDT)
            up_ref[:, sl] = ut.astype(CDT)
            a_sc[:, sl] = (gt * _sigmoid(gt) * ut).astype(CDT)
        o_ref[...] = xf + 0.5 * jnp.dot(a_sc[...], wd_ref[...], preferred_element_type=F32)

    wspec = pl.BlockSpec((F, D), lambda i: (0, 0), pipeline_mode=pl.Buffered(1))
    return _hosted(comm)(
        body, name="ffn_fwd",
        grid=(S // tm,),
        in_specs=[pl.BlockSpec((tm, D), lambda i: (i, 0)), pl.BlockSpec((1, D), lambda i: (0, 0)),
                  wspec, wspec, wspec],
        out_specs=[pl.BlockSpec((tm, D), lambda i: (i, 0)),
                   pl.BlockSpec((tm, F), lambda i: (i, 0)),
                   pl.BlockSpec((tm, F), lambda i: (i, 0))],
        out_shape=[jax.ShapeDtypeStruct((S, D), F32),
                   jax.ShapeDtypeStruct((S, F), CDT),
                   jax.ShapeDtypeStruct((S, F), CDT)],
        scratch_shapes=[pltpu.VMEM((tm, F), CDT)],
        compiler_params=_params(1),
    )(x, g, wgT, wuT, wd)


def ffn_bwd_dgrad(x, g, dy, gate, up, wgT, wuT, wd, *, tm=256, comm=None):
    S, D = x.shape
    F = wgT.shape[0]

    def body(x_ref, g_ref, dy_ref, gate_ref, up_ref, wg_ref, wu_ref, wd_ref,
             dx_ref, dgate_ref, dup_ref, h_ref, dg_ref):
        @pl.when(pl.program_id(0) == 0)
        def _():
            dg_ref[...] = jnp.zeros_like(dg_ref)

        dyf = dy_ref[...]
        dacc = (0.5 * dyf).astype(CDT)
        gg = g_ref[...]
        xhat, r = _rms_parts(x_ref[...])
        h_ref[...] = (xhat * gg).astype(CDT)
        for c0, cw_ in _chunks(F):
            sl = slice(c0, c0 + cw_)
            d_a = lax.dot_general(dacc, wd_ref[sl, :], NT, preferred_element_type=F32)
            gt = gate_ref[:, sl].astype(F32)
            ut = up_ref[:, sl].astype(F32)
            sg = _sigmoid(gt)
            dup_ref[:, sl] = (d_a * (gt * sg)).astype(CDT)
            dgate_ref[:, sl] = (d_a * ut * (sg * (1.0 + gt * (1.0 - sg)))).astype(CDT)
        dh = (jnp.dot(dgate_ref[...], wg_ref[...], preferred_element_type=F32)
              + jnp.dot(dup_ref[...], wu_ref[...], preferred_element_type=F32))
        dx, dg = _rms_bwd(dh, xhat, r, gg)
        dx_ref[...] = dyf + dx
        dg_ref[...] += dg

    wspec = pl.BlockSpec((F, D), lambda i: (0, 0), pipeline_mode=pl.Buffered(1))
    row = pl.BlockSpec((tm, D), lambda i: (i, 0))
    act = pl.BlockSpec((tm, F), lambda i: (i, 0))
    vec = pl.BlockSpec((1, D), lambda i: (0, 0))
    return _hosted(comm)(
        body, name="ffn_bwd_dgrad",
        grid=(S // tm,),
        in_specs=[row, vec, row, act, act, wspec, wspec, wspec],
        out_specs=[row, act, act, row, vec],
        out_shape=[jax.ShapeDtypeStruct((S, D), F32),
                   jax.ShapeDtypeStruct((S, F), CDT),
                   jax.ShapeDtypeStruct((S, F), CDT),
                   jax.ShapeDtypeStruct((S, D), CDT),
                   jax.ShapeDtypeStruct((1, D), F32)],
        compiler_params=_params(1),
    )(x, g, dy, gate, up, wgT, wuT, wd)


def ffn_wgrad_down(dy, gate, up, *, tk=1024, comm=None):
    S, D = dy.shape
    F = gate.shape[1]
    tk = min(tk, S)

    def body(dy_ref, gate_ref, up_ref, dwd_ref):
        @pl.when(pl.program_id(0) == 0)
        def _():
            dwd_ref[...] = jnp.zeros_like(dwd_ref)

        dacc = (0.5 * dy_ref[...]).astype(CDT)
        for c0, cw_ in _chunks(F):
            sl = slice(c0, c0 + cw_)
            gt = gate_ref[:, sl].astype(F32)
            a = (gt * _sigmoid(gt) * up_ref[:, sl].astype(F32)).astype(CDT)
            dwd_ref[sl, :] += lax.dot_general(a, dacc, TN, preferred_element_type=F32)

    act = pl.BlockSpec((tk, F), lambda k: (k, 0))
    return _hosted(comm)(
        body, name="ffn_wgrad_down",
        grid=(S // tk,),
        in_specs=[pl.BlockSpec((tk, D), lambda k: (k, 0)), act, act],
        out_specs=pl.BlockSpec((F, D), lambda k: (0, 0), pipeline_mode=pl.Buffered(1)),
        out_shape=jax.ShapeDtypeStruct((F, D), F32),
        compiler_params=_params(1),
    )(dy, gate, up)


def ffn_wgrad_gate_up(h, dgate, dup, *, tk=512, comm=None):
    S, D = h.shape
    F = dgate.shape[1]
    tk = min(tk, S)

    def body(h_ref, dgate_ref, dup_ref, dwg_ref, dwu_ref):
        @pl.when(pl.program_id(0) == 0)
        def _():
            dwg_ref[...] = jnp.zeros_like(dwg_ref)
            dwu_ref[...] = jnp.zeros_like(dwu_ref)

        hh = h_ref[...]
        for c0, cw_ in _chunks(F):
            sl = slice(c0, c0 + cw_)
            dwg_ref[sl, :] += lax.dot_general(dgate_ref[:, sl], hh, TN, preferred_element_type=F32)
            dwu_ref[sl, :] += lax.dot_general(dup_ref[:, sl], hh, TN, preferred_element_type=F32)

    act = pl.BlockSpec((tk, F), lambda k: (k, 0))
    out = pl.BlockSpec((F, D), lambda k: (0, 0), pipeline_mode=pl.Buffered(1))
    return _hosted(comm)(
        body, name="ffn_wgrad_gate_up",
        grid=(S // tk,),
        in_specs=[pl.BlockSpec((tk, D), lambda k: (k, 0)), act, act],
        out_specs=[out, out],
        out_shape=[jax.ShapeDtypeStruct((F, D), F32)] * 2,
        compiler_params=_params(1),
    )(h, dgate, dup)


def loss_head(x, g, target, *, tm=512):
    S, D = x.shape

    def body(x_ref, g_ref, t_ref, loss_ref, dx_ref, dg_ref):
        @pl.when(pl.program_id(0) == 0)
        def _():
            loss_ref[...] = jnp.zeros_like(loss_ref)
            dg_ref[...] = jnp.zeros_like(dg_ref)

        xhat, r = _rms_parts(x_ref[...])
        gg = g_ref[...]
        err = xhat * gg - t_ref[...]
        loss_ref[...] += 0.5 * jnp.sum(jnp.mean(err * err, axis=-1, keepdims=True), axis=0, keepdims=True)
        dx, dg = _rms_bwd(err * (1.0 / D), xhat, r, gg)
        dx_ref[...] = dx
        dg_ref[...] += dg

    row = pl.BlockSpec((tm, D), lambda i: (i, 0))
    vec = pl.BlockSpec((1, D), lambda i: (0, 0))
    return pl.pallas_call(
        body, name="loss_head",
        grid=(S // tm,),
        in_specs=[row, vec, row],
        out_specs=[pl.BlockSpec((1, 1), lambda i: (0, 0)), row, vec],
        out_shape=[jax.ShapeDtypeStruct((1, 1), F32), jax.ShapeDtypeStruct((S, D), F32),
                   jax.ShapeDtypeStruct((1, D), F32)],
        compiler_params=_params(1),
    )(x, g, target)


def _rope_apply(t, cs, sn):
    lane = lax.broadcasted_iota(jnp.int32, t.shape, 1)
    first = (lane % HEAD_DIM) < (HEAD_DIM // 2)
    rot = jnp.where(first, pltpu.roll(t, 128 - HEAD_DIM // 2, 1), pltpu.roll(t, HEAD_DIM // 2, 1))
    return t * cs + rot * sn


def _rope_transpose(d, cs, sn):
    lane = lax.broadcasted_iota(jnp.int32, d.shape, 1)
    first = (lane % HEAD_DIM) < (HEAD_DIM // 2)
    ds = d * sn
    rot = jnp.where(first, pltpu.roll(ds, 128 - HEAD_DIM // 2, 1), pltpu.roll(ds, HEAD_DIM // 2, 1))
    return d * cs + rot


def inproj_fwd(x, g, wextT, cs, sn, *, tm=512, comm=None):
    S, D = x.shape
    scale = HEAD_DIM ** -0.5

    def body(x_ref, g_ref, w_ref, cs_ref, sn_ref, q_ref, k_ref, v_ref, u_ref):
        xhat, _ = _rms_parts(x_ref[...])
        h = (xhat * g_ref[...]).astype(CDT)
        p = lax.dot_general(h, w_ref[...], NT, preferred_element_type=F32)
        c, s = cs_ref[...], sn_ref[...]
        for b in range(4):
            q_ref[:, 128 * b:128 * (b + 1)] = (_rope_apply(p[:, 128 * b:128 * (b + 1)], c, s) * scale).astype(CDT)
        for b in range(2):
            k_ref[:, 128 * b:128 * (b + 1)] = _rope_apply(p[:, 512 + 128 * b:512 + 128 * (b + 1)], c, s).astype(CDT)
        v_ref[...] = p[:, 768:1024].astype(CDT)
        u_ref[...] = p[:, 1024:2048]

    def row(w):
        return pl.BlockSpec((tm, w), lambda i: (i, 0))

    return _hosted(comm)(
        body, name="inproj_fwd",
        grid=(S // tm,),
        in_specs=[row(D), pl.BlockSpec((1, D), lambda i: (0, 0)),
                  pl.BlockSpec((D_EXT, D), lambda i: (0, 0)), row(128), row(128)],
        out_specs=[row(512), row(256), row(256), row(1024)],
        out_shape=[jax.ShapeDtypeStruct((S, 512), CDT), jax.ShapeDtypeStruct((S, 256), CDT),
                   jax.ShapeDtypeStruct((S, 256), CDT), jax.ShapeDtypeStruct((S, 1024), F32)],
        compiler_params=_params(1),
    )(x, g, wextT, cs, sn)


def _stack_heads(p0, p1):
    lane = lax.broadcasted_iota(jnp.int32, p0.shape, 1)
    lo = lane < HEAD_DIM
    z = jnp.zeros_like(p0)
    return jnp.concatenate([jnp.where(lo, p0, z), jnp.where(lo, z, p0),
                            jnp.where(lo, p1, z), jnp.where(lo, z, p1)], axis=0)


def _unstack_heads(o):
    lane = lax.broadcasted_iota(jnp.int32, (BLOCK, 128), 1)
    lo = lane < HEAD_DIM
    return (jnp.where(lo, o[0:128], o[128:256]), jnp.where(lo, o[256:384], o[384:512]))


def _band_mask_kq(n):
    c = lax.broadcasted_iota(jnp.int32, (2 * BLOCK, 4 * BLOCK), 0)
    i = lax.broadcasted_iota(jnp.int32, (2 * BLOCK, 4 * BLOCK), 1) % BLOCK
    return (c > i) & (c <= i + BLOCK) & ((n > 0) | (c >= BLOCK))


def attn_fwd(q, k, v, sink_row, *, nb=4, comm=None):
    S = q.shape[0]
    tq = nb * BLOCK

    def body(q_ref, k_ref, v_ref, sink_ref, o_ref):
        t = pl.program_id(0)
        for b in range(nb):
            n = t * nb + b
            prev = pl.multiple_of(jnp.maximum(n - 1, 0) * BLOCK, BLOCK)
            cur = pl.multiple_of(n * BLOCK, BLOCK)
            rows = slice(b * BLOCK, (b + 1) * BLOCK)
            mask = _band_mask_kq(n)
            for gidx in range(2):
                lanes = slice(128 * gidx, 128 * (gidx + 1))
                qs = _stack_heads(q_ref[rows, 256 * gidx:256 * gidx + 128],
                                  q_ref[rows, 256 * gidx + 128:256 * gidx + 256])
                kb = jnp.concatenate([k_ref[pl.ds(prev, BLOCK), lanes], k_ref[pl.ds(cur, BLOCK), lanes]], axis=0)
                vb = jnp.concatenate([v_ref[pl.ds(prev, BLOCK), lanes], v_ref[pl.ds(cur, BLOCK), lanes]], axis=0)
                st = lax.dot_general(kb, qs, NT, preferred_element_type=F32)
                st = jnp.where(mask, st, NEG)
                sink = sink_ref[gidx]
                m = jnp.maximum(jnp.max(st, axis=0, keepdims=True), sink)
                e = jnp.exp(st - m)
                inv = 1.0 / (jnp.sum(e, axis=0, keepdims=True) + jnp.exp(sink - m))
                o = lax.dot_general((e * inv).astype(CDT), vb, TN, preferred_element_type=F32)
                o0, o1 = _unstack_heads(o)
                o_ref[rows, 256 * gidx:256 * gidx + 128] = o0.astype(CDT)
                o_ref[rows, 256 * gidx + 128:256 * gidx + 256] = o1.astype(CDT)

    return _hosted(comm)(
        body, name="attn_fwd",
        grid=(S // tq,),
        in_specs=[pl.BlockSpec((tq, 512), lambda t: (t, 0)),
                  pl.BlockSpec((S, 256), lambda t: (0, 0)),
                  pl.BlockSpec((S, 256), lambda t: (0, 0)),
                  pl.BlockSpec((2, 1, 4 * BLOCK), lambda t: (0, 0, 0))],
        out_specs=pl.BlockSpec((tq, 512), lambda t: (t, 0)),
        out_shape=jax.ShapeDtypeStruct((S, 512), CDT),
        compiler_params=_params(1),
    )(q, k, v, sink_row)


def attn_bwd(q, k, v, do, sink_row, *, nb=4, comm=None):
    S = q.shape[0]
    tq = nb * BLOCK
    scale = HEAD_DIM ** -0.5

    def body(q_ref, k_ref, v_ref, do_ref, sink_ref, dq_ref, dk_ref, dv_ref, dsink_ref):
        t = pl.program_id(0)

        @pl.when(t == 0)
        def _():
            dk_ref[...] = jnp.zeros_like(dk_ref)
            dv_ref[...] = jnp.zeros_like(dv_ref)
            dsink_ref[...] = jnp.zeros_like(dsink_ref)

        for b in range(nb):
            n = t * nb + b
            prev = pl.multiple_of(jnp.maximum(n - 1, 0) * BLOCK, BLOCK)
            cur = pl.multiple_of(n * BLOCK, BLOCK)
            rows = slice(b * BLOCK, (b + 1) * BLOCK)
            mask = _band_mask_kq(n)
            for gidx in range(2):
                lanes = slice(128 * gidx, 128 * (gidx + 1))
                qs = _stack_heads(q_ref[rows, 256 * gidx:256 * gidx + 128],
                                  q_ref[rows, 256 * gidx + 128:256 * gidx + 256])
                dos = _stack_heads(do_ref[rows, 256 * gidx:256 * gidx + 128],
                                   do_ref[rows, 256 * gidx + 128:256 * gidx + 256])
                kb = jnp.concatenate([k_ref[pl.ds(prev, BLOCK), lanes], k_ref[pl.ds(cur, BLOCK), lanes]], axis=0)
                vb = jnp.concatenate([v_ref[pl.ds(prev, BLOCK), lanes], v_ref[pl.ds(cur, BLOCK), lanes]], axis=0)
                st = lax.dot_general(kb, qs, NT, preferred_element_type=F32)
                st = jnp.where(mask, st, NEG)
                sink = sink_ref[gidx]
                m = jnp.maximum(jnp.max(st, axis=0, keepdims=True), sink)
                e = jnp.exp(st - m)
                es = jnp.exp(sink - m)
                inv = 1.0 / (jnp.sum(e, axis=0, keepdims=True) + es)
                pt = e * inv
                dpt = lax.dot_general(vb, dos, NT, preferred_element_type=F32)
                delta = jnp.sum(pt * dpt, axis=0, keepdims=True)
                dst = (pt * (dpt - delta)).astype(CDT)
                dsink_ref[gidx] += -(es * inv) * delta
                dvb = jnp.dot(pt.astype(CDT), dos, preferred_element_type=F32)
                dkb = jnp.dot(dst, qs, preferred_element_type=F32)
                dqs = lax.dot_general(dst, kb, TN, preferred_element_type=F32) * scale
                dq0, dq1 = _unstack_heads(dqs)
                dq_ref[rows, 256 * gidx:256 * gidx + 128] = dq0
                dq_ref[rows, 256 * gidx + 128:256 * gidx + 256] = dq1
                dk_ref[pl.ds(prev, BLOCK), lanes] += dkb[0:BLOCK]
                dk_ref[pl.ds(cur, BLOCK), lanes] += dkb[BLOCK:2 * BLOCK]
                dv_ref[pl.ds(prev, BLOCK), lanes] += dvb[0:BLOCK]
                dv_ref[pl.ds(cur, BLOCK), lanes] += dvb[BLOCK:2 * BLOCK]

    full = pl.BlockSpec((S, 256), lambda t: (0, 0))
    tile = pl.BlockSpec((tq, 512), lambda t: (t, 0))
    srow = pl.BlockSpec((2, 1, 4 * BLOCK), lambda t: (0, 0, 0))
    return _hosted(comm)(
        body, name="attn_bwd",
        grid=(S // tq,),
        in_specs=[tile, full, full, tile, srow],
        out_specs=[tile, full, full, srow],
        out_shape=[jax.ShapeDtypeStruct((S, 512), F32), jax.ShapeDtypeStruct((S, 256), F32),
                   jax.ShapeDtypeStruct((S, 256), F32), jax.ShapeDtypeStruct((2, 1, 4 * BLOCK), F32)],
        compiler_params=_params(1),
    )(q, k, v, do, sink_row)


def _glu(u):
    a = u[:, 0:CONV_C]
    gt = u[:, CONV_C:2 * CONV_C]
    sg = _sigmoid(gt)
    return a, sg, a * sg


CONV_CHUNK = 32


def _shifted_copies(buf, shifted, n):
    for r in range(1, 8):
        shifted[r - 1, 0:n, :] = buf[r:r + n, :]


def _shifted_rows(buf, shifted, start, rows):
    r = start % 8
    if r == 0:
        return buf[start:start + rows, :]
    return shifted[r - 1, start - r:start - r + rows, :]


def conv_fwd(u, cw, cb, lg, lb, *, tm=512, comm=None):
    S = u.shape[0]
    nh = tm // HALO

    def body(u_ref, uh_ref, cw_ref, cb_ref, lg_ref, lb_ref, o_ref, y_ref, hbuf, hsh):
        t = pl.program_id(0)
        _, _, hg = _glu(u_ref[...])
        _, _, hh = _glu(uh_ref[...])
        hbuf[0:HALO, :] = jnp.where(t > 0, hh, jnp.zeros_like(hh))
        hbuf[HALO:HALO + tm, :] = hg
        hbuf[HALO + tm:HALO + tm + 8, :] = jnp.zeros((8, CONV_C), F32)
        _shifted_copies(hbuf, hsh, HALO + tm)
        off = HALO - (CONV_K - 1)
        for c0 in range(0, tm, CONV_CHUNK):
            acc = jnp.zeros((CONV_CHUNK, CONV_C), F32) + cb_ref[...]
            for j in range(CONV_K):
                acc = acc + cw_ref[j:j + 1, :] * _shifted_rows(hbuf, hsh, c0 + off + j, CONV_CHUNK)
            y_ref[c0:c0 + CONV_CHUNK, :] = acc
        y = y_ref[...]
        yc = y - jnp.mean(y, axis=-1, keepdims=True)
        r = lax.rsqrt(jnp.mean(yc * yc, axis=-1, keepdims=True) + EPS)
        z = yc * r * lg_ref[...] + lb_ref[...]
        o_ref[...] = (z * _sigmoid(z)).astype(CDT)

    vec = pl.BlockSpec((1, CONV_C), lambda t: (0, 0))
    return _hosted(comm)(
        body, name="conv_fwd",
        grid=(S // tm,),
        in_specs=[pl.BlockSpec((tm, 2 * CONV_C), lambda t: (t, 0)),
                  pl.BlockSpec((HALO, 2 * CONV_C), lambda t: (jnp.maximum(t * nh - 1, 0), 0)),
                  pl.BlockSpec((CONV_K, CONV_C), lambda t: (0, 0)), vec, vec, vec],
        out_specs=[pl.BlockSpec((tm, CONV_C), lambda t: (t, 0)), pl.BlockSpec((tm, CONV_C), lambda t: (t, 0))],
        out_shape=[jax.ShapeDtypeStruct((S, CONV_C), CDT), jax.ShapeDtypeStruct((S, CONV_C), F32)],
        scratch_shapes=[pltpu.VMEM((HALO + tm + 8, CONV_C), F32), pltpu.VMEM((7, HALO + tm, CONV_C), F32)],
        compiler_params=_params(1),
    )(u, u, cw, cb, lg, lb)


def conv_bwd(dc, u, y, cw, lg, lb, *, tm=512, comm=None):
    S = u.shape[0]
    nh = tm // HALO
    nt = S // tm

    def ln_bwd(dcv, yv, lgv, lbv):
        yc = yv - jnp.mean(yv, axis=-1, keepdims=True)
        r = lax.rsqrt(jnp.mean(yc * yc, axis=-1, keepdims=True) + EPS)
        yhat = yc * r
        z = yhat * lgv + lbv
        sg = _sigmoid(z)
        dz = dcv * (sg * (1.0 + z * (1.0 - sg)))
        dyhat = dz * lgv
        dy = r * (dyhat - jnp.mean(dyhat, axis=-1, keepdims=True)
                  - yhat * jnp.mean(dyhat * yhat, axis=-1, keepdims=True))
        return dy, dz, yhat

    def body(dc_ref, dcn_ref, u_ref, uh_ref, y_ref, yn_ref, cw_ref, lg_ref, lb_ref,
             du_ref, dcw_ref, dcb_ref, dlg_ref, dlb_ref, hbuf, dybuf, dhg_sc, dw_sc, hsh, dysh):
        t = pl.program_id(0)

        @pl.when(t == 0)
        def _():
            dw_sc[...] = jnp.zeros_like(dw_sc)
            dcb_ref[...] = jnp.zeros_like(dcb_ref)
            dlg_ref[...] = jnp.zeros_like(dlg_ref)
            dlb_ref[...] = jnp.zeros_like(dlb_ref)

        lgv, lbv = lg_ref[...], lb_ref[...]
        dy, dz, yhat = ln_bwd(dc_ref[...].astype(F32), y_ref[...], lgv, lbv)
        dyn, _, _ = ln_bwd(dcn_ref[...].astype(F32), yn_ref[...], lgv, lbv)
        dlb_ref[...] += jnp.sum(dz, axis=0, keepdims=True)
        dlg_ref[...] += jnp.sum(dz * yhat, axis=0, keepdims=True)
        dcb_ref[...] += jnp.sum(dy, axis=0, keepdims=True)
        dybuf[0:tm, :] = dy
        dybuf[tm:tm + HALO, :] = jnp.where(t < nt - 1, dyn, jnp.zeros_like(dyn))
        dybuf[tm + HALO:tm + HALO + 8, :] = jnp.zeros((8, CONV_C), F32)
        _shifted_copies(dybuf, dysh, tm + HALO)

        a, sg, hg = _glu(u_ref[...])
        _, _, hh = _glu(uh_ref[...])
        hbuf[0:HALO, :] = jnp.where(t > 0, hh, jnp.zeros_like(hh))
        hbuf[HALO:HALO + tm, :] = hg
        hbuf[HALO + tm:HALO + tm + 8, :] = jnp.zeros((8, CONV_C), F32)
        _shifted_copies(hbuf, hsh, HALO + tm)

        off = HALO - (CONV_K - 1)
        for c0 in range(0, tm, CONV_CHUNK):
            acc = jnp.zeros((CONV_CHUNK, CONV_C), F32)
            dyc = dybuf[c0:c0 + CONV_CHUNK, :]
            for j in range(CONV_K):
                acc = acc + cw_ref[j:j + 1, :] * _shifted_rows(dybuf, dysh, c0 + (CONV_K - 1) - j, CONV_CHUNK)
                prod = dyc * _shifted_rows(hbuf, hsh, c0 + off + j, CONV_CHUNK)
                dw_sc[j] += jnp.sum(prod.reshape(CONV_CHUNK // 8, 8, CONV_C), axis=0)
            dhg_sc[c0:c0 + CONV_CHUNK, :] = acc

        dhg = dhg_sc[...]
        du_ref[:, 0:CONV_C] = dhg * sg
        du_ref[:, CONV_C:2 * CONV_C] = dhg * a * sg * (1.0 - sg)

        @pl.when(t == nt - 1)
        def _():
            dcw_ref[...] = jnp.sum(dw_sc[...], axis=1)

    vec = pl.BlockSpec((1, CONV_C), lambda t: (0, 0))
    tile = pl.BlockSpec((tm, CONV_C), lambda t: (t, 0))
    nxt = pl.BlockSpec((HALO, CONV_C), lambda t: (jnp.minimum((t + 1) * nh, S // HALO - 1), 0))
    return _hosted(comm)(
        body, name="conv_bwd",
        grid=(nt,),
        in_specs=[tile, nxt,
                  pl.BlockSpec((tm, 2 * CONV_C), lambda t: (t, 0)),
                  pl.BlockSpec((HALO, 2 * CONV_C), lambda t: (jnp.maximum(t * nh - 1, 0), 0)),
                  tile, nxt,
                  pl.BlockSpec((CONV_K, CONV_C), lambda t: (0, 0)), vec, vec],
        out_specs=[pl.BlockSpec((tm, 2 * CONV_C), lambda t: (t, 0)),
                   pl.BlockSpec((CONV_K, CONV_C), lambda t: (0, 0)), vec, vec, vec],
        out_shape=[jax.ShapeDtypeStruct((S, 2 * CONV_C), F32), jax.ShapeDtypeStruct((CONV_K, CONV_C), F32),
                   jax.ShapeDtypeStruct((1, CONV_C), F32), jax.ShapeDtypeStruct((1, CONV_C), F32),
                   jax.ShapeDtypeStruct((1, CONV_C), F32)],
        scratch_shapes=[pltpu.VMEM((HALO + tm + 8, CONV_C), F32), pltpu.VMEM((tm + HALO + 8, CONV_C), F32),
                        pltpu.VMEM((tm, CONV_C), F32), pltpu.VMEM((CONV_K, 8, CONV_C), F32),
                        pltpu.VMEM((7, HALO + tm, CONV_C), F32), pltpu.VMEM((7, tm + HALO, CONV_C), F32)],
        compiler_params=_params(1),
    )(dc, dc, u, u, y, y, cw, lg, lb)


def outproj_fwd(x, ao, co, wout, *, tm=512):
    S, D = x.shape

    def body(x_ref, a_ref, c_ref, w_ref, o_ref):
        o_ref[...] = (x_ref[...]
                      + jnp.dot(a_ref[...], w_ref[0:ATT_W, :], preferred_element_type=F32)
                      + jnp.dot(c_ref[...], w_ref[ATT_W:ATT_W + CONV_C, :], preferred_element_type=F32))

    return pl.pallas_call(
        body, name="outproj_fwd",
        grid=(S // tm,),
        in_specs=[pl.BlockSpec((tm, D), lambda i: (i, 0)), pl.BlockSpec((tm, ATT_W), lambda i: (i, 0)),
                  pl.BlockSpec((tm, CONV_C), lambda i: (i, 0)), pl.BlockSpec((D, D), lambda i: (0, 0))],
        out_specs=pl.BlockSpec((tm, D), lambda i: (i, 0)),
        out_shape=jax.ShapeDtypeStruct((S, D), F32),
        compiler_params=_params(1),
    )(x, ao, co, wout)


def outproj_bwd(dx, ao, co, wout, *, tm=512, comm=None):
    S, D = dx.shape

    def body(dx_ref, a_ref, c_ref, w_ref, da_ref, dc_ref, dw_ref):
        @pl.when(pl.program_id(0) == 0)
        def _():
            dw_ref[...] = jnp.zeros_like(dw_ref)

        dxb = dx_ref[...].astype(CDT)
        da_ref[...] = lax.dot_general(dxb, w_ref[0:ATT_W, :], NT, preferred_element_type=F32).astype(CDT)
        dc_ref[...] = lax.dot_general(dxb, w_ref[ATT_W:ATT_W + CONV_C, :], NT, preferred_element_type=F32)
        dw_ref[0:ATT_W, :] += lax.dot_general(a_ref[...], dxb, TN, preferred_element_type=F32)
        dw_ref[ATT_W:ATT_W + CONV_C, :] += lax.dot_general(c_ref[...], dxb, TN, preferred_element_type=F32)

    return _hosted(comm)(
        body, name="outproj_bwd",
        grid=(S // tm,),
        in_specs=[pl.BlockSpec((tm, D), lambda i: (i, 0)), pl.BlockSpec((tm, ATT_W), lambda i: (i, 0)),
                  pl.BlockSpec((tm, CONV_C), lambda i: (i, 0)), pl.BlockSpec((D, D), lambda i: (0, 0))],
        out_specs=[pl.BlockSpec((tm, ATT_W), lambda i: (i, 0)), pl.BlockSpec((tm, CONV_C), lambda i: (i, 0)),
                   pl.BlockSpec((D, D), lambda i: (0, 0))],
        out_shape=[jax.ShapeDtypeStruct((S, ATT_W), CDT), jax.ShapeDtypeStruct((S, CONV_C), F32),
                   jax.ShapeDtypeStruct((D, D), F32)],
        compiler_params=_params(1),
    )(dx, ao, co, wout)


def inproj_bwd(x, g, dres, dq, dk, dv, du, wextT, cs, sn, *, tm=512):
    S, D = x.shape

    def body(x_ref, g_ref, dres_ref, dq_ref, dk_ref, dv_ref, du_ref, w_ref, cs_ref, sn_ref,
             dx_ref, dw_ref, dg_ref, dp_sc):
        @pl.when(pl.program_id(0) == 0)
        def _():
            dw_ref[...] = jnp.zeros_like(dw_ref)
            dg_ref[...] = jnp.zeros_like(dg_ref)

        c, s = cs_ref[...], sn_ref[...]
        for b in range(4):
            dp_sc[:, 128 * b:128 * (b + 1)] = _rope_transpose(dq_ref[:, 128 * b:128 * (b + 1)], c, s).astype(CDT)
        for b in range(2):
            dp_sc[:, 512 + 128 * b:512 + 128 * (b + 1)] = _rope_transpose(
                dk_ref[:, 128 * b:128 * (b + 1)], c, s).astype(CDT)
        dp_sc[:, 768:1024] = dv_ref[...].astype(CDT)
        dp_sc[:, 1024:2048] = du_ref[...].astype(CDT)
        dp = dp_sc[...]
        xhat, r = _rms_parts(x_ref[...])
        gg = g_ref[...]
        h = (xhat * gg).astype(CDT)
        dh = jnp.dot(dp, w_ref[...], preferred_element_type=F32)
        dw_ref[...] += lax.dot_general(dp, h, TN, preferred_element_type=F32)
        dx, dg = _rms_bwd(dh, xhat, r, gg)
        dx_ref[...] = dres_ref[...] + dx
        dg_ref[...] += dg

    def row(w):
        return pl.BlockSpec((tm, w), lambda i: (i, 0))

    return pl.pallas_call(
        body, name="inproj_bwd",
        grid=(S // tm,),
        in_specs=[row(D), pl.BlockSpec((1, D), lambda i: (0, 0)), row(D), row(512), row(256), row(256),
                  row(1024), pl.BlockSpec((D_EXT, D), lambda i: (0, 0), pipeline_mode=pl.Buffered(1)),
                  row(128), row(128)],
        out_specs=[row(D), pl.BlockSpec((D_EXT, D), lambda i: (0, 0), pipeline_mode=pl.Buffered(1)),
                   pl.BlockSpec((1, D), lambda i: (0, 0))],
        out_shape=[jax.ShapeDtypeStruct((S, D), F32), jax.ShapeDtypeStruct((D_EXT, D), F32),
                   jax.ShapeDtypeStruct((1, D), F32)],
        scratch_shapes=[pltpu.VMEM((tm, D_EXT), CDT)],
        compiler_params=_params(1),
    )(x, g, dres, dq, dk, dv, du, wextT, cs, sn)


def _rope_tables(positions):
    inv_freq = 1.0 / (10000.0 ** (jnp.arange(0, HEAD_DIM, 2, dtype=F32) / HEAD_DIM))
    ang = positions.astype(F32).reshape(-1, 1) * inv_freq
    cos, sin = jnp.cos(ang), jnp.sin(ang)
    cs = jnp.tile(jnp.concatenate([cos, cos], axis=-1), (1, 2))
    sn = jnp.tile(jnp.concatenate([-sin, sin], axis=-1), (1, 2))
    return cs, sn


def _widen_w_in(w):
    q, u = w[0:512], w[768:1792]
    parts = [q]
    for base in (512, 576, 640, 704):
        parts += [w[base:base + 64], w[base:base + 64]]
    return jnp.concatenate(parts + [u], axis=0)


def _fold_w_in(d):
    parts = [d[0:512]]
    for base in (512, 640, 768, 896):
        parts.append(d[base:base + 64] + d[base + 64:base + 128])
    return jnp.concatenate(parts + [d[1024:2048]], axis=0)


def add_half(g5, r1, c_idx):
    _, _, r, D = g5.shape

    def body(c_ref, g_ref, r_ref, o_ref):
        o_ref[...] = (g_ref[...] + r_ref[...]).astype(CDT)

    return pl.pallas_call(
        body, name="add_half",
        grid_spec=pltpu.PrefetchScalarGridSpec(
            num_scalar_prefetch=1, grid=(N_CHIPS,),
            in_specs=[pl.BlockSpec((None, None, r, D), lambda s, cr: (s, cr[0], 0, 0)),
                      pl.BlockSpec((None, r, D), lambda s, cr: (s, 0, 0))],
            out_specs=pl.BlockSpec((None, r, D), lambda s, cr: (s, 0, 0))),
        out_shape=jax.ShapeDtypeStruct((N_CHIPS, r, D), CDT),
        compiler_params=_params(1),
    )(c_idx, g5, r1)


def sum_partials(part, recv3, j_idx):
    _, r, D = part.shape
    tr_ = r // 2

    def body(j_ref, p_ref, r_ref, o_ref):
        o_ref[...] = ((p_ref[...].astype(F32) + r_ref[0].astype(F32)) + r_ref[1].astype(F32)) + r_ref[2].astype(F32)

    return pl.pallas_call(
        body, name="sum_partials",
        grid_spec=pltpu.PrefetchScalarGridSpec(
            num_scalar_prefetch=1, grid=(2,),
            in_specs=[pl.BlockSpec((None, tr_, D), lambda i, jr: (jr[0], i, 0)),
                      pl.BlockSpec((3, tr_, D), lambda i, jr: (0, i, 0))],
            out_specs=pl.BlockSpec((tr_, D), lambda i, jr: (i, 0))),
        out_shape=jax.ShapeDtypeStruct((r, D), F32),
        compiler_params=_params(1),
    )(j_idx, part, recv3)


class _Chain:
    STAGES = ("swap", "xchg", "share")

    def __init__(self, grads, c_idx, j_idx):
        self.c_idx, self.j_idx = c_idx, j_idx
        self.g5 = [g.reshape(N_CHIPS, 2, g.shape[0] // (2 * N_CHIPS), g.shape[1]) for g in grads]
        self.stage_no = 0

    @property
    def done(self):
        return self.stage_no == len(self.STAGES)

    def next_stage(self):
        name = self.STAGES[self.stage_no]

        def callback(res):
            getattr(self, "after_" + name)(res)
            self.stage_no += 1

        return getattr(self, name)(), callback

    def swap(self):
        return swap_op(self.g5)

    def after_swap(self, recv):
        self.parts = [add_half(g, r, self.c_idx) for g, r in zip(self.g5, recv)]

    def xchg(self):
        return exchange_op(self.parts)

    def after_xchg(self, recv):
        self.totals = [sum_partials(p, r, self.j_idx) for p, r in zip(self.parts, recv)]

    def share(self):
        return share_op(self.totals)

    def after_share(self, recv):
        both = [_own_slab(h, t, self.c_idx[0]) for h, t in zip(recv, self.totals)]
        self.final = [h.reshape(2 * h.shape[1], h.shape[2]) for h in both]


def all_reduce_small(vec):
    R = vec.shape[0]

    def body(v_ref, o_ref, buf, send, recv):
        x, y, c = _coords()
        me = 4 * x + 2 * y + c
        buf[me] = v_ref[...]
        cps = []
        for m in range(1, N_DEV):
            dx, dy, dc = (m >> 2) & 1, (m >> 1) & 1, m & 1
            cp = pltpu.make_async_remote_copy(v_ref, buf.at[me], send.at[m - 1], recv.at[m - 1],
                                              device_id=((x + dx) % 2, (y + dy) % 2, (c + dc) % 2),
                                              device_id_type=MESH)
            cp.start()
            cps.append(cp)
        for cp in cps:
            cp.wait()
        acc = buf[0]
        for d in range(1, N_DEV):
            acc = acc + buf[d]
        o_ref[...] = acc

    return pl.pallas_call(
        body, name="all_reduce_small",
        in_specs=[pl.BlockSpec(memory_space=pltpu.VMEM)], out_specs=pl.BlockSpec(memory_space=pltpu.VMEM),
        out_shape=jax.ShapeDtypeStruct(vec.shape, F32),
        scratch_shapes=[pltpu.VMEM((N_DEV, R, 128), F32), pltpu.SemaphoreType.DMA((N_DEV - 1,)),
                        pltpu.SemaphoreType.DMA((N_DEV - 1,))],
    )(vec)


def adamw(w, g, m, v, *, tm=512):
    R, C = w.shape
    tm = max(t for t in range(8, min(tm, R) + 1, 8) if R % t == 0)
    c1 =1.0 - ADAM_B1 ** ADAM_STEP
    c2 = 1.0 - ADAM_B2 ** ADAM_STEP

    def body(w_ref, g_ref, m_ref, v_ref, d_ref, nm_ref, nv_ref):
        gg = g_ref[...]
        nm = ADAM_B1 * m_ref[...] + (1.0 - ADAM_B1) * gg
        nv = ADAM_B2 * v_ref[...] + (1.0 - ADAM_B2) * (gg * gg)
        nm_ref[...] = nm
        nv_ref[...] = nv
        d_ref[...] = -ADAM_LR * ((nm / c1) / (jnp.sqrt(nv / c2) + ADAM_EPS) + ADAM_WD * w_ref[...])

    blk = pl.BlockSpec((tm, C), lambda i: (i, 0))
    return pl.pallas_call(
        body, name="adamw",
        grid=(pl.cdiv(R, tm),),
        in_specs=[blk] * 4, out_specs=[blk] * 3,
        out_shape=[jax.ShapeDtypeStruct((R, C), F32)] * 3,
        compiler_params=_params(1),
    )(w, g, m, v)


def adamw_layers(w, m, v, g_layers, *, tm=352):
    L, R, C = w.shape
    tm = max(t for t in range(8, min(tm, R) + 1, 8) if R % t == 0)
    c1 = 1.0 - ADAM_B1 ** ADAM_STEP
    c2 = 1.0 - ADAM_B2 ** ADAM_STEP

    def body(w_ref, m_ref, v_ref, *rest):
        g_refs, (go_ref, d_ref, nm_ref, nv_ref) = rest[:L], rest[L:]
        layer = pl.program_id(0)
        gg = g_refs[0][...]
        for l in range(1, L):
            gg = jnp.where(layer == l, g_refs[l][...], gg)
        nm = ADAM_B1 * m_ref[...] + (1.0 - ADAM_B1) * gg
        nv = ADAM_B2 * v_ref[...] + (1.0 - ADAM_B2) * (gg * gg)
        go_ref[...] = gg
        nm_ref[...] = nm
        nv_ref[...] = nv
        d_ref[...] = -ADAM_LR * ((nm / c1) / (jnp.sqrt(nv / c2) + ADAM_EPS) + ADAM_WD * w_ref[...])

    blk = pl.BlockSpec((None, tm, C), lambda l, i: (l, i, 0))
    gblk = pl.BlockSpec((tm, C), lambda l, i: (i, 0))
    return pl.pallas_call(
        body, name="adamw_layers",
        grid=(L, R // tm),
        in_specs=[blk] * 3 + [gblk] * L, out_specs=[blk] * 4,
        out_shape=[jax.ShapeDtypeStruct((L, R, C), F32)] * 4,
        compiler_params=_params(2),
    )(w, m, v, *g_layers)


_SMALL = (("n1", (2, D_MODEL)), ("nm", (2, D_MODEL)), ("n2", (2, D_MODEL)), ("nf", (D_MODEL,)),
          ("cb", (2, CONV_C)), ("lg", (2, CONV_C)), ("lb", (2, CONV_C)), ("sinks", (2, N_HEADS)),
          ("cw", (2, CONV_K, CONV_C)))


def _pack(parts, rows):
    flat = jnp.concatenate([p.reshape(-1).astype(F32) for p in parts])
    return jnp.pad(flat, (0, rows * 128 - flat.shape[0])).reshape(rows, 128)


def _unpack(block, shapes):
    flat = block.reshape(-1)
    out, o = [], 0
    for shp in shapes:
        n = 1
        for s in shp:
            n *= s
        out.append(flat[o:o + n].reshape(shp))
        o += n
    return out


def kernel(x, positions, ffn1_norm, ffn1_w_gate, ffn1_w_up, ffn1_w_down, mix_norm, w_in, conv_w, conv_b, conv_ln_g, conv_ln_b, attn_sinks, w_out, ffn2_norm, ffn2_w_gate, ffn2_w_up, ffn2_w_down, final_norm, loss_target, m_ffn1_norm, m_ffn1_w_gate, m_ffn1_w_up, m_ffn1_w_down, m_mix_norm, m_w_in, m_conv_w, m_conv_b, m_conv_ln_g, m_conv_ln_b, m_attn_sinks, m_w_out, m_ffn2_norm, m_ffn2_w_gate, m_ffn2_w_up, m_ffn2_w_down, m_final_norm, v_ffn1_norm, v_ffn1_w_gate, v_ffn1_w_up, v_ffn1_w_down, v_mix_norm, v_w_in, v_conv_w, v_conv_b, v_conv_ln_g, v_conv_ln_b, v_attn_sinks, v_w_out, v_ffn2_norm, v_ffn2_w_gate, v_ffn2_w_up, v_ffn2_w_down, v_final_norm):
    cx, cy, cc = _coords()
    chip = 2 * cx + cy
    c_idx = jnp.reshape(cc, (1,)).astype(jnp.int32)
    j_idx = jnp.reshape(chip, (1,)).astype(jnp.int32)
    L = ffn1_norm.shape[0]
    tr = lambda a: jnp.swapaxes(a, 1, 2)

    sh = dict(f1g=tr(ffn1_w_gate), f1u=tr(ffn1_w_up), f1d=ffn1_w_down, f2g=tr(ffn2_w_gate),
              f2u=tr(ffn2_w_up), f2d=ffn2_w_down, win=tr(w_in), wout=w_out)
    sh = {k: [v[l].astype(CDT) for l in range(L)] for k, v in sh.items()}
    W = {}

    def gather_op(keys):
        return ag_op([conv_w if k == "cw" else sh[k[0]][k[1]] for k in keys])

    def take(keys, res):
        for k, a in zip(keys, res):
            if k == "cw":
                W[k] = _own_slab(a, conv_w, chip)
            else:
                W[k] = _own_slab(a, sh[k[0]][k[1]], chip).reshape(N_CHIPS * a.shape[1], a.shape[2])

    def with_ag(fn, keys, *args):
        if not keys:
            return fn(*args)
        main, res = fn(*args, comm=gather_op(keys))
        take(keys, res)
        return main

    ag_hosts = {("ffn1", 0): [("win", 0), "cw", ("f2g", 0), ("f1d", 1)],
                ("inproj", 0): [("wout", 0)], ("attn", 0): [("f2u", 0)], ("conv", 0): [("f2d", 0)],
                ("ffn2", 0): [("f1g", 1), ("f1u", 1)],
                ("ffn1", 1): [("win", 1), ("f2g", 1), ("f2u", 1)],
                ("inproj", 1): [("wout", 1)], ("attn", 1): [("f2d", 1)]}
    for k_, a in zip(("f1g", "f1u", "f1d"), first_gather([sh[k_][0] for k_ in ("f1g", "f1u", "f1d")])):
        own = sh[k_][0].reshape(a.shape[1:])
        W[(k_, 0)] = _own_slab(a, own, chip).reshape(N_CHIPS * sh[k_][0].shape[0], sh[k_][0].shape[1])

    cs, sn = _rope_tables(positions)
    saved = []
    h = x[0]
    for l in range(L):
        sink = attn_sinks[l].reshape(2, 4)
        sink_row = jnp.repeat(sink, BLOCK, axis=1).reshape(2, 1, 4 * BLOCK)
        x0 = h
        x1, g1, u1 = with_ag(ffn_fwd, ag_hosts.get(("ffn1", l)), x0, ffn1_norm[l][None],
                             W[("f1g", l)], W[("f1u", l)], W[("f1d", l)])
        wext = _widen_w_in(W[("win", l)])
        q, k, v, u = with_ag(inproj_fwd, ag_hosts.get(("inproj", l)), x1, mix_norm[l][None], wext, cs, sn)
        ao = with_ag(attn_fwd, ag_hosts.get(("attn", l)), q, k, v, sink_row)
        cwl = jnp.transpose(W["cw"][:, l], (1, 0, 2)).reshape(CONV_K, CONV_C)
        co, yc = with_ag(conv_fwd, ag_hosts.get(("conv", l)), u, cwl, conv_b[l][None], conv_ln_g[l][None],
                         conv_ln_b[l][None])
        x2 = outproj_fwd(x1, ao, co, W[("wout", l)])
        x3, g2, u2 = with_ag(ffn_fwd, ag_hosts.get(("ffn2", l)), x2, ffn2_norm[l][None],
                             W[("f2g", l)], W[("f2u", l)], W[("f2d", l)])
        saved.append((x0, x1, x2, g1, u1, g2, u2, q, k, v, u, ao, co, yc, sink, wext, cwl))
        h = x3

    loss, dx, dnf = loss_head(h, final_norm[None], loss_target[0])

    active = []

    def advance(run):
        stages = [ch.next_stage() for ch in active]
        ops = [op for op, _ in stages]
        main, res = run(_merge(*ops) if ops else None)
        for (_, cb), r in zip(stages, _split(res, *ops)):
            cb(r)
        active[:] = [ch for ch in active if not ch.done]
        return main

    def hosted(fn, *args):
        def run(comm):
            if comm is None:
                return fn(*args), []
            return fn(*args, comm=comm)
        return advance(run)

    def chain(key, grads):
        chains[key] = _Chain(grads, c_idx, j_idx)
        active.append(chains[key])

    small = {k_: [None] * L for k_ in ("n1", "nm", "n2", "cw", "cb", "lg", "lb", "sinks")}
    chains = {}
    for l in reversed(range(L)):
        x0, x1, x2, g1, u1, g2, u2, q, k, v, u, ao, co, yc, sink, wext, cwl = saved[l]
        sink_row = jnp.repeat(sink, BLOCK, axis=1).reshape(2, 1, 4 * BLOCK)
        chain(("f2d", l), [hosted(ffn_wgrad_down, dx, g2, u2)])
        dx2, dgt, dup, hh, small["n2"][l] = hosted(
            ffn_bwd_dgrad, x2, ffn2_norm[l][None], dx, g2, u2, W[("f2g", l)], W[("f2u", l)], W[("f2d", l)])
        chain(("f2gu", l), hosted(ffn_wgrad_gate_up, hh, dgt, dup))
        da, dc, gwout = hosted(outproj_bwd, dx2, ao, co, W[("wout", l)])
        du, small["cw"][l], small["cb"][l], small["lg"][l], small["lb"][l] = hosted(
            conv_bwd, dc, u, yc, cwl, conv_ln_g[l][None], conv_ln_b[l][None])
        dq, dk, dv, dsink = hosted(attn_bwd, q, k, v, da, sink_row)
        small["sinks"][l] = jnp.sum(dsink.reshape(2, 4, BLOCK), axis=-1).reshape(N_HEADS)
        dx1, gwext, small["nm"][l] = inproj_bwd(x1, mix_norm[l][None], dx2, dq, dk, dv, du, wext, cs, sn)
        chain(("mx", l), [gwout, _fold_w_in(gwext)])
        chain(("f1d", l), [hosted(ffn_wgrad_down, dx1, g1, u1)])
        dx, dgt, dup, hh, small["n1"][l] = hosted(
            ffn_bwd_dgrad, x0, ffn1_norm[l][None], dx1, g1, u1, W[("f1g", l)], W[("f1u", l)], W[("f1d", l)])
        chain(("f1gu", l), hosted(ffn_wgrad_gate_up, hh, dgt, dup))

    while active:
        advance(lambda comm: (None, _run_comm(comm)))

    G = {k_: jnp.stack(v_) for k_, v_ in small.items()}
    G["nf"] = dnf
    small_shapes = [shp for _, shp in _SMALL]
    n_small = 1 + sum(math.prod(s) for s in small_shapes)
    rows = -(-n_small // 1024) * 8
    packed = _pack([loss] + [G[k_] for k_, _ in _SMALL], rows)
    summed = all_reduce_small(packed)
    loss_out, *small_sum = _unpack(summed, [()] + small_shapes)
    gs = dict(zip([k_ for k_, _ in _SMALL], small_sum))
    gs["cw"] = lax.dynamic_slice_in_dim(gs["cw"], chip * (CONV_C // N_CHIPS), CONV_C // N_CHIPS, axis=2)

    grads = dict(ffn1_norm=gs["n1"], mix_norm=gs["nm"], conv_w=gs["cw"], conv_b=gs["cb"], conv_ln_g=gs["lg"],
                 conv_ln_b=gs["lb"], attn_sinks=gs["sinks"], ffn2_norm=gs["n2"], final_norm=gs["nf"])

    weights = dict(ffn1_norm=ffn1_norm, ffn1_w_gate=ffn1_w_gate, ffn1_w_up=ffn1_w_up, ffn1_w_down=ffn1_w_down,
                   mix_norm=mix_norm, w_in=w_in, conv_w=conv_w, conv_b=conv_b, conv_ln_g=conv_ln_g,
                   conv_ln_b=conv_ln_b, attn_sinks=attn_sinks, w_out=w_out, ffn2_norm=ffn2_norm,
                   ffn2_w_gate=ffn2_w_gate, ffn2_w_up=ffn2_w_up, ffn2_w_down=ffn2_w_down, final_norm=final_norm)
    moms = dict(ffn1_norm=(m_ffn1_norm, v_ffn1_norm), ffn1_w_gate=(m_ffn1_w_gate, v_ffn1_w_gate),
                ffn1_w_up=(m_ffn1_w_up, v_ffn1_w_up), ffn1_w_down=(m_ffn1_w_down, v_ffn1_w_down),
                mix_norm=(m_mix_norm, v_mix_norm), w_in=(m_w_in, v_w_in), conv_w=(m_conv_w, v_conv_w),
                conv_b=(m_conv_b, v_conv_b), conv_ln_g=(m_conv_ln_g, v_conv_ln_g),
                conv_ln_b=(m_conv_ln_b, v_conv_ln_b), attn_sinks=(m_attn_sinks, v_attn_sinks),
                w_out=(m_w_out, v_w_out), ffn2_norm=(m_ffn2_norm, v_ffn2_norm),
                ffn2_w_gate=(m_ffn2_w_gate, v_ffn2_w_gate), ffn2_w_up=(m_ffn2_w_up, v_ffn2_w_up),
                ffn2_w_down=(m_ffn2_w_down, v_ffn2_w_down), final_norm=(m_final_norm, v_final_norm))
    names = list(weights)
    big_names = dict(ffn1_w_gate=("f1gu", 0, True), ffn1_w_up=("f1gu", 1, True), ffn1_w_down=("f1d", 0, False),
                     w_in=("mx", 1, True), w_out=("mx", 0, False), ffn2_w_gate=("f2gu", 0, True),
                     ffn2_w_up=("f2gu", 1, True), ffn2_w_down=("f2d", 0, False))
    delta, new_m, new_v = {}, {}, {}
    for nme, (group, idx, transposed) in big_names.items():
        view = tr if transposed else (lambda a: a)
        res = adamw_layers(view(weights[nme]), view(moms[nme][0]), view(moms[nme][1]),
                           [chains[(group, l)].final[idx] for l in range(L)])
        grads[nme], delta[nme], new_m[nme], new_v[nme] = [view(a) for a in res]
    small_names = [nme for nme in names if nme not in big_names]
    s_shapes = [weights[nme].shape for nme in small_names]
    n_tot = sum(math.prod(s) for s in s_shapes)
    srows = -(-n_tot // 1024) * 8
    d, nm_, nv_ = adamw(_pack([weights[nme] for nme in small_names], srows),
                        _pack([grads[nme] for nme in small_names], srows),
                        _pack([moms[nme][0] for nme in small_names], srows),
                        _pack([moms[nme][1] for nme in small_names], srows))
    for nme, dd, mm, vv in zip(small_names, _unpack(d, s_shapes), _unpack(nm_, s_shapes), _unpack(nv_, s_shapes)):
        delta[nme], new_m[nme], new_v[nme] = dd, mm, vv

    return (loss_out, dx[None], *[grads[nme] for nme in names], *[delta[nme] for nme in names],
            *[new_m[nme] for nme in names], *[new_v[nme] for nme in names])
```

```python
import functools
import math

import jax
import jax.numpy as jnp
from jax import lax
from jax.experimental import pallas as pl
from jax.experimental.pallas import tpu as pltpu

F32 = jnp.float32
CDT = jnp.bfloat16
D_MODEL = 1024
D_FF = 2816
N_HEADS = 8
HEAD_DIM = 64
BLOCK = 128
CONV_K = 31
CONV_C = 512
ATT_W = 512
D_EXT = 2048
EPS = 1e-5
HALO = 32
FF_CHUNK = 256
NEG = float(jnp.finfo(jnp.float32).min)
VMEM_LIMIT = 56 * 1024 * 1024

ADAM_LR = 0.001
ADAM_B1 = 0.9
ADAM_B2 = 0.999
ADAM_EPS = 1e-08
ADAM_WD = 0.01
ADAM_STEP = 10

NT = (((1,), (1,)), ((), ()))
TN = (((0,), (0,)), ((), ()))


MESH = pl.DeviceIdType.MESH
ANY = pl.BlockSpec(memory_space=pl.ANY)
N_CHIPS = 4
N_DEV = 8


def _params(n_axes):
    return pltpu.CompilerParams(dimension_semantics=("arbitrary",) * n_axes, vmem_limit_bytes=VMEM_LIMIT)


class _Comm:
    def __init__(self, name, inputs, out_shape, sems, descs, relay=None):
        self.name, self.inputs, self.out_shape, self.sems = name, list(inputs), list(out_shape), list(sems)
        self.descs, self.relay = descs, relay


def _merge(*ops):
    ops = [o for o in ops if o is not None]
    if len(ops) == 1:
        return ops[0]
    assert all(o.relay is None for o in ops)

    def descs(cins, couts, sems):
        out, i, o, s = [], 0, 0, 0
        for op in ops:
            ni, no, ns = len(op.inputs), len(op.out_shape), len(op.sems)
            out += op.descs(cins[i:i + ni], couts[o:o + no], sems[s:s + ns])
            i, o, s = i + ni, o + no, s + ns
        return out

    return _Comm("_".join(o.name for o in ops), sum((o.inputs for o in ops), []),
                 sum((o.out_shape for o in ops), []), sum((o.sems for o in ops), []), descs)


def _split(couts, *ops):
    res, o = [], 0
    for op in ops:
        res.append(couts[o:o + len(op.out_shape)])
        o += len(op.out_shape)
    return res


def _hosted(comm):
    if comm is None:
        return pl.pallas_call

    def make(body, *, name, grid, in_specs, out_specs, out_shape, compiler_params, scratch_shapes=()):
        single = not isinstance(out_shape, (list, tuple))
        o_specs = [out_specs] if single else list(out_specs)
        o_shape = [out_shape] if single else list(out_shape)
        n_in, n_out, n_sc = len(in_specs), len(o_specs), len(scratch_shapes)
        c_in, c_out = len(comm.inputs), len(comm.out_shape)

        def hosted(*refs):
            ins, cins = refs[:n_in], refs[n_in:n_in + c_in]
            o0 = n_in + c_in
            outs, couts = refs[o0:o0 + n_out], refs[o0 + n_out:o0 + n_out + c_out]
            s0 = o0 + n_out + c_out
            scr, sems = refs[s0:s0 + n_sc], refs[s0 + n_sc:]
            first = pl.program_id(0) == 0
            last = pl.program_id(0) == grid[0] - 1
            for ax in range(1, len(grid)):
                first = first & (pl.program_id(ax) == 0)
                last = last & (pl.program_id(ax) == grid[ax] - 1)

            @pl.when(first)
            def _():
                for d in comm.descs(cins, couts, sems):
                    d.start()

            if comm.relay is not None:
                assert len(grid) == 1

                @pl.when(pl.program_id(0) == (3 * grid[0]) // 4)
                def _():
                    for d in comm.descs(cins, couts, sems):
                        d.wait()
                    for d in comm.relay(cins, couts, sems):
                        d.start()

            body(*ins, *outs, *scr)

            @pl.when(last)
            def _():
                for d in (comm.relay or comm.descs)(cins, couts, sems):
                    d.wait()

        call = pl.pallas_call(
            hosted, name=f"{name}_{comm.name}", grid=grid,
            in_specs=list(in_specs) + [ANY] * c_in, out_specs=o_specs + [ANY] * c_out,
            out_shape=o_shape + comm.out_shape, scratch_shapes=list(scratch_shapes) + comm.sems,
            compiler_params=compiler_params)

        def run(*args):
            res = call(*args, *comm.inputs)
            return (res[0] if single else list(res[:n_out])), list(res[n_out:])

        return run

    return make


def _run_comm(comm):
    c_in = len(comm.inputs)

    def body(*refs):
        cins, couts, sems = refs[:c_in], refs[c_in:c_in + len(comm.out_shape)], refs[c_in + len(comm.out_shape):]
        ds = comm.descs(cins, couts, sems)
        for d in ds:
            d.start()
        for d in ds:
            d.wait()

    return list(pl.pallas_call(
        body, name=comm.name, in_specs=[ANY] * c_in, out_specs=[ANY] * len(comm.out_shape),
        out_shape=comm.out_shape, scratch_shapes=comm.sems)(*comm.inputs))


def _coords():
    return lax.axis_index("x"), lax.axis_index("y"), lax.axis_index("c")


def _other_chips(x, y):
    return [(1 - x, y), (x, 1 - y), (1 - x, 1 - y)]


def ag_op(shards):
    n = len(shards)

    def descs(cins, couts, sems):
        send, recv = sems
        x, y, c = _coords()
        j = 2 * x + y
        ds = []
        for a in range(n):
            for p, (px, py) in enumerate(_other_chips(x, y)):
                ds.append(pltpu.make_async_remote_copy(cins[a], couts[a].at[j], send.at[a, p], recv.at[a, p],
                                                       device_id=(px, py, c), device_id_type=MESH))
        return ds

    return _Comm("ag", shards, [jax.ShapeDtypeStruct((N_CHIPS,) + s.shape, s.dtype) for s in shards],
                 [pltpu.SemaphoreType.DMA((n, 3)), pltpu.SemaphoreType.DMA((n, 3))], descs)


def ag2_op(shards):
    n = len(shards)
    halves = [s.reshape(2, s.shape[0] // 2, s.shape[1]) for s in shards]

    def descs(cins, couts, sems):
        x, y, c = _coords()
        j = 2 * x + y
        return [pltpu.make_async_remote_copy(cins[a].at[c], couts[a].at[j, c], sems[0].at[a, p], sems[1].at[a, p],
                                             device_id=(px, py, c), device_id_type=MESH)
                for a in range(n) for p, (px, py) in enumerate(_other_chips(x, y))]

    def relay(cins, couts, sems):
        x, y, c = _coords()
        return [pltpu.make_async_remote_copy(couts[a].at[2 * px + py, c], couts[a].at[2 * px + py, c],
                                             sems[2].at[a, p], sems[3].at[a, p],
                                             device_id=(x, y, 1 - c), device_id_type=MESH)
                for a in range(n) for p, (px, py) in enumerate(_other_chips(x, y))]

    return _Comm("ag2", halves, [jax.ShapeDtypeStruct((N_CHIPS,) + h.shape, h.dtype) for h in halves],
                 [pltpu.SemaphoreType.DMA((n, 3))] * 4, descs, relay)


def _own_slab(gathered, mine, idx):
    return lax.dynamic_update_slice_in_dim(gathered, mine[None], idx, axis=0)


def first_gather(shards):
    n = len(shards)
    halves = [s.reshape(2, s.shape[0] // 2, s.shape[1]) for s in shards]

    def body(*refs):
        ins, outs = refs[:n], refs[n:2 * n]
        send1, recv1, send2, recv2 = refs[2 * n:]
        x, y, c = _coords()
        j = 2 * x + y
        chips = _other_chips(x, y)
        ici = [pltpu.make_async_remote_copy(ins[a].at[c], outs[a].at[j, c], send1.at[a, p], recv1.at[a, p],
                                            device_id=(px, py, c), device_id_type=MESH)
               for a in range(n) for p, (px, py) in enumerate(chips)]
        for d in ici:
            d.start()
        d2d = [pltpu.make_async_remote_copy(outs[a].at[2 * px + py, c], outs[a].at[2 * px + py, c],
                                            send2.at[a, p], recv2.at[a, p],
                                            device_id=(x, y, 1 - c), device_id_type=MESH)
               for a in range(n) for p, (px, py) in enumerate(chips)]
        for d1, d2 in zip(ici, d2d):
            d1.wait()
            d2.start()
        for d in d2d:
            d.wait()

    return list(pl.pallas_call(
        body, name="first_gather", in_specs=[ANY] * n, out_specs=[ANY] * n,
        out_shape=[jax.ShapeDtypeStruct((N_CHIPS,) + h.shape, h.dtype) for h in halves],
        scratch_shapes=[pltpu.SemaphoreType.DMA((n, 3))] * 4)(*halves))


def swap_op(grads):
    n = len(grads)

    def descs(cins, couts, sems):
        send, recv = sems
        x, y, c = _coords()
        return [pltpu.make_async_remote_copy(cins[a].at[:, 1 - c], couts[a], send.at[a], recv.at[a],
                                             device_id=(x, y, 1 - c), device_id_type=MESH) for a in range(n)]

    return _Comm("swap", grads, [jax.ShapeDtypeStruct(g.shape[:1] + g.shape[2:], g.dtype) for g in grads],
                 [pltpu.SemaphoreType.DMA((n,)), pltpu.SemaphoreType.DMA((n,))], descs)


def exchange_op(parts):
    n = len(parts)

    def descs(cins, couts, sems):
        send, recv = sems
        x, y, c = _coords()
        ds = []
        for a in range(n):
            for p, (px, py) in enumerate(_other_chips(x, y)):
                ds.append(pltpu.make_async_remote_copy(cins[a].at[2 * px + py], couts[a].at[p], send.at[a, p],
                                                       recv.at[a, p], device_id=(px, py, c), device_id_type=MESH))
        return ds

    return _Comm("xchg", parts, [jax.ShapeDtypeStruct((3,) + p.shape[1:], p.dtype) for p in parts],
                 [pltpu.SemaphoreType.DMA((n, 3)), pltpu.SemaphoreType.DMA((n, 3))], descs)


def share_op(totals):
    n = len(totals)

    def descs(cins, couts, sems):
        send, recv = sems
        x, y, c = _coords()
        return [pltpu.make_async_remote_copy(cins[a], couts[a].at[c], send.at[a], recv.at[a],
                                             device_id=(x, y, 1 - c), device_id_type=MESH) for a in range(n)]

    return _Comm("share", totals, [jax.ShapeDtypeStruct((2,) + t.shape, t.dtype) for t in totals],
                 [pltpu.SemaphoreType.DMA((n,)), pltpu.SemaphoreType.DMA((n,))], descs)


def _sigmoid(z):
    return 1.0 / (1.0 + jnp.exp(-z))


def _rms_parts(xf):
    r = lax.rsqrt(jnp.mean(xf * xf, axis=-1, keepdims=True) + EPS)
    return xf * r, r


def _rms_bwd(dh, xhat, r, g):
    dg = jnp.sum(dh * xhat, axis=0, keepdims=True)
    dxhat = dh * g
    dx = r * (dxhat - xhat * jnp.mean(dxhat * xhat, axis=-1, keepdims=True))
    return dx, dg


def _chunks(n, ck=FF_CHUNK):
    return [(c0, min(ck, n - c0)) for c0 in range(0, n, ck)]


def ffn_fwd(x, g, wgT, wuT, wd, *, tm=512, comm=None):
    S, D = x.shape
    F = wgT.shape[0]

    def body(x_ref, g_ref, wg_ref, wu_ref, wd_ref, o_ref, gate_ref, up_ref, a_sc):
        xf = x_ref[...]
        xhat, _ = _rms_parts(xf)
        h = (xhat * g_ref[...]).astype(CDT)
        for c0, cw_ in _chunks(F):
            sl = slice(c0, c0 + cw_)
            gt = lax.dot_general(h, wg_ref[sl, :], NT, preferred_element_type=F32)
            ut = lax.dot_general(h, wu_ref[sl, :], NT, preferred_element_type=F32)
            gate_ref[:, sl] = gt.astype(CDT)
            up_ref[:, sl] = ut.astype(CDT)
            a_sc[:, sl] = (gt * _sigmoid(gt) * ut).astype(CDT)
        o_ref[...] = xf + 0.5 * jnp.dot(a_sc[...], wd_ref[...], preferred_element_type=F32)

    wspec = pl.BlockSpec((F, D), lambda i: (0, 0), pipeline_mode=pl.Buffered(1))
    return _hosted(comm)(
        body, name="ffn_fwd",
        grid=(S // tm,),
        in_specs=[pl.BlockSpec((tm, D), lambda i: (i, 0)), pl.BlockSpec((1, D), lambda i: (0, 0)),
                  wspec, wspec, wspec],
        out_specs=[pl.BlockSpec((tm, D), lambda i: (i, 0)),
                   pl.BlockSpec((tm, F), lambda i: (i, 0)),
                   pl.BlockSpec((tm, F), lambda i: (i, 0))],
        out_shape=[jax.ShapeDtypeStruct((S, D), F32),
                   jax.ShapeDtypeStruct((S, F), CDT),
                   jax.ShapeDtypeStruct((S, F), CDT)],
        scratch_shapes=[pltpu.VMEM((tm, F), CDT)],
        compiler_params=_params(1),
    )(x, g, wgT, wuT, wd)


def ffn_bwd_dgrad(x, g, dy, gate, up, wgT, wuT, wd, *, tm=256, comm=None):
    S, D = x.shape
    F = wgT.shape[0]

    def body(x_ref, g_ref, dy_ref, gate_ref, up_ref, wg_ref, wu_ref, wd_ref,
             dx_ref, dgate_ref, dup_ref, h_ref, dg_ref):
        @pl.when(pl.program_id(0) == 0)
        def _():
            dg_ref[...] = jnp.zeros_like(dg_ref)

        dyf = dy_ref[...]
        dacc = (0.5 * dyf).astype(CDT)
        gg = g_ref[...]
        xhat, r = _rms_parts(x_ref[...])
        h_ref[...] = (xhat * gg).astype(CDT)
        for c0, cw_ in _chunks(F):
            sl = slice(c0, c0 + cw_)
            d_a = lax.dot_general(dacc, wd_ref[sl, :], NT, preferred_element_type=F32)
            gt = gate_ref[:, sl].astype(F32)
            ut = up_ref[:, sl].astype(F32)
            sg = _sigmoid(gt)
            dup_ref[:, sl] = (d_a * (gt * sg)).astype(CDT)
            dgate_ref[:, sl] = (d_a * ut * (sg * (1.0 + gt * (1.0 - sg)))).astype(CDT)
        dh = (jnp.dot(dgate_ref[...], wg_ref[...], preferred_element_type=F32)
              + jnp.dot(dup_ref[...], wu_ref[...], preferred_element_type=F32))
        dx, dg = _rms_bwd(dh, xhat, r, gg)
        dx_ref[...] = dyf + dx
        dg_ref[...] += dg

    wspec = pl.BlockSpec((F, D), lambda i: (0, 0), pipeline_mode=pl.Buffered(1))
    row = pl.BlockSpec((tm, D), lambda i: (i, 0))
    act = pl.BlockSpec((tm, F), lambda i: (i, 0))
    vec = pl.BlockSpec((1, D), lambda i: (0, 0))
    return _hosted(comm)(
        body, name="ffn_bwd_dgrad",
        grid=(S // tm,),
        in_specs=[row, vec, row, act, act, wspec, wspec, wspec],
        out_specs=[row, act, act, row, vec],
        out_shape=[jax.ShapeDtypeStruct((S, D), F32),
                   jax.ShapeDtypeStruct((S, F), CDT),
                   jax.ShapeDtypeStruct((S, F), CDT),
                   jax.ShapeDtypeStruct((S, D), CDT),
                   jax.ShapeDtypeStruct((1, D), F32)],
        compiler_params=_params(1),
    )(x, g, dy, gate, up, wgT, wuT, wd)


def ffn_wgrad_down(dy, gate, up, *, tk=1024, comm=None):
    S, D = dy.shape
    F = gate.shape[1]
    tk = min(tk, S)

    def body(dy_ref, gate_ref, up_ref, dwd_ref):
        @pl.when(pl.program_id(0) == 0)
        def _():
            dwd_ref[...] = jnp.zeros_like(dwd_ref)

        dacc = (0.5 * dy_ref[...]).astype(CDT)
        for c0, cw_ in _chunks(F):
            sl = slice(c0, c0 + cw_)
            gt = gate_ref[:, sl].astype(F32)
            a = (gt * _sigmoid(gt) * up_ref[:, sl].astype(F32)).astype(CDT)
            dwd_ref[sl, :] += lax.dot_general(a, dacc, TN, preferred_element_type=F32)

    act = pl.BlockSpec((tk, F), lambda k: (k, 0))
    return _hosted(comm)(
        body, name="ffn_wgrad_down",
        grid=(S // tk,),
        in_specs=[pl.BlockSpec((tk, D), lambda k: (k, 0)), act, act],
        out_specs=pl.BlockSpec((F, D), lambda k: (0, 0), pipeline_mode=pl.Buffered(1)),
        out_shape=jax.ShapeDtypeStruct((F, D), F32),
        compiler_params=_params(1),
    )(dy, gate, up)


def ffn_wgrad_gate_up(h, dgate, dup, *, tk=1024, comm=None):
    S, D = h.shape
    F = dgate.shape[1]
    tk = min(tk, S)

    def body(h_ref, dgate_ref, dup_ref, dwg_ref, dwu_ref):
        @pl.when(pl.program_id(0) == 0)
        def _():
            dwg_ref[...] = jnp.zeros_like(dwg_ref)
            dwu_ref[...] = jnp.zeros_like(dwu_ref)

        hh = h_ref[...]
        for c0, cw_ in _chunks(F):
            sl = slice(c0, c0 + cw_)
            dwg_ref[sl, :] += lax.dot_general(dgate_ref[:, sl], hh, TN, preferred_element_type=F32)
            dwu_ref[sl, :] += lax.dot_general(dup_ref[:, sl], hh, TN, preferred_element_type=F32)

    act = pl.BlockSpec((tk, F), lambda k: (k, 0))
    out = pl.BlockSpec((F, D), lambda k: (0, 0), pipeline_mode=pl.Buffered(1))
    return _hosted(comm)(
        body, name="ffn_wgrad_gate_up",
        grid=(S // tk,),
        in_specs=[pl.BlockSpec((tk, D), lambda k: (k, 0)), act, act],
        out_specs=[out, out],
        out_shape=[jax.ShapeDtypeStruct((F, D), F32)] * 2,
        compiler_params=_params(1),
    )(h, dgate, dup)


def loss_head(x, g, target, *, tm=512):
    S, D = x.shape

    def body(x_ref, g_ref, t_ref, loss_ref, dx_ref, dg_ref):
        @pl.when(pl.program_id(0) == 0)
        def _():
            loss_ref[...] = jnp.zeros_like(loss_ref)
            dg_ref[...] = jnp.zeros_like(dg_ref)

        xhat, r = _rms_parts(x_ref[...])
        gg = g_ref[...]
        err = xhat * gg - t_ref[...]
        loss_ref[...] += 0.5 * jnp.sum(jnp.mean(err * err, axis=-1, keepdims=True), axis=0, keepdims=True)
        dx, dg = _rms_bwd(err * (1.0 / D), xhat, r, gg)
        dx_ref[...] = dx
        dg_ref[...] += dg

    row = pl.BlockSpec((tm, D), lambda i: (i, 0))
    vec = pl.BlockSpec((1, D), lambda i: (0, 0))
    return pl.pallas_call(
        body, name="loss_head",
        grid=(S // tm,),
        in_specs=[row, vec, row],
        out_specs=[pl.BlockSpec((1, 1), lambda i: (0, 0)), row, vec],
        out_shape=[jax.ShapeDtypeStruct((1, 1), F32), jax.ShapeDtypeStruct((S, D), F32),
                   jax.ShapeDtypeStruct((1, D), F32)],
        compiler_params=_params(1),
    )(x, g, target)


def _rope_apply(t, cs, sn):
    lane = lax.broadcasted_iota(jnp.int32, t.shape, 1)
    first = (lane % HEAD_DIM) < (HEAD_DIM // 2)
    rot = jnp.where(first, pltpu.roll(t, 128 - HEAD_DIM // 2, 1), pltpu.roll(t, HEAD_DIM // 2, 1))
    return t * cs + rot * sn


def _rope_transpose(d, cs, sn):
    lane = lax.broadcasted_iota(jnp.int32, d.shape, 1)
    first = (lane % HEAD_DIM) < (HEAD_DIM // 2)
    ds = d * sn
    rot = jnp.where(first, pltpu.roll(ds, 128 - HEAD_DIM // 2, 1), pltpu.roll(ds, HEAD_DIM // 2, 1))
    return d * cs + rot


def inproj_fwd(x, g, wextT, cs, sn, *, tm=512, comm=None):
    S, D = x.shape
    scale = HEAD_DIM ** -0.5

    def body(x_ref, g_ref, w_ref, cs_ref, sn_ref, q_ref, k_ref, v_ref, u_ref):
        xhat, _ = _rms_parts(x_ref[...])
        h = (xhat * g_ref[...]).astype(CDT)
        p = lax.dot_general(h, w_ref[...], NT, preferred_element_type=F32)
        c, s = cs_ref[...], sn_ref[...]
        for b in range(4):
            q_ref[:, 128 * b:128 * (b + 1)] = (_rope_apply(p[:, 128 * b:128 * (b + 1)], c, s) * scale).astype(CDT)
        for b in range(2):
            k_ref[:, 128 * b:128 * (b + 1)] = _rope_apply(p[:, 512 + 128 * b:512 + 128 * (b + 1)], c, s).astype(CDT)
        v_ref[...] = p[:, 768:1024].astype(CDT)
        u_ref[...] = p[:, 1024:2048]

    def row(w):
        return pl.BlockSpec((tm, w), lambda i: (i, 0))

    return _hosted(comm)(
        body, name="inproj_fwd",
        grid=(S // tm,),
        in_specs=[row(D), pl.BlockSpec((1, D), lambda i: (0, 0)),
                  pl.BlockSpec((D_EXT, D), lambda i: (0, 0)), row(128), row(128)],
        out_specs=[row(512), row(256), row(256), row(1024)],
        out_shape=[jax.ShapeDtypeStruct((S, 512), CDT), jax.ShapeDtypeStruct((S, 256), CDT),
                   jax.ShapeDtypeStruct((S, 256), CDT), jax.ShapeDtypeStruct((S, 1024), F32)],
        compiler_params=_params(1),
    )(x, g, wextT, cs, sn)


def _stack_heads(p0, p1):
    lane = lax.broadcasted_iota(jnp.int32, p0.shape, 1)
    lo = lane < HEAD_DIM
    z = jnp.zeros_like(p0)
    return jnp.concatenate([jnp.where(lo, p0, z), jnp.where(lo, z, p0),
                            jnp.where(lo, p1, z), jnp.where(lo, z, p1)], axis=0)


def _unstack_heads(o):
    lane = lax.broadcasted_iota(jnp.int32, (BLOCK, 128), 1)
    lo = lane < HEAD_DIM
    return (jnp.where(lo, o[0:128], o[128:256]), jnp.where(lo, o[256:384], o[384:512]))


def _band_mask_kq(n):
    c = lax.broadcasted_iota(jnp.int32, (2 * BLOCK, 4 * BLOCK), 0)
    i = lax.broadcasted_iota(jnp.int32, (2 * BLOCK, 4 * BLOCK), 1) % BLOCK
    return (c > i) & (c <= i + BLOCK) & ((n > 0) | (c >= BLOCK))


def attn_fwd(q, k, v, sink_row, *, nb=4, comm=None):
    S = q.shape[0]
    tq = nb * BLOCK

    def body(q_ref, k_ref, v_ref, sink_ref, o_ref):
        t = pl.program_id(0)
        for b in range(nb):
            n = t * nb + b
            prev = pl.multiple_of(jnp.maximum(n - 1, 0) * BLOCK, BLOCK)
            cur = pl.multiple_of(n * BLOCK, BLOCK)
            rows = slice(b * BLOCK, (b + 1) * BLOCK)
            mask = _band_mask_kq(n)
            for gidx in range(2):
                lanes = slice(128 * gidx, 128 * (gidx + 1))
                qs = _stack_heads(q_ref[rows, 256 * gidx:256 * gidx + 128],
                                  q_ref[rows, 256 * gidx + 128:256 * gidx + 256])
                kb = jnp.concatenate([k_ref[pl.ds(prev, BLOCK), lanes], k_ref[pl.ds(cur, BLOCK), lanes]], axis=0)
                vb = jnp.concatenate([v_ref[pl.ds(prev, BLOCK), lanes], v_ref[pl.ds(cur, BLOCK), lanes]], axis=0)
                st = lax.dot_general(kb, qs, NT, preferred_element_type=F32)
                st = jnp.where(mask, st, NEG)
                sink = sink_ref[gidx]
                m = jnp.maximum(jnp.max(st, axis=0, keepdims=True), sink)
                e = jnp.exp(st - m)
                inv = 1.0 / (jnp.sum(e, axis=0, keepdims=True) + jnp.exp(sink - m))
                o = lax.dot_general((e * inv).astype(CDT), vb, TN, preferred_element_type=F32)
                o0, o1 = _unstack_heads(o)
                o_ref[rows, 256 * gidx:256 * gidx + 128] = o0.astype(CDT)
                o_ref[rows, 256 * gidx + 128:256 * gidx + 256] = o1.astype(CDT)

    return _hosted(comm)(
        body, name="attn_fwd",
        grid=(S // tq,),
        in_specs=[pl.BlockSpec((tq, 512), lambda t: (t, 0)),
                  pl.BlockSpec((S, 256), lambda t: (0, 0)),
                  pl.BlockSpec((S, 256), lambda t: (0, 0)),
                  pl.BlockSpec((2, 1, 4 * BLOCK), lambda t: (0, 0, 0))],
        out_specs=pl.BlockSpec((tq, 512), lambda t: (t, 0)),
        out_shape=jax.ShapeDtypeStruct((S, 512), CDT),
        compiler_params=_params(1),
    )(q, k, v, sink_row)


def attn_bwd(q, k, v, do, sink_row, *, nb=4, comm=None):
    S = q.shape[0]
    tq = nb * BLOCK
    scale = HEAD_DIM ** -0.5

    def body(q_ref, k_ref, v_ref, do_ref, sink_ref, dq_ref, dk_ref, dv_ref, dsink_ref):
        t = pl.program_id(0)

        @pl.when(t == 0)
        def _():
            dk_ref[...] = jnp.zeros_like(dk_ref)
            dv_ref[...] = jnp.zeros_like(dv_ref)
            dsink_ref[...] = jnp.zeros_like(dsink_ref)

        for b in range(nb):
            n = t * nb + b
            prev = pl.multiple_of(jnp.maximum(n - 1, 0) * BLOCK, BLOCK)
            cur = pl.multiple_of(n * BLOCK, BLOCK)
            rows = slice(b * BLOCK, (b + 1) * BLOCK)
            mask = _band_mask_kq(n)
            for gidx in range(2):
                lanes = slice(128 * gidx, 128 * (gidx + 1))
                qs = _stack_heads(q_ref[rows, 256 * gidx:256 * gidx + 128],
                                  q_ref[rows, 256 * gidx + 128:256 * gidx + 256])
                dos = _stack_heads(do_ref[rows, 256 * gidx:256 * gidx + 128],
                                   do_ref[rows, 256 * gidx + 128:256 * gidx + 256])
                kb = jnp.concatenate([k_ref[pl.ds(prev, BLOCK), lanes], k_ref[pl.ds(cur, BLOCK), lanes]], axis=0)
                vb = jnp.concatenate([v_ref[pl.ds(prev, BLOCK), lanes], v_ref[pl.ds(cur, BLOCK), lanes]], axis=0)
                st = lax.dot_general(kb, qs, NT, preferred_element_type=F32)
                st = jnp.where(mask, st, NEG)
                sink = sink_ref[gidx]
                m = jnp.maximum(jnp.max(st, axis=0, keepdims=True), sink)
                e = jnp.exp(st - m)
                es = jnp.exp(sink - m)
                inv = 1.0 / (jnp.sum(e, axis=0, keepdims=True) + es)
                pt = e * inv
                dpt = lax.dot_general(vb, dos, NT, preferred_element_type=F32)
                delta = jnp.sum(pt * dpt, axis=0, keepdims=True)
                dst = (pt * (dpt - delta)).astype(CDT)
                dsink_ref[gidx] += -(es * inv) * delta
                dvb = jnp.dot(pt.astype(CDT), dos, preferred_element_type=F32)
                dkb = jnp.dot(dst, qs, preferred_element_type=F32)
                dqs = lax.dot_general(dst, kb, TN, preferred_element_type=F32) * scale
                dq0, dq1 = _unstack_heads(dqs)
                dq_ref[rows, 256 * gidx:256 * gidx + 128] = dq0
                dq_ref[rows, 256 * gidx + 128:256 * gidx + 256] = dq1
                dk_ref[pl.ds(prev, BLOCK), lanes] += dkb[0:BLOCK]
                dk_ref[pl.ds(cur, BLOCK), lanes] += dkb[BLOCK:2 * BLOCK]
                dv_ref[pl.ds(prev, BLOCK), lanes] += dvb[0:BLOCK]
                dv_ref[pl.ds(cur, BLOCK), lanes] += dvb[BLOCK:2 * BLOCK]

    full = pl.BlockSpec((S, 256), lambda t: (0, 0))
    tile = pl.BlockSpec((tq, 512), lambda t: (t, 0))
    srow = pl.BlockSpec((2, 1, 4 * BLOCK), lambda t: (0, 0, 0))
    return _hosted(comm)(
        body, name="attn_bwd",
        grid=(S // tq,),
        in_specs=[tile, full, full, tile, srow],
        out_specs=[tile, full, full, srow],
        out_shape=[jax.ShapeDtypeStruct((S, 512), F32), jax.ShapeDtypeStruct((S, 256), F32),
                   jax.ShapeDtypeStruct((S, 256), F32), jax.ShapeDtypeStruct((2, 1, 4 * BLOCK), F32)],
        compiler_params=_params(1),
    )(q, k, v, do, sink_row)


def _glu(u):
    a = u[:, 0:CONV_C]
    gt = u[:, CONV_C:2 * CONV_C]
    sg = _sigmoid(gt)
    return a, sg, a * sg


CONV_CHUNK = 32


def _shifted_copies(buf, shifted, n):
    for r in range(1, 8):
        shifted[r - 1, 0:n, :] = buf[r:r + n, :]


def _shifted_rows(buf, shifted, start, rows):
    r = start % 8
    if r == 0:
        return buf[start:start + rows, :]
    return shifted[r - 1, start - r:start - r + rows, :]


def conv_fwd(u, cw, cb, lg, lb, *, tm=512, comm=None):
    S = u.shape[0]
    nh = tm // HALO

    def body(u_ref, uh_ref, cw_ref, cb_ref, lg_ref, lb_ref, o_ref, y_ref, hbuf, hsh):
        t = pl.program_id(0)
        _, _, hg = _glu(u_ref[...])
        _, _, hh = _glu(uh_ref[...])
        hbuf[0:HALO, :] = jnp.where(t > 0, hh, jnp.zeros_like(hh))
        hbuf[HALO:HALO + tm, :] = hg
        hbuf[HALO + tm:HALO + tm + 8, :] = jnp.zeros((8, CONV_C), F32)
        _shifted_copies(hbuf, hsh, HALO + tm)
        off = HALO - (CONV_K - 1)
        for c0 in range(0, tm, CONV_CHUNK):
            acc = jnp.zeros((CONV_CHUNK, CONV_C), F32) + cb_ref[...]
            for j in range(CONV_K):
                acc = acc + cw_ref[j:j + 1, :] * _shifted_rows(hbuf, hsh, c0 + off + j, CONV_CHUNK)
            y_ref[c0:c0 + CONV_CHUNK, :] = acc
        y = y_ref[...]
        yc = y - jnp.mean(y, axis=-1, keepdims=True)
        r = lax.rsqrt(jnp.mean(yc * yc, axis=-1, keepdims=True) + EPS)
        z = yc * r * lg_ref[...] + lb_ref[...]
        o_ref[...] = (z * _sigmoid(z)).astype(CDT)

    vec = pl.BlockSpec((1, CONV_C), lambda t: (0, 0))
    return _hosted(comm)(
        body, name="conv_fwd",
        grid=(S // tm,),
        in_specs=[pl.BlockSpec((tm, 2 * CONV_C), lambda t: (t, 0)),
                  pl.BlockSpec((HALO, 2 * CONV_C), lambda t: (jnp.maximum(t * nh - 1, 0), 0)),
                  pl.BlockSpec((CONV_K, CONV_C), lambda t: (0, 0)), vec, vec, vec],
        out_specs=[pl.BlockSpec((tm, CONV_C), lambda t: (t, 0)), pl.BlockSpec((tm, CONV_C), lambda t: (t, 0))],
        out_shape=[jax.ShapeDtypeStruct((S, CONV_C), CDT), jax.ShapeDtypeStruct((S, CONV_C), F32)],
        scratch_shapes=[pltpu.VMEM((HALO + tm + 8, CONV_C), F32), pltpu.VMEM((7, HALO + tm, CONV_C), F32)],
        compiler_params=_params(1),
    )(u, u, cw, cb, lg, lb)


def conv_bwd(dc, u, y, cw, lg, lb, *, tm=512, comm=None):
    S = u.shape[0]
    nh = tm // HALO
    nt = S // tm

    def ln_bwd(dcv, yv, lgv, lbv):
        yc = yv - jnp.mean(yv, axis=-1, keepdims=True)
        r = lax.rsqrt(jnp.mean(yc * yc, axis=-1, keepdims=True) + EPS)
        yhat = yc * r
        z = yhat * lgv + lbv
        sg = _sigmoid(z)
        dz = dcv * (sg * (1.0 + z * (1.0 - sg)))
        dyhat = dz * lgv
        dy = r * (dyhat - jnp.mean(dyhat, axis=-1, keepdims=True)
                  - yhat * jnp.mean(dyhat * yhat, axis=-1, keepdims=True))
        return dy, dz, yhat

    def body(dc_ref, dcn_ref, u_ref, uh_ref, y_ref, yn_ref, cw_ref, lg_ref, lb_ref,
             du_ref, dcw_ref, dcb_ref, dlg_ref, dlb_ref, hbuf, dybuf, dhg_sc, dw_sc, hsh, dysh):
        t = pl.program_id(0)

        @pl.when(t == 0)
        def _():
            dw_sc[...] = jnp.zeros_like(dw_sc)
            dcb_ref[...] = jnp.zeros_like(dcb_ref)
            dlg_ref[...] = jnp.zeros_like(dlg_ref)
            dlb_ref[...] = jnp.zeros_like(dlb_ref)

        lgv, lbv = lg_ref[...], lb_ref[...]
        dy, dz, yhat = ln_bwd(dc_ref[...].astype(F32), y_ref[...], lgv, lbv)
        dyn, _, _ = ln_bwd(dcn_ref[...].astype(F32), yn_ref[...], lgv, lbv)
        dlb_ref[...] += jnp.sum(dz, axis=0, keepdims=True)
        dlg_ref[...] += jnp.sum(dz * yhat, axis=0, keepdims=True)
        dcb_ref[...] += jnp.sum(dy, axis=0, keepdims=True)
        dybuf[0:tm, :] = dy
        dybuf[tm:tm + HALO, :] = jnp.where(t < nt - 1, dyn, jnp.zeros_like(dyn))
        dybuf[tm + HALO:tm + HALO + 8, :] = jnp.zeros((8, CONV_C), F32)
        _shifted_copies(dybuf, dysh, tm + HALO)

        a, sg, hg = _glu(u_ref[...])
        _, _, hh = _glu(uh_ref[...])
        hbuf[0:HALO, :] = jnp.where(t > 0, hh, jnp.zeros_like(hh))
        hbuf[HALO:HALO + tm, :] = hg
        hbuf[HALO + tm:HALO + tm + 8, :] = jnp.zeros((8, CONV_C), F32)
        _shifted_copies(hbuf, hsh, HALO + tm)

        off = HALO - (CONV_K - 1)
        for c0 in range(0, tm, CONV_CHUNK):
            acc = jnp.zeros((CONV_CHUNK, CONV_C), F32)
            dyc = dybuf[c0:c0 + CONV_CHUNK, :]
            for j in range(CONV_K):
                acc = acc + cw_ref[j:j + 1, :] * _shifted_rows(dybuf, dysh, c0 + (CONV_K - 1) - j, CONV_CHUNK)
                prod = dyc * _shifted_rows(hbuf, hsh, c0 + off + j, CONV_CHUNK)
                dw_sc[j] += jnp.sum(prod.reshape(CONV_CHUNK // 8, 8, CONV_C), axis=0)
            dhg_sc[c0:c0 + CONV_CHUNK, :] = acc

        dhg = dhg_sc[...]
        du_ref[:, 0:CONV_C] = dhg * sg
        du_ref[:, CONV_C:2 * CONV_C] = dhg * a * sg * (1.0 - sg)

        @pl.when(t == nt - 1)
        def _():
            dcw_ref[...] = jnp.sum(dw_sc[...], axis=1)

    vec = pl.BlockSpec((1, CONV_C), lambda t: (0, 0))
    tile = pl.BlockSpec((tm, CONV_C), lambda t: (t, 0))
    nxt = pl.BlockSpec((HALO, CONV_C), lambda t: (jnp.minimum((t + 1) * nh, S // HALO - 1), 0))
    return _hosted(comm)(
        body, name="conv_bwd",
        grid=(nt,),
        in_specs=[tile, nxt,
                  pl.BlockSpec((tm, 2 * CONV_C), lambda t: (t, 0)),
                  pl.BlockSpec((HALO, 2 * CONV_C), lambda t: (jnp.maximum(t * nh - 1, 0), 0)),
                  tile, nxt,
                  pl.BlockSpec((CONV_K, CONV_C), lambda t: (0, 0)), vec, vec],
        out_specs=[pl.BlockSpec((tm, 2 * CONV_C), lambda t: (t, 0)),
                   pl.BlockSpec((CONV_K, CONV_C), lambda t: (0, 0)), vec, vec, vec],
        out_shape=[jax.ShapeDtypeStruct((S, 2 * CONV_C), F32), jax.ShapeDtypeStruct((CONV_K, CONV_C), F32),
                   jax.ShapeDtypeStruct((1, CONV_C), F32), jax.ShapeDtypeStruct((1, CONV_C), F32),
                   jax.ShapeDtypeStruct((1, CONV_C), F32)],
        scratch_shapes=[pltpu.VMEM((HALO + tm + 8, CONV_C), F32), pltpu.VMEM((tm + HALO + 8, CONV_C), F32),
                        pltpu.VMEM((tm, CONV_C), F32), pltpu.VMEM((CONV_K, 8, CONV_C), F32),
                        pltpu.VMEM((7, HALO + tm, CONV_C), F32), pltpu.VMEM((7, tm + HALO, CONV_C), F32)],
        compiler_params=_params(1),
    )(dc, dc, u, u, y, y, cw, lg, lb)


def outproj_fwd(x, ao, co, wout, *, tm=512):
    S, D = x.shape

    def body(x_ref, a_ref, c_ref, w_ref, o_ref):
        o_ref[...] = (x_ref[...]
                      + jnp.dot(a_ref[...], w_ref[0:ATT_W, :], preferred_element_type=F32)
                      + jnp.dot(c_ref[...], w_ref[ATT_W:ATT_W + CONV_C, :], preferred_element_type=F32))

    return pl.pallas_call(
        body, name="outproj_fwd",
        grid=(S // tm,),
        in_specs=[pl.BlockSpec((tm, D), lambda i: (i, 0)), pl.BlockSpec((tm, ATT_W), lambda i: (i, 0)),
                  pl.BlockSpec((tm, CONV_C), lambda i: (i, 0)), pl.BlockSpec((D, D), lambda i: (0, 0))],
        out_specs=pl.BlockSpec((tm, D), lambda i: (i, 0)),
        out_shape=jax.ShapeDtypeStruct((S, D), F32),
        compiler_params=_params(1),
    )(x, ao, co, wout)


def outproj_bwd(dx, ao, co, wout, *, tm=512, comm=None):
    S, D = dx.shape

    def body(dx_ref, a_ref, c_ref, w_ref, da_ref, dc_ref, dw_ref):
        @pl.when(pl.program_id(0) == 0)
        def _():
            dw_ref[...] = jnp.zeros_like(dw_ref)

        dxb = dx_ref[...].astype(CDT)
        da_ref[...] = lax.dot_general(dxb, w_ref[0:ATT_W, :], NT, preferred_element_type=F32).astype(CDT)
        dc_ref[...] = lax.dot_general(dxb, w_ref[ATT_W:ATT_W + CONV_C, :], NT, preferred_element_type=F32)
        dw_ref[0:ATT_W, :] += lax.dot_general(a_ref[...], dxb, TN, preferred_element_type=F32)
        dw_ref[ATT_W:ATT_W + CONV_C, :] += lax.dot_general(c_ref[...], dxb, TN, preferred_element_type=F32)

    return _hosted(comm)(
        body, name="outproj_bwd",
        grid=(S // tm,),
        in_specs=[pl.BlockSpec((tm, D), lambda i: (i, 0)), pl.BlockSpec((tm, ATT_W), lambda i: (i, 0)),
                  pl.BlockSpec((tm, CONV_C), lambda i: (i, 0)), pl.BlockSpec((D, D), lambda i: (0, 0))],
        out_specs=[pl.BlockSpec((tm, ATT_W), lambda i: (i, 0)), pl.BlockSpec((tm, CONV_C), lambda i: (i, 0)),
                   pl.BlockSpec((D, D), lambda i: (0, 0))],
        out_shape=[jax.ShapeDtypeStruct((S, ATT_W), CDT), jax.ShapeDtypeStruct((S, CONV_C), F32),
                   jax.ShapeDtypeStruct((D, D), F32)],
        compiler_params=_params(1),
    )(dx, ao, co, wout)


def inproj_bwd(x, g, dres, dq, dk, dv, du, wextT, cs, sn, *, tm=512):
    S, D = x.shape

    def body(x_ref, g_ref, dres_ref, dq_ref, dk_ref, dv_ref, du_ref, w_ref, cs_ref, sn_ref,
             dx_ref, dw_ref, dg_ref, dp_sc):
        @pl.when(pl.program_id(0) == 0)
        def _():
            dw_ref[...] = jnp.zeros_like(dw_ref)
            dg_ref[...] = jnp.zeros_like(dg_ref)

        c, s = cs_ref[...], sn_ref[...]
        for b in range(4):
            dp_sc[:, 128 * b:128 * (b + 1)] = _rope_transpose(dq_ref[:, 128 * b:128 * (b + 1)], c, s).astype(CDT)
        for b in range(2):
            dp_sc[:, 512 + 128 * b:512 + 128 * (b + 1)] = _rope_transpose(
                dk_ref[:, 128 * b:128 * (b + 1)], c, s).astype(CDT)
        dp_sc[:, 768:1024] = dv_ref[...].astype(CDT)
        dp_sc[:, 1024:2048] = du_ref[...].astype(CDT)
        dp = dp_sc[...]
        xhat, r = _rms_parts(x_ref[...])
        gg = g_ref[...]
        h = (xhat * gg).astype(CDT)
        dh = jnp.dot(dp, w_ref[...], preferred_element_type=F32)
        dw_ref[...] += lax.dot_general(dp, h, TN, preferred_element_type=F32)
        dx, dg = _rms_bwd(dh, xhat, r, gg)
        dx_ref[...] = dres_ref[...] + dx
        dg_ref[...] += dg

    def row(w):
        return pl.BlockSpec((tm, w), lambda i: (i, 0))

    return pl.pallas_call(
        body, name="inproj_bwd",
        grid=(S // tm,),
        in_specs=[row(D), pl.BlockSpec((1, D), lambda i: (0, 0)), row(D), row(512), row(256), row(256),
                  row(1024), pl.BlockSpec((D_EXT, D), lambda i: (0, 0), pipeline_mode=pl.Buffered(1)),
                  row(128), row(128)],
        out_specs=[row(D), pl.BlockSpec((D_EXT, D), lambda i: (0, 0), pipeline_mode=pl.Buffered(1)),
                   pl.BlockSpec((1, D), lambda i: (0, 0))],
        out_shape=[jax.ShapeDtypeStruct((S, D), F32), jax.ShapeDtypeStruct((D_EXT, D), F32),
                   jax.ShapeDtypeStruct((1, D), F32)],
        scratch_shapes=[pltpu.VMEM((tm, D_EXT), CDT)],
        compiler_params=_params(1),
    )(x, g, dres, dq, dk, dv, du, wextT, cs, sn)


def _rope_tables(positions):
    inv_freq = 1.0 / (10000.0 ** (jnp.arange(0, HEAD_DIM, 2, dtype=F32) / HEAD_DIM))
    ang = positions.astype(F32).reshape(-1, 1) * inv_freq
    cos, sin = jnp.cos(ang), jnp.sin(ang)
    cs = jnp.tile(jnp.concatenate([cos, cos], axis=-1), (1, 2))
    sn = jnp.tile(jnp.concatenate([-sin, sin], axis=-1), (1, 2))
    return cs, sn


def _widen_w_in(w):
    q, u = w[0:512], w[768:1792]
    parts = [q]
    for base in (512, 576, 640, 704):
        parts += [w[base:base + 64], w[base:base + 64]]
    return jnp.concatenate(parts + [u], axis=0)


def _fold_w_in(d):
    parts = [d[0:512]]
    for base in (512, 640, 768, 896):
        parts.append(d[base:base + 64] + d[base + 64:base + 128])
    return jnp.concatenate(parts + [d[1024:2048]], axis=0)


def add_half(g5, r1, c_idx):
    _, _, r, D = g5.shape

    def body(c_ref, g_ref, r_ref, o_ref):
        o_ref[...] = (g_ref[...] + r_ref[...]).astype(CDT)

    return pl.pallas_call(
        body, name="add_half",
        grid_spec=pltpu.PrefetchScalarGridSpec(
            num_scalar_prefetch=1, grid=(N_CHIPS,),
            in_specs=[pl.BlockSpec((None, None, r, D), lambda s, cr: (s, cr[0], 0, 0)),
                      pl.BlockSpec((None, r, D), lambda s, cr: (s, 0, 0))],
            out_specs=pl.BlockSpec((None, r, D), lambda s, cr: (s, 0, 0))),
        out_shape=jax.ShapeDtypeStruct((N_CHIPS, r, D), CDT),
        compiler_params=_params(1),
    )(c_idx, g5, r1)


def sum_partials(part, recv3, j_idx):
    _, r, D = part.shape
    tr_ = r // 2

    def body(j_ref, p_ref, r_ref, o_ref):
        o_ref[...] = ((p_ref[...].astype(F32) + r_ref[0].astype(F32)) + r_ref[1].astype(F32)) + r_ref[2].astype(F32)

    return pl.pallas_call(
        body, name="sum_partials",
        grid_spec=pltpu.PrefetchScalarGridSpec(
            num_scalar_prefetch=1, grid=(2,),
            in_specs=[pl.BlockSpec((None, tr_, D), lambda i, jr: (jr[0], i, 0)),
                      pl.BlockSpec((3, tr_, D), lambda i, jr: (0, i, 0))],
            out_specs=pl.BlockSpec((tr_, D), lambda i, jr: (i, 0))),
        out_shape=jax.ShapeDtypeStruct((r, D), F32),
        compiler_params=_params(1),
    )(j_idx, part, recv3)


class _Chain:
    STAGES = ("swap", "xchg", "share")

    def __init__(self, grads, c_idx, j_idx):
        self.c_idx, self.j_idx = c_idx, j_idx
        self.g5 = [g.reshape(N_CHIPS, 2, g.shape[0] // (2 * N_CHIPS), g.shape[1]) for g in grads]
        self.stage_no = 0

    @property
    def done(self):
        return self.stage_no == len(self.STAGES)

    def next_stage(self):
        name = self.STAGES[self.stage_no]

        def callback(res):
            getattr(self, "after_" + name)(res)
            self.stage_no += 1

        return getattr(self, name)(), callback

    def swap(self):
        return swap_op(self.g5)

    def after_swap(self, recv):
        self.parts = [add_half(g, r, self.c_idx) for g, r in zip(self.g5, recv)]

    def xchg(self):
        return exchange_op(self.parts)

    def after_xchg(self, recv):
        self.totals = [sum_partials(p, r, self.j_idx) for p, r in zip(self.parts, recv)]

    def share(self):
        return share_op(self.totals)

    def after_share(self, recv):
        both = [_own_slab(h, t, self.c_idx[0]) for h, t in zip(recv, self.totals)]
        self.final = [h.reshape(2 * h.shape[1], h.shape[2]) for h in both]


def all_reduce_small(vec):
    R = vec.shape[0]

    def body(v_ref, o_ref, buf, send, recv):
        x, y, c = _coords()
        me = 4 * x + 2 * y + c
        buf[me] = v_ref[...]
        cps = []
        for m in range(1, N_DEV):
            dx, dy, dc = (m >> 2) & 1, (m >> 1) & 1, m & 1
            cp = pltpu.make_async_remote_copy(v_ref, buf.at[me], send.at[m - 1], recv.at[m - 1],
                                              device_id=((x + dx) % 2, (y + dy) % 2, (c + dc) % 2),
                                              device_id_type=MESH)
            cp.start()
            cps.append(cp)
        for cp in cps:
            cp.wait()
        acc = buf[0]
        for d in range(1, N_DEV):
            acc = acc + buf[d]
        o_ref[...] = acc

    return pl.pallas_call(
        body, name="all_reduce_small",
        in_specs=[pl.BlockSpec(memory_space=pltpu.VMEM)], out_specs=pl.BlockSpec(memory_space=pltpu.VMEM),
        out_shape=jax.ShapeDtypeStruct(vec.shape, F32),
        scratch_shapes=[pltpu.VMEM((N_DEV, R, 128), F32), pltpu.SemaphoreType.DMA((N_DEV - 1,)),
                        pltpu.SemaphoreType.DMA((N_DEV - 1,))],
    )(vec)


def adamw(w, g, m, v, *, tm=512):
    R, C = w.shape
    tm = max(t for t in range(8, min(tm, R) + 1, 8) if R % t == 0)
    c1 =1.0 - ADAM_B1 ** ADAM_STEP
    c2 = 1.0 - ADAM_B2 ** ADAM_STEP

    def body(w_ref, g_ref, m_ref, v_ref, d_ref, nm_ref, nv_ref):
        gg = g_ref[...]
        nm = ADAM_B1 * m_ref[...] + (1.0 - ADAM_B1) * gg
        nv = ADAM_B2 * v_ref[...] + (1.0 - ADAM_B2) * (gg * gg)
        nm_ref[...] = nm
        nv_ref[...] = nv
        d_ref[...] = -ADAM_LR * ((nm / c1) / (jnp.sqrt(nv / c2) + ADAM_EPS) + ADAM_WD * w_ref[...])

    blk = pl.BlockSpec((tm, C), lambda i: (i, 0))
    return pl.pallas_call(
        body, name="adamw",
        grid=(pl.cdiv(R, tm),),
        in_specs=[blk] * 4, out_specs=[blk] * 3,
        out_shape=[jax.ShapeDtypeStruct((R, C), F32)] * 3,
        compiler_params=_params(1),
    )(w, g, m, v)


def adamw_layers(w, m, v, g_layers, *, tm=352, comm=None):
    L, R, C = w.shape
    tm = max(t for t in range(8, min(tm, R) + 1, 8) if R % t == 0)
    c1 = 1.0 - ADAM_B1 ** ADAM_STEP
    c2 = 1.0 - ADAM_B2 ** ADAM_STEP

    def body(w_ref, m_ref, v_ref, *rest):
        g_refs, (go_ref, d_ref, nm_ref, nv_ref) = rest[:L], rest[L:]
        layer = pl.program_id(0)
        gg = g_refs[0][...]
        for l in range(1, L):
            gg = jnp.where(layer == l, g_refs[l][...], gg)
        nm = ADAM_B1 * m_ref[...] + (1.0 - ADAM_B1) * gg
        nv = ADAM_B2 * v_ref[...] + (1.0 - ADAM_B2) * (gg * gg)
        go_ref[...] = gg
        nm_ref[...] = nm
        nv_ref[...] = nv
        d_ref[...] = -ADAM_LR * ((nm / c1) / (jnp.sqrt(nv / c2) + ADAM_EPS) + ADAM_WD * w_ref[...])

    blk = pl.BlockSpec((None, tm, C), lambda l, i: (l, i, 0))
    gblk = pl.BlockSpec((tm, C), lambda l, i: (i, 0))
    return _hosted(comm)(
        body, name="adamw_layers",
        grid=(L, R // tm),
        in_specs=[blk] * 3 + [gblk] * L, out_specs=[blk] * 4,
        out_shape=[jax.ShapeDtypeStruct((L, R, C), F32)] * 4,
        compiler_params=_params(2),
    )(w, m, v, *g_layers)


_SMALL = (("n1", (2, D_MODEL)), ("nm", (2, D_MODEL)), ("n2", (2, D_MODEL)), ("nf", (D_MODEL,)),
          ("cb", (2, CONV_C)), ("lg", (2, CONV_C)), ("lb", (2, CONV_C)), ("sinks", (2, N_HEADS)),
          ("cw", (2, CONV_K, CONV_C)))


def _pack(parts, rows):
    flat = jnp.concatenate([p.reshape(-1).astype(F32) for p in parts])
    return jnp.pad(flat, (0, rows * 128 - flat.shape[0])).reshape(rows, 128)


def _unpack(block, shapes):
    flat = block.reshape(-1)
    out, o = [], 0
    for shp in shapes:
        n = 1
        for s in shp:
            n *= s
        out.append(flat[o:o + n].reshape(shp))
        o += n
    return out


def kernel(x, positions, ffn1_norm, ffn1_w_gate, ffn1_w_up, ffn1_w_down, mix_norm, w_in, conv_w, conv_b, conv_ln_g, conv_ln_b, attn_sinks, w_out, ffn2_norm, ffn2_w_gate, ffn2_w_up, ffn2_w_down, final_norm, loss_target, m_ffn1_norm, m_ffn1_w_gate, m_ffn1_w_up, m_ffn1_w_down, m_mix_norm, m_w_in, m_conv_w, m_conv_b, m_conv_ln_g, m_conv_ln_b, m_attn_sinks, m_w_out, m_ffn2_norm, m_ffn2_w_gate, m_ffn2_w_up, m_ffn2_w_down, m_final_norm, v_ffn1_norm, v_ffn1_w_gate, v_ffn1_w_up, v_ffn1_w_down, v_mix_norm, v_w_in, v_conv_w, v_conv_b, v_conv_ln_g, v_conv_ln_b, v_attn_sinks, v_w_out, v_ffn2_norm, v_ffn2_w_gate, v_ffn2_w_up, v_ffn2_w_down, v_final_norm):
    cx, cy, cc = _coords()
    chip = 2 * cx + cy
    c_idx = jnp.reshape(cc, (1,)).astype(jnp.int32)
    j_idx = jnp.reshape(chip, (1,)).astype(jnp.int32)
    L = ffn1_norm.shape[0]
    tr = lambda a: jnp.swapaxes(a, 1, 2)

    sh = dict(f1g=tr(ffn1_w_gate), f1u=tr(ffn1_w_up), f1d=ffn1_w_down, f2g=tr(ffn2_w_gate),
              f2u=tr(ffn2_w_up), f2d=ffn2_w_down, win=tr(w_in), wout=w_out)
    sh = {k: [v[l].astype(CDT) for l in range(L)] for k, v in sh.items()}
    W = {}

    def gather_op(keys):
        if keys == ["cw"]:
            return ag_op([conv_w])
        return ag2_op([sh[k[0]][k[1]] for k in keys])

    def take(keys, res):
        for k, a in zip(keys, res):
            if k == "cw":
                W[k] = _own_slab(a, conv_w, chip)
            else:
                mine = sh[k[0]][k[1]]
                W[k] = _own_slab(a, mine.reshape(a.shape[1:]), chip).reshape(N_CHIPS * mine.shape[0], mine.shape[1])

    def with_ag(fn, keys, *args):
        if not keys:
            return fn(*args)
        main, res = fn(*args, comm=gather_op(keys))
        take(keys, res)
        return main

    ag_hosts = {("ffn1", 0): [("win", 0), ("f2g", 0), ("f1d", 1)],
                ("inproj", 0): ["cw"], ("attn", 0): [("wout", 0), ("f2u", 0)], ("conv", 0): [("f2d", 0)],
                ("ffn2", 0): [("f1g", 1), ("f1u", 1)],
                ("ffn1", 1): [("win", 1), ("f2g", 1), ("f2u", 1)],
                ("inproj", 1): [("wout", 1)], ("attn", 1): [("f2d", 1)]}
    first = [("f1g", 0), ("f1u", 0), ("f1d", 0)]
    take(first, first_gather([sh[k_][l] for k_, l in first]))

    cs, sn = _rope_tables(positions)
    saved = []
    h = x[0]
    for l in range(L):
        sink = attn_sinks[l].reshape(2, 4)
        sink_row = jnp.repeat(sink, BLOCK, axis=1).reshape(2, 1, 4 * BLOCK)
        x0 = h
        x1, g1, u1 = with_ag(ffn_fwd, ag_hosts.get(("ffn1", l)), x0, ffn1_norm[l][None],
                             W[("f1g", l)], W[("f1u", l)], W[("f1d", l)])
        wext = _widen_w_in(W[("win", l)])
        q, k, v, u = with_ag(inproj_fwd, ag_hosts.get(("inproj", l)), x1, mix_norm[l][None], wext, cs, sn)
        ao = with_ag(attn_fwd, ag_hosts.get(("attn", l)), q, k, v, sink_row)
        cwl = jnp.transpose(W["cw"][:, l], (1, 0, 2)).reshape(CONV_K, CONV_C)
        co, yc = with_ag(conv_fwd, ag_hosts.get(("conv", l)), u, cwl, conv_b[l][None], conv_ln_g[l][None],
                         conv_ln_b[l][None])
        x2 = outproj_fwd(x1, ao, co, W[("wout", l)])
        x3, g2, u2 = with_ag(ffn_fwd, ag_hosts.get(("ffn2", l)), x2, ffn2_norm[l][None],
                             W[("f2g", l)], W[("f2u", l)], W[("f2d", l)])
        saved.append((x0, x1, x2, g1, u1, g2, u2, q, k, v, u, ao, co, yc, sink, wext, cwl))
        h = x3

    loss, dx, dnf = loss_head(h, final_norm[None], loss_target[0])

    active = []

    def advance(run):
        stages = [ch.next_stage() for ch in active]
        ops = [op for op, _ in stages]
        main, res = run(_merge(*ops) if ops else None)
        for (_, cb), r in zip(stages, _split(res, *ops)):
            cb(r)
        active[:] = [ch for ch in active if not ch.done]
        return main

    def hosted(fn, *args):
        def run(comm):
            if comm is None:
                return fn(*args), []
            return fn(*args, comm=comm)
        return advance(run)

    def chain(key, grads):
        chains[key] = _Chain(grads, c_idx, j_idx)
        active.append(chains[key])

    small = {k_: [None] * L for k_ in ("n1", "nm", "n2", "cw", "cb", "lg", "lb", "sinks")}
    chains = {}
    for l in reversed(range(L)):
        x0, x1, x2, g1, u1, g2, u2, q, k, v, u, ao, co, yc, sink, wext, cwl = saved[l]
        sink_row = jnp.repeat(sink, BLOCK, axis=1).reshape(2, 1, 4 * BLOCK)
        chain(("f2d", l), [hosted(ffn_wgrad_down, dx, g2, u2)])
        dx2, dgt, dup, hh, small["n2"][l] = hosted(
            ffn_bwd_dgrad, x2, ffn2_norm[l][None], dx, g2, u2, W[("f2g", l)], W[("f2u", l)], W[("f2d", l)])
        chain(("f2gu", l), hosted(ffn_wgrad_gate_up, hh, dgt, dup))
        da, dc, gwout = hosted(outproj_bwd, dx2, ao, co, W[("wout", l)])
        du, small["cw"][l], small["cb"][l], small["lg"][l], small["lb"][l] = hosted(
            conv_bwd, dc, u, yc, cwl, conv_ln_g[l][None], conv_ln_b[l][None])
        dq, dk, dv, dsink = hosted(attn_bwd, q, k, v, da, sink_row)
        small["sinks"][l] = jnp.sum(dsink.reshape(2, 4, BLOCK), axis=-1).reshape(N_HEADS)
        dx1, gwext, small["nm"][l] = inproj_bwd(x1, mix_norm[l][None], dx2, dq, dk, dv, du, wext, cs, sn)
        chain(("mx", l), [gwout, _fold_w_in(gwext)])
        chain(("f1d", l), [hosted(ffn_wgrad_down, dx1, g1, u1)])
        dx, dgt, dup, hh, small["n1"][l] = hosted(
            ffn_bwd_dgrad, x0, ffn1_norm[l][None], dx1, g1, u1, W[("f1g", l)], W[("f1u", l)], W[("f1d", l)])
        chain(("f1gu", l), hosted(ffn_wgrad_gate_up, hh, dgt, dup))

    weights = dict(ffn1_norm=ffn1_norm, ffn1_w_gate=ffn1_w_gate, ffn1_w_up=ffn1_w_up, ffn1_w_down=ffn1_w_down,
                   mix_norm=mix_norm, w_in=w_in, conv_w=conv_w, conv_b=conv_b, conv_ln_g=conv_ln_g,
                   conv_ln_b=conv_ln_b, attn_sinks=attn_sinks, w_out=w_out, ffn2_norm=ffn2_norm,
                   ffn2_w_gate=ffn2_w_gate, ffn2_w_up=ffn2_w_up, ffn2_w_down=ffn2_w_down, final_norm=final_norm)
    moms = dict(ffn1_norm=(m_ffn1_norm, v_ffn1_norm), ffn1_w_gate=(m_ffn1_w_gate, v_ffn1_w_gate),
                ffn1_w_up=(m_ffn1_w_up, v_ffn1_w_up), ffn1_w_down=(m_ffn1_w_down, v_ffn1_w_down),
                mix_norm=(m_mix_norm, v_mix_norm), w_in=(m_w_in, v_w_in), conv_w=(m_conv_w, v_conv_w),
                conv_b=(m_conv_b, v_conv_b), conv_ln_g=(m_conv_ln_g, v_conv_ln_g),
                conv_ln_b=(m_conv_ln_b, v_conv_ln_b), attn_sinks=(m_attn_sinks, v_attn_sinks),
                w_out=(m_w_out, v_w_out), ffn2_norm=(m_ffn2_norm, v_ffn2_norm),
                ffn2_w_gate=(m_ffn2_w_gate, v_ffn2_w_gate), ffn2_w_up=(m_ffn2_w_up, v_ffn2_w_up),
                ffn2_w_down=(m_ffn2_w_down, v_ffn2_w_down), final_norm=(m_final_norm, v_final_norm))
    names = list(weights)
    big_names = dict(ffn2_w_gate=("f2gu", 0, True), ffn2_w_up=("f2gu", 1, True), ffn2_w_down=("f2d", 0, False),
                     w_out=("mx", 0, False), w_in=("mx", 1, True), ffn1_w_gate=("f1gu", 0, True),
                     ffn1_w_up=("f1gu", 1, True), ffn1_w_down=("f1d", 0, False))
    grads, delta, new_m, new_v = {}, {}, {}, {}

    def big_adamw(nme, comm=None):
        group, idx, transposed = big_names[nme]
        view = tr if transposed else (lambda a: a)
        args = (view(weights[nme]), view(moms[nme][0]), view(moms[nme][1]),
                [chains[(group, l)].final[idx] for l in range(L)])
        res, comm_res = (adamw_layers(*args), []) if comm is None else adamw_layers(*args, comm=comm)
        grads[nme], delta[nme], new_m[nme], new_v[nme] = [view(a) for a in res]
        return None, comm_res

    for nme in big_names:
        if not all(chains[(big_names[nme][0], l)].done for l in range(L)):
            break
        advance(functools.partial(big_adamw, nme))
    while active:
        advance(lambda comm: (None, _run_comm(comm)))
    for nme in big_names:
        if nme not in grads:
            big_adamw(nme)

    G = {k_: jnp.stack(v_) for k_, v_ in small.items()}
    G["nf"] = dnf
    small_shapes = [shp for _, shp in _SMALL]
    n_small = 1 + sum(math.prod(s) for s in small_shapes)
    rows = -(-n_small // 1024) * 8
    packed = _pack([loss] + [G[k_] for k_, _ in _SMALL], rows)
    summed = all_reduce_small(packed)
    loss_out, *small_sum = _unpack(summed, [()] + small_shapes)
    gs = dict(zip([k_ for k_, _ in _SMALL], small_sum))
    gs["cw"] = lax.dynamic_slice_in_dim(gs["cw"], chip * (CONV_C // N_CHIPS), CONV_C // N_CHIPS, axis=2)
    grads.update(ffn1_norm=gs["n1"], mix_norm=gs["nm"], conv_w=gs["cw"], conv_b=gs["cb"], conv_ln_g=gs["lg"],
                 conv_ln_b=gs["lb"], attn_sinks=gs["sinks"], ffn2_norm=gs["n2"], final_norm=gs["nf"])

    small_names = [nme for nme in names if nme not in big_names]
    s_shapes = [weights[nme].shape for nme in small_names]
    n_tot = sum(math.prod(s) for s in s_shapes)
    srows = -(-n_tot // 1024) * 8
    d, nm_, nv_ = adamw(_pack([weights[nme] for nme in small_names], srows),
                        _pack([grads[nme] for nme in small_names], srows),
                        _pack([moms[nme][0] for nme in small_names], srows),
                        _pack([moms[nme][1] for nme in small_names], srows))
    for nme, dd, mm, vv in zip(small_names, _unpack(d, s_shapes), _unpack(nm_, s_shapes), _unpack(nv_, s_shapes)):
        delta[nme], new_m[nme], new_v[nme] = dd, mm, vv

    return (loss_out, dx[None], *[grads[nme] for nme in names], *[delta[nme] for nme in names],
            *[new_m[nme] for nme in names], *[new_v[nme] for nme in names])
```

```python
import functools
import math

import jax
import jax.numpy as jnp
from jax import lax
from jax.experimental import pallas as pl
from jax.experimental.pallas import tpu as pltpu

F32 = jnp.float32
CDT = jnp.bfloat16
D_MODEL = 1024
D_FF = 2816
N_HEADS = 8
HEAD_DIM = 64
BLOCK = 128
CONV_K = 31
CONV_C = 512
ATT_W = 512
D_EXT = 2048
EPS = 1e-5
HALO = 32
FF_CHUNK = 256
NEG = float(jnp.finfo(jnp.float32).min)
VMEM_LIMIT = 56 * 1024 * 1024

ADAM_LR = 0.001
ADAM_B1 = 0.9
ADAM_B2 = 0.999
ADAM_EPS = 1e-08
ADAM_WD = 0.01
ADAM_STEP = 10

NT = (((1,), (1,)), ((), ()))
TN = (((0,), (0,)), ((), ()))


MESH = pl.DeviceIdType.MESH
ANY = pl.BlockSpec(memory_space=pl.ANY)
N_CHIPS = 4
N_DEV = 8


def _params(n_axes):
    return pltpu.CompilerParams(dimension_semantics=("arbitrary",) * n_axes, vmem_limit_bytes=VMEM_LIMIT)


class _Comm:
    def __init__(self, name, inputs, out_shape, sems, descs, relay=None):
        self.name, self.inputs, self.out_shape, self.sems = name, list(inputs), list(out_shape), list(sems)
        self.descs, self.relay = descs, relay


def _merge(*ops):
    ops = [o for o in ops if o is not None]
    if len(ops) == 1:
        return ops[0]
    assert all(o.relay is None for o in ops)

    def descs(cins, couts, sems):
        out, i, o, s = [], 0, 0, 0
        for op in ops:
            ni, no, ns = len(op.inputs), len(op.out_shape), len(op.sems)
            out += op.descs(cins[i:i + ni], couts[o:o + no], sems[s:s + ns])
            i, o, s = i + ni, o + no, s + ns
        return out

    return _Comm("_".join(o.name for o in ops), sum((o.inputs for o in ops), []),
                 sum((o.out_shape for o in ops), []), sum((o.sems for o in ops), []), descs)


def _split(couts, *ops):
    res, o = [], 0
    for op in ops:
        res.append(couts[o:o + len(op.out_shape)])
        o += len(op.out_shape)
    return res


def _hosted(comm):
    if comm is None:
        return pl.pallas_call

    def make(body, *, name, grid, in_specs, out_specs, out_shape, compiler_params, scratch_shapes=()):
        single = not isinstance(out_shape, (list, tuple))
        o_specs = [out_specs] if single else list(out_specs)
        o_shape = [out_shape] if single else list(out_shape)
        n_in, n_out, n_sc = len(in_specs), len(o_specs), len(scratch_shapes)
        c_in, c_out = len(comm.inputs), len(comm.out_shape)

        def hosted(*refs):
            ins, cins = refs[:n_in], refs[n_in:n_in + c_in]
            o0 = n_in + c_in
            outs, couts = refs[o0:o0 + n_out], refs[o0 + n_out:o0 + n_out + c_out]
            s0 = o0 + n_out + c_out
            scr, sems = refs[s0:s0 + n_sc], refs[s0 + n_sc:]
            first = pl.program_id(0) == 0
            last = pl.program_id(0) == grid[0] - 1
            for ax in range(1, len(grid)):
                first = first & (pl.program_id(ax) == 0)
                last = last & (pl.program_id(ax) == grid[ax] - 1)

            @pl.when(first)
            def _():
                for d in comm.descs(cins, couts, sems):
                    d.start()

            if comm.relay is not None:
                assert len(grid) == 1

                @pl.when(pl.program_id(0) == (3 * grid[0]) // 4)
                def _():
                    for d in comm.descs(cins, couts, sems):
                        d.wait()
                    for d in comm.relay(cins, couts, sems):
                        d.start()

            body(*ins, *outs, *scr)

            @pl.when(last)
            def _():
                for d in (comm.relay or comm.descs)(cins, couts, sems):
                    d.wait()

        call = pl.pallas_call(
            hosted, name=f"{name}_{comm.name}", grid=grid,
            in_specs=list(in_specs) + [ANY] * c_in, out_specs=o_specs + [ANY] * c_out,
            out_shape=o_shape + comm.out_shape, scratch_shapes=list(scratch_shapes) + comm.sems,
            compiler_params=compiler_params)

        def run(*args):
            res = call(*args, *comm.inputs)
            return (res[0] if single else list(res[:n_out])), list(res[n_out:])

        return run

    return make


def _run_comm(comm):
    c_in = len(comm.inputs)

    def body(*refs):
        cins, couts, sems = refs[:c_in], refs[c_in:c_in + len(comm.out_shape)], refs[c_in + len(comm.out_shape):]
        ds = comm.descs(cins, couts, sems)
        for d in ds:
            d.start()
        for d in ds:
            d.wait()

    return list(pl.pallas_call(
        body, name=comm.name, in_specs=[ANY] * c_in, out_specs=[ANY] * len(comm.out_shape),
        out_shape=comm.out_shape, scratch_shapes=comm.sems)(*comm.inputs))


def _coords():
    return lax.axis_index("x"), lax.axis_index("y"), lax.axis_index("c")


def _other_chips(x, y):
    return [(1 - x, y), (x, 1 - y), (1 - x, 1 - y)]


def ag_op(shards):
    n = len(shards)

    def descs(cins, couts, sems):
        send, recv = sems
        x, y, c = _coords()
        j = 2 * x + y
        ds = []
        for a in range(n):
            for p, (px, py) in enumerate(_other_chips(x, y)):
                ds.append(pltpu.make_async_remote_copy(cins[a], couts[a].at[j], send.at[a, p], recv.at[a, p],
                                                       device_id=(px, py, c), device_id_type=MESH))
        return ds

    return _Comm("ag", shards, [jax.ShapeDtypeStruct((N_CHIPS,) + s.shape, s.dtype) for s in shards],
                 [pltpu.SemaphoreType.DMA((n, 3)), pltpu.SemaphoreType.DMA((n, 3))], descs)


def ag2_op(shards):
    n = len(shards)
    halves = [s.reshape(2, s.shape[0] // 2, s.shape[1]) for s in shards]

    def descs(cins, couts, sems):
        x, y, c = _coords()
        j = 2 * x + y
        return [pltpu.make_async_remote_copy(cins[a].at[c], couts[a].at[j, c], sems[0].at[a, p], sems[1].at[a, p],
                                             device_id=(px, py, c), device_id_type=MESH)
                for a in range(n) for p, (px, py) in enumerate(_other_chips(x, y))]

    def relay(cins, couts, sems):
        x, y, c = _coords()
        return [pltpu.make_async_remote_copy(couts[a].at[2 * px + py, c], couts[a].at[2 * px + py, c],
                                             sems[2].at[a, p], sems[3].at[a, p],
                                             device_id=(x, y, 1 - c), device_id_type=MESH)
                for a in range(n) for p, (px, py) in enumerate(_other_chips(x, y))]

    return _Comm("ag2", halves, [jax.ShapeDtypeStruct((N_CHIPS,) + h.shape, h.dtype) for h in halves],
                 [pltpu.SemaphoreType.DMA((n, 3))] * 4, descs, relay)


def _own_slab(gathered, mine, idx):
    return lax.dynamic_update_slice_in_dim(gathered, mine[None], idx, axis=0)


def first_gather(shards):
    n = len(shards)
    halves = [s.reshape(2, s.shape[0] // 2, s.shape[1]) for s in shards]

    def body(*refs):
        ins, outs = refs[:n], refs[n:2 * n]
        send1, recv1, send2, recv2 = refs[2 * n:]
        x, y, c = _coords()
        j = 2 * x + y
        chips = _other_chips(x, y)
        ici = [pltpu.make_async_remote_copy(ins[a].at[c], outs[a].at[j, c], send1.at[a, p], recv1.at[a, p],
                                            device_id=(px, py, c), device_id_type=MESH)
               for a in range(n) for p, (px, py) in enumerate(chips)]
        for d in ici:
            d.start()
        d2d = [pltpu.make_async_remote_copy(outs[a].at[2 * px + py, c], outs[a].at[2 * px + py, c],
                                            send2.at[a, p], recv2.at[a, p],
                                            device_id=(x, y, 1 - c), device_id_type=MESH)
               for a in range(n) for p, (px, py) in enumerate(chips)]
        for d1, d2 in zip(ici, d2d):
            d1.wait()
            d2.start()
        for d in d2d:
            d.wait()

    return list(pl.pallas_call(
        body, name="first_gather", in_specs=[ANY] * n, out_specs=[ANY] * n,
        out_shape=[jax.ShapeDtypeStruct((N_CHIPS,) + h.shape, h.dtype) for h in halves],
        scratch_shapes=[pltpu.SemaphoreType.DMA((n, 3))] * 4)(*halves))


def swap_op(grads):
    n = len(grads)

    def descs(cins, couts, sems):
        send, recv = sems
        x, y, c = _coords()
        return [pltpu.make_async_remote_copy(cins[a].at[:, 1 - c], couts[a], send.at[a], recv.at[a],
                                             device_id=(x, y, 1 - c), device_id_type=MESH) for a in range(n)]

    return _Comm("swap", grads, [jax.ShapeDtypeStruct(g.shape[:1] + g.shape[2:], g.dtype) for g in grads],
                 [pltpu.SemaphoreType.DMA((n,)), pltpu.SemaphoreType.DMA((n,))], descs)


def exchange_op(parts):
    n = len(parts)

    def descs(cins, couts, sems):
        send, recv = sems
        x, y, c = _coords()
        ds = []
        for a in range(n):
            for p, (px, py) in enumerate(_other_chips(x, y)):
                ds.append(pltpu.make_async_remote_copy(cins[a].at[2 * px + py], couts[a].at[p], send.at[a, p],
                                                       recv.at[a, p], device_id=(px, py, c), device_id_type=MESH))
        return ds

    return _Comm("xchg", parts, [jax.ShapeDtypeStruct((3,) + p.shape[1:], p.dtype) for p in parts],
                 [pltpu.SemaphoreType.DMA((n, 3)), pltpu.SemaphoreType.DMA((n, 3))], descs)


def share_op(totals):
    n = len(totals)

    def descs(cins, couts, sems):
        send, recv = sems
        x, y, c = _coords()
        return [pltpu.make_async_remote_copy(cins[a], couts[a].at[c], send.at[a], recv.at[a],
                                             device_id=(x, y, 1 - c), device_id_type=MESH) for a in range(n)]

    return _Comm("share", totals, [jax.ShapeDtypeStruct((2,) + t.shape, t.dtype) for t in totals],
                 [pltpu.SemaphoreType.DMA((n,)), pltpu.SemaphoreType.DMA((n,))], descs)


def _sigmoid(z):
    return 1.0 / (1.0 + jnp.exp(-z))


def _rms_parts(xf):
    r = lax.rsqrt(jnp.mean(xf * xf, axis=-1, keepdims=True) + EPS)
    return xf * r, r


def _rms_bwd(dh, xhat, r, g):
    dg = jnp.sum(dh * xhat, axis=0, keepdims=True)
    dxhat = dh * g
    dx = r * (dxhat - xhat * jnp.mean(dxhat * xhat, axis=-1, keepdims=True))
    return dx, dg


def _chunks(n, ck=FF_CHUNK):
    return [(c0, min(ck, n - c0)) for c0 in range(0, n, ck)]


def ffn_fwd(x, g, wgT, wuT, wd, *, tm=512, comm=None):
    S, D = x.shape
    F = wgT.shape[0]

    def body(x_ref, g_ref, wg_ref, wu_ref, wd_ref, o_ref, gate_ref, up_ref, a_sc):
        xf = x_ref[...]
        xhat, _ = _rms_parts(xf)
        h = (xhat * g_ref[...]).astype(CDT)
        for c0, cw_ in _chunks(F):
            sl = slice(c0, c0 + cw_)
            gt = lax.dot_general(h, wg_ref[sl, :], NT, preferred_element_type=F32)
            ut = lax.dot_general(h, wu_ref[sl, :], NT, preferred_element_type=F32)
            gate_ref[:, sl] = gt.astype(CDT)
            up_ref[:, sl] = ut.astype(CDT)
            a_sc[:, sl] = (gt * _sigmoid(gt) * ut).astype(CDT)
        o_ref[...] = xf + 0.5 * jnp.dot(a_sc[...], wd_ref[...], preferred_element_type=F32)

    wspec = pl.BlockSpec((F, D), lambda i: (0, 0), pipeline_mode=pl.Buffered(1))
    return _hosted(comm)(
        body, name="ffn_fwd",
        grid=(S // tm,),
        in_specs=[pl.BlockSpec((tm, D), lambda i: (i, 0)), pl.BlockSpec((1, D), lambda i: (0, 0)),
                  wspec, wspec, wspec],
        out_specs=[pl.BlockSpec((tm, D), lambda i: (i, 0)),
                   pl.BlockSpec((tm, F), lambda i: (i, 0)),
                   pl.BlockSpec((tm, F), lambda i: (i, 0))],
        out_shape=[jax.ShapeDtypeStruct((S, D), F32),
                   jax.ShapeDtypeStruct((S, F), CDT),
                   jax.ShapeDtypeStruct((S, F), CDT)],
        scratch_shapes=[pltpu.VMEM((tm, F), CDT)],
        compiler_params=_params(1),
    )(x, g, wgT, wuT, wd)


def ffn_bwd_dgrad(x, g, dy, gate, up, wgT, wuT, wd, *, tm=256, comm=None):
    S, D = x.shape
    F = wgT.shape[0]

    def body(x_ref, g_ref, dy_ref, gate_ref, up_ref, wg_ref, wu_ref, wd_ref,
             dx_ref, dgate_ref, dup_ref, h_ref, dg_ref):
        @pl.when(pl.program_id(0) == 0)
        def _():
            dg_ref[...] = jnp.zeros_like(dg_ref)

        dyf = dy_ref[...]
        dacc = (0.5 * dyf).astype(CDT)
        gg = g_ref[...]
        xhat, r = _rms_parts(x_ref[...])
        h_ref[...] = (xhat * gg).astype(CDT)
        for c0, cw_ in _chunks(F):
            sl = slice(c0, c0 + cw_)
            d_a = lax.dot_general(dacc, wd_ref[sl, :], NT, preferred_element_type=F32)
            gt = gate_ref[:, sl].astype(F32)
            ut = up_ref[:, sl].astype(F32)
            sg = _sigmoid(gt)
            dup_ref[:, sl] = (d_a * (gt * sg)).astype(CDT)
            dgate_ref[:, sl] = (d_a * ut * (sg * (1.0 + gt * (1.0 - sg)))).astype(CDT)
        dh = (jnp.dot(dgate_ref[...], wg_ref[...], preferred_element_type=F32)
              + jnp.dot(dup_ref[...], wu_ref[...], preferred_element_type=F32))
        dx, dg = _rms_bwd(dh, xhat, r, gg)
        dx_ref[...] = dyf + dx
        dg_ref[...] += dg

    wspec = pl.BlockSpec((F, D), lambda i: (0, 0), pipeline_mode=pl.Buffered(1))
    row = pl.BlockSpec((tm, D), lambda i: (i, 0))
    act = pl.BlockSpec((tm, F), lambda i: (i, 0))
    vec = pl.BlockSpec((1, D), lambda i: (0, 0))
    return _hosted(comm)(
        body, name="ffn_bwd_dgrad",
        grid=(S // tm,),
        in_specs=[row, vec, row, act, act, wspec, wspec, wspec],
        out_specs=[row, act, act, row, vec],
        out_shape=[jax.ShapeDtypeStruct((S, D), F32),
                   jax.ShapeDtypeStruct((S, F), CDT),
                   jax.ShapeDtypeStruct((S, F), CDT),
                   jax.ShapeDtypeStruct((S, D), CDT),
                   jax.ShapeDtypeStruct((1, D), F32)],
        compiler_params=_params(1),
    )(x, g, dy, gate, up, wgT, wuT, wd)


def ffn_wgrad_down(dy, gate, up, *, tk=1024, comm=None):
    S, D = dy.shape
    F = gate.shape[1]
    tk = min(tk, S)

    def body(dy_ref, gate_ref, up_ref, dwd_ref):
        @pl.when(pl.program_id(0) == 0)
        def _():
            dwd_ref[...] = jnp.zeros_like(dwd_ref)

        dacc = (0.5 * dy_ref[...]).astype(CDT)
        for c0, cw_ in _chunks(F):
            sl = slice(c0, c0 + cw_)
            gt = gate_ref[:, sl].astype(F32)
            a = (gt * _sigmoid(gt) * up_ref[:, sl].astype(F32)).astype(CDT)
            dwd_ref[sl, :] += lax.dot_general(a, dacc, TN, preferred_element_type=F32)

    act = pl.BlockSpec((tk, F), lambda k: (k, 0))
    return _hosted(comm)(
        body, name="ffn_wgrad_down",
        grid=(S // tk,),
        in_specs=[pl.BlockSpec((tk, D), lambda k: (k, 0)), act, act],
        out_specs=pl.BlockSpec((F, D), lambda k: (0, 0), pipeline_mode=pl.Buffered(1)),
        out_shape=jax.ShapeDtypeStruct((F, D), F32),
        compiler_params=_params(1),
    )(dy, gate, up)


def ffn_wgrad_rows(h, acts, *, tk=1024, comm=None):
    S, D = h.shape
    F = acts[0].shape[1]
    n = len(acts)
    tk = min(tk, S)

    def body(h_ref, *refs):
        act_refs, dw_refs = refs[:n], refs[n:]

        @pl.when(pl.program_id(0) == 0)
        def _():
            for dw_ref in dw_refs:
                dw_ref[...] = jnp.zeros_like(dw_ref)

        hh = h_ref[...]
        for c0, cw_ in _chunks(F):
            sl = slice(c0, c0 + cw_)
            for act_ref, dw_ref in zip(act_refs, dw_refs):
                dw_ref[sl, :] += lax.dot_general(act_ref[:, sl], hh, TN, preferred_element_type=F32)

    act = pl.BlockSpec((tk, F), lambda k: (k, 0))
    out = pl.BlockSpec((F, D), lambda k: (0, 0), pipeline_mode=pl.Buffered(1))
    return _hosted(comm)(
        body, name="ffn_wgrad_rows",
        grid=(S // tk,),
        in_specs=[pl.BlockSpec((tk, D), lambda k: (k, 0))] + [act] * n,
        out_specs=[out] * n,
        out_shape=[jax.ShapeDtypeStruct((F, D), F32)] * n,
        compiler_params=_params(1),
    )(h, *acts)


def loss_head(x, g, target, *, tm=512):
    S, D = x.shape

    def body(x_ref, g_ref, t_ref, loss_ref, dx_ref, dg_ref):
        @pl.when(pl.program_id(0) == 0)
        def _():
            loss_ref[...] = jnp.zeros_like(loss_ref)
            dg_ref[...] = jnp.zeros_like(dg_ref)

        xhat, r = _rms_parts(x_ref[...])
        gg = g_ref[...]
        err = xhat * gg - t_ref[...]
        loss_ref[...] += 0.5 * jnp.sum(jnp.mean(err * err, axis=-1, keepdims=True), axis=0, keepdims=True)
        dx, dg = _rms_bwd(err * (1.0 / D), xhat, r, gg)
        dx_ref[...] = dx
        dg_ref[...] += dg

    row = pl.BlockSpec((tm, D), lambda i: (i, 0))
    vec = pl.BlockSpec((1, D), lambda i: (0, 0))
    return pl.pallas_call(
        body, name="loss_head",
        grid=(S // tm,),
        in_specs=[row, vec, row],
        out_specs=[pl.BlockSpec((1, 1), lambda i: (0, 0)), row, vec],
        out_shape=[jax.ShapeDtypeStruct((1, 1), F32), jax.ShapeDtypeStruct((S, D), F32),
                   jax.ShapeDtypeStruct((1, D), F32)],
        compiler_params=_params(1),
    )(x, g, target)


def _rope_apply(t, cs, sn):
    lane = lax.broadcasted_iota(jnp.int32, t.shape, 1)
    first = (lane % HEAD_DIM) < (HEAD_DIM // 2)
    rot = jnp.where(first, pltpu.roll(t, 128 - HEAD_DIM // 2, 1), pltpu.roll(t, HEAD_DIM // 2, 1))
    return t * cs + rot * sn


def _rope_transpose(d, cs, sn):
    lane = lax.broadcasted_iota(jnp.int32, d.shape, 1)
    first = (lane % HEAD_DIM) < (HEAD_DIM // 2)
    ds = d * sn
    rot = jnp.where(first, pltpu.roll(ds, 128 - HEAD_DIM // 2, 1), pltpu.roll(ds, HEAD_DIM // 2, 1))
    return d * cs + rot


def inproj_fwd(x, g, wextT, cs, sn, *, tm=512, comm=None):
    S, D = x.shape
    scale = HEAD_DIM ** -0.5

    def body(x_ref, g_ref, w_ref, cs_ref, sn_ref, q_ref, k_ref, v_ref, u_ref):
        xhat, _ = _rms_parts(x_ref[...])
        h = (xhat * g_ref[...]).astype(CDT)
        p = lax.dot_general(h, w_ref[...], NT, preferred_element_type=F32)
        c, s = cs_ref[...], sn_ref[...]
        for b in range(4):
            q_ref[:, 128 * b:128 * (b + 1)] = (_rope_apply(p[:, 128 * b:128 * (b + 1)], c, s) * scale).astype(CDT)
        for b in range(2):
            k_ref[:, 128 * b:128 * (b + 1)] = _rope_apply(p[:, 512 + 128 * b:512 + 128 * (b + 1)], c, s).astype(CDT)
        v_ref[...] = p[:, 768:1024].astype(CDT)
        u_ref[...] = p[:, 1024:2048]

    def row(w):
        return pl.BlockSpec((tm, w), lambda i: (i, 0))

    return _hosted(comm)(
        body, name="inproj_fwd",
        grid=(S // tm,),
        in_specs=[row(D), pl.BlockSpec((1, D), lambda i: (0, 0)),
                  pl.BlockSpec((D_EXT, D), lambda i: (0, 0)), row(128), row(128)],
        out_specs=[row(512), row(256), row(256), row(1024)],
        out_shape=[jax.ShapeDtypeStruct((S, 512), CDT), jax.ShapeDtypeStruct((S, 256), CDT),
                   jax.ShapeDtypeStruct((S, 256), CDT), jax.ShapeDtypeStruct((S, 1024), F32)],
        compiler_params=_params(1),
    )(x, g, wextT, cs, sn)


def _stack_heads(p0, p1):
    lane = lax.broadcasted_iota(jnp.int32, p0.shape, 1)
    lo = lane < HEAD_DIM
    z = jnp.zeros_like(p0)
    return jnp.concatenate([jnp.where(lo, p0, z), jnp.where(lo, z, p0),
                            jnp.where(lo, p1, z), jnp.where(lo, z, p1)], axis=0)


def _unstack_heads(o):
    lane = lax.broadcasted_iota(jnp.int32, (BLOCK, 128), 1)
    lo = lane < HEAD_DIM
    return (jnp.where(lo, o[0:128], o[128:256]), jnp.where(lo, o[256:384], o[384:512]))


def _band_mask_kq(n):
    c = lax.broadcasted_iota(jnp.int32, (2 * BLOCK, 4 * BLOCK), 0)
    i = lax.broadcasted_iota(jnp.int32, (2 * BLOCK, 4 * BLOCK), 1) % BLOCK
    return (c > i) & (c <= i + BLOCK) & ((n > 0) | (c >= BLOCK))


def attn_fwd(q, k, v, sink_row, *, nb=4, comm=None):
    S = q.shape[0]
    tq = nb * BLOCK

    def body(q_ref, k_ref, v_ref, sink_ref, o_ref):
        t = pl.program_id(0)
        for b in range(nb):
            n = t * nb + b
            prev = pl.multiple_of(jnp.maximum(n - 1, 0) * BLOCK, BLOCK)
            cur = pl.multiple_of(n * BLOCK, BLOCK)
            rows = slice(b * BLOCK, (b + 1) * BLOCK)
            mask = _band_mask_kq(n)
            for gidx in range(2):
                lanes = slice(128 * gidx, 128 * (gidx + 1))
                qs = _stack_heads(q_ref[rows, 256 * gidx:256 * gidx + 128],
                                  q_ref[rows, 256 * gidx + 128:256 * gidx + 256])
                kb = jnp.concatenate([k_ref[pl.ds(prev, BLOCK), lanes], k_ref[pl.ds(cur, BLOCK), lanes]], axis=0)
                vb = jnp.concatenate([v_ref[pl.ds(prev, BLOCK), lanes], v_ref[pl.ds(cur, BLOCK), lanes]], axis=0)
                st = lax.dot_general(kb, qs, NT, preferred_element_type=F32)
                st = jnp.where(mask, st, NEG)
                sink = sink_ref[gidx]
                m = jnp.maximum(jnp.max(st, axis=0, keepdims=True), sink)
                e = jnp.exp(st - m)
                inv = 1.0 / (jnp.sum(e, axis=0, keepdims=True) + jnp.exp(sink - m))
                o = lax.dot_general((e * inv).astype(CDT), vb, TN, preferred_element_type=F32)
                o0, o1 = _unstack_heads(o)
                o_ref[rows, 256 * gidx:256 * gidx + 128] = o0.astype(CDT)
                o_ref[rows, 256 * gidx + 128:256 * gidx + 256] = o1.astype(CDT)

    return _hosted(comm)(
        body, name="attn_fwd",
        grid=(S // tq,),
        in_specs=[pl.BlockSpec((tq, 512), lambda t: (t, 0)),
                  pl.BlockSpec((S, 256), lambda t: (0, 0)),
                  pl.BlockSpec((S, 256), lambda t: (0, 0)),
                  pl.BlockSpec((2, 1, 4 * BLOCK), lambda t: (0, 0, 0))],
        out_specs=pl.BlockSpec((tq, 512), lambda t: (t, 0)),
        out_shape=jax.ShapeDtypeStruct((S, 512), CDT),
        compiler_params=_params(1),
    )(q, k, v, sink_row)


def attn_bwd(q, k, v, do, sink_row, *, nb=4, comm=None):
    S = q.shape[0]
    tq = nb * BLOCK
    scale = HEAD_DIM ** -0.5

    def body(q_ref, k_ref, v_ref, do_ref, sink_ref, dq_ref, dk_ref, dv_ref, dsink_ref):
        t = pl.program_id(0)

        @pl.when(t == 0)
        def _():
            dk_ref[...] = jnp.zeros_like(dk_ref)
            dv_ref[...] = jnp.zeros_like(dv_ref)
            dsink_ref[...] = jnp.zeros_like(dsink_ref)

        for b in range(nb):
            n = t * nb + b
            prev = pl.multiple_of(jnp.maximum(n - 1, 0) * BLOCK, BLOCK)
            cur = pl.multiple_of(n * BLOCK, BLOCK)
            rows = slice(b * BLOCK, (b + 1) * BLOCK)
            mask = _band_mask_kq(n)
            for gidx in range(2):
                lanes = slice(128 * gidx, 128 * (gidx + 1))
                qs = _stack_heads(q_ref[rows, 256 * gidx:256 * gidx + 128],
                                  q_ref[rows, 256 * gidx + 128:256 * gidx + 256])
                dos = _stack_heads(do_ref[rows, 256 * gidx:256 * gidx + 128],
                                   do_ref[rows, 256 * gidx + 128:256 * gidx + 256])
                kb = jnp.concatenate([k_ref[pl.ds(prev, BLOCK), lanes], k_ref[pl.ds(cur, BLOCK), lanes]], axis=0)
                vb = jnp.concatenate([v_ref[pl.ds(prev, BLOCK), lanes], v_ref[pl.ds(cur, BLOCK), lanes]], axis=0)
                st = lax.dot_general(kb, qs, NT, preferred_element_type=F32)
                st = jnp.where(mask, st, NEG)
                sink = sink_ref[gidx]
                m = jnp.maximum(jnp.max(st, axis=0, keepdims=True), sink)
                e = jnp.exp(st - m)
                es = jnp.exp(sink - m)
                inv = 1.0 / (jnp.sum(e, axis=0, keepdims=True) + es)
                pt = e * inv
                dpt = lax.dot_general(vb, dos, NT, preferred_element_type=F32)
                delta = jnp.sum(pt * dpt, axis=0, keepdims=True)
                dst = (pt * (dpt - delta)).astype(CDT)
                dsink_ref[gidx] += -(es * inv) * delta
                dvb = jnp.dot(pt.astype(CDT), dos, preferred_element_type=F32)
                dkb = jnp.dot(dst, qs, preferred_element_type=F32)
                dqs = lax.dot_general(dst, kb, TN, preferred_element_type=F32) * scale
                dq0, dq1 = _unstack_heads(dqs)
                dq_ref[rows, 256 * gidx:256 * gidx + 128] = dq0
                dq_ref[rows, 256 * gidx + 128:256 * gidx + 256] = dq1
                dk_ref[pl.ds(prev, BLOCK), lanes] += dkb[0:BLOCK]
                dk_ref[pl.ds(cur, BLOCK), lanes] += dkb[BLOCK:2 * BLOCK]
                dv_ref[pl.ds(prev, BLOCK), lanes] += dvb[0:BLOCK]
                dv_ref[pl.ds(cur, BLOCK), lanes] += dvb[BLOCK:2 * BLOCK]

    full = pl.BlockSpec((S, 256), lambda t: (0, 0))
    tile = pl.BlockSpec((tq, 512), lambda t: (t, 0))
    srow = pl.BlockSpec((2, 1, 4 * BLOCK), lambda t: (0, 0, 0))
    return _hosted(comm)(
        body, name="attn_bwd",
        grid=(S // tq,),
        in_specs=[tile, full, full, tile, srow],
        out_specs=[tile, full, full, srow],
        out_shape=[jax.ShapeDtypeStruct((S, 512), F32), jax.ShapeDtypeStruct((S, 256), F32),
                   jax.ShapeDtypeStruct((S, 256), F32), jax.ShapeDtypeStruct((2, 1, 4 * BLOCK), F32)],
        compiler_params=_params(1),
    )(q, k, v, do, sink_row)


def _glu(u):
    a = u[:, 0:CONV_C]
    gt = u[:, CONV_C:2 * CONV_C]
    sg = _sigmoid(gt)
    return a, sg, a * sg


CONV_CHUNK = 32


def _shifted_copies(buf, shifted, n):
    for r in range(1, 8):
        shifted[r - 1, 0:n, :] = buf[r:r + n, :]


def _shifted_rows(buf, shifted, start, rows):
    r = start % 8
    if r == 0:
        return buf[start:start + rows, :]
    return shifted[r - 1, start - r:start - r + rows, :]


def conv_fwd(u, cw, cb, lg, lb, *, tm=512, comm=None):
    S = u.shape[0]
    nh = tm // HALO

    def body(u_ref, uh_ref, cw_ref, cb_ref, lg_ref, lb_ref, o_ref, y_ref, hbuf, hsh):
        t = pl.program_id(0)
        _, _, hg = _glu(u_ref[...])
        _, _, hh = _glu(uh_ref[...])
        hbuf[0:HALO, :] = jnp.where(t > 0, hh, jnp.zeros_like(hh))
        hbuf[HALO:HALO + tm, :] = hg
        hbuf[HALO + tm:HALO + tm + 8, :] = jnp.zeros((8, CONV_C), F32)
        _shifted_copies(hbuf, hsh, HALO + tm)
        off = HALO - (CONV_K - 1)
        for c0 in range(0, tm, CONV_CHUNK):
            acc = jnp.zeros((CONV_CHUNK, CONV_C), F32) + cb_ref[...]
            for j in range(CONV_K):
                acc = acc + cw_ref[j:j + 1, :] * _shifted_rows(hbuf, hsh, c0 + off + j, CONV_CHUNK)
            y_ref[c0:c0 + CONV_CHUNK, :] = acc
        y = y_ref[...]
        yc = y - jnp.mean(y, axis=-1, keepdims=True)
        r = lax.rsqrt(jnp.mean(yc * yc, axis=-1, keepdims=True) + EPS)
        z = yc * r * lg_ref[...] + lb_ref[...]
        o_ref[...] = (z * _sigmoid(z)).astype(CDT)

    vec = pl.BlockSpec((1, CONV_C), lambda t: (0, 0))
    return _hosted(comm)(
        body, name="conv_fwd",
        grid=(S // tm,),
        in_specs=[pl.BlockSpec((tm, 2 * CONV_C), lambda t: (t, 0)),
                  pl.BlockSpec((HALO, 2 * CONV_C), lambda t: (jnp.maximum(t * nh - 1, 0), 0)),
                  pl.BlockSpec((CONV_K, CONV_C), lambda t: (0, 0)), vec, vec, vec],
        out_specs=[pl.BlockSpec((tm, CONV_C), lambda t: (t, 0)), pl.BlockSpec((tm, CONV_C), lambda t: (t, 0))],
        out_shape=[jax.ShapeDtypeStruct((S, CONV_C), CDT), jax.ShapeDtypeStruct((S, CONV_C), F32)],
        scratch_shapes=[pltpu.VMEM((HALO + tm + 8, CONV_C), F32), pltpu.VMEM((7, HALO + tm, CONV_C), F32)],
        compiler_params=_params(1),
    )(u, u, cw, cb, lg, lb)


def conv_bwd(dc, u, y, cw, lg, lb, *, tm=512, comm=None):
    S = u.shape[0]
    nh = tm // HALO
    nt = S // tm

    def ln_bwd(dcv, yv, lgv, lbv):
        yc = yv - jnp.mean(yv, axis=-1, keepdims=True)
        r = lax.rsqrt(jnp.mean(yc * yc, axis=-1, keepdims=True) + EPS)
        yhat = yc * r
        z = yhat * lgv + lbv
        sg = _sigmoid(z)
        dz = dcv * (sg * (1.0 + z * (1.0 - sg)))
        dyhat = dz * lgv
        dy = r * (dyhat - jnp.mean(dyhat, axis=-1, keepdims=True)
                  - yhat * jnp.mean(dyhat * yhat, axis=-1, keepdims=True))
        return dy, dz, yhat

    def body(dc_ref, dcn_ref, u_ref, uh_ref, y_ref, yn_ref, cw_ref, lg_ref, lb_ref,
             du_ref, dcw_ref, dcb_ref, dlg_ref, dlb_ref, hbuf, dybuf, dhg_sc, dw_sc, hsh, dysh):
        t = pl.program_id(0)

        @pl.when(t == 0)
        def _():
            dw_sc[...] = jnp.zeros_like(dw_sc)
            dcb_ref[...] = jnp.zeros_like(dcb_ref)
            dlg_ref[...] = jnp.zeros_like(dlg_ref)
            dlb_ref[...] = jnp.zeros_like(dlb_ref)

        lgv, lbv = lg_ref[...], lb_ref[...]
        dy, dz, yhat = ln_bwd(dc_ref[...].astype(F32), y_ref[...], lgv, lbv)
        dyn, _, _ = ln_bwd(dcn_ref[...].astype(F32), yn_ref[...], lgv, lbv)
        dlb_ref[...] += jnp.sum(dz, axis=0, keepdims=True)
        dlg_ref[...] += jnp.sum(dz * yhat, axis=0, keepdims=True)
        dcb_ref[...] += jnp.sum(dy, axis=0, keepdims=True)
        dybuf[0:tm, :] = dy
        dybuf[tm:tm + HALO, :] = jnp.where(t < nt - 1, dyn, jnp.zeros_like(dyn))
        dybuf[tm + HALO:tm + HALO + 8, :] = jnp.zeros((8, CONV_C), F32)
        _shifted_copies(dybuf, dysh, tm + HALO)

        a, sg, hg = _glu(u_ref[...])
        _, _, hh = _glu(uh_ref[...])
        hbuf[0:HALO, :] = jnp.where(t > 0, hh, jnp.zeros_like(hh))
        hbuf[HALO:HALO + tm, :] = hg
        hbuf[HALO + tm:HALO + tm + 8, :] = jnp.zeros((8, CONV_C), F32)
        _shifted_copies(hbuf, hsh, HALO + tm)

        off = HALO - (CONV_K - 1)
        for c0 in range(0, tm, CONV_CHUNK):
            acc = jnp.zeros((CONV_CHUNK, CONV_C), F32)
            dyc = dybuf[c0:c0 + CONV_CHUNK, :]
            for j in range(CONV_K):
                acc = acc + cw_ref[j:j + 1, :] * _shifted_rows(dybuf, dysh, c0 + (CONV_K - 1) - j, CONV_CHUNK)
                prod = dyc * _shifted_rows(hbuf, hsh, c0 + off + j, CONV_CHUNK)
                dw_sc[j] += jnp.sum(prod.reshape(CONV_CHUNK // 8, 8, CONV_C), axis=0)
            dhg_sc[c0:c0 + CONV_CHUNK, :] = acc

        dhg = dhg_sc[...]
        du_ref[:, 0:CONV_C] = dhg * sg
        du_ref[:, CONV_C:2 * CONV_C] = dhg * a * sg * (1.0 - sg)

        @pl.when(t == nt - 1)
        def _():
            dcw_ref[...] = jnp.sum(dw_sc[...], axis=1)

    vec = pl.BlockSpec((1, CONV_C), lambda t: (0, 0))
    tile = pl.BlockSpec((tm, CONV_C), lambda t: (t, 0))
    nxt = pl.BlockSpec((HALO, CONV_C), lambda t: (jnp.minimum((t + 1) * nh, S // HALO - 1), 0))
    return _hosted(comm)(
        body, name="conv_bwd",
        grid=(nt,),
        in_specs=[tile, nxt,
                  pl.BlockSpec((tm, 2 * CONV_C), lambda t: (t, 0)),
                  pl.BlockSpec((HALO, 2 * CONV_C), lambda t: (jnp.maximum(t * nh - 1, 0), 0)),
                  tile, nxt,
                  pl.BlockSpec((CONV_K, CONV_C), lambda t: (0, 0)), vec, vec],
        out_specs=[pl.BlockSpec((tm, 2 * CONV_C), lambda t: (t, 0)),
                   pl.BlockSpec((CONV_K, CONV_C), lambda t: (0, 0)), vec, vec, vec],
        out_shape=[jax.ShapeDtypeStruct((S, 2 * CONV_C), F32), jax.ShapeDtypeStruct((CONV_K, CONV_C), F32),
                   jax.ShapeDtypeStruct((1, CONV_C), F32), jax.ShapeDtypeStruct((1, CONV_C), F32),
                   jax.ShapeDtypeStruct((1, CONV_C), F32)],
        scratch_shapes=[pltpu.VMEM((HALO + tm + 8, CONV_C), F32), pltpu.VMEM((tm + HALO + 8, CONV_C), F32),
                        pltpu.VMEM((tm, CONV_C), F32), pltpu.VMEM((CONV_K, 8, CONV_C), F32),
                        pltpu.VMEM((7, HALO + tm, CONV_C), F32), pltpu.VMEM((7, tm + HALO, CONV_C), F32)],
        compiler_params=_params(1),
    )(dc, dc, u, u, y, y, cw, lg, lb)


def outproj_fwd(x, ao, co, wout, *, tm=512):
    S, D = x.shape

    def body(x_ref, a_ref, c_ref, w_ref, o_ref):
        o_ref[...] = (x_ref[...]
                      + jnp.dot(a_ref[...], w_ref[0:ATT_W, :], preferred_element_type=F32)
                      + jnp.dot(c_ref[...], w_ref[ATT_W:ATT_W + CONV_C, :], preferred_element_type=F32))

    return pl.pallas_call(
        body, name="outproj_fwd",
        grid=(S // tm,),
        in_specs=[pl.BlockSpec((tm, D), lambda i: (i, 0)), pl.BlockSpec((tm, ATT_W), lambda i: (i, 0)),
                  pl.BlockSpec((tm, CONV_C), lambda i: (i, 0)), pl.BlockSpec((D, D), lambda i: (0, 0))],
        out_specs=pl.BlockSpec((tm, D), lambda i: (i, 0)),
        out_shape=jax.ShapeDtypeStruct((S, D), F32),
        compiler_params=_params(1),
    )(x, ao, co, wout)


def outproj_bwd(dx, ao, co, wout, *, tm=512, comm=None):
    S, D = dx.shape

    def body(dx_ref, a_ref, c_ref, w_ref, da_ref, dc_ref, dw_ref):
        @pl.when(pl.program_id(0) == 0)
        def _():
            dw_ref[...] = jnp.zeros_like(dw_ref)

        dxb = dx_ref[...].astype(CDT)
        da_ref[...] = lax.dot_general(dxb, w_ref[0:ATT_W, :], NT, preferred_element_type=F32).astype(CDT)
        dc_ref[...] = lax.dot_general(dxb, w_ref[ATT_W:ATT_W + CONV_C, :], NT, preferred_element_type=F32)
        dw_ref[0:ATT_W, :] += lax.dot_general(a_ref[...], dxb, TN, preferred_element_type=F32)
        dw_ref[ATT_W:ATT_W + CONV_C, :] += lax.dot_general(c_ref[...], dxb, TN, preferred_element_type=F32)

    return _hosted(comm)(
        body, name="outproj_bwd",
        grid=(S // tm,),
        in_specs=[pl.BlockSpec((tm, D), lambda i: (i, 0)), pl.BlockSpec((tm, ATT_W), lambda i: (i, 0)),
                  pl.BlockSpec((tm, CONV_C), lambda i: (i, 0)), pl.BlockSpec((D, D), lambda i: (0, 0))],
        out_specs=[pl.BlockSpec((tm, ATT_W), lambda i: (i, 0)), pl.BlockSpec((tm, CONV_C), lambda i: (i, 0)),
                   pl.BlockSpec((D, D), lambda i: (0, 0))],
        out_shape=[jax.ShapeDtypeStruct((S, ATT_W), CDT), jax.ShapeDtypeStruct((S, CONV_C), F32),
                   jax.ShapeDtypeStruct((D, D), F32)],
        compiler_params=_params(1),
    )(dx, ao, co, wout)


def inproj_bwd(x, g, dres, dq, dk, dv, du, wextT, cs, sn, *, tm=512):
    S, D = x.shape

    def body(x_ref, g_ref, dres_ref, dq_ref, dk_ref, dv_ref, du_ref, w_ref, cs_ref, sn_ref,
             dx_ref, dw_ref, dg_ref, dp_sc, h_sc):
        @pl.when(pl.program_id(0) == 0)
        def _():
            dw_ref[...] = jnp.zeros_like(dw_ref)
            dg_ref[...] = jnp.zeros_like(dg_ref)

        gg = g_ref[...]
        w = w_ref[...]
        halves = [slice(0, tm // 2), slice(tm // 2, tm)]
        parts = []
        for rows in halves:
            c, s = cs_ref[rows, :], sn_ref[rows, :]
            for b in range(4):
                dp_sc[rows, 128 * b:128 * (b + 1)] = _rope_transpose(
                    dq_ref[rows, 128 * b:128 * (b + 1)], c, s).astype(CDT)
            for b in range(2):
                dp_sc[rows, 512 + 128 * b:512 + 128 * (b + 1)] = _rope_transpose(
                    dk_ref[rows, 128 * b:128 * (b + 1)], c, s).astype(CDT)
            dp_sc[rows, 768:1024] = dv_ref[rows, :].astype(CDT)
            dp_sc[rows, 1024:2048] = du_ref[rows, :].astype(CDT)
            xhat, r = _rms_parts(x_ref[rows, :])
            h_sc[rows, :] = (xhat * gg).astype(CDT)
            dh = jnp.dot(dp_sc[rows, :], w, preferred_element_type=F32)
            parts.append((rows, dh, xhat, r))
        for k, (rows, dh, xhat, r) in enumerate(parts):
            dx, dg = _rms_bwd(dh, xhat, r, gg)
            dx_ref[rows, :] = dres_ref[rows, :] + dx
            dg_ref[...] += dg
            if k == 0:
                dw_ref[...] += lax.dot_general(dp_sc[...], h_sc[...], TN, preferred_element_type=F32)

    def row(w):
        return pl.BlockSpec((tm, w), lambda i: (i, 0))

    return pl.pallas_call(
        body, name="inproj_bwd",
        grid=(S // tm,),
        in_specs=[row(D), pl.BlockSpec((1, D), lambda i: (0, 0)), row(D), row(512), row(256), row(256),
                  row(1024), pl.BlockSpec((D_EXT, D), lambda i: (0, 0), pipeline_mode=pl.Buffered(1)),
                  row(128), row(128)],
        out_specs=[row(D), pl.BlockSpec((D_EXT, D), lambda i: (0, 0), pipeline_mode=pl.Buffered(1)),
                   pl.BlockSpec((1, D), lambda i: (0, 0))],
        out_shape=[jax.ShapeDtypeStruct((S, D), F32), jax.ShapeDtypeStruct((D_EXT, D), F32),
                   jax.ShapeDtypeStruct((1, D), F32)],
        scratch_shapes=[pltpu.VMEM((tm, D_EXT), CDT), pltpu.VMEM((tm, D), CDT)],
        compiler_params=_params(1),
    )(x, g, dres, dq, dk, dv, du, wextT, cs, sn)


def _rope_tables(positions):
    inv_freq = 1.0 / (10000.0 ** (jnp.arange(0, HEAD_DIM, 2, dtype=F32) / HEAD_DIM))
    ang = positions.astype(F32).reshape(-1, 1) * inv_freq
    cos, sin = jnp.cos(ang), jnp.sin(ang)
    cs = jnp.tile(jnp.concatenate([cos, cos], axis=-1), (1, 2))
    sn = jnp.tile(jnp.concatenate([-sin, sin], axis=-1), (1, 2))
    return cs, sn


def _widen_w_in(w):
    q, u = w[0:512], w[768:1792]
    parts = [q]
    for base in (512, 576, 640, 704):
        parts += [w[base:base + 64], w[base:base + 64]]
    return jnp.concatenate(parts + [u], axis=0)


def _fold_w_in(d):
    parts = [d[0:512]]
    for base in (512, 640, 768, 896):
        parts.append(d[base:base + 64] + d[base + 64:base + 128])
    return jnp.concatenate(parts + [d[1024:2048]], axis=0)


def add_halves(g5s, r1s, c_idx):
    n = len(g5s)

    def body(c_ref, *refs):
        for a in range(n):
            refs[2 * n + a][...] = (refs[a][...] + refs[n + a][...]).astype(CDT)

    def g_spec(g):
        return pl.BlockSpec((2, None) + g.shape[2:], lambda s, cr: (s, cr[0], 0, 0))

    def r_spec(g):
        return pl.BlockSpec((2,) + g.shape[2:], lambda s, cr: (s, 0, 0))

    return list(pl.pallas_call(
        body, name="add_halves",
        grid_spec=pltpu.PrefetchScalarGridSpec(
            num_scalar_prefetch=1, grid=(N_CHIPS // 2,),
            in_specs=[g_spec(g) for g in g5s] + [r_spec(g) for g in g5s],
            out_specs=[r_spec(g) for g in g5s]),
        out_shape=[jax.ShapeDtypeStruct((N_CHIPS,) + g.shape[2:], CDT) for g in g5s],
        compiler_params=_params(1),
    )(c_idx, *g5s, *r1s))


def sum_partials(parts, recv3s, j_idx):
    n = len(parts)

    def body(j_ref, *refs):
        for a in range(n):
            p_ref, r_ref = refs[a], refs[n + a]
            refs[2 * n + a][...] = ((p_ref[...].astype(F32) + r_ref[0].astype(F32))
                                    + r_ref[1].astype(F32)) + r_ref[2].astype(F32)

    def half(p):
        return p.shape[1] // 2

    return list(pl.pallas_call(
        body, name="sum_partials",
        grid_spec=pltpu.PrefetchScalarGridSpec(
            num_scalar_prefetch=1, grid=(2,),
            in_specs=[pl.BlockSpec((None, half(p), p.shape[2]), lambda i, jr: (jr[0], i, 0)) for p in parts]
            + [pl.BlockSpec((3, half(p), p.shape[2]), lambda i, jr: (0, i, 0)) for p in parts],
            out_specs=[pl.BlockSpec((half(p), p.shape[2]), lambda i, jr: (i, 0)) for p in parts]),
        out_shape=[jax.ShapeDtypeStruct(p.shape[1:], F32) for p in parts],
        compiler_params=_params(1),
    )(j_idx, *parts, *recv3s))


class _Chain:
    STAGES = ("swap", "xchg", "share")

    def __init__(self, grads, c_idx, j_idx):
        self.c_idx, self.j_idx = c_idx, j_idx
        self.g5 = [g.reshape(N_CHIPS, 2, g.shape[0] // (2 * N_CHIPS), g.shape[1]) for g in grads]
        self.stage_no = 0

    @property
    def done(self):
        return self.stage_no == len(self.STAGES)

    def next_stage(self):
        name = self.STAGES[self.stage_no]

        def callback(res):
            getattr(self, "after_" + name)(res)
            self.stage_no += 1

        return getattr(self, name)(), callback

    def swap(self):
        return swap_op(self.g5)

    def after_swap(self, recv):
        self.parts = add_halves(self.g5, recv, self.c_idx)

    def xchg(self):
        return exchange_op(self.parts)

    def after_xchg(self, recv):
        self.totals = sum_partials(self.parts, recv, self.j_idx)

    def share(self):
        return share_op(self.totals)

    def after_share(self, recv):
        both = [_own_slab(h, t, self.c_idx[0]) for h, t in zip(recv, self.totals)]
        self.final = [h.reshape(2 * h.shape[1], h.shape[2]) for h in both]


def all_reduce_small(vec):
    R = vec.shape[0]

    def body(v_ref, o_ref, buf, send, recv):
        x, y, c = _coords()
        me = 4 * x + 2 * y + c
        buf[me] = v_ref[...]
        cps = []
        for m in range(1, N_DEV):
            dx, dy, dc = (m >> 2) & 1, (m >> 1) & 1, m & 1
            cp = pltpu.make_async_remote_copy(v_ref, buf.at[me], send.at[m - 1], recv.at[m - 1],
                                              device_id=((x + dx) % 2, (y + dy) % 2, (c + dc) % 2),
                                              device_id_type=MESH)
            cp.start()
            cps.append(cp)
        for cp in cps:
            cp.wait()
        acc = buf[0]
        for d in range(1, N_DEV):
            acc = acc + buf[d]
        o_ref[...] = acc

    return pl.pallas_call(
        body, name="all_reduce_small",
        in_specs=[pl.BlockSpec(memory_space=pltpu.VMEM)], out_specs=pl.BlockSpec(memory_space=pltpu.VMEM),
        out_shape=jax.ShapeDtypeStruct(vec.shape, F32),
        scratch_shapes=[pltpu.VMEM((N_DEV, R, 128), F32), pltpu.SemaphoreType.DMA((N_DEV - 1,)),
                        pltpu.SemaphoreType.DMA((N_DEV - 1,))],
    )(vec)


def adamw(w, g, m, v, *, tm=512):
    R, C = w.shape
    tm = max(t for t in range(8, min(tm, R) + 1, 8) if R % t == 0)
    c1 =1.0 - ADAM_B1 ** ADAM_STEP
    c2 = 1.0 - ADAM_B2 ** ADAM_STEP

    def body(w_ref, g_ref, m_ref, v_ref, d_ref, nm_ref, nv_ref):
        gg = g_ref[...]
        nm = ADAM_B1 * m_ref[...] + (1.0 - ADAM_B1) * gg
        nv = ADAM_B2 * v_ref[...] + (1.0 - ADAM_B2) * (gg * gg)
        nm_ref[...] = nm
        nv_ref[...] = nv
        d_ref[...] = -ADAM_LR * ((nm / c1) / (jnp.sqrt(nv / c2) + ADAM_EPS) + ADAM_WD * w_ref[...])

    blk = pl.BlockSpec((tm, C), lambda i: (i, 0))
    return pl.pallas_call(
        body, name="adamw",
        grid=(pl.cdiv(R, tm),),
        in_specs=[blk] * 4, out_specs=[blk] * 3,
        out_shape=[jax.ShapeDtypeStruct((R, C), F32)] * 3,
        compiler_params=_params(1),
    )(w, g, m, v)


def adamw_layers(w, m, v, g_layers, *, tm=352, comm=None):
    L, R, C = w.shape
    tm = max(t for t in range(8, min(tm, R) + 1, 8) if R % t == 0)
    c1 = 1.0 - ADAM_B1 ** ADAM_STEP
    c2 = 1.0 - ADAM_B2 ** ADAM_STEP

    def body(w_ref, m_ref, v_ref, *rest):
        g_refs, (go_ref, d_ref, nm_ref, nv_ref) = rest[:L], rest[L:]
        layer = pl.program_id(0)
        gg = g_refs[0][...]
        for l in range(1, L):
            gg = jnp.where(layer == l, g_refs[l][...], gg)
        nm = ADAM_B1 * m_ref[...] + (1.0 - ADAM_B1) * gg
        nv = ADAM_B2 * v_ref[...] + (1.0 - ADAM_B2) * (gg * gg)
        go_ref[...] = gg
        nm_ref[...] = nm
        nv_ref[...] = nv
        d_ref[...] = -ADAM_LR * ((nm / c1) / (jnp.sqrt(nv / c2) + ADAM_EPS) + ADAM_WD * w_ref[...])

    blk = pl.BlockSpec((None, tm, C), lambda l, i: (l, i, 0))
    gblk = pl.BlockSpec((tm, C), lambda l, i: (i, 0))
    return _hosted(comm)(
        body, name="adamw_layers",
        grid=(L, R // tm),
        in_specs=[blk] * 3 + [gblk] * L, out_specs=[blk] * 4,
        out_shape=[jax.ShapeDtypeStruct((L, R, C), F32)] * 4,
        compiler_params=_params(2),
    )(w, m, v, *g_layers)


_SMALL = (("n1", (2, D_MODEL)), ("nm", (2, D_MODEL)), ("n2", (2, D_MODEL)), ("nf", (D_MODEL,)),
          ("cb", (2, CONV_C)), ("lg", (2, CONV_C)), ("lb", (2, CONV_C)), ("sinks", (2, N_HEADS)),
          ("cw", (2, CONV_K, CONV_C)))


def _pack(parts, rows):
    flat = jnp.concatenate([p.reshape(-1).astype(F32) for p in parts])
    return jnp.pad(flat, (0, rows * 128 - flat.shape[0])).reshape(rows, 128)


def _unpack(block, shapes):
    flat = block.reshape(-1)
    out, o = [], 0
    for shp in shapes:
        n = 1
        for s in shp:
            n *= s
        out.append(flat[o:o + n].reshape(shp))
        o += n
    return out


def kernel(x, positions, ffn1_norm, ffn1_w_gate, ffn1_w_up, ffn1_w_down, mix_norm, w_in, conv_w, conv_b, conv_ln_g, conv_ln_b, attn_sinks, w_out, ffn2_norm, ffn2_w_gate, ffn2_w_up, ffn2_w_down, final_norm, loss_target, m_ffn1_norm, m_ffn1_w_gate, m_ffn1_w_up, m_ffn1_w_down, m_mix_norm, m_w_in, m_conv_w, m_conv_b, m_conv_ln_g, m_conv_ln_b, m_attn_sinks, m_w_out, m_ffn2_norm, m_ffn2_w_gate, m_ffn2_w_up, m_ffn2_w_down, m_final_norm, v_ffn1_norm, v_ffn1_w_gate, v_ffn1_w_up, v_ffn1_w_down, v_mix_norm, v_w_in, v_conv_w, v_conv_b, v_conv_ln_g, v_conv_ln_b, v_attn_sinks, v_w_out, v_ffn2_norm, v_ffn2_w_gate, v_ffn2_w_up, v_ffn2_w_down, v_final_norm):
    cx, cy, cc = _coords()
    chip = 2 * cx + cy
    c_idx = jnp.reshape(cc, (1,)).astype(jnp.int32)
    j_idx = jnp.reshape(chip, (1,)).astype(jnp.int32)
    L = ffn1_norm.shape[0]
    tr = lambda a: jnp.swapaxes(a, 1, 2)

    sh = dict(f1g=tr(ffn1_w_gate), f1u=tr(ffn1_w_up), f1d=ffn1_w_down, f2g=tr(ffn2_w_gate),
              f2u=tr(ffn2_w_up), f2d=ffn2_w_down, win=tr(w_in), wout=w_out)
    sh = {k: [v[l].astype(CDT) for l in range(L)] for k, v in sh.items()}
    W = {}

    def gather_op(keys):
        if keys == ["cw"]:
            return ag_op([conv_w])
        return ag2_op([sh[k[0]][k[1]] for k in keys])

    def take(keys, res):
        for k, a in zip(keys, res):
            if k == "cw":
                W[k] = _own_slab(a, conv_w, chip)
            else:
                mine = sh[k[0]][k[1]]
                W[k] = _own_slab(a, mine.reshape(a.shape[1:]), chip).reshape(N_CHIPS * mine.shape[0], mine.shape[1])

    def with_ag(fn, keys, *args):
        if not keys:
            return fn(*args)
        main, res = fn(*args, comm=gather_op(keys))
        take(keys, res)
        return main

    ag_hosts = {("ffn1", 0): [("win", 0), ("f2g", 0), ("f1d", 1)],
                ("inproj", 0): ["cw"], ("attn", 0): [("wout", 0), ("f2u", 0)], ("conv", 0): [("f2d", 0)],
                ("ffn2", 0): [("f1g", 1), ("f1u", 1)],
                ("ffn1", 1): [("win", 1), ("f2g", 1), ("f2u", 1)],
                ("inproj", 1): [("wout", 1)], ("attn", 1): [("f2d", 1)]}
    first = [("f1g", 0), ("f1u", 0), ("f1d", 0)]
    take(first, first_gather([sh[k_][l] for k_, l in first]))

    cs, sn = _rope_tables(positions)
    saved = []
    h = x[0]
    for l in range(L):
        sink = attn_sinks[l].reshape(2, 4)
        sink_row = jnp.repeat(sink, BLOCK, axis=1).reshape(2, 1, 4 * BLOCK)
        x0 = h
        x1, g1, u1 = with_ag(ffn_fwd, ag_hosts.get(("ffn1", l)), x0, ffn1_norm[l][None],
                             W[("f1g", l)], W[("f1u", l)], W[("f1d", l)])
        wext = _widen_w_in(W[("win", l)])
        q, k, v, u = with_ag(inproj_fwd, ag_hosts.get(("inproj", l)), x1, mix_norm[l][None], wext, cs, sn)
        ao = with_ag(attn_fwd, ag_hosts.get(("attn", l)), q, k, v, sink_row)
        cwl = jnp.transpose(W["cw"][:, l], (1, 0, 2)).reshape(CONV_K, CONV_C)
        co, yc = with_ag(conv_fwd, ag_hosts.get(("conv", l)), u, cwl, conv_b[l][None], conv_ln_g[l][None],
                         conv_ln_b[l][None])
        x2 = outproj_fwd(x1, ao, co, W[("wout", l)])
        x3, g2, u2 = with_ag(ffn_fwd, ag_hosts.get(("ffn2", l)), x2, ffn2_norm[l][None],
                             W[("f2g", l)], W[("f2u", l)], W[("f2d", l)])
        saved.append((x0, x1, x2, g1, u1, g2, u2, q, k, v, u, ao, co, yc, sink, wext, cwl))
        h = x3

    loss, dx, dnf = loss_head(h, final_norm[None], loss_target[0])

    active = []

    def advance(run):
        stages = [ch.next_stage() for ch in active]
        ops = [op for op, _ in stages]
        main, res = run(_merge(*ops) if ops else None)
        for (_, cb), r in zip(stages, _split(res, *ops)):
            cb(r)
        active[:] = [ch for ch in active if not ch.done]
        return main

    def hosted(fn, *args):
        def run(comm):
            if comm is None:
                return fn(*args), []
            return fn(*args, comm=comm)
        return advance(run)

    def chain(key, names_, grads):
        chains[key] = _Chain(grads, c_idx, j_idx)
        active.append(chains[key])
        for i_, nme_ in enumerate(names_):
            where[(nme_, key[1])] = (key, i_)

    small = {k_: [None] * L for k_ in ("n1", "nm", "n2", "cw", "cb", "lg", "lb", "sinks")}
    chains, where = {}, {}
    for l in reversed(range(L)):
        x0, x1, x2, g1, u1, g2, u2, q, k, v, u, ao, co, yc, sink, wext, cwl = saved[l]
        sink_row = jnp.repeat(sink, BLOCK, axis=1).reshape(2, 1, 4 * BLOCK)
        chain(("f2d", l), ["ffn2_w_down"], [hosted(ffn_wgrad_down, dx, g2, u2)])
        dx2, dgt, dup, hh, small["n2"][l] = hosted(
            ffn_bwd_dgrad, x2, ffn2_norm[l][None], dx, g2, u2, W[("f2g", l)], W[("f2u", l)], W[("f2d", l)])
        chain(("f2gu", l), ["ffn2_w_gate", "ffn2_w_up"], hosted(ffn_wgrad_rows, hh, [dgt, dup]))
        da, dc, gwout = hosted(outproj_bwd, dx2, ao, co, W[("wout", l)])
        du, small["cw"][l], small["cb"][l], small["lg"][l], small["lb"][l] = hosted(
            conv_bwd, dc, u, yc, cwl, conv_ln_g[l][None], conv_ln_b[l][None])
        dq, dk, dv, dsink = hosted(attn_bwd, q, k, v, da, sink_row)
        small["sinks"][l] = jnp.sum(dsink.reshape(2, 4, BLOCK), axis=-1).reshape(N_HEADS)
        dx1, gwext, small["nm"][l] = inproj_bwd(x1, mix_norm[l][None], dx2, dq, dk, dv, du, wext, cs, sn)
        chain(("mx", l), ["w_out", "w_in"], [gwout, _fold_w_in(gwext)])
        chain(("f1d", l), ["ffn1_w_down"], [hosted(ffn_wgrad_down, dx1, g1, u1)])
        dx, dgt, dup, hh, small["n1"][l] = hosted(
            ffn_bwd_dgrad, x0, ffn1_norm[l][None], dx1, g1, u1, W[("f1g", l)], W[("f1u", l)], W[("f1d", l)])
        if l > 0:
            chain(("f1gu", l), ["ffn1_w_gate", "ffn1_w_up"], hosted(ffn_wgrad_rows, hh, [dgt, dup]))
        else:
            chain(("f1u", l), ["ffn1_w_up"], hosted(ffn_wgrad_rows, hh, [dup]))
            chain(("f1g", l), ["ffn1_w_gate"], hosted(ffn_wgrad_rows, hh, [dgt]))

    weights = dict(ffn1_norm=ffn1_norm, ffn1_w_gate=ffn1_w_gate, ffn1_w_up=ffn1_w_up, ffn1_w_down=ffn1_w_down,
                   mix_norm=mix_norm, w_in=w_in, conv_w=conv_w, conv_b=conv_b, conv_ln_g=conv_ln_g,
                   conv_ln_b=conv_ln_b, attn_sinks=attn_sinks, w_out=w_out, ffn2_norm=ffn2_norm,
                   ffn2_w_gate=ffn2_w_gate, ffn2_w_up=ffn2_w_up, ffn2_w_down=ffn2_w_down, final_norm=final_norm)
    moms = dict(ffn1_norm=(m_ffn1_norm, v_ffn1_norm), ffn1_w_gate=(m_ffn1_w_gate, v_ffn1_w_gate),
                ffn1_w_up=(m_ffn1_w_up, v_ffn1_w_up), ffn1_w_down=(m_ffn1_w_down, v_ffn1_w_down),
                mix_norm=(m_mix_norm, v_mix_norm), w_in=(m_w_in, v_w_in), conv_w=(m_conv_w, v_conv_w),
                conv_b=(m_conv_b, v_conv_b), conv_ln_g=(m_conv_ln_g, v_conv_ln_g),
                conv_ln_b=(m_conv_ln_b, v_conv_ln_b), attn_sinks=(m_attn_sinks, v_attn_sinks),
                w_out=(m_w_out, v_w_out), ffn2_norm=(m_ffn2_norm, v_ffn2_norm),
                ffn2_w_gate=(m_ffn2_w_gate, v_ffn2_w_gate), ffn2_w_up=(m_ffn2_w_up, v_ffn2_w_up),
                ffn2_w_down=(m_ffn2_w_down, v_ffn2_w_down), final_norm=(m_final_norm, v_final_norm))
    names = list(weights)
    big_names = dict(ffn2_w_gate=True, ffn2_w_up=True, ffn2_w_down=False, w_out=False, w_in=True,
                     ffn1_w_down=False, ffn1_w_up=True, ffn1_w_gate=True)
    grads, delta, new_m, new_v = {}, {}, {}, {}

    def big_adamw(nme, comm=None):
        view = tr if big_names[nme] else (lambda a: a)
        args = (view(weights[nme]), view(moms[nme][0]), view(moms[nme][1]),
                [chains[where[(nme, l)][0]].final[where[(nme, l)][1]] for l in range(L)])
        res, comm_res = (adamw_layers(*args), []) if comm is None else adamw_layers(*args, comm=comm)
        grads[nme], delta[nme], new_m[nme], new_v[nme] = [view(a) for a in res]
        return None, comm_res

    for nme in big_names:
        if not all(chains[where[(nme, l)][0]].done for l in range(L)):
            break
        advance(functools.partial(big_adamw, nme))
    while active:
        advance(lambda comm: (None, _run_comm(comm)))
    for nme in big_names:
        if nme not in grads:
            big_adamw(nme)

    G = {k_: jnp.stack(v_) for k_, v_ in small.items()}
    G["nf"] = dnf
    small_shapes = [shp for _, shp in _SMALL]
    n_small = 1 + sum(math.prod(s) for s in small_shapes)
    rows = -(-n_small // 1024) * 8
    packed = _pack([loss] + [G[k_] for k_, _ in _SMALL], rows)
    summed = all_reduce_small(packed)
    loss_out, *small_sum = _unpack(summed, [()] + small_shapes)
    gs = dict(zip([k_ for k_, _ in _SMALL], small_sum))
    gs["cw"] = lax.dynamic_slice_in_dim(gs["cw"], chip * (CONV_C // N_CHIPS), CONV_C // N_CHIPS, axis=2)
    grads.update(ffn1_norm=gs["n1"], mix_norm=gs["nm"], conv_w=gs["cw"], conv_b=gs["cb"], conv_ln_g=gs["lg"],
                 conv_ln_b=gs["lb"], attn_sinks=gs["sinks"], ffn2_norm=gs["n2"], final_norm=gs["nf"])

    small_names = [nme for nme in names if nme not in big_names]
    s_shapes = [weights[nme].shape for nme in small_names]
    n_tot = sum(math.prod(s) for s in s_shapes)
    srows = -(-n_tot // 1024) * 8
    d, nm_, nv_ = adamw(_pack([weights[nme] for nme in small_names], srows),
                        _pack([grads[nme] for nme in small_names], srows),
                        _pack([moms[nme][0] for nme in small_names], srows),
                        _pack([moms[nme][1] for nme in small_names], srows))
    for nme, dd, mm, vv in zip(small_names, _unpack(d, s_shapes), _unpack(nm_, s_shapes), _unpack(nv_, s_shapes)):
        delta[nme], new_m[nme], new_v[nme] = dd, mm, vv

    return (loss_out, dx[None], *[grads[nme] for nme in names], *[delta[nme] for nme in names],
            *[new_m[nme] for nme in names], *[new_v[nme] for nme in names])
```

```python
import math

import jax
import jax.numpy as jnp
from jax import lax
from jax.experimental import pallas as pl
from jax.experimental.pallas import tpu as pltpu

F32 = jnp.float32
CDT = jnp.bfloat16
D_MODEL = 1024
D_FF = 2816
N_HEADS = 8
HEAD_DIM = 64
BLOCK = 128
CONV_K = 31
CONV_C = 512
ATT_W = 512
D_EXT = 2048
EPS = 1e-5
HALO = 32
FF_CHUNK = 256
NEG = float(jnp.finfo(jnp.float32).min)
VMEM_LIMIT = 56 * 1024 * 1024

ADAM_LR = 0.001
ADAM_B1 = 0.9
ADAM_B2 = 0.999
ADAM_EPS = 1e-08
ADAM_WD = 0.01
ADAM_STEP = 10

NT = (((1,), (1,)), ((), ()))
TN = (((0,), (0,)), ((), ()))


MESH = pl.DeviceIdType.MESH
ANY = pl.BlockSpec(memory_space=pl.ANY)
N_CHIPS = 4
N_DEV = 8


def _params(n_axes):
    return pltpu.CompilerParams(dimension_semantics=("arbitrary",) * n_axes, vmem_limit_bytes=VMEM_LIMIT)


class _Comm:
    def __init__(self, name, inputs, out_shape, sems, descs, relay=None):
        self.name, self.inputs, self.out_shape, self.sems = name, list(inputs), list(out_shape), list(sems)
        self.descs, self.relay = descs, relay


def _merge(*ops):
    ops = [o for o in ops if o is not None]
    if len(ops) == 1:
        return ops[0]
    assert all(o.relay is None for o in ops)

    def descs(cins, couts, sems):
        out, i, o, s = [], 0, 0, 0
        for op in ops:
            ni, no, ns = len(op.inputs), len(op.out_shape), len(op.sems)
            out += op.descs(cins[i:i + ni], couts[o:o + no], sems[s:s + ns])
            i, o, s = i + ni, o + no, s + ns
        return out

    return _Comm("_".join(o.name for o in ops), sum((o.inputs for o in ops), []),
                 sum((o.out_shape for o in ops), []), sum((o.sems for o in ops), []), descs)


def _split(couts, *ops):
    res, o = [], 0
    for op in ops:
        res.append(couts[o:o + len(op.out_shape)])
        o += len(op.out_shape)
    return res


def _hosted(comm):
    if comm is None:
        return pl.pallas_call

    def make(body, *, name, grid, in_specs, out_specs, out_shape, compiler_params, scratch_shapes=()):
        single = not isinstance(out_shape, (list, tuple))
        o_specs = [out_specs] if single else list(out_specs)
        o_shape = [out_shape] if single else list(out_shape)
        n_in, n_out, n_sc = len(in_specs), len(o_specs), len(scratch_shapes)
        c_in, c_out = len(comm.inputs), len(comm.out_shape)

        def hosted(*refs):
            ins, cins = refs[:n_in], refs[n_in:n_in + c_in]
            o0 = n_in + c_in
            outs, couts = refs[o0:o0 + n_out], refs[o0 + n_out:o0 + n_out + c_out]
            s0 = o0 + n_out + c_out
            scr, sems = refs[s0:s0 + n_sc], refs[s0 + n_sc:]
            first = pl.program_id(0) == 0
            last = pl.program_id(0) == grid[0] - 1
            for ax in range(1, len(grid)):
                first = first & (pl.program_id(ax) == 0)
                last = last & (pl.program_id(ax) == grid[ax] - 1)

            @pl.when(first)
            def _():
                for d in comm.descs(cins, couts, sems):
                    d.start()

            if comm.relay is not None:
                assert len(grid) == 1

                @pl.when(pl.program_id(0) == (3 * grid[0]) // 4)
                def _():
                    for d in comm.descs(cins, couts, sems):
                        d.wait()
                    for d in comm.relay(cins, couts, sems):
                        d.start()

            body(*ins, *outs, *scr)

            @pl.when(last)
            def _():
                for d in (comm.relay or comm.descs)(cins, couts, sems):
                    d.wait()

        call = pl.pallas_call(
            hosted, name=f"{name}_{comm.name}", grid=grid,
            in_specs=list(in_specs) + [ANY] * c_in, out_specs=o_specs + [ANY] * c_out,
            out_shape=o_shape + comm.out_shape, scratch_shapes=list(scratch_shapes) + comm.sems,
            compiler_params=compiler_params)

        def run(*args):
            res = call(*args, *comm.inputs)
            return (res[0] if single else list(res[:n_out])), list(res[n_out:])

        return run

    return make


def _run_comm(comm):
    c_in = len(comm.inputs)

    def body(*refs):
        cins, couts, sems = refs[:c_in], refs[c_in:c_in + len(comm.out_shape)], refs[c_in + len(comm.out_shape):]
        ds = comm.descs(cins, couts, sems)
        for d in ds:
            d.start()
        for d in ds:
            d.wait()

    return list(pl.pallas_call(
        body, name=comm.name, in_specs=[ANY] * c_in, out_specs=[ANY] * len(comm.out_shape),
        out_shape=comm.out_shape, scratch_shapes=comm.sems)(*comm.inputs))


def _coords():
    return lax.axis_index("x"), lax.axis_index("y"), lax.axis_index("c")


def _other_chips(x, y):
    return [(1 - x, y), (x, 1 - y), (1 - x, 1 - y)]


def ag_op(shards):
    n = len(shards)

    def descs(cins, couts, sems):
        send, recv = sems
        x, y, c = _coords()
        j = 2 * x + y
        ds = []
        for a in range(n):
            for p, (px, py) in enumerate(_other_chips(x, y)):
                ds.append(pltpu.make_async_remote_copy(cins[a], couts[a].at[j], send.at[a, p], recv.at[a, p],
                                                       device_id=(px, py, c), device_id_type=MESH))
        return ds

    return _Comm("ag", shards, [jax.ShapeDtypeStruct((N_CHIPS,) + s.shape, s.dtype) for s in shards],
                 [pltpu.SemaphoreType.DMA((n, 3)), pltpu.SemaphoreType.DMA((n, 3))], descs)


def ag2_op(shards):
    n = len(shards)
    halves = [s.reshape(2, s.shape[0] // 2, s.shape[1]) for s in shards]

    def descs(cins, couts, sems):
        x, y, c = _coords()
        j = 2 * x + y
        return [pltpu.make_async_remote_copy(cins[a].at[c], couts[a].at[j, c], sems[0].at[a, p], sems[1].at[a, p],
                                             device_id=(px, py, c), device_id_type=MESH)
                for a in range(n) for p, (px, py) in enumerate(_other_chips(x, y))]

    def relay(cins, couts, sems):
        x, y, c = _coords()
        return [pltpu.make_async_remote_copy(couts[a].at[2 * px + py, c], couts[a].at[2 * px + py, c],
                                             sems[2].at[a, p], sems[3].at[a, p],
                                             device_id=(x, y, 1 - c), device_id_type=MESH)
                for a in range(n) for p, (px, py) in enumerate(_other_chips(x, y))]

    return _Comm("ag2", halves, [jax.ShapeDtypeStruct((N_CHIPS,) + h.shape, h.dtype) for h in halves],
                 [pltpu.SemaphoreType.DMA((n, 3))] * 4, descs, relay)


def _own_slab(gathered, mine, idx):
    return lax.dynamic_update_slice_in_dim(gathered, mine[None], idx, axis=0)


def first_gather(shards):
    n = len(shards)
    halves = [s.reshape(2, s.shape[0] // 2, s.shape[1]) for s in shards]

    def body(*refs):
        ins, outs = refs[:n], refs[n:2 * n]
        send1, recv1, send2, recv2 = refs[2 * n:]
        x, y, c = _coords()
        j = 2 * x + y
        chips = _other_chips(x, y)
        ici = [pltpu.make_async_remote_copy(ins[a].at[c], outs[a].at[j, c], send1.at[a, p], recv1.at[a, p],
                                            device_id=(px, py, c), device_id_type=MESH)
               for a in range(n) for p, (px, py) in enumerate(chips)]
        for d in ici:
            d.start()
        d2d = [pltpu.make_async_remote_copy(outs[a].at[2 * px + py, c], outs[a].at[2 * px + py, c],
                                            send2.at[a, p], recv2.at[a, p],
                                            device_id=(x, y, 1 - c), device_id_type=MESH)
               for a in range(n) for p, (px, py) in enumerate(chips)]
        for d1, d2 in zip(ici, d2d):
            d1.wait()
            d2.start()
        for d in d2d:
            d.wait()

    return list(pl.pallas_call(
        body, name="first_gather", in_specs=[ANY] * n, out_specs=[ANY] * n,
        out_shape=[jax.ShapeDtypeStruct((N_CHIPS,) + h.shape, h.dtype) for h in halves],
        scratch_shapes=[pltpu.SemaphoreType.DMA((n, 3))] * 4)(*halves))


def swap_op(grads):
    n = len(grads)

    def descs(cins, couts, sems):
        send, recv = sems
        x, y, c = _coords()
        return [pltpu.make_async_remote_copy(cins[a].at[:, 1 - c], couts[a], send.at[a], recv.at[a],
                                             device_id=(x, y, 1 - c), device_id_type=MESH) for a in range(n)]

    return _Comm("swap", grads, [jax.ShapeDtypeStruct(g.shape[:1] + g.shape[2:], g.dtype) for g in grads],
                 [pltpu.SemaphoreType.DMA((n,)), pltpu.SemaphoreType.DMA((n,))], descs)


def exchange_op(parts):
    n = len(parts)

    def descs(cins, couts, sems):
        send, recv = sems
        x, y, c = _coords()
        ds = []
        for a in range(n):
            for p, (px, py) in enumerate(_other_chips(x, y)):
                ds.append(pltpu.make_async_remote_copy(cins[a].at[2 * px + py], couts[a].at[p], send.at[a, p],
                                                       recv.at[a, p], device_id=(px, py, c), device_id_type=MESH))
        return ds

    return _Comm("xchg", parts, [jax.ShapeDtypeStruct((3,) + p.shape[1:], p.dtype) for p in parts],
                 [pltpu.SemaphoreType.DMA((n, 3)), pltpu.SemaphoreType.DMA((n, 3))], descs)


def share_op(totals):
    n = len(totals)

    def descs(cins, couts, sems):
        send, recv = sems
        x, y, c = _coords()
        return [pltpu.make_async_remote_copy(cins[a], couts[a].at[c], send.at[a], recv.at[a],
                                             device_id=(x, y, 1 - c), device_id_type=MESH) for a in range(n)]

    return _Comm("share", totals, [jax.ShapeDtypeStruct((2,) + t.shape, t.dtype) for t in totals],
                 [pltpu.SemaphoreType.DMA((n,)), pltpu.SemaphoreType.DMA((n,))], descs)


def _sigmoid(z):
    return 1.0 / (1.0 + jnp.exp(-z))


def _rms_parts(xf):
    r = lax.rsqrt(jnp.mean(xf * xf, axis=-1, keepdims=True) + EPS)
    return xf * r, r


def _rms_bwd(dh, xhat, r, g):
    dg = jnp.sum(dh * xhat, axis=0, keepdims=True)
    dxhat = dh * g
    dx = r * (dxhat - xhat * jnp.mean(dxhat * xhat, axis=-1, keepdims=True))
    return dx, dg


def _chunks(n, ck=FF_CHUNK):
    return [(c0, min(ck, n - c0)) for c0 in range(0, n, ck)]


def ffn_fwd(x, g, wgT, wuT, wd, *, tm=512, comm=None):
    S, D = x.shape
    F = wgT.shape[0]

    def body(x_ref, g_ref, wg_ref, wu_ref, wd_ref, o_ref, gate_ref, up_ref, a_sc):
        xf = x_ref[...]
        xhat, _ = _rms_parts(xf)
        h = (xhat * g_ref[...]).astype(CDT)
        for c0, cw_ in _chunks(F):
            sl = slice(c0, c0 + cw_)
            gt = lax.dot_general(h, wg_ref[sl, :], NT, preferred_element_type=F32)
            ut = lax.dot_general(h, wu_ref[sl, :], NT, preferred_element_type=F32)
            gate_ref[:, sl] = gt.astype(CDT)
            up_ref[:, sl] = ut.astype(CDT)
            a_sc[:, sl] = (gt * _sigmoid(gt) * ut).astype(CDT)
        o_ref[...] = xf + 0.5 * jnp.dot(a_sc[...], wd_ref[...], preferred_element_type=F32)

    wspec = pl.BlockSpec((F, D), lambda i: (0, 0), pipeline_mode=pl.Buffered(1))
    return _hosted(comm)(
        body, name="ffn_fwd",
        grid=(S // tm,),
        in_specs=[pl.BlockSpec((tm, D), lambda i: (i, 0)), pl.BlockSpec((1, D), lambda i: (0, 0)),
                  wspec, wspec, wspec],
        out_specs=[pl.BlockSpec((tm, D), lambda i: (i, 0)),
                   pl.BlockSpec((tm, F), lambda i: (i, 0)),
                   pl.BlockSpec((tm, F), lambda i: (i, 0))],
        out_shape=[jax.ShapeDtypeStruct((S, D), F32),
                   jax.ShapeDtypeStruct((S, F), CDT),
                   jax.ShapeDtypeStruct((S, F), CDT)],
        scratch_shapes=[pltpu.VMEM((tm, F), CDT)],
        compiler_params=_params(1),
    )(x, g, wgT, wuT, wd)


def ffn_bwd_dgrad(x, g, dy, gate, up, wgT, wuT, wd, *, tm=256, comm=None):
    S, D = x.shape
    F = wgT.shape[0]

    def body(x_ref, g_ref, dy_ref, gate_ref, up_ref, wg_ref, wu_ref, wd_ref,
             dx_ref, dgate_ref, dup_ref, h_ref, dg_ref):
        @pl.when(pl.program_id(0) == 0)
        def _():
            dg_ref[...] = jnp.zeros_like(dg_ref)

        dyf = dy_ref[...]
        dacc = (0.5 * dyf).astype(CDT)
        gg = g_ref[...]
        xhat, r = _rms_parts(x_ref[...])
        h_ref[...] = (xhat * gg).astype(CDT)
        for c0, cw_ in _chunks(F):
            sl = slice(c0, c0 + cw_)
            d_a = lax.dot_general(dacc, wd_ref[sl, :], NT, preferred_element_type=F32)
            gt = gate_ref[:, sl].astype(F32)
            ut = up_ref[:, sl].astype(F32)
            sg = _sigmoid(gt)
            dup_ref[:, sl] = (d_a * (gt * sg)).astype(CDT)
            dgate_ref[:, sl] = (d_a * ut * (sg * (1.0 + gt * (1.0 - sg)))).astype(CDT)
        dh = (jnp.dot(dgate_ref[...], wg_ref[...], preferred_element_type=F32)
              + jnp.dot(dup_ref[...], wu_ref[...], preferred_element_type=F32))
        dx, dg = _rms_bwd(dh, xhat, r, gg)
        dx_ref[...] = dyf + dx
        dg_ref[...] += dg

    wspec = pl.BlockSpec((F, D), lambda i: (0, 0), pipeline_mode=pl.Buffered(1))
    row = pl.BlockSpec((tm, D), lambda i: (i, 0))
    act = pl.BlockSpec((tm, F), lambda i: (i, 0))
    vec = pl.BlockSpec((1, D), lambda i: (0, 0))
    return _hosted(comm)(
        body, name="ffn_bwd_dgrad",
        grid=(S // tm,),
        in_specs=[row, vec, row, act, act, wspec, wspec, wspec],
        out_specs=[row, act, act, row, vec],
        out_shape=[jax.ShapeDtypeStruct((S, D), F32),
                   jax.ShapeDtypeStruct((S, F), CDT),
                   jax.ShapeDtypeStruct((S, F), CDT),
                   jax.ShapeDtypeStruct((S, D), CDT),
                   jax.ShapeDtypeStruct((1, D), F32)],
        compiler_params=_params(1),
    )(x, g, dy, gate, up, wgT, wuT, wd)


def ffn_wgrad_down(dy, gate, up, *, tk=1024, comm=None):
    S, D = dy.shape
    F = gate.shape[1]
    tk = min(tk, S)

    def body(dy_ref, gate_ref, up_ref, dwd_ref):
        @pl.when(pl.program_id(0) == 0)
        def _():
            dwd_ref[...] = jnp.zeros_like(dwd_ref)

        dacc = (0.5 * dy_ref[...]).astype(CDT)
        for c0, cw_ in _chunks(F):
            sl = slice(c0, c0 + cw_)
            gt = gate_ref[:, sl].astype(F32)
            a = (gt * _sigmoid(gt) * up_ref[:, sl].astype(F32)).astype(CDT)
            dwd_ref[sl, :] += lax.dot_general(a, dacc, TN, preferred_element_type=F32)

    act = pl.BlockSpec((tk, F), lambda k: (k, 0))
    return _hosted(comm)(
        body, name="ffn_wgrad_down",
        grid=(S // tk,),
        in_specs=[pl.BlockSpec((tk, D), lambda k: (k, 0)), act, act],
        out_specs=pl.BlockSpec((F, D), lambda k: (0, 0), pipeline_mode=pl.Buffered(1)),
        out_shape=jax.ShapeDtypeStruct((F, D), F32),
        compiler_params=_params(1),
    )(dy, gate, up)


def ffn_wgrad_rows(h, acts, *, tk=1024, comm=None):
    S, D = h.shape
    F = acts[0].shape[1]
    n = len(acts)
    tk = min(tk, S)

    def body(h_ref, *refs):
        act_refs, dw_refs = refs[:n], refs[n:]

        @pl.when(pl.program_id(0) == 0)
        def _():
            for dw_ref in dw_refs:
                dw_ref[...] = jnp.zeros_like(dw_ref)

        hh = h_ref[...]
        for c0, cw_ in _chunks(F):
            sl = slice(c0, c0 + cw_)
            for act_ref, dw_ref in zip(act_refs, dw_refs):
                dw_ref[sl, :] += lax.dot_general(act_ref[:, sl], hh, TN, preferred_element_type=F32)

    act = pl.BlockSpec((tk, F), lambda k: (k, 0))
    out = pl.BlockSpec((F, D), lambda k: (0, 0), pipeline_mode=pl.Buffered(1))
    return _hosted(comm)(
        body, name="ffn_wgrad_rows",
        grid=(S // tk,),
        in_specs=[pl.BlockSpec((tk, D), lambda k: (k, 0))] + [act] * n,
        out_specs=[out] * n,
        out_shape=[jax.ShapeDtypeStruct((F, D), F32)] * n,
        compiler_params=_params(1),
    )(h, *acts)


def loss_head(x, g, target, *, tm=512):
    S, D = x.shape

    def body(x_ref, g_ref, t_ref, loss_ref, dx_ref, dg_ref):
        @pl.when(pl.program_id(0) == 0)
        def _():
            loss_ref[...] = jnp.zeros_like(loss_ref)
            dg_ref[...] = jnp.zeros_like(dg_ref)

        xhat, r = _rms_parts(x_ref[...])
        gg = g_ref[...]
        err = xhat * gg - t_ref[...]
        loss_ref[...] += 0.5 * jnp.sum(jnp.mean(err * err, axis=-1, keepdims=True), axis=0, keepdims=True)
        dx, dg = _rms_bwd(err * (1.0 / D), xhat, r, gg)
        dx_ref[...] = dx
        dg_ref[...] += dg

    row = pl.BlockSpec((tm, D), lambda i: (i, 0))
    vec = pl.BlockSpec((1, D), lambda i: (0, 0))
    return pl.pallas_call(
        body, name="loss_head",
        grid=(S // tm,),
        in_specs=[row, vec, row],
        out_specs=[pl.BlockSpec((1, 1), lambda i: (0, 0)), row, vec],
        out_shape=[jax.ShapeDtypeStruct((1, 1), F32), jax.ShapeDtypeStruct((S, D), F32),
                   jax.ShapeDtypeStruct((1, D), F32)],
        compiler_params=_params(1),
    )(x, g, target)


def _rope_apply(t, cs, sn):
    lane = lax.broadcasted_iota(jnp.int32, t.shape, 1)
    first = (lane % HEAD_DIM) < (HEAD_DIM // 2)
    rot = jnp.where(first, pltpu.roll(t, 128 - HEAD_DIM // 2, 1), pltpu.roll(t, HEAD_DIM // 2, 1))
    return t * cs + rot * sn


def _rope_transpose(d, cs, sn):
    lane = lax.broadcasted_iota(jnp.int32, d.shape, 1)
    first = (lane % HEAD_DIM) < (HEAD_DIM // 2)
    ds = d * sn
    rot = jnp.where(first, pltpu.roll(ds, 128 - HEAD_DIM // 2, 1), pltpu.roll(ds, HEAD_DIM // 2, 1))
    return d * cs + rot


def inproj_fwd(x, g, wextT, cs, sn, *, tm=512, comm=None):
    S, D = x.shape
    scale = HEAD_DIM ** -0.5

    def body(x_ref, g_ref, w_ref, cs_ref, sn_ref, q_ref, k_ref, v_ref, u_ref):
        xhat, _ = _rms_parts(x_ref[...])
        h = (xhat * g_ref[...]).astype(CDT)
        p = lax.dot_general(h, w_ref[...], NT, preferred_element_type=F32)
        c, s = cs_ref[...], sn_ref[...]
        for b in range(4):
            q_ref[:, 128 * b:128 * (b + 1)] = (_rope_apply(p[:, 128 * b:128 * (b + 1)], c, s) * scale).astype(CDT)
        for b in range(2):
            k_ref[:, 128 * b:128 * (b + 1)] = _rope_apply(p[:, 512 + 128 * b:512 + 128 * (b + 1)], c, s).astype(CDT)
        v_ref[...] = p[:, 768:1024].astype(CDT)
        u_ref[...] = p[:, 1024:2048]

    def row(w):
        return pl.BlockSpec((tm, w), lambda i: (i, 0))

    return _hosted(comm)(
        body, name="inproj_fwd",
        grid=(S // tm,),
        in_specs=[row(D), pl.BlockSpec((1, D), lambda i: (0, 0)),
                  pl.BlockSpec((D_EXT, D), lambda i: (0, 0)), row(128), row(128)],
        out_specs=[row(512), row(256), row(256), row(1024)],
        out_shape=[jax.ShapeDtypeStruct((S, 512), CDT), jax.ShapeDtypeStruct((S, 256), CDT),
                   jax.ShapeDtypeStruct((S, 256), CDT), jax.ShapeDtypeStruct((S, 1024), F32)],
        compiler_params=_params(1),
    )(x, g, wextT, cs, sn)


def _stack_heads(p0, p1):
    lane = lax.broadcasted_iota(jnp.int32, p0.shape, 1)
    lo = lane < HEAD_DIM
    z = jnp.zeros_like(p0)
    return jnp.concatenate([jnp.where(lo, p0, z), jnp.where(lo, z, p0),
                            jnp.where(lo, p1, z), jnp.where(lo, z, p1)], axis=0)


def _unstack_heads(o):
    lane = lax.broadcasted_iota(jnp.int32, (BLOCK, 128), 1)
    lo = lane < HEAD_DIM
    return (jnp.where(lo, o[0:128], o[128:256]), jnp.where(lo, o[256:384], o[384:512]))


def _band_mask_kq(n):
    c = lax.broadcasted_iota(jnp.int32, (2 * BLOCK, 4 * BLOCK), 0)
    i = lax.broadcasted_iota(jnp.int32, (2 * BLOCK, 4 * BLOCK), 1) % BLOCK
    return (c > i) & (c <= i + BLOCK) & ((n > 0) | (c >= BLOCK))


def attn_fwd(q, k, v, sink_row, *, nb=4, comm=None):
    S = q.shape[0]
    tq = nb * BLOCK

    def body(q_ref, k_ref, v_ref, sink_ref, o_ref):
        t = pl.program_id(0)
        for b in range(nb):
            n = t * nb + b
            prev = pl.multiple_of(jnp.maximum(n - 1, 0) * BLOCK, BLOCK)
            cur = pl.multiple_of(n * BLOCK, BLOCK)
            rows = slice(b * BLOCK, (b + 1) * BLOCK)
            mask = _band_mask_kq(n)
            for gidx in range(2):
                lanes = slice(128 * gidx, 128 * (gidx + 1))
                qs = _stack_heads(q_ref[rows, 256 * gidx:256 * gidx + 128],
                                  q_ref[rows, 256 * gidx + 128:256 * gidx + 256])
                kb = jnp.concatenate([k_ref[pl.ds(prev, BLOCK), lanes], k_ref[pl.ds(cur, BLOCK), lanes]], axis=0)
                vb = jnp.concatenate([v_ref[pl.ds(prev, BLOCK), lanes], v_ref[pl.ds(cur, BLOCK), lanes]], axis=0)
                st = lax.dot_general(kb, qs, NT, preferred_element_type=F32)
                st = jnp.where(mask, st, NEG)
                sink = sink_ref[gidx]
                m = jnp.maximum(jnp.max(st, axis=0, keepdims=True), sink)
                e = jnp.exp(st - m)
                inv = 1.0 / (jnp.sum(e, axis=0, keepdims=True) + jnp.exp(sink - m))
                o = lax.dot_general((e * inv).astype(CDT), vb, TN, preferred_element_type=F32)
                o0, o1 = _unstack_heads(o)
                o_ref[rows, 256 * gidx:256 * gidx + 128] = o0.astype(CDT)
                o_ref[rows, 256 * gidx + 128:256 * gidx + 256] = o1.astype(CDT)

    return _hosted(comm)(
        body, name="attn_fwd",
        grid=(S // tq,),
        in_specs=[pl.BlockSpec((tq, 512), lambda t: (t, 0)),
                  pl.BlockSpec((S, 256), lambda t: (0, 0)),
                  pl.BlockSpec((S, 256), lambda t: (0, 0)),
                  pl.BlockSpec((2, 1, 4 * BLOCK), lambda t: (0, 0, 0))],
        out_specs=pl.BlockSpec((tq, 512), lambda t: (t, 0)),
        out_shape=jax.ShapeDtypeStruct((S, 512), CDT),
        compiler_params=_params(1),
    )(q, k, v, sink_row)


def attn_bwd(q, k, v, do, sink_row, *, nb=4, comm=None):
    S = q.shape[0]
    tq = nb * BLOCK
    scale = HEAD_DIM ** -0.5

    def body(q_ref, k_ref, v_ref, do_ref, sink_ref, dq_ref, dk_ref, dv_ref, dsink_ref):
        t = pl.program_id(0)

        @pl.when(t == 0)
        def _():
            dk_ref[...] = jnp.zeros_like(dk_ref)
            dv_ref[...] = jnp.zeros_like(dv_ref)
            dsink_ref[...] = jnp.zeros_like(dsink_ref)

        for b in range(nb):
            n = t * nb + b
            prev = pl.multiple_of(jnp.maximum(n - 1, 0) * BLOCK, BLOCK)
            cur = pl.multiple_of(n * BLOCK, BLOCK)
            rows = slice(b * BLOCK, (b + 1) * BLOCK)
            mask = _band_mask_kq(n)
            for gidx in range(2):
                lanes = slice(128 * gidx, 128 * (gidx + 1))
                qs = _stack_heads(q_ref[rows, 256 * gidx:256 * gidx + 128],
                                  q_ref[rows, 256 * gidx + 128:256 * gidx + 256])
                dos = _stack_heads(do_ref[rows, 256 * gidx:256 * gidx + 128],
                                   do_ref[rows, 256 * gidx + 128:256 * gidx + 256])
                kb = jnp.concatenate([k_ref[pl.ds(prev, BLOCK), lanes], k_ref[pl.ds(cur, BLOCK), lanes]], axis=0)
                vb = jnp.concatenate([v_ref[pl.ds(prev, BLOCK), lanes], v_ref[pl.ds(cur, BLOCK), lanes]], axis=0)
                st = lax.dot_general(kb, qs, NT, preferred_element_type=F32)
                st = jnp.where(mask, st, NEG)
                sink = sink_ref[gidx]
                m = jnp.maximum(jnp.max(st, axis=0, keepdims=True), sink)
                e = jnp.exp(st - m)
                es = jnp.exp(sink - m)
                inv = 1.0 / (jnp.sum(e, axis=0, keepdims=True) + es)
                pt = e * inv
                dpt = lax.dot_general(vb, dos, NT, preferred_element_type=F32)
                delta = jnp.sum(pt * dpt, axis=0, keepdims=True)
                dst = (pt * (dpt - delta)).astype(CDT)
                dsink_ref[gidx] += -(es * inv) * delta
                dvb = jnp.dot(pt.astype(CDT), dos, preferred_element_type=F32)
                dkb = jnp.dot(dst, qs, preferred_element_type=F32)
                dqs = lax.dot_general(dst, kb, TN, preferred_element_type=F32) * scale
                dq0, dq1 = _unstack_heads(dqs)
                dq_ref[rows, 256 * gidx:256 * gidx + 128] = dq0
                dq_ref[rows, 256 * gidx + 128:256 * gidx + 256] = dq1
                dk_ref[pl.ds(prev, BLOCK), lanes] += dkb[0:BLOCK]
                dk_ref[pl.ds(cur, BLOCK), lanes] += dkb[BLOCK:2 * BLOCK]
                dv_ref[pl.ds(prev, BLOCK), lanes] += dvb[0:BLOCK]
                dv_ref[pl.ds(cur, BLOCK), lanes] += dvb[BLOCK:2 * BLOCK]

    full = pl.BlockSpec((S, 256), lambda t: (0, 0))
    tile = pl.BlockSpec((tq, 512), lambda t: (t, 0))
    srow = pl.BlockSpec((2, 1, 4 * BLOCK), lambda t: (0, 0, 0))
    return _hosted(comm)(
        body, name="attn_bwd",
        grid=(S // tq,),
        in_specs=[tile, full, full, tile, srow],
        out_specs=[tile, full, full, srow],
        out_shape=[jax.ShapeDtypeStruct((S, 512), F32), jax.ShapeDtypeStruct((S, 256), F32),
                   jax.ShapeDtypeStruct((S, 256), F32), jax.ShapeDtypeStruct((2, 1, 4 * BLOCK), F32)],
        compiler_params=_params(1),
    )(q, k, v, do, sink_row)


def _glu(u):
    a = u[:, 0:CONV_C]
    gt = u[:, CONV_C:2 * CONV_C]
    sg = _sigmoid(gt)
    return a, sg, a * sg


CONV_CHUNK = 32


def _shifted_copies(buf, shifted, n):
    for r in range(1, 8):
        shifted[r - 1, 0:n, :] = buf[r:r + n, :]


def _shifted_rows(buf, shifted, start, rows):
    r = start % 8
    if r == 0:
        return buf[start:start + rows, :]
    return shifted[r - 1, start - r:start - r + rows, :]


def conv_fwd(u, cw, cb, lg, lb, *, tm=512, comm=None):
    S = u.shape[0]
    nh = tm // HALO

    def body(u_ref, uh_ref, cw_ref, cb_ref, lg_ref, lb_ref, o_ref, y_ref, hbuf, hsh):
        t = pl.program_id(0)
        _, _, hg = _glu(u_ref[...])
        _, _, hh = _glu(uh_ref[...])
        hbuf[0:HALO, :] = jnp.where(t > 0, hh, jnp.zeros_like(hh))
        hbuf[HALO:HALO + tm, :] = hg
        hbuf[HALO + tm:HALO + tm + 8, :] = jnp.zeros((8, CONV_C), F32)
        _shifted_copies(hbuf, hsh, HALO + tm)
        off = HALO - (CONV_K - 1)
        for c0 in range(0, tm, CONV_CHUNK):
            acc = jnp.zeros((CONV_CHUNK, CONV_C), F32) + cb_ref[...]
            for j in range(CONV_K):
                acc = acc + cw_ref[j:j + 1, :] * _shifted_rows(hbuf, hsh, c0 + off + j, CONV_CHUNK)
            y_ref[c0:c0 + CONV_CHUNK, :] = acc
        y = y_ref[...]
        yc = y - jnp.mean(y, axis=-1, keepdims=True)
        r = lax.rsqrt(jnp.mean(yc * yc, axis=-1, keepdims=True) + EPS)
        z = yc * r * lg_ref[...] + lb_ref[...]
        o_ref[...] = (z * _sigmoid(z)).astype(CDT)

    vec = pl.BlockSpec((1, CONV_C), lambda t: (0, 0))
    return _hosted(comm)(
        body, name="conv_fwd",
        grid=(S // tm,),
        in_specs=[pl.BlockSpec((tm, 2 * CONV_C), lambda t: (t, 0)),
                  pl.BlockSpec((HALO, 2 * CONV_C), lambda t: (jnp.maximum(t * nh - 1, 0), 0)),
                  pl.BlockSpec((CONV_K, CONV_C), lambda t: (0, 0)), vec, vec, vec],
        out_specs=[pl.BlockSpec((tm, CONV_C), lambda t: (t, 0)), pl.BlockSpec((tm, CONV_C), lambda t: (t, 0))],
        out_shape=[jax.ShapeDtypeStruct((S, CONV_C), CDT), jax.ShapeDtypeStruct((S, CONV_C), F32)],
        scratch_shapes=[pltpu.VMEM((HALO + tm + 8, CONV_C), F32), pltpu.VMEM((7, HALO + tm, CONV_C), F32)],
        compiler_params=_params(1),
    )(u, u, cw, cb, lg, lb)


def conv_bwd(dc, u, y, cw, lg, lb, *, tm=512, comm=None):
    S = u.shape[0]
    nh = tm // HALO
    nt = S // tm

    def ln_bwd(dcv, yv, lgv, lbv):
        yc = yv - jnp.mean(yv, axis=-1, keepdims=True)
        r = lax.rsqrt(jnp.mean(yc * yc, axis=-1, keepdims=True) + EPS)
        yhat = yc * r
        z = yhat * lgv + lbv
        sg = _sigmoid(z)
        dz = dcv * (sg * (1.0 + z * (1.0 - sg)))
        dyhat = dz * lgv
        dy = r * (dyhat - jnp.mean(dyhat, axis=-1, keepdims=True)
                  - yhat * jnp.mean(dyhat * yhat, axis=-1, keepdims=True))
        return dy, dz, yhat

    def body(dc_ref, dcn_ref, u_ref, uh_ref, y_ref, yn_ref, cw_ref, lg_ref, lb_ref,
             du_ref, dcw_ref, dcb_ref, dlg_ref, dlb_ref, hbuf, dybuf, dhg_sc, dw_sc, hsh, dysh):
        t = pl.program_id(0)

        @pl.when(t == 0)
        def _():
            dw_sc[...] = jnp.zeros_like(dw_sc)
            dcb_ref[...] = jnp.zeros_like(dcb_ref)
            dlg_ref[...] = jnp.zeros_like(dlg_ref)
            dlb_ref[...] = jnp.zeros_like(dlb_ref)

        lgv, lbv = lg_ref[...], lb_ref[...]
        dy, dz, yhat = ln_bwd(dc_ref[...].astype(F32), y_ref[...], lgv, lbv)
        dyn, _, _ = ln_bwd(dcn_ref[...].astype(F32), yn_ref[...], lgv, lbv)
        dlb_ref[...] += jnp.sum(dz, axis=0, keepdims=True)
        dlg_ref[...] += jnp.sum(dz * yhat, axis=0, keepdims=True)
        dcb_ref[...] += jnp.sum(dy, axis=0, keepdims=True)
        dybuf[0:tm, :] = dy
        dybuf[tm:tm + HALO, :] = jnp.where(t < nt - 1, dyn, jnp.zeros_like(dyn))
        dybuf[tm + HALO:tm + HALO + 8, :] = jnp.zeros((8, CONV_C), F32)
        _shifted_copies(dybuf, dysh, tm + HALO)

        a, sg, hg = _glu(u_ref[...])
        _, _, hh = _glu(uh_ref[...])
        hbuf[0:HALO, :] = jnp.where(t > 0, hh, jnp.zeros_like(hh))
        hbuf[HALO:HALO + tm, :] = hg
        hbuf[HALO + tm:HALO + tm + 8, :] = jnp.zeros((8, CONV_C), F32)
        _shifted_copies(hbuf, hsh, HALO + tm)

        off = HALO - (CONV_K - 1)
        for c0 in range(0, tm, CONV_CHUNK):
            acc = jnp.zeros((CONV_CHUNK, CONV_C), F32)
            dyc = dybuf[c0:c0 + CONV_CHUNK, :]
            for j in range(CONV_K):
                acc = acc + cw_ref[j:j + 1, :] * _shifted_rows(dybuf, dysh, c0 + (CONV_K - 1) - j, CONV_CHUNK)
                prod = dyc * _shifted_rows(hbuf, hsh, c0 + off + j, CONV_CHUNK)
                dw_sc[j] += jnp.sum(prod.reshape(CONV_CHUNK // 8, 8, CONV_C), axis=0)
            dhg_sc[c0:c0 + CONV_CHUNK, :] = acc

        dhg = dhg_sc[...]
        du_ref[:, 0:CONV_C] = dhg * sg
        du_ref[:, CONV_C:2 * CONV_C] = dhg * a * sg * (1.0 - sg)

        @pl.when(t == nt - 1)
        def _():
            dcw_ref[...] = jnp.sum(dw_sc[...], axis=1)

    vec = pl.BlockSpec((1, CONV_C), lambda t: (0, 0))
    tile = pl.BlockSpec((tm, CONV_C), lambda t: (t, 0))
    nxt = pl.BlockSpec((HALO, CONV_C), lambda t: (jnp.minimum((t + 1) * nh, S // HALO - 1), 0))
    return _hosted(comm)(
        body, name="conv_bwd",
        grid=(nt,),
        in_specs=[tile, nxt,
                  pl.BlockSpec((tm, 2 * CONV_C), lambda t: (t, 0)),
                  pl.BlockSpec((HALO, 2 * CONV_C), lambda t: (jnp.maximum(t * nh - 1, 0), 0)),
                  tile, nxt,
                  pl.BlockSpec((CONV_K, CONV_C), lambda t: (0, 0)), vec, vec],
        out_specs=[pl.BlockSpec((tm, 2 * CONV_C), lambda t: (t, 0)),
                   pl.BlockSpec((CONV_K, CONV_C), lambda t: (0, 0)), vec, vec, vec],
        out_shape=[jax.ShapeDtypeStruct((S, 2 * CONV_C), F32), jax.ShapeDtypeStruct((CONV_K, CONV_C), F32),
                   jax.ShapeDtypeStruct((1, CONV_C), F32), jax.ShapeDtypeStruct((1, CONV_C), F32),
                   jax.ShapeDtypeStruct((1, CONV_C), F32)],
        scratch_shapes=[pltpu.VMEM((HALO + tm + 8, CONV_C), F32), pltpu.VMEM((tm + HALO + 8, CONV_C), F32),
                        pltpu.VMEM((tm, CONV_C), F32), pltpu.VMEM((CONV_K, 8, CONV_C), F32),
                        pltpu.VMEM((7, HALO + tm, CONV_C), F32), pltpu.VMEM((7, tm + HALO, CONV_C), F32)],
        compiler_params=_params(1),
    )(dc, dc, u, u, y, y, cw, lg, lb)


def outproj_fwd(x, ao, co, wout, *, tm=512):
    S, D = x.shape

    def body(x_ref, a_ref, c_ref, w_ref, o_ref):
        o_ref[...] = (x_ref[...]
                      + jnp.dot(a_ref[...], w_ref[0:ATT_W, :], preferred_element_type=F32)
                      + jnp.dot(c_ref[...], w_ref[ATT_W:ATT_W + CONV_C, :], preferred_element_type=F32))

    return pl.pallas_call(
        body, name="outproj_fwd",
        grid=(S // tm,),
        in_specs=[pl.BlockSpec((tm, D), lambda i: (i, 0)), pl.BlockSpec((tm, ATT_W), lambda i: (i, 0)),
                  pl.BlockSpec((tm, CONV_C), lambda i: (i, 0)), pl.BlockSpec((D, D), lambda i: (0, 0))],
        out_specs=pl.BlockSpec((tm, D), lambda i: (i, 0)),
        out_shape=jax.ShapeDtypeStruct((S, D), F32),
        compiler_params=_params(1),
    )(x, ao, co, wout)


def outproj_bwd(dx, ao, co, wout, *, tm=512, comm=None):
    S, D = dx.shape

    def body(dx_ref, a_ref, c_ref, w_ref, da_ref, dc_ref, dw_ref):
        @pl.when(pl.program_id(0) == 0)
        def _():
            dw_ref[...] = jnp.zeros_like(dw_ref)

        dxb = dx_ref[...].astype(CDT)
        da_ref[...] = lax.dot_general(dxb, w_ref[0:ATT_W, :], NT, preferred_element_type=F32).astype(CDT)
        dc_ref[...] = lax.dot_general(dxb, w_ref[ATT_W:ATT_W + CONV_C, :], NT, preferred_element_type=F32)
        dw_ref[0:ATT_W, :] += lax.dot_general(a_ref[...], dxb, TN, preferred_element_type=F32)
        dw_ref[ATT_W:ATT_W + CONV_C, :] += lax.dot_general(c_ref[...], dxb, TN, preferred_element_type=F32)

    return _hosted(comm)(
        body, name="outproj_bwd",
        grid=(S // tm,),
        in_specs=[pl.BlockSpec((tm, D), lambda i: (i, 0)), pl.BlockSpec((tm, ATT_W), lambda i: (i, 0)),
                  pl.BlockSpec((tm, CONV_C), lambda i: (i, 0)), pl.BlockSpec((D, D), lambda i: (0, 0))],
        out_specs=[pl.BlockSpec((tm, ATT_W), lambda i: (i, 0)), pl.BlockSpec((tm, CONV_C), lambda i: (i, 0)),
                   pl.BlockSpec((D, D), lambda i: (0, 0))],
        out_shape=[jax.ShapeDtypeStruct((S, ATT_W), CDT), jax.ShapeDtypeStruct((S, CONV_C), F32),
                   jax.ShapeDtypeStruct((D, D), F32)],
        compiler_params=_params(1),
    )(dx, ao, co, wout)


def inproj_bwd(x, g, dres, dq, dk, dv, du, wextT, cs, sn, *, tm=512):
    S, D = x.shape

    def body(x_ref, g_ref, dres_ref, dq_ref, dk_ref, dv_ref, du_ref, w_ref, cs_ref, sn_ref,
             dx_ref, dw_ref, dg_ref, dp_sc):
        @pl.when(pl.program_id(0) == 0)
        def _():
            dw_ref[...] = jnp.zeros_like(dw_ref)
            dg_ref[...] = jnp.zeros_like(dg_ref)

        c, s = cs_ref[...], sn_ref[...]
        for b in range(4):
            dp_sc[:, 128 * b:128 * (b + 1)] = _rope_transpose(dq_ref[:, 128 * b:128 * (b + 1)], c, s).astype(CDT)
        for b in range(2):
            dp_sc[:, 512 + 128 * b:512 + 128 * (b + 1)] = _rope_transpose(
                dk_ref[:, 128 * b:128 * (b + 1)], c, s).astype(CDT)
        dp_sc[:, 768:1024] = dv_ref[...].astype(CDT)
        dp_sc[:, 1024:2048] = du_ref[...].astype(CDT)
        dp = dp_sc[...]
        xhat, r = _rms_parts(x_ref[...])
        gg = g_ref[...]
        h = (xhat * gg).astype(CDT)
        dh = jnp.dot(dp, w_ref[...], preferred_element_type=F32)
        dw_ref[...] += lax.dot_general(dp, h, TN, preferred_element_type=F32)
        dx, dg = _rms_bwd(dh, xhat, r, gg)
        dx_ref[...] = dres_ref[...] + dx
        dg_ref[...] += dg

    def row(w):
        return pl.BlockSpec((tm, w), lambda i: (i, 0))

    return pl.pallas_call(
        body, name="inproj_bwd",
        grid=(S // tm,),
        in_specs=[row(D), pl.BlockSpec((1, D), lambda i: (0, 0)), row(D), row(512), row(256), row(256),
                  row(1024), pl.BlockSpec((D_EXT, D), lambda i: (0, 0), pipeline_mode=pl.Buffered(1)),
                  row(128), row(128)],
        out_specs=[row(D), pl.BlockSpec((D_EXT, D), lambda i: (0, 0), pipeline_mode=pl.Buffered(1)),
                   pl.BlockSpec((1, D), lambda i: (0, 0))],
        out_shape=[jax.ShapeDtypeStruct((S, D), F32), jax.ShapeDtypeStruct((D_EXT, D), F32),
                   jax.ShapeDtypeStruct((1, D), F32)],
        scratch_shapes=[pltpu.VMEM((tm, D_EXT), CDT)],
        compiler_params=_params(1),
    )(x, g, dres, dq, dk, dv, du, wextT, cs, sn)


def _rope_tables(positions):
    inv_freq = 1.0 / (10000.0 ** (jnp.arange(0, HEAD_DIM, 2, dtype=F32) / HEAD_DIM))
    ang = positions.astype(F32).reshape(-1, 1) * inv_freq
    cos, sin = jnp.cos(ang), jnp.sin(ang)
    cs = jnp.tile(jnp.concatenate([cos, cos], axis=-1), (1, 2))
    sn = jnp.tile(jnp.concatenate([-sin, sin], axis=-1), (1, 2))
    return cs, sn


def _widen_w_in(w):
    q, u = w[0:512], w[768:1792]
    parts = [q]
    for base in (512, 576, 640, 704):
        parts += [w[base:base + 64], w[base:base + 64]]
    return jnp.concatenate(parts + [u], axis=0)


def _fold_w_in(d):
    parts = [d[0:512]]
    for base in (512, 640, 768, 896):
        parts.append(d[base:base + 64] + d[base + 64:base + 128])
    return jnp.concatenate(parts + [d[1024:2048]], axis=0)


def add_halves(g5s, r1s, c_idx):
    n = len(g5s)

    def body(c_ref, *refs):
        for a in range(n):
            refs[2 * n + a][...] = (refs[a][...] + refs[n + a][...]).astype(CDT)

    def g_spec(g):
        return pl.BlockSpec((2, None) + g.shape[2:], lambda s, cr: (s, cr[0], 0, 0))

    def r_spec(g):
        return pl.BlockSpec((2,) + g.shape[2:], lambda s, cr: (s, 0, 0))

    return list(pl.pallas_call(
        body, name="add_halves",
        grid_spec=pltpu.PrefetchScalarGridSpec(
            num_scalar_prefetch=1, grid=(N_CHIPS // 2,),
            in_specs=[g_spec(g) for g in g5s] + [r_spec(g) for g in g5s],
            out_specs=[r_spec(g) for g in g5s]),
        out_shape=[jax.ShapeDtypeStruct((N_CHIPS,) + g.shape[2:], CDT) for g in g5s],
        compiler_params=_params(1),
    )(c_idx, *g5s, *r1s))


def sum_partials(parts, recv3s, j_idx):
    n = len(parts)

    def body(j_ref, *refs):
        for a in range(n):
            p_ref, r_ref = refs[a], refs[n + a]
            refs[2 * n + a][...] = ((p_ref[...].astype(F32) + r_ref[0].astype(F32))
                                    + r_ref[1].astype(F32)) + r_ref[2].astype(F32)

    def half(p):
        return p.shape[1] // 2

    return list(pl.pallas_call(
        body, name="sum_partials",
        grid_spec=pltpu.PrefetchScalarGridSpec(
            num_scalar_prefetch=1, grid=(2,),
            in_specs=[pl.BlockSpec((None, half(p), p.shape[2]), lambda i, jr: (jr[0], i, 0)) for p in parts]
            + [pl.BlockSpec((3, half(p), p.shape[2]), lambda i, jr: (0, i, 0)) for p in parts],
            out_specs=[pl.BlockSpec((half(p), p.shape[2]), lambda i, jr: (i, 0)) for p in parts]),
        out_shape=[jax.ShapeDtypeStruct(p.shape[1:], F32) for p in parts],
        compiler_params=_params(1),
    )(j_idx, *parts, *recv3s))


class _Chain:
    STAGES = ("swap", "xchg", "share")

    def __init__(self, grads, c_idx, j_idx):
        self.c_idx, self.j_idx = c_idx, j_idx
        self.g5 = [g.reshape(N_CHIPS, 2, g.shape[0] // (2 * N_CHIPS), g.shape[1]) for g in grads]
        self.stage_no = 0

    @property
    def done(self):
        return self.stage_no == len(self.STAGES)

    def next_stage(self):
        name = self.STAGES[self.stage_no]

        def callback(res):
            getattr(self, "after_" + name)(res)
            self.stage_no += 1

        return getattr(self, name)(), callback

    def swap(self):
        return swap_op(self.g5)

    def after_swap(self, recv):
        self.parts = add_halves(self.g5, recv, self.c_idx)

    def xchg(self):
        return exchange_op(self.parts)

    def after_xchg(self, recv):
        self.totals = sum_partials(self.parts, recv, self.j_idx)

    def share(self):
        return share_op(self.totals)

    def after_share(self, recv):
        both = [_own_slab(h, t, self.c_idx[0]) for h, t in zip(recv, self.totals)]
        self.final = [h.reshape(2 * h.shape[1], h.shape[2]) for h in both]


def all_reduce_small(vec):
    R = vec.shape[0]

    def body(v_ref, o_ref, buf, send, recv):
        x, y, c = _coords()
        me = 4 * x + 2 * y + c
        buf[me] = v_ref[...]
        cps = []
        for m in range(1, N_DEV):
            dx, dy, dc = (m >> 2) & 1, (m >> 1) & 1, m & 1
            cp = pltpu.make_async_remote_copy(v_ref, buf.at[me], send.at[m - 1], recv.at[m - 1],
                                              device_id=((x + dx) % 2, (y + dy) % 2, (c + dc) % 2),
                                              device_id_type=MESH)
            cp.start()
            cps.append(cp)
        for cp in cps:
            cp.wait()
        acc = buf[0]
        for d in range(1, N_DEV):
            acc = acc + buf[d]
        o_ref[...] = acc

    return pl.pallas_call(
        body, name="all_reduce_small",
        in_specs=[pl.BlockSpec(memory_space=pltpu.VMEM)], out_specs=pl.BlockSpec(memory_space=pltpu.VMEM),
        out_shape=jax.ShapeDtypeStruct(vec.shape, F32),
        scratch_shapes=[pltpu.VMEM((N_DEV, R, 128), F32), pltpu.SemaphoreType.DMA((N_DEV - 1,)),
                        pltpu.SemaphoreType.DMA((N_DEV - 1,))],
    )(vec)


def adamw(w, g, m, v, *, tm=512):
    R, C = w.shape
    tm = max(t for t in range(8, min(tm, R) + 1, 8) if R % t == 0)
    c1 =1.0 - ADAM_B1 ** ADAM_STEP
    c2 = 1.0 - ADAM_B2 ** ADAM_STEP

    def body(w_ref, g_ref, m_ref, v_ref, d_ref, nm_ref, nv_ref):
        gg = g_ref[...]
        nm = ADAM_B1 * m_ref[...] + (1.0 - ADAM_B1) * gg
        nv = ADAM_B2 * v_ref[...] + (1.0 - ADAM_B2) * (gg * gg)
        nm_ref[...] = nm
        nv_ref[...] = nv
        d_ref[...] = -ADAM_LR * ((nm / c1) / (jnp.sqrt(nv / c2) + ADAM_EPS) + ADAM_WD * w_ref[...])

    blk = pl.BlockSpec((tm, C), lambda i: (i, 0))
    return pl.pallas_call(
        body, name="adamw",
        grid=(pl.cdiv(R, tm),),
        in_specs=[blk] * 4, out_specs=[blk] * 3,
        out_shape=[jax.ShapeDtypeStruct((R, C), F32)] * 3,
        compiler_params=_params(1),
    )(w, g, m, v)


def adamw_layers(w, m, v, g_layers, *, tm=352):
    L, R, C = w.shape
    tm = max(t for t in range(8, min(tm, R) + 1, 8) if R % t == 0)
    c1 = 1.0 - ADAM_B1 ** ADAM_STEP
    c2 = 1.0 - ADAM_B2 ** ADAM_STEP

    def body(w_ref, m_ref, v_ref, *rest):
        g_refs, (go_ref, d_ref, nm_ref, nv_ref) = rest[:L], rest[L:]
        layer = pl.program_id(0)
        gg = g_refs[0][...]
        for l in range(1, L):
            gg = jnp.where(layer == l, g_refs[l][...], gg)
        nm = ADAM_B1 * m_ref[...] + (1.0 - ADAM_B1) * gg
        nv = ADAM_B2 * v_ref[...] + (1.0 - ADAM_B2) * (gg * gg)
        go_ref[...] = gg
        nm_ref[...] = nm
        nv_ref[...] = nv
        d_ref[...] = -ADAM_LR * ((nm / c1) / (jnp.sqrt(nv / c2) + ADAM_EPS) + ADAM_WD * w_ref[...])

    blk = pl.BlockSpec((None, tm, C), lambda l, i: (l, i, 0))
    gblk = pl.BlockSpec((tm, C), lambda l, i: (i, 0))
    return pl.pallas_call(
        body, name="adamw_layers",
        grid=(L, R // tm),
        in_specs=[blk] * 3 + [gblk] * L, out_specs=[blk] * 4,
        out_shape=[jax.ShapeDtypeStruct((L, R, C), F32)] * 4,
        compiler_params=_params(2),
    )(w, m, v, *g_layers)


_SMALL = (("n1", (2, D_MODEL)), ("nm", (2, D_MODEL)), ("n2", (2, D_MODEL)), ("nf", (D_MODEL,)),
          ("cb", (2, CONV_C)), ("lg", (2, CONV_C)), ("lb", (2, CONV_C)), ("sinks", (2, N_HEADS)),
          ("cw", (2, CONV_K, CONV_C)))


def _pack(parts, rows):
    flat = jnp.concatenate([p.reshape(-1).astype(F32) for p in parts])
    return jnp.pad(flat, (0, rows * 128 - flat.shape[0])).reshape(rows, 128)


def _unpack(block, shapes):
    flat = block.reshape(-1)
    out, o = [], 0
    for shp in shapes:
        n = 1
        for s in shp:
            n *= s
        out.append(flat[o:o + n].reshape(shp))
        o += n
    return out


def kernel(x, positions, ffn1_norm, ffn1_w_gate, ffn1_w_up, ffn1_w_down, mix_norm, w_in, conv_w, conv_b, conv_ln_g, conv_ln_b, attn_sinks, w_out, ffn2_norm, ffn2_w_gate, ffn2_w_up, ffn2_w_down, final_norm, loss_target, m_ffn1_norm, m_ffn1_w_gate, m_ffn1_w_up, m_ffn1_w_down, m_mix_norm, m_w_in, m_conv_w, m_conv_b, m_conv_ln_g, m_conv_ln_b, m_attn_sinks, m_w_out, m_ffn2_norm, m_ffn2_w_gate, m_ffn2_w_up, m_ffn2_w_down, m_final_norm, v_ffn1_norm, v_ffn1_w_gate, v_ffn1_w_up, v_ffn1_w_down, v_mix_norm, v_w_in, v_conv_w, v_conv_b, v_conv_ln_g, v_conv_ln_b, v_attn_sinks, v_w_out, v_ffn2_norm, v_ffn2_w_gate, v_ffn2_w_up, v_ffn2_w_down, v_final_norm):
    cx, cy, cc = _coords()
    chip = 2 * cx + cy
    c_idx = jnp.reshape(cc, (1,)).astype(jnp.int32)
    j_idx = jnp.reshape(chip, (1,)).astype(jnp.int32)
    L = ffn1_norm.shape[0]
    tr = lambda a: jnp.swapaxes(a, 1, 2)

    sh = dict(f1g=tr(ffn1_w_gate), f1u=tr(ffn1_w_up), f1d=ffn1_w_down, f2g=tr(ffn2_w_gate),
              f2u=tr(ffn2_w_up), f2d=ffn2_w_down, win=tr(w_in), wout=w_out)
    sh = {k: [v[l].astype(CDT) for l in range(L)] for k, v in sh.items()}
    W = {}

    def gather_op(keys):
        if keys == ["cw"]:
            return ag_op([conv_w])
        return ag2_op([sh[k[0]][k[1]] for k in keys])

    def take(keys, res):
        for k, a in zip(keys, res):
            if k == "cw":
                W[k] = _own_slab(a, conv_w, chip)
            else:
                mine = sh[k[0]][k[1]]
                W[k] = _own_slab(a, mine.reshape(a.shape[1:]), chip).reshape(N_CHIPS * mine.shape[0], mine.shape[1])

    def with_ag(fn, keys, *args):
        if not keys:
            return fn(*args)
        main, res = fn(*args, comm=gather_op(keys))
        take(keys, res)
        return main

    ag_hosts = {("ffn1", 0): [("win", 0), ("f2g", 0), ("f1d", 1)],
                ("inproj", 0): ["cw"], ("attn", 0): [("wout", 0), ("f2u", 0)], ("conv", 0): [("f2d", 0)],
                ("ffn2", 0): [("f1g", 1), ("f1u", 1)],
                ("ffn1", 1): [("win", 1), ("f2g", 1), ("f2u", 1)],
                ("inproj", 1): [("wout", 1)], ("attn", 1): [("f2d", 1)]}
    first = [("f1g", 0), ("f1u", 0), ("f1d", 0)]
    take(first, first_gather([sh[k_][l] for k_, l in first]))

    cs, sn = _rope_tables(positions)
    saved = []
    h = x[0]
    for l in range(L):
        sink = attn_sinks[l].reshape(2, 4)
        sink_row = jnp.repeat(sink, BLOCK, axis=1).reshape(2, 1, 4 * BLOCK)
        x0 = h
        x1, g1, u1 = with_ag(ffn_fwd, ag_hosts.get(("ffn1", l)), x0, ffn1_norm[l][None],
                             W[("f1g", l)], W[("f1u", l)], W[("f1d", l)])
        wext = _widen_w_in(W[("win", l)])
        q, k, v, u = with_ag(inproj_fwd, ag_hosts.get(("inproj", l)), x1, mix_norm[l][None], wext, cs, sn)
        ao = with_ag(attn_fwd, ag_hosts.get(("attn", l)), q, k, v, sink_row)
        cwl = jnp.transpose(W["cw"][:, l], (1, 0, 2)).reshape(CONV_K, CONV_C)
        co, yc = with_ag(conv_fwd, ag_hosts.get(("conv", l)), u, cwl, conv_b[l][None], conv_ln_g[l][None],
                         conv_ln_b[l][None])
        x2 = outproj_fwd(x1, ao, co, W[("wout", l)])
        x3, g2, u2 = with_ag(ffn_fwd, ag_hosts.get(("ffn2", l)), x2, ffn2_norm[l][None],
                             W[("f2g", l)], W[("f2u", l)], W[("f2d", l)])
        saved.append((x0, x1, x2, g1, u1, g2, u2, q, k, v, u, ao, co, yc, sink, wext, cwl))
        h = x3

    loss, dx, dnf = loss_head(h, final_norm[None], loss_target[0])

    active = []

    def advance(run):
        stages = [ch.next_stage() for ch in active]
        ops = [op for op, _ in stages]
        main, res = run(_merge(*ops) if ops else None)
        for (_, cb), r in zip(stages, _split(res, *ops)):
            cb(r)
        active[:] = [ch for ch in active if not ch.done]
        return main

    def hosted(fn, *args):
        def run(comm):
            if comm is None:
                return fn(*args), []
            return fn(*args, comm=comm)
        return advance(run)

    def chain(key, names_, grads):
        chains[key] = _Chain(grads, c_idx, j_idx)
        active.append(chains[key])
        for i_, nme_ in enumerate(names_):
            where[(nme_, key[1])] = (key, i_)

    small = {k_: [None] * L for k_ in ("n1", "nm", "n2", "cw", "cb", "lg", "lb", "sinks")}
    chains, where = {}, {}
    for l in reversed(range(L)):
        x0, x1, x2, g1, u1, g2, u2, q, k, v, u, ao, co, yc, sink, wext, cwl = saved[l]
        sink_row = jnp.repeat(sink, BLOCK, axis=1).reshape(2, 1, 4 * BLOCK)
        chain(("f2d", l), ["ffn2_w_down"], [hosted(ffn_wgrad_down, dx, g2, u2)])
        dx2, dgt, dup, hh, small["n2"][l] = hosted(
            ffn_bwd_dgrad, x2, ffn2_norm[l][None], dx, g2, u2, W[("f2g", l)], W[("f2u", l)], W[("f2d", l)])
        chain(("f2gu", l), ["ffn2_w_gate", "ffn2_w_up"], hosted(ffn_wgrad_rows, hh, [dgt, dup]))
        da, dc, gwout = hosted(outproj_bwd, dx2, ao, co, W[("wout", l)])
        du, small["cw"][l], small["cb"][l], small["lg"][l], small["lb"][l] = hosted(
            conv_bwd, dc, u, yc, cwl, conv_ln_g[l][None], conv_ln_b[l][None])
        dq, dk, dv, dsink = hosted(attn_bwd, q, k, v, da, sink_row)
        small["sinks"][l] = jnp.sum(dsink.reshape(2, 4, BLOCK), axis=-1).reshape(N_HEADS)
        dx1, gwext, small["nm"][l] = inproj_bwd(x1, mix_norm[l][None], dx2, dq, dk, dv, du, wext, cs, sn)
        chain(("mx", l), ["w_out", "w_in"], [gwout, _fold_w_in(gwext)])
        chain(("f1d", l), ["ffn1_w_down"], [hosted(ffn_wgrad_down, dx1, g1, u1)])
        dx, dgt, dup, hh, small["n1"][l] = (hosted if l > 0 else lambda fn, *a: fn(*a))(
            ffn_bwd_dgrad, x0, ffn1_norm[l][None], dx1, g1, u1, W[("f1g", l)], W[("f1u", l)], W[("f1d", l)])
        if l > 0:
            chain(("f1gu", l), ["ffn1_w_gate", "ffn1_w_up"], hosted(ffn_wgrad_rows, hh, [dgt, dup]))
        else:
            chain(("f1u", l), ["ffn1_w_up"], hosted(ffn_wgrad_rows, hh, [dup]))
            chain(("f1g", l), ["ffn1_w_gate"], hosted(ffn_wgrad_rows, hh, [dgt]))

    weights = dict(ffn1_norm=ffn1_norm, ffn1_w_gate=ffn1_w_gate, ffn1_w_up=ffn1_w_up, ffn1_w_down=ffn1_w_down,
                   mix_norm=mix_norm, w_in=w_in, conv_w=conv_w, conv_b=conv_b, conv_ln_g=conv_ln_g,
                   conv_ln_b=conv_ln_b, attn_sinks=attn_sinks, w_out=w_out, ffn2_norm=ffn2_norm,
                   ffn2_w_gate=ffn2_w_gate, ffn2_w_up=ffn2_w_up, ffn2_w_down=ffn2_w_down, final_norm=final_norm)
    moms = dict(ffn1_norm=(m_ffn1_norm, v_ffn1_norm), ffn1_w_gate=(m_ffn1_w_gate, v_ffn1_w_gate),
                ffn1_w_up=(m_ffn1_w_up, v_ffn1_w_up), ffn1_w_down=(m_ffn1_w_down, v_ffn1_w_down),
                mix_norm=(m_mix_norm, v_mix_norm), w_in=(m_w_in, v_w_in), conv_w=(m_conv_w, v_conv_w),
                conv_b=(m_conv_b, v_conv_b), conv_ln_g=(m_conv_ln_g, v_conv_ln_g),
                conv_ln_b=(m_conv_ln_b, v_conv_ln_b), attn_sinks=(m_attn_sinks, v_attn_sinks),
                w_out=(m_w_out, v_w_out), ffn2_norm=(m_ffn2_norm, v_ffn2_norm),
                ffn2_w_gate=(m_ffn2_w_gate, v_ffn2_w_gate), ffn2_w_up=(m_ffn2_w_up, v_ffn2_w_up),
                ffn2_w_down=(m_ffn2_w_down, v_ffn2_w_down), final_norm=(m_final_norm, v_final_norm))
    names = list(weights)
    big_names = dict(ffn2_w_gate=True, ffn2_w_up=True, ffn2_w_down=False, w_out=False, w_in=True,
                     ffn1_w_down=False, ffn1_w_up=True, ffn1_w_gate=True)
    grads, delta, new_m, new_v = {}, {}, {}, {}

    def big_adamw(nme):
        view = tr if big_names[nme] else (lambda a: a)
        res = adamw_layers(view(weights[nme]), view(moms[nme][0]), view(moms[nme][1]),
                           [chains[where[(nme, l)][0]].final[where[(nme, l)][1]] for l in range(L)])
        grads[nme], delta[nme], new_m[nme], new_v[nme] = [view(a) for a in res]

    while active:
        advance(lambda comm: (None, _run_comm(comm)))
    for nme in big_names:
        big_adamw(nme)

    G = {k_: jnp.stack(v_) for k_, v_ in small.items()}
    G["nf"] = dnf
    small_shapes = [shp for _, shp in _SMALL]
    n_small = 1 + sum(math.prod(s) for s in small_shapes)
    rows = -(-n_small // 1024) * 8
    packed = _pack([loss] + [G[k_] for k_, _ in _SMALL], rows)
    summed = all_reduce_small(packed)
    loss_out, *small_sum = _unpack(summed, [()] + small_shapes)
    gs = dict(zip([k_ for k_, _ in _SMALL], small_sum))
    gs["cw"] = lax.dynamic_slice_in_dim(gs["cw"], chip * (CONV_C // N_CHIPS), CONV_C // N_CHIPS, axis=2)
    grads.update(ffn1_norm=gs["n1"], mix_norm=gs["nm"], conv_w=gs["cw"], conv_b=gs["cb"], conv_ln_g=gs["lg"],
                 conv_ln_b=gs["lb"], attn_sinks=gs["sinks"], ffn2_norm=gs["n2"], final_norm=gs["nf"])

    small_names = [nme for nme in names if nme not in big_names]
    s_shapes = [weights[nme].shape for nme in small_names]
    n_tot = sum(math.prod(s) for s in s_shapes)
    srows = -(-n_tot // 1024) * 8
    d, nm_, nv_ = adamw(_pack([weights[nme] for nme in small_names], srows),
                        _pack([grads[nme] for nme in small_names], srows),
                        _pack([moms[nme][0] for nme in small_names], srows),
                        _pack([moms[nme][1] for nme in small_names], srows))
    for nme, dd, mm, vv in zip(small_names, _unpack(d, s_shapes), _unpack(nm_, s_shapes), _unpack(nv_, s_shapes)):
        delta[nme], new_m[nme], new_v[nme] = dd, mm, vv

    return (loss_out, dx[None], *[grads[nme] for nme in names], *[delta[nme] for nme in names],
            *[new_m[nme] for nme in names], *[new_v[nme] for nme in names])
```

```python
import math

import jax
import jax.numpy as jnp
from jax import lax
from jax.experimental import pallas as pl
from jax.experimental.pallas import tpu as pltpu

F32 = jnp.float32
CDT = jnp.bfloat16
D_MODEL = 1024
D_FF = 2816
N_HEADS = 8
HEAD_DIM = 64
BLOCK = 128
CONV_K = 31
CONV_C = 512
ATT_W = 512
D_EXT = 2048
EPS = 1e-5
HALO = 32
FF_CHUNK = 256
NEG = float(jnp.finfo(jnp.float32).min)
VMEM_LIMIT = 56 * 1024 * 1024

ADAM_LR = 0.001
ADAM_B1 = 0.9
ADAM_B2 = 0.999
ADAM_EPS = 1e-08
ADAM_WD = 0.01
ADAM_STEP = 10

NT = (((1,), (1,)), ((), ()))
TN = (((0,), (0,)), ((), ()))


MESH = pl.DeviceIdType.MESH
ANY = pl.BlockSpec(memory_space=pl.ANY)
N_CHIPS = 4
N_DEV = 8


def _params(n_axes):
    return pltpu.CompilerParams(dimension_semantics=("arbitrary",) * n_axes, vmem_limit_bytes=VMEM_LIMIT)


class _Comm:
    def __init__(self, name, inputs, out_shape, sems, descs, relay=None):
        self.name, self.inputs, self.out_shape, self.sems = name, list(inputs), list(out_shape), list(sems)
        self.descs, self.relay = descs, relay


def _merge(*ops):
    ops = [o for o in ops if o is not None]
    if len(ops) == 1:
        return ops[0]
    assert all(o.relay is None for o in ops)

    def descs(cins, couts, sems):
        out, i, o, s = [], 0, 0, 0
        for op in ops:
            ni, no, ns = len(op.inputs), len(op.out_shape), len(op.sems)
            out += op.descs(cins[i:i + ni], couts[o:o + no], sems[s:s + ns])
            i, o, s = i + ni, o + no, s + ns
        return out

    return _Comm("_".join(o.name for o in ops), sum((o.inputs for o in ops), []),
                 sum((o.out_shape for o in ops), []), sum((o.sems for o in ops), []), descs)


def _split(couts, *ops):
    res, o = [], 0
    for op in ops:
        res.append(couts[o:o + len(op.out_shape)])
        o += len(op.out_shape)
    return res


def _hosted(comm):
    if comm is None:
        return pl.pallas_call

    def make(body, *, name, grid, in_specs, out_specs, out_shape, compiler_params, scratch_shapes=()):
        single = not isinstance(out_shape, (list, tuple))
        o_specs = [out_specs] if single else list(out_specs)
        o_shape = [out_shape] if single else list(out_shape)
        n_in, n_out, n_sc = len(in_specs), len(o_specs), len(scratch_shapes)
        c_in, c_out = len(comm.inputs), len(comm.out_shape)

        def hosted(*refs):
            ins, cins = refs[:n_in], refs[n_in:n_in + c_in]
            o0 = n_in + c_in
            outs, couts = refs[o0:o0 + n_out], refs[o0 + n_out:o0 + n_out + c_out]
            s0 = o0 + n_out + c_out
            scr, sems = refs[s0:s0 + n_sc], refs[s0 + n_sc:]
            first = pl.program_id(0) == 0
            last = pl.program_id(0) == grid[0] - 1
            for ax in range(1, len(grid)):
                first = first & (pl.program_id(ax) == 0)
                last = last & (pl.program_id(ax) == grid[ax] - 1)

            @pl.when(first)
            def _():
                for d in comm.descs(cins, couts, sems):
                    d.start()

            if comm.relay is not None:
                assert len(grid) == 1

                @pl.when(pl.program_id(0) == (3 * grid[0]) // 4)
                def _():
                    for d in comm.descs(cins, couts, sems):
                        d.wait()
                    for d in comm.relay(cins, couts, sems):
                        d.start()

            body(*ins, *outs, *scr)

            @pl.when(last)
            def _():
                for d in (comm.relay or comm.descs)(cins, couts, sems):
                    d.wait()

        call = pl.pallas_call(
            hosted, name=f"{name}_{comm.name}", grid=grid,
            in_specs=list(in_specs) + [ANY] * c_in, out_specs=o_specs + [ANY] * c_out,
            out_shape=o_shape + comm.out_shape, scratch_shapes=list(scratch_shapes) + comm.sems,
            compiler_params=compiler_params)

        def run(*args):
            res = call(*args, *comm.inputs)
            return (res[0] if single else list(res[:n_out])), list(res[n_out:])

        return run

    return make


def _run_comm(comm):
    c_in = len(comm.inputs)

    def body(*refs):
        cins, couts, sems = refs[:c_in], refs[c_in:c_in + len(comm.out_shape)], refs[c_in + len(comm.out_shape):]
        ds = comm.descs(cins, couts, sems)
        for d in ds:
            d.start()
        for d in ds:
            d.wait()

    return list(pl.pallas_call(
        body, name=comm.name, in_specs=[ANY] * c_in, out_specs=[ANY] * len(comm.out_shape),
        out_shape=comm.out_shape, scratch_shapes=comm.sems)(*comm.inputs))


def _coords():
    return lax.axis_index("x"), lax.axis_index("y"), lax.axis_index("c")


def _other_chips(x, y):
    return [(1 - x, y), (x, 1 - y), (1 - x, 1 - y)]


def ag_op(shards):
    n = len(shards)

    def descs(cins, couts, sems):
        send, recv = sems
        x, y, c = _coords()
        j = 2 * x + y
        ds = []
        for a in range(n):
            for p, (px, py) in enumerate(_other_chips(x, y)):
                ds.append(pltpu.make_async_remote_copy(cins[a], couts[a].at[j], send.at[a, p], recv.at[a, p],
                                                       device_id=(px, py, c), device_id_type=MESH))
        return ds

    return _Comm("ag", shards, [jax.ShapeDtypeStruct((N_CHIPS,) + s.shape, s.dtype) for s in shards],
                 [pltpu.SemaphoreType.DMA((n, 3)), pltpu.SemaphoreType.DMA((n, 3))], descs)


def ag2_op(shards):
    n = len(shards)
    halves = [s.reshape(2, s.shape[0] // 2, s.shape[1]) for s in shards]

    def descs(cins, couts, sems):
        x, y, c = _coords()
        j = 2 * x + y
        return [pltpu.make_async_remote_copy(cins[a].at[c], couts[a].at[j, c], sems[0].at[a, p], sems[1].at[a, p],
                                             device_id=(px, py, c), device_id_type=MESH)
                for a in range(n) for p, (px, py) in enumerate(_other_chips(x, y))]

    def relay(cins, couts, sems):
        x, y, c = _coords()
        return [pltpu.make_async_remote_copy(couts[a].at[2 * px + py, c], couts[a].at[2 * px + py, c],
                                             sems[2].at[a, p], sems[3].at[a, p],
                                             device_id=(x, y, 1 - c), device_id_type=MESH)
                for a in range(n) for p, (px, py) in enumerate(_other_chips(x, y))]

    return _Comm("ag2", halves, [jax.ShapeDtypeStruct((N_CHIPS,) + h.shape, h.dtype) for h in halves],
                 [pltpu.SemaphoreType.DMA((n, 3))] * 4, descs, relay)


def _own_slab(gathered, mine, idx):
    return lax.dynamic_update_slice_in_dim(gathered, mine[None], idx, axis=0)


def first_gather(shards):
    n = len(shards)
    halves = [s.reshape(2, s.shape[0] // 2, s.shape[1]) for s in shards]

    def body(*refs):
        ins, outs = refs[:n], refs[n:2 * n]
        send1, recv1, send2, recv2 = refs[2 * n:]
        x, y, c = _coords()
        j = 2 * x + y
        chips = _other_chips(x, y)
        ici = [pltpu.make_async_remote_copy(ins[a].at[c], outs[a].at[j, c], send1.at[a, p], recv1.at[a, p],
                                            device_id=(px, py, c), device_id_type=MESH)
               for a in range(n) for p, (px, py) in enumerate(chips)]
        for d in ici:
            d.start()
        d2d = [pltpu.make_async_remote_copy(outs[a].at[2 * px + py, c], outs[a].at[2 * px + py, c],
                                            send2.at[a, p], recv2.at[a, p],
                                            device_id=(x, y, 1 - c), device_id_type=MESH)
               for a in range(n) for p, (px, py) in enumerate(chips)]
        for d1, d2 in zip(ici, d2d):
            d1.wait()
            d2.start()
        for d in d2d:
            d.wait()

    return list(pl.pallas_call(
        body, name="first_gather", in_specs=[ANY] * n, out_specs=[ANY] * n,
        out_shape=[jax.ShapeDtypeStruct((N_CHIPS,) + h.shape, h.dtype) for h in halves],
        scratch_shapes=[pltpu.SemaphoreType.DMA((n, 3))] * 4)(*halves))


def swap_op(grads):
    n = len(grads)

    def descs(cins, couts, sems):
        send, recv = sems
        x, y, c = _coords()
        return [pltpu.make_async_remote_copy(cins[a].at[:, 1 - c], couts[a], send.at[a], recv.at[a],
                                             device_id=(x, y, 1 - c), device_id_type=MESH) for a in range(n)]

    return _Comm("swap", grads, [jax.ShapeDtypeStruct(g.shape[:1] + g.shape[2:], g.dtype) for g in grads],
                 [pltpu.SemaphoreType.DMA((n,)), pltpu.SemaphoreType.DMA((n,))], descs)


def exchange_op(parts):
    n = len(parts)

    def descs(cins, couts, sems):
        send, recv = sems
        x, y, c = _coords()
        ds = []
        for a in range(n):
            for p, (px, py) in enumerate(_other_chips(x, y)):
                ds.append(pltpu.make_async_remote_copy(cins[a].at[2 * px + py], couts[a].at[p], send.at[a, p],
                                                       recv.at[a, p], device_id=(px, py, c), device_id_type=MESH))
        return ds

    return _Comm("xchg", parts, [jax.ShapeDtypeStruct((3,) + p.shape[1:], p.dtype) for p in parts],
                 [pltpu.SemaphoreType.DMA((n, 3)), pltpu.SemaphoreType.DMA((n, 3))], descs)


def share_op(totals):
    n = len(totals)

    def descs(cins, couts, sems):
        send, recv = sems
        x, y, c = _coords()
        return [pltpu.make_async_remote_copy(cins[a], couts[a].at[c], send.at[a], recv.at[a],
                                             device_id=(x, y, 1 - c), device_id_type=MESH) for a in range(n)]

    return _Comm("share", totals, [jax.ShapeDtypeStruct((2,) + t.shape, t.dtype) for t in totals],
                 [pltpu.SemaphoreType.DMA((n,)), pltpu.SemaphoreType.DMA((n,))], descs)


def _sigmoid(z):
    return 1.0 / (1.0 + jnp.exp(-z))


def _rms_parts(xf):
    r = lax.rsqrt(jnp.mean(xf * xf, axis=-1, keepdims=True) + EPS)
    return xf * r, r


def _rms_bwd(dh, xhat, r, g):
    dg = jnp.sum(dh * xhat, axis=0, keepdims=True)
    dxhat = dh * g
    dx = r * (dxhat - xhat * jnp.mean(dxhat * xhat, axis=-1, keepdims=True))
    return dx, dg


def _chunks(n, ck=FF_CHUNK):
    return [(c0, min(ck, n - c0)) for c0 in range(0, n, ck)]


def ffn_fwd(x, g, wgT, wuT, wd, *, tm=512, comm=None):
    S, D = x.shape
    F = wgT.shape[0]

    def body(x_ref, g_ref, wg_ref, wu_ref, wd_ref, o_ref, gate_ref, up_ref, a_sc):
        xf = x_ref[...]
        xhat, _ = _rms_parts(xf)
        h = (xhat * g_ref[...]).astype(CDT)
        for c0, cw_ in _chunks(F):
            sl = slice(c0, c0 + cw_)
            gt = lax.dot_general(h, wg_ref[sl, :], NT, preferred_element_type=F32)
            ut = lax.dot_general(h, wu_ref[sl, :], NT, preferred_element_type=F32)
            gate_ref[:, sl] = gt.astype(CDT)
            up_ref[:, sl] = ut.astype(CDT)
            a_sc[:, sl] = (gt * _sigmoid(gt) * ut).astype(CDT)
        o_ref[...] = xf + 0.5 * jnp.dot(a_sc[...], wd_ref[...], preferred_element_type=F32)

    wspec = pl.BlockSpec((F, D), lambda i: (0, 0), pipeline_mode=pl.Buffered(1))
    return _hosted(comm)(
        body, name="ffn_fwd",
        grid=(S // tm,),
        in_specs=[pl.BlockSpec((tm, D), lambda i: (i, 0)), pl.BlockSpec((1, D), lambda i: (0, 0)),
                  wspec, wspec, wspec],
        out_specs=[pl.BlockSpec((tm, D), lambda i: (i, 0)),
                   pl.BlockSpec((tm, F), lambda i: (i, 0)),
                   pl.BlockSpec((tm, F), lambda i: (i, 0))],
        out_shape=[jax.ShapeDtypeStruct((S, D), F32),
                   jax.ShapeDtypeStruct((S, F), CDT),
                   jax.ShapeDtypeStruct((S, F), CDT)],
        scratch_shapes=[pltpu.VMEM((tm, F), CDT)],
        compiler_params=_params(1),
    )(x, g, wgT, wuT, wd)


def ffn_bwd_dgrad(x, g, dy, gate, up, wgT, wuT, wd, *, tm=256, comm=None):
    S, D = x.shape
    F = wgT.shape[0]

    def body(x_ref, g_ref, dy_ref, gate_ref, up_ref, wg_ref, wu_ref, wd_ref,
             dx_ref, dgate_ref, dup_ref, h_ref, dg_ref, dwd_ref):
        @pl.when(pl.program_id(0) == 0)
        def _():
            dg_ref[...] = jnp.zeros_like(dg_ref)
            dwd_ref[...] = jnp.zeros_like(dwd_ref)

        dyf = dy_ref[...]
        dacc = (0.5 * dyf).astype(CDT)
        gg = g_ref[...]
        xhat, r = _rms_parts(x_ref[...])
        h_ref[...] = (xhat * gg).astype(CDT)
        for c0, cw_ in _chunks(F):
            sl = slice(c0, c0 + cw_)
            d_a = lax.dot_general(dacc, wd_ref[sl, :], NT, preferred_element_type=F32)
            gt = gate_ref[:, sl].astype(F32)
            ut = up_ref[:, sl].astype(F32)
            sg = _sigmoid(gt)
            silu = gt * sg
            dup_ref[:, sl] = (d_a * silu).astype(CDT)
            dgate_ref[:, sl] = (d_a * ut * (sg * (1.0 + gt * (1.0 - sg)))).astype(CDT)
            dwd_ref[sl, :] += lax.dot_general((silu * ut).astype(CDT), dacc, TN, preferred_element_type=F32)
        dh = (jnp.dot(dgate_ref[...], wg_ref[...], preferred_element_type=F32)
              + jnp.dot(dup_ref[...], wu_ref[...], preferred_element_type=F32))
        dx, dg = _rms_bwd(dh, xhat, r, gg)
        dx_ref[...] = dyf + dx
        dg_ref[...] += dg

    wspec = pl.BlockSpec((F, D), lambda i: (0, 0), pipeline_mode=pl.Buffered(1))
    row = pl.BlockSpec((tm, D), lambda i: (i, 0))
    act = pl.BlockSpec((tm, F), lambda i: (i, 0))
    vec = pl.BlockSpec((1, D), lambda i: (0, 0))
    return _hosted(comm)(
        body, name="ffn_bwd_dgrad",
        grid=(S // tm,),
        in_specs=[row, vec, row, act, act, wspec, wspec, wspec],
        out_specs=[row, act, act, row, vec, wspec],
        out_shape=[jax.ShapeDtypeStruct((S, D), F32),
                   jax.ShapeDtypeStruct((S, F), CDT),
                   jax.ShapeDtypeStruct((S, F), CDT),
                   jax.ShapeDtypeStruct((S, D), CDT),
                   jax.ShapeDtypeStruct((1, D), F32),
                   jax.ShapeDtypeStruct((F, D), F32)],
        compiler_params=_params(1),
    )(x, g, dy, gate, up, wgT, wuT, wd)


def ffn_wgrad_rows(h, acts, *, tk=1024, comm=None):
    S, D = h.shape
    F = acts[0].shape[1]
    n = len(acts)
    tk = min(tk, S)

    def body(h_ref, *refs):
        act_refs, dw_refs = refs[:n], refs[n:]

        @pl.when(pl.program_id(0) == 0)
        def _():
            for dw_ref in dw_refs:
                dw_ref[...] = jnp.zeros_like(dw_ref)

        hh = h_ref[...]
        for c0, cw_ in _chunks(F):
            sl = slice(c0, c0 + cw_)
            for act_ref, dw_ref in zip(act_refs, dw_refs):
                dw_ref[sl, :] += lax.dot_general(act_ref[:, sl], hh, TN, preferred_element_type=F32)

    act = pl.BlockSpec((tk, F), lambda k: (k, 0))
    out = pl.BlockSpec((F, D), lambda k: (0, 0), pipeline_mode=pl.Buffered(1))
    return _hosted(comm)(
        body, name="ffn_wgrad_rows",
        grid=(S // tk,),
        in_specs=[pl.BlockSpec((tk, D), lambda k: (k, 0))] + [act] * n,
        out_specs=[out] * n,
        out_shape=[jax.ShapeDtypeStruct((F, D), F32)] * n,
        compiler_params=_params(1),
    )(h, *acts)


def loss_head(x, g, target, *, tm=512):
    S, D = x.shape

    def body(x_ref, g_ref, t_ref, loss_ref, dx_ref, dg_ref):
        @pl.when(pl.program_id(0) == 0)
        def _():
            loss_ref[...] = jnp.zeros_like(loss_ref)
            dg_ref[...] = jnp.zeros_like(dg_ref)

        xhat, r = _rms_parts(x_ref[...])
        gg = g_ref[...]
        err = xhat * gg - t_ref[...]
        loss_ref[...] += 0.5 * jnp.sum(jnp.mean(err * err, axis=-1, keepdims=True), axis=0, keepdims=True)
        dx, dg = _rms_bwd(err * (1.0 / D), xhat, r, gg)
        dx_ref[...] = dx
        dg_ref[...] += dg

    row = pl.BlockSpec((tm, D), lambda i: (i, 0))
    vec = pl.BlockSpec((1, D), lambda i: (0, 0))
    return pl.pallas_call(
        body, name="loss_head",
        grid=(S // tm,),
        in_specs=[row, vec, row],
        out_specs=[pl.BlockSpec((1, 1), lambda i: (0, 0)), row, vec],
        out_shape=[jax.ShapeDtypeStruct((1, 1), F32), jax.ShapeDtypeStruct((S, D), F32),
                   jax.ShapeDtypeStruct((1, D), F32)],
        compiler_params=_params(1),
    )(x, g, target)


def _rope_apply(t, cs, sn):
    lane = lax.broadcasted_iota(jnp.int32, t.shape, 1)
    first = (lane % HEAD_DIM) < (HEAD_DIM // 2)
    rot = jnp.where(first, pltpu.roll(t, 128 - HEAD_DIM // 2, 1), pltpu.roll(t, HEAD_DIM // 2, 1))
    return t * cs + rot * sn


def _rope_transpose(d, cs, sn):
    lane = lax.broadcasted_iota(jnp.int32, d.shape, 1)
    first = (lane % HEAD_DIM) < (HEAD_DIM // 2)
    ds = d * sn
    rot = jnp.where(first, pltpu.roll(ds, 128 - HEAD_DIM // 2, 1), pltpu.roll(ds, HEAD_DIM // 2, 1))
    return d * cs + rot


def inproj_fwd(x, g, wextT, cs, sn, *, tm=512, comm=None):
    S, D = x.shape
    scale = HEAD_DIM ** -0.5

    def body(x_ref, g_ref, w_ref, cs_ref, sn_ref, q_ref, k_ref, v_ref, u_ref):
        xhat, _ = _rms_parts(x_ref[...])
        h = (xhat * g_ref[...]).astype(CDT)
        p = lax.dot_general(h, w_ref[...], NT, preferred_element_type=F32)
        c, s = cs_ref[...], sn_ref[...]
        for b in range(4):
            q_ref[:, 128 * b:128 * (b + 1)] = (_rope_apply(p[:, 128 * b:128 * (b + 1)], c, s) * scale).astype(CDT)
        for b in range(2):
            k_ref[:, 128 * b:128 * (b + 1)] = _rope_apply(p[:, 512 + 128 * b:512 + 128 * (b + 1)], c, s).astype(CDT)
        v_ref[...] = p[:, 768:1024].astype(CDT)
        u_ref[...] = p[:, 1024:2048]

    def row(w):
        return pl.BlockSpec((tm, w), lambda i: (i, 0))

    return _hosted(comm)(
        body, name="inproj_fwd",
        grid=(S // tm,),
        in_specs=[row(D), pl.BlockSpec((1, D), lambda i: (0, 0)),
                  pl.BlockSpec((D_EXT, D), lambda i: (0, 0)), row(128), row(128)],
        out_specs=[row(512), row(256), row(256), row(1024)],
        out_shape=[jax.ShapeDtypeStruct((S, 512), CDT), jax.ShapeDtypeStruct((S, 256), CDT),
                   jax.ShapeDtypeStruct((S, 256), CDT), jax.ShapeDtypeStruct((S, 1024), F32)],
        compiler_params=_params(1),
    )(x, g, wextT, cs, sn)


def _stack_heads(p0, p1):
    lane = lax.broadcasted_iota(jnp.int32, p0.shape, 1)
    lo = lane < HEAD_DIM
    z = jnp.zeros_like(p0)
    return jnp.concatenate([jnp.where(lo, p0, z), jnp.where(lo, z, p0),
                            jnp.where(lo, p1, z), jnp.where(lo, z, p1)], axis=0)


def _unstack_heads(o):
    lane = lax.broadcasted_iota(jnp.int32, (BLOCK, 128), 1)
    lo = lane < HEAD_DIM
    return (jnp.where(lo, o[0:128], o[128:256]), jnp.where(lo, o[256:384], o[384:512]))


def _band_mask_kq(n):
    c = lax.broadcasted_iota(jnp.int32, (2 * BLOCK, 4 * BLOCK), 0)
    i = lax.broadcasted_iota(jnp.int32, (2 * BLOCK, 4 * BLOCK), 1) % BLOCK
    return (c > i) & (c <= i + BLOCK) & ((n > 0) | (c >= BLOCK))


def attn_fwd(q, k, v, sink_row, *, nb=4, comm=None):
    S = q.shape[0]
    tq = nb * BLOCK

    def body(q_ref, k_ref, v_ref, sink_ref, o_ref):
        t = pl.program_id(0)
        for b in range(nb):
            n = t * nb + b
            prev = pl.multiple_of(jnp.maximum(n - 1, 0) * BLOCK, BLOCK)
            cur = pl.multiple_of(n * BLOCK, BLOCK)
            rows = slice(b * BLOCK, (b + 1) * BLOCK)
            mask = _band_mask_kq(n)
            for gidx in range(2):
                lanes = slice(128 * gidx, 128 * (gidx + 1))
                qs = _stack_heads(q_ref[rows, 256 * gidx:256 * gidx + 128],
                                  q_ref[rows, 256 * gidx + 128:256 * gidx + 256])
                kb = jnp.concatenate([k_ref[pl.ds(prev, BLOCK), lanes], k_ref[pl.ds(cur, BLOCK), lanes]], axis=0)
                vb = jnp.concatenate([v_ref[pl.ds(prev, BLOCK), lanes], v_ref[pl.ds(cur, BLOCK), lanes]], axis=0)
                st = lax.dot_general(kb, qs, NT, preferred_element_type=F32)
                st = jnp.where(mask, st, NEG)
                sink = sink_ref[gidx]
                m = jnp.maximum(jnp.max(st, axis=0, keepdims=True), sink)
                e = jnp.exp(st - m)
                inv = 1.0 / (jnp.sum(e, axis=0, keepdims=True) + jnp.exp(sink - m))
                o = lax.dot_general((e * inv).astype(CDT), vb, TN, preferred_element_type=F32)
                o0, o1 = _unstack_heads(o)
                o_ref[rows, 256 * gidx:256 * gidx + 128] = o0.astype(CDT)
                o_ref[rows, 256 * gidx + 128:256 * gidx + 256] = o1.astype(CDT)

    return _hosted(comm)(
        body, name="attn_fwd",
        grid=(S // tq,),
        in_specs=[pl.BlockSpec((tq, 512), lambda t: (t, 0)),
                  pl.BlockSpec((S, 256), lambda t: (0, 0)),
                  pl.BlockSpec((S, 256), lambda t: (0, 0)),
                  pl.BlockSpec((2, 1, 4 * BLOCK), lambda t: (0, 0, 0))],
        out_specs=pl.BlockSpec((tq, 512), lambda t: (t, 0)),
        out_shape=jax.ShapeDtypeStruct((S, 512), CDT),
        compiler_params=_params(1),
    )(q, k, v, sink_row)


def attn_bwd(q, k, v, do, sink_row, *, nb=4, comm=None):
    S = q.shape[0]
    tq = nb * BLOCK
    scale = HEAD_DIM ** -0.5

    def body(q_ref, k_ref, v_ref, do_ref, sink_ref, dq_ref, dk_ref, dv_ref, dsink_ref):
        t = pl.program_id(0)

        @pl.when(t == 0)
        def _():
            dk_ref[...] = jnp.zeros_like(dk_ref)
            dv_ref[...] = jnp.zeros_like(dv_ref)
            dsink_ref[...] = jnp.zeros_like(dsink_ref)

        for b in range(nb):
            n = t * nb + b
            prev = pl.multiple_of(jnp.maximum(n - 1, 0) * BLOCK, BLOCK)
            cur = pl.multiple_of(n * BLOCK, BLOCK)
            rows = slice(b * BLOCK, (b + 1) * BLOCK)
            mask = _band_mask_kq(n)
            for gidx in range(2):
                lanes = slice(128 * gidx, 128 * (gidx + 1))
                qs = _stack_heads(q_ref[rows, 256 * gidx:256 * gidx + 128],
                                  q_ref[rows, 256 * gidx + 128:256 * gidx + 256])
                dos = _stack_heads(do_ref[rows, 256 * gidx:256 * gidx + 128],
                                   do_ref[rows, 256 * gidx + 128:256 * gidx + 256])
                kb = jnp.concatenate([k_ref[pl.ds(prev, BLOCK), lanes], k_ref[pl.ds(cur, BLOCK), lanes]], axis=0)
                vb = jnp.concatenate([v_ref[pl.ds(prev, BLOCK), lanes], v_ref[pl.ds(cur, BLOCK), lanes]], axis=0)
                st = lax.dot_general(kb, qs, NT, preferred_element_type=F32)
                st = jnp.where(mask, st, NEG)
                sink = sink_ref[gidx]
                m = jnp.maximum(jnp.max(st, axis=0, keepdims=True), sink)
                e = jnp.exp(st - m)
                es = jnp.exp(sink - m)
                inv = 1.0 / (jnp.sum(e, axis=0, keepdims=True) + es)
                pt = e * inv
                dpt = lax.dot_general(vb, dos, NT, preferred_element_type=F32)
                delta = jnp.sum(pt * dpt, axis=0, keepdims=True)
                dst = (pt * (dpt - delta)).astype(CDT)
                dsink_ref[gidx] += -(es * inv) * delta
                dvb = jnp.dot(pt.astype(CDT), dos, preferred_element_type=F32)
                dkb = jnp.dot(dst, qs, preferred_element_type=F32)
                dqs = lax.dot_general(dst, kb, TN, preferred_element_type=F32) * scale
                dq0, dq1 = _unstack_heads(dqs)
                dq_ref[rows, 256 * gidx:256 * gidx + 128] = dq0
                dq_ref[rows, 256 * gidx + 128:256 * gidx + 256] = dq1
                dk_ref[pl.ds(prev, BLOCK), lanes] += dkb[0:BLOCK]
                dk_ref[pl.ds(cur, BLOCK), lanes] += dkb[BLOCK:2 * BLOCK]
                dv_ref[pl.ds(prev, BLOCK), lanes] += dvb[0:BLOCK]
                dv_ref[pl.ds(cur, BLOCK), lanes] += dvb[BLOCK:2 * BLOCK]

    full = pl.BlockSpec((S, 256), lambda t: (0, 0))
    tile = pl.BlockSpec((tq, 512), lambda t: (t, 0))
    srow = pl.BlockSpec((2, 1, 4 * BLOCK), lambda t: (0, 0, 0))
    return _hosted(comm)(
        body, name="attn_bwd",
        grid=(S // tq,),
        in_specs=[tile, full, full, tile, srow],
        out_specs=[tile, full, full, srow],
        out_shape=[jax.ShapeDtypeStruct((S, 512), F32), jax.ShapeDtypeStruct((S, 256), F32),
                   jax.ShapeDtypeStruct((S, 256), F32), jax.ShapeDtypeStruct((2, 1, 4 * BLOCK), F32)],
        compiler_params=_params(1),
    )(q, k, v, do, sink_row)


def _glu(u):
    a = u[:, 0:CONV_C]
    gt = u[:, CONV_C:2 * CONV_C]
    sg = _sigmoid(gt)
    return a, sg, a * sg


CONV_CHUNK = 32


def _shifted_copies(buf, shifted, n):
    for r in range(1, 8):
        shifted[r - 1, 0:n, :] = buf[r:r + n, :]


def _shifted_rows(buf, shifted, start, rows):
    r = start % 8
    if r == 0:
        return buf[start:start + rows, :]
    return shifted[r - 1, start - r:start - r + rows, :]


def conv_fwd(u, cw, cb, lg, lb, *, tm=512, comm=None):
    S = u.shape[0]
    nh = tm // HALO

    def body(u_ref, uh_ref, cw_ref, cb_ref, lg_ref, lb_ref, o_ref, y_ref, hbuf, hsh):
        t = pl.program_id(0)
        _, _, hg = _glu(u_ref[...])
        _, _, hh = _glu(uh_ref[...])
        hbuf[0:HALO, :] = jnp.where(t > 0, hh, jnp.zeros_like(hh))
        hbuf[HALO:HALO + tm, :] = hg
        hbuf[HALO + tm:HALO + tm + 8, :] = jnp.zeros((8, CONV_C), F32)
        _shifted_copies(hbuf, hsh, HALO + tm)
        off = HALO - (CONV_K - 1)
        for c0 in range(0, tm, CONV_CHUNK):
            acc = jnp.zeros((CONV_CHUNK, CONV_C), F32) + cb_ref[...]
            for j in range(CONV_K):
                acc = acc + cw_ref[j:j + 1, :] * _shifted_rows(hbuf, hsh, c0 + off + j, CONV_CHUNK)
            y_ref[c0:c0 + CONV_CHUNK, :] = acc
        y = y_ref[...]
        yc = y - jnp.mean(y, axis=-1, keepdims=True)
        r = lax.rsqrt(jnp.mean(yc * yc, axis=-1, keepdims=True) + EPS)
        z = yc * r * lg_ref[...] + lb_ref[...]
        o_ref[...] = (z * _sigmoid(z)).astype(CDT)

    vec = pl.BlockSpec((1, CONV_C), lambda t: (0, 0))
    return _hosted(comm)(
        body, name="conv_fwd",
        grid=(S // tm,),
        in_specs=[pl.BlockSpec((tm, 2 * CONV_C), lambda t: (t, 0)),
                  pl.BlockSpec((HALO, 2 * CONV_C), lambda t: (jnp.maximum(t * nh - 1, 0), 0)),
                  pl.BlockSpec((CONV_K, CONV_C), lambda t: (0, 0)), vec, vec, vec],
        out_specs=[pl.BlockSpec((tm, CONV_C), lambda t: (t, 0)), pl.BlockSpec((tm, CONV_C), lambda t: (t, 0))],
        out_shape=[jax.ShapeDtypeStruct((S, CONV_C), CDT), jax.ShapeDtypeStruct((S, CONV_C), F32)],
        scratch_shapes=[pltpu.VMEM((HALO + tm + 8, CONV_C), F32), pltpu.VMEM((7, HALO + tm, CONV_C), F32)],
        compiler_params=_params(1),
    )(u, u, cw, cb, lg, lb)


def conv_bwd(dc, u, y, cw, lg, lb, *, tm=512, comm=None):
    S = u.shape[0]
    nh = tm // HALO
    nt = S // tm

    def ln_bwd(dcv, yv, lgv, lbv):
        yc = yv - jnp.mean(yv, axis=-1, keepdims=True)
        r = lax.rsqrt(jnp.mean(yc * yc, axis=-1, keepdims=True) + EPS)
        yhat = yc * r
        z = yhat * lgv + lbv
        sg = _sigmoid(z)
        dz = dcv * (sg * (1.0 + z * (1.0 - sg)))
        dyhat = dz * lgv
        dy = r * (dyhat - jnp.mean(dyhat, axis=-1, keepdims=True)
                  - yhat * jnp.mean(dyhat * yhat, axis=-1, keepdims=True))
        return dy, dz, yhat

    def body(dc_ref, dcn_ref, u_ref, uh_ref, y_ref, yn_ref, cw_ref, lg_ref, lb_ref,
             du_ref, dcw_ref, dcb_ref, dlg_ref, dlb_ref, hbuf, dybuf, dhg_sc, dw_sc, hsh, dysh):
        t = pl.program_id(0)

        @pl.when(t == 0)
        def _():
            dw_sc[...] = jnp.zeros_like(dw_sc)
            dcb_ref[...] = jnp.zeros_like(dcb_ref)
            dlg_ref[...] = jnp.zeros_like(dlg_ref)
            dlb_ref[...] = jnp.zeros_like(dlb_ref)

        lgv, lbv = lg_ref[...], lb_ref[...]
        dy, dz, yhat = ln_bwd(dc_ref[...].astype(F32), y_ref[...], lgv, lbv)
        dyn, _, _ = ln_bwd(dcn_ref[...].astype(F32), yn_ref[...], lgv, lbv)
        dlb_ref[...] += jnp.sum(dz, axis=0, keepdims=True)
        dlg_ref[...] += jnp.sum(dz * yhat, axis=0, keepdims=True)
        dcb_ref[...] += jnp.sum(dy, axis=0, keepdims=True)
        dybuf[0:tm, :] = dy
        dybuf[tm:tm + HALO, :] = jnp.where(t < nt - 1, dyn, jnp.zeros_like(dyn))
        dybuf[tm + HALO:tm + HALO + 8, :] = jnp.zeros((8, CONV_C), F32)
        _shifted_copies(dybuf, dysh, tm + HALO)

        a, sg, hg = _glu(u_ref[...])
        _, _, hh = _glu(uh_ref[...])
        hbuf[0:HALO, :] = jnp.where(t > 0, hh, jnp.zeros_like(hh))
        hbuf[HALO:HALO + tm, :] = hg
        hbuf[HALO + tm:HALO + tm + 8, :] = jnp.zeros((8, CONV_C), F32)
        _shifted_copies(hbuf, hsh, HALO + tm)

        off = HALO - (CONV_K - 1)
        for c0 in range(0, tm, CONV_CHUNK):
            acc = jnp.zeros((CONV_CHUNK, CONV_C), F32)
            dyc = dybuf[c0:c0 + CONV_CHUNK, :]
            for j in range(CONV_K):
                acc = acc + cw_ref[j:j + 1, :] * _shifted_rows(dybuf, dysh, c0 + (CONV_K - 1) - j, CONV_CHUNK)
                prod = dyc * _shifted_rows(hbuf, hsh, c0 + off + j, CONV_CHUNK)
                dw_sc[j] += jnp.sum(prod.reshape(CONV_CHUNK // 8, 8, CONV_C), axis=0)
            dhg_sc[c0:c0 + CONV_CHUNK, :] = acc

        dhg = dhg_sc[...]
        du_ref[:, 0:CONV_C] = dhg * sg
        du_ref[:, CONV_C:2 * CONV_C] = dhg * a * sg * (1.0 - sg)

        @pl.when(t == nt - 1)
        def _():
            dcw_ref[...] = jnp.sum(dw_sc[...], axis=1)

    vec = pl.BlockSpec((1, CONV_C), lambda t: (0, 0))
    tile = pl.BlockSpec((tm, CONV_C), lambda t: (t, 0))
    nxt = pl.BlockSpec((HALO, CONV_C), lambda t: (jnp.minimum((t + 1) * nh, S // HALO - 1), 0))
    return _hosted(comm)(
        body, name="conv_bwd",
        grid=(nt,),
        in_specs=[tile, nxt,
                  pl.BlockSpec((tm, 2 * CONV_C), lambda t: (t, 0)),
                  pl.BlockSpec((HALO, 2 * CONV_C), lambda t: (jnp.maximum(t * nh - 1, 0), 0)),
                  tile, nxt,
                  pl.BlockSpec((CONV_K, CONV_C), lambda t: (0, 0)), vec, vec],
        out_specs=[pl.BlockSpec((tm, 2 * CONV_C), lambda t: (t, 0)),
                   pl.BlockSpec((CONV_K, CONV_C), lambda t: (0, 0)), vec, vec, vec],
        out_shape=[jax.ShapeDtypeStruct((S, 2 * CONV_C), F32), jax.ShapeDtypeStruct((CONV_K, CONV_C), F32),
                   jax.ShapeDtypeStruct((1, CONV_C), F32), jax.ShapeDtypeStruct((1, CONV_C), F32),
                   jax.ShapeDtypeStruct((1, CONV_C), F32)],
        scratch_shapes=[pltpu.VMEM((HALO + tm + 8, CONV_C), F32), pltpu.VMEM((tm + HALO + 8, CONV_C), F32),
                        pltpu.VMEM((tm, CONV_C), F32), pltpu.VMEM((CONV_K, 8, CONV_C), F32),
                        pltpu.VMEM((7, HALO + tm, CONV_C), F32), pltpu.VMEM((7, tm + HALO, CONV_C), F32)],
        compiler_params=_params(1),
    )(dc, dc, u, u, y, y, cw, lg, lb)


def outproj_fwd(x, ao, co, wout, *, tm=512):
    S, D = x.shape

    def body(x_ref, a_ref, c_ref, w_ref, o_ref):
        o_ref[...] = (x_ref[...]
                      + jnp.dot(a_ref[...], w_ref[0:ATT_W, :], preferred_element_type=F32)
                      + jnp.dot(c_ref[...], w_ref[ATT_W:ATT_W + CONV_C, :], preferred_element_type=F32))

    return pl.pallas_call(
        body, name="outproj_fwd",
        grid=(S // tm,),
        in_specs=[pl.BlockSpec((tm, D), lambda i: (i, 0)), pl.BlockSpec((tm, ATT_W), lambda i: (i, 0)),
                  pl.BlockSpec((tm, CONV_C), lambda i: (i, 0)), pl.BlockSpec((D, D), lambda i: (0, 0))],
        out_specs=pl.BlockSpec((tm, D), lambda i: (i, 0)),
        out_shape=jax.ShapeDtypeStruct((S, D), F32),
        compiler_params=_params(1),
    )(x, ao, co, wout)


def outproj_bwd(dx, ao, co, wout, *, tm=512, comm=None):
    S, D = dx.shape

    def body(dx_ref, a_ref, c_ref, w_ref, da_ref, dc_ref, dw_ref):
        @pl.when(pl.program_id(0) == 0)
        def _():
            dw_ref[...] = jnp.zeros_like(dw_ref)

        dxb = dx_ref[...].astype(CDT)
        da_ref[...] = lax.dot_general(dxb, w_ref[0:ATT_W, :], NT, preferred_element_type=F32).astype(CDT)
        dc_ref[...] = lax.dot_general(dxb, w_ref[ATT_W:ATT_W + CONV_C, :], NT, preferred_element_type=F32)
        dw_ref[0:ATT_W, :] += lax.dot_general(a_ref[...], dxb, TN, preferred_element_type=F32)
        dw_ref[ATT_W:ATT_W + CONV_C, :] += lax.dot_general(c_ref[...], dxb, TN, preferred_element_type=F32)

    return _hosted(comm)(
        body, name="outproj_bwd",
        grid=(S // tm,),
        in_specs=[pl.BlockSpec((tm, D), lambda i: (i, 0)), pl.BlockSpec((tm, ATT_W), lambda i: (i, 0)),
                  pl.BlockSpec((tm, CONV_C), lambda i: (i, 0)), pl.BlockSpec((D, D), lambda i: (0, 0))],
        out_specs=[pl.BlockSpec((tm, ATT_W), lambda i: (i, 0)), pl.BlockSpec((tm, CONV_C), lambda i: (i, 0)),
                   pl.BlockSpec((D, D), lambda i: (0, 0))],
        out_shape=[jax.ShapeDtypeStruct((S, ATT_W), CDT), jax.ShapeDtypeStruct((S, CONV_C), F32),
                   jax.ShapeDtypeStruct((D, D), F32)],
        compiler_params=_params(1),
    )(dx, ao, co, wout)


def inproj_bwd(x, g, dres, dq, dk, dv, du, wextT, cs, sn, *, tm=512):
    S, D = x.shape

    def body(x_ref, g_ref, dres_ref, dq_ref, dk_ref, dv_ref, du_ref, w_ref, cs_ref, sn_ref,
             dx_ref, dw_ref, dg_ref, dp_sc):
        @pl.when(pl.program_id(0) == 0)
        def _():
            dw_ref[...] = jnp.zeros_like(dw_ref)
            dg_ref[...] = jnp.zeros_like(dg_ref)

        c, s = cs_ref[...], sn_ref[...]
        for b in range(4):
            dp_sc[:, 128 * b:128 * (b + 1)] = _rope_transpose(dq_ref[:, 128 * b:128 * (b + 1)], c, s).astype(CDT)
        for b in range(2):
            dp_sc[:, 512 + 128 * b:512 + 128 * (b + 1)] = _rope_transpose(
                dk_ref[:, 128 * b:128 * (b + 1)], c, s).astype(CDT)
        dp_sc[:, 768:1024] = dv_ref[...].astype(CDT)
        dp_sc[:, 1024:2048] = du_ref[...].astype(CDT)
        dp = dp_sc[...]
        xhat, r = _rms_parts(x_ref[...])
        gg = g_ref[...]
        h = (xhat * gg).astype(CDT)
        dh = jnp.dot(dp, w_ref[...], preferred_element_type=F32)
        dw_ref[...] += lax.dot_general(dp, h, TN, preferred_element_type=F32)
        dx, dg = _rms_bwd(dh, xhat, r, gg)
        dx_ref[...] = dres_ref[...] + dx
        dg_ref[...] += dg

    def row(w):
        return pl.BlockSpec((tm, w), lambda i: (i, 0))

    return pl.pallas_call(
        body, name="inproj_bwd",
        grid=(S // tm,),
        in_specs=[row(D), pl.BlockSpec((1, D), lambda i: (0, 0)), row(D), row(512), row(256), row(256),
                  row(1024), pl.BlockSpec((D_EXT, D), lambda i: (0, 0), pipeline_mode=pl.Buffered(1)),
                  row(128), row(128)],
        out_specs=[row(D), pl.BlockSpec((D_EXT, D), lambda i: (0, 0), pipeline_mode=pl.Buffered(1)),
                   pl.BlockSpec((1, D), lambda i: (0, 0))],
        out_shape=[jax.ShapeDtypeStruct((S, D), F32), jax.ShapeDtypeStruct((D_EXT, D), F32),
                   jax.ShapeDtypeStruct((1, D), F32)],
        scratch_shapes=[pltpu.VMEM((tm, D_EXT), CDT)],
        compiler_params=_params(1),
    )(x, g, dres, dq, dk, dv, du, wextT, cs, sn)


def _rope_tables(positions):
    inv_freq = 1.0 / (10000.0 ** (jnp.arange(0, HEAD_DIM, 2, dtype=F32) / HEAD_DIM))
    ang = positions.astype(F32).reshape(-1, 1) * inv_freq
    cos, sin = jnp.cos(ang), jnp.sin(ang)
    cs = jnp.tile(jnp.concatenate([cos, cos], axis=-1), (1, 2))
    sn = jnp.tile(jnp.concatenate([-sin, sin], axis=-1), (1, 2))
    return cs, sn


def _widen_w_in(w):
    q, u = w[0:512], w[768:1792]
    parts = [q]
    for base in (512, 576, 640, 704):
        parts += [w[base:base + 64], w[base:base + 64]]
    return jnp.concatenate(parts + [u], axis=0)


def _fold_w_in(d):
    parts = [d[0:512]]
    for base in (512, 640, 768, 896):
        parts.append(d[base:base + 64] + d[base + 64:base + 128])
    return jnp.concatenate(parts + [d[1024:2048]], axis=0)


def add_halves(g5s, r1s, c_idx):
    n = len(g5s)

    def body(c_ref, *refs):
        for a in range(n):
            refs[2 * n + a][...] = (refs[a][...] + refs[n + a][...]).astype(CDT)

    def g_spec(g):
        return pl.BlockSpec((2, None) + g.shape[2:], lambda s, cr: (s, cr[0], 0, 0))

    def r_spec(g):
        return pl.BlockSpec((2,) + g.shape[2:], lambda s, cr: (s, 0, 0))

    return list(pl.pallas_call(
        body, name="add_halves",
        grid_spec=pltpu.PrefetchScalarGridSpec(
            num_scalar_prefetch=1, grid=(N_CHIPS // 2,),
            in_specs=[g_spec(g) for g in g5s] + [r_spec(g) for g in g5s],
            out_specs=[r_spec(g) for g in g5s]),
        out_shape=[jax.ShapeDtypeStruct((N_CHIPS,) + g.shape[2:], CDT) for g in g5s],
        compiler_params=_params(1),
    )(c_idx, *g5s, *r1s))


def sum_partials(parts, recv3s, j_idx):
    n = len(parts)

    def body(j_ref, *refs):
        for a in range(n):
            p_ref, r_ref = refs[a], refs[n + a]
            refs[2 * n + a][...] = ((p_ref[...].astype(F32) + r_ref[0].astype(F32))
                                    + r_ref[1].astype(F32)) + r_ref[2].astype(F32)

    def half(p):
        return p.shape[1] // 2

    return list(pl.pallas_call(
        body, name="sum_partials",
        grid_spec=pltpu.PrefetchScalarGridSpec(
            num_scalar_prefetch=1, grid=(2,),
            in_specs=[pl.BlockSpec((None, half(p), p.shape[2]), lambda i, jr: (jr[0], i, 0)) for p in parts]
            + [pl.BlockSpec((3, half(p), p.shape[2]), lambda i, jr: (0, i, 0)) for p in parts],
            out_specs=[pl.BlockSpec((half(p), p.shape[2]), lambda i, jr: (i, 0)) for p in parts]),
        out_shape=[jax.ShapeDtypeStruct(p.shape[1:], F32) for p in parts],
        compiler_params=_params(1),
    )(j_idx, *parts, *recv3s))


class _Chain:
    STAGES = ("swap", "xchg", "share")

    def __init__(self, grads, c_idx, j_idx):
        self.c_idx, self.j_idx = c_idx, j_idx
        self.g5 = [g.reshape(N_CHIPS, 2, g.shape[0] // (2 * N_CHIPS), g.shape[1]) for g in grads]
        self.stage_no = 0

    @property
    def done(self):
        return self.stage_no == len(self.STAGES)

    def next_stage(self):
        name = self.STAGES[self.stage_no]

        def callback(res):
            getattr(self, "after_" + name)(res)
            self.stage_no += 1

        return getattr(self, name)(), callback

    def swap(self):
        return swap_op(self.g5)

    def after_swap(self, recv):
        self.parts = add_halves(self.g5, recv, self.c_idx)

    def xchg(self):
        return exchange_op(self.parts)

    def after_xchg(self, recv):
        self.totals = sum_partials(self.parts, recv, self.j_idx)

    def share(self):
        return share_op(self.totals)

    def after_share(self, recv):
        both = [_own_slab(h, t, self.c_idx[0]) for h, t in zip(recv, self.totals)]
        self.final = [h.reshape(2 * h.shape[1], h.shape[2]) for h in both]


def all_reduce_small(vec):
    R = vec.shape[0]

    def body(v_ref, o_ref, buf, send, recv):
        x, y, c = _coords()
        me = 4 * x + 2 * y + c
        buf[me] = v_ref[...]
        cps = []
        for m in range(1, N_DEV):
            dx, dy, dc = (m >> 2) & 1, (m >> 1) & 1, m & 1
            cp = pltpu.make_async_remote_copy(v_ref, buf.at[me], send.at[m - 1], recv.at[m - 1],
                                              device_id=((x + dx) % 2, (y + dy) % 2, (c + dc) % 2),
                                              device_id_type=MESH)
            cp.start()
            cps.append(cp)
        for cp in cps:
            cp.wait()
        acc = buf[0]
        for d in range(1, N_DEV):
            acc = acc + buf[d]
        o_ref[...] = acc

    return pl.pallas_call(
        body, name="all_reduce_small",
        in_specs=[pl.BlockSpec(memory_space=pltpu.VMEM)], out_specs=pl.BlockSpec(memory_space=pltpu.VMEM),
        out_shape=jax.ShapeDtypeStruct(vec.shape, F32),
        scratch_shapes=[pltpu.VMEM((N_DEV, R, 128), F32), pltpu.SemaphoreType.DMA((N_DEV - 1,)),
                        pltpu.SemaphoreType.DMA((N_DEV - 1,))],
    )(vec)


def adamw(w, g, m, v, *, tm=512):
    R, C = w.shape
    tm = max(t for t in range(8, min(tm, R) + 1, 8) if R % t == 0)
    c1 =1.0 - ADAM_B1 ** ADAM_STEP
    c2 = 1.0 - ADAM_B2 ** ADAM_STEP

    def body(w_ref, g_ref, m_ref, v_ref, d_ref, nm_ref, nv_ref):
        gg = g_ref[...]
        nm = ADAM_B1 * m_ref[...] + (1.0 - ADAM_B1) * gg
        nv = ADAM_B2 * v_ref[...] + (1.0 - ADAM_B2) * (gg * gg)
        nm_ref[...] = nm
        nv_ref[...] = nv
        d_ref[...] = -ADAM_LR * ((nm / c1) / (jnp.sqrt(nv / c2) + ADAM_EPS) + ADAM_WD * w_ref[...])

    blk = pl.BlockSpec((tm, C), lambda i: (i, 0))
    return pl.pallas_call(
        body, name="adamw",
        grid=(pl.cdiv(R, tm),),
        in_specs=[blk] * 4, out_specs=[blk] * 3,
        out_shape=[jax.ShapeDtypeStruct((R, C), F32)] * 3,
        compiler_params=_params(1),
    )(w, g, m, v)


def adamw_layers(w, m, v, g_layers, *, tm=352):
    L, R, C = w.shape
    tm = max(t for t in range(8, min(tm, R) + 1, 8) if R % t == 0)
    c1 = 1.0 - ADAM_B1 ** ADAM_STEP
    c2 = 1.0 - ADAM_B2 ** ADAM_STEP

    def body(w_ref, m_ref, v_ref, *rest):
        g_refs, (go_ref, d_ref, nm_ref, nv_ref) = rest[:L], rest[L:]
        layer = pl.program_id(0)
        gg = g_refs[0][...]
        for l in range(1, L):
            gg = jnp.where(layer == l, g_refs[l][...], gg)
        nm = ADAM_B1 * m_ref[...] + (1.0 - ADAM_B1) * gg
        nv = ADAM_B2 * v_ref[...] + (1.0 - ADAM_B2) * (gg * gg)
        go_ref[...] = gg
        nm_ref[...] = nm
        nv_ref[...] = nv
        d_ref[...] = -ADAM_LR * ((nm / c1) / (jnp.sqrt(nv / c2) + ADAM_EPS) + ADAM_WD * w_ref[...])

    blk = pl.BlockSpec((None, tm, C), lambda l, i: (l, i, 0))
    gblk = pl.BlockSpec((tm, C), lambda l, i: (i, 0))
    return pl.pallas_call(
        body, name="adamw_layers",
        grid=(L, R // tm),
        in_specs=[blk] * 3 + [gblk] * L, out_specs=[blk] * 4,
        out_shape=[jax.ShapeDtypeStruct((L, R, C), F32)] * 4,
        compiler_params=_params(2),
    )(w, m, v, *g_layers)


_SMALL = (("n1", (2, D_MODEL)), ("nm", (2, D_MODEL)), ("n2", (2, D_MODEL)), ("nf", (D_MODEL,)),
          ("cb", (2, CONV_C)), ("lg", (2, CONV_C)), ("lb", (2, CONV_C)), ("sinks", (2, N_HEADS)),
          ("cw", (2, CONV_K, CONV_C)))


def _pack(parts, rows):
    flat = jnp.concatenate([p.reshape(-1).astype(F32) for p in parts])
    return jnp.pad(flat, (0, rows * 128 - flat.shape[0])).reshape(rows, 128)


def _unpack(block, shapes):
    flat = block.reshape(-1)
    out, o = [], 0
    for shp in shapes:
        n = 1
        for s in shp:
            n *= s
        out.append(flat[o:o + n].reshape(shp))
        o += n
    return out


def kernel(x, positions, ffn1_norm, ffn1_w_gate, ffn1_w_up, ffn1_w_down, mix_norm, w_in, conv_w, conv_b, conv_ln_g, conv_ln_b, attn_sinks, w_out, ffn2_norm, ffn2_w_gate, ffn2_w_up, ffn2_w_down, final_norm, loss_target, m_ffn1_norm, m_ffn1_w_gate, m_ffn1_w_up, m_ffn1_w_down, m_mix_norm, m_w_in, m_conv_w, m_conv_b, m_conv_ln_g, m_conv_ln_b, m_attn_sinks, m_w_out, m_ffn2_norm, m_ffn2_w_gate, m_ffn2_w_up, m_ffn2_w_down, m_final_norm, v_ffn1_norm, v_ffn1_w_gate, v_ffn1_w_up, v_ffn1_w_down, v_mix_norm, v_w_in, v_conv_w, v_conv_b, v_conv_ln_g, v_conv_ln_b, v_attn_sinks, v_w_out, v_ffn2_norm, v_ffn2_w_gate, v_ffn2_w_up, v_ffn2_w_down, v_final_norm):
    cx, cy, cc = _coords()
    chip = 2 * cx + cy
    c_idx = jnp.reshape(cc, (1,)).astype(jnp.int32)
    j_idx = jnp.reshape(chip, (1,)).astype(jnp.int32)
    L = ffn1_norm.shape[0]
    tr = lambda a: jnp.swapaxes(a, 1, 2)

    sh = dict(f1g=tr(ffn1_w_gate), f1u=tr(ffn1_w_up), f1d=ffn1_w_down, f2g=tr(ffn2_w_gate),
              f2u=tr(ffn2_w_up), f2d=ffn2_w_down, win=tr(w_in), wout=w_out)
    sh = {k: [v[l].astype(CDT) for l in range(L)] for k, v in sh.items()}
    W = {}

    def gather_op(keys):
        if keys == ["cw"]:
            return ag_op([conv_w])
        return ag2_op([sh[k[0]][k[1]] for k in keys])

    def take(keys, res):
        for k, a in zip(keys, res):
            if k == "cw":
                W[k] = _own_slab(a, conv_w, chip)
            else:
                mine = sh[k[0]][k[1]]
                W[k] = _own_slab(a, mine.reshape(a.shape[1:]), chip).reshape(N_CHIPS * mine.shape[0], mine.shape[1])

    def with_ag(fn, keys, *args):
        if not keys:
            return fn(*args)
        main, res = fn(*args, comm=gather_op(keys))
        take(keys, res)
        return main

    ag_hosts = {("ffn1", 0): [("win", 0), ("f2g", 0), ("f1d", 1)],
                ("inproj", 0): ["cw"], ("attn", 0): [("wout", 0), ("f2u", 0)], ("conv", 0): [("f2d", 0)],
                ("ffn2", 0): [("f1g", 1), ("f1u", 1)],
                ("ffn1", 1): [("win", 1), ("f2g", 1), ("f2u", 1)],
                ("inproj", 1): [("wout", 1)], ("attn", 1): [("f2d", 1)]}
    first = [("f1g", 0), ("f1u", 0), ("f1d", 0)]
    take(first, first_gather([sh[k_][l] for k_, l in first]))

    cs, sn = _rope_tables(positions)
    saved = []
    h = x[0]
    for l in range(L):
        sink = attn_sinks[l].reshape(2, 4)
        sink_row = jnp.repeat(sink, BLOCK, axis=1).reshape(2, 1, 4 * BLOCK)
        x0 = h
        x1, g1, u1 = with_ag(ffn_fwd, ag_hosts.get(("ffn1", l)), x0, ffn1_norm[l][None],
                             W[("f1g", l)], W[("f1u", l)], W[("f1d", l)])
        wext = _widen_w_in(W[("win", l)])
        q, k, v, u = with_ag(inproj_fwd, ag_hosts.get(("inproj", l)), x1, mix_norm[l][None], wext, cs, sn)
        ao = with_ag(attn_fwd, ag_hosts.get(("attn", l)), q, k, v, sink_row)
        cwl = jnp.transpose(W["cw"][:, l], (1, 0, 2)).reshape(CONV_K, CONV_C)
        co, yc = with_ag(conv_fwd, ag_hosts.get(("conv", l)), u, cwl, conv_b[l][None], conv_ln_g[l][None],
                         conv_ln_b[l][None])
        x2 = outproj_fwd(x1, ao, co, W[("wout", l)])
        x3, g2, u2 = with_ag(ffn_fwd, ag_hosts.get(("ffn2", l)), x2, ffn2_norm[l][None],
                             W[("f2g", l)], W[("f2u", l)], W[("f2d", l)])
        saved.append((x0, x1, x2, g1, u1, g2, u2, q, k, v, u, ao, co, yc, sink, wext, cwl))
        h = x3

    loss, dx, dnf = loss_head(h, final_norm[None], loss_target[0])

    active = []

    def advance(run):
        stages = [ch.next_stage() for ch in active]
        ops = [op for op, _ in stages]
        main, res = run(_merge(*ops) if ops else None)
        for (_, cb), r in zip(stages, _split(res, *ops)):
            cb(r)
        active[:] = [ch for ch in active if not ch.done]
        return main

    def hosted(fn, *args):
        def run(comm):
            if comm is None:
                return fn(*args), []
            return fn(*args, comm=comm)
        return advance(run)

    def chain(key, names_, grads):
        chains[key] = _Chain(grads, c_idx, j_idx)
        active.append(chains[key])
        for i_, nme_ in enumerate(names_):
            where[(nme_, key[1])] = (key, i_)

    small = {k_: [None] * L for k_ in ("n1", "nm", "n2", "cw", "cb", "lg", "lb", "sinks")}
    chains, where = {}, {}
    for l in reversed(range(L)):
        x0, x1, x2, g1, u1, g2, u2, q, k, v, u, ao, co, yc, sink, wext, cwl = saved[l]
        sink_row = jnp.repeat(sink, BLOCK, axis=1).reshape(2, 1, 4 * BLOCK)
        dx2, dgt, dup, hh, small["n2"][l], gwd = hosted(
            ffn_bwd_dgrad, x2, ffn2_norm[l][None], dx, g2, u2, W[("f2g", l)], W[("f2u", l)], W[("f2d", l)])
        chain(("f2d", l), ["ffn2_w_down"], [gwd])
        chain(("f2gu", l), ["ffn2_w_gate", "ffn2_w_up"], hosted(ffn_wgrad_rows, hh, [dgt, dup]))
        da, dc, gwout = hosted(outproj_bwd, dx2, ao, co, W[("wout", l)])
        du, small["cw"][l], small["cb"][l], small["lg"][l], small["lb"][l] = hosted(
            conv_bwd, dc, u, yc, cwl, conv_ln_g[l][None], conv_ln_b[l][None])
        dq, dk, dv, dsink = hosted(attn_bwd, q, k, v, da, sink_row)
        small["sinks"][l] = jnp.sum(dsink.reshape(2, 4, BLOCK), axis=-1).reshape(N_HEADS)
        dx1, gwext, small["nm"][l] = inproj_bwd(x1, mix_norm[l][None], dx2, dq, dk, dv, du, wext, cs, sn)
        chain(("mx", l), ["w_out", "w_in"], [gwout, _fold_w_in(gwext)])
        dx, dgt, dup, hh, small["n1"][l], gwd = (hosted if l > 0 else lambda fn, *a: fn(*a))(
            ffn_bwd_dgrad, x0, ffn1_norm[l][None], dx1, g1, u1, W[("f1g", l)], W[("f1u", l)], W[("f1d", l)])
        chain(("f1d", l), ["ffn1_w_down"], [gwd])
        if l > 0:
            chain(("f1gu", l), ["ffn1_w_gate", "ffn1_w_up"], hosted(ffn_wgrad_rows, hh, [dgt, dup]))
        else:
            chain(("f1u", l), ["ffn1_w_up"], hosted(ffn_wgrad_rows, hh, [dup]))
            chain(("f1g", l), ["ffn1_w_gate"], hosted(ffn_wgrad_rows, hh, [dgt]))

    weights = dict(ffn1_norm=ffn1_norm, ffn1_w_gate=ffn1_w_gate, ffn1_w_up=ffn1_w_up, ffn1_w_down=ffn1_w_down,
                   mix_norm=mix_norm, w_in=w_in, conv_w=conv_w, conv_b=conv_b, conv_ln_g=conv_ln_g,
                   conv_ln_b=conv_ln_b, attn_sinks=attn_sinks, w_out=w_out, ffn2_norm=ffn2_norm,
                   ffn2_w_gate=ffn2_w_gate, ffn2_w_up=ffn2_w_up, ffn2_w_down=ffn2_w_down, final_norm=final_norm)
    moms = dict(ffn1_norm=(m_ffn1_norm, v_ffn1_norm), ffn1_w_gate=(m_ffn1_w_gate, v_ffn1_w_gate),
                ffn1_w_up=(m_ffn1_w_up, v_ffn1_w_up), ffn1_w_down=(m_ffn1_w_down, v_ffn1_w_down),
                mix_norm=(m_mix_norm, v_mix_norm), w_in=(m_w_in, v_w_in), conv_w=(m_conv_w, v_conv_w),
                conv_b=(m_conv_b, v_conv_b), conv_ln_g=(m_conv_ln_g, v_conv_ln_g),
                conv_ln_b=(m_conv_ln_b, v_conv_ln_b), attn_sinks=(m_attn_sinks, v_attn_sinks),
                w_out=(m_w_out, v_w_out), ffn2_norm=(m_ffn2_norm, v_ffn2_norm),
                ffn2_w_gate=(m_ffn2_w_gate, v_ffn2_w_gate), ffn2_w_up=(m_ffn2_w_up, v_ffn2_w_up),
                ffn2_w_down=(m_ffn2_w_down, v_ffn2_w_down), final_norm=(m_final_norm, v_final_norm))
    names = list(weights)
    big_names = dict(ffn2_w_gate=True, ffn2_w_up=True, ffn2_w_down=False, w_out=False, w_in=True,
                     ffn1_w_down=False, ffn1_w_up=True, ffn1_w_gate=True)
    grads, delta, new_m, new_v = {}, {}, {}, {}

    def big_adamw(nme):
        view = tr if big_names[nme] else (lambda a: a)
        res = adamw_layers(view(weights[nme]), view(moms[nme][0]), view(moms[nme][1]),
                           [chains[where[(nme, l)][0]].final[where[(nme, l)][1]] for l in range(L)])
        grads[nme], delta[nme], new_m[nme], new_v[nme] = [view(a) for a in res]

    while active:
        advance(lambda comm: (None, _run_comm(comm)))
    for nme in big_names:
        big_adamw(nme)

    G = {k_: jnp.stack(v_) for k_, v_ in small.items()}
    G["nf"] = dnf
    small_shapes = [shp for _, shp in _SMALL]
    n_small = 1 + sum(math.prod(s) for s in small_shapes)
    rows = -(-n_small // 1024) * 8
    packed = _pack([loss] + [G[k_] for k_, _ in _SMALL], rows)
    summed = all_reduce_small(packed)
    loss_out, *small_sum = _unpack(summed, [()] + small_shapes)
    gs = dict(zip([k_ for k_, _ in _SMALL], small_sum))
    gs["cw"] = lax.dynamic_slice_in_dim(gs["cw"], chip * (CONV_C // N_CHIPS), CONV_C // N_CHIPS, axis=2)
    grads.update(ffn1_norm=gs["n1"], mix_norm=gs["nm"], conv_w=gs["cw"], conv_b=gs["cb"], conv_ln_g=gs["lg"],
                 conv_ln_b=gs["lb"], attn_sinks=gs["sinks"], ffn2_norm=gs["n2"], final_norm=gs["nf"])

    small_names = [nme for nme in names if nme not in big_names]
    s_shapes = [weights[nme].shape for nme in small_names]
    n_tot = sum(math.prod(s) for s in s_shapes)
    srows = -(-n_tot // 1024) * 8
    d, nm_, nv_ = adamw(_pack([weights[nme] for nme in small_names], srows),
                        _pack([grads[nme] for nme in small_names], srows),
                        _pack([moms[nme][0] for nme in small_names], srows),
                        _pack([moms[nme][1] for nme in small_names], srows))
    for nme, dd, mm, vv in zip(small_names, _unpack(d, s_shapes), _unpack(nm_, s_shapes), _unpack(nv_, s_shapes)):
        delta[nme], new_m[nme], new_v[nme] = dd, mm, vv

    return (loss_out, dx[None], *[grads[nme] for nme in names], *[delta[nme] for nme in names],
            *[new_m[nme] for nme in names], *[new_v[nme] for nme in names])
```

```python
import functools
import math

import jax
import jax.numpy as jnp
from jax import lax
from jax.experimental import pallas as pl
from jax.experimental.pallas import tpu as pltpu

F32 = jnp.float32
CDT = jnp.bfloat16
D_MODEL = 1024
D_FF = 2816
N_HEADS = 8
HEAD_DIM = 64
BLOCK = 128
CONV_K = 31
CONV_C = 512
ATT_W = 512
D_EXT = 2048
EPS = 1e-5
HALO = 32
FF_CHUNK = 256
NEG = float(jnp.finfo(jnp.float32).min)
VMEM_LIMIT = 56 * 1024 * 1024

ADAM_LR = 0.001
ADAM_B1 = 0.9
ADAM_B2 = 0.999
ADAM_EPS = 1e-08
ADAM_WD = 0.01
ADAM_STEP = 10

NT = (((1,), (1,)), ((), ()))
TN = (((0,), (0,)), ((), ()))


MESH = pl.DeviceIdType.MESH
ANY = pl.BlockSpec(memory_space=pl.ANY)
N_CHIPS = 4
N_DEV = 8


def _params(n_axes):
    return pltpu.CompilerParams(dimension_semantics=("arbitrary",) * n_axes, vmem_limit_bytes=VMEM_LIMIT)


class _Comm:
    def __init__(self, name, inputs, out_shape, sems, descs, relay=None):
        self.name, self.inputs, self.out_shape, self.sems = name, list(inputs), list(out_shape), list(sems)
        self.descs, self.relay = descs, relay


def _merge(*ops):
    ops = [o for o in ops if o is not None]
    if len(ops) == 1:
        return ops[0]
    assert all(o.relay is None for o in ops)

    def descs(cins, couts, sems):
        out, i, o, s = [], 0, 0, 0
        for op in ops:
            ni, no, ns = len(op.inputs), len(op.out_shape), len(op.sems)
            out += op.descs(cins[i:i + ni], couts[o:o + no], sems[s:s + ns])
            i, o, s = i + ni, o + no, s + ns
        return out

    return _Comm("_".join(o.name for o in ops), sum((o.inputs for o in ops), []),
                 sum((o.out_shape for o in ops), []), sum((o.sems for o in ops), []), descs)


def _split(couts, *ops):
    res, o = [], 0
    for op in ops:
        res.append(couts[o:o + len(op.out_shape)])
        o += len(op.out_shape)
    return res


def _hosted(comm):
    if comm is None:
        return pl.pallas_call

    def make(body, *, name, grid, in_specs, out_specs, out_shape, compiler_params, scratch_shapes=()):
        single = not isinstance(out_shape, (list, tuple))
        o_specs = [out_specs] if single else list(out_specs)
        o_shape = [out_shape] if single else list(out_shape)
        n_in, n_out, n_sc = len(in_specs), len(o_specs), len(scratch_shapes)
        c_in, c_out = len(comm.inputs), len(comm.out_shape)

        def hosted(*refs):
            ins, cins = refs[:n_in], refs[n_in:n_in + c_in]
            o0 = n_in + c_in
            outs, couts = refs[o0:o0 + n_out], refs[o0 + n_out:o0 + n_out + c_out]
            s0 = o0 + n_out + c_out
            scr, sems = refs[s0:s0 + n_sc], refs[s0 + n_sc:]
            first = pl.program_id(0) == 0
            last = pl.program_id(0) == grid[0] - 1
            for ax in range(1, len(grid)):
                first = first & (pl.program_id(ax) == 0)
                last = last & (pl.program_id(ax) == grid[ax] - 1)

            @pl.when(first)
            def _():
                for d in comm.descs(cins, couts, sems):
                    d.start()

            if comm.relay is not None:
                assert len(grid) == 1

                @pl.when(pl.program_id(0) == (3 * grid[0]) // 4)
                def _():
                    for d in comm.descs(cins, couts, sems):
                        d.wait()
                    for d in comm.relay(cins, couts, sems):
                        d.start()

            body(*ins, *outs, *scr)

            @pl.when(last)
            def _():
                for d in (comm.relay or comm.descs)(cins, couts, sems):
                    d.wait()

        call = pl.pallas_call(
            hosted, name=f"{name}_{comm.name}", grid=grid,
            in_specs=list(in_specs) + [ANY] * c_in, out_specs=o_specs + [ANY] * c_out,
            out_shape=o_shape + comm.out_shape, scratch_shapes=list(scratch_shapes) + comm.sems,
            compiler_params=compiler_params)

        def run(*args):
            res = call(*args, *comm.inputs)
            return (res[0] if single else list(res[:n_out])), list(res[n_out:])

        return run

    return make


def _run_comm(comm):
    c_in = len(comm.inputs)

    def body(*refs):
        cins, couts, sems = refs[:c_in], refs[c_in:c_in + len(comm.out_shape)], refs[c_in + len(comm.out_shape):]
        ds = comm.descs(cins, couts, sems)
        for d in ds:
            d.start()
        for d in ds:
            d.wait()

    return list(pl.pallas_call(
        body, name=comm.name, in_specs=[ANY] * c_in, out_specs=[ANY] * len(comm.out_shape),
        out_shape=comm.out_shape, scratch_shapes=comm.sems)(*comm.inputs))


def _coords():
    return lax.axis_index("x"), lax.axis_index("y"), lax.axis_index("c")


def _other_chips(x, y):
    return [(1 - x, y), (x, 1 - y), (1 - x, 1 - y)]


def ag_op(shards):
    n = len(shards)

    def descs(cins, couts, sems):
        send, recv = sems
        x, y, c = _coords()
        j = 2 * x + y
        ds = []
        for a in range(n):
            for p, (px, py) in enumerate(_other_chips(x, y)):
                ds.append(pltpu.make_async_remote_copy(cins[a], couts[a].at[j], send.at[a, p], recv.at[a, p],
                                                       device_id=(px, py, c), device_id_type=MESH))
        return ds

    return _Comm("ag", shards, [jax.ShapeDtypeStruct((N_CHIPS,) + s.shape, s.dtype) for s in shards],
                 [pltpu.SemaphoreType.DMA((n, 3)), pltpu.SemaphoreType.DMA((n, 3))], descs)


def ag2_op(shards):
    n = len(shards)
    halves = [s.reshape(2, s.shape[0] // 2, s.shape[1]) for s in shards]

    def descs(cins, couts, sems):
        x, y, c = _coords()
        j = 2 * x + y
        return [pltpu.make_async_remote_copy(cins[a].at[c], couts[a].at[j, c], sems[0].at[a, p], sems[1].at[a, p],
                                             device_id=(px, py, c), device_id_type=MESH)
                for a in range(n) for p, (px, py) in enumerate(_other_chips(x, y))]

    def relay(cins, couts, sems):
        x, y, c = _coords()
        return [pltpu.make_async_remote_copy(couts[a].at[2 * px + py, c], couts[a].at[2 * px + py, c],
                                             sems[2].at[a, p], sems[3].at[a, p],
                                             device_id=(x, y, 1 - c), device_id_type=MESH)
                for a in range(n) for p, (px, py) in enumerate(_other_chips(x, y))]

    return _Comm("ag2", halves, [jax.ShapeDtypeStruct((N_CHIPS,) + h.shape, h.dtype) for h in halves],
                 [pltpu.SemaphoreType.DMA((n, 3))] * 4, descs, relay)


def _own_slab(gathered, mine, idx):
    return lax.dynamic_update_slice_in_dim(gathered, mine[None], idx, axis=0)


def first_gather(shards):
    n = len(shards)
    halves = [s.reshape(2, s.shape[0] // 2, s.shape[1]) for s in shards]

    def body(*refs):
        ins, outs = refs[:n], refs[n:2 * n]
        send1, recv1, send2, recv2 = refs[2 * n:]
        x, y, c = _coords()
        j = 2 * x + y
        chips = _other_chips(x, y)
        ici = [pltpu.make_async_remote_copy(ins[a].at[c], outs[a].at[j, c], send1.at[a, p], recv1.at[a, p],
                                            device_id=(px, py, c), device_id_type=MESH)
               for a in range(n) for p, (px, py) in enumerate(chips)]
        for d in ici:
            d.start()
        d2d = [pltpu.make_async_remote_copy(outs[a].at[2 * px + py, c], outs[a].at[2 * px + py, c],
                                            send2.at[a, p], recv2.at[a, p],
                                            device_id=(x, y, 1 - c), device_id_type=MESH)
               for a in range(n) for p, (px, py) in enumerate(chips)]
        for d1, d2 in zip(ici, d2d):
            d1.wait()
            d2.start()
        for d in d2d:
            d.wait()

    return list(pl.pallas_call(
        body, name="first_gather", in_specs=[ANY] * n, out_specs=[ANY] * n,
        out_shape=[jax.ShapeDtypeStruct((N_CHIPS,) + h.shape, h.dtype) for h in halves],
        scratch_shapes=[pltpu.SemaphoreType.DMA((n, 3))] * 4)(*halves))


def swap_op(grads):
    n = len(grads)

    def descs(cins, couts, sems):
        send, recv = sems
        x, y, c = _coords()
        return [pltpu.make_async_remote_copy(cins[a].at[:, 1 - c], couts[a], send.at[a], recv.at[a],
                                             device_id=(x, y, 1 - c), device_id_type=MESH) for a in range(n)]

    return _Comm("swap", grads, [jax.ShapeDtypeStruct(g.shape[:1] + g.shape[2:], g.dtype) for g in grads],
                 [pltpu.SemaphoreType.DMA((n,)), pltpu.SemaphoreType.DMA((n,))], descs)


def exchange_op(parts):
    n = len(parts)

    def descs(cins, couts, sems):
        send, recv = sems
        x, y, c = _coords()
        ds = []
        for a in range(n):
            for p, (px, py) in enumerate(_other_chips(x, y)):
                ds.append(pltpu.make_async_remote_copy(cins[a].at[2 * px + py], couts[a].at[p], send.at[a, p],
                                                       recv.at[a, p], device_id=(px, py, c), device_id_type=MESH))
        return ds

    return _Comm("xchg", parts, [jax.ShapeDtypeStruct((3,) + p.shape[1:], p.dtype) for p in parts],
                 [pltpu.SemaphoreType.DMA((n, 3)), pltpu.SemaphoreType.DMA((n, 3))], descs)


def share_op(totals):
    n = len(totals)

    def descs(cins, couts, sems):
        send, recv = sems
        x, y, c = _coords()
        return [pltpu.make_async_remote_copy(cins[a], couts[a].at[c], send.at[a], recv.at[a],
                                             device_id=(x, y, 1 - c), device_id_type=MESH) for a in range(n)]

    return _Comm("share", totals, [jax.ShapeDtypeStruct((2,) + t.shape, t.dtype) for t in totals],
                 [pltpu.SemaphoreType.DMA((n,)), pltpu.SemaphoreType.DMA((n,))], descs)


def _sigmoid(z):
    return 1.0 / (1.0 + jnp.exp(-z))


def _rms_parts(xf):
    r = lax.rsqrt(jnp.mean(xf * xf, axis=-1, keepdims=True) + EPS)
    return xf * r, r


def _rms_bwd(dh, xhat, r, g):
    dg = jnp.sum(dh * xhat, axis=0, keepdims=True)
    dxhat = dh * g
    dx = r * (dxhat - xhat * jnp.mean(dxhat * xhat, axis=-1, keepdims=True))
    return dx, dg


def _chunks(n, ck=FF_CHUNK):
    return [(c0, min(ck, n - c0)) for c0 in range(0, n, ck)]


def ffn_fwd(x, g, wgT, wuT, wd, *, mix=None, tm=512, comm=None):
    S, D = x.shape
    F = wgT.shape[0]
    n_mix = 0 if mix is None else 3

    def body(x_ref, g_ref, wg_ref, wu_ref, wd_ref, *refs):
        (o_ref, gate_ref, up_ref), a_sc = refs[n_mix:n_mix + 3], refs[-1]
        xf = x_ref[...]
        if mix is not None:
            ao_ref, co_ref, wo_ref = refs[:3]
            xf = (xf + jnp.dot(ao_ref[...], wo_ref[0:ATT_W, :], preferred_element_type=F32)
                  + jnp.dot(co_ref[...], wo_ref[ATT_W:ATT_W + CONV_C, :], preferred_element_type=F32))
            refs[n_mix + 3][...] = xf
        xhat, _ = _rms_parts(xf)
        h = (xhat * g_ref[...]).astype(CDT)
        for c0, cw_ in _chunks(F):
            sl = slice(c0, c0 + cw_)
            gt = lax.dot_general(h, wg_ref[sl, :], NT, preferred_element_type=F32)
            ut = lax.dot_general(h, wu_ref[sl, :], NT, preferred_element_type=F32)
            gate_ref[:, sl] = gt.astype(CDT)
            up_ref[:, sl] = ut.astype(CDT)
            a_sc[:, sl] = (gt * _sigmoid(gt) * ut).astype(CDT)
        o_ref[...] = xf + 0.5 * jnp.dot(a_sc[...], wd_ref[...], preferred_element_type=F32)

    wspec = pl.BlockSpec((F, D), lambda i: (0, 0), pipeline_mode=pl.Buffered(1))
    row = pl.BlockSpec((tm, D), lambda i: (i, 0))
    act = pl.BlockSpec((tm, F), lambda i: (i, 0))
    in_specs = [row, pl.BlockSpec((1, D), lambda i: (0, 0)), wspec, wspec, wspec]
    out_specs = [row, act, act]
    out_shape = [jax.ShapeDtypeStruct((S, D), F32), jax.ShapeDtypeStruct((S, F), CDT),
                 jax.ShapeDtypeStruct((S, F), CDT)]
    if mix is not None:
        in_specs += [pl.BlockSpec((tm, ATT_W), lambda i: (i, 0)), pl.BlockSpec((tm, CONV_C), lambda i: (i, 0)),
                     pl.BlockSpec((D, D), lambda i: (0, 0), pipeline_mode=pl.Buffered(1))]
        out_specs.append(row)
        out_shape.append(jax.ShapeDtypeStruct((S, D), F32))
    return _hosted(comm)(
        body, name="ffn_fwd" if mix is None else "mix_ffn_fwd",
        grid=(S // tm,),
        in_specs=in_specs, out_specs=out_specs, out_shape=out_shape,
        scratch_shapes=[pltpu.VMEM((tm, F), CDT)],
        compiler_params=_params(1),
    )(x, g, wgT, wuT, wd, *(mix or ()))


def ffn_bwd_dgrad(x, g, dy, gate, up, wgT, wuT, wd, *, tm=256, comm=None):
    S, D = x.shape
    F = wgT.shape[0]

    def body(x_ref, g_ref, dy_ref, gate_ref, up_ref, wg_ref, wu_ref, wd_ref,
             dx_ref, dgate_ref, dup_ref, h_ref, dg_ref, dwd_ref):
        @pl.when(pl.program_id(0) == 0)
        def _():
            dg_ref[...] = jnp.zeros_like(dg_ref)
            dwd_ref[...] = jnp.zeros_like(dwd_ref)

        dyf = dy_ref[...]
        dacc = (0.5 * dyf).astype(CDT)
        gg = g_ref[...]
        xhat, r = _rms_parts(x_ref[...])
        h_ref[...] = (xhat * gg).astype(CDT)
        for c0, cw_ in _chunks(F):
            sl = slice(c0, c0 + cw_)
            d_a = lax.dot_general(dacc, wd_ref[sl, :], NT, preferred_element_type=F32)
            gt = gate_ref[:, sl].astype(F32)
            ut = up_ref[:, sl].astype(F32)
            sg = _sigmoid(gt)
            silu = gt * sg
            dup_ref[:, sl] = (d_a * silu).astype(CDT)
            dgate_ref[:, sl] = (d_a * ut * (sg * (1.0 + gt * (1.0 - sg)))).astype(CDT)
            dwd_ref[sl, :] += lax.dot_general((silu * ut).astype(CDT), dacc, TN, preferred_element_type=F32)
        dh = (jnp.dot(dgate_ref[...], wg_ref[...], preferred_element_type=F32)
              + jnp.dot(dup_ref[...], wu_ref[...], preferred_element_type=F32))
        dx, dg = _rms_bwd(dh, xhat, r, gg)
        dx_ref[...] = dyf + dx
        dg_ref[...] += dg

    wspec = pl.BlockSpec((F, D), lambda i: (0, 0), pipeline_mode=pl.Buffered(1))
    row = pl.BlockSpec((tm, D), lambda i: (i, 0))
    act = pl.BlockSpec((tm, F), lambda i: (i, 0))
    vec = pl.BlockSpec((1, D), lambda i: (0, 0))
    return _hosted(comm)(
        body, name="ffn_bwd_dgrad",
        grid=(S // tm,),
        in_specs=[row, vec, row, act, act, wspec, wspec, wspec],
        out_specs=[row, act, act, row, vec, wspec],
        out_shape=[jax.ShapeDtypeStruct((S, D), F32),
                   jax.ShapeDtypeStruct((S, F), CDT),
                   jax.ShapeDtypeStruct((S, F), CDT),
                   jax.ShapeDtypeStruct((S, D), CDT),
                   jax.ShapeDtypeStruct((1, D), F32),
                   jax.ShapeDtypeStruct((F, D), F32)],
        compiler_params=_params(1),
    )(x, g, dy, gate, up, wgT, wuT, wd)


def ffn_wgrad_rows(h, acts, *, tk=1024, comm=None):
    S, D = h.shape
    F = acts[0].shape[1]
    n = len(acts)
    tk = min(tk, S)

    def body(h_ref, *refs):
        act_refs, dw_refs = refs[:n], refs[n:]

        @pl.when(pl.program_id(0) == 0)
        def _():
            for dw_ref in dw_refs:
                dw_ref[...] = jnp.zeros_like(dw_ref)

        hh = h_ref[...]
        for c0, cw_ in _chunks(F):
            sl = slice(c0, c0 + cw_)
            for act_ref, dw_ref in zip(act_refs, dw_refs):
                dw_ref[sl, :] += lax.dot_general(act_ref[:, sl], hh, TN, preferred_element_type=F32)

    act = pl.BlockSpec((tk, F), lambda k: (k, 0))
    out = pl.BlockSpec((F, D), lambda k: (0, 0), pipeline_mode=pl.Buffered(1))
    return _hosted(comm)(
        body, name="ffn_wgrad_rows",
        grid=(S // tk,),
        in_specs=[pl.BlockSpec((tk, D), lambda k: (k, 0))] + [act] * n,
        out_specs=[out] * n,
        out_shape=[jax.ShapeDtypeStruct((F, D), F32)] * n,
        compiler_params=_params(1),
    )(h, *acts)


def loss_head(x, g, target, *, tm=512):
    S, D = x.shape

    def body(x_ref, g_ref, t_ref, loss_ref, dx_ref, dg_ref):
        @pl.when(pl.program_id(0) == 0)
        def _():
            loss_ref[...] = jnp.zeros_like(loss_ref)
            dg_ref[...] = jnp.zeros_like(dg_ref)

        xhat, r = _rms_parts(x_ref[...])
        gg = g_ref[...]
        err = xhat * gg - t_ref[...]
        loss_ref[...] += 0.5 * jnp.sum(jnp.mean(err * err, axis=-1, keepdims=True), axis=0, keepdims=True)
        dx, dg = _rms_bwd(err * (1.0 / D), xhat, r, gg)
        dx_ref[...] = dx
        dg_ref[...] += dg

    row = pl.BlockSpec((tm, D), lambda i: (i, 0))
    vec = pl.BlockSpec((1, D), lambda i: (0, 0))
    return pl.pallas_call(
        body, name="loss_head",
        grid=(S // tm,),
        in_specs=[row, vec, row],
        out_specs=[pl.BlockSpec((1, 1), lambda i: (0, 0)), row, vec],
        out_shape=[jax.ShapeDtypeStruct((1, 1), F32), jax.ShapeDtypeStruct((S, D), F32),
                   jax.ShapeDtypeStruct((1, D), F32)],
        compiler_params=_params(1),
    )(x, g, target)


def _rope_apply(t, cs, sn):
    lane = lax.broadcasted_iota(jnp.int32, t.shape, 1)
    first = (lane % HEAD_DIM) < (HEAD_DIM // 2)
    rot = jnp.where(first, pltpu.roll(t, 128 - HEAD_DIM // 2, 1), pltpu.roll(t, HEAD_DIM // 2, 1))
    return t * cs + rot * sn


def _rope_transpose(d, cs, sn):
    lane = lax.broadcasted_iota(jnp.int32, d.shape, 1)
    first = (lane % HEAD_DIM) < (HEAD_DIM // 2)
    ds = d * sn
    rot = jnp.where(first, pltpu.roll(ds, 128 - HEAD_DIM // 2, 1), pltpu.roll(ds, HEAD_DIM // 2, 1))
    return d * cs + rot


def inproj_fwd(x, g, wextT, cs, sn, *, tm=512, comm=None):
    S, D = x.shape
    scale = HEAD_DIM ** -0.5

    def body(x_ref, g_ref, w_ref, cs_ref, sn_ref, q_ref, k_ref, v_ref, u_ref):
        xhat, _ = _rms_parts(x_ref[...])
        h = (xhat * g_ref[...]).astype(CDT)
        p = lax.dot_general(h, w_ref[...], NT, preferred_element_type=F32)
        c, s = cs_ref[...], sn_ref[...]
        for b in range(4):
            q_ref[:, 128 * b:128 * (b + 1)] = (_rope_apply(p[:, 128 * b:128 * (b + 1)], c, s) * scale).astype(CDT)
        for b in range(2):
            k_ref[:, 128 * b:128 * (b + 1)] = _rope_apply(p[:, 512 + 128 * b:512 + 128 * (b + 1)], c, s).astype(CDT)
        v_ref[...] = p[:, 768:1024].astype(CDT)
        u_ref[...] = p[:, 1024:2048]

    def row(w):
        return pl.BlockSpec((tm, w), lambda i: (i, 0))

    return _hosted(comm)(
        body, name="inproj_fwd",
        grid=(S // tm,),
        in_specs=[row(D), pl.BlockSpec((1, D), lambda i: (0, 0)),
                  pl.BlockSpec((D_EXT, D), lambda i: (0, 0)), row(128), row(128)],
        out_specs=[row(512), row(256), row(256), row(1024)],
        out_shape=[jax.ShapeDtypeStruct((S, 512), CDT), jax.ShapeDtypeStruct((S, 256), CDT),
                   jax.ShapeDtypeStruct((S, 256), CDT), jax.ShapeDtypeStruct((S, 1024), F32)],
        compiler_params=_params(1),
    )(x, g, wextT, cs, sn)


def _stack_heads(p0, p1):
    lane = lax.broadcasted_iota(jnp.int32, p0.shape, 1)
    lo = lane < HEAD_DIM
    z = jnp.zeros_like(p0)
    return jnp.concatenate([jnp.where(lo, p0, z), jnp.where(lo, z, p0),
                            jnp.where(lo, p1, z), jnp.where(lo, z, p1)], axis=0)


def _unstack_heads(o):
    lane = lax.broadcasted_iota(jnp.int32, (BLOCK, 128), 1)
    lo = lane < HEAD_DIM
    return (jnp.where(lo, o[0:128], o[128:256]), jnp.where(lo, o[256:384], o[384:512]))


def _band_mask_kq(n):
    c = lax.broadcasted_iota(jnp.int32, (2 * BLOCK, 4 * BLOCK), 0)
    i = lax.broadcasted_iota(jnp.int32, (2 * BLOCK, 4 * BLOCK), 1) % BLOCK
    return (c > i) & (c <= i + BLOCK) & ((n > 0) | (c >= BLOCK))


def attn_fwd(q, k, v, sink_row, *, nb=4, comm=None):
    S = q.shape[0]
    tq = nb * BLOCK

    def body(q_ref, k_ref, v_ref, sink_ref, o_ref):
        t = pl.program_id(0)
        for b in range(nb):
            n = t * nb + b
            prev = pl.multiple_of(jnp.maximum(n - 1, 0) * BLOCK, BLOCK)
            cur = pl.multiple_of(n * BLOCK, BLOCK)
            rows = slice(b * BLOCK, (b + 1) * BLOCK)
            mask = _band_mask_kq(n)
            for gidx in range(2):
                lanes = slice(128 * gidx, 128 * (gidx + 1))
                qs = _stack_heads(q_ref[rows, 256 * gidx:256 * gidx + 128],
                                  q_ref[rows, 256 * gidx + 128:256 * gidx + 256])
                kb = jnp.concatenate([k_ref[pl.ds(prev, BLOCK), lanes], k_ref[pl.ds(cur, BLOCK), lanes]], axis=0)
                vb = jnp.concatenate([v_ref[pl.ds(prev, BLOCK), lanes], v_ref[pl.ds(cur, BLOCK), lanes]], axis=0)
                st = lax.dot_general(kb, qs, NT, preferred_element_type=F32)
                st = jnp.where(mask, st, NEG)
                sink = sink_ref[gidx]
                m = jnp.maximum(jnp.max(st, axis=0, keepdims=True), sink)
                e = jnp.exp(st - m)
                inv = 1.0 / (jnp.sum(e, axis=0, keepdims=True) + jnp.exp(sink - m))
                o = lax.dot_general((e * inv).astype(CDT), vb, TN, preferred_element_type=F32)
                o0, o1 = _unstack_heads(o)
                o_ref[rows, 256 * gidx:256 * gidx + 128] = o0.astype(CDT)
                o_ref[rows, 256 * gidx + 128:256 * gidx + 256] = o1.astype(CDT)

    return _hosted(comm)(
        body, name="attn_fwd",
        grid=(S // tq,),
        in_specs=[pl.BlockSpec((tq, 512), lambda t: (t, 0)),
                  pl.BlockSpec((S, 256), lambda t: (0, 0)),
                  pl.BlockSpec((S, 256), lambda t: (0, 0)),
                  pl.BlockSpec((2, 1, 4 * BLOCK), lambda t: (0, 0, 0))],
        out_specs=pl.BlockSpec((tq, 512), lambda t: (t, 0)),
        out_shape=jax.ShapeDtypeStruct((S, 512), CDT),
        compiler_params=_params(1),
    )(q, k, v, sink_row)


def attn_bwd(q, k, v, do, sink_row, *, nb=4, comm=None):
    S = q.shape[0]
    tq = nb * BLOCK
    scale = HEAD_DIM ** -0.5

    def body(q_ref, k_ref, v_ref, do_ref, sink_ref, dq_ref, dk_ref, dv_ref, dsink_ref):
        t = pl.program_id(0)

        @pl.when(t == 0)
        def _():
            dk_ref[...] = jnp.zeros_like(dk_ref)
            dv_ref[...] = jnp.zeros_like(dv_ref)
            dsink_ref[...] = jnp.zeros_like(dsink_ref)

        for b in range(nb):
            n = t * nb + b
            prev = pl.multiple_of(jnp.maximum(n - 1, 0) * BLOCK, BLOCK)
            cur = pl.multiple_of(n * BLOCK, BLOCK)
            rows = slice(b * BLOCK, (b + 1) * BLOCK)
            mask = _band_mask_kq(n)
            for gidx in range(2):
                lanes = slice(128 * gidx, 128 * (gidx + 1))
                qs = _stack_heads(q_ref[rows, 256 * gidx:256 * gidx + 128],
                                  q_ref[rows, 256 * gidx + 128:256 * gidx + 256])
                dos = _stack_heads(do_ref[rows, 256 * gidx:256 * gidx + 128],
                                   do_ref[rows, 256 * gidx + 128:256 * gidx + 256])
                kb = jnp.concatenate([k_ref[pl.ds(prev, BLOCK), lanes], k_ref[pl.ds(cur, BLOCK), lanes]], axis=0)
                vb = jnp.concatenate([v_ref[pl.ds(prev, BLOCK), lanes], v_ref[pl.ds(cur, BLOCK), lanes]], axis=0)
                st = lax.dot_general(kb, qs, NT, preferred_element_type=F32)
                st = jnp.where(mask, st, NEG)
                sink = sink_ref[gidx]
                m = jnp.maximum(jnp.max(st, axis=0, keepdims=True), sink)
                e = jnp.exp(st - m)
                es = jnp.exp(sink - m)
                inv = 1.0 / (jnp.sum(e, axis=0, keepdims=True) + es)
                pt = e * inv
                dpt = lax.dot_general(vb, dos, NT, preferred_element_type=F32)
                delta = jnp.sum(pt * dpt, axis=0, keepdims=True)
                dst = (pt * (dpt - delta)).astype(CDT)
                dsink_ref[gidx] += -(es * inv) * delta
                dvb = jnp.dot(pt.astype(CDT), dos, preferred_element_type=F32)
                dkb = jnp.dot(dst, qs, preferred_element_type=F32)
                dqs = lax.dot_general(dst, kb, TN, preferred_element_type=F32) * scale
                dq0, dq1 = _unstack_heads(dqs)
                dq_ref[rows, 256 * gidx:256 * gidx + 128] = dq0
                dq_ref[rows, 256 * gidx + 128:256 * gidx + 256] = dq1
                dk_ref[pl.ds(prev, BLOCK), lanes] += dkb[0:BLOCK]
                dk_ref[pl.ds(cur, BLOCK), lanes] += dkb[BLOCK:2 * BLOCK]
                dv_ref[pl.ds(prev, BLOCK), lanes] += dvb[0:BLOCK]
                dv_ref[pl.ds(cur, BLOCK), lanes] += dvb[BLOCK:2 * BLOCK]

    full = pl.BlockSpec((S, 256), lambda t: (0, 0))
    tile = pl.BlockSpec((tq, 512), lambda t: (t, 0))
    srow = pl.BlockSpec((2, 1, 4 * BLOCK), lambda t: (0, 0, 0))
    return _hosted(comm)(
        body, name="attn_bwd",
        grid=(S // tq,),
        in_specs=[tile, full, full, tile, srow],
        out_specs=[tile, full, full, srow],
        out_shape=[jax.ShapeDtypeStruct((S, 512), F32), jax.ShapeDtypeStruct((S, 256), F32),
                   jax.ShapeDtypeStruct((S, 256), F32), jax.ShapeDtypeStruct((2, 1, 4 * BLOCK), F32)],
        compiler_params=_params(1),
    )(q, k, v, do, sink_row)


def _glu(u):
    a = u[:, 0:CONV_C]
    gt = u[:, CONV_C:2 * CONV_C]
    sg = _sigmoid(gt)
    return a, sg, a * sg


CONV_CHUNK = 32


def _shifted_copies(buf, shifted, n):
    for r in range(1, 8):
        shifted[r - 1, 0:n, :] = buf[r:r + n, :]


def _shifted_rows(buf, shifted, start, rows):
    r = start % 8
    if r == 0:
        return buf[start:start + rows, :]
    return shifted[r - 1, start - r:start - r + rows, :]


def conv_fwd(u, cw, cb, lg, lb, *, tm=512, comm=None):
    S = u.shape[0]
    nh = tm // HALO

    def body(u_ref, uh_ref, cw_ref, cb_ref, lg_ref, lb_ref, o_ref, y_ref, hbuf, hsh):
        t = pl.program_id(0)
        _, _, hg = _glu(u_ref[...])
        _, _, hh = _glu(uh_ref[...])
        hbuf[0:HALO, :] = jnp.where(t > 0, hh, jnp.zeros_like(hh))
        hbuf[HALO:HALO + tm, :] = hg
        hbuf[HALO + tm:HALO + tm + 8, :] = jnp.zeros((8, CONV_C), F32)
        _shifted_copies(hbuf, hsh, HALO + tm)
        off = HALO - (CONV_K - 1)
        for c0 in range(0, tm, CONV_CHUNK):
            acc = jnp.zeros((CONV_CHUNK, CONV_C), F32) + cb_ref[...]
            for j in range(CONV_K):
                acc = acc + cw_ref[j:j + 1, :] * _shifted_rows(hbuf, hsh, c0 + off + j, CONV_CHUNK)
            y_ref[c0:c0 + CONV_CHUNK, :] = acc
        y = y_ref[...]
        yc = y - jnp.mean(y, axis=-1, keepdims=True)
        r = lax.rsqrt(jnp.mean(yc * yc, axis=-1, keepdims=True) + EPS)
        z = yc * r * lg_ref[...] + lb_ref[...]
        o_ref[...] = (z * _sigmoid(z)).astype(CDT)

    vec = pl.BlockSpec((1, CONV_C), lambda t: (0, 0))
    return _hosted(comm)(
        body, name="conv_fwd",
        grid=(S // tm,),
        in_specs=[pl.BlockSpec((tm, 2 * CONV_C), lambda t: (t, 0)),
                  pl.BlockSpec((HALO, 2 * CONV_C), lambda t: (jnp.maximum(t * nh - 1, 0), 0)),
                  pl.BlockSpec((CONV_K, CONV_C), lambda t: (0, 0)), vec, vec, vec],
        out_specs=[pl.BlockSpec((tm, CONV_C), lambda t: (t, 0)), pl.BlockSpec((tm, CONV_C), lambda t: (t, 0))],
        out_shape=[jax.ShapeDtypeStruct((S, CONV_C), CDT), jax.ShapeDtypeStruct((S, CONV_C), F32)],
        scratch_shapes=[pltpu.VMEM((HALO + tm + 8, CONV_C), F32), pltpu.VMEM((7, HALO + tm, CONV_C), F32)],
        compiler_params=_params(1),
    )(u, u, cw, cb, lg, lb)


def conv_bwd(dc, u, y, cw, lg, lb, *, tm=512, comm=None):
    S = u.shape[0]
    nh = tm // HALO
    nt = S // tm

    def ln_bwd(dcv, yv, lgv, lbv):
        yc = yv - jnp.mean(yv, axis=-1, keepdims=True)
        r = lax.rsqrt(jnp.mean(yc * yc, axis=-1, keepdims=True) + EPS)
        yhat = yc * r
        z = yhat * lgv + lbv
        sg = _sigmoid(z)
        dz = dcv * (sg * (1.0 + z * (1.0 - sg)))
        dyhat = dz * lgv
        dy = r * (dyhat - jnp.mean(dyhat, axis=-1, keepdims=True)
                  - yhat * jnp.mean(dyhat * yhat, axis=-1, keepdims=True))
        return dy, dz, yhat

    def body(dc_ref, dcn_ref, u_ref, uh_ref, y_ref, yn_ref, cw_ref, lg_ref, lb_ref,
             du_ref, dcw_ref, dcb_ref, dlg_ref, dlb_ref, hbuf, dybuf, dhg_sc, dw_sc, hsh, dysh):
        t = pl.program_id(0)

        @pl.when(t == 0)
        def _():
            dw_sc[...] = jnp.zeros_like(dw_sc)
            dcb_ref[...] = jnp.zeros_like(dcb_ref)
            dlg_ref[...] = jnp.zeros_like(dlg_ref)
            dlb_ref[...] = jnp.zeros_like(dlb_ref)

        lgv, lbv = lg_ref[...], lb_ref[...]
        dy, dz, yhat = ln_bwd(dc_ref[...].astype(F32), y_ref[...], lgv, lbv)
        dyn, _, _ = ln_bwd(dcn_ref[...].astype(F32), yn_ref[...], lgv, lbv)
        dlb_ref[...] += jnp.sum(dz, axis=0, keepdims=True)
        dlg_ref[...] += jnp.sum(dz * yhat, axis=0, keepdims=True)
        dcb_ref[...] += jnp.sum(dy, axis=0, keepdims=True)
        dybuf[0:tm, :] = dy
        dybuf[tm:tm + HALO, :] = jnp.where(t < nt - 1, dyn, jnp.zeros_like(dyn))
        dybuf[tm + HALO:tm + HALO + 8, :] = jnp.zeros((8, CONV_C), F32)
        _shifted_copies(dybuf, dysh, tm + HALO)

        a, sg, hg = _glu(u_ref[...])
        _, _, hh = _glu(uh_ref[...])
        hbuf[0:HALO, :] = jnp.where(t > 0, hh, jnp.zeros_like(hh))
        hbuf[HALO:HALO + tm, :] = hg
        hbuf[HALO + tm:HALO + tm + 8, :] = jnp.zeros((8, CONV_C), F32)
        _shifted_copies(hbuf, hsh, HALO + tm)

        off = HALO - (CONV_K - 1)
        for c0 in range(0, tm, CONV_CHUNK):
            acc = jnp.zeros((CONV_CHUNK, CONV_C), F32)
            dyc = dybuf[c0:c0 + CONV_CHUNK, :]
            for j in range(CONV_K):
                acc = acc + cw_ref[j:j + 1, :] * _shifted_rows(dybuf, dysh, c0 + (CONV_K - 1) - j, CONV_CHUNK)
                prod = dyc * _shifted_rows(hbuf, hsh, c0 + off + j, CONV_CHUNK)
                dw_sc[j] += jnp.sum(prod.reshape(CONV_CHUNK // 8, 8, CONV_C), axis=0)
            dhg_sc[c0:c0 + CONV_CHUNK, :] = acc

        dhg = dhg_sc[...]
        du_ref[:, 0:CONV_C] = dhg * sg
        du_ref[:, CONV_C:2 * CONV_C] = dhg * a * sg * (1.0 - sg)

        @pl.when(t == nt - 1)
        def _():
            dcw_ref[...] = jnp.sum(dw_sc[...], axis=1)

    vec = pl.BlockSpec((1, CONV_C), lambda t: (0, 0))
    tile = pl.BlockSpec((tm, CONV_C), lambda t: (t, 0))
    nxt = pl.BlockSpec((HALO, CONV_C), lambda t: (jnp.minimum((t + 1) * nh, S // HALO - 1), 0))
    return _hosted(comm)(
        body, name="conv_bwd",
        grid=(nt,),
        in_specs=[tile, nxt,
                  pl.BlockSpec((tm, 2 * CONV_C), lambda t: (t, 0)),
                  pl.BlockSpec((HALO, 2 * CONV_C), lambda t: (jnp.maximum(t * nh - 1, 0), 0)),
                  tile, nxt,
                  pl.BlockSpec((CONV_K, CONV_C), lambda t: (0, 0)), vec, vec],
        out_specs=[pl.BlockSpec((tm, 2 * CONV_C), lambda t: (t, 0)),
                   pl.BlockSpec((CONV_K, CONV_C), lambda t: (0, 0)), vec, vec, vec],
        out_shape=[jax.ShapeDtypeStruct((S, 2 * CONV_C), F32), jax.ShapeDtypeStruct((CONV_K, CONV_C), F32),
                   jax.ShapeDtypeStruct((1, CONV_C), F32), jax.ShapeDtypeStruct((1, CONV_C), F32),
                   jax.ShapeDtypeStruct((1, CONV_C), F32)],
        scratch_shapes=[pltpu.VMEM((HALO + tm + 8, CONV_C), F32), pltpu.VMEM((tm + HALO + 8, CONV_C), F32),
                        pltpu.VMEM((tm, CONV_C), F32), pltpu.VMEM((CONV_K, 8, CONV_C), F32),
                        pltpu.VMEM((7, HALO + tm, CONV_C), F32), pltpu.VMEM((7, tm + HALO, CONV_C), F32)],
        compiler_params=_params(1),
    )(dc, dc, u, u, y, y, cw, lg, lb)


def outproj_bwd(dx, ao, co, wout, *, tm=512, comm=None):
    S, D = dx.shape

    def body(dx_ref, a_ref, c_ref, w_ref, da_ref, dc_ref, dw_ref):
        @pl.when(pl.program_id(0) == 0)
        def _():
            dw_ref[...] = jnp.zeros_like(dw_ref)

        dxb = dx_ref[...].astype(CDT)
        da_ref[...] = lax.dot_general(dxb, w_ref[0:ATT_W, :], NT, preferred_element_type=F32).astype(CDT)
        dc_ref[...] = lax.dot_general(dxb, w_ref[ATT_W:ATT_W + CONV_C, :], NT, preferred_element_type=F32)
        dw_ref[0:ATT_W, :] += lax.dot_general(a_ref[...], dxb, TN, preferred_element_type=F32)
        dw_ref[ATT_W:ATT_W + CONV_C, :] += lax.dot_general(c_ref[...], dxb, TN, preferred_element_type=F32)

    return _hosted(comm)(
        body, name="outproj_bwd",
        grid=(S // tm,),
        in_specs=[pl.BlockSpec((tm, D), lambda i: (i, 0)), pl.BlockSpec((tm, ATT_W), lambda i: (i, 0)),
                  pl.BlockSpec((tm, CONV_C), lambda i: (i, 0)), pl.BlockSpec((D, D), lambda i: (0, 0))],
        out_specs=[pl.BlockSpec((tm, ATT_W), lambda i: (i, 0)), pl.BlockSpec((tm, CONV_C), lambda i: (i, 0)),
                   pl.BlockSpec((D, D), lambda i: (0, 0))],
        out_shape=[jax.ShapeDtypeStruct((S, ATT_W), CDT), jax.ShapeDtypeStruct((S, CONV_C), F32),
                   jax.ShapeDtypeStruct((D, D), F32)],
        compiler_params=_params(1),
    )(dx, ao, co, wout)


def inproj_bwd(x, g, dres, dq, dk, dv, du, wextT, cs, sn, *, tm=512):
    S, D = x.shape

    def body(x_ref, g_ref, dres_ref, dq_ref, dk_ref, dv_ref, du_ref, w_ref, cs_ref, sn_ref,
             dx_ref, dw_ref, dg_ref, dp_sc):
        @pl.when(pl.program_id(0) == 0)
        def _():
            dw_ref[...] = jnp.zeros_like(dw_ref)
            dg_ref[...] = jnp.zeros_like(dg_ref)

        c, s = cs_ref[...], sn_ref[...]
        for b in range(4):
            dp_sc[:, 128 * b:128 * (b + 1)] = _rope_transpose(dq_ref[:, 128 * b:128 * (b + 1)], c, s).astype(CDT)
        for b in range(2):
            dp_sc[:, 512 + 128 * b:512 + 128 * (b + 1)] = _rope_transpose(
                dk_ref[:, 128 * b:128 * (b + 1)], c, s).astype(CDT)
        dp_sc[:, 768:1024] = dv_ref[...].astype(CDT)
        dp_sc[:, 1024:2048] = du_ref[...].astype(CDT)
        dp = dp_sc[...]
        xhat, r = _rms_parts(x_ref[...])
        gg = g_ref[...]
        h = (xhat * gg).astype(CDT)
        dh = jnp.dot(dp, w_ref[...], preferred_element_type=F32)
        dw_ref[...] += lax.dot_general(dp, h, TN, preferred_element_type=F32)
        dx, dg = _rms_bwd(dh, xhat, r, gg)
        dx_ref[...] = dres_ref[...] + dx
        dg_ref[...] += dg

    def row(w):
        return pl.BlockSpec((tm, w), lambda i: (i, 0))

    return pl.pallas_call(
        body, name="inproj_bwd",
        grid=(S // tm,),
        in_specs=[row(D), pl.BlockSpec((1, D), lambda i: (0, 0)), row(D), row(512), row(256), row(256),
                  row(1024), pl.BlockSpec((D_EXT, D), lambda i: (0, 0), pipeline_mode=pl.Buffered(1)),
                  row(128), row(128)],
        out_specs=[row(D), pl.BlockSpec((D_EXT, D), lambda i: (0, 0), pipeline_mode=pl.Buffered(1)),
                   pl.BlockSpec((1, D), lambda i: (0, 0))],
        out_shape=[jax.ShapeDtypeStruct((S, D), F32), jax.ShapeDtypeStruct((D_EXT, D), F32),
                   jax.ShapeDtypeStruct((1, D), F32)],
        scratch_shapes=[pltpu.VMEM((tm, D_EXT), CDT)],
        compiler_params=_params(1),
    )(x, g, dres, dq, dk, dv, du, wextT, cs, sn)


def _rope_tables(positions):
    inv_freq = 1.0 / (10000.0 ** (jnp.arange(0, HEAD_DIM, 2, dtype=F32) / HEAD_DIM))
    ang = positions.astype(F32).reshape(-1, 1) * inv_freq
    cos, sin = jnp.cos(ang), jnp.sin(ang)
    cs = jnp.tile(jnp.concatenate([cos, cos], axis=-1), (1, 2))
    sn = jnp.tile(jnp.concatenate([-sin, sin], axis=-1), (1, 2))
    return cs, sn


def _widen_w_in(w):
    q, u = w[0:512], w[768:1792]
    parts = [q]
    for base in (512, 576, 640, 704):
        parts += [w[base:base + 64], w[base:base + 64]]
    return jnp.concatenate(parts + [u], axis=0)


def _fold_w_in(d):
    parts = [d[0:512]]
    for base in (512, 640, 768, 896):
        parts.append(d[base:base + 64] + d[base + 64:base + 128])
    return jnp.concatenate(parts + [d[1024:2048]], axis=0)


def add_halves(g5s, r1s, c_idx):
    n = len(g5s)

    def body(c_ref, *refs):
        for a in range(n):
            refs[2 * n + a][...] = (refs[a][...] + refs[n + a][...]).astype(CDT)

    def g_spec(g):
        return pl.BlockSpec((2, None) + g.shape[2:], lambda s, cr: (s, cr[0], 0, 0))

    def r_spec(g):
        return pl.BlockSpec((2,) + g.shape[2:], lambda s, cr: (s, 0, 0))

    return list(pl.pallas_call(
        body, name="add_halves",
        grid_spec=pltpu.PrefetchScalarGridSpec(
            num_scalar_prefetch=1, grid=(N_CHIPS // 2,),
            in_specs=[g_spec(g) for g in g5s] + [r_spec(g) for g in g5s],
            out_specs=[r_spec(g) for g in g5s]),
        out_shape=[jax.ShapeDtypeStruct((N_CHIPS,) + g.shape[2:], CDT) for g in g5s],
        compiler_params=_params(1),
    )(c_idx, *g5s, *r1s))


def sum_partials(parts, recv3s, j_idx):
    n = len(parts)

    def body(j_ref, *refs):
        for a in range(n):
            p_ref, r_ref = refs[a], refs[n + a]
            refs[2 * n + a][...] = ((p_ref[...].astype(F32) + r_ref[0].astype(F32))
                                    + r_ref[1].astype(F32)) + r_ref[2].astype(F32)

    def half(p):
        return p.shape[1] // 2

    return list(pl.pallas_call(
        body, name="sum_partials",
        grid_spec=pltpu.PrefetchScalarGridSpec(
            num_scalar_prefetch=1, grid=(2,),
            in_specs=[pl.BlockSpec((None, half(p), p.shape[2]), lambda i, jr: (jr[0], i, 0)) for p in parts]
            + [pl.BlockSpec((3, half(p), p.shape[2]), lambda i, jr: (0, i, 0)) for p in parts],
            out_specs=[pl.BlockSpec((half(p), p.shape[2]), lambda i, jr: (i, 0)) for p in parts]),
        out_shape=[jax.ShapeDtypeStruct(p.shape[1:], F32) for p in parts],
        compiler_params=_params(1),
    )(j_idx, *parts, *recv3s))


class _Chain:
    STAGES = ("swap", "xchg", "share")

    def __init__(self, grads, c_idx, j_idx):
        self.c_idx, self.j_idx = c_idx, j_idx
        self.g5 = [g.reshape(N_CHIPS, 2, g.shape[0] // (2 * N_CHIPS), g.shape[1]) for g in grads]
        self.stage_no = 0

    @property
    def done(self):
        return self.stage_no == len(self.STAGES)

    def next_stage(self):
        name = self.STAGES[self.stage_no]

        def callback(res):
            getattr(self, "after_" + name)(res)
            self.stage_no += 1

        return getattr(self, name)(), callback

    def swap(self):
        return swap_op(self.g5)

    def after_swap(self, recv):
        self.parts = add_halves(self.g5, recv, self.c_idx)

    def xchg(self):
        return exchange_op(self.parts)

    def after_xchg(self, recv):
        self.totals = sum_partials(self.parts, recv, self.j_idx)

    def share(self):
        return share_op(self.totals)

    def after_share(self, recv):
        both = [_own_slab(h, t, self.c_idx[0]) for h, t in zip(recv, self.totals)]
        self.final = [h.reshape(2 * h.shape[1], h.shape[2]) for h in both]


def all_reduce_small(vec):
    R = vec.shape[0]

    def body(v_ref, o_ref, buf, send, recv):
        x, y, c = _coords()
        me = 4 * x + 2 * y + c
        buf[me] = v_ref[...]
        cps = []
        for m in range(1, N_DEV):
            dx, dy, dc = (m >> 2) & 1, (m >> 1) & 1, m & 1
            cp = pltpu.make_async_remote_copy(v_ref, buf.at[me], send.at[m - 1], recv.at[m - 1],
                                              device_id=((x + dx) % 2, (y + dy) % 2, (c + dc) % 2),
                                              device_id_type=MESH)
            cp.start()
            cps.append(cp)
        for cp in cps:
            cp.wait()
        acc = buf[0]
        for d in range(1, N_DEV):
            acc = acc + buf[d]
        o_ref[...] = acc

    return pl.pallas_call(
        body, name="all_reduce_small",
        in_specs=[pl.BlockSpec(memory_space=pltpu.VMEM)], out_specs=pl.BlockSpec(memory_space=pltpu.VMEM),
        out_shape=jax.ShapeDtypeStruct(vec.shape, F32),
        scratch_shapes=[pltpu.VMEM((N_DEV, R, 128), F32), pltpu.SemaphoreType.DMA((N_DEV - 1,)),
                        pltpu.SemaphoreType.DMA((N_DEV - 1,))],
    )(vec)


def adamw(w, g, m, v, *, tm=512):
    R, C = w.shape
    tm = max(t for t in range(8, min(tm, R) + 1, 8) if R % t == 0)
    c1 =1.0 - ADAM_B1 ** ADAM_STEP
    c2 = 1.0 - ADAM_B2 ** ADAM_STEP

    def body(w_ref, g_ref, m_ref, v_ref, d_ref, nm_ref, nv_ref):
        gg = g_ref[...]
        nm = ADAM_B1 * m_ref[...] + (1.0 - ADAM_B1) * gg
        nv = ADAM_B2 * v_ref[...] + (1.0 - ADAM_B2) * (gg * gg)
        nm_ref[...] = nm
        nv_ref[...] = nv
        d_ref[...] = -ADAM_LR * ((nm / c1) / (jnp.sqrt(nv / c2) + ADAM_EPS) + ADAM_WD * w_ref[...])

    blk = pl.BlockSpec((tm, C), lambda i: (i, 0))
    return pl.pallas_call(
        body, name="adamw",
        grid=(pl.cdiv(R, tm),),
        in_specs=[blk] * 4, out_specs=[blk] * 3,
        out_shape=[jax.ShapeDtypeStruct((R, C), F32)] * 3,
        compiler_params=_params(1),
    )(w, g, m, v)


def adamw_layers(w, m, v, g_layers, *, tm=352):
    L, R, C = w.shape
    tm = max(t for t in range(8, min(tm, R) + 1, 8) if R % t == 0)
    c1 = 1.0 - ADAM_B1 ** ADAM_STEP
    c2 = 1.0 - ADAM_B2 ** ADAM_STEP

    def body(w_ref, m_ref, v_ref, *rest):
        g_refs, (go_ref, d_ref, nm_ref, nv_ref) = rest[:L], rest[L:]
        layer = pl.program_id(0)
        gg = g_refs[0][...]
        for l in range(1, L):
            gg = jnp.where(layer == l, g_refs[l][...], gg)
        nm = ADAM_B1 * m_ref[...] + (1.0 - ADAM_B1) * gg
        nv = ADAM_B2 * v_ref[...] + (1.0 - ADAM_B2) * (gg * gg)
        go_ref[...] = gg
        nm_ref[...] = nm
        nv_ref[...] = nv
        d_ref[...] = -ADAM_LR * ((nm / c1) / (jnp.sqrt(nv / c2) + ADAM_EPS) + ADAM_WD * w_ref[...])

    blk = pl.BlockSpec((None, tm, C), lambda l, i: (l, i, 0))
    gblk = pl.BlockSpec((tm, C), lambda l, i: (i, 0))
    return pl.pallas_call(
        body, name="adamw_layers",
        grid=(L, R // tm),
        in_specs=[blk] * 3 + [gblk] * L, out_specs=[blk] * 4,
        out_shape=[jax.ShapeDtypeStruct((L, R, C), F32)] * 4,
        compiler_params=_params(2),
    )(w, m, v, *g_layers)


_SMALL = (("n1", (2, D_MODEL)), ("nm", (2, D_MODEL)), ("n2", (2, D_MODEL)), ("nf", (D_MODEL,)),
          ("cb", (2, CONV_C)), ("lg", (2, CONV_C)), ("lb", (2, CONV_C)), ("sinks", (2, N_HEADS)),
          ("cw", (2, CONV_K, CONV_C)))


def _pack(parts, rows):
    flat = jnp.concatenate([p.reshape(-1).astype(F32) for p in parts])
    return jnp.pad(flat, (0, rows * 128 - flat.shape[0])).reshape(rows, 128)


def _unpack(block, shapes):
    flat = block.reshape(-1)
    out, o = [], 0
    for shp in shapes:
        n = 1
        for s in shp:
            n *= s
        out.append(flat[o:o + n].reshape(shp))
        o += n
    return out


def kernel(x, positions, ffn1_norm, ffn1_w_gate, ffn1_w_up, ffn1_w_down, mix_norm, w_in, conv_w, conv_b, conv_ln_g, conv_ln_b, attn_sinks, w_out, ffn2_norm, ffn2_w_gate, ffn2_w_up, ffn2_w_down, final_norm, loss_target, m_ffn1_norm, m_ffn1_w_gate, m_ffn1_w_up, m_ffn1_w_down, m_mix_norm, m_w_in, m_conv_w, m_conv_b, m_conv_ln_g, m_conv_ln_b, m_attn_sinks, m_w_out, m_ffn2_norm, m_ffn2_w_gate, m_ffn2_w_up, m_ffn2_w_down, m_final_norm, v_ffn1_norm, v_ffn1_w_gate, v_ffn1_w_up, v_ffn1_w_down, v_mix_norm, v_w_in, v_conv_w, v_conv_b, v_conv_ln_g, v_conv_ln_b, v_attn_sinks, v_w_out, v_ffn2_norm, v_ffn2_w_gate, v_ffn2_w_up, v_ffn2_w_down, v_final_norm):
    cx, cy, cc = _coords()
    chip = 2 * cx + cy
    c_idx = jnp.reshape(cc, (1,)).astype(jnp.int32)
    j_idx = jnp.reshape(chip, (1,)).astype(jnp.int32)
    L = ffn1_norm.shape[0]
    tr = lambda a: jnp.swapaxes(a, 1, 2)

    sh = dict(f1g=tr(ffn1_w_gate), f1u=tr(ffn1_w_up), f1d=ffn1_w_down, f2g=tr(ffn2_w_gate),
              f2u=tr(ffn2_w_up), f2d=ffn2_w_down, win=tr(w_in), wout=w_out)
    sh = {k: [v[l].astype(CDT) for l in range(L)] for k, v in sh.items()}
    W = {}

    def gather_op(keys):
        if keys == ["cw"]:
            return ag_op([conv_w])
        return ag2_op([sh[k[0]][k[1]] for k in keys])

    def take(keys, res):
        for k, a in zip(keys, res):
            if k == "cw":
                W[k] = _own_slab(a, conv_w, chip)
            else:
                mine = sh[k[0]][k[1]]
                W[k] = _own_slab(a, mine.reshape(a.shape[1:]), chip).reshape(N_CHIPS * mine.shape[0], mine.shape[1])

    def with_ag(fn, keys, *args):
        if not keys:
            return fn(*args)
        main, res = fn(*args, comm=gather_op(keys))
        take(keys, res)
        return main

    ag_hosts = {("ffn1", 0): [("win", 0), ("f2g", 0), ("f1d", 1)],
                ("inproj", 0): ["cw"], ("attn", 0): [("wout", 0), ("f2u", 0)], ("conv", 0): [("f2d", 0)],
                ("ffn2", 0): [("f1g", 1), ("f1u", 1)],
                ("ffn1", 1): [("win", 1), ("f2g", 1), ("f2u", 1)],
                ("inproj", 1): [("wout", 1)], ("attn", 1): [("f2d", 1)]}
    first = [("f1g", 0), ("f1u", 0), ("f1d", 0)]
    take(first, first_gather([sh[k_][l] for k_, l in first]))

    cs, sn = _rope_tables(positions)
    saved = []
    h = x[0]
    for l in range(L):
        sink = attn_sinks[l].reshape(2, 4)
        sink_row = jnp.repeat(sink, BLOCK, axis=1).reshape(2, 1, 4 * BLOCK)
        x0 = h
        x1, g1, u1 = with_ag(ffn_fwd, ag_hosts.get(("ffn1", l)), x0, ffn1_norm[l][None],
                             W[("f1g", l)], W[("f1u", l)], W[("f1d", l)])
        wext = _widen_w_in(W[("win", l)])
        q, k, v, u = with_ag(inproj_fwd, ag_hosts.get(("inproj", l)), x1, mix_norm[l][None], wext, cs, sn)
        ao = with_ag(attn_fwd, ag_hosts.get(("attn", l)), q, k, v, sink_row)
        cwl = jnp.transpose(W["cw"][:, l], (1, 0, 2)).reshape(CONV_K, CONV_C)
        co, yc = with_ag(conv_fwd, ag_hosts.get(("conv", l)), u, cwl, conv_b[l][None], conv_ln_g[l][None],
                         conv_ln_b[l][None])
        x3, g2, u2, x2 = with_ag(functools.partial(ffn_fwd, mix=(ao, co, W[("wout", l)])),
                                 ag_hosts.get(("ffn2", l)), x1, ffn2_norm[l][None],
                                 W[("f2g", l)], W[("f2u", l)], W[("f2d", l)])
        saved.append((x0, x1, x2, g1, u1, g2, u2, q, k, v, u, ao, co, yc, sink, wext, cwl))
        h = x3

    loss, dx, dnf = loss_head(h, final_norm[None], loss_target[0])

    active = []

    def advance(run):
        stages = [ch.next_stage() for ch in active]
        ops = [op for op, _ in stages]
        main, res = run(_merge(*ops) if ops else None)
        for (_, cb), r in zip(stages, _split(res, *ops)):
            cb(r)
        active[:] = [ch for ch in active if not ch.done]
        return main

    def hosted(fn, *args):
        def run(comm):
            if comm is None:
                return fn(*args), []
            return fn(*args, comm=comm)
        return advance(run)

    def chain(key, names_, grads):
        chains[key] = _Chain(grads, c_idx, j_idx)
        active.append(chains[key])
        for i_, nme_ in enumerate(names_):
            where[(nme_, key[1])] = (key, i_)

    small = {k_: [None] * L for k_ in ("n1", "nm", "n2", "cw", "cb", "lg", "lb", "sinks")}
    chains, where = {}, {}
    for l in reversed(range(L)):
        x0, x1, x2, g1, u1, g2, u2, q, k, v, u, ao, co, yc, sink, wext, cwl = saved[l]
        sink_row = jnp.repeat(sink, BLOCK, axis=1).reshape(2, 1, 4 * BLOCK)
        dx2, dgt, dup, hh, small["n2"][l], gwd = hosted(
            ffn_bwd_dgrad, x2, ffn2_norm[l][None], dx, g2, u2, W[("f2g", l)], W[("f2u", l)], W[("f2d", l)])
        chain(("f2d", l), ["ffn2_w_down"], [gwd])
        chain(("f2gu", l), ["ffn2_w_gate", "ffn2_w_up"], hosted(ffn_wgrad_rows, hh, [dgt, dup]))
        da, dc, gwout = hosted(outproj_bwd, dx2, ao, co, W[("wout", l)])
        du, small["cw"][l], small["cb"][l], small["lg"][l], small["lb"][l] = hosted(
            conv_bwd, dc, u, yc, cwl, conv_ln_g[l][None], conv_ln_b[l][None])
        dq, dk, dv, dsink = hosted(attn_bwd, q, k, v, da, sink_row)
        small["sinks"][l] = jnp.sum(dsink.reshape(2, 4, BLOCK), axis=-1).reshape(N_HEADS)
        dx1, gwext, small["nm"][l] = inproj_bwd(x1, mix_norm[l][None], dx2, dq, dk, dv, du, wext, cs, sn)
        chain(("mx", l), ["w_out", "w_in"], [gwout, _fold_w_in(gwext)])
        dx, dgt, dup, hh, small["n1"][l], gwd = (hosted if l > 0 else lambda fn, *a: fn(*a))(
            ffn_bwd_dgrad, x0, ffn1_norm[l][None], dx1, g1, u1, W[("f1g", l)], W[("f1u", l)], W[("f1d", l)])
        chain(("f1d", l), ["ffn1_w_down"], [gwd])
        if l > 0:
            chain(("f1gu", l), ["ffn1_w_gate", "ffn1_w_up"], hosted(ffn_wgrad_rows, hh, [dgt, dup]))
        else:
            chain(("f1u", l), ["ffn1_w_up"], hosted(ffn_wgrad_rows, hh, [dup]))
            chain(("f1g", l), ["ffn1_w_gate"], hosted(ffn_wgrad_rows, hh, [dgt]))

    weights = dict(ffn1_norm=ffn1_norm, ffn1_w_gate=ffn1_w_gate, ffn1_w_up=ffn1_w_up, ffn1_w_down=ffn1_w_down,
                   mix_norm=mix_norm, w_in=w_in, conv_w=conv_w, conv_b=conv_b, conv_ln_g=conv_ln_g,
                   conv_ln_b=conv_ln_b, attn_sinks=attn_sinks, w_out=w_out, ffn2_norm=ffn2_norm,
                   ffn2_w_gate=ffn2_w_gate, ffn2_w_up=ffn2_w_up, ffn2_w_down=ffn2_w_down, final_norm=final_norm)
    moms = dict(ffn1_norm=(m_ffn1_norm, v_ffn1_norm), ffn1_w_gate=(m_ffn1_w_gate, v_ffn1_w_gate),
                ffn1_w_up=(m_ffn1_w_up, v_ffn1_w_up), ffn1_w_down=(m_ffn1_w_down, v_ffn1_w_down),
                mix_norm=(m_mix_norm, v_mix_norm), w_in=(m_w_in, v_w_in), conv_w=(m_conv_w, v_conv_w),
                conv_b=(m_conv_b, v_conv_b), conv_ln_g=(m_conv_ln_g, v_conv_ln_g),
                conv_ln_b=(m_conv_ln_b, v_conv_ln_b), attn_sinks=(m_attn_sinks, v_attn_sinks),
                w_out=(m_w_out, v_w_out), ffn2_norm=(m_ffn2_norm, v_ffn2_norm),
                ffn2_w_gate=(m_ffn2_w_gate, v_ffn2_w_gate), ffn2_w_up=(m_ffn2_w_up, v_ffn2_w_up),
                ffn2_w_down=(m_ffn2_w_down, v_ffn2_w_down), final_norm=(m_final_norm, v_final_norm))
    names = list(weights)
    big_names = dict(ffn2_w_gate=True, ffn2_w_up=True, ffn2_w_down=False, w_out=False, w_in=True,
                     ffn1_w_down=False, ffn1_w_up=True, ffn1_w_gate=True)
    grads, delta, new_m, new_v = {}, {}, {}, {}

    def big_adamw(nme):
        view = tr if big_names[nme] else (lambda a: a)
        res = adamw_layers(view(weights[nme]), view(moms[nme][0]), view(moms[nme][1]),
                           [chains[where[(nme, l)][0]].final[where[(nme, l)][1]] for l in range(L)])
        grads[nme], delta[nme], new_m[nme], new_v[nme] = [view(a) for a in res]

    while active:
        advance(lambda comm: (None, _run_comm(comm)))
    for nme in big_names:
        big_adamw(nme)

    G = {k_: jnp.stack(v_) for k_, v_ in small.items()}
    G["nf"] = dnf
    small_shapes = [shp for _, shp in _SMALL]
    n_small = 1 + sum(math.prod(s) for s in small_shapes)
    rows = -(-n_small // 1024) * 8
    packed = _pack([loss] + [G[k_] for k_, _ in _SMALL], rows)
    summed = all_reduce_small(packed)
    loss_out, *small_sum = _unpack(summed, [()] + small_shapes)
    gs = dict(zip([k_ for k_, _ in _SMALL], small_sum))
    gs["cw"] = lax.dynamic_slice_in_dim(gs["cw"], chip * (CONV_C // N_CHIPS), CONV_C // N_CHIPS, axis=2)
    grads.update(ffn1_norm=gs["n1"], mix_norm=gs["nm"], conv_w=gs["cw"], conv_b=gs["cb"], conv_ln_g=gs["lg"],
                 conv_ln_b=gs["lb"], attn_sinks=gs["sinks"], ffn2_norm=gs["n2"], final_norm=gs["nf"])

    small_names = [nme for nme in names if nme not in big_names]
    s_shapes = [weights[nme].shape for nme in small_names]
    n_tot = sum(math.prod(s) for s in s_shapes)
    srows = -(-n_tot // 1024) * 8
    d, nm_, nv_ = adamw(_pack([weights[nme] for nme in small_names], srows),
                        _pack([grads[nme] for nme in small_names], srows),
                        _pack([moms[nme][0] for nme in small_names], srows),
                        _pack([moms[nme][1] for nme in small_names], srows))
    for nme, dd, mm, vv in zip(small_names, _unpack(d, s_shapes), _unpack(nm_, s_shapes), _unpack(nv_, s_shapes)):
        delta[nme], new_m[nme], new_v[nme] = dd, mm, vv

    return (loss_out, dx[None], *[grads[nme] for nme in names], *[delta[nme] for nme in names],
            *[new_m[nme] for nme in names], *[new_v[nme] for nme in names])
```

```python
import functools
import math

import jax
import jax.numpy as jnp
from jax import lax
from jax.experimental import pallas as pl
from jax.experimental.pallas import tpu as pltpu

F32 = jnp.float32
CDT = jnp.bfloat16
D_MODEL = 1024
D_FF = 2816
N_HEADS = 8
HEAD_DIM = 64
BLOCK = 128
CONV_K = 31
CONV_C = 512
ATT_W = 512
D_EXT = 2048
EPS = 1e-5
HALO = 32
FF_CHUNK = 256
NEG = float(jnp.finfo(jnp.float32).min)
VMEM_LIMIT = 56 * 1024 * 1024

ADAM_LR = 0.001
ADAM_B1 = 0.9
ADAM_B2 = 0.999
ADAM_EPS = 1e-08
ADAM_WD = 0.01
ADAM_STEP = 10

NT = (((1,), (1,)), ((), ()))
TN = (((0,), (0,)), ((), ()))


MESH = pl.DeviceIdType.MESH
ANY = pl.BlockSpec(memory_space=pl.ANY)
N_CHIPS = 4
N_DEV = 8


def _params(n_axes):
    return pltpu.CompilerParams(dimension_semantics=("arbitrary",) * n_axes, vmem_limit_bytes=VMEM_LIMIT)


class _Comm:
    def __init__(self, name, inputs, out_shape, sems, descs, relay=None):
        self.name, self.inputs, self.out_shape, self.sems = name, list(inputs), list(out_shape), list(sems)
        self.descs, self.relay = descs, relay


def _merge(*ops):
    ops = [o for o in ops if o is not None]
    if len(ops) == 1:
        return ops[0]
    assert all(o.relay is None for o in ops)

    def descs(cins, couts, sems):
        out, i, o, s = [], 0, 0, 0
        for op in ops:
            ni, no, ns = len(op.inputs), len(op.out_shape), len(op.sems)
            out += op.descs(cins[i:i + ni], couts[o:o + no], sems[s:s + ns])
            i, o, s = i + ni, o + no, s + ns
        return out

    return _Comm("_".join(o.name for o in ops), sum((o.inputs for o in ops), []),
                 sum((o.out_shape for o in ops), []), sum((o.sems for o in ops), []), descs)


def _split(couts, *ops):
    res, o = [], 0
    for op in ops:
        res.append(couts[o:o + len(op.out_shape)])
        o += len(op.out_shape)
    return res


def _hosted(comm):
    if comm is None:
        return pl.pallas_call

    def make(body, *, name, grid, in_specs, out_specs, out_shape, compiler_params, scratch_shapes=()):
        single = not isinstance(out_shape, (list, tuple))
        o_specs = [out_specs] if single else list(out_specs)
        o_shape = [out_shape] if single else list(out_shape)
        n_in, n_out, n_sc = len(in_specs), len(o_specs), len(scratch_shapes)
        c_in, c_out = len(comm.inputs), len(comm.out_shape)

        def hosted(*refs):
            ins, cins = refs[:n_in], refs[n_in:n_in + c_in]
            o0 = n_in + c_in
            outs, couts = refs[o0:o0 + n_out], refs[o0 + n_out:o0 + n_out + c_out]
            s0 = o0 + n_out + c_out
            scr, sems = refs[s0:s0 + n_sc], refs[s0 + n_sc:]
            first = pl.program_id(0) == 0
            last = pl.program_id(0) == grid[0] - 1
            for ax in range(1, len(grid)):
                first = first & (pl.program_id(ax) == 0)
                last = last & (pl.program_id(ax) == grid[ax] - 1)

            @pl.when(first)
            def _():
                for d in comm.descs(cins, couts, sems):
                    d.start()

            if comm.relay is not None:
                assert len(grid) == 1

                @pl.when(pl.program_id(0) == (3 * grid[0]) // 4)
                def _():
                    for d in comm.descs(cins, couts, sems):
                        d.wait()
                    for d in comm.relay(cins, couts, sems):
                        d.start()

            body(*ins, *outs, *scr)

            @pl.when(last)
            def _():
                for d in (comm.relay or comm.descs)(cins, couts, sems):
                    d.wait()

        call = pl.pallas_call(
            hosted, name=f"{name}_{comm.name}", grid=grid,
            in_specs=list(in_specs) + [ANY] * c_in, out_specs=o_specs + [ANY] * c_out,
            out_shape=o_shape + comm.out_shape, scratch_shapes=list(scratch_shapes) + comm.sems,
            compiler_params=compiler_params)

        def run(*args):
            res = call(*args, *comm.inputs)
            return (res[0] if single else list(res[:n_out])), list(res[n_out:])

        return run

    return make


def _run_comm(comm):
    c_in = len(comm.inputs)

    def body(*refs):
        cins, couts, sems = refs[:c_in], refs[c_in:c_in + len(comm.out_shape)], refs[c_in + len(comm.out_shape):]
        ds = comm.descs(cins, couts, sems)
        for d in ds:
            d.start()
        for d in ds:
            d.wait()

    return list(pl.pallas_call(
        body, name=comm.name, in_specs=[ANY] * c_in, out_specs=[ANY] * len(comm.out_shape),
        out_shape=comm.out_shape, scratch_shapes=comm.sems)(*comm.inputs))


def _coords():
    return lax.axis_index("x"), lax.axis_index("y"), lax.axis_index("c")


def _other_chips(x, y):
    return [(1 - x, y), (x, 1 - y), (1 - x, 1 - y)]


def ag_op(shards):
    n = len(shards)

    def descs(cins, couts, sems):
        send, recv = sems
        x, y, c = _coords()
        j = 2 * x + y
        ds = []
        for a in range(n):
            for p, (px, py) in enumerate(_other_chips(x, y)):
                ds.append(pltpu.make_async_remote_copy(cins[a], couts[a].at[j], send.at[a, p], recv.at[a, p],
                                                       device_id=(px, py, c), device_id_type=MESH))
        return ds

    return _Comm("ag", shards, [jax.ShapeDtypeStruct((N_CHIPS,) + s.shape, s.dtype) for s in shards],
                 [pltpu.SemaphoreType.DMA((n, 3)), pltpu.SemaphoreType.DMA((n, 3))], descs)


def ag2_op(shards):
    n = len(shards)
    halves = [s.reshape(2, s.shape[0] // 2, s.shape[1]) for s in shards]

    def descs(cins, couts, sems):
        x, y, c = _coords()
        j = 2 * x + y
        return [pltpu.make_async_remote_copy(cins[a].at[c], couts[a].at[j, c], sems[0].at[a, p], sems[1].at[a, p],
                                             device_id=(px, py, c), device_id_type=MESH)
                for a in range(n) for p, (px, py) in enumerate(_other_chips(x, y))]

    def relay(cins, couts, sems):
        x, y, c = _coords()
        return [pltpu.make_async_remote_copy(couts[a].at[2 * px + py, c], couts[a].at[2 * px + py, c],
                                             sems[2].at[a, p], sems[3].at[a, p],
                                             device_id=(x, y, 1 - c), device_id_type=MESH)
                for a in range(n) for p, (px, py) in enumerate(_other_chips(x, y))]

    return _Comm("ag2", halves, [jax.ShapeDtypeStruct((N_CHIPS,) + h.shape, h.dtype) for h in halves],
                 [pltpu.SemaphoreType.DMA((n, 3))] * 4, descs, relay)


def _own_slab(gathered, mine, idx):
    return lax.dynamic_update_slice_in_dim(gathered, mine[None], idx, axis=0)


def first_gather(shards):
    n = len(shards)
    halves = [s.reshape(2, s.shape[0] // 2, s.shape[1]) for s in shards]

    def body(*refs):
        ins, outs = refs[:n], refs[n:2 * n]
        send1, recv1, send2, recv2 = refs[2 * n:]
        x, y, c = _coords()
        j = 2 * x + y
        chips = _other_chips(x, y)
        ici = [pltpu.make_async_remote_copy(ins[a].at[c], outs[a].at[j, c], send1.at[a, p], recv1.at[a, p],
                                            device_id=(px, py, c), device_id_type=MESH)
               for a in range(n) for p, (px, py) in enumerate(chips)]
        for d in ici:
            d.start()
        d2d = [pltpu.make_async_remote_copy(outs[a].at[2 * px + py, c], outs[a].at[2 * px + py, c],
                                            send2.at[a, p], recv2.at[a, p],
                                            device_id=(x, y, 1 - c), device_id_type=MESH)
               for a in range(n) for p, (px, py) in enumerate(chips)]
        for d1, d2 in zip(ici, d2d):
            d1.wait()
            d2.start()
        for d in d2d:
            d.wait()

    return list(pl.pallas_call(
        body, name="first_gather", in_specs=[ANY] * n, out_specs=[ANY] * n,
        out_shape=[jax.ShapeDtypeStruct((N_CHIPS,) + h.shape, h.dtype) for h in halves],
        scratch_shapes=[pltpu.SemaphoreType.DMA((n, 3))] * 4)(*halves))


def swap_op(grads):
    n = len(grads)

    def descs(cins, couts, sems):
        send, recv = sems
        x, y, c = _coords()
        return [pltpu.make_async_remote_copy(cins[a].at[:, 1 - c], couts[a], send.at[a], recv.at[a],
                                             device_id=(x, y, 1 - c), device_id_type=MESH) for a in range(n)]

    return _Comm("swap", grads, [jax.ShapeDtypeStruct(g.shape[:1] + g.shape[2:], g.dtype) for g in grads],
                 [pltpu.SemaphoreType.DMA((n,)), pltpu.SemaphoreType.DMA((n,))], descs)


def exchange_op(parts):
    n = len(parts)

    def descs(cins, couts, sems):
        send, recv = sems
        x, y, c = _coords()
        ds = []
        for a in range(n):
            for p, (px, py) in enumerate(_other_chips(x, y)):
                ds.append(pltpu.make_async_remote_copy(cins[a].at[2 * px + py], couts[a].at[p], send.at[a, p],
                                                       recv.at[a, p], device_id=(px, py, c), device_id_type=MESH))
        return ds

    return _Comm("xchg", parts, [jax.ShapeDtypeStruct((3,) + p.shape[1:], p.dtype) for p in parts],
                 [pltpu.SemaphoreType.DMA((n, 3)), pltpu.SemaphoreType.DMA((n, 3))], descs)


def share_op(totals):
    n = len(totals)

    def descs(cins, couts, sems):
        send, recv = sems
        x, y, c = _coords()
        return [pltpu.make_async_remote_copy(cins[a], couts[a].at[c], send.at[a], recv.at[a],
                                             device_id=(x, y, 1 - c), device_id_type=MESH) for a in range(n)]

    return _Comm("share", totals, [jax.ShapeDtypeStruct((2,) + t.shape, t.dtype) for t in totals],
                 [pltpu.SemaphoreType.DMA((n,)), pltpu.SemaphoreType.DMA((n,))], descs)


def _sigmoid(z):
    return 1.0 / (1.0 + jnp.exp(-z))


def _rms_parts(xf):
    r = lax.rsqrt(jnp.mean(xf * xf, axis=-1, keepdims=True) + EPS)
    return xf * r, r


def _rms_bwd(dh, xhat, r, g):
    dg = jnp.sum(dh * xhat, axis=0, keepdims=True)
    dxhat = dh * g
    dx = r * (dxhat - xhat * jnp.mean(dxhat * xhat, axis=-1, keepdims=True))
    return dx, dg


def _chunks(n, ck=FF_CHUNK):
    return [(c0, min(ck, n - c0)) for c0 in range(0, n, ck)]


def ffn_fwd(x, g, wgT, wuT, wd, *, mix=None, head=None, tm=512, comm=None):
    S, D = x.shape
    F = wgT.shape[0]
    n_in = (3 if mix is not None else 0) + (2 if head is not None else 0)

    def body(x_ref, g_ref, wg_ref, wu_ref, wd_ref, *refs):
        ins, outs, a_sc = list(refs[:n_in]), list(refs[n_in:-1]), refs[-1]
        o_ref, gate_ref, up_ref = outs[:3]
        xf = x_ref[...]
        if mix is not None:
            ao_ref, co_ref, wo_ref = ins[:3]
            xf = (xf + jnp.dot(ao_ref[...], wo_ref[0:ATT_W, :], preferred_element_type=F32)
                  + jnp.dot(co_ref[...], wo_ref[ATT_W:ATT_W + CONV_C, :], preferred_element_type=F32))
            outs[3][...] = xf
        xhat, _ = _rms_parts(xf)
        h = (xhat * g_ref[...]).astype(CDT)
        for c0, cw_ in _chunks(F):
            sl = slice(c0, c0 + cw_)
            gt = lax.dot_general(h, wg_ref[sl, :], NT, preferred_element_type=F32)
            ut = lax.dot_general(h, wu_ref[sl, :], NT, preferred_element_type=F32)
            gate_ref[:, sl] = gt.astype(CDT)
            up_ref[:, sl] = ut.astype(CDT)
            a_sc[:, sl] = (gt * _sigmoid(gt) * ut).astype(CDT)
        y = xf + 0.5 * jnp.dot(a_sc[...], wd_ref[...], preferred_element_type=F32)
        if head is None:
            o_ref[...] = y
        else:
            fg_ref, t_ref = ins[-2:]
            loss_ref, dfg_ref = outs[-2:]

            @pl.when(pl.program_id(0) == 0)
            def _():
                loss_ref[...] = jnp.zeros_like(loss_ref)
                dfg_ref[...] = jnp.zeros_like(dfg_ref)

            yhat, r = _rms_parts(y)
            fg = fg_ref[...]
            err = yhat * fg - t_ref[...]
            loss_ref[...] += 0.5 * jnp.sum(jnp.mean(err * err, axis=-1, keepdims=True), axis=0, keepdims=True)
            dy, dfg = _rms_bwd(err * (1.0 / D), yhat, r, fg)
            o_ref[...] = dy
            dfg_ref[...] += dfg

    wspec = pl.BlockSpec((F, D), lambda i: (0, 0), pipeline_mode=pl.Buffered(1))
    row = pl.BlockSpec((tm, D), lambda i: (i, 0))
    act = pl.BlockSpec((tm, F), lambda i: (i, 0))
    vec = pl.BlockSpec((1, D), lambda i: (0, 0))
    in_specs = [row, vec, wspec, wspec, wspec]
    out_specs = [row, act, act]
    out_shape = [jax.ShapeDtypeStruct((S, D), F32), jax.ShapeDtypeStruct((S, F), CDT),
                 jax.ShapeDtypeStruct((S, F), CDT)]
    if mix is not None:
        in_specs += [pl.BlockSpec((tm, ATT_W), lambda i: (i, 0)), pl.BlockSpec((tm, CONV_C), lambda i: (i, 0)),
                     pl.BlockSpec((D, D), lambda i: (0, 0), pipeline_mode=pl.Buffered(1))]
        out_specs.append(row)
        out_shape.append(jax.ShapeDtypeStruct((S, D), F32))
    if head is not None:
        in_specs += [vec, row]
        out_specs += [pl.BlockSpec((1, 1), lambda i: (0, 0)), vec]
        out_shape += [jax.ShapeDtypeStruct((1, 1), F32), jax.ShapeDtypeStruct((1, D), F32)]
    name = "ffn_fwd" if mix is None else "mix_ffn_fwd"
    return _hosted(comm)(
        body, name=name if head is None else name + "_loss",
        grid=(S // tm,),
        in_specs=in_specs, out_specs=out_specs, out_shape=out_shape,
        scratch_shapes=[pltpu.VMEM((tm, F), CDT)],
        compiler_params=_params(1),
    )(x, g, wgT, wuT, wd, *(mix or ()), *(head or ()))


def ffn_bwd_dgrad(x, g, dy, gate, up, wgT, wuT, wd, *, tm=256, comm=None):
    S, D = x.shape
    F = wgT.shape[0]

    def body(x_ref, g_ref, dy_ref, gate_ref, up_ref, wg_ref, wu_ref, wd_ref,
             dx_ref, dgate_ref, dup_ref, h_ref, dg_ref, dwd_ref):
        @pl.when(pl.program_id(0) == 0)
        def _():
            dg_ref[...] = jnp.zeros_like(dg_ref)
            dwd_ref[...] = jnp.zeros_like(dwd_ref)

        dyf = dy_ref[...]
        dacc = (0.5 * dyf).astype(CDT)
        gg = g_ref[...]
        xhat, r = _rms_parts(x_ref[...])
        h_ref[...] = (xhat * gg).astype(CDT)
        for c0, cw_ in _chunks(F):
            sl = slice(c0, c0 + cw_)
            d_a = lax.dot_general(dacc, wd_ref[sl, :], NT, preferred_element_type=F32)
            gt = gate_ref[:, sl].astype(F32)
            ut = up_ref[:, sl].astype(F32)
            sg = _sigmoid(gt)
            silu = gt * sg
            dup_ref[:, sl] = (d_a * silu).astype(CDT)
            dgate_ref[:, sl] = (d_a * ut * (sg * (1.0 + gt * (1.0 - sg)))).astype(CDT)
            dwd_ref[sl, :] += lax.dot_general((silu * ut).astype(CDT), dacc, TN, preferred_element_type=F32)
        dh = (jnp.dot(dgate_ref[...], wg_ref[...], preferred_element_type=F32)
              + jnp.dot(dup_ref[...], wu_ref[...], preferred_element_type=F32))
        dx, dg = _rms_bwd(dh, xhat, r, gg)
        dx_ref[...] = dyf + dx
        dg_ref[...] += dg

    wspec = pl.BlockSpec((F, D), lambda i: (0, 0), pipeline_mode=pl.Buffered(1))
    row = pl.BlockSpec((tm, D), lambda i: (i, 0))
    act = pl.BlockSpec((tm, F), lambda i: (i, 0))
    vec = pl.BlockSpec((1, D), lambda i: (0, 0))
    return _hosted(comm)(
        body, name="ffn_bwd_dgrad",
        grid=(S // tm,),
        in_specs=[row, vec, row, act, act, wspec, wspec, wspec],
        out_specs=[row, act, act, row, vec, wspec],
        out_shape=[jax.ShapeDtypeStruct((S, D), F32),
                   jax.ShapeDtypeStruct((S, F), CDT),
                   jax.ShapeDtypeStruct((S, F), CDT),
                   jax.ShapeDtypeStruct((S, D), CDT),
                   jax.ShapeDtypeStruct((1, D), F32),
                   jax.ShapeDtypeStruct((F, D), F32)],
        compiler_params=_params(1),
    )(x, g, dy, gate, up, wgT, wuT, wd)


def ffn_wgrad_rows(h, acts, *, tk=1024, comm=None):
    S, D = h.shape
    F = acts[0].shape[1]
    n = len(acts)
    tk = min(tk, S)

    def body(h_ref, *refs):
        act_refs, dw_refs = refs[:n], refs[n:]

        @pl.when(pl.program_id(0) == 0)
        def _():
            for dw_ref in dw_refs:
                dw_ref[...] = jnp.zeros_like(dw_ref)

        hh = h_ref[...]
        for c0, cw_ in _chunks(F):
            sl = slice(c0, c0 + cw_)
            for act_ref, dw_ref in zip(act_refs, dw_refs):
                dw_ref[sl, :] += lax.dot_general(act_ref[:, sl], hh, TN, preferred_element_type=F32)

    act = pl.BlockSpec((tk, F), lambda k: (k, 0))
    out = pl.BlockSpec((F, D), lambda k: (0, 0), pipeline_mode=pl.Buffered(1))
    return _hosted(comm)(
        body, name="ffn_wgrad_rows",
        grid=(S // tk,),
        in_specs=[pl.BlockSpec((tk, D), lambda k: (k, 0))] + [act] * n,
        out_specs=[out] * n,
        out_shape=[jax.ShapeDtypeStruct((F, D), F32)] * n,
        compiler_params=_params(1),
    )(h, *acts)


def _rope_apply(t, cs, sn):
    lane = lax.broadcasted_iota(jnp.int32, t.shape, 1)
    first = (lane % HEAD_DIM) < (HEAD_DIM // 2)
    rot = jnp.where(first, pltpu.roll(t, 128 - HEAD_DIM // 2, 1), pltpu.roll(t, HEAD_DIM // 2, 1))
    return t * cs + rot * sn


def _rope_transpose(d, cs, sn):
    lane = lax.broadcasted_iota(jnp.int32, d.shape, 1)
    first = (lane % HEAD_DIM) < (HEAD_DIM // 2)
    ds = d * sn
    rot = jnp.where(first, pltpu.roll(ds, 128 - HEAD_DIM // 2, 1), pltpu.roll(ds, HEAD_DIM // 2, 1))
    return d * cs + rot


def inproj_fwd(x, g, wextT, cs, sn, *, tm=512, comm=None):
    S, D = x.shape
    scale = HEAD_DIM ** -0.5

    def body(x_ref, g_ref, w_ref, cs_ref, sn_ref, q_ref, k_ref, v_ref, u_ref):
        xhat, _ = _rms_parts(x_ref[...])
        h = (xhat * g_ref[...]).astype(CDT)
        p = lax.dot_general(h, w_ref[...], NT, preferred_element_type=F32)
        c, s = cs_ref[...], sn_ref[...]
        for b in range(4):
            q_ref[:, 128 * b:128 * (b + 1)] = (_rope_apply(p[:, 128 * b:128 * (b + 1)], c, s) * scale).astype(CDT)
        for b in range(2):
            k_ref[:, 128 * b:128 * (b + 1)] = _rope_apply(p[:, 512 + 128 * b:512 + 128 * (b + 1)], c, s).astype(CDT)
        v_ref[...] = p[:, 768:1024].astype(CDT)
        u_ref[...] = p[:, 1024:2048]

    def row(w):
        return pl.BlockSpec((tm, w), lambda i: (i, 0))

    return _hosted(comm)(
        body, name="inproj_fwd",
        grid=(S // tm,),
        in_specs=[row(D), pl.BlockSpec((1, D), lambda i: (0, 0)),
                  pl.BlockSpec((D_EXT, D), lambda i: (0, 0)), row(128), row(128)],
        out_specs=[row(512), row(256), row(256), row(1024)],
        out_shape=[jax.ShapeDtypeStruct((S, 512), CDT), jax.ShapeDtypeStruct((S, 256), CDT),
                   jax.ShapeDtypeStruct((S, 256), CDT), jax.ShapeDtypeStruct((S, 1024), F32)],
        compiler_params=_params(1),
    )(x, g, wextT, cs, sn)


def _stack_heads(p0, p1):
    lane = lax.broadcasted_iota(jnp.int32, p0.shape, 1)
    lo = lane < HEAD_DIM
    z = jnp.zeros_like(p0)
    return jnp.concatenate([jnp.where(lo, p0, z), jnp.where(lo, z, p0),
                            jnp.where(lo, p1, z), jnp.where(lo, z, p1)], axis=0)


def _unstack_heads(o):
    lane = lax.broadcasted_iota(jnp.int32, (BLOCK, 128), 1)
    lo = lane < HEAD_DIM
    return (jnp.where(lo, o[0:128], o[128:256]), jnp.where(lo, o[256:384], o[384:512]))


def _band_mask_kq(n):
    c = lax.broadcasted_iota(jnp.int32, (2 * BLOCK, 4 * BLOCK), 0)
    i = lax.broadcasted_iota(jnp.int32, (2 * BLOCK, 4 * BLOCK), 1) % BLOCK
    return (c > i) & (c <= i + BLOCK) & ((n > 0) | (c >= BLOCK))


def attn_fwd(q, k, v, sink_row, *, nb=4, comm=None):
    S = q.shape[0]
    tq = nb * BLOCK

    def body(q_ref, k_ref, v_ref, sink_ref, o_ref):
        t = pl.program_id(0)
        for b in range(nb):
            n = t * nb + b
            prev = pl.multiple_of(jnp.maximum(n - 1, 0) * BLOCK, BLOCK)
            cur = pl.multiple_of(n * BLOCK, BLOCK)
            rows = slice(b * BLOCK, (b + 1) * BLOCK)
            mask = _band_mask_kq(n)
            for gidx in range(2):
                lanes = slice(128 * gidx, 128 * (gidx + 1))
                qs = _stack_heads(q_ref[rows, 256 * gidx:256 * gidx + 128],
                                  q_ref[rows, 256 * gidx + 128:256 * gidx + 256])
                kb = jnp.concatenate([k_ref[pl.ds(prev, BLOCK), lanes], k_ref[pl.ds(cur, BLOCK), lanes]], axis=0)
                vb = jnp.concatenate([v_ref[pl.ds(prev, BLOCK), lanes], v_ref[pl.ds(cur, BLOCK), lanes]], axis=0)
                st = lax.dot_general(kb, qs, NT, preferred_element_type=F32)
                st = jnp.where(mask, st, NEG)
                sink = sink_ref[gidx]
                m = jnp.maximum(jnp.max(st, axis=0, keepdims=True), sink)
                e = jnp.exp(st - m)
                inv = 1.0 / (jnp.sum(e, axis=0, keepdims=True) + jnp.exp(sink - m))
                o = lax.dot_general((e * inv).astype(CDT), vb, TN, preferred_element_type=F32)
                o0, o1 = _unstack_heads(o)
                o_ref[rows, 256 * gidx:256 * gidx + 128] = o0.astype(CDT)
                o_ref[rows, 256 * gidx + 128:256 * gidx + 256] = o1.astype(CDT)

    return _hosted(comm)(
        body, name="attn_fwd",
        grid=(S // tq,),
        in_specs=[pl.BlockSpec((tq, 512), lambda t: (t, 0)),
                  pl.BlockSpec((S, 256), lambda t: (0, 0)),
                  pl.BlockSpec((S, 256), lambda t: (0, 0)),
                  pl.BlockSpec((2, 1, 4 * BLOCK), lambda t: (0, 0, 0))],
        out_specs=pl.BlockSpec((tq, 512), lambda t: (t, 0)),
        out_shape=jax.ShapeDtypeStruct((S, 512), CDT),
        compiler_params=_params(1),
    )(q, k, v, sink_row)


def attn_bwd(q, k, v, do, sink_row, *, nb=4, comm=None):
    S = q.shape[0]
    tq = nb * BLOCK
    scale = HEAD_DIM ** -0.5

    def body(q_ref, k_ref, v_ref, do_ref, sink_ref, dq_ref, dk_ref, dv_ref, dsink_ref):
        t = pl.program_id(0)

        @pl.when(t == 0)
        def _():
            dk_ref[...] = jnp.zeros_like(dk_ref)
            dv_ref[...] = jnp.zeros_like(dv_ref)
            dsink_ref[...] = jnp.zeros_like(dsink_ref)

        for b in range(nb):
            n = t * nb + b
            prev = pl.multiple_of(jnp.maximum(n - 1, 0) * BLOCK, BLOCK)
            cur = pl.multiple_of(n * BLOCK, BLOCK)
            rows = slice(b * BLOCK, (b + 1) * BLOCK)
            mask = _band_mask_kq(n)
            for gidx in range(2):
                lanes = slice(128 * gidx, 128 * (gidx + 1))
                qs = _stack_heads(q_ref[rows, 256 * gidx:256 * gidx + 128],
                                  q_ref[rows, 256 * gidx + 128:256 * gidx + 256])
                dos = _stack_heads(do_ref[rows, 256 * gidx:256 * gidx + 128],
                                   do_ref[rows, 256 * gidx + 128:256 * gidx + 256])
                kb = jnp.concatenate([k_ref[pl.ds(prev, BLOCK), lanes], k_ref[pl.ds(cur, BLOCK), lanes]], axis=0)
                vb = jnp.concatenate([v_ref[pl.ds(prev, BLOCK), lanes], v_ref[pl.ds(cur, BLOCK), lanes]], axis=0)
                st = lax.dot_general(kb, qs, NT, preferred_element_type=F32)
                st = jnp.where(mask, st, NEG)
                sink = sink_ref[gidx]
                m = jnp.maximum(jnp.max(st, axis=0, keepdims=True), sink)
                e = jnp.exp(st - m)
                es = jnp.exp(sink - m)
                inv = 1.0 / (jnp.sum(e, axis=0, keepdims=True) + es)
                pt = e * inv
                dpt = lax.dot_general(vb, dos, NT, preferred_element_type=F32)
                delta = jnp.sum(pt * dpt, axis=0, keepdims=True)
                dst = (pt * (dpt - delta)).astype(CDT)
                dsink_ref[gidx] += -(es * inv) * delta
                dvb = jnp.dot(pt.astype(CDT), dos, preferred_element_type=F32)
                dkb = jnp.dot(dst, qs, preferred_element_type=F32)
                dqs = lax.dot_general(dst, kb, TN, preferred_element_type=F32) * scale
                dq0, dq1 = _unstack_heads(dqs)
                dq_ref[rows, 256 * gidx:256 * gidx + 128] = dq0
                dq_ref[rows, 256 * gidx + 128:256 * gidx + 256] = dq1
                dk_ref[pl.ds(prev, BLOCK), lanes] += dkb[0:BLOCK]
                dk_ref[pl.ds(cur, BLOCK), lanes] += dkb[BLOCK:2 * BLOCK]
                dv_ref[pl.ds(prev, BLOCK), lanes] += dvb[0:BLOCK]
                dv_ref[pl.ds(cur, BLOCK), lanes] += dvb[BLOCK:2 * BLOCK]

    full = pl.BlockSpec((S, 256), lambda t: (0, 0))
    tile = pl.BlockSpec((tq, 512), lambda t: (t, 0))
    srow = pl.BlockSpec((2, 1, 4 * BLOCK), lambda t: (0, 0, 0))
    return _hosted(comm)(
        body, name="attn_bwd",
        grid=(S // tq,),
        in_specs=[tile, full, full, tile, srow],
        out_specs=[tile, full, full, srow],
        out_shape=[jax.ShapeDtypeStruct((S, 512), F32), jax.ShapeDtypeStruct((S, 256), F32),
                   jax.ShapeDtypeStruct((S, 256), F32), jax.ShapeDtypeStruct((2, 1, 4 * BLOCK), F32)],
        compiler_params=_params(1),
    )(q, k, v, do, sink_row)


def _glu(u):
    a = u[:, 0:CONV_C]
    gt = u[:, CONV_C:2 * CONV_C]
    sg = _sigmoid(gt)
    return a, sg, a * sg


CONV_CHUNK = 32


def _shifted_copies(buf, shifted, n):
    for r in range(1, 8):
        shifted[r - 1, 0:n, :] = buf[r:r + n, :]


def _shifted_rows(buf, shifted, start, rows):
    r = start % 8
    if r == 0:
        return buf[start:start + rows, :]
    return shifted[r - 1, start - r:start - r + rows, :]


def conv_fwd(u, cw, cb, lg, lb, *, tm=512, comm=None):
    S = u.shape[0]
    nh = tm // HALO

    def body(u_ref, uh_ref, cw_ref, cb_ref, lg_ref, lb_ref, o_ref, y_ref, hbuf, hsh):
        t = pl.program_id(0)
        _, _, hg = _glu(u_ref[...])
        _, _, hh = _glu(uh_ref[...])
        hbuf[0:HALO, :] = jnp.where(t > 0, hh, jnp.zeros_like(hh))
        hbuf[HALO:HALO + tm, :] = hg
        hbuf[HALO + tm:HALO + tm + 8, :] = jnp.zeros((8, CONV_C), F32)
        _shifted_copies(hbuf, hsh, HALO + tm)
        off = HALO - (CONV_K - 1)
        for c0 in range(0, tm, CONV_CHUNK):
            acc = jnp.zeros((CONV_CHUNK, CONV_C), F32) + cb_ref[...]
            for j in range(CONV_K):
                acc = acc + cw_ref[j:j + 1, :] * _shifted_rows(hbuf, hsh, c0 + off + j, CONV_CHUNK)
            y_ref[c0:c0 + CONV_CHUNK, :] = acc
        y = y_ref[...]
        yc = y - jnp.mean(y, axis=-1, keepdims=True)
        r = lax.rsqrt(jnp.mean(yc * yc, axis=-1, keepdims=True) + EPS)
        z = yc * r * lg_ref[...] + lb_ref[...]
        o_ref[...] = (z * _sigmoid(z)).astype(CDT)

    vec = pl.BlockSpec((1, CONV_C), lambda t: (0, 0))
    return _hosted(comm)(
        body, name="conv_fwd",
        grid=(S // tm,),
        in_specs=[pl.BlockSpec((tm, 2 * CONV_C), lambda t: (t, 0)),
                  pl.BlockSpec((HALO, 2 * CONV_C), lambda t: (jnp.maximum(t * nh - 1, 0), 0)),
                  pl.BlockSpec((CONV_K, CONV_C), lambda t: (0, 0)), vec, vec, vec],
        out_specs=[pl.BlockSpec((tm, CONV_C), lambda t: (t, 0)), pl.BlockSpec((tm, CONV_C), lambda t: (t, 0))],
        out_shape=[jax.ShapeDtypeStruct((S, CONV_C), CDT), jax.ShapeDtypeStruct((S, CONV_C), F32)],
        scratch_shapes=[pltpu.VMEM((HALO + tm + 8, CONV_C), F32), pltpu.VMEM((7, HALO + tm, CONV_C), F32)],
        compiler_params=_params(1),
    )(u, u, cw, cb, lg, lb)


def conv_bwd(dc, u, y, cw, lg, lb, *, tm=512, comm=None):
    S = u.shape[0]
    nh = tm // HALO
    nt = S // tm

    def ln_bwd(dcv, yv, lgv, lbv):
        yc = yv - jnp.mean(yv, axis=-1, keepdims=True)
        r = lax.rsqrt(jnp.mean(yc * yc, axis=-1, keepdims=True) + EPS)
        yhat = yc * r
        z = yhat * lgv + lbv
        sg = _sigmoid(z)
        dz = dcv * (sg * (1.0 + z * (1.0 - sg)))
        dyhat = dz * lgv
        dy = r * (dyhat - jnp.mean(dyhat, axis=-1, keepdims=True)
                  - yhat * jnp.mean(dyhat * yhat, axis=-1, keepdims=True))
        return dy, dz, yhat

    def body(dc_ref, dcn_ref, u_ref, uh_ref, y_ref, yn_ref, cw_ref, lg_ref, lb_ref,
             du_ref, dcw_ref, dcb_ref, dlg_ref, dlb_ref, hbuf, dybuf, dhg_sc, dw_sc, hsh, dysh):
        t = pl.program_id(0)

        @pl.when(t == 0)
        def _():
            dw_sc[...] = jnp.zeros_like(dw_sc)
            dcb_ref[...] = jnp.zeros_like(dcb_ref)
            dlg_ref[...] = jnp.zeros_like(dlg_ref)
            dlb_ref[...] = jnp.zeros_like(dlb_ref)

        lgv, lbv = lg_ref[...], lb_ref[...]
        dy, dz, yhat = ln_bwd(dc_ref[...].astype(F32), y_ref[...], lgv, lbv)
        dyn, _, _ = ln_bwd(dcn_ref[...].astype(F32), yn_ref[...], lgv, lbv)
        dlb_ref[...] += jnp.sum(dz, axis=0, keepdims=True)
        dlg_ref[...] += jnp.sum(dz * yhat, axis=0, keepdims=True)
        dcb_ref[...] += jnp.sum(dy, axis=0, keepdims=True)
        dybuf[0:tm, :] = dy
        dybuf[tm:tm + HALO, :] = jnp.where(t < nt - 1, dyn, jnp.zeros_like(dyn))
        dybuf[tm + HALO:tm + HALO + 8, :] = jnp.zeros((8, CONV_C), F32)
        _shifted_copies(dybuf, dysh, tm + HALO)

        a, sg, hg = _glu(u_ref[...])
        _, _, hh = _glu(uh_ref[...])
        hbuf[0:HALO, :] = jnp.where(t > 0, hh, jnp.zeros_like(hh))
        hbuf[HALO:HALO + tm, :] = hg
        hbuf[HALO + tm:HALO + tm + 8, :] = jnp.zeros((8, CONV_C), F32)
        _shifted_copies(hbuf, hsh, HALO + tm)

        off = HALO - (CONV_K - 1)
        for c0 in range(0, tm, CONV_CHUNK):
            acc = jnp.zeros((CONV_CHUNK, CONV_C), F32)
            dyc = dybuf[c0:c0 + CONV_CHUNK, :]
            for j in range(CONV_K):
                acc = acc + cw_ref[j:j + 1, :] * _shifted_rows(dybuf, dysh, c0 + (CONV_K - 1) - j, CONV_CHUNK)
                prod = dyc * _shifted_rows(hbuf, hsh, c0 + off + j, CONV_CHUNK)
                dw_sc[j] += jnp.sum(prod.reshape(CONV_CHUNK // 8, 8, CONV_C), axis=0)
            dhg_sc[c0:c0 + CONV_CHUNK, :] = acc

        dhg = dhg_sc[...]
        du_ref[:, 0:CONV_C] = dhg * sg
        du_ref[:, CONV_C:2 * CONV_C] = dhg * a * sg * (1.0 - sg)

        @pl.when(t == nt - 1)
        def _():
            dcw_ref[...] = jnp.sum(dw_sc[...], axis=1)

    vec = pl.BlockSpec((1, CONV_C), lambda t: (0, 0))
    tile = pl.BlockSpec((tm, CONV_C), lambda t: (t, 0))
    nxt = pl.BlockSpec((HALO, CONV_C), lambda t: (jnp.minimum((t + 1) * nh, S // HALO - 1), 0))
    return _hosted(comm)(
        body, name="conv_bwd",
        grid=(nt,),
        in_specs=[tile, nxt,
                  pl.BlockSpec((tm, 2 * CONV_C), lambda t: (t, 0)),
                  pl.BlockSpec((HALO, 2 * CONV_C), lambda t: (jnp.maximum(t * nh - 1, 0), 0)),
                  tile, nxt,
                  pl.BlockSpec((CONV_K, CONV_C), lambda t: (0, 0)), vec, vec],
        out_specs=[pl.BlockSpec((tm, 2 * CONV_C), lambda t: (t, 0)),
                   pl.BlockSpec((CONV_K, CONV_C), lambda t: (0, 0)), vec, vec, vec],
        out_shape=[jax.ShapeDtypeStruct((S, 2 * CONV_C), F32), jax.ShapeDtypeStruct((CONV_K, CONV_C), F32),
                   jax.ShapeDtypeStruct((1, CONV_C), F32), jax.ShapeDtypeStruct((1, CONV_C), F32),
                   jax.ShapeDtypeStruct((1, CONV_C), F32)],
        scratch_shapes=[pltpu.VMEM((HALO + tm + 8, CONV_C), F32), pltpu.VMEM((tm + HALO + 8, CONV_C), F32),
                        pltpu.VMEM((tm, CONV_C), F32), pltpu.VMEM((CONV_K, 8, CONV_C), F32),
                        pltpu.VMEM((7, HALO + tm, CONV_C), F32), pltpu.VMEM((7, tm + HALO, CONV_C), F32)],
        compiler_params=_params(1),
    )(dc, dc, u, u, y, y, cw, lg, lb)


def outproj_bwd(dx, ao, co, wout, *, tm=512, comm=None):
    S, D = dx.shape

    def body(dx_ref, a_ref, c_ref, w_ref, da_ref, dc_ref, dw_ref):
        @pl.when(pl.program_id(0) == 0)
        def _():
            dw_ref[...] = jnp.zeros_like(dw_ref)

        dxb = dx_ref[...].astype(CDT)
        da_ref[...] = lax.dot_general(dxb, w_ref[0:ATT_W, :], NT, preferred_element_type=F32).astype(CDT)
        dc_ref[...] = lax.dot_general(dxb, w_ref[ATT_W:ATT_W + CONV_C, :], NT, preferred_element_type=F32)
        dw_ref[0:ATT_W, :] += lax.dot_general(a_ref[...], dxb, TN, preferred_element_type=F32)
        dw_ref[ATT_W:ATT_W + CONV_C, :] += lax.dot_general(c_ref[...], dxb, TN, preferred_element_type=F32)

    return _hosted(comm)(
        body, name="outproj_bwd",
        grid=(S // tm,),
        in_specs=[pl.BlockSpec((tm, D), lambda i: (i, 0)), pl.BlockSpec((tm, ATT_W), lambda i: (i, 0)),
                  pl.BlockSpec((tm, CONV_C), lambda i: (i, 0)), pl.BlockSpec((D, D), lambda i: (0, 0))],
        out_specs=[pl.BlockSpec((tm, ATT_W), lambda i: (i, 0)), pl.BlockSpec((tm, CONV_C), lambda i: (i, 0)),
                   pl.BlockSpec((D, D), lambda i: (0, 0))],
        out_shape=[jax.ShapeDtypeStruct((S, ATT_W), CDT), jax.ShapeDtypeStruct((S, CONV_C), F32),
                   jax.ShapeDtypeStruct((D, D), F32)],
        compiler_params=_params(1),
    )(dx, ao, co, wout)


def inproj_bwd(x, g, dres, dq, dk, dv, du, wextT, cs, sn, *, tm=512):
    S, D = x.shape

    def body(x_ref, g_ref, dres_ref, dq_ref, dk_ref, dv_ref, du_ref, w_ref, cs_ref, sn_ref,
             dx_ref, dw_ref, dg_ref, dp_sc):
        @pl.when(pl.program_id(0) == 0)
        def _():
            dw_ref[...] = jnp.zeros_like(dw_ref)
            dg_ref[...] = jnp.zeros_like(dg_ref)

        c, s = cs_ref[...], sn_ref[...]
        for b in range(4):
            dp_sc[:, 128 * b:128 * (b + 1)] = _rope_transpose(dq_ref[:, 128 * b:128 * (b + 1)], c, s).astype(CDT)
        for b in range(2):
            dp_sc[:, 512 + 128 * b:512 + 128 * (b + 1)] = _rope_transpose(
                dk_ref[:, 128 * b:128 * (b + 1)], c, s).astype(CDT)
        dp_sc[:, 768:1024] = dv_ref[...].astype(CDT)
        dp_sc[:, 1024:2048] = du_ref[...].astype(CDT)
        dp = dp_sc[...]
        xhat, r = _rms_parts(x_ref[...])
        gg = g_ref[...]
        h = (xhat * gg).astype(CDT)
        dh = jnp.dot(dp, w_ref[...], preferred_element_type=F32)
        dw_ref[...] += lax.dot_general(dp, h, TN, preferred_element_type=F32)
        dx, dg = _rms_bwd(dh, xhat, r, gg)
        dx_ref[...] = dres_ref[...] + dx
        dg_ref[...] += dg

    def row(w):
        return pl.BlockSpec((tm, w), lambda i: (i, 0))

    return pl.pallas_call(
        body, name="inproj_bwd",
        grid=(S // tm,),
        in_specs=[row(D), pl.BlockSpec((1, D), lambda i: (0, 0)), row(D), row(512), row(256), row(256),
                  row(1024), pl.BlockSpec((D_EXT, D), lambda i: (0, 0), pipeline_mode=pl.Buffered(1)),
                  row(128), row(128)],
        out_specs=[row(D), pl.BlockSpec((D_EXT, D), lambda i: (0, 0), pipeline_mode=pl.Buffered(1)),
                   pl.BlockSpec((1, D), lambda i: (0, 0))],
        out_shape=[jax.ShapeDtypeStruct((S, D), F32), jax.ShapeDtypeStruct((D_EXT, D), F32),
                   jax.ShapeDtypeStruct((1, D), F32)],
        scratch_shapes=[pltpu.VMEM((tm, D_EXT), CDT)],
        compiler_params=_params(1),
    )(x, g, dres, dq, dk, dv, du, wextT, cs, sn)


def _rope_tables(positions):
    inv_freq = 1.0 / (10000.0 ** (jnp.arange(0, HEAD_DIM, 2, dtype=F32) / HEAD_DIM))
    ang = positions.astype(F32).reshape(-1, 1) * inv_freq
    cos, sin = jnp.cos(ang), jnp.sin(ang)
    cs = jnp.tile(jnp.concatenate([cos, cos], axis=-1), (1, 2))
    sn = jnp.tile(jnp.concatenate([-sin, sin], axis=-1), (1, 2))
    return cs, sn


def _widen_w_in(w):
    q, u = w[0:512], w[768:1792]
    parts = [q]
    for base in (512, 576, 640, 704):
        parts += [w[base:base + 64], w[base:base + 64]]
    return jnp.concatenate(parts + [u], axis=0)


def _fold_w_in(d):
    parts = [d[0:512]]
    for base in (512, 640, 768, 896):
        parts.append(d[base:base + 64] + d[base + 64:base + 128])
    return jnp.concatenate(parts + [d[1024:2048]], axis=0)


def add_halves(g5s, r1s, c_idx):
    n = len(g5s)

    def body(c_ref, *refs):
        for a in range(n):
            refs[2 * n + a][...] = (refs[a][...] + refs[n + a][...]).astype(CDT)

    def g_spec(g):
        return pl.BlockSpec((2, None) + g.shape[2:], lambda s, cr: (s, cr[0], 0, 0))

    def r_spec(g):
        return pl.BlockSpec((2,) + g.shape[2:], lambda s, cr: (s, 0, 0))

    return list(pl.pallas_call(
        body, name="add_halves",
        grid_spec=pltpu.PrefetchScalarGridSpec(
            num_scalar_prefetch=1, grid=(N_CHIPS // 2,),
            in_specs=[g_spec(g) for g in g5s] + [r_spec(g) for g in g5s],
            out_specs=[r_spec(g) for g in g5s]),
        out_shape=[jax.ShapeDtypeStruct((N_CHIPS,) + g.shape[2:], CDT) for g in g5s],
        compiler_params=_params(1),
    )(c_idx, *g5s, *r1s))


def sum_partials(parts, recv3s, j_idx):
    n = len(parts)

    def body(j_ref, *refs):
        for a in range(n):
            p_ref, r_ref = refs[a], refs[n + a]
            refs[2 * n + a][...] = ((p_ref[...].astype(F32) + r_ref[0].astype(F32))
                                    + r_ref[1].astype(F32)) + r_ref[2].astype(F32)

    def half(p):
        return p.shape[1] // 2

    return list(pl.pallas_call(
        body, name="sum_partials",
        grid_spec=pltpu.PrefetchScalarGridSpec(
            num_scalar_prefetch=1, grid=(2,),
            in_specs=[pl.BlockSpec((None, half(p), p.shape[2]), lambda i, jr: (jr[0], i, 0)) for p in parts]
            + [pl.BlockSpec((3, half(p), p.shape[2]), lambda i, jr: (0, i, 0)) for p in parts],
            out_specs=[pl.BlockSpec((half(p), p.shape[2]), lambda i, jr: (i, 0)) for p in parts]),
        out_shape=[jax.ShapeDtypeStruct(p.shape[1:], F32) for p in parts],
        compiler_params=_params(1),
    )(j_idx, *parts, *recv3s))


class _Chain:
    STAGES = ("swap", "xchg", "share")

    def __init__(self, grads, c_idx, j_idx):
        self.c_idx, self.j_idx = c_idx, j_idx
        self.g5 = [g.reshape(N_CHIPS, 2, g.shape[0] // (2 * N_CHIPS), g.shape[1]) for g in grads]
        self.stage_no = 0

    @property
    def done(self):
        return self.stage_no == len(self.STAGES)

    def next_stage(self):
        name = self.STAGES[self.stage_no]

        def callback(res):
            getattr(self, "after_" + name)(res)
            self.stage_no += 1

        return getattr(self, name)(), callback

    def swap(self):
        return swap_op(self.g5)

    def after_swap(self, recv):
        self.parts = add_halves(self.g5, recv, self.c_idx)

    def xchg(self):
        return exchange_op(self.parts)

    def after_xchg(self, recv):
        self.totals = sum_partials(self.parts, recv, self.j_idx)

    def share(self):
        return share_op(self.totals)

    def after_share(self, recv):
        both = [_own_slab(h, t, self.c_idx[0]) for h, t in zip(recv, self.totals)]
        self.final = [h.reshape(2 * h.shape[1], h.shape[2]) for h in both]


def all_reduce_small(vec):
    R = vec.shape[0]

    def body(v_ref, o_ref, buf, send, recv):
        x, y, c = _coords()
        me = 4 * x + 2 * y + c
        buf[me] = v_ref[...]
        cps = []
        for m in range(1, N_DEV):
            dx, dy, dc = (m >> 2) & 1, (m >> 1) & 1, m & 1
            cp = pltpu.make_async_remote_copy(v_ref, buf.at[me], send.at[m - 1], recv.at[m - 1],
                                              device_id=((x + dx) % 2, (y + dy) % 2, (c + dc) % 2),
                                              device_id_type=MESH)
            cp.start()
            cps.append(cp)
        for cp in cps:
            cp.wait()
        acc = buf[0]
        for d in range(1, N_DEV):
            acc = acc + buf[d]
        o_ref[...] = acc

    return pl.pallas_call(
        body, name="all_reduce_small",
        in_specs=[pl.BlockSpec(memory_space=pltpu.VMEM)], out_specs=pl.BlockSpec(memory_space=pltpu.VMEM),
        out_shape=jax.ShapeDtypeStruct(vec.shape, F32),
        scratch_shapes=[pltpu.VMEM((N_DEV, R, 128), F32), pltpu.SemaphoreType.DMA((N_DEV - 1,)),
                        pltpu.SemaphoreType.DMA((N_DEV - 1,))],
    )(vec)


def adamw(w, g, m, v, *, tm=512):
    R, C = w.shape
    tm = max(t for t in range(8, min(tm, R) + 1, 8) if R % t == 0)
    c1 =1.0 - ADAM_B1 ** ADAM_STEP
    c2 = 1.0 - ADAM_B2 ** ADAM_STEP

    def body(w_ref, g_ref, m_ref, v_ref, d_ref, nm_ref, nv_ref):
        gg = g_ref[...]
        nm = ADAM_B1 * m_ref[...] + (1.0 - ADAM_B1) * gg
        nv = ADAM_B2 * v_ref[...] + (1.0 - ADAM_B2) * (gg * gg)
        nm_ref[...] = nm
        nv_ref[...] = nv
        d_ref[...] = -ADAM_LR * ((nm / c1) / (jnp.sqrt(nv / c2) + ADAM_EPS) + ADAM_WD * w_ref[...])

    blk = pl.BlockSpec((tm, C), lambda i: (i, 0))
    return pl.pallas_call(
        body, name="adamw",
        grid=(pl.cdiv(R, tm),),
        in_specs=[blk] * 4, out_specs=[blk] * 3,
        out_shape=[jax.ShapeDtypeStruct((R, C), F32)] * 3,
        compiler_params=_params(1),
    )(w, g, m, v)


def adamw_layers(w, m, v, g_layers, *, tm=352):
    L, R, C = w.shape
    tm = max(t for t in range(8, min(tm, R) + 1, 8) if R % t == 0)
    c1 = 1.0 - ADAM_B1 ** ADAM_STEP
    c2 = 1.0 - ADAM_B2 ** ADAM_STEP

    def body(w_ref, m_ref, v_ref, *rest):
        g_refs, (go_ref, d_ref, nm_ref, nv_ref) = rest[:L], rest[L:]
        layer = pl.program_id(0)
        gg = g_refs[0][...]
        for l in range(1, L):
            gg = jnp.where(layer == l, g_refs[l][...], gg)
        nm = ADAM_B1 * m_ref[...] + (1.0 - ADAM_B1) * gg
        nv = ADAM_B2 * v_ref[...] + (1.0 - ADAM_B2) * (gg * gg)
        go_ref[...] = gg
        nm_ref[...] = nm
        nv_ref[...] = nv
        d_ref[...] = -ADAM_LR * ((nm / c1) / (jnp.sqrt(nv / c2) + ADAM_EPS) + ADAM_WD * w_ref[...])

    blk = pl.BlockSpec((None, tm, C), lambda l, i: (l, i, 0))
    gblk = pl.BlockSpec((tm, C), lambda l, i: (i, 0))
    return pl.pallas_call(
        body, name="adamw_layers",
        grid=(L, R // tm),
        in_specs=[blk] * 3 + [gblk] * L, out_specs=[blk] * 4,
        out_shape=[jax.ShapeDtypeStruct((L, R, C), F32)] * 4,
        compiler_params=_params(2),
    )(w, m, v, *g_layers)


_SMALL = (("n1", (2, D_MODEL)), ("nm", (2, D_MODEL)), ("n2", (2, D_MODEL)), ("nf", (D_MODEL,)),
          ("cb", (2, CONV_C)), ("lg", (2, CONV_C)), ("lb", (2, CONV_C)), ("sinks", (2, N_HEADS)),
          ("cw", (2, CONV_K, CONV_C)))


def _pack(parts, rows):
    flat = jnp.concatenate([p.reshape(-1).astype(F32) for p in parts])
    return jnp.pad(flat, (0, rows * 128 - flat.shape[0])).reshape(rows, 128)


def _unpack(block, shapes):
    flat = block.reshape(-1)
    out, o = [], 0
    for shp in shapes:
        n = 1
        for s in shp:
            n *= s
        out.append(flat[o:o + n].reshape(shp))
        o += n
    return out


def kernel(x, positions, ffn1_norm, ffn1_w_gate, ffn1_w_up, ffn1_w_down, mix_norm, w_in, conv_w, conv_b, conv_ln_g, conv_ln_b, attn_sinks, w_out, ffn2_norm, ffn2_w_gate, ffn2_w_up, ffn2_w_down, final_norm, loss_target, m_ffn1_norm, m_ffn1_w_gate, m_ffn1_w_up, m_ffn1_w_down, m_mix_norm, m_w_in, m_conv_w, m_conv_b, m_conv_ln_g, m_conv_ln_b, m_attn_sinks, m_w_out, m_ffn2_norm, m_ffn2_w_gate, m_ffn2_w_up, m_ffn2_w_down, m_final_norm, v_ffn1_norm, v_ffn1_w_gate, v_ffn1_w_up, v_ffn1_w_down, v_mix_norm, v_w_in, v_conv_w, v_conv_b, v_conv_ln_g, v_conv_ln_b, v_attn_sinks, v_w_out, v_ffn2_norm, v_ffn2_w_gate, v_ffn2_w_up, v_ffn2_w_down, v_final_norm):
    cx, cy, cc = _coords()
    chip = 2 * cx + cy
    c_idx = jnp.reshape(cc, (1,)).astype(jnp.int32)
    j_idx = jnp.reshape(chip, (1,)).astype(jnp.int32)
    L = ffn1_norm.shape[0]
    tr = lambda a: jnp.swapaxes(a, 1, 2)

    sh = dict(f1g=tr(ffn1_w_gate), f1u=tr(ffn1_w_up), f1d=ffn1_w_down, f2g=tr(ffn2_w_gate),
              f2u=tr(ffn2_w_up), f2d=ffn2_w_down, win=tr(w_in), wout=w_out)
    sh = {k: [v[l].astype(CDT) for l in range(L)] for k, v in sh.items()}
    W = {}

    def gather_op(keys):
        if keys == ["cw"]:
            return ag_op([conv_w])
        return ag2_op([sh[k[0]][k[1]] for k in keys])

    def take(keys, res):
        for k, a in zip(keys, res):
            if k == "cw":
                W[k] = _own_slab(a, conv_w, chip)
            else:
                mine = sh[k[0]][k[1]]
                W[k] = _own_slab(a, mine.reshape(a.shape[1:]), chip).reshape(N_CHIPS * mine.shape[0], mine.shape[1])

    def with_ag(fn, keys, *args):
        if not keys:
            return fn(*args)
        main, res = fn(*args, comm=gather_op(keys))
        take(keys, res)
        return main

    ag_hosts = {("ffn1", 0): [("win", 0), ("f2g", 0), ("f1d", 1)],
                ("inproj", 0): ["cw"], ("attn", 0): [("wout", 0), ("f2u", 0)], ("conv", 0): [("f2d", 0)],
                ("ffn2", 0): [("f1g", 1), ("f1u", 1)],
                ("ffn1", 1): [("win", 1), ("f2g", 1), ("f2u", 1)],
                ("inproj", 1): [("wout", 1)], ("attn", 1): [("f2d", 1)]}
    first = [("f1g", 0), ("f1u", 0), ("f1d", 0)]
    take(first, first_gather([sh[k_][l] for k_, l in first]))

    cs, sn = _rope_tables(positions)
    saved = []
    h = x[0]
    for l in range(L):
        sink = attn_sinks[l].reshape(2, 4)
        sink_row = jnp.repeat(sink, BLOCK, axis=1).reshape(2, 1, 4 * BLOCK)
        x0 = h
        x1, g1, u1 = with_ag(ffn_fwd, ag_hosts.get(("ffn1", l)), x0, ffn1_norm[l][None],
                             W[("f1g", l)], W[("f1u", l)], W[("f1d", l)])
        wext = _widen_w_in(W[("win", l)])
        q, k, v, u = with_ag(inproj_fwd, ag_hosts.get(("inproj", l)), x1, mix_norm[l][None], wext, cs, sn)
        ao = with_ag(attn_fwd, ag_hosts.get(("attn", l)), q, k, v, sink_row)
        cwl = jnp.transpose(W["cw"][:, l], (1, 0, 2)).reshape(CONV_K, CONV_C)
        co, yc = with_ag(conv_fwd, ag_hosts.get(("conv", l)), u, cwl, conv_b[l][None], conv_ln_g[l][None],
                         conv_ln_b[l][None])
        head = (final_norm[None], loss_target[0]) if l == L - 1 else None
        h, g2, u2, x2, *tail = with_ag(functools.partial(ffn_fwd, mix=(ao, co, W[("wout", l)]), head=head),
                                       ag_hosts.get(("ffn2", l)), x1, ffn2_norm[l][None],
                                       W[("f2g", l)], W[("f2u", l)], W[("f2d", l)])
        saved.append((x0, x1, x2, g1, u1, g2, u2, q, k, v, u, ao, co, yc, sink, wext, cwl))

    dx, (loss, dnf) = h, tail

    active = []

    def advance(run):
        stages = [ch.next_stage() for ch in active]
        ops = [op for op, _ in stages]
        main, res = run(_merge(*ops) if ops else None)
        for (_, cb), r in zip(stages, _split(res, *ops)):
            cb(r)
        active[:] = [ch for ch in active if not ch.done]
        return main

    def hosted(fn, *args):
        def run(comm):
            if comm is None:
                return fn(*args), []
            return fn(*args, comm=comm)
        return advance(run)

    def chain(key, names_, grads):
        chains[key] = _Chain(grads, c_idx, j_idx)
        active.append(chains[key])
        for i_, nme_ in enumerate(names_):
            where[(nme_, key[1])] = (key, i_)

    small = {k_: [None] * L for k_ in ("n1", "nm", "n2", "cw", "cb", "lg", "lb", "sinks")}
    chains, where = {}, {}
    for l in reversed(range(L)):
        x0, x1, x2, g1, u1, g2, u2, q, k, v, u, ao, co, yc, sink, wext, cwl = saved[l]
        sink_row = jnp.repeat(sink, BLOCK, axis=1).reshape(2, 1, 4 * BLOCK)
        dx2, dgt, dup, hh, small["n2"][l], gwd = hosted(
            ffn_bwd_dgrad, x2, ffn2_norm[l][None], dx, g2, u2, W[("f2g", l)], W[("f2u", l)], W[("f2d", l)])
        chain(("f2d", l), ["ffn2_w_down"], [gwd])
        chain(("f2gu", l), ["ffn2_w_gate", "ffn2_w_up"], hosted(ffn_wgrad_rows, hh, [dgt, dup]))
        da, dc, gwout = hosted(outproj_bwd, dx2, ao, co, W[("wout", l)])
        du, small["cw"][l], small["cb"][l], small["lg"][l], small["lb"][l] = hosted(
            conv_bwd, dc, u, yc, cwl, conv_ln_g[l][None], conv_ln_b[l][None])
        dq, dk, dv, dsink = hosted(attn_bwd, q, k, v, da, sink_row)
        small["sinks"][l] = jnp.sum(dsink.reshape(2, 4, BLOCK), axis=-1).reshape(N_HEADS)
        dx1, gwext, small["nm"][l] = inproj_bwd(x1, mix_norm[l][None], dx2, dq, dk, dv, du, wext, cs, sn)
        chain(("mx", l), ["w_out", "w_in"], [gwout, _fold_w_in(gwext)])
        dx, dgt, dup, hh, small["n1"][l], gwd = (hosted if l > 0 else lambda fn, *a: fn(*a))(
            ffn_bwd_dgrad, x0, ffn1_norm[l][None], dx1, g1, u1, W[("f1g", l)], W[("f1u", l)], W[("f1d", l)])
        chain(("f1d", l), ["ffn1_w_down"], [gwd])
        if l > 0:
            chain(("f1gu", l), ["ffn1_w_gate", "ffn1_w_up"], hosted(ffn_wgrad_rows, hh, [dgt, dup]))
        else:
            chain(("f1u", l), ["ffn1_w_up"], hosted(ffn_wgrad_rows, hh, [dup]))
            chain(("f1g", l), ["ffn1_w_gate"], hosted(ffn_wgrad_rows, hh, [dgt]))

    weights = dict(ffn1_norm=ffn1_norm, ffn1_w_gate=ffn1_w_gate, ffn1_w_up=ffn1_w_up, ffn1_w_down=ffn1_w_down,
                   mix_norm=mix_norm, w_in=w_in, conv_w=conv_w, conv_b=conv_b, conv_ln_g=conv_ln_g,
                   conv_ln_b=conv_ln_b, attn_sinks=attn_sinks, w_out=w_out, ffn2_norm=ffn2_norm,
                   ffn2_w_gate=ffn2_w_gate, ffn2_w_up=ffn2_w_up, ffn2_w_down=ffn2_w_down, final_norm=final_norm)
    moms = dict(ffn1_norm=(m_ffn1_norm, v_ffn1_norm), ffn1_w_gate=(m_ffn1_w_gate, v_ffn1_w_gate),
                ffn1_w_up=(m_ffn1_w_up, v_ffn1_w_up), ffn1_w_down=(m_ffn1_w_down, v_ffn1_w_down),
                mix_norm=(m_mix_norm, v_mix_norm), w_in=(m_w_in, v_w_in), conv_w=(m_conv_w, v_conv_w),
                conv_b=(m_conv_b, v_conv_b), conv_ln_g=(m_conv_ln_g, v_conv_ln_g),
                conv_ln_b=(m_conv_ln_b, v_conv_ln_b), attn_sinks=(m_attn_sinks, v_attn_sinks),
                w_out=(m_w_out, v_w_out), ffn2_norm=(m_ffn2_norm, v_ffn2_norm),
                ffn2_w_gate=(m_ffn2_w_gate, v_ffn2_w_gate), ffn2_w_up=(m_ffn2_w_up, v_ffn2_w_up),
                ffn2_w_down=(m_ffn2_w_down, v_ffn2_w_down), final_norm=(m_final_norm, v_final_norm))
    names = list(weights)
    big_names = dict(ffn2_w_gate=True, ffn2_w_up=True, ffn2_w_down=False, w_out=False, w_in=True,
                     ffn1_w_down=False, ffn1_w_up=True, ffn1_w_gate=True)
    grads, delta, new_m, new_v = {}, {}, {}, {}

    def big_adamw(nme):
        view = tr if big_names[nme] else (lambda a: a)
        res = adamw_layers(view(weights[nme]), view(moms[nme][0]), view(moms[nme][1]),
                           [chains[where[(nme, l)][0]].final[where[(nme, l)][1]] for l in range(L)])
        grads[nme], delta[nme], new_m[nme], new_v[nme] = [view(a) for a in res]

    while active:
        advance(lambda comm: (None, _run_comm(comm)))
    for nme in big_names:
        big_adamw(nme)

    G = {k_: jnp.stack(v_) for k_, v_ in small.items()}
    G["nf"] = dnf
    small_shapes = [shp for _, shp in _SMALL]
    n_small = 1 + sum(math.prod(s) for s in small_shapes)
    rows = -(-n_small // 1024) * 8
    packed = _pack([loss] + [G[k_] for k_, _ in _SMALL], rows)
    summed = all_reduce_small(packed)
    loss_out, *small_sum = _unpack(summed, [()] + small_shapes)
    gs = dict(zip([k_ for k_, _ in _SMALL], small_sum))
    gs["cw"] = lax.dynamic_slice_in_dim(gs["cw"], chip * (CONV_C // N_CHIPS), CONV_C // N_CHIPS, axis=2)
    grads.update(ffn1_norm=gs["n1"], mix_norm=gs["nm"], conv_w=gs["cw"], conv_b=gs["cb"], conv_ln_g=gs["lg"],
                 conv_ln_b=gs["lb"], attn_sinks=gs["sinks"], ffn2_norm=gs["n2"], final_norm=gs["nf"])

    small_names = [nme for nme in names if nme not in big_names]
    s_shapes = [weights[nme].shape for nme in small_names]
    n_tot = sum(math.prod(s) for s in s_shapes)
    srows = -(-n_tot // 1024) * 8
    d, nm_, nv_ = adamw(_pack([weights[nme] for nme in small_names], srows),
                        _pack([grads[nme] for nme in small_names], srows),
                        _pack([moms[nme][0] for nme in small_names], srows),
                        _pack([moms[nme][1] for nme in small_names], srows))
    for nme, dd, mm, vv in zip(small_names, _unpack(d, s_shapes), _unpack(nm_, s_shapes), _unpack(nv_, s_shapes)):
        delta[nme], new_m[nme], new_v[nme] = dd, mm, vv

    return (loss_out, dx[None], *[grads[nme] for nme in names], *[delta[nme] for nme in names],
            *[new_m[nme] for nme in names], *[new_v[nme] for nme in names])
```

```python
import functools
import math

import jax
import jax.numpy as jnp
from jax import lax
from jax.experimental import pallas as pl
from jax.experimental.pallas import tpu as pltpu

F32 = jnp.float32
CDT = jnp.bfloat16
D_MODEL = 1024
D_FF = 2816
N_HEADS = 8
HEAD_DIM = 64
BLOCK = 128
CONV_K = 31
CONV_C = 512
ATT_W = 512
D_EXT = 2048
EPS = 1e-5
HALO = 32
FF_CHUNK = 256
NEG = float(jnp.finfo(jnp.float32).min)
VMEM_LIMIT = 56 * 1024 * 1024

ADAM_LR = 0.001
ADAM_B1 = 0.9
ADAM_B2 = 0.999
ADAM_EPS = 1e-08
ADAM_WD = 0.01
ADAM_STEP = 10

NT = (((1,), (1,)), ((), ()))
TN = (((0,), (0,)), ((), ()))


MESH = pl.DeviceIdType.MESH
ANY = pl.BlockSpec(memory_space=pl.ANY)
N_CHIPS = 4
N_DEV = 8


def _params(n_axes):
    return pltpu.CompilerParams(dimension_semantics=("arbitrary",) * n_axes, vmem_limit_bytes=VMEM_LIMIT)


class _Comm:
    def __init__(self, name, inputs, out_shape, sems, descs, relay=None):
        self.name, self.inputs, self.out_shape, self.sems = name, list(inputs), list(out_shape), list(sems)
        self.descs, self.relay = descs, relay


def _merge(*ops):
    ops = [o for o in ops if o is not None]
    if len(ops) == 1:
        return ops[0]
    assert all(o.relay is None for o in ops)

    def descs(cins, couts, sems):
        out, i, o, s = [], 0, 0, 0
        for op in ops:
            ni, no, ns = len(op.inputs), len(op.out_shape), len(op.sems)
            out += op.descs(cins[i:i + ni], couts[o:o + no], sems[s:s + ns])
            i, o, s = i + ni, o + no, s + ns
        return out

    return _Comm("_".join(o.name for o in ops), sum((o.inputs for o in ops), []),
                 sum((o.out_shape for o in ops), []), sum((o.sems for o in ops), []), descs)


def _split(couts, *ops):
    res, o = [], 0
    for op in ops:
        res.append(couts[o:o + len(op.out_shape)])
        o += len(op.out_shape)
    return res


def _hosted(comm):
    if comm is None:
        return pl.pallas_call

    def make(body, *, name, grid, in_specs, out_specs, out_shape, compiler_params, scratch_shapes=()):
        single = not isinstance(out_shape, (list, tuple))
        o_specs = [out_specs] if single else list(out_specs)
        o_shape = [out_shape] if single else list(out_shape)
        n_in, n_out, n_sc = len(in_specs), len(o_specs), len(scratch_shapes)
        c_in, c_out = len(comm.inputs), len(comm.out_shape)

        def hosted(*refs):
            ins, cins = refs[:n_in], refs[n_in:n_in + c_in]
            o0 = n_in + c_in
            outs, couts = refs[o0:o0 + n_out], refs[o0 + n_out:o0 + n_out + c_out]
            s0 = o0 + n_out + c_out
            scr, sems = refs[s0:s0 + n_sc], refs[s0 + n_sc:]
            first = pl.program_id(0) == 0
            last = pl.program_id(0) == grid[0] - 1
            for ax in range(1, len(grid)):
                first = first & (pl.program_id(ax) == 0)
                last = last & (pl.program_id(ax) == grid[ax] - 1)

            @pl.when(first)
            def _():
                for d in comm.descs(cins, couts, sems):
                    d.start()

            if comm.relay is not None:
                assert len(grid) == 1

                @pl.when(pl.program_id(0) == (3 * grid[0]) // 4)
                def _():
                    for d in comm.descs(cins, couts, sems):
                        d.wait()
                    for d in comm.relay(cins, couts, sems):
                        d.start()

            body(*ins, *outs, *scr)

            @pl.when(last)
            def _():
                for d in (comm.relay or comm.descs)(cins, couts, sems):
                    d.wait()

        call = pl.pallas_call(
            hosted, name=f"{name}_{comm.name}", grid=grid,
            in_specs=list(in_specs) + [ANY] * c_in, out_specs=o_specs + [ANY] * c_out,
            out_shape=o_shape + comm.out_shape, scratch_shapes=list(scratch_shapes) + comm.sems,
            compiler_params=compiler_params)

        def run(*args):
            res = call(*args, *comm.inputs)
            return (res[0] if single else list(res[:n_out])), list(res[n_out:])

        return run

    return make


def _run_comm(comm):
    c_in = len(comm.inputs)

    def body(*refs):
        cins, couts, sems = refs[:c_in], refs[c_in:c_in + len(comm.out_shape)], refs[c_in + len(comm.out_shape):]
        ds = comm.descs(cins, couts, sems)
        for d in ds:
            d.start()
        for d in ds:
            d.wait()

    return list(pl.pallas_call(
        body, name=comm.name, in_specs=[ANY] * c_in, out_specs=[ANY] * len(comm.out_shape),
        out_shape=comm.out_shape, scratch_shapes=comm.sems)(*comm.inputs))


def _coords():
    return lax.axis_index("x"), lax.axis_index("y"), lax.axis_index("c")


def _other_chips(x, y):
    return [(1 - x, y), (x, 1 - y), (1 - x, 1 - y)]


def ag_op(shards):
    n = len(shards)

    def descs(cins, couts, sems):
        send, recv = sems
        x, y, c = _coords()
        j = 2 * x + y
        ds = []
        for a in range(n):
            for p, (px, py) in enumerate(_other_chips(x, y)):
                ds.append(pltpu.make_async_remote_copy(cins[a], couts[a].at[j], send.at[a, p], recv.at[a, p],
                                                       device_id=(px, py, c), device_id_type=MESH))
        return ds

    return _Comm("ag", shards, [jax.ShapeDtypeStruct((N_CHIPS,) + s.shape, s.dtype) for s in shards],
                 [pltpu.SemaphoreType.DMA((n, 3)), pltpu.SemaphoreType.DMA((n, 3))], descs)


def ag2_op(shards):
    n = len(shards)
    halves = [s.reshape(2, s.shape[0] // 2, s.shape[1]) for s in shards]

    def descs(cins, couts, sems):
        x, y, c = _coords()
        j = 2 * x + y
        return [pltpu.make_async_remote_copy(cins[a].at[c], couts[a].at[j, c], sems[0].at[a, p], sems[1].at[a, p],
                                             device_id=(px, py, c), device_id_type=MESH)
                for a in range(n) for p, (px, py) in enumerate(_other_chips(x, y))]

    def relay(cins, couts, sems):
        x, y, c = _coords()
        return [pltpu.make_async_remote_copy(couts[a].at[2 * px + py, c], couts[a].at[2 * px + py, c],
                                             sems[2].at[a, p], sems[3].at[a, p],
                                             device_id=(x, y, 1 - c), device_id_type=MESH)
                for a in range(n) for p, (px, py) in enumerate(_other_chips(x, y))]

    return _Comm("ag2", halves, [jax.ShapeDtypeStruct((N_CHIPS,) + h.shape, h.dtype) for h in halves],
                 [pltpu.SemaphoreType.DMA((n, 3))] * 4, descs, relay)


def _own_slab(gathered, mine, idx):
    return lax.dynamic_update_slice_in_dim(gathered, mine[None], idx, axis=0)


def first_gather(shards):
    n = len(shards)
    halves = [s.reshape(2, s.shape[0] // 2, s.shape[1]) for s in shards]

    def body(*refs):
        ins, outs = refs[:n], refs[n:2 * n]
        send1, recv1, send2, recv2 = refs[2 * n:]
        x, y, c = _coords()
        j = 2 * x + y
        chips = _other_chips(x, y)
        ici = [pltpu.make_async_remote_copy(ins[a].at[c], outs[a].at[j, c], send1.at[a, p], recv1.at[a, p],
                                            device_id=(px, py, c), device_id_type=MESH)
               for a in range(n) for p, (px, py) in enumerate(chips)]
        for d in ici:
            d.start()
        d2d = [pltpu.make_async_remote_copy(outs[a].at[2 * px + py, c], outs[a].at[2 * px + py, c],
                                            send2.at[a, p], recv2.at[a, p],
                                            device_id=(x, y, 1 - c), device_id_type=MESH)
               for a in range(n) for p, (px, py) in enumerate(chips)]
        for d1, d2 in zip(ici, d2d):
            d1.wait()
            d2.start()
        for d in d2d:
            d.wait()

    return list(pl.pallas_call(
        body, name="first_gather", in_specs=[ANY] * n, out_specs=[ANY] * n,
        out_shape=[jax.ShapeDtypeStruct((N_CHIPS,) + h.shape, h.dtype) for h in halves],
        scratch_shapes=[pltpu.SemaphoreType.DMA((n, 3))] * 4)(*halves))


def swap_op(grads):
    n = len(grads)

    def descs(cins, couts, sems):
        send, recv = sems
        x, y, c = _coords()
        return [pltpu.make_async_remote_copy(cins[a].at[:, 1 - c], couts[a], send.at[a], recv.at[a],
                                             device_id=(x, y, 1 - c), device_id_type=MESH) for a in range(n)]

    return _Comm("swap", grads, [jax.ShapeDtypeStruct(g.shape[:1] + g.shape[2:], g.dtype) for g in grads],
                 [pltpu.SemaphoreType.DMA((n,)), pltpu.SemaphoreType.DMA((n,))], descs)


def exchange_op(parts):
    n = len(parts)

    def descs(cins, couts, sems):
        send, recv = sems
        x, y, c = _coords()
        ds = []
        for a in range(n):
            for p, (px, py) in enumerate(_other_chips(x, y)):
                ds.append(pltpu.make_async_remote_copy(cins[a].at[2 * px + py], couts[a].at[p], send.at[a, p],
                                                       recv.at[a, p], device_id=(px, py, c), device_id_type=MESH))
        return ds

    return _Comm("xchg", parts, [jax.ShapeDtypeStruct((3,) + p.shape[1:], p.dtype) for p in parts],
                 [pltpu.SemaphoreType.DMA((n, 3)), pltpu.SemaphoreType.DMA((n, 3))], descs)


def share_op(totals):
    n = len(totals)

    def descs(cins, couts, sems):
        send, recv = sems
        x, y, c = _coords()
        return [pltpu.make_async_remote_copy(cins[a], couts[a].at[c], send.at[a], recv.at[a],
                                             device_id=(x, y, 1 - c), device_id_type=MESH) for a in range(n)]

    return _Comm("share", totals, [jax.ShapeDtypeStruct((2,) + t.shape, t.dtype) for t in totals],
                 [pltpu.SemaphoreType.DMA((n,)), pltpu.SemaphoreType.DMA((n,))], descs)


def _sigmoid(z):
    return 1.0 / (1.0 + jnp.exp(-z))


def _rms_parts(xf):
    r = lax.rsqrt(jnp.mean(xf * xf, axis=-1, keepdims=True) + EPS)
    return xf * r, r


def _rms_bwd(dh, xhat, r, g):
    dg = jnp.sum(dh * xhat, axis=0, keepdims=True)
    dxhat = dh * g
    dx = r * (dxhat - xhat * jnp.mean(dxhat * xhat, axis=-1, keepdims=True))
    return dx, dg


def _chunks(n, ck=FF_CHUNK):
    return [(c0, min(ck, n - c0)) for c0 in range(0, n, ck)]


def ffn_fwd(x, g, wgT, wuT, wd, *, mix=None, head=None, tm=512, comm=None):
    S, D = x.shape
    F = wgT.shape[0]
    n_in = (3 if mix is not None else 0) + (2 if head is not None else 0)

    def body(x_ref, g_ref, wg_ref, wu_ref, wd_ref, *refs):
        ins, outs, a_sc = list(refs[:n_in]), list(refs[n_in:-1]), refs[-1]
        o_ref, gate_ref, up_ref = outs[:3]
        xf = x_ref[...]
        if mix is not None:
            ao_ref, co_ref, wo_ref = ins[:3]
            xf = (xf + jnp.dot(ao_ref[...], wo_ref[0:ATT_W, :], preferred_element_type=F32)
                  + jnp.dot(co_ref[...], wo_ref[ATT_W:ATT_W + CONV_C, :], preferred_element_type=F32))
            outs[3][...] = xf
        xhat, _ = _rms_parts(xf)
        h = (xhat * g_ref[...]).astype(CDT)
        for c0, cw_ in _chunks(F):
            sl = slice(c0, c0 + cw_)
            gt = lax.dot_general(h, wg_ref[sl, :], NT, preferred_element_type=F32)
            ut = lax.dot_general(h, wu_ref[sl, :], NT, preferred_element_type=F32)
            gate_ref[:, sl] = gt.astype(CDT)
            up_ref[:, sl] = ut.astype(CDT)
            a_sc[:, sl] = (gt * _sigmoid(gt) * ut).astype(CDT)
        y = xf + 0.5 * jnp.dot(a_sc[...], wd_ref[...], preferred_element_type=F32)
        if head is None:
            o_ref[...] = y
        else:
            fg_ref, t_ref = ins[-2:]
            loss_ref, dfg_ref = outs[-2:]

            @pl.when(pl.program_id(0) == 0)
            def _():
                loss_ref[...] = jnp.zeros_like(loss_ref)
                dfg_ref[...] = jnp.zeros_like(dfg_ref)

            yhat, r = _rms_parts(y)
            fg = fg_ref[...]
            err = yhat * fg - t_ref[...]
            loss_ref[...] += 0.5 * jnp.sum(jnp.mean(err * err, axis=-1, keepdims=True), axis=0, keepdims=True)
            dy, dfg = _rms_bwd(err * (1.0 / D), yhat, r, fg)
            o_ref[...] = dy
            dfg_ref[...] += dfg

    wspec = pl.BlockSpec((F, D), lambda i: (0, 0), pipeline_mode=pl.Buffered(1))
    row = pl.BlockSpec((tm, D), lambda i: (i, 0))
    act = pl.BlockSpec((tm, F), lambda i: (i, 0))
    vec = pl.BlockSpec((1, D), lambda i: (0, 0))
    in_specs = [row, vec, wspec, wspec, wspec]
    out_specs = [row, act, act]
    out_shape = [jax.ShapeDtypeStruct((S, D), F32), jax.ShapeDtypeStruct((S, F), CDT),
                 jax.ShapeDtypeStruct((S, F), CDT)]
    if mix is not None:
        in_specs += [pl.BlockSpec((tm, ATT_W), lambda i: (i, 0)), pl.BlockSpec((tm, CONV_C), lambda i: (i, 0)),
                     pl.BlockSpec((D, D), lambda i: (0, 0), pipeline_mode=pl.Buffered(1))]
        out_specs.append(row)
        out_shape.append(jax.ShapeDtypeStruct((S, D), F32))
    if head is not None:
        in_specs += [vec, row]
        out_specs += [pl.BlockSpec((1, 1), lambda i: (0, 0)), vec]
        out_shape += [jax.ShapeDtypeStruct((1, 1), F32), jax.ShapeDtypeStruct((1, D), F32)]
    name = "ffn_fwd" if mix is None else "mix_ffn_fwd"
    return _hosted(comm)(
        body, name=name if head is None else name + "_loss",
        grid=(S // tm,),
        in_specs=in_specs, out_specs=out_specs, out_shape=out_shape,
        scratch_shapes=[pltpu.VMEM((tm, F), CDT)],
        compiler_params=_params(1),
    )(x, g, wgT, wuT, wd, *(mix or ()), *(head or ()))


def ffn_bwd_dgrad(x, g, dy, gate, up, wgT, wuT, wd, *, tm=256, comm=None):
    S, D = x.shape
    F = wgT.shape[0]

    def body(x_ref, g_ref, dy_ref, gate_ref, up_ref, wg_ref, wu_ref, wd_ref,
             dx_ref, dgate_ref, dup_ref, h_ref, dg_ref, dwd_ref):
        @pl.when(pl.program_id(0) == 0)
        def _():
            dg_ref[...] = jnp.zeros_like(dg_ref)
            dwd_ref[...] = jnp.zeros_like(dwd_ref)

        dyf = dy_ref[...]
        dacc = (0.5 * dyf).astype(CDT)
        gg = g_ref[...]
        xhat, r = _rms_parts(x_ref[...])
        h_ref[...] = (xhat * gg).astype(CDT)
        for c0, cw_ in _chunks(F):
            sl = slice(c0, c0 + cw_)
            d_a = lax.dot_general(dacc, wd_ref[sl, :], NT, preferred_element_type=F32)
            gt = gate_ref[:, sl].astype(F32)
            ut = up_ref[:, sl].astype(F32)
            sg = _sigmoid(gt)
            silu = gt * sg
            dup_ref[:, sl] = (d_a * silu).astype(CDT)
            dgate_ref[:, sl] = (d_a * ut * (sg * (1.0 + gt * (1.0 - sg)))).astype(CDT)
            dwd_ref[sl, :] += lax.dot_general((silu * ut).astype(CDT), dacc, TN, preferred_element_type=F32)
        dh = (jnp.dot(dgate_ref[...], wg_ref[...], preferred_element_type=F32)
              + jnp.dot(dup_ref[...], wu_ref[...], preferred_element_type=F32))
        dx, dg = _rms_bwd(dh, xhat, r, gg)
        dx_ref[...] = dyf + dx
        dg_ref[...] += dg

    wspec = pl.BlockSpec((F, D), lambda i: (0, 0), pipeline_mode=pl.Buffered(1))
    row = pl.BlockSpec((tm, D), lambda i: (i, 0))
    act = pl.BlockSpec((tm, F), lambda i: (i, 0))
    vec = pl.BlockSpec((1, D), lambda i: (0, 0))
    return _hosted(comm)(
        body, name="ffn_bwd_dgrad",
        grid=(S // tm,),
        in_specs=[row, vec, row, act, act, wspec, wspec, wspec],
        out_specs=[row, act, act, row, vec, wspec],
        out_shape=[jax.ShapeDtypeStruct((S, D), F32),
                   jax.ShapeDtypeStruct((S, F), CDT),
                   jax.ShapeDtypeStruct((S, F), CDT),
                   jax.ShapeDtypeStruct((S, D), CDT),
                   jax.ShapeDtypeStruct((1, D), F32),
                   jax.ShapeDtypeStruct((F, D), F32)],
        compiler_params=_params(1),
    )(x, g, dy, gate, up, wgT, wuT, wd)


def ffn_wgrad_rows(h, acts, *, tk=1024, comm=None):
    S, D = h.shape
    F = acts[0].shape[1]
    n = len(acts)
    tk = min(tk, S)

    def body(h_ref, *refs):
        act_refs, dw_refs = refs[:n], refs[n:]

        @pl.when(pl.program_id(0) == 0)
        def _():
            for dw_ref in dw_refs:
                dw_ref[...] = jnp.zeros_like(dw_ref)

        hh = h_ref[...]
        for c0, cw_ in _chunks(F):
            sl = slice(c0, c0 + cw_)
            for act_ref, dw_ref in zip(act_refs, dw_refs):
                dw_ref[sl, :] += lax.dot_general(act_ref[:, sl], hh, TN, preferred_element_type=F32)

    act = pl.BlockSpec((tk, F), lambda k: (k, 0))
    out = pl.BlockSpec((F, D), lambda k: (0, 0), pipeline_mode=pl.Buffered(1))
    return _hosted(comm)(
        body, name="ffn_wgrad_rows",
        grid=(S // tk,),
        in_specs=[pl.BlockSpec((tk, D), lambda k: (k, 0))] + [act] * n,
        out_specs=[out] * n,
        out_shape=[jax.ShapeDtypeStruct((F, D), F32)] * n,
        compiler_params=_params(1),
    )(h, *acts)


def _rope_apply(t, cs, sn):
    lane = lax.broadcasted_iota(jnp.int32, t.shape, 1)
    first = (lane % HEAD_DIM) < (HEAD_DIM // 2)
    rot = jnp.where(first, pltpu.roll(t, 128 - HEAD_DIM // 2, 1), pltpu.roll(t, HEAD_DIM // 2, 1))
    return t * cs + rot * sn


def _rope_transpose(d, cs, sn):
    lane = lax.broadcasted_iota(jnp.int32, d.shape, 1)
    first = (lane % HEAD_DIM) < (HEAD_DIM // 2)
    ds = d * sn
    rot = jnp.where(first, pltpu.roll(ds, 128 - HEAD_DIM // 2, 1), pltpu.roll(ds, HEAD_DIM // 2, 1))
    return d * cs + rot


def inproj_fwd(x, g, wextT, cs, sn, *, tm=512, comm=None):
    S, D = x.shape
    scale = HEAD_DIM ** -0.5

    def body(x_ref, g_ref, w_ref, cs_ref, sn_ref, q_ref, k_ref, v_ref, u_ref):
        xhat, _ = _rms_parts(x_ref[...])
        h = (xhat * g_ref[...]).astype(CDT)
        p = lax.dot_general(h, w_ref[...], NT, preferred_element_type=F32)
        c, s = cs_ref[...], sn_ref[...]
        for b in range(4):
            q_ref[:, 128 * b:128 * (b + 1)] = (_rope_apply(p[:, 128 * b:128 * (b + 1)], c, s) * scale).astype(CDT)
        for b in range(2):
            k_ref[:, 128 * b:128 * (b + 1)] = _rope_apply(p[:, 512 + 128 * b:512 + 128 * (b + 1)], c, s).astype(CDT)
        v_ref[...] = p[:, 768:1024].astype(CDT)
        u_ref[...] = p[:, 1024:2048]

    def row(w):
        return pl.BlockSpec((tm, w), lambda i: (i, 0))

    return _hosted(comm)(
        body, name="inproj_fwd",
        grid=(S // tm,),
        in_specs=[row(D), pl.BlockSpec((1, D), lambda i: (0, 0)),
                  pl.BlockSpec((D_EXT, D), lambda i: (0, 0)), row(128), row(128)],
        out_specs=[row(512), row(256), row(256), row(1024)],
        out_shape=[jax.ShapeDtypeStruct((S, 512), CDT), jax.ShapeDtypeStruct((S, 256), CDT),
                   jax.ShapeDtypeStruct((S, 256), CDT), jax.ShapeDtypeStruct((S, 1024), F32)],
        compiler_params=_params(1),
    )(x, g, wextT, cs, sn)


def _stack_heads(p0, p1):
    lane = lax.broadcasted_iota(jnp.int32, p0.shape, 1)
    lo = lane < HEAD_DIM
    z = jnp.zeros_like(p0)
    return jnp.concatenate([jnp.where(lo, p0, z), jnp.where(lo, z, p0),
                            jnp.where(lo, p1, z), jnp.where(lo, z, p1)], axis=0)


def _unstack_heads(o):
    lane = lax.broadcasted_iota(jnp.int32, (BLOCK, 128), 1)
    lo = lane < HEAD_DIM
    return (jnp.where(lo, o[0:128], o[128:256]), jnp.where(lo, o[256:384], o[384:512]))


def _band_mask_kq(n):
    c = lax.broadcasted_iota(jnp.int32, (2 * BLOCK, 4 * BLOCK), 0)
    i = lax.broadcasted_iota(jnp.int32, (2 * BLOCK, 4 * BLOCK), 1) % BLOCK
    return (c > i) & (c <= i + BLOCK) & ((n > 0) | (c >= BLOCK))


def attn_fwd(q, k, v, sink_row, *, nb=4, comm=None):
    S = q.shape[0]
    tq = nb * BLOCK

    def body(q_ref, k_ref, v_ref, sink_ref, o_ref):
        t = pl.program_id(0)
        for b in range(nb):
            n = t * nb + b
            prev = pl.multiple_of(jnp.maximum(n - 1, 0) * BLOCK, BLOCK)
            cur = pl.multiple_of(n * BLOCK, BLOCK)
            rows = slice(b * BLOCK, (b + 1) * BLOCK)
            mask = _band_mask_kq(n)
            for gidx in range(2):
                lanes = slice(128 * gidx, 128 * (gidx + 1))
                qs = _stack_heads(q_ref[rows, 256 * gidx:256 * gidx + 128],
                                  q_ref[rows, 256 * gidx + 128:256 * gidx + 256])
                kb = jnp.concatenate([k_ref[pl.ds(prev, BLOCK), lanes], k_ref[pl.ds(cur, BLOCK), lanes]], axis=0)
                vb = jnp.concatenate([v_ref[pl.ds(prev, BLOCK), lanes], v_ref[pl.ds(cur, BLOCK), lanes]], axis=0)
                st = lax.dot_general(kb, qs, NT, preferred_element_type=F32)
                st = jnp.where(mask, st, NEG)
                sink = sink_ref[gidx]
                m = jnp.maximum(jnp.max(st, axis=0, keepdims=True), sink)
                e = jnp.exp(st - m)
                inv = 1.0 / (jnp.sum(e, axis=0, keepdims=True) + jnp.exp(sink - m))
                o = lax.dot_general((e * inv).astype(CDT), vb, TN, preferred_element_type=F32)
                o0, o1 = _unstack_heads(o)
                o_ref[rows, 256 * gidx:256 * gidx + 128] = o0.astype(CDT)
                o_ref[rows, 256 * gidx + 128:256 * gidx + 256] = o1.astype(CDT)

    return _hosted(comm)(
        body, name="attn_fwd",
        grid=(S // tq,),
        in_specs=[pl.BlockSpec((tq, 512), lambda t: (t, 0)),
                  pl.BlockSpec((S, 256), lambda t: (0, 0)),
                  pl.BlockSpec((S, 256), lambda t: (0, 0)),
                  pl.BlockSpec((2, 1, 4 * BLOCK), lambda t: (0, 0, 0))],
        out_specs=pl.BlockSpec((tq, 512), lambda t: (t, 0)),
        out_shape=jax.ShapeDtypeStruct((S, 512), CDT),
        compiler_params=_params(1),
    )(q, k, v, sink_row)


def attn_bwd(q, k, v, do, sink_row, *, nb=4, comm=None):
    S = q.shape[0]
    tq = nb * BLOCK
    scale = HEAD_DIM ** -0.5

    def body(q_ref, k_ref, v_ref, do_ref, sink_ref, dq_ref, dk_ref, dv_ref, dsink_ref):
        t = pl.program_id(0)

        @pl.when(t == 0)
        def _():
            dk_ref[...] = jnp.zeros_like(dk_ref)
            dv_ref[...] = jnp.zeros_like(dv_ref)
            dsink_ref[...] = jnp.zeros_like(dsink_ref)

        for b in range(nb):
            n = t * nb + b
            prev = pl.multiple_of(jnp.maximum(n - 1, 0) * BLOCK, BLOCK)
            cur = pl.multiple_of(n * BLOCK, BLOCK)
            rows = slice(b * BLOCK, (b + 1) * BLOCK)
            mask = _band_mask_kq(n)
            for gidx in range(2):
                lanes = slice(128 * gidx, 128 * (gidx + 1))
                qs = _stack_heads(q_ref[rows, 256 * gidx:256 * gidx + 128],
                                  q_ref[rows, 256 * gidx + 128:256 * gidx + 256])
                dos = _stack_heads(do_ref[rows, 256 * gidx:256 * gidx + 128],
                                   do_ref[rows, 256 * gidx + 128:256 * gidx + 256])
                kb = jnp.concatenate([k_ref[pl.ds(prev, BLOCK), lanes], k_ref[pl.ds(cur, BLOCK), lanes]], axis=0)
                vb = jnp.concatenate([v_ref[pl.ds(prev, BLOCK), lanes], v_ref[pl.ds(cur, BLOCK), lanes]], axis=0)
                st = lax.dot_general(kb, qs, NT, preferred_element_type=F32)
                st = jnp.where(mask, st, NEG)
                sink = sink_ref[gidx]
                m = jnp.maximum(jnp.max(st, axis=0, keepdims=True), sink)
                e = jnp.exp(st - m)
                es = jnp.exp(sink - m)
                inv = 1.0 / (jnp.sum(e, axis=0, keepdims=True) + es)
                pt = e * inv
                dpt = lax.dot_general(vb, dos, NT, preferred_element_type=F32)
                delta = jnp.sum(pt * dpt, axis=0, keepdims=True)
                dst = (pt * (dpt - delta)).astype(CDT)
                dsink_ref[gidx] += -(es * inv) * delta
                dvb = jnp.dot(pt.astype(CDT), dos, preferred_element_type=F32)
                dkb = jnp.dot(dst, qs, preferred_element_type=F32)
                dqs = lax.dot_general(dst, kb, TN, preferred_element_type=F32) * scale
                dq0, dq1 = _unstack_heads(dqs)
                dq_ref[rows, 256 * gidx:256 * gidx + 128] = dq0
                dq_ref[rows, 256 * gidx + 128:256 * gidx + 256] = dq1
                dk_ref[pl.ds(prev, BLOCK), lanes] += dkb[0:BLOCK]
                dk_ref[pl.ds(cur, BLOCK), lanes] += dkb[BLOCK:2 * BLOCK]
                dv_ref[pl.ds(prev, BLOCK), lanes] += dvb[0:BLOCK]
                dv_ref[pl.ds(cur, BLOCK), lanes] += dvb[BLOCK:2 * BLOCK]

    full = pl.BlockSpec((S, 256), lambda t: (0, 0))
    tile = pl.BlockSpec((tq, 512), lambda t: (t, 0))
    srow = pl.BlockSpec((2, 1, 4 * BLOCK), lambda t: (0, 0, 0))
    return _hosted(comm)(
        body, name="attn_bwd",
        grid=(S // tq,),
        in_specs=[tile, full, full, tile, srow],
        out_specs=[tile, full, full, srow],
        out_shape=[jax.ShapeDtypeStruct((S, 512), F32), jax.ShapeDtypeStruct((S, 256), F32),
                   jax.ShapeDtypeStruct((S, 256), F32), jax.ShapeDtypeStruct((2, 1, 4 * BLOCK), F32)],
        compiler_params=_params(1),
    )(q, k, v, do, sink_row)


def _glu(u):
    a = u[:, 0:CONV_C]
    gt = u[:, CONV_C:2 * CONV_C]
    sg = _sigmoid(gt)
    return a, sg, a * sg


CONV_CHUNK = 32


def _shifted_copies(buf, shifted, n):
    for r in range(1, 8):
        shifted[r - 1, 0:n, :] = buf[r:r + n, :]


def _shifted_rows(buf, shifted, start, rows):
    r = start % 8
    if r == 0:
        return buf[start:start + rows, :]
    return shifted[r - 1, start - r:start - r + rows, :]


def conv_fwd(u, cw, cb, lg, lb, *, tm=512, comm=None):
    S = u.shape[0]
    nh = tm // HALO

    def body(u_ref, uh_ref, cw_ref, cb_ref, lg_ref, lb_ref, o_ref, y_ref, hbuf, hsh):
        t = pl.program_id(0)
        _, _, hg = _glu(u_ref[...])
        _, _, hh = _glu(uh_ref[...])
        hbuf[0:HALO, :] = jnp.where(t > 0, hh, jnp.zeros_like(hh))
        hbuf[HALO:HALO + tm, :] = hg
        hbuf[HALO + tm:HALO + tm + 8, :] = jnp.zeros((8, CONV_C), F32)
        _shifted_copies(hbuf, hsh, HALO + tm)
        off = HALO - (CONV_K - 1)
        for c0 in range(0, tm, CONV_CHUNK):
            acc = jnp.zeros((CONV_CHUNK, CONV_C), F32) + cb_ref[...]
            for j in range(CONV_K):
                acc = acc + cw_ref[j:j + 1, :] * _shifted_rows(hbuf, hsh, c0 + off + j, CONV_CHUNK)
            y_ref[c0:c0 + CONV_CHUNK, :] = acc
        y = y_ref[...]
        yc = y - jnp.mean(y, axis=-1, keepdims=True)
        r = lax.rsqrt(jnp.mean(yc * yc, axis=-1, keepdims=True) + EPS)
        z = yc * r * lg_ref[...] + lb_ref[...]
        o_ref[...] = (z * _sigmoid(z)).astype(CDT)

    vec = pl.BlockSpec((1, CONV_C), lambda t: (0, 0))
    return _hosted(comm)(
        body, name="conv_fwd",
        grid=(S // tm,),
        in_specs=[pl.BlockSpec((tm, 2 * CONV_C), lambda t: (t, 0)),
                  pl.BlockSpec((HALO, 2 * CONV_C), lambda t: (jnp.maximum(t * nh - 1, 0), 0)),
                  pl.BlockSpec((CONV_K, CONV_C), lambda t: (0, 0)), vec, vec, vec],
        out_specs=[pl.BlockSpec((tm, CONV_C), lambda t: (t, 0)), pl.BlockSpec((tm, CONV_C), lambda t: (t, 0))],
        out_shape=[jax.ShapeDtypeStruct((S, CONV_C), CDT), jax.ShapeDtypeStruct((S, CONV_C), F32)],
        scratch_shapes=[pltpu.VMEM((HALO + tm + 8, CONV_C), F32), pltpu.VMEM((7, HALO + tm, CONV_C), F32)],
        compiler_params=_params(1),
    )(u, u, cw, cb, lg, lb)


def conv_bwd(dc, u, y, cw, lg, lb, *, tm=512, comm=None):
    S = u.shape[0]
    nh = tm // HALO
    nt = S // tm

    def ln_bwd(dcv, yv, lgv, lbv):
        yc = yv - jnp.mean(yv, axis=-1, keepdims=True)
        r = lax.rsqrt(jnp.mean(yc * yc, axis=-1, keepdims=True) + EPS)
        yhat = yc * r
        z = yhat * lgv + lbv
        sg = _sigmoid(z)
        dz = dcv * (sg * (1.0 + z * (1.0 - sg)))
        dyhat = dz * lgv
        dy = r * (dyhat - jnp.mean(dyhat, axis=-1, keepdims=True)
                  - yhat * jnp.mean(dyhat * yhat, axis=-1, keepdims=True))
        return dy, dz, yhat

    def body(dc_ref, dcn_ref, u_ref, uh_ref, y_ref, yn_ref, cw_ref, lg_ref, lb_ref,
             du_ref, dcw_ref, dcb_ref, dlg_ref, dlb_ref, hbuf, dybuf, dhg_sc, dw_sc, hsh, dysh):
        t = pl.program_id(0)

        @pl.when(t == 0)
        def _():
            dw_sc[...] = jnp.zeros_like(dw_sc)
            dcb_ref[...] = jnp.zeros_like(dcb_ref)
            dlg_ref[...] = jnp.zeros_like(dlg_ref)
            dlb_ref[...] = jnp.zeros_like(dlb_ref)

        lgv, lbv = lg_ref[...], lb_ref[...]
        dy, dz, yhat = ln_bwd(dc_ref[...].astype(F32), y_ref[...], lgv, lbv)
        dyn, _, _ = ln_bwd(dcn_ref[...].astype(F32), yn_ref[...], lgv, lbv)
        dlb_ref[...] += jnp.sum(dz, axis=0, keepdims=True)
        dlg_ref[...] += jnp.sum(dz * yhat, axis=0, keepdims=True)
        dcb_ref[...] += jnp.sum(dy, axis=0, keepdims=True)
        dybuf[0:tm, :] = dy
        dybuf[tm:tm + HALO, :] = jnp.where(t < nt - 1, dyn, jnp.zeros_like(dyn))
        dybuf[tm + HALO:tm + HALO + 8, :] = jnp.zeros((8, CONV_C), F32)
        _shifted_copies(dybuf, dysh, tm + HALO)

        a, sg, hg = _glu(u_ref[...])
        _, _, hh = _glu(uh_ref[...])
        hbuf[0:HALO, :] = jnp.where(t > 0, hh, jnp.zeros_like(hh))
        hbuf[HALO:HALO + tm, :] = hg
        hbuf[HALO + tm:HALO + tm + 8, :] = jnp.zeros((8, CONV_C), F32)
        _shifted_copies(hbuf, hsh, HALO + tm)

        off = HALO - (CONV_K - 1)
        for c0 in range(0, tm, CONV_CHUNK):
            acc = jnp.zeros((CONV_CHUNK, CONV_C), F32)
            dyc = dybuf[c0:c0 + CONV_CHUNK, :]
            for j in range(CONV_K):
                acc = acc + cw_ref[j:j + 1, :] * _shifted_rows(dybuf, dysh, c0 + (CONV_K - 1) - j, CONV_CHUNK)
                prod = dyc * _shifted_rows(hbuf, hsh, c0 + off + j, CONV_CHUNK)
                dw_sc[j] += jnp.sum(prod.reshape(CONV_CHUNK // 8, 8, CONV_C), axis=0)
            dhg_sc[c0:c0 + CONV_CHUNK, :] = acc

        dhg = dhg_sc[...]
        du_ref[:, 0:CONV_C] = dhg * sg
        du_ref[:, CONV_C:2 * CONV_C] = dhg * a * sg * (1.0 - sg)

        @pl.when(t == nt - 1)
        def _():
            dcw_ref[...] = jnp.sum(dw_sc[...], axis=1)

    vec = pl.BlockSpec((1, CONV_C), lambda t: (0, 0))
    tile = pl.BlockSpec((tm, CONV_C), lambda t: (t, 0))
    nxt = pl.BlockSpec((HALO, CONV_C), lambda t: (jnp.minimum((t + 1) * nh, S // HALO - 1), 0))
    return _hosted(comm)(
        body, name="conv_bwd",
        grid=(nt,),
        in_specs=[tile, nxt,
                  pl.BlockSpec((tm, 2 * CONV_C), lambda t: (t, 0)),
                  pl.BlockSpec((HALO, 2 * CONV_C), lambda t: (jnp.maximum(t * nh - 1, 0), 0)),
                  tile, nxt,
                  pl.BlockSpec((CONV_K, CONV_C), lambda t: (0, 0)), vec, vec],
        out_specs=[pl.BlockSpec((tm, 2 * CONV_C), lambda t: (t, 0)),
                   pl.BlockSpec((CONV_K, CONV_C), lambda t: (0, 0)), vec, vec, vec],
        out_shape=[jax.ShapeDtypeStruct((S, 2 * CONV_C), F32), jax.ShapeDtypeStruct((CONV_K, CONV_C), F32),
                   jax.ShapeDtypeStruct((1, CONV_C), F32), jax.ShapeDtypeStruct((1, CONV_C), F32),
                   jax.ShapeDtypeStruct((1, CONV_C), F32)],
        scratch_shapes=[pltpu.VMEM((HALO + tm + 8, CONV_C), F32), pltpu.VMEM((tm + HALO + 8, CONV_C), F32),
                        pltpu.VMEM((tm, CONV_C), F32), pltpu.VMEM((CONV_K, 8, CONV_C), F32),
                        pltpu.VMEM((7, HALO + tm, CONV_C), F32), pltpu.VMEM((7, tm + HALO, CONV_C), F32)],
        compiler_params=_params(1),
    )(dc, dc, u, u, y, y, cw, lg, lb)


def outproj_bwd(dx, ao, co, wout, *, tm=512, comm=None):
    S, D = dx.shape

    def body(dx_ref, a_ref, c_ref, w_ref, da_ref, dc_ref, dw_ref):
        @pl.when(pl.program_id(0) == 0)
        def _():
            dw_ref[...] = jnp.zeros_like(dw_ref)

        dxb = dx_ref[...].astype(CDT)
        da_ref[...] = lax.dot_general(dxb, w_ref[0:ATT_W, :], NT, preferred_element_type=F32).astype(CDT)
        dc_ref[...] = lax.dot_general(dxb, w_ref[ATT_W:ATT_W + CONV_C, :], NT, preferred_element_type=F32)
        dw_ref[0:ATT_W, :] += lax.dot_general(a_ref[...], dxb, TN, preferred_element_type=F32)
        dw_ref[ATT_W:ATT_W + CONV_C, :] += lax.dot_general(c_ref[...], dxb, TN, preferred_element_type=F32)

    return _hosted(comm)(
        body, name="outproj_bwd",
        grid=(S // tm,),
        in_specs=[pl.BlockSpec((tm, D), lambda i: (i, 0)), pl.BlockSpec((tm, ATT_W), lambda i: (i, 0)),
                  pl.BlockSpec((tm, CONV_C), lambda i: (i, 0)), pl.BlockSpec((D, D), lambda i: (0, 0))],
        out_specs=[pl.BlockSpec((tm, ATT_W), lambda i: (i, 0)), pl.BlockSpec((tm, CONV_C), lambda i: (i, 0)),
                   pl.BlockSpec((D, D), lambda i: (0, 0))],
        out_shape=[jax.ShapeDtypeStruct((S, ATT_W), CDT), jax.ShapeDtypeStruct((S, CONV_C), F32),
                   jax.ShapeDtypeStruct((D, D), F32)],
        compiler_params=_params(1),
    )(dx, ao, co, wout)


def inproj_bwd(x, g, dres, dq, dk, dv, du, wextT, cs, sn, *, tm=512):
    S, D = x.shape

    def body(x_ref, g_ref, dres_ref, dq_ref, dk_ref, dv_ref, du_ref, w_ref, cs_ref, sn_ref,
             dx_ref, dw_ref, dg_ref, dp_sc):
        @pl.when(pl.program_id(0) == 0)
        def _():
            dw_ref[...] = jnp.zeros_like(dw_ref)
            dg_ref[...] = jnp.zeros_like(dg_ref)

        c, s = cs_ref[...], sn_ref[...]
        for b in range(4):
            dp_sc[:, 128 * b:128 * (b + 1)] = _rope_transpose(dq_ref[:, 128 * b:128 * (b + 1)], c, s).astype(CDT)
        for b in range(2):
            dp_sc[:, 512 + 128 * b:512 + 128 * (b + 1)] = _rope_transpose(
                dk_ref[:, 128 * b:128 * (b + 1)], c, s).astype(CDT)
        dp_sc[:, 768:1024] = dv_ref[...].astype(CDT)
        dp_sc[:, 1024:2048] = du_ref[...].astype(CDT)
        dp = dp_sc[...]
        xhat, r = _rms_parts(x_ref[...])
        gg = g_ref[...]
        h = (xhat * gg).astype(CDT)
        dh = jnp.dot(dp, w_ref[...], preferred_element_type=F32)
        dw_ref[...] += lax.dot_general(dp, h, TN, preferred_element_type=F32)
        dx, dg = _rms_bwd(dh, xhat, r, gg)
        dx_ref[...] = dres_ref[...] + dx
        dg_ref[...] += dg

    def row(w):
        return pl.BlockSpec((tm, w), lambda i: (i, 0))

    return pl.pallas_call(
        body, name="inproj_bwd",
        grid=(S // tm,),
        in_specs=[row(D), pl.BlockSpec((1, D), lambda i: (0, 0)), row(D), row(512), row(256), row(256),
                  row(1024), pl.BlockSpec((D_EXT, D), lambda i: (0, 0), pipeline_mode=pl.Buffered(1)),
                  row(128), row(128)],
        out_specs=[row(D), pl.BlockSpec((D_EXT, D), lambda i: (0, 0), pipeline_mode=pl.Buffered(1)),
                   pl.BlockSpec((1, D), lambda i: (0, 0))],
        out_shape=[jax.ShapeDtypeStruct((S, D), F32), jax.ShapeDtypeStruct((D_EXT, D), F32),
                   jax.ShapeDtypeStruct((1, D), F32)],
        scratch_shapes=[pltpu.VMEM((tm, D_EXT), CDT)],
        compiler_params=_params(1),
    )(x, g, dres, dq, dk, dv, du, wextT, cs, sn)


def _rope_tables(positions):
    inv_freq = 1.0 / (10000.0 ** (jnp.arange(0, HEAD_DIM, 2, dtype=F32) / HEAD_DIM))
    ang = positions.astype(F32).reshape(-1, 1) * inv_freq
    cos, sin = jnp.cos(ang), jnp.sin(ang)
    cs = jnp.tile(jnp.concatenate([cos, cos], axis=-1), (1, 2))
    sn = jnp.tile(jnp.concatenate([-sin, sin], axis=-1), (1, 2))
    return cs, sn


def _widen_w_in(w):
    q, u = w[0:512], w[768:1792]
    parts = [q]
    for base in (512, 576, 640, 704):
        parts += [w[base:base + 64], w[base:base + 64]]
    return jnp.concatenate(parts + [u], axis=0)


def _fold_w_in(d):
    parts = [d[0:512]]
    for base in (512, 640, 768, 896):
        parts.append(d[base:base + 64] + d[base + 64:base + 128])
    return jnp.concatenate(parts + [d[1024:2048]], axis=0)


def add_halves(g5s, r1s, c_idx):
    n = len(g5s)

    def body(c_ref, *refs):
        for a in range(n):
            refs[2 * n + a][...] = (refs[a][...] + refs[n + a][...]).astype(CDT)

    def g_spec(g):
        return pl.BlockSpec((2, None) + g.shape[2:], lambda s, cr: (s, cr[0], 0, 0))

    def r_spec(g):
        return pl.BlockSpec((2,) + g.shape[2:], lambda s, cr: (s, 0, 0))

    return list(pl.pallas_call(
        body, name="add_halves",
        grid_spec=pltpu.PrefetchScalarGridSpec(
            num_scalar_prefetch=1, grid=(N_CHIPS // 2,),
            in_specs=[g_spec(g) for g in g5s] + [r_spec(g) for g in g5s],
            out_specs=[r_spec(g) for g in g5s]),
        out_shape=[jax.ShapeDtypeStruct((N_CHIPS,) + g.shape[2:], CDT) for g in g5s],
        compiler_params=_params(1),
    )(c_idx, *g5s, *r1s))


def sum_partials(parts, recv3s, j_idx):
    n = len(parts)

    def body(j_ref, *refs):
        for a in range(n):
            p_ref, r_ref = refs[a], refs[n + a]
            refs[2 * n + a][...] = ((p_ref[...].astype(F32) + r_ref[0].astype(F32))
                                    + r_ref[1].astype(F32)) + r_ref[2].astype(F32)

    def half(p):
        return p.shape[1] // 2

    return list(pl.pallas_call(
        body, name="sum_partials",
        grid_spec=pltpu.PrefetchScalarGridSpec(
            num_scalar_prefetch=1, grid=(2,),
            in_specs=[pl.BlockSpec((None, half(p), p.shape[2]), lambda i, jr: (jr[0], i, 0)) for p in parts]
            + [pl.BlockSpec((3, half(p), p.shape[2]), lambda i, jr: (0, i, 0)) for p in parts],
            out_specs=[pl.BlockSpec((half(p), p.shape[2]), lambda i, jr: (i, 0)) for p in parts]),
        out_shape=[jax.ShapeDtypeStruct(p.shape[1:], F32) for p in parts],
        compiler_params=_params(1),
    )(j_idx, *parts, *recv3s))


class _Chain:
    STAGES = ("swap", "xchg", "share")

    def __init__(self, grads, c_idx, j_idx):
        self.c_idx, self.j_idx = c_idx, j_idx
        self.g5 = [g.reshape(N_CHIPS, 2, g.shape[0] // (2 * N_CHIPS), g.shape[1]) for g in grads]
        self.stage_no = 0

    @property
    def done(self):
        return self.stage_no == len(self.STAGES)

    def next_stage(self):
        name = self.STAGES[self.stage_no]

        def callback(res):
            getattr(self, "after_" + name)(res)
            self.stage_no += 1

        return getattr(self, name)(), callback

    def swap(self):
        return swap_op(self.g5)

    def after_swap(self, recv):
        self.parts = add_halves(self.g5, recv, self.c_idx)

    def xchg(self):
        return exchange_op(self.parts)

    def after_xchg(self, recv):
        self.totals = sum_partials(self.parts, recv, self.j_idx)

    def share(self):
        return share_op(self.totals)

    def after_share(self, recv):
        both = [_own_slab(h, t, self.c_idx[0]) for h, t in zip(recv, self.totals)]
        self.final = [h.reshape(2 * h.shape[1], h.shape[2]) for h in both]


def all_reduce_small(vec):
    R = vec.shape[0]

    def body(v_ref, o_ref, buf, send, recv):
        x, y, c = _coords()
        me = 4 * x + 2 * y + c
        buf[me] = v_ref[...]
        cps = []
        for m in range(1, N_DEV):
            dx, dy, dc = (m >> 2) & 1, (m >> 1) & 1, m & 1
            cp = pltpu.make_async_remote_copy(v_ref, buf.at[me], send.at[m - 1], recv.at[m - 1],
                                              device_id=((x + dx) % 2, (y + dy) % 2, (c + dc) % 2),
                                              device_id_type=MESH)
            cp.start()
            cps.append(cp)
        for cp in cps:
            cp.wait()
        acc = buf[0]
        for d in range(1, N_DEV):
            acc = acc + buf[d]
        o_ref[...] = acc

    return pl.pallas_call(
        body, name="all_reduce_small",
        in_specs=[pl.BlockSpec(memory_space=pltpu.VMEM)], out_specs=pl.BlockSpec(memory_space=pltpu.VMEM),
        out_shape=jax.ShapeDtypeStruct(vec.shape, F32),
        scratch_shapes=[pltpu.VMEM((N_DEV, R, 128), F32), pltpu.SemaphoreType.DMA((N_DEV - 1,)),
                        pltpu.SemaphoreType.DMA((N_DEV - 1,))],
    )(vec)


def adamw(w, g, m, v, *, tm=512):
    R, C = w.shape
    tm = max(t for t in range(8, min(tm, R) + 1, 8) if R % t == 0)
    c1 =1.0 - ADAM_B1 ** ADAM_STEP
    c2 = 1.0 - ADAM_B2 ** ADAM_STEP

    def body(w_ref, g_ref, m_ref, v_ref, d_ref, nm_ref, nv_ref):
        gg = g_ref[...]
        nm = ADAM_B1 * m_ref[...] + (1.0 - ADAM_B1) * gg
        nv = ADAM_B2 * v_ref[...] + (1.0 - ADAM_B2) * (gg * gg)
        nm_ref[...] = nm
        nv_ref[...] = nv
        d_ref[...] = -ADAM_LR * ((nm / c1) / (jnp.sqrt(nv / c2) + ADAM_EPS) + ADAM_WD * w_ref[...])

    blk = pl.BlockSpec((tm, C), lambda i: (i, 0))
    return pl.pallas_call(
        body, name="adamw",
        grid=(pl.cdiv(R, tm),),
        in_specs=[blk] * 4, out_specs=[blk] * 3,
        out_shape=[jax.ShapeDtypeStruct((R, C), F32)] * 3,
        compiler_params=_params(1),
    )(w, g, m, v)


def adamw_layers(w, m, v, g_layers, *, tm=352):
    L, R, C = w.shape
    tm = max(t for t in range(8, min(tm, R) + 1, 8) if R % t == 0)
    c1 = 1.0 - ADAM_B1 ** ADAM_STEP
    c2 = 1.0 - ADAM_B2 ** ADAM_STEP

    def body(w_ref, m_ref, v_ref, *rest):
        g_refs, (go_ref, d_ref, nm_ref, nv_ref) = rest[:L], rest[L:]
        layer = pl.program_id(0)
        gg = g_refs[0][...]
        for l in range(1, L):
            gg = jnp.where(layer == l, g_refs[l][...], gg)
        nm = ADAM_B1 * m_ref[...] + (1.0 - ADAM_B1) * gg
        nv = ADAM_B2 * v_ref[...] + (1.0 - ADAM_B2) * (gg * gg)
        go_ref[...] = gg
        nm_ref[...] = nm
        nv_ref[...] = nv
        d_ref[...] = -ADAM_LR * ((nm / c1) / (jnp.sqrt(nv / c2) + ADAM_EPS) + ADAM_WD * w_ref[...])

    blk = pl.BlockSpec((None, tm, C), lambda l, i: (l, i, 0))
    gblk = pl.BlockSpec((tm, C), lambda l, i: (i, 0))
    return pl.pallas_call(
        body, name="adamw_layers",
        grid=(L, R // tm),
        in_specs=[blk] * 3 + [gblk] * L, out_specs=[blk] * 4,
        out_shape=[jax.ShapeDtypeStruct((L, R, C), F32)] * 4,
        compiler_params=_params(2),
    )(w, m, v, *g_layers)


_SMALL = (("n1", (2, D_MODEL)), ("nm", (2, D_MODEL)), ("n2", (2, D_MODEL)), ("nf", (D_MODEL,)),
          ("cb", (2, CONV_C)), ("lg", (2, CONV_C)), ("lb", (2, CONV_C)), ("sinks", (2, N_HEADS)),
          ("cw", (2, CONV_K, CONV_C)))


def _pack(parts, rows):
    flat = jnp.concatenate([p.reshape(-1).astype(F32) for p in parts])
    return jnp.pad(flat, (0, rows * 128 - flat.shape[0])).reshape(rows, 128)


def _unpack(block, shapes):
    flat = block.reshape(-1)
    out, o = [], 0
    for shp in shapes:
        n = 1
        for s in shp:
            n *= s
        out.append(flat[o:o + n].reshape(shp))
        o += n
    return out


def kernel(x, positions, ffn1_norm, ffn1_w_gate, ffn1_w_up, ffn1_w_down, mix_norm, w_in, conv_w, conv_b, conv_ln_g, conv_ln_b, attn_sinks, w_out, ffn2_norm, ffn2_w_gate, ffn2_w_up, ffn2_w_down, final_norm, loss_target, m_ffn1_norm, m_ffn1_w_gate, m_ffn1_w_up, m_ffn1_w_down, m_mix_norm, m_w_in, m_conv_w, m_conv_b, m_conv_ln_g, m_conv_ln_b, m_attn_sinks, m_w_out, m_ffn2_norm, m_ffn2_w_gate, m_ffn2_w_up, m_ffn2_w_down, m_final_norm, v_ffn1_norm, v_ffn1_w_gate, v_ffn1_w_up, v_ffn1_w_down, v_mix_norm, v_w_in, v_conv_w, v_conv_b, v_conv_ln_g, v_conv_ln_b, v_attn_sinks, v_w_out, v_ffn2_norm, v_ffn2_w_gate, v_ffn2_w_up, v_ffn2_w_down, v_final_norm):
    cx, cy, cc = _coords()
    chip = 2 * cx + cy
    c_idx = jnp.reshape(cc, (1,)).astype(jnp.int32)
    j_idx = jnp.reshape(chip, (1,)).astype(jnp.int32)
    L = ffn1_norm.shape[0]
    tr = lambda a: jnp.swapaxes(a, 1, 2)

    sh = dict(f1g=tr(ffn1_w_gate), f1u=tr(ffn1_w_up), f1d=ffn1_w_down, f2g=tr(ffn2_w_gate),
              f2u=tr(ffn2_w_up), f2d=ffn2_w_down, win=tr(w_in), wout=w_out)
    sh = {k: [v[l].astype(CDT) for l in range(L)] for k, v in sh.items()}
    W = {}

    def gather_op(keys):
        if keys == ["cw"]:
            return ag_op([conv_w])
        return ag2_op([sh[k[0]][k[1]] for k in keys])

    def take(keys, res):
        for k, a in zip(keys, res):
            if k == "cw":
                W[k] = _own_slab(a, conv_w, chip)
            else:
                mine = sh[k[0]][k[1]]
                W[k] = _own_slab(a, mine.reshape(a.shape[1:]), chip).reshape(N_CHIPS * mine.shape[0], mine.shape[1])

    def with_ag(fn, keys, *args):
        if not keys:
            return fn(*args)
        main, res = fn(*args, comm=gather_op(keys))
        take(keys, res)
        return main

    ag_hosts = {("ffn1", 0): [("win", 0), ("wout", 0), ("f2g", 0), ("f2u", 0)],
                ("inproj", 0): ["cw"], ("conv", 0): [("f2d", 0)],
                ("ffn2", 0): [("f1g", 1), ("f1u", 1), ("f1d", 1)],
                ("ffn1", 1): [("win", 1), ("wout", 1), ("f2g", 1), ("f2u", 1)],
                ("conv", 1): [("f2d", 1)]}
    first = [("f1g", 0), ("f1u", 0), ("f1d", 0)]
    take(first, first_gather([sh[k_][l] for k_, l in first]))

    cs, sn = _rope_tables(positions)
    saved = []
    h = x[0]
    for l in range(L):
        sink = attn_sinks[l].reshape(2, 4)
        sink_row = jnp.repeat(sink, BLOCK, axis=1).reshape(2, 1, 4 * BLOCK)
        x0 = h
        x1, g1, u1 = with_ag(ffn_fwd, ag_hosts.get(("ffn1", l)), x0, ffn1_norm[l][None],
                             W[("f1g", l)], W[("f1u", l)], W[("f1d", l)])
        wext = _widen_w_in(W[("win", l)])
        q, k, v, u = with_ag(inproj_fwd, ag_hosts.get(("inproj", l)), x1, mix_norm[l][None], wext, cs, sn)
        ao = with_ag(attn_fwd, ag_hosts.get(("attn", l)), q, k, v, sink_row)
        cwl = jnp.transpose(W["cw"][:, l], (1, 0, 2)).reshape(CONV_K, CONV_C)
        co, yc = with_ag(conv_fwd, ag_hosts.get(("conv", l)), u, cwl, conv_b[l][None], conv_ln_g[l][None],
                         conv_ln_b[l][None])
        head = (final_norm[None], loss_target[0]) if l == L - 1 else None
        h, g2, u2, x2, *tail = with_ag(functools.partial(ffn_fwd, mix=(ao, co, W[("wout", l)]), head=head),
                                       ag_hosts.get(("ffn2", l)), x1, ffn2_norm[l][None],
                                       W[("f2g", l)], W[("f2u", l)], W[("f2d", l)])
        saved.append((x0, x1, x2, g1, u1, g2, u2, q, k, v, u, ao, co, yc, sink, wext, cwl))

    dx, (loss, dnf) = h, tail

    active = []

    def advance(run):
        stages = [ch.next_stage() for ch in active]
        ops = [op for op, _ in stages]
        main, res = run(_merge(*ops) if ops else None)
        for (_, cb), r in zip(stages, _split(res, *ops)):
            cb(r)
        active[:] = [ch for ch in active if not ch.done]
        return main

    def hosted(fn, *args):
        def run(comm):
            if comm is None:
                return fn(*args), []
            return fn(*args, comm=comm)
        return advance(run)

    def chain(key, names_, grads):
        chains[key] = _Chain(grads, c_idx, j_idx)
        active.append(chains[key])
        for i_, nme_ in enumerate(names_):
            where[(nme_, key[1])] = (key, i_)

    small = {k_: [None] * L for k_ in ("n1", "nm", "n2", "cw", "cb", "lg", "lb", "sinks")}
    chains, where = {}, {}
    for l in reversed(range(L)):
        x0, x1, x2, g1, u1, g2, u2, q, k, v, u, ao, co, yc, sink, wext, cwl = saved[l]
        sink_row = jnp.repeat(sink, BLOCK, axis=1).reshape(2, 1, 4 * BLOCK)
        dx2, dgt, dup, hh, small["n2"][l], gwd = hosted(
            ffn_bwd_dgrad, x2, ffn2_norm[l][None], dx, g2, u2, W[("f2g", l)], W[("f2u", l)], W[("f2d", l)])
        chain(("f2d", l), ["ffn2_w_down"], [gwd])
        chain(("f2gu", l), ["ffn2_w_gate", "ffn2_w_up"], hosted(ffn_wgrad_rows, hh, [dgt, dup]))
        da, dc, gwout = hosted(outproj_bwd, dx2, ao, co, W[("wout", l)])
        du, small["cw"][l], small["cb"][l], small["lg"][l], small["lb"][l] = hosted(
            conv_bwd, dc, u, yc, cwl, conv_ln_g[l][None], conv_ln_b[l][None])
        dq, dk, dv, dsink = hosted(attn_bwd, q, k, v, da, sink_row)
        small["sinks"][l] = jnp.sum(dsink.reshape(2, 4, BLOCK), axis=-1).reshape(N_HEADS)
        dx1, gwext, small["nm"][l] = inproj_bwd(x1, mix_norm[l][None], dx2, dq, dk, dv, du, wext, cs, sn)
        chain(("mx", l), ["w_out", "w_in"], [gwout, _fold_w_in(gwext)])
        dx, dgt, dup, hh, small["n1"][l], gwd = (hosted if l > 0 else lambda fn, *a: fn(*a))(
            ffn_bwd_dgrad, x0, ffn1_norm[l][None], dx1, g1, u1, W[("f1g", l)], W[("f1u", l)], W[("f1d", l)])
        chain(("f1d", l), ["ffn1_w_down"], [gwd])
        if l > 0:
            chain(("f1gu", l), ["ffn1_w_gate", "ffn1_w_up"], hosted(ffn_wgrad_rows, hh, [dgt, dup]))
        else:
            chain(("f1u", l), ["ffn1_w_up"], hosted(ffn_wgrad_rows, hh, [dup]))
            chain(("f1g", l), ["ffn1_w_gate"], hosted(ffn_wgrad_rows, hh, [dgt]))

    weights = dict(ffn1_norm=ffn1_norm, ffn1_w_gate=ffn1_w_gate, ffn1_w_up=ffn1_w_up, ffn1_w_down=ffn1_w_down,
                   mix_norm=mix_norm, w_in=w_in, conv_w=conv_w, conv_b=conv_b, conv_ln_g=conv_ln_g,
                   conv_ln_b=conv_ln_b, attn_sinks=attn_sinks, w_out=w_out, ffn2_norm=ffn2_norm,
                   ffn2_w_gate=ffn2_w_gate, ffn2_w_up=ffn2_w_up, ffn2_w_down=ffn2_w_down, final_norm=final_norm)
    moms = dict(ffn1_norm=(m_ffn1_norm, v_ffn1_norm), ffn1_w_gate=(m_ffn1_w_gate, v_ffn1_w_gate),
                ffn1_w_up=(m_ffn1_w_up, v_ffn1_w_up), ffn1_w_down=(m_ffn1_w_down, v_ffn1_w_down),
                mix_norm=(m_mix_norm, v_mix_norm), w_in=(m_w_in, v_w_in), conv_w=(m_conv_w, v_conv_w),
                conv_b=(m_conv_b, v_conv_b), conv_ln_g=(m_conv_ln_g, v_conv_ln_g),
                conv_ln_b=(m_conv_ln_b, v_conv_ln_b), attn_sinks=(m_attn_sinks, v_attn_sinks),
                w_out=(m_w_out, v_w_out), ffn2_norm=(m_ffn2_norm, v_ffn2_norm),
                ffn2_w_gate=(m_ffn2_w_gate, v_ffn2_w_gate), ffn2_w_up=(m_ffn2_w_up, v_ffn2_w_up),
                ffn2_w_down=(m_ffn2_w_down, v_ffn2_w_down), final_norm=(m_final_norm, v_final_norm))
    names = list(weights)
    big_names = dict(ffn2_w_gate=True, ffn2_w_up=True, ffn2_w_down=False, w_out=False, w_in=True,
                     ffn1_w_down=False, ffn1_w_up=True, ffn1_w_gate=True)
    grads, delta, new_m, new_v = {}, {}, {}, {}

    def big_adamw(nme):
        view = tr if big_names[nme] else (lambda a: a)
        res = adamw_layers(view(weights[nme]), view(moms[nme][0]), view(moms[nme][1]),
                           [chains[where[(nme, l)][0]].final[where[(nme, l)][1]] for l in range(L)])
        grads[nme], delta[nme], new_m[nme], new_v[nme] = [view(a) for a in res]

    while active:
        advance(lambda comm: (None, _run_comm(comm)))
    for nme in big_names:
        big_adamw(nme)

    G = {k_: jnp.stack(v_) for k_, v_ in small.items()}
    G["nf"] = dnf
    small_shapes = [shp for _, shp in _SMALL]
    n_small = 1 + sum(math.prod(s) for s in small_shapes)
    rows = -(-n_small // 1024) * 8
    packed = _pack([loss] + [G[k_] for k_, _ in _SMALL], rows)
    summed = all_reduce_small(packed)
    loss_out, *small_sum = _unpack(summed, [()] + small_shapes)
    gs = dict(zip([k_ for k_, _ in _SMALL], small_sum))
    gs["cw"] = lax.dynamic_slice_in_dim(gs["cw"], chip * (CONV_C // N_CHIPS), CONV_C // N_CHIPS, axis=2)
    grads.update(ffn1_norm=gs["n1"], mix_norm=gs["nm"], conv_w=gs["cw"], conv_b=gs["cb"], conv_ln_g=gs["lg"],
                 conv_ln_b=gs["lb"], attn_sinks=gs["sinks"], ffn2_norm=gs["n2"], final_norm=gs["nf"])

    small_names = [nme for nme in names if nme not in big_names]
    s_shapes = [weights[nme].shape for nme in small_names]
    n_tot = sum(math.prod(s) for s in s_shapes)
    srows = -(-n_tot // 1024) * 8
    d, nm_, nv_ = adamw(_pack([weights[nme] for nme in small_names], srows),
                        _pack([grads[nme] for nme in small_names], srows),
                        _pack([moms[nme][0] for nme in small_names], srows),
                        _pack([moms[nme][1] for nme in small_names], srows))
    for nme, dd, mm, vv in zip(small_names, _unpack(d, s_shapes), _unpack(nm_, s_shapes), _unpack(nv_, s_shapes)):
        delta[nme], new_m[nme], new_v[nme] = dd, mm, vv

    return (loss_out, dx[None], *[grads[nme] for nme in names], *[delta[nme] for nme in names],
            *[new_m[nme] for nme in names], *[new_v[nme] for nme in names])
```
